```python
import jax, jax.numpy as jnp
from jax import lax
import numpy as np

D_MODEL = 1024
BATCH = 8
SEQ = 2048
DEPTH = 1
DEC_BATCH = 32
DEC_SEQ = 4
PAST_LEN = 8192
PAGE_SIZE = 128

NSA_HEADS = 8
NSA_KV_HEADS = 2
NSA_GROUP = NSA_HEADS // NSA_KV_HEADS
NSA_HD = 64
NSA_QW = NSA_HEADS * NSA_HD
NSA_KVW = NSA_KV_HEADS * NSA_HD
CMP_BLOCK = 32
CMP_STRIDE = 16
SEL_BLOCK = 64
SEL_TOP = 16
WINDOW = 512
PHI_HIDDEN = 256
Q_BLOCK = 64
ATTN_SCALE = NSA_HD ** -0.5
ROPE_THETA = 10000.0
M_HEADS = 4
M_HD = 128
M_W = M_HEADS * M_HD
M_CHUNK = 64
CONV_W = 4
PEER_HEADS = 8
N_KEYS = 128
N_EXPERTS = N_KEYS * N_KEYS
PEER_TOPK = 16
PEER_QDIM = 256
PEER_BLOCK = 128
MIX_W = NSA_QW + M_W
IN_SPLITS = (NSA_QW, 6 * NSA_KVW, 3 * NSA_HEADS, 2 * M_W, M_W, M_W, 2 * M_HEADS)
IN_W = sum(IN_SPLITS)
LN_EPS = 1e-5
ALPHA = (2 * DEPTH) ** 0.25
BETA = (8 * DEPTH) ** -0.25

kernel_name = 'nsa_mlstm_peer_hybrid_step'


def layer_norm(x, g, b):
    xf = x.astype(jnp.float32)
    mu = xf.mean(-1, keepdims=True)
    var = jnp.square(xf - mu).mean(-1, keepdims=True)
    return ((xf - mu) * lax.rsqrt(var + LN_EPS) * g + b).astype(x.dtype)


def rope(x, pos):
    half = x.shape[-1] // 2
    inv = ROPE_THETA ** (-jnp.arange(half, dtype=jnp.float32) / half)
    ang = pos.astype(jnp.float32)[:, None] * inv[None, :]
    cos = jnp.cos(ang)[:, None, :]
    sin = jnp.sin(ang)[:, None, :]
    xf = x.astype(jnp.float32)
    x1, x2 = xf[..., :half], xf[..., half:]
    return jnp.concatenate([x1 * cos - x2 * sin, x2 * cos + x1 * sin], axis=-1).astype(x.dtype)


def split_in_proj(x, w_in):
    z = x @ w_in
    cuts = [int(c) for c in np.cumsum(IN_SPLITS)[:-1]]
    return jnp.split(z, cuts, axis=-1)


def nsa_project(zq, zkv, zg, pos):
    B, T, _ = zq.shape
    q = rope(zq.reshape(B, T, NSA_HEADS, NSA_HD), pos)
    kv = zkv.reshape(B, T, 6, NSA_KV_HEADS, NSA_HD)
    rows = (rope(kv[:, :, 0], pos), kv[:, :, 1], rope(kv[:, :, 2], pos), kv[:, :, 3],
            rope(kv[:, :, 4], pos), kv[:, :, 5])
    gates = jax.nn.sigmoid(zg.astype(jnp.float32)).reshape(B, T, NSA_HEADS, 3)
    return q, rows, gates


def compress(rows, pe, w1, b1, w2):
    B, L, KV, hd = rows.shape
    r = CMP_BLOCK // CMP_STRIDE
    nch = L // CMP_STRIDE
    ch = rows[:, :nch * CMP_STRIDE].reshape(B, nch, CMP_STRIDE, KV, hd)
    nblk = nch - r + 1
    blocks = jnp.concatenate([ch[:, j:j + nblk] for j in range(r)], axis=2)
    blocks = blocks + pe[None, None, :, None, :]
    flat = blocks.transpose(0, 1, 3, 2, 4).reshape(B, nblk, KV, CMP_BLOCK * hd)
    hid = jax.nn.gelu(flat @ w1 + b1, approximate=False)
    return hid @ w2


def cmp_attend(q, qpos, kc, vc):
    B, T = q.shape[:2]
    qg = q.reshape(B, T, NSA_KV_HEADS, NSA_GROUP, NSA_HD)
    s = jnp.einsum('btngd,bcnd->btngc', qg, kc).astype(jnp.float32) * ATTN_SCALE
    nblk = kc.shape[1]
    blk_end = jnp.arange(nblk) * CMP_STRIDE + CMP_BLOCK - 1
    valid = (blk_end[None, :] <= qpos[:, None])[None, :, None, None, :]
    p = jax.nn.softmax(jnp.where(valid, s, -1e30), axis=-1) * valid
    o = jnp.einsum('btngc,bcnd->btngd', p.astype(vc.dtype), vc)
    return o.reshape(B, T, NSA_HEADS, NSA_HD), p


def select_blocks(p, qpos, n_sel):
    imp = p.sum(axis=3)
    R = SEL_BLOCK // CMP_STRIDE
    r = CMP_BLOCK // CMP_STRIDE
    nb = imp.shape[-1]
    right = n_sel * R + R - 1 - nb
    padded = jnp.pad(imp, ((0, 0), (0, 0), (0, 0), (r - 1, right)))
    score = padded[..., 0:(n_sel - 1) * R + 1:R]
    for o in range(1, R + r - 1):
        score = score + padded[..., o:o + (n_sel - 1) * R + 1:R]
    j = jnp.arange(n_sel)[None, :]
    cur = (qpos // SEL_BLOCK)[:, None]
    valid = (j * SEL_BLOCK <= qpos[:, None])[None, :, None, :]
    forced = ((j == 0) | (j == cur) | (j == cur - 1))[None, :, None, :]
    score = jnp.where(forced, jnp.inf, jnp.where(valid, score, -jnp.inf))
    _, sel = lax.top_k(score, min(SEL_TOP, n_sel))
    return sel


def to_blocks(rows, n_sel):
    B, L, KV, hd = rows.shape
    rows = jnp.pad(rows, ((0, 0), (0, n_sel * SEL_BLOCK - L), (0, 0), (0, 0)))
    return rows.reshape(B, n_sel, SEL_BLOCK, KV, hd).transpose(0, 3, 1, 2, 4)


def take_rows(table, idx):
    return table[idx]


def sel_attend(q, qpos, sel, kb, vb):
    B, Tq = q.shape[:2]
    k = sel.shape[-1]
    sel_t = sel.transpose(0, 2, 1, 3)
    gather = jax.vmap(jax.vmap(take_rows))
    kg = gather(kb, sel_t).reshape(B, NSA_KV_HEADS, Tq, k * SEL_BLOCK, NSA_HD)
    vg = gather(vb, sel_t).reshape(B, NSA_KV_HEADS, Tq, k * SEL_BLOCK, NSA_HD)
    kpos = (sel_t[..., None] * SEL_BLOCK + jnp.arange(SEL_BLOCK)).reshape(B, NSA_KV_HEADS, Tq, k * SEL_BLOCK)
    qg = q.reshape(B, Tq, NSA_KV_HEADS, NSA_GROUP, NSA_HD).transpose(0, 2, 1, 3, 4)
    s = jnp.einsum('bntgd,bntsd->bntgs', qg, kg).astype(jnp.float32) * ATTN_SCALE
    mask = kpos[:, :, :, None, :] <= qpos[None, None, :, None, None]
    pr = jax.nn.softmax(jnp.where(mask, s, -jnp.inf), axis=-1)
    o = jnp.einsum('bntgs,bntsd->bntgd', pr.astype(vg.dtype), vg)
    return o.transpose(0, 2, 1, 3, 4).reshape(B, Tq, NSA_HEADS, NSA_HD)


def win_attend(q, qpos, k, v, kpos):
    B, Tq = q.shape[:2]
    qg = q.reshape(B, Tq, NSA_KV_HEADS, NSA_GROUP, NSA_HD)
    s = jnp.einsum('btngd,bsnd->btngs', qg, k).astype(jnp.float32) * ATTN_SCALE
    diff = qpos[:, None] - kpos[None, :]
    mask = ((diff >= 0) & (diff < WINDOW) & (kpos[None, :] >= 0))[None, :, None, None, :]
    pr = jax.nn.softmax(jnp.where(mask, s, -jnp.inf), axis=-1)
    o = jnp.einsum('btngs,bsnd->btngd', pr.astype(v.dtype), v)
    return o.reshape(B, Tq, NSA_HEADS, NSA_HD)


def nsa_global(q, qpos, k_cmp, v_cmp, k_slc, v_slc, pe, w1, b1, w2):
    kc = compress(k_cmp, pe[0], w1[0], b1[0], w2[0])
    vc = compress(v_cmp, pe[1], w1[1], b1[1], w2[1])
    o_cmp, p = cmp_attend(q, qpos, kc, vc)
    n_sel = -(-k_slc.shape[1] // SEL_BLOCK)
    sel = select_blocks(p, qpos, n_sel)
    return o_cmp, sel, to_blocks(k_slc, n_sel), to_blocks(v_slc, n_sel)


def nsa_combine(gates, o_cmp, o_sel, o_win):
    B, T = gates.shape[:2]
    o = gates[..., 0:1] * o_cmp + gates[..., 1:2] * o_sel + gates[..., 2:3] * o_win
    return o.reshape(B, T, NSA_QW).astype(o_cmp.dtype)


def mlstm_chunk(carry, inp):
    C, n, m = carry
    q, k, v, ig, lf = inp
    L = q.shape[2]
    b = jnp.cumsum(lf, axis=-1)
    causal = jnp.tril(jnp.ones((L, L), dtype=bool))
    dmat = jnp.where(causal, b[..., :, None] - b[..., None, :] + ig[..., None, :], -jnp.inf)
    inter = b + m[..., None]
    m_t = jnp.maximum(inter, dmat.max(axis=-1))
    w_intra = jnp.exp(dmat - m_t[..., None])
    w_inter = jnp.exp(inter - m_t)
    s = jnp.einsum('bhtd,bhsd->bhts', q, k) * w_intra
    num = jnp.einsum('bhts,bhsv->bhtv', s, v) + w_inter[..., None] * jnp.einsum('bhtd,bhdv->bhtv', q, C)
    den = s.sum(-1) + w_inter * jnp.einsum('bhtd,bhd->bht', q, n)
    h = num / jnp.maximum(jnp.abs(den), jnp.exp(-m_t))[..., None]
    m_new = m_t[..., -1]
    w_s = jnp.exp(b[..., -1:] - b + ig - m_new[..., None])
    w_p = jnp.exp(b[..., -1] + m - m_new)
    C_new = w_p[..., None, None] * C + jnp.einsum('bhs,bhsd,bhsv->bhdv', w_s, k, v)
    n_new = w_p[..., None] * n + jnp.einsum('bhs,bhsd->bhd', w_s, k)
    return (C_new, n_new, m_new), h


def mlstm_mix(zqk, zv, zo, zif, buf0, C0, n0, m0, conv_w, conv_b, b_if, chunk):
    B, T, _ = zqk.shape
    f32 = jnp.float32
    full = jnp.concatenate([buf0.astype(zqk.dtype), zqk], axis=1)
    conv = conv_b
    for j in range(CONV_W):
        conv = conv + full[:, j:j + T] * conv_w[j]
    qk = jax.nn.silu(conv.astype(f32))

    def heads(a):
        return a.reshape(B, T, M_HEADS, M_HD).transpose(0, 2, 1, 3)

    q = heads(qk[..., :M_W])
    k = heads(qk[..., M_W:]) * (M_HD ** -0.5)
    v = heads(zv.astype(f32))
    gif = zif.astype(f32) + b_if.astype(f32)
    ig = gif[..., :M_HEADS].transpose(0, 2, 1)
    lf = jax.nn.log_sigmoid(gif[..., M_HEADS:]).transpose(0, 2, 1)
    nc = T // chunk

    def to_chunks(a):
        return jnp.moveaxis(a.reshape(B, M_HEADS, nc, chunk, *a.shape[3:]), 2, 0)

    carry0 = (C0.astype(f32), n0.astype(f32), m0.astype(f32))
    (C, n, m), h = lax.scan(mlstm_chunk, carry0,
                            (to_chunks(q), to_chunks(k), to_chunks(v), to_chunks(ig), to_chunks(lf)))
    h = jnp.moveaxis(h, 0, 2).reshape(B, M_HEADS, T, M_HD).transpose(0, 2, 1, 3).reshape(B, T, M_W)
    out = (jax.nn.sigmoid(zo.astype(f32)) * h).astype(zqk.dtype)
    return out, (C.astype(C0.dtype), n.astype(n0.dtype), m.astype(m0.dtype), full[:, T:])


def peer_ffn(x, w_pq, sub_keys, u_tab, v_tab):
    B, T, D = x.shape
    nt = B * T
    nb = -(-nt // PEER_BLOCK)
    xf = jnp.pad(x.reshape(nt, D), ((0, nb * PEER_BLOCK - nt), (0, 0))).reshape(nb, PEER_BLOCK, D)

    def one(xb):
        q = (xb @ w_pq).reshape(PEER_BLOCK, PEER_HEADS, 2, PEER_QDIM // 2)
        s = jnp.einsum('phcd,hckd->phck', q, sub_keys).astype(jnp.float32)
        ts, ti = lax.top_k(s, PEER_TOPK)
        cand = ts[:, :, 0, :, None] + ts[:, :, 1, None, :]
        cidx = ti[:, :, 0, :, None] * N_KEYS + ti[:, :, 1, None, :]
        fs, fpos = lax.top_k(cand.reshape(PEER_BLOCK, PEER_HEADS, -1), PEER_TOPK)
        eidx = jnp.take_along_axis(cidx.reshape(PEER_BLOCK, PEER_HEADS, -1), fpos, axis=-1)
        g = jax.nn.softmax(fs, axis=-1)
        u = u_tab[eidx]
        v = v_tab[eidx]
        a = jax.nn.gelu(jnp.einsum('pd,phkd->phk', xb, u).astype(jnp.float32), approximate=False)
        return jnp.einsum('phk,phkd->pd', (g * a).astype(v.dtype), v)

    y = lax.map(one, xf)
    return y.reshape(nb * PEER_BLOCK, D)[:nt].reshape(B, T, D)


def block_tail(x, mix, w_out, ln_g, ln_b, w_pq, sub_keys, u_tab, v_tab):
    h = layer_norm(ALPHA * x + mix @ w_out, ln_g[0], ln_b[0])
    return layer_norm(ALPHA * h + peer_ffn(h, w_pq, sub_keys, u_tab, v_tab), ln_g[1], ln_b[1])


def prompt_mix(x, w_in, pe, w1, b1, w2, conv_w, conv_b, b_if):
    B, S, _ = x.shape
    pos = jnp.arange(S)
    zq, zkv, zg, zqk, zv, zo, zif = split_in_proj(x, w_in)
    q, (k_cmp, v_cmp, k_slc, v_slc, k_win, v_win), gates = nsa_project(zq, zkv, zg, pos)
    o_cmp, sel, kb, vb = nsa_global(q, pos, k_cmp, v_cmp, k_slc, v_slc, pe, w1, b1, w2)
    pad = ((0, 0), (WINDOW, 0), (0, 0), (0, 0))
    kwp = jnp.pad(k_win, pad)
    vwp = jnp.pad(v_win, pad)

    def local(i):
        s0 = i * Q_BLOCK
        qb = lax.dynamic_slice_in_dim(q, s0, Q_BLOCK, axis=1)
        pb = lax.dynamic_slice_in_dim(pos, s0, Q_BLOCK, axis=0)
        sb = lax.dynamic_slice_in_dim(sel, s0, Q_BLOCK, axis=1)
        kw = lax.dynamic_slice_in_dim(kwp, s0, WINDOW + Q_BLOCK, axis=1)
        vw = lax.dynamic_slice_in_dim(vwp, s0, WINDOW + Q_BLOCK, axis=1)
        kpos = s0 - WINDOW + jnp.arange(WINDOW + Q_BLOCK)
        return sel_attend(qb, pb, sb, kb, vb), win_attend(qb, pb, kw, vw, kpos)

    o_sel, o_win = lax.map(local, jnp.arange(S // Q_BLOCK))
    o_sel = jnp.moveaxis(o_sel, 0, 1).reshape(B, S, NSA_HEADS, NSA_HD)
    o_win = jnp.moveaxis(o_win, 0, 1).reshape(B, S, NSA_HEADS, NSA_HD)
    o_nsa = nsa_combine(gates, o_cmp, o_sel, o_win)
    C0 = jnp.zeros((B, M_HEADS, M_HD, M_HD), jnp.float32)
    n0 = jnp.zeros((B, M_HEADS, M_HD), jnp.float32)
    m0 = jnp.zeros((B, M_HEADS), jnp.float32)
    buf0 = jnp.zeros((B, CONV_W - 1, 2 * M_W), x.dtype)
    o_m, (C, n, m, buf) = mlstm_mix(zqk, zv, zo, zif, buf0, C0, n0, m0, conv_w, conv_b, b_if, M_CHUNK)
    wl = min(WINDOW, S)
    mix = jnp.concatenate([o_nsa, o_m], axis=-1)
    return mix, (k_cmp, v_cmp, k_slc, v_slc, k_win[:, S - wl:], v_win[:, S - wl:], C, n, m, buf)


def sample_mix(x, kc_pool, vc_pool, ks_pool, vs_pool, kw_buf, vw_buf, C0, n0, m0, buf0, page_table,
               w_in, pe, w1, b1, w2, conv_w, conv_b, b_if):
    B, T, _ = x.shape
    past = page_table.shape[1] * PAGE_SIZE
    pos = past + jnp.arange(T)
    zq, zkv, zg, zqk, zv, zo, zif = split_in_proj(x, w_in)
    q, (k_cmp, v_cmp, k_slc, v_slc, k_win, v_win), gates = nsa_project(zq, zkv, zg, pos)

    def with_past(pool, new):
        old = pool[page_table].reshape(B, past, NSA_KV_HEADS, NSA_HD)
        return jnp.concatenate([old, new.astype(old.dtype)], axis=1)

    o_cmp, sel, kb, vb = nsa_global(q, pos, with_past(kc_pool, k_cmp), with_past(vc_pool, v_cmp),
                                    with_past(ks_pool, k_slc), with_past(vs_pool, v_slc), pe, w1, b1, w2)
    wb = kw_buf.shape[1]
    kw = jnp.concatenate([kw_buf, k_win.astype(kw_buf.dtype)], axis=1)
    vw = jnp.concatenate([vw_buf, v_win.astype(vw_buf.dtype)], axis=1)
    kpos = past - wb + jnp.arange(wb + T)
    o_sel = sel_attend(q, pos, sel, kb, vb)
    o_win = win_attend(q, pos, kw, vw, kpos)
    o_nsa = nsa_combine(gates, o_cmp, o_sel, o_win)
    o_m, (C, n, m, buf) = mlstm_mix(zqk, zv, zo, zif, buf0, C0, n0, m0, conv_w, conv_b, b_if, T)
    mix = jnp.concatenate([o_nsa, o_m], axis=-1)
    return mix, (k_cmp, v_cmp, k_slc, v_slc, kw[:, T:], vw[:, T:], C, n, m, buf)


def setup_inputs(seed: int = 0) -> dict:
    key = jax.random.key(seed)
    ks = jax.random.split(key, 32)
    f32 = jnp.float32
    n_pages = PAST_LEN // PAGE_SIZE
    n_used = DEC_BATCH * n_pages
    n_pool = n_used + max(1, n_used // 4)
    wb = min(WINDOW, PAST_LEN)

    def nrm(k, shape, scale=1.0):
        return jax.random.normal(k, shape, f32) * scale

    pool_shape = (DEPTH, n_pool, PAGE_SIZE, NSA_KV_HEADS, NSA_HD)
    win_shape = (DEPTH, DEC_BATCH, wb, NSA_KV_HEADS, NSA_HD)
    page_table = jax.random.permutation(ks[12], n_pool)[:n_used].reshape(DEC_BATCH, n_pages).astype(jnp.int32)
    b_i = nrm(ks[21], (DEPTH, M_HEADS), 0.1)
    b_f = jnp.linspace(3.0, 6.0, M_HEADS, dtype=f32)[None, :] + nrm(ks[22], (DEPTH, M_HEADS), 0.1)
    return {
        'x_prompt': nrm(ks[0], (BATCH, SEQ, D_MODEL)),
        'x_sample': nrm(ks[1], (DEC_BATCH, DEC_SEQ, D_MODEL)),
        'cache_k_cmp': nrm(ks[2], pool_shape),
        'cache_v_cmp': nrm(ks[3], pool_shape),
        'cache_k_slc': nrm(ks[4], pool_shape),
        'cache_v_slc': nrm(ks[5], pool_shape),
        'cache_k_win': nrm(ks[6], win_shape),
        'cache_v_win': nrm(ks[7], win_shape),
        'state_C': nrm(ks[8], (DEPTH, DEC_BATCH, M_HEADS, M_HD, M_HD), 0.1),
        'state_n': nrm(ks[9], (DEPTH, DEC_BATCH, M_HEADS, M_HD), 0.1),
        'state_m': nrm(ks[10], (DEPTH, DEC_BATCH, M_HEADS)),
        'state_conv': nrm(ks[11], (DEPTH, DEC_BATCH, CONV_W - 1, 2 * M_W)),
        'page_table': page_table,
        'w_in': nrm(ks[13], (DEPTH, D_MODEL, IN_W), D_MODEL ** -0.5),
        'w_out': nrm(ks[14], (DEPTH, MIX_W, D_MODEL), MIX_W ** -0.5 * BETA),
        'w_phi1': nrm(ks[15], (DEPTH, 2, CMP_BLOCK * NSA_HD, PHI_HIDDEN), (CMP_BLOCK * NSA_HD) ** -0.5),
        'b_phi1': nrm(ks[16], (DEPTH, 2, PHI_HIDDEN), 0.02),
        'w_phi2': nrm(ks[17], (DEPTH, 2, PHI_HIDDEN, NSA_HD), PHI_HIDDEN ** -0.5),
        'pe_cmp': nrm(ks[18], (DEPTH, 2, CMP_BLOCK, NSA_HD), 0.02),
        'conv_w': nrm(ks[19], (DEPTH, CONV_W, 2 * M_W), CONV_W ** -0.5),
        'conv_b': nrm(ks[20], (DEPTH, 2 * M_W), 0.02),
        'b_if': jnp.concatenate([b_i, b_f], axis=-1),
        'ln_g': 1.0 + nrm(ks[23], (DEPTH, 2, D_MODEL), 0.02),
        'ln_b': nrm(ks[24], (DEPTH, 2, D_MODEL), 0.02),
        'w_pq': nrm(ks[25], (DEPTH, D_MODEL, PEER_HEADS * PEER_QDIM), D_MODEL ** -0.5),
        'sub_keys': nrm(ks[26], (DEPTH, PEER_HEADS, 2, N_KEYS, PEER_QDIM // 2), (PEER_QDIM // 2) ** -0.5),
        'u_tab': nrm(ks[27], (DEPTH, N_EXPERTS, D_MODEL), D_MODEL ** -0.5),
        'v_tab': nrm(ks[28], (DEPTH, N_EXPERTS, D_MODEL), BETA),
    }


def reference(x_prompt, x_sample, cache_k_cmp, cache_v_cmp, cache_k_slc, cache_v_slc, cache_k_win, cache_v_win,
              state_C, state_n, state_m, state_conv, page_table, w_in, w_out, w_phi1, b_phi1, w_phi2, pe_cmp,
              conv_w, conv_b, b_if, ln_g, ln_b, w_pq, sub_keys, u_tab, v_tab):
    xp, xs = x_prompt, x_sample
    p_states, s_states = [], []
    for l in range(DEPTH):
        mix_p, st_p = prompt_mix(xp, w_in[l], pe_cmp[l], w_phi1[l], b_phi1[l], w_phi2[l],
                                 conv_w[l], conv_b[l], b_if[l])
        mix_s, st_s = sample_mix(xs, cache_k_cmp[l], cache_v_cmp[l], cache_k_slc[l], cache_v_slc[l],
                                 cache_k_win[l], cache_v_win[l], state_C[l], state_n[l], state_m[l],
                                 state_conv[l], page_table, w_in[l], pe_cmp[l], w_phi1[l], b_phi1[l],
                                 w_phi2[l], conv_w[l], conv_b[l], b_if[l])
        xp = block_tail(xp, mix_p, w_out[l], ln_g[l], ln_b[l], w_pq[l], sub_keys[l], u_tab[l], v_tab[l])
        xs = block_tail(xs, mix_s, w_out[l], ln_g[l], ln_b[l], w_pq[l], sub_keys[l], u_tab[l], v_tab[l])
        p_states.append(st_p)
        s_states.append(st_s)
    p_kc, p_vc, p_ks, p_vs, p_kw, p_vw, p_C, p_n, p_m, p_conv = [jnp.stack(a) for a in zip(*p_states)]
    s_kc, s_vc, s_ks, s_vs, s_kw, s_vw, s_C, s_n, s_m, s_conv = [jnp.stack(a) for a in zip(*s_states)]
    return (xp, xs, p_kc, p_vc, p_ks, p_vs, p_kw, p_vw, p_C, p_n, p_m, p_conv,
            s_kc, s_vc, s_ks, s_vs, s_kw, s_vw, s_C, s_n, s_m, s_conv)
```

```python
import jax
import jax.numpy as jnp
import numpy as np
from jax import lax
from jax.experimental import pallas as pl
from jax.experimental.pallas import tpu as pltpu

D_MODEL = 1024
DEPTH = 1
PAGE_SIZE = 128
NSA_HEADS = 8
NSA_KV_HEADS = 2
NSA_GROUP = NSA_HEADS // NSA_KV_HEADS
NSA_HD = 64
NSA_QW = NSA_HEADS * NSA_HD
NSA_KVW = NSA_KV_HEADS * NSA_HD
CMP_BLOCK = 32
CMP_STRIDE = 16
SEL_BLOCK = 64
SEL_TOP = 16
WINDOW = 512
Q_BLOCK = 64
ATTN_SCALE = NSA_HD ** -0.5
ROPE_THETA = 10000.0
M_HEADS = 4
M_HD = 128
M_W = M_HEADS * M_HD
M_CHUNK = 64
CONV_W = 4
PEER_HEADS = 8
N_KEYS = 128
PEER_TOPK = 16
PEER_QDIM = 256
PEER_BLOCK = 128
IN_SPLITS = (NSA_QW, 6 * NSA_KVW, 3 * NSA_HEADS, 2 * M_W, M_W, M_W, 2 * M_HEADS)
LN_EPS = 1e-5
ALPHA = (2 * DEPTH) ** 0.25

VMEM_LIMIT_BYTES = 56 * 1024 * 1024


def _mm_kernel(x_ref, w_ref, o_ref):
    o_ref[...] = jnp.dot(x_ref[...].astype(jnp.bfloat16), w_ref[...], preferred_element_type=jnp.float32)


def pallas_matmul(x, w, tm=512):
    M, K = x.shape
    N = w.shape[1]
    tm = min(tm, M)
    assert M % tm == 0
    return pl.pallas_call(
        _mm_kernel,
        out_shape=jax.ShapeDtypeStruct((M, N), jnp.float32),
        grid=(M // tm,),
        in_specs=[pl.BlockSpec((tm, K), lambda i: (i, 0)), pl.BlockSpec((K, N), lambda i: (0, 0))],
        out_specs=pl.BlockSpec((tm, N), lambda i: (i, 0)),
        compiler_params=pltpu.CompilerParams(dimension_semantics=("arbitrary",),
                                             vmem_limit_bytes=VMEM_LIMIT_BYTES),
        name="proj_matmul",
    )(x, w.astype(jnp.bfloat16))


def mm3(x, w):
    lead = x.shape[:-1]
    return pallas_matmul(x.reshape(-1, x.shape[-1]), w).reshape(*lead, w.shape[1])


def layer_norm(x, g, b):
    mu = x.mean(-1, keepdims=True)
    var = jnp.square(x - mu).mean(-1, keepdims=True)
    return (x - mu) * lax.rsqrt(var + LN_EPS) * g + b


def rope(x, pos):
    half = x.shape[-1] // 2
    inv = ROPE_THETA ** (-jnp.arange(half, dtype=jnp.float32) / half)
    ang = pos.astype(jnp.float32)[:, None] * inv[None, :]
    cos = jnp.cos(ang)[:, None, :]
    sin = jnp.sin(ang)[:, None, :]
    x1, x2 = x[..., :half], x[..., half:]
    return jnp.concatenate([x1 * cos - x2 * sin, x2 * cos + x1 * sin], axis=-1)


def split_in_proj(x, w_in):
    z = mm3(x, w_in)
    cuts = [int(c) for c in np.cumsum(IN_SPLITS)[:-1]]
    return jnp.split(z, cuts, axis=-1)


def nsa_project(zq, zkv, zg, pos):
    B, T, _ = zq.shape
    q = rope(zq.reshape(B, T, NSA_HEADS, NSA_HD), pos)
    kv = zkv.reshape(B, T, 6, NSA_KV_HEADS, NSA_HD)
    rows = (rope(kv[:, :, 0], pos), kv[:, :, 1], rope(kv[:, :, 2], pos), kv[:, :, 3],
            rope(kv[:, :, 4], pos), kv[:, :, 5])
    gates = jax.nn.sigmoid(zg).reshape(B, T, NSA_HEADS, 3)
    return q, rows, gates


def compress(rows, pe, w1, b1, w2):
    B, L, KV, hd = rows.shape
    r = CMP_BLOCK // CMP_STRIDE
    nch = L // CMP_STRIDE
    ch = rows[:, :nch * CMP_STRIDE].reshape(B, nch, CMP_STRIDE, KV, hd)
    nblk = nch - r + 1
    blocks = jnp.concatenate([ch[:, j:j + nblk] for j in range(r)], axis=2)
    blocks = blocks + pe[None, None, :, None, :]
    flat = blocks.transpose(0, 1, 3, 2, 4).reshape(B, nblk, KV, CMP_BLOCK * hd)
    hid = jax.nn.gelu(flat @ w1 + b1, approximate=False)
    return hid @ w2


def cmp_attend(q, qpos, kc, vc):
    B, T = q.shape[:2]
    qg = q.reshape(B, T, NSA_KV_HEADS, NSA_GROUP, NSA_HD)
    s = jnp.einsum('btngd,bcnd->btngc', qg, kc) * ATTN_SCALE
    nblk = kc.shape[1]
    blk_end = jnp.arange(nblk) * CMP_STRIDE + CMP_BLOCK - 1
    valid = (blk_end[None, :] <= qpos[:, None])[None, :, None, None, :]
    p = jax.nn.softmax(jnp.where(valid, s, -1e30), axis=-1) * valid
    o = jnp.einsum('btngc,bcnd->btngd', p, vc)
    return o.reshape(B, T, NSA_HEADS, NSA_HD), p


def select_blocks(p, qpos, n_sel):
    imp = p.sum(axis=3)
    R = SEL_BLOCK // CMP_STRIDE
    r = CMP_BLOCK // CMP_STRIDE
    nb = imp.shape[-1]
    right = n_sel * R + R - 1 - nb
    padded = jnp.pad(imp, ((0, 0), (0, 0), (0, 0), (r - 1, right)))
    score = padded[..., 0:(n_sel - 1) * R + 1:R]
    for o in range(1, R + r - 1):
        score = score + padded[..., o:o + (n_sel - 1) * R + 1:R]
    j = jnp.arange(n_sel)[None, :]
    cur = (qpos // SEL_BLOCK)[:, None]
    valid = (j * SEL_BLOCK <= qpos[:, None])[None, :, None, :]
    forced = ((j == 0) | (j == cur) | (j == cur - 1))[None, :, None, :]
    score = jnp.where(forced, jnp.inf, jnp.where(valid, score, -jnp.inf))
    _, sel = lax.top_k(score, min(SEL_TOP, n_sel))
    return sel


def to_blocks(rows, n_sel):
    B, L, KV, hd = rows.shape
    rows = jnp.pad(rows, ((0, 0), (0, n_sel * SEL_BLOCK - L), (0, 0), (0, 0)))
    return rows.reshape(B, n_sel, SEL_BLOCK, KV, hd).transpose(0, 3, 1, 2, 4)


def take_rows(table, idx):
    return table[idx]


def sel_attend(q, qpos, sel, kb, vb):
    B, Tq = q.shape[:2]
    k = sel.shape[-1]
    sel_t = sel.transpose(0, 2, 1, 3)
    gather = jax.vmap(jax.vmap(take_rows))
    kg = gather(kb, sel_t).reshape(B, NSA_KV_HEADS, Tq, k * SEL_BLOCK, NSA_HD)
    vg = gather(vb, sel_t).reshape(B, NSA_KV_HEADS, Tq, k * SEL_BLOCK, NSA_HD)
    kpos = (sel_t[..., None] * SEL_BLOCK + jnp.arange(SEL_BLOCK)).reshape(B, NSA_KV_HEADS, Tq, k * SEL_BLOCK)
    qg = q.reshape(B, Tq, NSA_KV_HEADS, NSA_GROUP, NSA_HD).transpose(0, 2, 1, 3, 4)
    s = jnp.einsum('bntgd,bntsd->bntgs', qg, kg) * ATTN_SCALE
    mask = kpos[:, :, :, None, :] <= qpos[None, None, :, None, None]
    pr = jax.nn.softmax(jnp.where(mask, s, -jnp.inf), axis=-1)
    o = jnp.einsum('bntgs,bntsd->bntgd', pr, vg)
    return o.transpose(0, 2, 1, 3, 4).reshape(B, Tq, NSA_HEADS, NSA_HD)


def win_attend(q, qpos, k, v, kpos):
    B, Tq = q.shape[:2]
    qg = q.reshape(B, Tq, NSA_KV_HEADS, NSA_GROUP, NSA_HD)
    s = jnp.einsum('btngd,bsnd->btngs', qg, k) * ATTN_SCALE
    diff = qpos[:, None] - kpos[None, :]
    mask = ((diff >= 0) & (diff < WINDOW) & (kpos[None, :] >= 0))[None, :, None, None, :]
    pr = jax.nn.softmax(jnp.where(mask, s, -jnp.inf), axis=-1)
    o = jnp.einsum('btngs,bsnd->btngd', pr, v)
    return o.reshape(B, Tq, NSA_HEADS, NSA_HD)


def nsa_global(q, qpos, k_cmp, v_cmp, k_slc, v_slc, pe, w1, b1, w2):
    kc = compress(k_cmp, pe[0], w1[0], b1[0], w2[0])
    vc = compress(v_cmp, pe[1], w1[1], b1[1], w2[1])
    o_cmp, p = cmp_attend(q, qpos, kc, vc)
    n_sel = -(-k_slc.shape[1] // SEL_BLOCK)
    sel = select_blocks(p, qpos, n_sel)
    return o_cmp, sel, to_blocks(k_slc, n_sel), to_blocks(v_slc, n_sel)


def nsa_combine(gates, o_cmp, o_sel, o_win):
    B, T = gates.shape[:2]
    o = gates[..., 0:1] * o_cmp + gates[..., 1:2] * o_sel + gates[..., 2:3] * o_win
    return o.reshape(B, T, NSA_QW)


def mlstm_chunk(carry, inp):
    C, n, m = carry
    q, k, v, ig, lf = inp
    L = q.shape[2]
    b = jnp.cumsum(lf, axis=-1)
    causal = jnp.tril(jnp.ones((L, L), dtype=bool))
    dmat = jnp.where(causal, b[..., :, None] - b[..., None, :] + ig[..., None, :], -jnp.inf)
    inter = b + m[..., None]
    m_t = jnp.maximum(inter, dmat.max(axis=-1))
    w_intra = jnp.exp(dmat - m_t[..., None])
    w_inter = jnp.exp(inter - m_t)
    s = jnp.einsum('bhtd,bhsd->bhts', q, k) * w_intra
    num = jnp.einsum('bhts,bhsv->bhtv', s, v) + w_inter[..., None] * jnp.einsum('bhtd,bhdv->bhtv', q, C)
    den = s.sum(-1) + w_inter * jnp.einsum('bhtd,bhd->bht', q, n)
    h = num / jnp.maximum(jnp.abs(den), jnp.exp(-m_t))[..., None]
    m_new = m_t[..., -1]
    w_s = jnp.exp(b[..., -1:] - b + ig - m_new[..., None])
    w_p = jnp.exp(b[..., -1] + m - m_new)
    C_new = w_p[..., None, None] * C + jnp.einsum('bhs,bhsd,bhsv->bhdv', w_s, k, v)
    n_new = w_p[..., None] * n + jnp.einsum('bhs,bhsd->bhd', w_s, k)
    return (C_new, n_new, m_new), h


def mlstm_mix(zqk, zv, zo, zif, buf0, C0, n0, m0, conv_w, conv_b, b_if, chunk):
    B, T, _ = zqk.shape
    full = jnp.concatenate([buf0, zqk], axis=1)
    conv = conv_b
    for j in range(CONV_W):
        conv = conv + full[:, j:j + T] * conv_w[j]
    qk = jax.nn.silu(conv)

    def heads(a):
        return a.reshape(B, T, M_HEADS, M_HD).transpose(0, 2, 1, 3)

    q = heads(qk[..., :M_W])
    k = heads(qk[..., M_W:]) * (M_HD ** -0.5)
    v = heads(zv)
    gif = zif + b_if
    ig = gif[..., :M_HEADS].transpose(0, 2, 1)
    lf = jax.nn.log_sigmoid(gif[..., M_HEADS:]).transpose(0, 2, 1)
    nc = T // chunk

    def to_chunks(a):
        return jnp.moveaxis(a.reshape(B, M_HEADS, nc, chunk, *a.shape[3:]), 2, 0)

    (C, n, m), h = lax.scan(mlstm_chunk, (C0, n0, m0),
                            (to_chunks(q), to_chunks(k), to_chunks(v), to_chunks(ig), to_chunks(lf)))
    h = jnp.moveaxis(h, 0, 2).reshape(B, M_HEADS, T, M_HD).transpose(0, 2, 1, 3).reshape(B, T, M_W)
    out = jax.nn.sigmoid(zo) * h
    return out, (C, n, m, full[:, T:])


def peer_ffn(x, w_pq, sub_keys, u_tab, v_tab):
    B, T, D = x.shape
    nt = B * T
    nb = -(-nt // PEER_BLOCK)
    xf = jnp.pad(x.reshape(nt, D), ((0, nb * PEER_BLOCK - nt), (0, 0)))
    qf = pallas_matmul(xf, w_pq).reshape(nb, PEER_BLOCK, PEER_HEADS, 2, PEER_QDIM // 2)
    xf = xf.reshape(nb, PEER_BLOCK, D)

    def one(args):
        xb, q = args
        s = jnp.einsum('phcd,hckd->phck', q, sub_keys)
        ts, ti = lax.top_k(s, PEER_TOPK)
        cand = ts[:, :, 0, :, None] + ts[:, :, 1, None, :]
        cidx = ti[:, :, 0, :, None] * N_KEYS + ti[:, :, 1, None, :]
        fs, fpos = lax.top_k(cand.reshape(PEER_BLOCK, PEER_HEADS, -1), PEER_TOPK)
        eidx = jnp.take_along_axis(cidx.reshape(PEER_BLOCK, PEER_HEADS, -1), fpos, axis=-1)
        g = jax.nn.softmax(fs, axis=-1)
        u = u_tab[eidx]
        v = v_tab[eidx]
        a = jax.nn.gelu(jnp.einsum('pd,phkd->phk', xb, u), approximate=False)
        return jnp.einsum('phk,phkd->pd', g * a, v)

    y = lax.map(one, (xf, qf))
    return y.reshape(nb * PEER_BLOCK, D)[:nt].reshape(B, T, D)


def block_tail(x, mix, w_out, ln_g, ln_b, w_pq, sub_keys, u_tab, v_tab):
    h = layer_norm(ALPHA * x + mm3(mix, w_out), ln_g[0], ln_b[0])
    return layer_norm(ALPHA * h + peer_ffn(h, w_pq, sub_keys, u_tab, v_tab), ln_g[1], ln_b[1])


def prompt_mix(x, w_in, pe, w1, b1, w2, conv_w, conv_b, b_if):
    B, S, _ = x.shape
    pos = jnp.arange(S)
    zq, zkv, zg, zqk, zv, zo, zif = split_in_proj(x, w_in)
    q, (k_cmp, v_cmp, k_slc, v_slc, k_win, v_win), gates = nsa_project(zq, zkv, zg, pos)
    o_cmp, sel, kb, vb = nsa_global(q, pos, k_cmp, v_cmp, k_slc, v_slc, pe, w1, b1, w2)
    pad = ((0, 0), (WINDOW, 0), (0, 0), (0, 0))
    kwp = jnp.pad(k_win, pad)
    vwp = jnp.pad(v_win, pad)

    def local(i):
        s0 = i * Q_BLOCK
        qb = lax.dynamic_slice_in_dim(q, s0, Q_BLOCK, axis=1)
        pb = lax.dynamic_slice_in_dim(pos, s0, Q_BLOCK, axis=0)
        sb = lax.dynamic_slice_in_dim(sel, s0, Q_BLOCK, axis=1)
        kw = lax.dynamic_slice_in_dim(kwp, s0, WINDOW + Q_BLOCK, axis=1)
        vw = lax.dynamic_slice_in_dim(vwp, s0, WINDOW + Q_BLOCK, axis=1)
        kpos = s0 - WINDOW + jnp.arange(WINDOW + Q_BLOCK)
        return sel_attend(qb, pb, sb, kb, vb), win_attend(qb, pb, kw, vw, kpos)

    o_sel, o_win = lax.map(local, jnp.arange(S // Q_BLOCK))
    o_sel = jnp.moveaxis(o_sel, 0, 1).reshape(B, S, NSA_HEADS, NSA_HD)
    o_win = jnp.moveaxis(o_win, 0, 1).reshape(B, S, NSA_HEADS, NSA_HD)
    o_nsa = nsa_combine(gates, o_cmp, o_sel, o_win)
    C0 = jnp.zeros((B, M_HEADS, M_HD, M_HD), jnp.float32)
    n0 = jnp.zeros((B, M_HEADS, M_HD), jnp.float32)
    m0 = jnp.zeros((B, M_HEADS), jnp.float32)
    buf0 = jnp.zeros((B, CONV_W - 1, 2 * M_W), x.dtype)
    o_m, (C, n, m, buf) = mlstm_mix(zqk, zv, zo, zif, buf0, C0, n0, m0, conv_w, conv_b, b_if, M_CHUNK)
    wl = min(WINDOW, S)
    mix = jnp.concatenate([o_nsa, o_m], axis=-1)
    return mix, (k_cmp, v_cmp, k_slc, v_slc, k_win[:, S - wl:], v_win[:, S - wl:], C, n, m, buf)


def sample_mix(x, kc_pool, vc_pool, ks_pool, vs_pool, kw_buf, vw_buf, C0, n0, m0, buf0, page_table,
               w_in, pe, w1, b1, w2, conv_w, conv_b, b_if):
    B, T, _ = x.shape
    past = page_table.shape[1] * PAGE_SIZE
    pos = past + jnp.arange(T)
    zq, zkv, zg, zqk, zv, zo, zif = split_in_proj(x, w_in)
    q, (k_cmp, v_cmp, k_slc, v_slc, k_win, v_win), gates = nsa_project(zq, zkv, zg, pos)

    def with_past(pool, new):
        old = pool[page_table].reshape(B, past, NSA_KV_HEADS, NSA_HD)
        return jnp.concatenate([old, new], axis=1)

    o_cmp, sel, kb, vb = nsa_global(q, pos, with_past(kc_pool, k_cmp), with_past(vc_pool, v_cmp),
                                    with_past(ks_pool, k_slc), with_past(vs_pool, v_slc), pe, w1, b1, w2)
    wb = kw_buf.shape[1]
    kw = jnp.concatenate([kw_buf, k_win], axis=1)
    vw = jnp.concatenate([vw_buf, v_win], axis=1)
    kpos = past - wb + jnp.arange(wb + T)
    o_sel = sel_attend(q, pos, sel, kb, vb)
    o_win = win_attend(q, pos, kw, vw, kpos)
    o_nsa = nsa_combine(gates, o_cmp, o_sel, o_win)
    o_m, (C, n, m, buf) = mlstm_mix(zqk, zv, zo, zif, buf0, C0, n0, m0, conv_w, conv_b, b_if, T)
    mix = jnp.concatenate([o_nsa, o_m], axis=-1)
    return mix, (k_cmp, v_cmp, k_slc, v_slc, kw[:, T:], vw[:, T:], C, n, m, buf)


def kernel(x_prompt, x_sample, cache_k_cmp, cache_v_cmp, cache_k_slc, cache_v_slc, cache_k_win, cache_v_win,
           state_C, state_n, state_m, state_conv, page_table, w_in, w_out, w_phi1, b_phi1, w_phi2, pe_cmp,
           conv_w, conv_b, b_if, ln_g, ln_b, w_pq, sub_keys, u_tab, v_tab):
    l = 0
    mix_p, st_p = prompt_mix(x_prompt, w_in[l], pe_cmp[l], w_phi1[l], b_phi1[l], w_phi2[l],
                             conv_w[l], conv_b[l], b_if[l])
    mix_s, st_s = sample_mix(x_sample, cache_k_cmp[l], cache_v_cmp[l], cache_k_slc[l], cache_v_slc[l],
                             cache_k_win[l], cache_v_win[l], state_C[l], state_n[l], state_m[l],
                             state_conv[l], page_table, w_in[l], pe_cmp[l], w_phi1[l], b_phi1[l],
                             w_phi2[l], conv_w[l], conv_b[l], b_if[l])
    xp = block_tail(x_prompt, mix_p, w_out[l], ln_g[l], ln_b[l], w_pq[l], sub_keys[l], u_tab[l], v_tab[l])
    xs = block_tail(x_sample, mix_s, w_out[l], ln_g[l], ln_b[l], w_pq[l], sub_keys[l], u_tab[l], v_tab[l])
    return (xp, xs) + tuple(a[None] for a in st_p) + tuple(a[None] for a in st_s)
```

```python
import jax
import jax.numpy as jnp
import numpy as np
from jax import lax
from jax.experimental import pallas as pl
from jax.experimental.pallas import tpu as pltpu

D_MODEL = 1024
DEPTH = 1
PAGE_SIZE = 128
NSA_HEADS = 8
NSA_KV_HEADS = 2
NSA_GROUP = NSA_HEADS // NSA_KV_HEADS
NSA_HD = 64
NSA_QW = NSA_HEADS * NSA_HD
NSA_KVW = NSA_KV_HEADS * NSA_HD
CMP_BLOCK = 32
CMP_STRIDE = 16
SEL_BLOCK = 64
SEL_TOP = 16
WINDOW = 512
Q_BLOCK = 64
ATTN_SCALE = NSA_HD ** -0.5
ROPE_THETA = 10000.0
M_HEADS = 4
M_HD = 128
M_W = M_HEADS * M_HD
M_CHUNK = 64
CONV_W = 4
PEER_HEADS = 8
N_KEYS = 128
PEER_TOPK = 16
PEER_QDIM = 256
PEER_BLOCK = 128
IN_SPLITS = (NSA_QW, 6 * NSA_KVW, 3 * NSA_HEADS, 2 * M_W, M_W, M_W, 2 * M_HEADS)
LN_EPS = 1e-5
ALPHA = (2 * DEPTH) ** 0.25

VMEM_LIMIT_BYTES = 56 * 1024 * 1024


def _mm_kernel(x_ref, w_ref, o_ref):
    o_ref[...] = jnp.dot(x_ref[...].astype(jnp.bfloat16), w_ref[...], preferred_element_type=jnp.float32)


def pallas_matmul(x, w, tm=512):
    M, K = x.shape
    N = w.shape[1]
    tm = min(tm, M)
    assert M % tm == 0
    return pl.pallas_call(
        _mm_kernel,
        out_shape=jax.ShapeDtypeStruct((M, N), jnp.float32),
        grid=(M // tm,),
        in_specs=[pl.BlockSpec((tm, K), lambda i: (i, 0)), pl.BlockSpec((K, N), lambda i: (0, 0))],
        out_specs=pl.BlockSpec((tm, N), lambda i: (i, 0)),
        compiler_params=pltpu.CompilerParams(dimension_semantics=("arbitrary",),
                                             vmem_limit_bytes=VMEM_LIMIT_BYTES),
        name="proj_matmul",
    )(x, w.astype(jnp.bfloat16))


def mm3(x, w):
    lead = x.shape[:-1]
    return pallas_matmul(x.reshape(-1, x.shape[-1]), w).reshape(*lead, w.shape[1])


def layer_norm(x, g, b):
    mu = x.mean(-1, keepdims=True)
    var = jnp.square(x - mu).mean(-1, keepdims=True)
    return (x - mu) * lax.rsqrt(var + LN_EPS) * g + b


def rope(x, pos):
    half = x.shape[-1] // 2
    inv = ROPE_THETA ** (-jnp.arange(half, dtype=jnp.float32) / half)
    ang = pos.astype(jnp.float32)[:, None] * inv[None, :]
    cos = jnp.cos(ang)[:, None, :]
    sin = jnp.sin(ang)[:, None, :]
    x1, x2 = x[..., :half], x[..., half:]
    return jnp.concatenate([x1 * cos - x2 * sin, x2 * cos + x1 * sin], axis=-1)


def split_in_proj(x, w_in):
    z = mm3(x, w_in)
    cuts = [int(c) for c in np.cumsum(IN_SPLITS)[:-1]]
    return jnp.split(z, cuts, axis=-1)


def nsa_project(zq, zkv, zg, pos):
    B, T, _ = zq.shape
    q = rope(zq.reshape(B, T, NSA_HEADS, NSA_HD), pos)
    kv = zkv.reshape(B, T, 6, NSA_KV_HEADS, NSA_HD)
    rows = (rope(kv[:, :, 0], pos), kv[:, :, 1], rope(kv[:, :, 2], pos), kv[:, :, 3],
            rope(kv[:, :, 4], pos), kv[:, :, 5])
    gates = jax.nn.sigmoid(zg).reshape(B, T, NSA_HEADS, 3)
    return q, rows, gates


def compress(rows, pe, w1, b1, w2):
    B, L, KV, hd = rows.shape
    r = CMP_BLOCK // CMP_STRIDE
    nch = L // CMP_STRIDE
    ch = rows[:, :nch * CMP_STRIDE].reshape(B, nch, CMP_STRIDE, KV, hd)
    nblk = nch - r + 1
    blocks = jnp.concatenate([ch[:, j:j + nblk] for j in range(r)], axis=2)
    blocks = blocks + pe[None, None, :, None, :]
    flat = blocks.transpose(0, 1, 3, 2, 4).reshape(B, nblk, KV, CMP_BLOCK * hd)
    hid = jax.nn.gelu(flat @ w1 + b1, approximate=False)
    return hid @ w2


def cmp_attend(q, qpos, kc, vc):
    B, T = q.shape[:2]
    qg = q.reshape(B, T, NSA_KV_HEADS, NSA_GROUP, NSA_HD)
    s = jnp.einsum('btngd,bcnd->btngc', qg, kc) * ATTN_SCALE
    nblk = kc.shape[1]
    blk_end = jnp.arange(nblk) * CMP_STRIDE + CMP_BLOCK - 1
    valid = (blk_end[None, :] <= qpos[:, None])[None, :, None, None, :]
    p = jax.nn.softmax(jnp.where(valid, s, -1e30), axis=-1) * valid
    o = jnp.einsum('btngc,bcnd->btngd', p, vc)
    return o.reshape(B, T, NSA_HEADS, NSA_HD), p


def select_blocks(p, qpos, n_sel):
    imp = p.sum(axis=3)
    R = SEL_BLOCK // CMP_STRIDE
    r = CMP_BLOCK // CMP_STRIDE
    nb = imp.shape[-1]
    right = n_sel * R + R - 1 - nb
    padded = jnp.pad(imp, ((0, 0), (0, 0), (0, 0), (r - 1, right)))
    score = padded[..., 0:(n_sel - 1) * R + 1:R]
    for o in range(1, R + r - 1):
        score = score + padded[..., o:o + (n_sel - 1) * R + 1:R]
    j = jnp.arange(n_sel)[None, :]
    cur = (qpos // SEL_BLOCK)[:, None]
    valid = (j * SEL_BLOCK <= qpos[:, None])[None, :, None, :]
    forced = ((j == 0) | (j == cur) | (j == cur - 1))[None, :, None, :]
    score = jnp.where(forced, jnp.inf, jnp.where(valid, score, -jnp.inf))
    _, sel = lax.top_k(score, min(SEL_TOP, n_sel))
    return sel


def to_blocks(rows, n_sel):
    B, L, KV, hd = rows.shape
    rows = jnp.pad(rows, ((0, 0), (0, n_sel * SEL_BLOCK - L), (0, 0), (0, 0)))
    return rows.reshape(B, n_sel, SEL_BLOCK, KV, hd).transpose(0, 3, 1, 2, 4)


def take_rows(table, idx):
    return table[idx]


def sel_attend(q, qpos, sel, kb, vb):
    B, Tq = q.shape[:2]
    k = sel.shape[-1]
    sel_t = sel.transpose(0, 2, 1, 3)
    gather = jax.vmap(jax.vmap(take_rows))
    kg = gather(kb, sel_t).reshape(B, NSA_KV_HEADS, Tq, k * SEL_BLOCK, NSA_HD)
    vg = gather(vb, sel_t).reshape(B, NSA_KV_HEADS, Tq, k * SEL_BLOCK, NSA_HD)
    kpos = (sel_t[..., None] * SEL_BLOCK + jnp.arange(SEL_BLOCK)).reshape(B, NSA_KV_HEADS, Tq, k * SEL_BLOCK)
    qg = q.reshape(B, Tq, NSA_KV_HEADS, NSA_GROUP, NSA_HD).transpose(0, 2, 1, 3, 4)
    s = jnp.einsum('bntgd,bntsd->bntgs', qg, kg) * ATTN_SCALE
    mask = kpos[:, :, :, None, :] <= qpos[None, None, :, None, None]
    pr = jax.nn.softmax(jnp.where(mask, s, -jnp.inf), axis=-1)
    o = jnp.einsum('bntgs,bntsd->bntgd', pr, vg)
    return o.transpose(0, 2, 1, 3, 4).reshape(B, Tq, NSA_HEADS, NSA_HD)


def win_attend(q, qpos, k, v, kpos):
    B, Tq = q.shape[:2]
    qg = q.reshape(B, Tq, NSA_KV_HEADS, NSA_GROUP, NSA_HD)
    s = jnp.einsum('btngd,bsnd->btngs', qg, k) * ATTN_SCALE
    diff = qpos[:, None] - kpos[None, :]
    mask = ((diff >= 0) & (diff < WINDOW) & (kpos[None, :] >= 0))[None, :, None, None, :]
    pr = jax.nn.softmax(jnp.where(mask, s, -jnp.inf), axis=-1)
    o = jnp.einsum('btngs,bsnd->btngd', pr, v)
    return o.reshape(B, Tq, NSA_HEADS, NSA_HD)


def nsa_global(q, qpos, k_cmp, v_cmp, k_slc, v_slc, pe, w1, b1, w2):
    kc = compress(k_cmp, pe[0], w1[0], b1[0], w2[0])
    vc = compress(v_cmp, pe[1], w1[1], b1[1], w2[1])
    o_cmp, p = cmp_attend(q, qpos, kc, vc)
    n_sel = -(-k_slc.shape[1] // SEL_BLOCK)
    sel = select_blocks(p, qpos, n_sel)
    return o_cmp, sel, to_blocks(k_slc, n_sel), to_blocks(v_slc, n_sel)


def nsa_combine(gates, o_cmp, o_sel, o_win):
    B, T = gates.shape[:2]
    o = gates[..., 0:1] * o_cmp + gates[..., 1:2] * o_sel + gates[..., 2:3] * o_win
    return o.reshape(B, T, NSA_QW)


def mlstm_chunk(carry, inp):
    C, n, m = carry
    q, k, v, ig, lf = inp
    L = q.shape[2]
    b = jnp.cumsum(lf, axis=-1)
    causal = jnp.tril(jnp.ones((L, L), dtype=bool))
    dmat = jnp.where(causal, b[..., :, None] - b[..., None, :] + ig[..., None, :], -jnp.inf)
    inter = b + m[..., None]
    m_t = jnp.maximum(inter, dmat.max(axis=-1))
    w_intra = jnp.exp(dmat - m_t[..., None])
    w_inter = jnp.exp(inter - m_t)
    s = jnp.einsum('bhtd,bhsd->bhts', q, k) * w_intra
    num = jnp.einsum('bhts,bhsv->bhtv', s, v) + w_inter[..., None] * jnp.einsum('bhtd,bhdv->bhtv', q, C)
    den = s.sum(-1) + w_inter * jnp.einsum('bhtd,bhd->bht', q, n)
    h = num / jnp.maximum(jnp.abs(den), jnp.exp(-m_t))[..., None]
    m_new = m_t[..., -1]
    w_s = jnp.exp(b[..., -1:] - b + ig - m_new[..., None])
    w_p = jnp.exp(b[..., -1] + m - m_new)
    C_new = w_p[..., None, None] * C + jnp.einsum('bhs,bhsd,bhsv->bhdv', w_s, k, v)
    n_new = w_p[..., None] * n + jnp.einsum('bhs,bhsd->bhd', w_s, k)
    return (C_new, n_new, m_new), h


def mlstm_mix(zqk, zv, zo, zif, buf0, C0, n0, m0, conv_w, conv_b, b_if, chunk):
    B, T, _ = zqk.shape
    full = jnp.concatenate([buf0, zqk], axis=1)
    conv = conv_b
    for j in range(CONV_W):
        conv = conv + full[:, j:j + T] * conv_w[j]
    qk = jax.nn.silu(conv)

    def heads(a):
        return a.reshape(B, T, M_HEADS, M_HD).transpose(0, 2, 1, 3)

    q = heads(qk[..., :M_W])
    k = heads(qk[..., M_W:]) * (M_HD ** -0.5)
    v = heads(zv)
    gif = zif + b_if
    ig = gif[..., :M_HEADS].transpose(0, 2, 1)
    lf = jax.nn.log_sigmoid(gif[..., M_HEADS:]).transpose(0, 2, 1)
    nc = T // chunk

    def to_chunks(a):
        return jnp.moveaxis(a.reshape(B, M_HEADS, nc, chunk, *a.shape[3:]), 2, 0)

    (C, n, m), h = lax.scan(mlstm_chunk, (C0, n0, m0),
                            (to_chunks(q), to_chunks(k), to_chunks(v), to_chunks(ig), to_chunks(lf)))
    h = jnp.moveaxis(h, 0, 2).reshape(B, M_HEADS, T, M_HD).transpose(0, 2, 1, 3).reshape(B, T, M_W)
    out = jax.nn.sigmoid(zo) * h
    return out, (C, n, m, full[:, T:])


PEER_COMBOS = 2 * PEER_HEADS
PEER_KEY_ROWS = 8
PEER_TILE = PEER_KEY_ROWS * N_KEYS
_NT = (((1,), (1,)), ((), ()))


def _peer_topk_kernel(q_ref, keys_ref, s_ref, e0_ref, e1_ref, tau_ref, ts_ref):
    c = pl.program_id(1)
    s = lax.dot_general(keys_ref[0], q_ref[...].astype(jnp.bfloat16), _NT,
                        preferred_element_type=jnp.float32)
    s_ref[c] = s
    key_id = lax.broadcasted_iota(jnp.int32, s.shape, 0)
    work = s
    rows = []
    for _ in range(PEER_TOPK):
        m = jnp.max(work, axis=0, keepdims=True)
        first = jnp.min(jnp.where(work == m, key_id, N_KEYS), axis=0, keepdims=True)
        work = jnp.where(key_id == first, -jnp.inf, work)
        rows.append(m)
    ts_ref[c] = jnp.concatenate(rows, axis=0)

    @pl.when(c == PEER_COMBOS - 1)
    def _():
        for h in range(PEER_HEADS):
            t0 = ts_ref[2 * h]
            t1 = ts_ref[2 * h + 1]
            pieces = [t0[0:1] + t1] + [t0[a:a + 1] + t1[0:8] for a in range(1, 8)] + [t0[8:16] + t1[0:1]]
            cand = jnp.concatenate(pieces, axis=0)
            top = t0[0:1] + t1[0:1]
            tau = top
            z = jnp.zeros_like(top)
            seen = jnp.zeros_like(top)
            for _ in range(PEER_TOPK):
                m = jnp.max(cand, axis=0, keepdims=True)
                eq = cand == m
                cnt = jnp.sum(jnp.where(eq, 1.0, 0.0), axis=0, keepdims=True)
                active = seen < PEER_TOPK
                take = jnp.minimum(cnt, PEER_TOPK - seen)
                tau = jnp.where(active, m, tau)
                z = z + jnp.where(active, take * jnp.exp(m - top), 0.0)
                seen = seen + cnt
                cand = jnp.where(eq, -jnp.inf, cand)
            tau_ref[h:h + 1, :] = tau
            e0_ref[h] = jnp.exp(s_ref[2 * h] - t0[0:1]) / z
            e1_ref[h] = jnp.exp(s_ref[2 * h + 1] - t1[0:1])


def peer_scores(q, sub_keys, tt):
    n = q.shape[0]
    assert n % tt == 0
    keys = sub_keys.reshape(PEER_COMBOS, N_KEYS, PEER_QDIM // 2).astype(jnp.bfloat16)
    f32 = jnp.float32
    return pl.pallas_call(
        _peer_topk_kernel,
        out_shape=(jax.ShapeDtypeStruct((PEER_COMBOS, N_KEYS, n), f32),
                   jax.ShapeDtypeStruct((PEER_HEADS, N_KEYS, n), f32),
                   jax.ShapeDtypeStruct((PEER_HEADS, N_KEYS, n), f32),
                   jax.ShapeDtypeStruct((PEER_HEADS, n), f32)),
        grid=(n // tt, PEER_COMBOS),
        in_specs=[pl.BlockSpec((tt, PEER_QDIM // 2), lambda i, c: (i, c)),
                  pl.BlockSpec((1, N_KEYS, PEER_QDIM // 2), lambda i, c: (c, 0, 0))],
        out_specs=(pl.BlockSpec((PEER_COMBOS, N_KEYS, tt), lambda i, c: (0, 0, i)),
                   pl.BlockSpec((PEER_HEADS, N_KEYS, tt), lambda i, c: (0, 0, i)),
                   pl.BlockSpec((PEER_HEADS, N_KEYS, tt), lambda i, c: (0, 0, i)),
                   pl.BlockSpec((PEER_HEADS, tt), lambda i, c: (0, i))),
        scratch_shapes=[pltpu.VMEM((PEER_COMBOS, PEER_TOPK, tt), f32)],
        compiler_params=pltpu.CompilerParams(dimension_semantics=("arbitrary", "arbitrary"),
                                             vmem_limit_bytes=VMEM_LIMIT_BYTES),
        name="peer_topk",
    )(q, keys)


def _peer_dense_kernel(xb_ref, h_ref, u_ref, vt_ref, s0_ref, ez_ref, s_ref, e1_ref, tau_ref, g_ref, b_ref,
                       o_ref, acc_ref):
    e = pl.program_id(1)

    @pl.when(e == 0)
    def _():
        acc_ref[...] = jnp.zeros_like(acc_ref)

    a = lax.dot_general(u_ref[...], xb_ref[...], _NT, preferred_element_type=jnp.float32)
    ws = []
    for r in range(PEER_KEY_ROWS):
        ar = a[r * N_KEYS:(r + 1) * N_KEYS]
        act = 0.5 * ar * (1.0 + lax.erf(ar * (2.0 ** -0.5)))
        gate = jnp.zeros_like(ar)
        for h in range(PEER_HEADS):
            pair = s_ref[2 * h + 1] + s0_ref[2 * h, r:r + 1, :]
            gate = gate + jnp.where(pair >= tau_ref[h:h + 1, :], e1_ref[h], 0.0) * ez_ref[h, r:r + 1, :]
        ws.append((gate * act).astype(jnp.bfloat16))
    w = jnp.concatenate(ws, axis=0)
    acc_ref[...] += jnp.dot(vt_ref[...], w, preferred_element_type=jnp.float32)

    @pl.when(e == pl.num_programs(1) - 1)
    def _():
        r = ALPHA * h_ref[...] + acc_ref[...].T
        mu = jnp.mean(r, axis=-1, keepdims=True)
        d = r - mu
        var = jnp.mean(d * d, axis=-1, keepdims=True)
        o_ref[...] = d * lax.rsqrt(var + LN_EPS) * g_ref[...] + b_ref[...]


def peer_tail(h, q, sub_keys, u_bf, vt_bf, ln_g, ln_b, tt):
    n, d = h.shape
    s, e0z, e1, tau = peer_scores(q, sub_keys, tt)
    n_exp = u_bf.shape[0]
    return pl.pallas_call(
        _peer_dense_kernel,
        out_shape=jax.ShapeDtypeStruct((n, d), jnp.float32),
        grid=(n // tt, n_exp // PEER_TILE),
        in_specs=[pl.BlockSpec((tt, d), lambda i, e: (i, 0)),
                  pl.BlockSpec((tt, d), lambda i, e: (i, 0)),
                  pl.BlockSpec((PEER_TILE, d), lambda i, e: (e, 0)),
                  pl.BlockSpec((d, PEER_TILE), lambda i, e: (0, e)),
                  pl.BlockSpec((PEER_COMBOS, PEER_KEY_ROWS, tt), lambda i, e: (0, e, i)),
                  pl.BlockSpec((PEER_HEADS, PEER_KEY_ROWS, tt), lambda i, e: (0, e, i)),
                  pl.BlockSpec((PEER_COMBOS, N_KEYS, tt), lambda i, e: (0, 0, i)),
                  pl.BlockSpec((PEER_HEADS, N_KEYS, tt), lambda i, e: (0, 0, i)),
                  pl.BlockSpec((PEER_HEADS, tt), lambda i, e: (0, i)),
                  pl.BlockSpec((1, d), lambda i, e: (0, 0)),
                  pl.BlockSpec((1, d), lambda i, e: (0, 0))],
        out_specs=pl.BlockSpec((tt, d), lambda i, e: (i, 0)),
        scratch_shapes=[pltpu.VMEM((d, tt), jnp.float32)],
        compiler_params=pltpu.CompilerParams(dimension_semantics=("arbitrary", "arbitrary"),
                                             vmem_limit_bytes=VMEM_LIMIT_BYTES),
        name="peer_dense",
    )(h.astype(jnp.bfloat16), h, u_bf, vt_bf, s, e0z, s, e1, tau, ln_g[None], ln_b[None])


def block_tail(x, mix, w_out, ln_g, ln_b, w_pq, sub_keys, u_bf, vt_bf, tt):
    lead = x.shape[:-1]
    h = layer_norm(ALPHA * x + mm3(mix, w_out), ln_g[0], ln_b[0]).reshape(-1, D_MODEL)
    q = pallas_matmul(h, w_pq)
    return peer_tail(h, q, sub_keys, u_bf, vt_bf, ln_g[1], ln_b[1], tt).reshape(*lead, D_MODEL)


def prompt_mix(x, w_in, pe, w1, b1, w2, conv_w, conv_b, b_if):
    B, S, _ = x.shape
    pos = jnp.arange(S)
    zq, zkv, zg, zqk, zv, zo, zif = split_in_proj(x, w_in)
    q, (k_cmp, v_cmp, k_slc, v_slc, k_win, v_win), gates = nsa_project(zq, zkv, zg, pos)
    o_cmp, sel, kb, vb = nsa_global(q, pos, k_cmp, v_cmp, k_slc, v_slc, pe, w1, b1, w2)
    pad = ((0, 0), (WINDOW, 0), (0, 0), (0, 0))
    kwp = jnp.pad(k_win, pad)
    vwp = jnp.pad(v_win, pad)

    def local(i):
        s0 = i * Q_BLOCK
        qb = lax.dynamic_slice_in_dim(q, s0, Q_BLOCK, axis=1)
        pb = lax.dynamic_slice_in_dim(pos, s0, Q_BLOCK, axis=0)
        sb = lax.dynamic_slice_in_dim(sel, s0, Q_BLOCK, axis=1)
        kw = lax.dynamic_slice_in_dim(kwp, s0, WINDOW + Q_BLOCK, axis=1)
        vw = lax.dynamic_slice_in_dim(vwp, s0, WINDOW + Q_BLOCK, axis=1)
        kpos = s0 - WINDOW + jnp.arange(WINDOW + Q_BLOCK)
        return sel_attend(qb, pb, sb, kb, vb), win_attend(qb, pb, kw, vw, kpos)

    o_sel, o_win = lax.map(local, jnp.arange(S // Q_BLOCK))
    o_sel = jnp.moveaxis(o_sel, 0, 1).reshape(B, S, NSA_HEADS, NSA_HD)
    o_win = jnp.moveaxis(o_win, 0, 1).reshape(B, S, NSA_HEADS, NSA_HD)
    o_nsa = nsa_combine(gates, o_cmp, o_sel, o_win)
    C0 = jnp.zeros((B, M_HEADS, M_HD, M_HD), jnp.float32)
    n0 = jnp.zeros((B, M_HEADS, M_HD), jnp.float32)
    m0 = jnp.zeros((B, M_HEADS), jnp.float32)
    buf0 = jnp.zeros((B, CONV_W - 1, 2 * M_W), x.dtype)
    o_m, (C, n, m, buf) = mlstm_mix(zqk, zv, zo, zif, buf0, C0, n0, m0, conv_w, conv_b, b_if, M_CHUNK)
    wl = min(WINDOW, S)
    mix = jnp.concatenate([o_nsa, o_m], axis=-1)
    return mix, (k_cmp, v_cmp, k_slc, v_slc, k_win[:, S - wl:], v_win[:, S - wl:], C, n, m, buf)


def sample_mix(x, kc_pool, vc_pool, ks_pool, vs_pool, kw_buf, vw_buf, C0, n0, m0, buf0, page_table,
               w_in, pe, w1, b1, w2, conv_w, conv_b, b_if):
    B, T, _ = x.shape
    past = page_table.shape[1] * PAGE_SIZE
    pos = past + jnp.arange(T)
    zq, zkv, zg, zqk, zv, zo, zif = split_in_proj(x, w_in)
    q, (k_cmp, v_cmp, k_slc, v_slc, k_win, v_win), gates = nsa_project(zq, zkv, zg, pos)

    def with_past(pool, new):
        old = pool[page_table].reshape(B, past, NSA_KV_HEADS, NSA_HD)
        return jnp.concatenate([old, new], axis=1)

    o_cmp, sel, kb, vb = nsa_global(q, pos, with_past(kc_pool, k_cmp), with_past(vc_pool, v_cmp),
                                    with_past(ks_pool, k_slc), with_past(vs_pool, v_slc), pe, w1, b1, w2)
    wb = kw_buf.shape[1]
    kw = jnp.concatenate([kw_buf, k_win], axis=1)
    vw = jnp.concatenate([vw_buf, v_win], axis=1)
    kpos = past - wb + jnp.arange(wb + T)
    o_sel = sel_attend(q, pos, sel, kb, vb)
    o_win = win_attend(q, pos, kw, vw, kpos)
    o_nsa = nsa_combine(gates, o_cmp, o_sel, o_win)
    o_m, (C, n, m, buf) = mlstm_mix(zqk, zv, zo, zif, buf0, C0, n0, m0, conv_w, conv_b, b_if, T)
    mix = jnp.concatenate([o_nsa, o_m], axis=-1)
    return mix, (k_cmp, v_cmp, k_slc, v_slc, kw[:, T:], vw[:, T:], C, n, m, buf)


def kernel(x_prompt, x_sample, cache_k_cmp, cache_v_cmp, cache_k_slc, cache_v_slc, cache_k_win, cache_v_win,
           state_C, state_n, state_m, state_conv, page_table, w_in, w_out, w_phi1, b_phi1, w_phi2, pe_cmp,
           conv_w, conv_b, b_if, ln_g, ln_b, w_pq, sub_keys, u_tab, v_tab):
    l = 0
    mix_p, st_p = prompt_mix(x_prompt, w_in[l], pe_cmp[l], w_phi1[l], b_phi1[l], w_phi2[l],
                             conv_w[l], conv_b[l], b_if[l])
    mix_s, st_s = sample_mix(x_sample, cache_k_cmp[l], cache_v_cmp[l], cache_k_slc[l], cache_v_slc[l],
                             cache_k_win[l], cache_v_win[l], state_C[l], state_n[l], state_m[l],
                             state_conv[l], page_table, w_in[l], pe_cmp[l], w_phi1[l], b_phi1[l],
                             w_phi2[l], conv_w[l], conv_b[l], b_if[l])
    u_bf = u_tab[l].astype(jnp.bfloat16)
    vt_bf = v_tab[l].astype(jnp.bfloat16).T
    xp = block_tail(x_prompt, mix_p, w_out[l], ln_g[l], ln_b[l], w_pq[l], sub_keys[l], u_bf, vt_bf, 512)
    xs = block_tail(x_sample, mix_s, w_out[l], ln_g[l], ln_b[l], w_pq[l], sub_keys[l], u_bf, vt_bf, 128)
    return (xp, xs) + tuple(a[None] for a in st_p) + tuple(a[None] for a in st_s)
```

```python
import jax
import jax.numpy as jnp
import numpy as np
from jax import lax
from jax.experimental import pallas as pl
from jax.experimental.pallas import tpu as pltpu

D_MODEL = 1024
DEPTH = 1
PAGE_SIZE = 128
NSA_HEADS = 8
NSA_KV_HEADS = 2
NSA_GROUP = NSA_HEADS // NSA_KV_HEADS
NSA_HD = 64
NSA_QW = NSA_HEADS * NSA_HD
NSA_KVW = NSA_KV_HEADS * NSA_HD
CMP_BLOCK = 32
CMP_STRIDE = 16
SEL_BLOCK = 64
SEL_TOP = 16
WINDOW = 512
Q_BLOCK = 64
ATTN_SCALE = NSA_HD ** -0.5
ROPE_THETA = 10000.0
M_HEADS = 4
M_HD = 128
M_W = M_HEADS * M_HD
M_CHUNK = 64
CONV_W = 4
PEER_HEADS = 8
N_KEYS = 128
PEER_TOPK = 16
PEER_QDIM = 256
PEER_BLOCK = 128
IN_SPLITS = (NSA_QW, 6 * NSA_KVW, 3 * NSA_HEADS, 2 * M_W, M_W, M_W, 2 * M_HEADS)
LN_EPS = 1e-5
ALPHA = (2 * DEPTH) ** 0.25

VMEM_LIMIT_BYTES = 56 * 1024 * 1024


def _mm_kernel(x_ref, w_ref, o_ref):
    o_ref[...] = jnp.dot(x_ref[...].astype(jnp.bfloat16), w_ref[...], preferred_element_type=jnp.float32)


def pallas_matmul(x, w, tm=512):
    M, K = x.shape
    N = w.shape[1]
    tm = min(tm, M)
    assert M % tm == 0
    return pl.pallas_call(
        _mm_kernel,
        out_shape=jax.ShapeDtypeStruct((M, N), jnp.float32),
        grid=(M // tm,),
        in_specs=[pl.BlockSpec((tm, K), lambda i: (i, 0)), pl.BlockSpec((K, N), lambda i: (0, 0))],
        out_specs=pl.BlockSpec((tm, N), lambda i: (i, 0)),
        compiler_params=pltpu.CompilerParams(dimension_semantics=("arbitrary",),
                                             vmem_limit_bytes=VMEM_LIMIT_BYTES),
        name="proj_matmul",
    )(x, w.astype(jnp.bfloat16))


def mm3(x, w):
    lead = x.shape[:-1]
    return pallas_matmul(x.reshape(-1, x.shape[-1]), w).reshape(*lead, w.shape[1])


def layer_norm(x, g, b):
    mu = x.mean(-1, keepdims=True)
    var = jnp.square(x - mu).mean(-1, keepdims=True)
    return (x - mu) * lax.rsqrt(var + LN_EPS) * g + b


def rope(x, pos):
    half = x.shape[-1] // 2
    inv = ROPE_THETA ** (-jnp.arange(half, dtype=jnp.float32) / half)
    ang = pos.astype(jnp.float32)[:, None] * inv[None, :]
    cos = jnp.cos(ang)[:, None, :]
    sin = jnp.sin(ang)[:, None, :]
    x1, x2 = x[..., :half], x[..., half:]
    return jnp.concatenate([x1 * cos - x2 * sin, x2 * cos + x1 * sin], axis=-1)


def split_in_proj(x, w_in):
    z = mm3(x, w_in)
    cuts = [int(c) for c in np.cumsum(IN_SPLITS)[:-1]]
    return jnp.split(z, cuts, axis=-1)


def nsa_project(zq, zkv, zg, pos):
    B, T, _ = zq.shape
    q = rope(zq.reshape(B, T, NSA_HEADS, NSA_HD), pos)
    kv = zkv.reshape(B, T, 6, NSA_KV_HEADS, NSA_HD)
    rows = (rope(kv[:, :, 0], pos), kv[:, :, 1], rope(kv[:, :, 2], pos), kv[:, :, 3],
            rope(kv[:, :, 4], pos), kv[:, :, 5])
    gates = jax.nn.sigmoid(zg).reshape(B, T, NSA_HEADS, 3)
    return q, rows, gates


def compress(rows, pe, w1, b1, w2):
    B, L, KV, hd = rows.shape
    r = CMP_BLOCK // CMP_STRIDE
    nch = L // CMP_STRIDE
    ch = rows[:, :nch * CMP_STRIDE].reshape(B, nch, CMP_STRIDE, KV, hd)
    nblk = nch - r + 1
    blocks = jnp.concatenate([ch[:, j:j + nblk] for j in range(r)], axis=2)
    blocks = blocks + pe[None, None, :, None, :]
    flat = blocks.transpose(0, 1, 3, 2, 4).reshape(B, nblk, KV, CMP_BLOCK * hd)
    hid = jax.nn.gelu(flat @ w1 + b1, approximate=False)
    return hid @ w2


def cmp_attend(q, qpos, kc, vc):
    B, T = q.shape[:2]
    qg = q.reshape(B, T, NSA_KV_HEADS, NSA_GROUP, NSA_HD)
    s = jnp.einsum('btngd,bcnd->btngc', qg, kc) * ATTN_SCALE
    nblk = kc.shape[1]
    blk_end = jnp.arange(nblk) * CMP_STRIDE + CMP_BLOCK - 1
    valid = (blk_end[None, :] <= qpos[:, None])[None, :, None, None, :]
    p = jax.nn.softmax(jnp.where(valid, s, -1e30), axis=-1) * valid
    o = jnp.einsum('btngc,bcnd->btngd', p, vc)
    return o.reshape(B, T, NSA_HEADS, NSA_HD), p


def select_blocks(p, qpos, n_sel):
    imp = p.sum(axis=3)
    R = SEL_BLOCK // CMP_STRIDE
    r = CMP_BLOCK // CMP_STRIDE
    nb = imp.shape[-1]
    right = n_sel * R + R - 1 - nb
    padded = jnp.pad(imp, ((0, 0), (0, 0), (0, 0), (r - 1, right)))
    score = padded[..., 0:(n_sel - 1) * R + 1:R]
    for o in range(1, R + r - 1):
        score = score + padded[..., o:o + (n_sel - 1) * R + 1:R]
    j = jnp.arange(n_sel)[None, :]
    cur = (qpos // SEL_BLOCK)[:, None]
    valid = (j * SEL_BLOCK <= qpos[:, None])[None, :, None, :]
    forced = ((j == 0) | (j == cur) | (j == cur - 1))[None, :, None, :]
    score = jnp.where(forced, jnp.inf, jnp.where(valid, score, -jnp.inf))
    _, sel = lax.top_k(score, min(SEL_TOP, n_sel))
    return sel


def to_blocks(rows, n_sel):
    B, L, KV, hd = rows.shape
    rows = jnp.pad(rows, ((0, 0), (0, n_sel * SEL_BLOCK - L), (0, 0), (0, 0)))
    return rows.reshape(B, n_sel, SEL_BLOCK, KV, hd).transpose(0, 3, 1, 2, 4)


def take_rows(table, idx):
    return table[idx]


def sel_attend(q, qpos, sel, kb, vb):
    B, Tq = q.shape[:2]
    k = sel.shape[-1]
    sel_t = sel.transpose(0, 2, 1, 3)
    gather = jax.vmap(jax.vmap(take_rows))
    kg = gather(kb, sel_t).reshape(B, NSA_KV_HEADS, Tq, k * SEL_BLOCK, NSA_HD)
    vg = gather(vb, sel_t).reshape(B, NSA_KV_HEADS, Tq, k * SEL_BLOCK, NSA_HD)
    kpos = (sel_t[..., None] * SEL_BLOCK + jnp.arange(SEL_BLOCK)).reshape(B, NSA_KV_HEADS, Tq, k * SEL_BLOCK)
    qg = q.reshape(B, Tq, NSA_KV_HEADS, NSA_GROUP, NSA_HD).transpose(0, 2, 1, 3, 4)
    s = jnp.einsum('bntgd,bntsd->bntgs', qg, kg) * ATTN_SCALE
    mask = kpos[:, :, :, None, :] <= qpos[None, None, :, None, None]
    pr = jax.nn.softmax(jnp.where(mask, s, -jnp.inf), axis=-1)
    o = jnp.einsum('bntgs,bntsd->bntgd', pr, vg)
    return o.transpose(0, 2, 1, 3, 4).reshape(B, Tq, NSA_HEADS, NSA_HD)


def win_attend(q, qpos, k, v, kpos):
    B, Tq = q.shape[:2]
    qg = q.reshape(B, Tq, NSA_KV_HEADS, NSA_GROUP, NSA_HD)
    s = jnp.einsum('btngd,bsnd->btngs', qg, k) * ATTN_SCALE
    diff = qpos[:, None] - kpos[None, :]
    mask = ((diff >= 0) & (diff < WINDOW) & (kpos[None, :] >= 0))[None, :, None, None, :]
    pr = jax.nn.softmax(jnp.where(mask, s, -jnp.inf), axis=-1)
    o = jnp.einsum('btngs,bsnd->btngd', pr, v)
    return o.reshape(B, Tq, NSA_HEADS, NSA_HD)


def nsa_global(q, qpos, k_cmp, v_cmp, k_slc, v_slc, pe, w1, b1, w2):
    kc = compress(k_cmp, pe[0], w1[0], b1[0], w2[0])
    vc = compress(v_cmp, pe[1], w1[1], b1[1], w2[1])
    o_cmp, p = cmp_attend(q, qpos, kc, vc)
    n_sel = -(-k_slc.shape[1] // SEL_BLOCK)
    sel = select_blocks(p, qpos, n_sel)
    return o_cmp, sel, to_blocks(k_slc, n_sel), to_blocks(v_slc, n_sel)


def nsa_combine(gates, o_cmp, o_sel, o_win):
    B, T = gates.shape[:2]
    o = gates[..., 0:1] * o_cmp + gates[..., 1:2] * o_sel + gates[..., 2:3] * o_win
    return o.reshape(B, T, NSA_QW)


NSA_TQ = 128
NSA_CK = 512
MASKED = -1e30


def _softmax_rows(s):
    m = jnp.max(s, axis=-1, keepdims=True)
    e = jnp.exp(s - m)
    return e / jnp.sum(e, axis=-1, keepdims=True)


def _nsa_prompt_kernel(q_ref, kc_ref, vc_ref, ks_ref, vs_ref, kw_ref, vw_ref, zg_ref, msel_ref, exp_ref, o_ref):
    f32, bf16 = jnp.float32, jnp.bfloat16
    tq = NSA_TQ
    q0 = pl.program_id(2) * tq
    qb = q_ref[0]
    qs = jnp.concatenate([qb[:, g * NSA_HD:(g + 1) * NSA_HD] for g in range(NSA_GROUP)], axis=0)
    tpos = q0 + lax.broadcasted_iota(jnp.int32, (tq, 1), 0)

    def per_head(a):
        return jnp.concatenate([a] * NSA_GROUP, axis=0)

    s = lax.dot_general(qs, kc_ref[0, 0], _NT, preferred_element_type=f32)
    cblk = lax.broadcasted_iota(jnp.int32, (tq, 128), 1)
    cvalid = cblk * CMP_STRIDE + (CMP_BLOCK - 1) <= tpos
    s = s + per_head(jnp.where(cvalid, 0.0, MASKED))
    e = jnp.exp(s - jnp.max(s, axis=-1, keepdims=True)) * per_head(jnp.where(cvalid, 1.0, 0.0))
    l = jnp.sum(e, axis=-1, keepdims=True)
    p = e / jnp.where(l > 0.0, l, 1.0)
    o_cmp = jnp.dot(p.astype(bf16), vc_ref[0, 0], preferred_element_type=f32)

    imp = p[0:tq]
    for g in range(1, NSA_GROUP):
        imp = imp + p[g * tq:(g + 1) * tq]
    hi = imp.astype(bf16)
    r1 = imp - hi.astype(f32)
    mid = r1.astype(bf16)
    lo = (r1 - mid.astype(f32)).astype(bf16)
    msel = msel_ref[...]
    score = (jnp.dot(hi, msel, preferred_element_type=f32) + jnp.dot(mid, msel, preferred_element_type=f32)
             + jnp.dot(lo, msel, preferred_element_type=f32))
    n_sel = score.shape[1]
    j = lax.broadcasted_iota(jnp.int32, (tq, n_sel), 1)
    cur = tpos // SEL_BLOCK
    forced = (j == 0) | (j == cur) | (j == cur - 1)
    score = jnp.where(forced, jnp.inf, jnp.where(j * SEL_BLOCK <= tpos, score, -jnp.inf))
    rank = jnp.zeros((tq, n_sel), f32)
    for jp in range(n_sel):
        col = score[:, jp:jp + 1]
        before = (col > score) | ((col == score) & (j > jp))
        rank = rank + jnp.where(before, 1.0, 0.0)
    sel01 = jnp.where(rank < SEL_TOP, 1.0, 0.0).astype(bf16)

    ck = NSA_CK
    rows = NSA_GROUP * tq

    def sel_chunk(c, carry):
        m, l, acc = carry
        k0 = pl.multiple_of(c * ck, ck)
        s = lax.dot_general(qs, ks_ref[0, 0, pl.ds(k0, ck), :], _NT, preferred_element_type=f32)
        chosen = jnp.dot(sel01, exp_ref[c], preferred_element_type=f32)
        kpos = k0 + lax.broadcasted_iota(jnp.int32, (tq, ck), 1)
        ok = (chosen > 0.5) & (kpos <= tpos)
        s = s + per_head(jnp.where(ok, 0.0, MASKED))
        m_new = jnp.maximum(m, jnp.max(s, axis=-1, keepdims=True))
        a = jnp.exp(m - m_new)
        pr = jnp.exp(s - m_new)
        l = a * l + jnp.sum(pr, axis=-1, keepdims=True)
        acc = a * acc + jnp.dot(pr.astype(bf16), vs_ref[0, 0, pl.ds(k0, ck), :], preferred_element_type=f32)
        return m_new, l, acc

    init = (jnp.full((rows, 1), MASKED, f32), jnp.zeros((rows, 1), f32), jnp.zeros((rows, NSA_HD), f32))
    n_chunks = (q0 + tq + ck - 1) // ck
    _, l_sel, acc_sel = lax.fori_loop(0, n_chunks, sel_chunk, init)
    o_sel = acc_sel / l_sel

    w0 = pl.multiple_of(jnp.maximum(q0 - WINDOW, 0), tq)
    wl = WINDOW + tq
    s = lax.dot_general(qs, kw_ref[0, 0, pl.ds(w0, wl), :], _NT, preferred_element_type=f32)
    diff = tpos - (w0 + lax.broadcasted_iota(jnp.int32, (tq, wl), 1))
    s = s + per_head(jnp.where((diff >= 0) & (diff < WINDOW), 0.0, MASKED))
    o_win = jnp.dot(_softmax_rows(s).astype(bf16), vw_ref[0, 0, pl.ds(w0, wl), :], preferred_element_type=f32)

    gates = jax.nn.sigmoid(zg_ref[0, 0])
    for g in range(NSA_GROUP):
        r = slice(g * tq, (g + 1) * tq)
        o_ref[0, :, g * NSA_HD:(g + 1) * NSA_HD] = (gates[:, 3 * g:3 * g + 1] * o_cmp[r]
                                                    + gates[:, 3 * g + 1:3 * g + 2] * o_sel[r]
                                                    + gates[:, 3 * g + 2:3 * g + 3] * o_win[r])


def nsa_prompt_attention(q, kc, vc, k_slc, v_slc, k_win, v_win, zg):
    B, S = q.shape[:2]
    bf16 = jnp.bfloat16
    assert S % NSA_CK == 0 and S % NSA_TQ == 0 and WINDOW % NSA_TQ == 0 and WINDOW + NSA_TQ <= S
    n_sel = S // SEL_BLOCK
    nb = kc.shape[1]
    assert nb <= 128

    def kv_major(a):
        return a.transpose(0, 2, 1, 3).astype(bf16)

    def pad_blocks(a):
        return jnp.pad(kv_major(a), ((0, 0), (0, 0), (0, 128 - nb), (0, 0)))

    qs = (q * ATTN_SCALE).reshape(B, S, NSA_QW).astype(bf16)
    zg4 = zg.reshape(B, S, NSA_KV_HEADS, 3 * NSA_GROUP).transpose(0, 2, 1, 3)
    c = np.arange(128)[:, None]
    jj = np.arange(n_sel)[None, :]
    ratio = SEL_BLOCK // CMP_STRIDE
    msel = ((c >= jj * ratio - (CMP_BLOCK // CMP_STRIDE - 1)) & (c <= jj * ratio + ratio - 1) & (c < nb))
    expand = (np.arange(S)[None, :] // SEL_BLOCK == np.arange(n_sel)[:, None])
    expand = expand.reshape(n_sel, S // NSA_CK, NSA_CK).transpose(1, 0, 2)
    row_spec = pl.BlockSpec((1, 1, S, NSA_HD), lambda b, n, i: (b, n, 0, 0))
    blk_spec = pl.BlockSpec((1, 1, 128, NSA_HD), lambda b, n, i: (b, n, 0, 0))
    return pl.pallas_call(
        _nsa_prompt_kernel,
        out_shape=jax.ShapeDtypeStruct((B, S, NSA_QW), jnp.float32),
        grid=(B, NSA_KV_HEADS, S // NSA_TQ),
        in_specs=[pl.BlockSpec((1, NSA_TQ, NSA_GROUP * NSA_HD), lambda b, n, i: (b, i, n)),
                  blk_spec, blk_spec, row_spec, row_spec, row_spec, row_spec,
                  pl.BlockSpec((1, 1, NSA_TQ, 3 * NSA_GROUP), lambda b, n, i: (b, n, i, 0)),
                  pl.BlockSpec((128, n_sel), lambda b, n, i: (0, 0)),
                  pl.BlockSpec((S // NSA_CK, n_sel, NSA_CK), lambda b, n, i: (0, 0, 0))],
        out_specs=pl.BlockSpec((1, NSA_TQ, NSA_GROUP * NSA_HD), lambda b, n, i: (b, i, n)),
        compiler_params=pltpu.CompilerParams(dimension_semantics=("arbitrary", "arbitrary", "arbitrary"),
                                             vmem_limit_bytes=VMEM_LIMIT_BYTES),
        name="nsa_prompt_attention",
    )(qs, pad_blocks(kc), pad_blocks(vc), kv_major(k_slc), kv_major(v_slc), kv_major(k_win), kv_major(v_win),
      zg4, jnp.asarray(msel, bf16), jnp.asarray(expand, bf16))


def mlstm_chunk(carry, inp):
    C, n, m = carry
    q, k, v, ig, lf = inp
    L = q.shape[2]
    b = jnp.cumsum(lf, axis=-1)
    causal = jnp.tril(jnp.ones((L, L), dtype=bool))
    dmat = jnp.where(causal, b[..., :, None] - b[..., None, :] + ig[..., None, :], -jnp.inf)
    inter = b + m[..., None]
    m_t = jnp.maximum(inter, dmat.max(axis=-1))
    w_intra = jnp.exp(dmat - m_t[..., None])
    w_inter = jnp.exp(inter - m_t)
    s = jnp.einsum('bhtd,bhsd->bhts', q, k) * w_intra
    num = jnp.einsum('bhts,bhsv->bhtv', s, v) + w_inter[..., None] * jnp.einsum('bhtd,bhdv->bhtv', q, C)
    den = s.sum(-1) + w_inter * jnp.einsum('bhtd,bhd->bht', q, n)
    h = num / jnp.maximum(jnp.abs(den), jnp.exp(-m_t))[..., None]
    m_new = m_t[..., -1]
    w_s = jnp.exp(b[..., -1:] - b + ig - m_new[..., None])
    w_p = jnp.exp(b[..., -1] + m - m_new)
    C_new = w_p[..., None, None] * C + jnp.einsum('bhs,bhsd,bhsv->bhdv', w_s, k, v)
    n_new = w_p[..., None] * n + jnp.einsum('bhs,bhsd->bhd', w_s, k)
    return (C_new, n_new, m_new), h


def mlstm_mix(zqk, zv, zo, zif, buf0, C0, n0, m0, conv_w, conv_b, b_if, chunk):
    B, T, _ = zqk.shape
    full = jnp.concatenate([buf0, zqk], axis=1)
    conv = conv_b
    for j in range(CONV_W):
        conv = conv + full[:, j:j + T] * conv_w[j]
    qk = jax.nn.silu(conv)

    def heads(a):
        return a.reshape(B, T, M_HEADS, M_HD).transpose(0, 2, 1, 3)

    q = heads(qk[..., :M_W])
    k = heads(qk[..., M_W:]) * (M_HD ** -0.5)
    v = heads(zv)
    gif = zif + b_if
    ig = gif[..., :M_HEADS].transpose(0, 2, 1)
    lf = jax.nn.log_sigmoid(gif[..., M_HEADS:]).transpose(0, 2, 1)
    nc = T // chunk

    def to_chunks(a):
        return jnp.moveaxis(a.reshape(B, M_HEADS, nc, chunk, *a.shape[3:]), 2, 0)

    (C, n, m), h = lax.scan(mlstm_chunk, (C0, n0, m0),
                            (to_chunks(q), to_chunks(k), to_chunks(v), to_chunks(ig), to_chunks(lf)))
    h = jnp.moveaxis(h, 0, 2).reshape(B, M_HEADS, T, M_HD).transpose(0, 2, 1, 3).reshape(B, T, M_W)
    out = jax.nn.sigmoid(zo) * h
    return out, (C, n, m, full[:, T:])


PEER_COMBOS = 2 * PEER_HEADS
PEER_KEY_ROWS = 8
PEER_TILE = PEER_KEY_ROWS * N_KEYS
_NT = (((1,), (1,)), ((), ()))


def _peer_topk_kernel(q_ref, keys_ref, s_ref, e0_ref, e1_ref, tau_ref, ts_ref):
    c = pl.program_id(1)
    s = lax.dot_general(keys_ref[0], q_ref[...].astype(jnp.bfloat16), _NT,
                        preferred_element_type=jnp.float32)
    s_ref[c] = s
    key_id = lax.broadcasted_iota(jnp.int32, s.shape, 0)
    work = s
    rows = []
    for _ in range(PEER_TOPK):
        m = jnp.max(work, axis=0, keepdims=True)
        first = jnp.min(jnp.where(work == m, key_id, N_KEYS), axis=0, keepdims=True)
        work = jnp.where(key_id == first, -jnp.inf, work)
        rows.append(m)
    ts_ref[c] = jnp.concatenate(rows, axis=0)

    @pl.when(c == PEER_COMBOS - 1)
    def _():
        for h in range(PEER_HEADS):
            t0 = ts_ref[2 * h]
            t1 = ts_ref[2 * h + 1]
            pieces = [t0[0:1] + t1] + [t0[a:a + 1] + t1[0:8] for a in range(1, 8)] + [t0[8:16] + t1[0:1]]
            cand = jnp.concatenate(pieces, axis=0)
            top = t0[0:1] + t1[0:1]
            tau = top
            z = jnp.zeros_like(top)
            seen = jnp.zeros_like(top)
            for _ in range(PEER_TOPK):
                m = jnp.max(cand, axis=0, keepdims=True)
                eq = cand == m
                cnt = jnp.sum(jnp.where(eq, 1.0, 0.0), axis=0, keepdims=True)
                active = seen < PEER_TOPK
                take = jnp.minimum(cnt, PEER_TOPK - seen)
                tau = jnp.where(active, m, tau)
                z = z + jnp.where(active, take * jnp.exp(m - top), 0.0)
                seen = seen + cnt
                cand = jnp.where(eq, -jnp.inf, cand)
            tau_ref[h:h + 1, :] = tau
            e0_ref[h] = jnp.exp(s_ref[2 * h] - t0[0:1]) / z
            e1_ref[h] = jnp.exp(s_ref[2 * h + 1] - t1[0:1])


def peer_scores(q, sub_keys, tt):
    n = q.shape[0]
    assert n % tt == 0
    keys = sub_keys.reshape(PEER_COMBOS, N_KEYS, PEER_QDIM // 2).astype(jnp.bfloat16)
    f32 = jnp.float32
    return pl.pallas_call(
        _peer_topk_kernel,
        out_shape=(jax.ShapeDtypeStruct((PEER_COMBOS, N_KEYS, n), f32),
                   jax.ShapeDtypeStruct((PEER_HEADS, N_KEYS, n), f32),
                   jax.ShapeDtypeStruct((PEER_HEADS, N_KEYS, n), f32),
                   jax.ShapeDtypeStruct((PEER_HEADS, n), f32)),
        grid=(n // tt, PEER_COMBOS),
        in_specs=[pl.BlockSpec((tt, PEER_QDIM // 2), lambda i, c: (i, c)),
                  pl.BlockSpec((1, N_KEYS, PEER_QDIM // 2), lambda i, c: (c, 0, 0))],
        out_specs=(pl.BlockSpec((PEER_COMBOS, N_KEYS, tt), lambda i, c: (0, 0, i)),
                   pl.BlockSpec((PEER_HEADS, N_KEYS, tt), lambda i, c: (0, 0, i)),
                   pl.BlockSpec((PEER_HEADS, N_KEYS, tt), lambda i, c: (0, 0, i)),
                   pl.BlockSpec((PEER_HEADS, tt), lambda i, c: (0, i))),
        scratch_shapes=[pltpu.VMEM((PEER_COMBOS, PEER_TOPK, tt), f32)],
        compiler_params=pltpu.CompilerParams(dimension_semantics=("arbitrary", "arbitrary"),
                                             vmem_limit_bytes=VMEM_LIMIT_BYTES),
        name="peer_topk",
    )(q, keys)


def _peer_dense_kernel(xb_ref, h_ref, u_ref, vt_ref, s0_ref, ez_ref, s_ref, e1_ref, tau_ref, g_ref, b_ref,
                       o_ref, acc_ref):
    e = pl.program_id(1)

    @pl.when(e == 0)
    def _():
        acc_ref[...] = jnp.zeros_like(acc_ref)

    a = lax.dot_general(u_ref[...], xb_ref[...], _NT, preferred_element_type=jnp.float32)
    ws = []
    for r in range(PEER_KEY_ROWS):
        ar = a[r * N_KEYS:(r + 1) * N_KEYS]
        act = 0.5 * ar * (1.0 + lax.erf(ar * (2.0 ** -0.5)))
        gate = jnp.zeros_like(ar)
        for h in range(PEER_HEADS):
            pair = s_ref[2 * h + 1] + s0_ref[2 * h, r:r + 1, :]
            gate = gate + jnp.where(pair >= tau_ref[h:h + 1, :], e1_ref[h], 0.0) * ez_ref[h, r:r + 1, :]
        ws.append((gate * act).astype(jnp.bfloat16))
    w = jnp.concatenate(ws, axis=0)
    acc_ref[...] += jnp.dot(vt_ref[...], w, preferred_element_type=jnp.float32)

    @pl.when(e == pl.num_programs(1) - 1)
    def _():
        r = ALPHA * h_ref[...] + acc_ref[...].T
        mu = jnp.mean(r, axis=-1, keepdims=True)
        d = r - mu
        var = jnp.mean(d * d, axis=-1, keepdims=True)
        o_ref[...] = d * lax.rsqrt(var + LN_EPS) * g_ref[...] + b_ref[...]


def peer_tail(h, q, sub_keys, u_bf, vt_bf, ln_g, ln_b, tt):
    n, d = h.shape
    s, e0z, e1, tau = peer_scores(q, sub_keys, tt)
    n_exp = u_bf.shape[0]
    return pl.pallas_call(
        _peer_dense_kernel,
        out_shape=jax.ShapeDtypeStruct((n, d), jnp.float32),
        grid=(n // tt, n_exp // PEER_TILE),
        in_specs=[pl.BlockSpec((tt, d), lambda i, e: (i, 0)),
                  pl.BlockSpec((tt, d), lambda i, e: (i, 0)),
                  pl.BlockSpec((PEER_TILE, d), lambda i, e: (e, 0)),
                  pl.BlockSpec((d, PEER_TILE), lambda i, e: (0, e)),
                  pl.BlockSpec((PEER_COMBOS, PEER_KEY_ROWS, tt), lambda i, e: (0, e, i)),
                  pl.BlockSpec((PEER_HEADS, PEER_KEY_ROWS, tt), lambda i, e: (0, e, i)),
                  pl.BlockSpec((PEER_COMBOS, N_KEYS, tt), lambda i, e: (0, 0, i)),
                  pl.BlockSpec((PEER_HEADS, N_KEYS, tt), lambda i, e: (0, 0, i)),
                  pl.BlockSpec((PEER_HEADS, tt), lambda i, e: (0, i)),
                  pl.BlockSpec((1, d), lambda i, e: (0, 0)),
                  pl.BlockSpec((1, d), lambda i, e: (0, 0))],
        out_specs=pl.BlockSpec((tt, d), lambda i, e: (i, 0)),
        scratch_shapes=[pltpu.VMEM((d, tt), jnp.float32)],
        compiler_params=pltpu.CompilerParams(dimension_semantics=("arbitrary", "arbitrary"),
                                             vmem_limit_bytes=VMEM_LIMIT_BYTES),
        name="peer_dense",
    )(h.astype(jnp.bfloat16), h, u_bf, vt_bf, s, e0z, s, e1, tau, ln_g[None], ln_b[None])


def block_tail(x, mix, w_out, ln_g, ln_b, w_pq, sub_keys, u_bf, vt_bf, tt):
    lead = x.shape[:-1]
    h = layer_norm(ALPHA * x + mm3(mix, w_out), ln_g[0], ln_b[0]).reshape(-1, D_MODEL)
    q = pallas_matmul(h, w_pq)
    return peer_tail(h, q, sub_keys, u_bf, vt_bf, ln_g[1], ln_b[1], tt).reshape(*lead, D_MODEL)


def prompt_mix(x, w_in, pe, w1, b1, w2, conv_w, conv_b, b_if):
    B, S, _ = x.shape
    pos = jnp.arange(S)
    zq, zkv, zg, zqk, zv, zo, zif = split_in_proj(x, w_in)
    q, (k_cmp, v_cmp, k_slc, v_slc, k_win, v_win), gates = nsa_project(zq, zkv, zg, pos)
    kc = compress(k_cmp, pe[0], w1[0], b1[0], w2[0])
    vc = compress(v_cmp, pe[1], w1[1], b1[1], w2[1])
    o_nsa = nsa_prompt_attention(q, kc, vc, k_slc, v_slc, k_win, v_win, zg)
    C0 = jnp.zeros((B, M_HEADS, M_HD, M_HD), jnp.float32)
    n0 = jnp.zeros((B, M_HEADS, M_HD), jnp.float32)
    m0 = jnp.zeros((B, M_HEADS), jnp.float32)
    buf0 = jnp.zeros((B, CONV_W - 1, 2 * M_W), x.dtype)
    o_m, (C, n, m, buf) = mlstm_mix(zqk, zv, zo, zif, buf0, C0, n0, m0, conv_w, conv_b, b_if, M_CHUNK)
    wl = min(WINDOW, S)
    mix = jnp.concatenate([o_nsa, o_m], axis=-1)
    return mix, (k_cmp, v_cmp, k_slc, v_slc, k_win[:, S - wl:], v_win[:, S - wl:], C, n, m, buf)


def sample_mix(x, kc_pool, vc_pool, ks_pool, vs_pool, kw_buf, vw_buf, C0, n0, m0, buf0, page_table,
               w_in, pe, w1, b1, w2, conv_w, conv_b, b_if):
    B, T, _ = x.shape
    past = page_table.shape[1] * PAGE_SIZE
    pos = past + jnp.arange(T)
    zq, zkv, zg, zqk, zv, zo, zif = split_in_proj(x, w_in)
    q, (k_cmp, v_cmp, k_slc, v_slc, k_win, v_win), gates = nsa_project(zq, zkv, zg, pos)

    def with_past(pool, new):
        old = pool[page_table].reshape(B, past, NSA_KV_HEADS, NSA_HD)
        return jnp.concatenate([old, new], axis=1)

    o_cmp, sel, kb, vb = nsa_global(q, pos, with_past(kc_pool, k_cmp), with_past(vc_pool, v_cmp),
                                    with_past(ks_pool, k_slc), with_past(vs_pool, v_slc), pe, w1, b1, w2)
    wb = kw_buf.shape[1]
    kw = jnp.concatenate([kw_buf, k_win], axis=1)
    vw = jnp.concatenate([vw_buf, v_win], axis=1)
    kpos = past - wb + jnp.arange(wb + T)
    o_sel = sel_attend(q, pos, sel, kb, vb)
    o_win = win_attend(q, pos, kw, vw, kpos)
    o_nsa = nsa_combine(gates, o_cmp, o_sel, o_win)
    o_m, (C, n, m, buf) = mlstm_mix(zqk, zv, zo, zif, buf0, C0, n0, m0, conv_w, conv_b, b_if, T)
    mix = jnp.concatenate([o_nsa, o_m], axis=-1)
    return mix, (k_cmp, v_cmp, k_slc, v_slc, kw[:, T:], vw[:, T:], C, n, m, buf)


def kernel(x_prompt, x_sample, cache_k_cmp, cache_v_cmp, cache_k_slc, cache_v_slc, cache_k_win, cache_v_win,
           state_C, state_n, state_m, state_conv, page_table, w_in, w_out, w_phi1, b_phi1, w_phi2, pe_cmp,
           conv_w, conv_b, b_if, ln_g, ln_b, w_pq, sub_keys, u_tab, v_tab):
    l = 0
    mix_p, st_p = prompt_mix(x_prompt, w_in[l], pe_cmp[l], w_phi1[l], b_phi1[l], w_phi2[l],
                             conv_w[l], conv_b[l], b_if[l])
    mix_s, st_s = sample_mix(x_sample, cache_k_cmp[l], cache_v_cmp[l], cache_k_slc[l], cache_v_slc[l],
                             cache_k_win[l], cache_v_win[l], state_C[l], state_n[l], state_m[l],
                             state_conv[l], page_table, w_in[l], pe_cmp[l], w_phi1[l], b_phi1[l],
                             w_phi2[l], conv_w[l], conv_b[l], b_if[l])
    u_bf = u_tab[l].astype(jnp.bfloat16)
    vt_bf = v_tab[l].astype(jnp.bfloat16).T
    xp = block_tail(x_prompt, mix_p, w_out[l], ln_g[l], ln_b[l], w_pq[l], sub_keys[l], u_bf, vt_bf, 512)
    xs = block_tail(x_sample, mix_s, w_out[l], ln_g[l], ln_b[l], w_pq[l], sub_keys[l], u_bf, vt_bf, 128)
    return (xp, xs) + tuple(a[None] for a in st_p) + tuple(a[None] for a in st_s)
```

```python
import jax
import jax.numpy as jnp
import numpy as np
from jax import lax
from jax.experimental import pallas as pl
from jax.experimental.pallas import tpu as pltpu

D_MODEL = 1024
DEPTH = 1
PAGE_SIZE = 128
NSA_HEADS = 8
NSA_KV_HEADS = 2
NSA_GROUP = NSA_HEADS // NSA_KV_HEADS
NSA_HD = 64
NSA_QW = NSA_HEADS * NSA_HD
NSA_KVW = NSA_KV_HEADS * NSA_HD
CMP_BLOCK = 32
CMP_STRIDE = 16
SEL_BLOCK = 64
SEL_TOP = 16
WINDOW = 512
Q_BLOCK = 64
ATTN_SCALE = NSA_HD ** -0.5
ROPE_THETA = 10000.0
M_HEADS = 4
M_HD = 128
M_W = M_HEADS * M_HD
M_CHUNK = 64
CONV_W = 4
PEER_HEADS = 8
N_KEYS = 128
PEER_TOPK = 16
PEER_QDIM = 256
PEER_BLOCK = 128
IN_SPLITS = (NSA_QW, 6 * NSA_KVW, 3 * NSA_HEADS, 2 * M_W, M_W, M_W, 2 * M_HEADS)
LN_EPS = 1e-5
ALPHA = (2 * DEPTH) ** 0.25

VMEM_LIMIT_BYTES = 56 * 1024 * 1024


def _mm_kernel(x_ref, w_ref, o_ref):
    o_ref[...] = jnp.dot(x_ref[...].astype(jnp.bfloat16), w_ref[...], preferred_element_type=jnp.float32)


def pallas_matmul(x, w, tm=512):
    M, K = x.shape
    N = w.shape[1]
    tm = min(tm, M)
    assert M % tm == 0
    return pl.pallas_call(
        _mm_kernel,
        out_shape=jax.ShapeDtypeStruct((M, N), jnp.float32),
        grid=(M // tm,),
        in_specs=[pl.BlockSpec((tm, K), lambda i: (i, 0)), pl.BlockSpec((K, N), lambda i: (0, 0))],
        out_specs=pl.BlockSpec((tm, N), lambda i: (i, 0)),
        compiler_params=pltpu.CompilerParams(dimension_semantics=("arbitrary",),
                                             vmem_limit_bytes=VMEM_LIMIT_BYTES),
        name="proj_matmul",
    )(x, w.astype(jnp.bfloat16))


def mm3(x, w):
    lead = x.shape[:-1]
    return pallas_matmul(x.reshape(-1, x.shape[-1]), w).reshape(*lead, w.shape[1])


def layer_norm(x, g, b):
    mu = x.mean(-1, keepdims=True)
    var = jnp.square(x - mu).mean(-1, keepdims=True)
    return (x - mu) * lax.rsqrt(var + LN_EPS) * g + b


def rope(x, pos):
    half = x.shape[-1] // 2
    inv = ROPE_THETA ** (-jnp.arange(half, dtype=jnp.float32) / half)
    ang = pos.astype(jnp.float32)[:, None] * inv[None, :]
    cos = jnp.cos(ang)[:, None, :]
    sin = jnp.sin(ang)[:, None, :]
    x1, x2 = x[..., :half], x[..., half:]
    return jnp.concatenate([x1 * cos - x2 * sin, x2 * cos + x1 * sin], axis=-1)


def split_in_proj(x, w_in):
    z = mm3(x, w_in)
    cuts = [int(c) for c in np.cumsum(IN_SPLITS)[:-1]]
    return jnp.split(z, cuts, axis=-1)


def nsa_project(zq, zkv, zg, pos):
    B, T, _ = zq.shape
    q = rope(zq.reshape(B, T, NSA_HEADS, NSA_HD), pos)
    kv = zkv.reshape(B, T, 6, NSA_KV_HEADS, NSA_HD)
    rows = (rope(kv[:, :, 0], pos), kv[:, :, 1], rope(kv[:, :, 2], pos), kv[:, :, 3],
            rope(kv[:, :, 4], pos), kv[:, :, 5])
    gates = jax.nn.sigmoid(zg).reshape(B, T, NSA_HEADS, 3)
    return q, rows, gates


def chunk_projection(chunks, w1):
    assert CMP_BLOCK == 2 * CMP_STRIDE
    w1r = w1.reshape(2, CMP_STRIDE, NSA_HD, w1.shape[-1])
    wbig = jnp.einsum('hpdf,kn->pkdnhf', w1r, jnp.eye(NSA_KV_HEADS, dtype=w1.dtype))
    return pallas_matmul(chunks, wbig.reshape(CMP_STRIDE * NSA_KVW, 2 * NSA_KV_HEADS * w1.shape[-1]))


def compress_from_projection(proj, pe, w1, b1, w2):
    B, nch, _ = proj.shape
    proj = proj.reshape(B, nch, NSA_KV_HEADS, 2, w1.shape[-1])
    bias = jnp.dot(pe.reshape(-1), w1, precision=lax.Precision.HIGHEST) + b1
    hid = jax.nn.gelu(proj[:, :-1, :, 0] + proj[:, 1:, :, 1] + bias, approximate=False)
    return hid @ w2


def cmp_attend(q, qpos, kc, vc):
    B, T = q.shape[:2]
    qg = q.reshape(B, T, NSA_KV_HEADS, NSA_GROUP, NSA_HD)
    s = jnp.einsum('btngd,bcnd->btngc', qg, kc) * ATTN_SCALE
    nblk = kc.shape[1]
    blk_end = jnp.arange(nblk) * CMP_STRIDE + CMP_BLOCK - 1
    valid = (blk_end[None, :] <= qpos[:, None])[None, :, None, None, :]
    p = jax.nn.softmax(jnp.where(valid, s, -1e30), axis=-1) * valid
    o = jnp.einsum('btngc,bcnd->btngd', p, vc)
    return o.reshape(B, T, NSA_HEADS, NSA_HD), p


def select_blocks(p, qpos, n_sel):
    imp = p.sum(axis=3)
    R = SEL_BLOCK // CMP_STRIDE
    r = CMP_BLOCK // CMP_STRIDE
    nb = imp.shape[-1]
    right = n_sel * R + R - 1 - nb
    padded = jnp.pad(imp, ((0, 0), (0, 0), (0, 0), (r - 1, right)))
    score = padded[..., 0:(n_sel - 1) * R + 1:R]
    for o in range(1, R + r - 1):
        score = score + padded[..., o:o + (n_sel - 1) * R + 1:R]
    j = jnp.arange(n_sel)[None, :]
    cur = (qpos // SEL_BLOCK)[:, None]
    valid = (j * SEL_BLOCK <= qpos[:, None])[None, :, None, :]
    forced = ((j == 0) | (j == cur) | (j == cur - 1))[None, :, None, :]
    score = jnp.where(forced, jnp.inf, jnp.where(valid, score, -jnp.inf))
    idx = j[0]
    before = (score[..., None, :] > score[..., :, None]) | ((score[..., None, :] == score[..., :, None])
                                                          & (idx[None, :] < idx[:, None]))
    rank = before.sum(-1)
    hit = rank[..., None, :] == jnp.arange(min(SEL_TOP, n_sel))[:, None]
    return (hit * idx).sum(-1)


def to_blocks(rows, n_sel):
    B, L, KV, hd = rows.shape
    rows = jnp.pad(rows, ((0, 0), (0, n_sel * SEL_BLOCK - L), (0, 0), (0, 0)))
    return rows.reshape(B, n_sel, SEL_BLOCK, KV, hd).transpose(0, 3, 1, 2, 4)


def take_rows(table, idx):
    return table[idx]


def sel_attend(q, qpos, sel, kb, vb):
    B, Tq = q.shape[:2]
    k = sel.shape[-1]
    sel_t = sel.transpose(0, 2, 1, 3)
    gather = jax.vmap(jax.vmap(take_rows))
    kg = gather(kb, sel_t).reshape(B, NSA_KV_HEADS, Tq, k * SEL_BLOCK, NSA_HD)
    vg = gather(vb, sel_t).reshape(B, NSA_KV_HEADS, Tq, k * SEL_BLOCK, NSA_HD)
    kpos = (sel_t[..., None] * SEL_BLOCK + jnp.arange(SEL_BLOCK)).reshape(B, NSA_KV_HEADS, Tq, k * SEL_BLOCK)
    qg = q.reshape(B, Tq, NSA_KV_HEADS, NSA_GROUP, NSA_HD).transpose(0, 2, 1, 3, 4)
    s = jnp.einsum('bntgd,bntsd->bntgs', qg, kg) * ATTN_SCALE
    mask = kpos[:, :, :, None, :] <= qpos[None, None, :, None, None]
    pr = jax.nn.softmax(jnp.where(mask, s, -jnp.inf), axis=-1)
    o = jnp.einsum('bntgs,bntsd->bntgd', pr, vg)
    return o.transpose(0, 2, 1, 3, 4).reshape(B, Tq, NSA_HEADS, NSA_HD)


def win_attend(q, qpos, k, v, kpos):
    B, Tq = q.shape[:2]
    qg = q.reshape(B, Tq, NSA_KV_HEADS, NSA_GROUP, NSA_HD)
    s = jnp.einsum('btngd,bsnd->btngs', qg, k) * ATTN_SCALE
    diff = qpos[:, None] - kpos[None, :]
    mask = ((diff >= 0) & (diff < WINDOW) & (kpos[None, :] >= 0))[None, :, None, None, :]
    pr = jax.nn.softmax(jnp.where(mask, s, -jnp.inf), axis=-1)
    o = jnp.einsum('btngs,bsnd->btngd', pr, v)
    return o.reshape(B, Tq, NSA_HEADS, NSA_HD)


def nsa_combine(gates, o_cmp, o_sel, o_win):
    B, T = gates.shape[:2]
    o = gates[..., 0:1] * o_cmp + gates[..., 1:2] * o_sel + gates[..., 2:3] * o_win
    return o.reshape(B, T, NSA_QW)


NSA_TQ = 128
NSA_CK = 512
MASKED = -1e30


def _softmax_rows(s):
    m = jnp.max(s, axis=-1, keepdims=True)
    e = jnp.exp(s - m)
    return e / jnp.sum(e, axis=-1, keepdims=True)


def _nsa_prompt_kernel(q_ref, kc_ref, vc_ref, ks_ref, vs_ref, kw_ref, vw_ref, zg_ref, msel_ref, exp_ref, o_ref):
    f32, bf16 = jnp.float32, jnp.bfloat16
    tq = NSA_TQ
    q0 = pl.program_id(2) * tq
    qb = q_ref[0]
    qs = jnp.concatenate([qb[:, g * NSA_HD:(g + 1) * NSA_HD] for g in range(NSA_GROUP)], axis=0)
    tpos = q0 + lax.broadcasted_iota(jnp.int32, (tq, 1), 0)

    def per_head(a):
        return jnp.concatenate([a] * NSA_GROUP, axis=0)

    s = lax.dot_general(qs, kc_ref[0, 0], _NT, preferred_element_type=f32)
    cblk = lax.broadcasted_iota(jnp.int32, (tq, 128), 1)
    cvalid = cblk * CMP_STRIDE + (CMP_BLOCK - 1) <= tpos
    s = s + per_head(jnp.where(cvalid, 0.0, MASKED))
    e = jnp.exp(s - jnp.max(s, axis=-1, keepdims=True)) * per_head(jnp.where(cvalid, 1.0, 0.0))
    l = jnp.sum(e, axis=-1, keepdims=True)
    p = e / jnp.where(l > 0.0, l, 1.0)
    o_cmp = jnp.dot(p.astype(bf16), vc_ref[0, 0], preferred_element_type=f32)

    imp = p[0:tq]
    for g in range(1, NSA_GROUP):
        imp = imp + p[g * tq:(g + 1) * tq]
    hi = imp.astype(bf16)
    r1 = imp - hi.astype(f32)
    mid = r1.astype(bf16)
    lo = (r1 - mid.astype(f32)).astype(bf16)
    msel = msel_ref[...]
    score = (jnp.dot(hi, msel, preferred_element_type=f32) + jnp.dot(mid, msel, preferred_element_type=f32)
             + jnp.dot(lo, msel, preferred_element_type=f32))
    n_sel = score.shape[1]
    j = lax.broadcasted_iota(jnp.int32, (tq, n_sel), 1)
    cur = tpos // SEL_BLOCK
    forced = (j == 0) | (j == cur) | (j == cur - 1)
    score = jnp.where(forced, jnp.inf, jnp.where(j * SEL_BLOCK <= tpos, score, -jnp.inf))
    rank = jnp.zeros((tq, n_sel), f32)
    for jp in range(n_sel):
        col = score[:, jp:jp + 1]
        before = (col > score) | ((col == score) & (j > jp))
        rank = rank + jnp.where(before, 1.0, 0.0)
    sel01 = jnp.where(rank < SEL_TOP, 1.0, 0.0).astype(bf16)

    ck = NSA_CK
    rows = NSA_GROUP * tq

    def sel_chunk(c, carry):
        m, l, acc = carry
        k0 = pl.multiple_of(c * ck, ck)
        s = lax.dot_general(qs, ks_ref[0, 0, pl.ds(k0, ck), :], _NT, preferred_element_type=f32)
        chosen = jnp.dot(sel01, exp_ref[c], preferred_element_type=f32)
        kpos = k0 + lax.broadcasted_iota(jnp.int32, (tq, ck), 1)
        ok = (chosen > 0.5) & (kpos <= tpos)
        s = s + per_head(jnp.where(ok, 0.0, MASKED))
        m_new = jnp.maximum(m, jnp.max(s, axis=-1, keepdims=True))
        a = jnp.exp(m - m_new)
        pr = jnp.exp(s - m_new)
        l = a * l + jnp.sum(pr, axis=-1, keepdims=True)
        acc = a * acc + jnp.dot(pr.astype(bf16), vs_ref[0, 0, pl.ds(k0, ck), :], preferred_element_type=f32)
        return m_new, l, acc

    init = (jnp.full((rows, 1), MASKED, f32), jnp.zeros((rows, 1), f32), jnp.zeros((rows, NSA_HD), f32))
    n_chunks = (q0 + tq + ck - 1) // ck
    _, l_sel, acc_sel = lax.fori_loop(0, n_chunks, sel_chunk, init)
    o_sel = acc_sel / l_sel

    w0 = pl.multiple_of(jnp.maximum(q0 - WINDOW, 0), tq)
    wl = WINDOW + tq
    s = lax.dot_general(qs, kw_ref[0, 0, pl.ds(w0, wl), :], _NT, preferred_element_type=f32)
    diff = tpos - (w0 + lax.broadcasted_iota(jnp.int32, (tq, wl), 1))
    s = s + per_head(jnp.where((diff >= 0) & (diff < WINDOW), 0.0, MASKED))
    o_win = jnp.dot(_softmax_rows(s).astype(bf16), vw_ref[0, 0, pl.ds(w0, wl), :], preferred_element_type=f32)

    gates = jax.nn.sigmoid(zg_ref[0, 0])
    for g in range(NSA_GROUP):
        r = slice(g * tq, (g + 1) * tq)
        o_ref[0, :, g * NSA_HD:(g + 1) * NSA_HD] = (gates[:, 3 * g:3 * g + 1] * o_cmp[r]
                                                    + gates[:, 3 * g + 1:3 * g + 2] * o_sel[r]
                                                    + gates[:, 3 * g + 2:3 * g + 3] * o_win[r])


def nsa_prompt_attention(q, kc, vc, k_slc, v_slc, k_win, v_win, zg):
    B, S = q.shape[:2]
    bf16 = jnp.bfloat16
    assert S % NSA_CK == 0 and S % NSA_TQ == 0 and WINDOW % NSA_TQ == 0 and WINDOW + NSA_TQ <= S
    n_sel = S // SEL_BLOCK
    nb = kc.shape[1]
    assert nb <= 128

    def kv_major(a):
        return a.transpose(0, 2, 1, 3).astype(bf16)

    def pad_blocks(a):
        return jnp.pad(kv_major(a), ((0, 0), (0, 0), (0, 128 - nb), (0, 0)))

    qs = (q * ATTN_SCALE).reshape(B, S, NSA_QW).astype(bf16)
    zg4 = zg.reshape(B, S, NSA_KV_HEADS, 3 * NSA_GROUP).transpose(0, 2, 1, 3)
    c = np.arange(128)[:, None]
    jj = np.arange(n_sel)[None, :]
    ratio = SEL_BLOCK // CMP_STRIDE
    msel = ((c >= jj * ratio - (CMP_BLOCK // CMP_STRIDE - 1)) & (c <= jj * ratio + ratio - 1) & (c < nb))
    expand = (np.arange(S)[None, :] // SEL_BLOCK == np.arange(n_sel)[:, None])
    expand = expand.reshape(n_sel, S // NSA_CK, NSA_CK).transpose(1, 0, 2)
    row_spec = pl.BlockSpec((1, 1, S, NSA_HD), lambda b, n, i: (b, n, 0, 0))
    blk_spec = pl.BlockSpec((1, 1, 128, NSA_HD), lambda b, n, i: (b, n, 0, 0))
    return pl.pallas_call(
        _nsa_prompt_kernel,
        out_shape=jax.ShapeDtypeStruct((B, S, NSA_QW), jnp.float32),
        grid=(B, NSA_KV_HEADS, S // NSA_TQ),
        in_specs=[pl.BlockSpec((1, NSA_TQ, NSA_GROUP * NSA_HD), lambda b, n, i: (b, i, n)),
                  blk_spec, blk_spec, row_spec, row_spec, row_spec, row_spec,
                  pl.BlockSpec((1, 1, NSA_TQ, 3 * NSA_GROUP), lambda b, n, i: (b, n, i, 0)),
                  pl.BlockSpec((128, n_sel), lambda b, n, i: (0, 0)),
                  pl.BlockSpec((S // NSA_CK, n_sel, NSA_CK), lambda b, n, i: (0, 0, 0))],
        out_specs=pl.BlockSpec((1, NSA_TQ, NSA_GROUP * NSA_HD), lambda b, n, i: (b, i, n)),
        compiler_params=pltpu.CompilerParams(dimension_semantics=("arbitrary", "arbitrary", "arbitrary"),
                                             vmem_limit_bytes=VMEM_LIMIT_BYTES),
        name="nsa_prompt_attention",
    )(qs, pad_blocks(kc), pad_blocks(vc), kv_major(k_slc), kv_major(v_slc), kv_major(k_win), kv_major(v_win),
      zg4, jnp.asarray(msel, bf16), jnp.asarray(expand, bf16))


def mlstm_chunk(carry, inp):
    C, n, m = carry
    q, k, v, ig, lf = inp
    L = q.shape[2]
    b = jnp.cumsum(lf, axis=-1)
    causal = jnp.tril(jnp.ones((L, L), dtype=bool))
    dmat = jnp.where(causal, b[..., :, None] - b[..., None, :] + ig[..., None, :], -jnp.inf)
    inter = b + m[..., None]
    m_t = jnp.maximum(inter, dmat.max(axis=-1))
    w_intra = jnp.exp(dmat - m_t[..., None])
    w_inter = jnp.exp(inter - m_t)
    s = jnp.einsum('bhtd,bhsd->bhts', q, k) * w_intra
    num = jnp.einsum('bhts,bhsv->bhtv', s, v) + w_inter[..., None] * jnp.einsum('bhtd,bhdv->bhtv', q, C)
    den = s.sum(-1) + w_inter * jnp.einsum('bhtd,bhd->bht', q, n)
    h = num / jnp.maximum(jnp.abs(den), jnp.exp(-m_t))[..., None]
    m_new = m_t[..., -1]
    w_s = jnp.exp(b[..., -1:] - b + ig - m_new[..., None])
    w_p = jnp.exp(b[..., -1] + m - m_new)
    C_new = w_p[..., None, None] * C + jnp.einsum('bhs,bhsd,bhsv->bhdv', w_s, k, v)
    n_new = w_p[..., None] * n + jnp.einsum('bhs,bhsd->bhd', w_s, k)
    return (C_new, n_new, m_new), h


def mlstm_mix(zqk, zv, zo, zif, buf0, C0, n0, m0, conv_w, conv_b, b_if, chunk):
    B, T, _ = zqk.shape
    full = jnp.concatenate([buf0, zqk], axis=1)
    conv = conv_b
    for j in range(CONV_W):
        conv = conv + full[:, j:j + T] * conv_w[j]
    qk = jax.nn.silu(conv)

    def heads(a):
        return a.reshape(B, T, M_HEADS, M_HD).transpose(0, 2, 1, 3)

    q = heads(qk[..., :M_W])
    k = heads(qk[..., M_W:]) * (M_HD ** -0.5)
    v = heads(zv)
    gif = zif + b_if
    ig = gif[..., :M_HEADS].transpose(0, 2, 1)
    lf = jax.nn.log_sigmoid(gif[..., M_HEADS:]).transpose(0, 2, 1)
    nc = T // chunk

    def to_chunks(a):
        return jnp.moveaxis(a.reshape(B, M_HEADS, nc, chunk, *a.shape[3:]), 2, 0)

    (C, n, m), h = lax.scan(mlstm_chunk, (C0, n0, m0),
                            (to_chunks(q), to_chunks(k), to_chunks(v), to_chunks(ig), to_chunks(lf)))
    h = jnp.moveaxis(h, 0, 2).reshape(B, M_HEADS, T, M_HD).transpose(0, 2, 1, 3).reshape(B, T, M_W)
    out = jax.nn.sigmoid(zo) * h
    return out, (C, n, m, full[:, T:])


PEER_COMBOS = 2 * PEER_HEADS
PEER_KEY_ROWS = 8
PEER_TILE = PEER_KEY_ROWS * N_KEYS
PEER_TS_ROWS = 24
LANES = 128
_NT = (((1,), (1,)), ((), ()))


def _peer_topk_kernel(q_ref, keys_ref, s_ref, e0_ref, e1_ref, tau_ref, ts_ref):
    c = pl.program_id(1)
    tt = q_ref.shape[0]
    s = lax.dot_general(keys_ref[0], q_ref[...].astype(jnp.bfloat16), _NT,
                        preferred_element_type=jnp.float32)
    s_ref[c] = s
    key_id = lax.broadcasted_iota(jnp.int32, s.shape, 0)
    work = s
    rows = []
    for _ in range(PEER_TOPK + 1):
        m = jnp.max(work, axis=0, keepdims=True)
        first = jnp.min(jnp.where(work == m, key_id, N_KEYS), axis=0, keepdims=True)
        work = jnp.where(key_id == first, -jnp.inf, work)
        rows.append(m)
    rows.append(jnp.full((PEER_TS_ROWS - PEER_TOPK - 1, tt), -jnp.inf, jnp.float32))
    ts_ref[c] = jnp.concatenate(rows, axis=0)

    @pl.when(c == PEER_COMBOS - 1)
    def _():
        for h in range(PEER_HEADS):
            t0 = ts_ref[2 * h]
            t1 = ts_ref[2 * h + 1]
            pieces = [t0[0:1] + t1] + [t0[a:a + 1] + t1[0:8] for a in range(1, 8)] + [t0[8:24] + t1[0:1]]
            cand = jnp.concatenate(pieces, axis=0)
            top = t0[0:1] + t1[0:1]
            v16 = top
            v17 = top
            z = jnp.zeros_like(top)
            seen = jnp.zeros_like(top)
            for _ in range(PEER_TOPK + 1):
                m = jnp.max(cand, axis=0, keepdims=True)
                eq = cand == m
                cnt = jnp.sum(jnp.where(eq, 1.0, 0.0), axis=0, keepdims=True)
                active = seen < PEER_TOPK
                take = jnp.minimum(cnt, PEER_TOPK - seen)
                v16 = jnp.where(active, m, v16)
                v17 = jnp.where(seen < PEER_TOPK + 1, m, v17)
                z = z + jnp.where(active, take * jnp.exp(m - top), 0.0)
                seen = seen + cnt
                cand = jnp.where(eq, -jnp.inf, cand)
            tau_ref[h:h + 1, :] = 0.5 * v16 + 0.5 * v17
            e0_ref[h] = jnp.exp(s_ref[2 * h] - t0[0:1]) / z
            e1_ref[h] = jnp.exp(s_ref[2 * h + 1] - t1[0:1])


def peer_scores(q, sub_keys, tt):
    n = q.shape[0]
    assert n % tt == 0
    keys = sub_keys.reshape(PEER_COMBOS, N_KEYS, PEER_QDIM // 2).astype(jnp.bfloat16)
    f32 = jnp.float32
    return pl.pallas_call(
        _peer_topk_kernel,
        out_shape=(jax.ShapeDtypeStruct((PEER_COMBOS, N_KEYS, n), f32),
                   jax.ShapeDtypeStruct((PEER_HEADS, N_KEYS, n), f32),
                   jax.ShapeDtypeStruct((PEER_HEADS, N_KEYS, n), f32),
                   jax.ShapeDtypeStruct((PEER_HEADS, n), f32)),
        grid=(n // tt, PEER_COMBOS),
        in_specs=[pl.BlockSpec((tt, PEER_QDIM // 2), lambda i, c: (i, c)),
                  pl.BlockSpec((1, N_KEYS, PEER_QDIM // 2), lambda i, c: (c, 0, 0))],
        out_specs=(pl.BlockSpec((PEER_COMBOS, N_KEYS, tt), lambda i, c: (0, 0, i)),
                   pl.BlockSpec((PEER_HEADS, N_KEYS, tt), lambda i, c: (0, 0, i)),
                   pl.BlockSpec((PEER_HEADS, N_KEYS, tt), lambda i, c: (0, 0, i)),
                   pl.BlockSpec((PEER_HEADS, tt), lambda i, c: (0, i))),
        scratch_shapes=[pltpu.VMEM((PEER_COMBOS, PEER_TS_ROWS, tt), f32)],
        compiler_params=pltpu.CompilerParams(dimension_semantics=("arbitrary", "arbitrary"),
                                             vmem_limit_bytes=VMEM_LIMIT_BYTES),
        name="peer_topk",
    )(q, keys)


def _peer_dense_kernel(xb_ref, h_ref, u_ref, vt_ref, s0_ref, ez_ref, s_ref, e1_ref, tau_ref, g_ref, b_ref,
                       o_ref, acc_ref, a_ref, w_ref):
    e = pl.program_id(1)
    tt = xb_ref.shape[0]

    @pl.when(e == 0)
    def _():
        acc_ref[...] = jnp.zeros_like(acc_ref)

    a_ref[...] = lax.dot_general(u_ref[...], xb_ref[...], _NT, preferred_element_type=jnp.float32)
    for t in range(tt // LANES):
        tok = slice(t * LANES, (t + 1) * LANES)
        for r in range(PEER_KEY_ROWS):
            rows = slice(r * N_KEYS, (r + 1) * N_KEYS)
            gate = jnp.zeros((N_KEYS, LANES), jnp.float32)
            for h in range(PEER_HEADS):
                need = tau_ref[h:h + 1, tok] - s0_ref[2 * h, r:r + 1, tok]
                picked = jnp.where(s_ref[2 * h + 1, :, tok] >= need, e1_ref[h, :, tok], 0.0)
                gate = gate + picked * ez_ref[h, r:r + 1, tok]
            ar = a_ref[rows, tok]
            act = 0.5 * ar * (1.0 + lax.erf(ar * (2.0 ** -0.5)))
            w_ref[rows, tok] = (gate * act).astype(jnp.bfloat16)
    acc_ref[...] += jnp.dot(vt_ref[...], w_ref[...], preferred_element_type=jnp.float32)

    @pl.when(e == pl.num_programs(1) - 1)
    def _():
        r = ALPHA * h_ref[...] + acc_ref[...].T
        mu = jnp.mean(r, axis=-1, keepdims=True)
        d = r - mu
        var = jnp.mean(d * d, axis=-1, keepdims=True)
        o_ref[...] = d * lax.rsqrt(var + LN_EPS) * g_ref[...] + b_ref[...]


def peer_tail(h, q, sub_keys, u_bf, vt_bf, ln_g, ln_b, tt):
    n, d = h.shape
    s, e0z, e1, tau = peer_scores(q, sub_keys, tt)
    n_exp = u_bf.shape[0]
    return pl.pallas_call(
        _peer_dense_kernel,
        out_shape=jax.ShapeDtypeStruct((n, d), jnp.float32),
        grid=(n // tt, n_exp // PEER_TILE),
        in_specs=[pl.BlockSpec((tt, d), lambda i, e: (i, 0)),
                  pl.BlockSpec((tt, d), lambda i, e: (i, 0)),
                  pl.BlockSpec((PEER_TILE, d), lambda i, e: (e, 0)),
                  pl.BlockSpec((d, PEER_TILE), lambda i, e: (0, e)),
                  pl.BlockSpec((PEER_COMBOS, PEER_KEY_ROWS, tt), lambda i, e: (0, e, i)),
                  pl.BlockSpec((PEER_HEADS, PEER_KEY_ROWS, tt), lambda i, e: (0, e, i)),
                  pl.BlockSpec((PEER_COMBOS, N_KEYS, tt), lambda i, e: (0, 0, i)),
                  pl.BlockSpec((PEER_HEADS, N_KEYS, tt), lambda i, e: (0, 0, i)),
                  pl.BlockSpec((PEER_HEADS, tt), lambda i, e: (0, i)),
                  pl.BlockSpec((1, d), lambda i, e: (0, 0)),
                  pl.BlockSpec((1, d), lambda i, e: (0, 0))],
        out_specs=pl.BlockSpec((tt, d), lambda i, e: (i, 0)),
        scratch_shapes=[pltpu.VMEM((d, tt), jnp.float32), pltpu.VMEM((PEER_TILE, tt), jnp.float32),
                        pltpu.VMEM((PEER_TILE, tt), jnp.bfloat16)],
        compiler_params=pltpu.CompilerParams(dimension_semantics=("arbitrary", "arbitrary"),
                                             vmem_limit_bytes=VMEM_LIMIT_BYTES),
        name="peer_dense",
    )(h.astype(jnp.bfloat16), h, u_bf, vt_bf, s, e0z, s, e1, tau, ln_g[None], ln_b[None])


def block_tail(x, mix, w_out, ln_g, ln_b, w_pq, sub_keys, u_bf, vt_bf, tt):
    lead = x.shape[:-1]
    h = layer_norm(ALPHA * x + mm3(mix, w_out), ln_g[0], ln_b[0]).reshape(-1, D_MODEL)
    q = pallas_matmul(h, w_pq)
    return peer_tail(h, q, sub_keys, u_bf, vt_bf, ln_g[1], ln_b[1], tt).reshape(*lead, D_MODEL)


def prompt_mix(x, w_in, pe, w1, b1, w2, conv_w, conv_b, b_if):
    B, S, _ = x.shape
    pos = jnp.arange(S)
    zq, zkv, zg, zqk, zv, zo, zif = split_in_proj(x, w_in)
    q, (k_cmp, v_cmp, k_slc, v_slc, k_win, v_win), gates = nsa_project(zq, zkv, zg, pos)
    chunk_w = CMP_STRIDE * NSA_KVW

    def compressed(rows, c):
        proj = chunk_projection(rows.reshape(B * (S // CMP_STRIDE), chunk_w), w1[c])
        return compress_from_projection(proj.reshape(B, S // CMP_STRIDE, -1), pe[c], w1[c], b1[c], w2[c])

    kc = compressed(k_cmp, 0)
    vc = compressed(v_cmp, 1)
    o_nsa = nsa_prompt_attention(q, kc, vc, k_slc, v_slc, k_win, v_win, zg)
    C0 = jnp.zeros((B, M_HEADS, M_HD, M_HD), jnp.float32)
    n0 = jnp.zeros((B, M_HEADS, M_HD), jnp.float32)
    m0 = jnp.zeros((B, M_HEADS), jnp.float32)
    buf0 = jnp.zeros((B, CONV_W - 1, 2 * M_W), x.dtype)
    o_m, (C, n, m, buf) = mlstm_mix(zqk, zv, zo, zif, buf0, C0, n0, m0, conv_w, conv_b, b_if, M_CHUNK)
    wl = min(WINDOW, S)
    mix = jnp.concatenate([o_nsa, o_m], axis=-1)
    return mix, (k_cmp, v_cmp, k_slc, v_slc, k_win[:, S - wl:], v_win[:, S - wl:], C, n, m, buf)


def sample_mix(x, kc_pool, vc_pool, ks_pool, vs_pool, kw_buf, vw_buf, C0, n0, m0, buf0, page_table,
               w_in, pe, w1, b1, w2, conv_w, conv_b, b_if):
    B, T, _ = x.shape
    past = page_table.shape[1] * PAGE_SIZE
    pos = past + jnp.arange(T)
    zq, zkv, zg, zqk, zv, zo, zif = split_in_proj(x, w_in)
    q, (k_cmp, v_cmp, k_slc, v_slc, k_win, v_win), gates = nsa_project(zq, zkv, zg, pos)

    def with_past(pool, new):
        old = pool[page_table].reshape(B, past, NSA_KV_HEADS, NSA_HD)
        return jnp.concatenate([old, new], axis=1)

    chunks_per_page = PAGE_SIZE // CMP_STRIDE
    assert (past + T) // CMP_STRIDE == past // CMP_STRIDE

    def compressed(pool, c):
        proj = chunk_projection(pool.reshape(pool.shape[0] * chunks_per_page, CMP_STRIDE * NSA_KVW), w1[c])
        proj = proj.reshape(pool.shape[0], chunks_per_page, -1)[page_table]
        return compress_from_projection(proj.reshape(B, past // CMP_STRIDE, -1), pe[c], w1[c], b1[c], w2[c])

    o_cmp, p = cmp_attend(q, pos, compressed(kc_pool, 0), compressed(vc_pool, 1))
    n_sel = -(-(past + T) // SEL_BLOCK)
    sel = select_blocks(p, pos, n_sel)
    kb = to_blocks(with_past(ks_pool, k_slc), n_sel)
    vb = to_blocks(with_past(vs_pool, v_slc), n_sel)
    wb = kw_buf.shape[1]
    kw = jnp.concatenate([kw_buf, k_win], axis=1)
    vw = jnp.concatenate([vw_buf, v_win], axis=1)
    kpos = past - wb + jnp.arange(wb + T)
    o_sel = sel_attend(q, pos, sel, kb, vb)
    o_win = win_attend(q, pos, kw, vw, kpos)
    o_nsa = nsa_combine(gates, o_cmp, o_sel, o_win)
    o_m, (C, n, m, buf) = mlstm_mix(zqk, zv, zo, zif, buf0, C0, n0, m0, conv_w, conv_b, b_if, T)
    mix = jnp.concatenate([o_nsa, o_m], axis=-1)
    return mix, (k_cmp, v_cmp, k_slc, v_slc, kw[:, T:], vw[:, T:], C, n, m, buf)


def kernel(x_prompt, x_sample, cache_k_cmp, cache_v_cmp, cache_k_slc, cache_v_slc, cache_k_win, cache_v_win,
           state_C, state_n, state_m, state_conv, page_table, w_in, w_out, w_phi1, b_phi1, w_phi2, pe_cmp,
           conv_w, conv_b, b_if, ln_g, ln_b, w_pq, sub_keys, u_tab, v_tab):
    l = 0
    mix_p, st_p = prompt_mix(x_prompt, w_in[l], pe_cmp[l], w_phi1[l], b_phi1[l], w_phi2[l],
                             conv_w[l], conv_b[l], b_if[l])
    mix_s, st_s = sample_mix(x_sample, cache_k_cmp[l], cache_v_cmp[l], cache_k_slc[l], cache_v_slc[l],
                             cache_k_win[l], cache_v_win[l], state_C[l], state_n[l], state_m[l],
                             state_conv[l], page_table, w_in[l], pe_cmp[l], w_phi1[l], b_phi1[l],
                             w_phi2[l], conv_w[l], conv_b[l], b_if[l])
    u_bf = u_tab[l].astype(jnp.bfloat16)
    vt_bf = v_tab[l].astype(jnp.bfloat16).T
    xp = block_tail(x_prompt, mix_p, w_out[l], ln_g[l], ln_b[l], w_pq[l], sub_keys[l], u_bf, vt_bf, 512)
    xs = block_tail(x_sample, mix_s, w_out[l], ln_g[l], ln_b[l], w_pq[l], sub_keys[l], u_bf, vt_bf, 128)
    return (xp, xs) + tuple(a[None] for a in st_p) + tuple(a[None] for a in st_s)
```

```python
import jax
import jax.numpy as jnp
import numpy as np
from jax import lax
from jax.experimental import pallas as pl
from jax.experimental.pallas import tpu as pltpu

D_MODEL = 1024
DEPTH = 1
PAGE_SIZE = 128
NSA_HEADS = 8
NSA_KV_HEADS = 2
NSA_GROUP = NSA_HEADS // NSA_KV_HEADS
NSA_HD = 64
NSA_QW = NSA_HEADS * NSA_HD
NSA_KVW = NSA_KV_HEADS * NSA_HD
CMP_BLOCK = 32
CMP_STRIDE = 16
SEL_BLOCK = 64
SEL_TOP = 16
WINDOW = 512
Q_BLOCK = 64
ATTN_SCALE = NSA_HD ** -0.5
ROPE_THETA = 10000.0
M_HEADS = 4
M_HD = 128
M_W = M_HEADS * M_HD
M_CHUNK = 64
CONV_W = 4
PEER_HEADS = 8
N_KEYS = 128
PEER_TOPK = 16
PEER_QDIM = 256
PEER_BLOCK = 128
IN_SPLITS = (NSA_QW, 6 * NSA_KVW, 3 * NSA_HEADS, 2 * M_W, M_W, M_W, 2 * M_HEADS)
LN_EPS = 1e-5
ALPHA = (2 * DEPTH) ** 0.25

VMEM_LIMIT_BYTES = 56 * 1024 * 1024


def _mm_kernel(x_ref, w_ref, o_ref):
    o_ref[...] = jnp.dot(x_ref[...].astype(jnp.bfloat16), w_ref[...], preferred_element_type=jnp.float32)


def pallas_matmul(x, w, tm=512):
    M, K = x.shape
    N = w.shape[1]
    tm = min(tm, M)
    assert M % tm == 0
    return pl.pallas_call(
        _mm_kernel,
        out_shape=jax.ShapeDtypeStruct((M, N), jnp.float32),
        grid=(M // tm,),
        in_specs=[pl.BlockSpec((tm, K), lambda i: (i, 0)), pl.BlockSpec((K, N), lambda i: (0, 0))],
        out_specs=pl.BlockSpec((tm, N), lambda i: (i, 0)),
        compiler_params=pltpu.CompilerParams(dimension_semantics=("arbitrary",),
                                             vmem_limit_bytes=VMEM_LIMIT_BYTES),
        name="proj_matmul",
    )(x, w.astype(jnp.bfloat16))


def mm3(x, w):
    lead = x.shape[:-1]
    return pallas_matmul(x.reshape(-1, x.shape[-1]), w).reshape(*lead, w.shape[1])


def layer_norm(x, g, b):
    mu = x.mean(-1, keepdims=True)
    var = jnp.square(x - mu).mean(-1, keepdims=True)
    return (x - mu) * lax.rsqrt(var + LN_EPS) * g + b


def rope(x, pos):
    half = x.shape[-1] // 2
    inv = ROPE_THETA ** (-jnp.arange(half, dtype=jnp.float32) / half)
    ang = pos.astype(jnp.float32)[:, None] * inv[None, :]
    cos = jnp.cos(ang)[:, None, :]
    sin = jnp.sin(ang)[:, None, :]
    x1, x2 = x[..., :half], x[..., half:]
    return jnp.concatenate([x1 * cos - x2 * sin, x2 * cos + x1 * sin], axis=-1)


def split_in_proj(x, w_in):
    z = mm3(x, w_in)
    cuts = [int(c) for c in np.cumsum(IN_SPLITS)[:-1]]
    return jnp.split(z, cuts, axis=-1)


def nsa_project(zq, zkv, zg, pos):
    B, T, _ = zq.shape
    q = rope(zq.reshape(B, T, NSA_HEADS, NSA_HD), pos)
    kv = zkv.reshape(B, T, 6, NSA_KV_HEADS, NSA_HD)
    rows = (rope(kv[:, :, 0], pos), kv[:, :, 1], rope(kv[:, :, 2], pos), kv[:, :, 3],
            rope(kv[:, :, 4], pos), kv[:, :, 5])
    gates = jax.nn.sigmoid(zg).reshape(B, T, NSA_HEADS, 3)
    return q, rows, gates


def chunk_projection(chunks, w1):
    assert CMP_BLOCK == 2 * CMP_STRIDE
    w1r = w1.reshape(2, CMP_STRIDE, NSA_HD, w1.shape[-1])
    wbig = jnp.einsum('hpdf,kn->pkdnhf', w1r, jnp.eye(NSA_KV_HEADS, dtype=w1.dtype))
    return pallas_matmul(chunks, wbig.reshape(CMP_STRIDE * NSA_KVW, 2 * NSA_KV_HEADS * w1.shape[-1]))


def compress_from_projection(proj, pe, w1, b1, w2):
    f = w1.shape[-1]
    bias = jnp.dot(pe.reshape(-1), w1, precision=lax.Precision.HIGHEST) + b1
    heads = []
    for n in range(NSA_KV_HEADS):
        first = proj[:, :-1, 2 * n * f:(2 * n + 1) * f]
        second = proj[:, 1:, (2 * n + 1) * f:(2 * n + 2) * f]
        heads.append(jax.nn.gelu(first + second + bias, approximate=False) @ w2)
    return jnp.stack(heads, axis=2)


def cmp_attend(q, qpos, kc, vc):
    B, T = q.shape[:2]
    qg = q.reshape(B, T, NSA_KV_HEADS, NSA_GROUP, NSA_HD)
    s = jnp.einsum('btngd,bcnd->btngc', qg, kc) * ATTN_SCALE
    nblk = kc.shape[1]
    blk_end = jnp.arange(nblk) * CMP_STRIDE + CMP_BLOCK - 1
    valid = (blk_end[None, :] <= qpos[:, None])[None, :, None, None, :]
    p = jax.nn.softmax(jnp.where(valid, s, -1e30), axis=-1) * valid
    o = jnp.einsum('btngc,bcnd->btngd', p, vc)
    return o.reshape(B, T, NSA_HEADS, NSA_HD), p


def select_blocks(p, qpos, n_sel):
    imp = p.sum(axis=3)
    R = SEL_BLOCK // CMP_STRIDE
    r = CMP_BLOCK // CMP_STRIDE
    nb = imp.shape[-1]
    right = n_sel * R + R - 1 - nb
    padded = jnp.pad(imp, ((0, 0), (0, 0), (0, 0), (r - 1, right)))
    score = padded[..., 0:(n_sel - 1) * R + 1:R]
    for o in range(1, R + r - 1):
        score = score + padded[..., o:o + (n_sel - 1) * R + 1:R]
    j = jnp.arange(n_sel)[None, :]
    cur = (qpos // SEL_BLOCK)[:, None]
    valid = (j * SEL_BLOCK <= qpos[:, None])[None, :, None, :]
    forced = ((j == 0) | (j == cur) | (j == cur - 1))[None, :, None, :]
    score = jnp.where(forced, jnp.inf, jnp.where(valid, score, -jnp.inf))
    idx = j[0]
    before = (score[..., None, :] > score[..., :, None]) | ((score[..., None, :] == score[..., :, None])
                                                          & (idx[None, :] < idx[:, None]))
    rank = before.sum(-1)
    hit = rank[..., None, :] == jnp.arange(min(SEL_TOP, n_sel))[:, None]
    return (hit * idx).sum(-1)


def to_blocks(rows, n_sel):
    B, L, KV, hd = rows.shape
    rows = jnp.pad(rows, ((0, 0), (0, n_sel * SEL_BLOCK - L), (0, 0), (0, 0)))
    return rows.reshape(B, n_sel, SEL_BLOCK, KV, hd).transpose(0, 3, 1, 2, 4)


def take_rows(table, idx):
    return table[idx]


def sel_attend(q, qpos, sel, kb, vb):
    B, Tq = q.shape[:2]
    k = sel.shape[-1]
    sel_t = sel.transpose(0, 2, 1, 3)
    gather = jax.vmap(jax.vmap(take_rows))
    kg = gather(kb, sel_t).reshape(B, NSA_KV_HEADS, Tq, k * SEL_BLOCK, NSA_HD)
    vg = gather(vb, sel_t).reshape(B, NSA_KV_HEADS, Tq, k * SEL_BLOCK, NSA_HD)
    kpos = (sel_t[..., None] * SEL_BLOCK + jnp.arange(SEL_BLOCK)).reshape(B, NSA_KV_HEADS, Tq, k * SEL_BLOCK)
    qg = q.reshape(B, Tq, NSA_KV_HEADS, NSA_GROUP, NSA_HD).transpose(0, 2, 1, 3, 4)
    s = jnp.einsum('bntgd,bntsd->bntgs', qg, kg) * ATTN_SCALE
    mask = kpos[:, :, :, None, :] <= qpos[None, None, :, None, None]
    pr = jax.nn.softmax(jnp.where(mask, s, -jnp.inf), axis=-1)
    o = jnp.einsum('bntgs,bntsd->bntgd', pr, vg)
    return o.transpose(0, 2, 1, 3, 4).reshape(B, Tq, NSA_HEADS, NSA_HD)


def win_attend(q, qpos, k, v, kpos):
    B, Tq = q.shape[:2]
    qg = q.reshape(B, Tq, NSA_KV_HEADS, NSA_GROUP, NSA_HD)
    s = jnp.einsum('btngd,bsnd->btngs', qg, k) * ATTN_SCALE
    diff = qpos[:, None] - kpos[None, :]
    mask = ((diff >= 0) & (diff < WINDOW) & (kpos[None, :] >= 0))[None, :, None, None, :]
    pr = jax.nn.softmax(jnp.where(mask, s, -jnp.inf), axis=-1)
    o = jnp.einsum('btngs,bsnd->btngd', pr, v)
    return o.reshape(B, Tq, NSA_HEADS, NSA_HD)


def nsa_combine(gates, o_cmp, o_sel, o_win):
    B, T = gates.shape[:2]
    o = gates[..., 0:1] * o_cmp + gates[..., 1:2] * o_sel + gates[..., 2:3] * o_win
    return o.reshape(B, T, NSA_QW)


NSA_TQ = 128
NSA_CK = 512
MASKED = -1e30


def _softmax_rows(s):
    m = jnp.max(s, axis=-1, keepdims=True)
    e = jnp.exp(s - m)
    return e / jnp.sum(e, axis=-1, keepdims=True)


def _nsa_prompt_kernel(q_ref, kc_ref, vc_ref, ks_ref, vs_ref, kw_ref, vw_ref, zg_ref, msel_ref, exp_ref, o_ref):
    f32, bf16 = jnp.float32, jnp.bfloat16
    tq = NSA_TQ
    q0 = pl.program_id(2) * tq
    qb = q_ref[0]
    qs = jnp.concatenate([qb[:, g * NSA_HD:(g + 1) * NSA_HD] for g in range(NSA_GROUP)], axis=0)
    tpos = q0 + lax.broadcasted_iota(jnp.int32, (tq, 1), 0)

    def per_head(a):
        return jnp.concatenate([a] * NSA_GROUP, axis=0)

    s = lax.dot_general(qs, kc_ref[0, 0], _NT, preferred_element_type=f32)
    cblk = lax.broadcasted_iota(jnp.int32, (tq, 128), 1)
    cvalid = cblk * CMP_STRIDE + (CMP_BLOCK - 1) <= tpos
    s = s + per_head(jnp.where(cvalid, 0.0, MASKED))
    e = jnp.exp(s - jnp.max(s, axis=-1, keepdims=True)) * per_head(jnp.where(cvalid, 1.0, 0.0))
    l = jnp.sum(e, axis=-1, keepdims=True)
    p = e / jnp.where(l > 0.0, l, 1.0)
    o_cmp = jnp.dot(p.astype(bf16), vc_ref[0, 0], preferred_element_type=f32)

    imp = p[0:tq]
    for g in range(1, NSA_GROUP):
        imp = imp + p[g * tq:(g + 1) * tq]
    hi = imp.astype(bf16)
    r1 = imp - hi.astype(f32)
    mid = r1.astype(bf16)
    lo = (r1 - mid.astype(f32)).astype(bf16)
    msel = msel_ref[...]
    score = (jnp.dot(hi, msel, preferred_element_type=f32) + jnp.dot(mid, msel, preferred_element_type=f32)
             + jnp.dot(lo, msel, preferred_element_type=f32))
    n_sel = score.shape[1]
    j = lax.broadcasted_iota(jnp.int32, (tq, n_sel), 1)
    cur = tpos // SEL_BLOCK
    forced = (j == 0) | (j == cur) | (j == cur - 1)
    score = jnp.where(forced, jnp.inf, jnp.where(j * SEL_BLOCK <= tpos, score, -jnp.inf))
    rank = jnp.zeros((tq, n_sel), f32)
    for jp in range(n_sel):
        col = score[:, jp:jp + 1]
        before = (col > score) | ((col == score) & (j > jp))
        rank = rank + jnp.where(before, 1.0, 0.0)
    sel01 = jnp.where(rank < SEL_TOP, 1.0, 0.0).astype(bf16)

    ck = NSA_CK
    rows = NSA_GROUP * tq

    def sel_chunk(c, carry):
        m, l, acc = carry
        k0 = pl.multiple_of(c * ck, ck)
        s = lax.dot_general(qs, ks_ref[0, 0, pl.ds(k0, ck), :], _NT, preferred_element_type=f32)
        chosen = jnp.dot(sel01, exp_ref[c], preferred_element_type=f32)
        kpos = k0 + lax.broadcasted_iota(jnp.int32, (tq, ck), 1)
        ok = (chosen > 0.5) & (kpos <= tpos)
        s = s + per_head(jnp.where(ok, 0.0, MASKED))
        m_new = jnp.maximum(m, jnp.max(s, axis=-1, keepdims=True))
        a = jnp.exp(m - m_new)
        pr = jnp.exp(s - m_new)
        l = a * l + jnp.sum(pr, axis=-1, keepdims=True)
        acc = a * acc + jnp.dot(pr.astype(bf16), vs_ref[0, 0, pl.ds(k0, ck), :], preferred_element_type=f32)
        return m_new, l, acc

    init = (jnp.full((rows, 1), MASKED, f32), jnp.zeros((rows, 1), f32), jnp.zeros((rows, NSA_HD), f32))
    n_chunks = (q0 + tq + ck - 1) // ck
    _, l_sel, acc_sel = lax.fori_loop(0, n_chunks, sel_chunk, init)
    o_sel = acc_sel / l_sel

    w0 = pl.multiple_of(jnp.maximum(q0 - WINDOW, 0), tq)
    wl = WINDOW + tq
    s = lax.dot_general(qs, kw_ref[0, 0, pl.ds(w0, wl), :], _NT, preferred_element_type=f32)
    diff = tpos - (w0 + lax.broadcasted_iota(jnp.int32, (tq, wl), 1))
    s = s + per_head(jnp.where((diff >= 0) & (diff < WINDOW), 0.0, MASKED))
    o_win = jnp.dot(_softmax_rows(s).astype(bf16), vw_ref[0, 0, pl.ds(w0, wl), :], preferred_element_type=f32)

    gates = jax.nn.sigmoid(zg_ref[0, 0])
    for g in range(NSA_GROUP):
        r = slice(g * tq, (g + 1) * tq)
        o_ref[0, :, g * NSA_HD:(g + 1) * NSA_HD] = (gates[:, 3 * g:3 * g + 1] * o_cmp[r]
                                                    + gates[:, 3 * g + 1:3 * g + 2] * o_sel[r]
                                                    + gates[:, 3 * g + 2:3 * g + 3] * o_win[r])


def nsa_prompt_attention(q, kc, vc, k_slc, v_slc, k_win, v_win, zg):
    B, S = q.shape[:2]
    bf16 = jnp.bfloat16
    assert S % NSA_CK == 0 and S % NSA_TQ == 0 and WINDOW % NSA_TQ == 0 and WINDOW + NSA_TQ <= S
    n_sel = S // SEL_BLOCK
    nb = kc.shape[1]
    assert nb <= 128

    def kv_major(a):
        return a.transpose(0, 2, 1, 3).astype(bf16)

    def pad_blocks(a):
        return jnp.pad(kv_major(a), ((0, 0), (0, 0), (0, 128 - nb), (0, 0)))

    qs = (q * ATTN_SCALE).reshape(B, S, NSA_QW).astype(bf16)
    zg4 = zg.reshape(B, S, NSA_KV_HEADS, 3 * NSA_GROUP).transpose(0, 2, 1, 3)
    c = np.arange(128)[:, None]
    jj = np.arange(n_sel)[None, :]
    ratio = SEL_BLOCK // CMP_STRIDE
    msel = ((c >= jj * ratio - (CMP_BLOCK // CMP_STRIDE - 1)) & (c <= jj * ratio + ratio - 1) & (c < nb))
    expand = (np.arange(S)[None, :] // SEL_BLOCK == np.arange(n_sel)[:, None])
    expand = expand.reshape(n_sel, S // NSA_CK, NSA_CK).transpose(1, 0, 2)
    row_spec = pl.BlockSpec((1, 1, S, NSA_HD), lambda b, n, i: (b, n, 0, 0))
    blk_spec = pl.BlockSpec((1, 1, 128, NSA_HD), lambda b, n, i: (b, n, 0, 0))
    return pl.pallas_call(
        _nsa_prompt_kernel,
        out_shape=jax.ShapeDtypeStruct((B, S, NSA_QW), jnp.float32),
        grid=(B, NSA_KV_HEADS, S // NSA_TQ),
        in_specs=[pl.BlockSpec((1, NSA_TQ, NSA_GROUP * NSA_HD), lambda b, n, i: (b, i, n)),
                  blk_spec, blk_spec, row_spec, row_spec, row_spec, row_spec,
                  pl.BlockSpec((1, 1, NSA_TQ, 3 * NSA_GROUP), lambda b, n, i: (b, n, i, 0)),
                  pl.BlockSpec((128, n_sel), lambda b, n, i: (0, 0)),
                  pl.BlockSpec((S // NSA_CK, n_sel, NSA_CK), lambda b, n, i: (0, 0, 0))],
        out_specs=pl.BlockSpec((1, NSA_TQ, NSA_GROUP * NSA_HD), lambda b, n, i: (b, i, n)),
        compiler_params=pltpu.CompilerParams(dimension_semantics=("arbitrary", "arbitrary", "arbitrary"),
                                             vmem_limit_bytes=VMEM_LIMIT_BYTES),
        name="nsa_prompt_attention",
    )(qs, pad_blocks(kc), pad_blocks(vc), kv_major(k_slc), kv_major(v_slc), kv_major(k_win), kv_major(v_win),
      zg4, jnp.asarray(msel, bf16), jnp.asarray(expand, bf16))


MLSTM_L = 128
CONV_HALO = 8


def _log_sigmoid(x):
    return -(jnp.maximum(-x, 0.0) + jnp.log1p(jnp.exp(-jnp.abs(x))))


def _mlstm_prompt_kernel(x_ref, xprev_ref, halo0_ref, v_ref, o_ref, gcol_ref, grow_ref, cw_ref, cb_ref,
                         out_ref, c_out, n_out, m_out, c_ref, n_ref, m_ref):
    f32, bf16 = jnp.float32, jnp.bfloat16
    c = pl.program_id(1)
    L = MLSTM_L

    @pl.when(c == 0)
    def _():
        c_ref[...] = jnp.zeros_like(c_ref)
        n_ref[...] = jnp.zeros_like(n_ref)
        m_ref[...] = jnp.zeros_like(m_ref)

    x = x_ref[0]
    halo = jnp.where(c == 0, halo0_ref[0], xprev_ref[0, L - CONV_HALO:L, :])
    ext = jnp.concatenate([halo, x], axis=0)
    conv = cb_ref[...]
    for j in range(CONV_W):
        o = CONV_HALO - (CONV_W - 1) + j
        conv = conv + ext[o:o + L] * cw_ref[j:j + 1, :]
    qk = conv * jax.nn.sigmoid(conv)

    t_id = lax.broadcasted_iota(jnp.int32, (L, L), 0)
    s_id = lax.broadcasted_iota(jnp.int32, (L, L), 1)
    causal = t_id >= s_id
    gcol = gcol_ref[0, 0]
    grow = grow_ref[0, 0]
    for h in range(M_HEADS):
        hd = slice(h * M_HD, (h + 1) * M_HD)
        q = qk[:, hd]
        k = qk[:, M_W + h * M_HD:M_W + (h + 1) * M_HD] * (M_HD ** -0.5)
        v = v_ref[0, :, hd].astype(bf16)
        ig_r = grow[h:h + 1, :]
        ig_c = gcol[:, h:h + 1]
        lf_r = _log_sigmoid(grow[M_HEADS + h:M_HEADS + h + 1, :])
        lf_c = _log_sigmoid(gcol[:, M_HEADS + h:M_HEADS + h + 1])
        b_c = jnp.sum(jnp.where(causal, lf_r, 0.0), axis=1, keepdims=True)
        b_r = jnp.sum(jnp.where(t_id <= s_id, lf_c, 0.0), axis=0, keepdims=True)
        m_prev = m_ref[h]
        dmat = jnp.where(causal, b_c - b_r + ig_r, -jnp.inf)
        inter = b_c + m_prev
        m_t = jnp.maximum(inter, jnp.max(dmat, axis=1, keepdims=True))
        w_intra = jnp.exp(dmat - m_t)
        w_inter = jnp.exp(inter - m_t)
        qb = q.astype(bf16)
        s = lax.dot_general(qb, k.astype(bf16), _NT, preferred_element_type=f32) * w_intra
        num = (jnp.dot(s.astype(bf16), v, preferred_element_type=f32)
               + w_inter * jnp.dot(qb, c_ref[h].astype(bf16), preferred_element_type=f32))
        den = jnp.sum(s, axis=1, keepdims=True) + w_inter * jnp.sum(q * n_ref[h], axis=1, keepdims=True)
        hh = num / jnp.maximum(jnp.abs(den), jnp.exp(-m_t))
        out_ref[0, :, hd] = jax.nn.sigmoid(o_ref[0, :, hd]) * hh
        m_new = m_t[L - 1:L]
        b_last = b_c[L - 1:L]
        w_s = jnp.exp(b_last - b_c + ig_c - m_new)
        w_p = jnp.exp(b_last + m_prev - m_new)
        kw = k * w_s
        c_ref[h] = w_p * c_ref[h] + jnp.dot(kw.T.astype(bf16), v, preferred_element_type=f32)
        n_ref[h] = w_p * n_ref[h] + jnp.sum(kw, axis=0, keepdims=True)
        m_ref[h] = m_new

    @pl.when(c == pl.num_programs(1) - 1)
    def _():
        c_out[0] = c_ref[...]
        n_out[0] = n_ref[...]
        m_out[0] = m_ref[...]


def mlstm_prompt(zqk, zv, zo, zif, conv_w, conv_b, b_if):
    B, T, _ = zqk.shape
    L = MLSTM_L
    assert T % L == 0
    nc = T // L
    f32 = jnp.float32
    gif = zif + b_if
    gcol = gif.reshape(B, nc, L, 2 * M_HEADS)
    grow = gcol.transpose(0, 1, 3, 2)
    halo0 = jnp.zeros((B, CONV_HALO, 2 * M_W), f32)
    out, C, n, m = pl.pallas_call(
        _mlstm_prompt_kernel,
        out_shape=(jax.ShapeDtypeStruct((B, T, M_W), f32),
                   jax.ShapeDtypeStruct((B, M_HEADS, M_HD, M_HD), f32),
                   jax.ShapeDtypeStruct((B, M_HEADS, 1, M_HD), f32),
                   jax.ShapeDtypeStruct((B, M_HEADS, 1, 1), f32)),
        grid=(B, nc),
        in_specs=[pl.BlockSpec((1, L, 2 * M_W), lambda b, c: (b, c, 0)),
                  pl.BlockSpec((1, L, 2 * M_W), lambda b, c: (b, jnp.maximum(c - 1, 0), 0)),
                  pl.BlockSpec((1, CONV_HALO, 2 * M_W), lambda b, c: (b, 0, 0)),
                  pl.BlockSpec((1, L, M_W), lambda b, c: (b, c, 0)),
                  pl.BlockSpec((1, L, M_W), lambda b, c: (b, c, 0)),
                  pl.BlockSpec((1, 1, L, 2 * M_HEADS), lambda b, c: (b, c, 0, 0)),
                  pl.BlockSpec((1, 1, 2 * M_HEADS, L), lambda b, c: (b, c, 0, 0)),
                  pl.BlockSpec((CONV_W, 2 * M_W), lambda b, c: (0, 0)),
                  pl.BlockSpec((1, 2 * M_W), lambda b, c: (0, 0))],
        out_specs=(pl.BlockSpec((1, L, M_W), lambda b, c: (b, c, 0)),
                   pl.BlockSpec((1, M_HEADS, M_HD, M_HD), lambda b, c: (b, 0, 0, 0)),
                   pl.BlockSpec((1, M_HEADS, 1, M_HD), lambda b, c: (b, 0, 0, 0)),
                   pl.BlockSpec((1, M_HEADS, 1, 1), lambda b, c: (b, 0, 0, 0))),
        scratch_shapes=[pltpu.VMEM((M_HEADS, M_HD, M_HD), f32), pltpu.VMEM((M_HEADS, 1, M_HD), f32),
                        pltpu.VMEM((M_HEADS, 1, 1), f32)],
        compiler_params=pltpu.CompilerParams(dimension_semantics=("arbitrary", "arbitrary"),
                                             vmem_limit_bytes=VMEM_LIMIT_BYTES),
        name="mlstm_prompt",
    )(zqk, zqk, halo0, zv, zo, gcol, grow, conv_w, conv_b[None])
    return out, C, n.reshape(B, M_HEADS, M_HD), m.reshape(B, M_HEADS)


def mlstm_chunk(carry, inp):
    C, n, m = carry
    q, k, v, ig, lf = inp
    L = q.shape[2]
    b = jnp.cumsum(lf, axis=-1)
    causal = jnp.tril(jnp.ones((L, L), dtype=bool))
    dmat = jnp.where(causal, b[..., :, None] - b[..., None, :] + ig[..., None, :], -jnp.inf)
    inter = b + m[..., None]
    m_t = jnp.maximum(inter, dmat.max(axis=-1))
    w_intra = jnp.exp(dmat - m_t[..., None])
    w_inter = jnp.exp(inter - m_t)
    s = jnp.einsum('bhtd,bhsd->bhts', q, k) * w_intra
    num = jnp.einsum('bhts,bhsv->bhtv', s, v) + w_inter[..., None] * jnp.einsum('bhtd,bhdv->bhtv', q, C)
    den = s.sum(-1) + w_inter * jnp.einsum('bhtd,bhd->bht', q, n)
    h = num / jnp.maximum(jnp.abs(den), jnp.exp(-m_t))[..., None]
    m_new = m_t[..., -1]
    w_s = jnp.exp(b[..., -1:] - b + ig - m_new[..., None])
    w_p = jnp.exp(b[..., -1] + m - m_new)
    C_new = w_p[..., None, None] * C + jnp.einsum('bhs,bhsd,bhsv->bhdv', w_s, k, v)
    n_new = w_p[..., None] * n + jnp.einsum('bhs,bhsd->bhd', w_s, k)
    return (C_new, n_new, m_new), h


def mlstm_mix(zqk, zv, zo, zif, buf0, C0, n0, m0, conv_w, conv_b, b_if, chunk):
    B, T, _ = zqk.shape
    full = jnp.concatenate([buf0, zqk], axis=1)
    conv = conv_b
    for j in range(CONV_W):
        conv = conv + full[:, j:j + T] * conv_w[j]
    qk = jax.nn.silu(conv)

    def heads(a):
        return a.reshape(B, T, M_HEADS, M_HD).transpose(0, 2, 1, 3)

    q = heads(qk[..., :M_W])
    k = heads(qk[..., M_W:]) * (M_HD ** -0.5)
    v = heads(zv)
    gif = zif + b_if
    ig = gif[..., :M_HEADS].transpose(0, 2, 1)
    lf = jax.nn.log_sigmoid(gif[..., M_HEADS:]).transpose(0, 2, 1)
    nc = T // chunk

    def to_chunks(a):
        return jnp.moveaxis(a.reshape(B, M_HEADS, nc, chunk, *a.shape[3:]), 2, 0)

    (C, n, m), h = lax.scan(mlstm_chunk, (C0, n0, m0),
                            (to_chunks(q), to_chunks(k), to_chunks(v), to_chunks(ig), to_chunks(lf)))
    h = jnp.moveaxis(h, 0, 2).reshape(B, M_HEADS, T, M_HD).transpose(0, 2, 1, 3).reshape(B, T, M_W)
    out = jax.nn.sigmoid(zo) * h
    return out, (C, n, m, full[:, T:])


PEER_COMBOS = 2 * PEER_HEADS
PEER_KEY_ROWS = 8
PEER_TILE = PEER_KEY_ROWS * N_KEYS
PEER_TS_ROWS = 24
LANES = 128
_NT = (((1,), (1,)), ((), ()))


def _peer_topk_kernel(q_ref, keys_ref, s_ref, e0_ref, e1_ref, tau_ref, ts_ref):
    c = pl.program_id(1)
    tt = q_ref.shape[0]
    s = lax.dot_general(keys_ref[0], q_ref[...].astype(jnp.bfloat16), _NT,
                        preferred_element_type=jnp.float32)
    s_ref[c] = s
    key_id = lax.broadcasted_iota(jnp.int32, s.shape, 0)
    work = s
    rows = []
    for _ in range(PEER_TOPK + 1):
        m = jnp.max(work, axis=0, keepdims=True)
        first = jnp.min(jnp.where(work == m, key_id, N_KEYS), axis=0, keepdims=True)
        work = jnp.where(key_id == first, -jnp.inf, work)
        rows.append(m)
    rows.append(jnp.full((PEER_TS_ROWS - PEER_TOPK - 1, tt), -jnp.inf, jnp.float32))
    ts_ref[c] = jnp.concatenate(rows, axis=0)

    @pl.when(c == PEER_COMBOS - 1)
    def _():
        for h in range(PEER_HEADS):
            t0 = ts_ref[2 * h]
            t1 = ts_ref[2 * h + 1]
            pieces = [t0[0:1] + t1] + [t0[a:a + 1] + t1[0:8] for a in range(1, 8)] + [t0[8:24] + t1[0:1]]
            cand = jnp.concatenate(pieces, axis=0)
            top = t0[0:1] + t1[0:1]
            v16 = top
            v17 = top
            z = jnp.zeros_like(top)
            seen = jnp.zeros_like(top)
            for _ in range(PEER_TOPK + 1):
                m = jnp.max(cand, axis=0, keepdims=True)
                eq = cand == m
                cnt = jnp.sum(jnp.where(eq, 1.0, 0.0), axis=0, keepdims=True)
                active = seen < PEER_TOPK
                take = jnp.minimum(cnt, PEER_TOPK - seen)
                v16 = jnp.where(active, m, v16)
                v17 = jnp.where(seen < PEER_TOPK + 1, m, v17)
                z = z + jnp.where(active, take * jnp.exp(m - top), 0.0)
                seen = seen + cnt
                cand = jnp.where(eq, -jnp.inf, cand)
            tau_ref[h:h + 1, :] = 0.5 * v16 + 0.5 * v17
            e0_ref[h] = jnp.exp(s_ref[2 * h] - t0[0:1]) / z
            e1_ref[h] = jnp.exp(s_ref[2 * h + 1] - t1[0:1])


def peer_scores(q, sub_keys, tt):
    n = q.shape[0]
    assert n % tt == 0
    keys = sub_keys.reshape(PEER_COMBOS, N_KEYS, PEER_QDIM // 2).astype(jnp.bfloat16)
    f32 = jnp.float32
    return pl.pallas_call(
        _peer_topk_kernel,
        out_shape=(jax.ShapeDtypeStruct((PEER_COMBOS, N_KEYS, n), f32),
                   jax.ShapeDtypeStruct((PEER_HEADS, N_KEYS, n), f32),
                   jax.ShapeDtypeStruct((PEER_HEADS, N_KEYS, n), f32),
                   jax.ShapeDtypeStruct((PEER_HEADS, n), f32)),
        grid=(n // tt, PEER_COMBOS),
        in_specs=[pl.BlockSpec((tt, PEER_QDIM // 2), lambda i, c: (i, c)),
                  pl.BlockSpec((1, N_KEYS, PEER_QDIM // 2), lambda i, c: (c, 0, 0))],
        out_specs=(pl.BlockSpec((PEER_COMBOS, N_KEYS, tt), lambda i, c: (0, 0, i)),
                   pl.BlockSpec((PEER_HEADS, N_KEYS, tt), lambda i, c: (0, 0, i)),
                   pl.BlockSpec((PEER_HEADS, N_KEYS, tt), lambda i, c: (0, 0, i)),
                   pl.BlockSpec((PEER_HEADS, tt), lambda i, c: (0, i))),
        scratch_shapes=[pltpu.VMEM((PEER_COMBOS, PEER_TS_ROWS, tt), f32)],
        compiler_params=pltpu.CompilerParams(dimension_semantics=("arbitrary", "arbitrary"),
                                             vmem_limit_bytes=VMEM_LIMIT_BYTES),
        name="peer_topk",
    )(q, keys)


def _peer_dense_kernel(xb_ref, h_ref, u_ref, vt_ref, s0_ref, ez_ref, s_ref, e1_ref, tau_ref, g_ref, b_ref,
                       o_ref, acc_ref, a_ref, w_ref):
    e = pl.program_id(1)
    tt = xb_ref.shape[0]

    @pl.when(e == 0)
    def _():
        acc_ref[...] = jnp.zeros_like(acc_ref)

    a_ref[...] = lax.dot_general(u_ref[...], xb_ref[...], _NT, preferred_element_type=jnp.float32)
    for t in range(tt // LANES):
        tok = slice(t * LANES, (t + 1) * LANES)
        for r in range(PEER_KEY_ROWS):
            rows = slice(r * N_KEYS, (r + 1) * N_KEYS)
            gate = jnp.zeros((N_KEYS, LANES), jnp.float32)
            for h in range(PEER_HEADS):
                need = tau_ref[h:h + 1, tok] - s0_ref[2 * h, r:r + 1, tok]
                picked = jnp.where(s_ref[2 * h + 1, :, tok] >= need, e1_ref[h, :, tok], 0.0)
                gate = gate + picked * ez_ref[h, r:r + 1, tok]
            ar = a_ref[rows, tok]
            act = 0.5 * ar * (1.0 + lax.erf(ar * (2.0 ** -0.5)))
            w_ref[rows, tok] = (gate * act).astype(jnp.bfloat16)
    acc_ref[...] += jnp.dot(vt_ref[...], w_ref[...], preferred_element_type=jnp.float32)

    @pl.when(e == pl.num_programs(1) - 1)
    def _():
        r = ALPHA * h_ref[...] + acc_ref[...].T
        mu = jnp.mean(r, axis=-1, keepdims=True)
        d = r - mu
        var = jnp.mean(d * d, axis=-1, keepdims=True)
        o_ref[...] = d * lax.rsqrt(var + LN_EPS) * g_ref[...] + b_ref[...]


def peer_tail(h, q, sub_keys, u_bf, vt_bf, ln_g, ln_b, tt):
    n, d = h.shape
    s, e0z, e1, tau = peer_scores(q, sub_keys, tt)
    n_exp = u_bf.shape[0]
    return pl.pallas_call(
        _peer_dense_kernel,
        out_shape=jax.ShapeDtypeStruct((n, d), jnp.float32),
        grid=(n // tt, n_exp // PEER_TILE),
        in_specs=[pl.BlockSpec((tt, d), lambda i, e: (i, 0)),
                  pl.BlockSpec((tt, d), lambda i, e: (i, 0)),
                  pl.BlockSpec((PEER_TILE, d), lambda i, e: (e, 0)),
                  pl.BlockSpec((d, PEER_TILE), lambda i, e: (0, e)),
                  pl.BlockSpec((PEER_COMBOS, PEER_KEY_ROWS, tt), lambda i, e: (0, e, i)),
                  pl.BlockSpec((PEER_HEADS, PEER_KEY_ROWS, tt), lambda i, e: (0, e, i)),
                  pl.BlockSpec((PEER_COMBOS, N_KEYS, tt), lambda i, e: (0, 0, i)),
                  pl.BlockSpec((PEER_HEADS, N_KEYS, tt), lambda i, e: (0, 0, i)),
                  pl.BlockSpec((PEER_HEADS, tt), lambda i, e: (0, i)),
                  pl.BlockSpec((1, d), lambda i, e: (0, 0)),
                  pl.BlockSpec((1, d), lambda i, e: (0, 0))],
        out_specs=pl.BlockSpec((tt, d), lambda i, e: (i, 0)),
        scratch_shapes=[pltpu.VMEM((d, tt), jnp.float32), pltpu.VMEM((PEER_TILE, tt), jnp.float32),
                        pltpu.VMEM((PEER_TILE, tt), jnp.bfloat16)],
        compiler_params=pltpu.CompilerParams(dimension_semantics=("arbitrary", "arbitrary"),
                                             vmem_limit_bytes=VMEM_LIMIT_BYTES),
        name="peer_dense",
    )(h.astype(jnp.bfloat16), h, u_bf, vt_bf, s, e0z, s, e1, tau, ln_g[None], ln_b[None])


def block_tail(x, mix, w_out, ln_g, ln_b, w_pq, sub_keys, u_bf, vt_bf, tt):
    lead = x.shape[:-1]
    h = layer_norm(ALPHA * x + mm3(mix, w_out), ln_g[0], ln_b[0]).reshape(-1, D_MODEL)
    q = pallas_matmul(h, w_pq)
    return peer_tail(h, q, sub_keys, u_bf, vt_bf, ln_g[1], ln_b[1], tt).reshape(*lead, D_MODEL)


def prompt_mix(x, w_in, pe, w1, b1, w2, conv_w, conv_b, b_if):
    B, S, _ = x.shape
    pos = jnp.arange(S)
    zq, zkv, zg, zqk, zv, zo, zif = split_in_proj(x, w_in)
    q, (k_cmp, v_cmp, k_slc, v_slc, k_win, v_win), gates = nsa_project(zq, zkv, zg, pos)
    chunk_w = CMP_STRIDE * NSA_KVW

    def compressed(rows, c):
        proj = chunk_projection(rows.reshape(B * (S // CMP_STRIDE), chunk_w), w1[c])
        return compress_from_projection(proj.reshape(B, S // CMP_STRIDE, -1), pe[c], w1[c], b1[c], w2[c])

    kc = compressed(k_cmp, 0)
    vc = compressed(v_cmp, 1)
    o_nsa = nsa_prompt_attention(q, kc, vc, k_slc, v_slc, k_win, v_win, zg)
    o_m, C, n, m = mlstm_prompt(zqk, zv, zo, zif, conv_w, conv_b, b_if)
    buf = zqk[:, S - (CONV_W - 1):]
    wl = min(WINDOW, S)
    mix = jnp.concatenate([o_nsa, o_m], axis=-1)
    return mix, (k_cmp, v_cmp, k_slc, v_slc, k_win[:, S - wl:], v_win[:, S - wl:], C, n, m, buf)


def sample_mix(x, kc_pool, vc_pool, ks_pool, vs_pool, kw_buf, vw_buf, C0, n0, m0, buf0, page_table,
               w_in, pe, w1, b1, w2, conv_w, conv_b, b_if):
    B, T, _ = x.shape
    past = page_table.shape[1] * PAGE_SIZE
    pos = past + jnp.arange(T)
    zq, zkv, zg, zqk, zv, zo, zif = split_in_proj(x, w_in)
    q, (k_cmp, v_cmp, k_slc, v_slc, k_win, v_win), gates = nsa_project(zq, zkv, zg, pos)

    def with_past(pool, new):
        old = pool[page_table].reshape(B, past, NSA_KV_HEADS, NSA_HD)
        return jnp.concatenate([old, new], axis=1)

    chunks_per_page = PAGE_SIZE // CMP_STRIDE
    assert (past + T) // CMP_STRIDE == past // CMP_STRIDE

    def compressed(pool, c):
        proj = chunk_projection(pool.reshape(pool.shape[0] * chunks_per_page, CMP_STRIDE * NSA_KVW), w1[c])
        proj = proj.reshape(pool.shape[0], chunks_per_page, -1)[page_table]
        return compress_from_projection(proj.reshape(B, past // CMP_STRIDE, -1), pe[c], w1[c], b1[c], w2[c])

    o_cmp, p = cmp_attend(q, pos, compressed(kc_pool, 0), compressed(vc_pool, 1))
    n_sel = -(-(past + T) // SEL_BLOCK)
    sel = select_blocks(p, pos, n_sel)
    kb = to_blocks(with_past(ks_pool, k_slc), n_sel)
    vb = to_blocks(with_past(vs_pool, v_slc), n_sel)
    wb = kw_buf.shape[1]
    kw = jnp.concatenate([kw_buf, k_win], axis=1)
    vw = jnp.concatenate([vw_buf, v_win], axis=1)
    kpos = past - wb + jnp.arange(wb + T)
    o_sel = sel_attend(q, pos, sel, kb, vb)
    o_win = win_attend(q, pos, kw, vw, kpos)
    o_nsa = nsa_combine(gates, o_cmp, o_sel, o_win)
    o_m, (C, n, m, buf) = mlstm_mix(zqk, zv, zo, zif, buf0, C0, n0, m0, conv_w, conv_b, b_if, T)
    mix = jnp.concatenate([o_nsa, o_m], axis=-1)
    return mix, (k_cmp, v_cmp, k_slc, v_slc, kw[:, T:], vw[:, T:], C, n, m, buf)


def kernel(x_prompt, x_sample, cache_k_cmp, cache_v_cmp, cache_k_slc, cache_v_slc, cache_k_win, cache_v_win,
           state_C, state_n, state_m, state_conv, page_table, w_in, w_out, w_phi1, b_phi1, w_phi2, pe_cmp,
           conv_w, conv_b, b_if, ln_g, ln_b, w_pq, sub_keys, u_tab, v_tab):
    l = 0
    mix_p, st_p = prompt_mix(x_prompt, w_in[l], pe_cmp[l], w_phi1[l], b_phi1[l], w_phi2[l],
                             conv_w[l], conv_b[l], b_if[l])
    mix_s, st_s = sample_mix(x_sample, cache_k_cmp[l], cache_v_cmp[l], cache_k_slc[l], cache_v_slc[l],
                             cache_k_win[l], cache_v_win[l], state_C[l], state_n[l], state_m[l],
                             state_conv[l], page_table, w_in[l], pe_cmp[l], w_phi1[l], b_phi1[l],
                             w_phi2[l], conv_w[l], conv_b[l], b_if[l])
    u_bf = u_tab[l].astype(jnp.bfloat16)
    vt_bf = v_tab[l].astype(jnp.bfloat16).T
    xp = block_tail(x_prompt, mix_p, w_out[l], ln_g[l], ln_b[l], w_pq[l], sub_keys[l], u_bf, vt_bf, 512)
    xs = block_tail(x_sample, mix_s, w_out[l], ln_g[l], ln_b[l], w_pq[l], sub_keys[l], u_bf, vt_bf, 128)
    return (xp, xs) + tuple(a[None] for a in st_p) + tuple(a[None] for a in st_s)
```

```python
import jax
import jax.numpy as jnp
import numpy as np
from jax import lax
from jax.experimental import pallas as pl
from jax.experimental.pallas import tpu as pltpu

D_MODEL = 1024
DEPTH = 1
PAGE_SIZE = 128
NSA_HEADS = 8
NSA_KV_HEADS = 2
NSA_GROUP = NSA_HEADS // NSA_KV_HEADS
NSA_HD = 64
NSA_QW = NSA_HEADS * NSA_HD
NSA_KVW = NSA_KV_HEADS * NSA_HD
CMP_BLOCK = 32
CMP_STRIDE = 16
SEL_BLOCK = 64
SEL_TOP = 16
WINDOW = 512
Q_BLOCK = 64
ATTN_SCALE = NSA_HD ** -0.5
ROPE_THETA = 10000.0
M_HEADS = 4
M_HD = 128
M_W = M_HEADS * M_HD
M_CHUNK = 64
CONV_W = 4
PEER_HEADS = 8
N_KEYS = 128
PEER_TOPK = 16
PEER_QDIM = 256
PEER_BLOCK = 128
IN_SPLITS = (NSA_QW, 6 * NSA_KVW, 3 * NSA_HEADS, 2 * M_W, M_W, M_W, 2 * M_HEADS)
LN_EPS = 1e-5
ALPHA = (2 * DEPTH) ** 0.25

VMEM_LIMIT_BYTES = 56 * 1024 * 1024


def _mm_kernel(x_ref, w_ref, o_ref):
    o_ref[...] = jnp.dot(x_ref[...].astype(jnp.bfloat16), w_ref[...], preferred_element_type=jnp.float32)


def pallas_matmul(x, w, tm=512):
    M, K = x.shape
    N = w.shape[1]
    tm = min(tm, M)
    assert M % tm == 0
    return pl.pallas_call(
        _mm_kernel,
        out_shape=jax.ShapeDtypeStruct((M, N), jnp.float32),
        grid=(M // tm,),
        in_specs=[pl.BlockSpec((tm, K), lambda i: (i, 0)), pl.BlockSpec((K, N), lambda i: (0, 0))],
        out_specs=pl.BlockSpec((tm, N), lambda i: (i, 0)),
        compiler_params=pltpu.CompilerParams(dimension_semantics=("arbitrary",),
                                             vmem_limit_bytes=VMEM_LIMIT_BYTES),
        name="proj_matmul",
    )(x, w.astype(jnp.bfloat16))


def mm3(x, w):
    lead = x.shape[:-1]
    return pallas_matmul(x.reshape(-1, x.shape[-1]), w).reshape(*lead, w.shape[1])


def layer_norm(x, g, b):
    mu = x.mean(-1, keepdims=True)
    var = jnp.square(x - mu).mean(-1, keepdims=True)
    return (x - mu) * lax.rsqrt(var + LN_EPS) * g + b


def rope(x, pos):
    half = x.shape[-1] // 2
    inv = ROPE_THETA ** (-jnp.arange(half, dtype=jnp.float32) / half)
    ang = pos.astype(jnp.float32)[:, None] * inv[None, :]
    cos = jnp.cos(ang)[:, None, :]
    sin = jnp.sin(ang)[:, None, :]
    x1, x2 = x[..., :half], x[..., half:]
    return jnp.concatenate([x1 * cos - x2 * sin, x2 * cos + x1 * sin], axis=-1)


def split_in_proj(x, w_in):
    z = mm3(x, w_in)
    cuts = [int(c) for c in np.cumsum(IN_SPLITS)[:-1]]
    return jnp.split(z, cuts, axis=-1)


def nsa_project(zq, zkv, zg, pos):
    B, T, _ = zq.shape
    q = rope(zq.reshape(B, T, NSA_HEADS, NSA_HD), pos)
    kv = zkv.reshape(B, T, 6, NSA_KV_HEADS, NSA_HD)
    rows = (rope(kv[:, :, 0], pos), kv[:, :, 1], rope(kv[:, :, 2], pos), kv[:, :, 3],
            rope(kv[:, :, 4], pos), kv[:, :, 5])
    gates = jax.nn.sigmoid(zg).reshape(B, T, NSA_HEADS, 3)
    return q, rows, gates


def chunk_projection(chunks, w1):
    assert CMP_BLOCK == 2 * CMP_STRIDE
    w1r = w1.reshape(2, CMP_STRIDE, NSA_HD, w1.shape[-1])
    wbig = jnp.einsum('hpdf,kn->pkdnhf', w1r, jnp.eye(NSA_KV_HEADS, dtype=w1.dtype))
    return pallas_matmul(chunks, wbig.reshape(CMP_STRIDE * NSA_KVW, 2 * NSA_KV_HEADS * w1.shape[-1]))


def compress_from_projection(proj, pe, w1, b1, w2):
    f = w1.shape[-1]
    bias = jnp.dot(pe.reshape(-1), w1, precision=lax.Precision.HIGHEST) + b1
    heads = []
    for n in range(NSA_KV_HEADS):
        first = proj[:, :-1, 2 * n * f:(2 * n + 1) * f]
        second = proj[:, 1:, (2 * n + 1) * f:(2 * n + 2) * f]
        heads.append(jax.nn.gelu(first + second + bias, approximate=False) @ w2)
    return jnp.stack(heads, axis=2)


def cmp_attend(q, qpos, kc, vc):
    B, T = q.shape[:2]
    qg = q.reshape(B, T, NSA_KV_HEADS, NSA_GROUP, NSA_HD)
    s = jnp.einsum('btngd,bcnd->btngc', qg, kc) * ATTN_SCALE
    nblk = kc.shape[1]
    blk_end = jnp.arange(nblk) * CMP_STRIDE + CMP_BLOCK - 1
    valid = (blk_end[None, :] <= qpos[:, None])[None, :, None, None, :]
    p = jax.nn.softmax(jnp.where(valid, s, -1e30), axis=-1) * valid
    o = jnp.einsum('btngc,bcnd->btngd', p, vc)
    return o.reshape(B, T, NSA_HEADS, NSA_HD), p


def select_blocks(p, qpos, n_sel):
    imp = p.sum(axis=3)
    R = SEL_BLOCK // CMP_STRIDE
    r = CMP_BLOCK // CMP_STRIDE
    nb = imp.shape[-1]
    right = n_sel * R + R - 1 - nb
    padded = jnp.pad(imp, ((0, 0), (0, 0), (0, 0), (r - 1, right)))
    score = padded[..., 0:(n_sel - 1) * R + 1:R]
    for o in range(1, R + r - 1):
        score = score + padded[..., o:o + (n_sel - 1) * R + 1:R]
    j = jnp.arange(n_sel)[None, :]
    cur = (qpos // SEL_BLOCK)[:, None]
    valid = (j * SEL_BLOCK <= qpos[:, None])[None, :, None, :]
    forced = ((j == 0) | (j == cur) | (j == cur - 1))[None, :, None, :]
    score = jnp.where(forced, jnp.inf, jnp.where(valid, score, -jnp.inf))
    idx = j[0]
    before = (score[..., None, :] > score[..., :, None]) | ((score[..., None, :] == score[..., :, None])
                                                          & (idx[None, :] < idx[:, None]))
    rank = before.sum(-1)
    hit = rank[..., None, :] == jnp.arange(min(SEL_TOP, n_sel))[:, None]
    return (hit * idx).sum(-1)


def to_blocks(rows, n_sel):
    B, L, KV, hd = rows.shape
    rows = jnp.pad(rows, ((0, 0), (0, n_sel * SEL_BLOCK - L), (0, 0), (0, 0)))
    return rows.reshape(B, n_sel, SEL_BLOCK, KV, hd).transpose(0, 3, 1, 2, 4)


def take_rows(table, idx):
    return table[idx]


def sel_attend(q, qpos, sel, kb, vb):
    B, Tq = q.shape[:2]
    k = sel.shape[-1]
    sel_t = sel.transpose(0, 2, 1, 3)
    gather = jax.vmap(jax.vmap(take_rows))
    kg = gather(kb, sel_t).reshape(B, NSA_KV_HEADS, Tq, k * SEL_BLOCK, NSA_HD)
    vg = gather(vb, sel_t).reshape(B, NSA_KV_HEADS, Tq, k * SEL_BLOCK, NSA_HD)
    kpos = (sel_t[..., None] * SEL_BLOCK + jnp.arange(SEL_BLOCK)).reshape(B, NSA_KV_HEADS, Tq, k * SEL_BLOCK)
    qg = q.reshape(B, Tq, NSA_KV_HEADS, NSA_GROUP, NSA_HD).transpose(0, 2, 1, 3, 4)
    s = jnp.einsum('bntgd,bntsd->bntgs', qg, kg) * ATTN_SCALE
    mask = kpos[:, :, :, None, :] <= qpos[None, None, :, None, None]
    pr = jax.nn.softmax(jnp.where(mask, s, -jnp.inf), axis=-1)
    o = jnp.einsum('bntgs,bntsd->bntgd', pr, vg)
    return o.transpose(0, 2, 1, 3, 4).reshape(B, Tq, NSA_HEADS, NSA_HD)


def win_attend(q, qpos, k, v, kpos):
    B, Tq = q.shape[:2]
    qg = q.reshape(B, Tq, NSA_KV_HEADS, NSA_GROUP, NSA_HD)
    s = jnp.einsum('btngd,bsnd->btngs', qg, k) * ATTN_SCALE
    diff = qpos[:, None] - kpos[None, :]
    mask = ((diff >= 0) & (diff < WINDOW) & (kpos[None, :] >= 0))[None, :, None, None, :]
    pr = jax.nn.softmax(jnp.where(mask, s, -jnp.inf), axis=-1)
    o = jnp.einsum('btngs,bsnd->btngd', pr, v)
    return o.reshape(B, Tq, NSA_HEADS, NSA_HD)


def nsa_combine(gates, o_cmp, o_sel, o_win):
    B, T = gates.shape[:2]
    o = gates[..., 0:1] * o_cmp + gates[..., 1:2] * o_sel + gates[..., 2:3] * o_win
    return o.reshape(B, T, NSA_QW)


NSA_TQ = 128
NSA_CK = 512
MASKED = -1e30


def _softmax_rows(s):
    m = jnp.max(s, axis=-1, keepdims=True)
    e = jnp.exp(s - m)
    return e / jnp.sum(e, axis=-1, keepdims=True)


def _nsa_prompt_kernel(q_ref, kc_ref, vc_ref, ks_ref, vs_ref, kw_ref, vw_ref, zg_ref, msel_ref, exp_ref, o_ref):
    f32, bf16 = jnp.float32, jnp.bfloat16
    tq = NSA_TQ
    q0 = pl.program_id(2) * tq
    qb = q_ref[0]
    qs = jnp.concatenate([qb[:, g * NSA_HD:(g + 1) * NSA_HD] for g in range(NSA_GROUP)], axis=0)
    tpos = q0 + lax.broadcasted_iota(jnp.int32, (tq, 1), 0)

    def per_head(a):
        return jnp.concatenate([a] * NSA_GROUP, axis=0)

    s = lax.dot_general(qs, kc_ref[0, 0], _NT, preferred_element_type=f32)
    cblk = lax.broadcasted_iota(jnp.int32, (tq, 128), 1)
    cvalid = cblk * CMP_STRIDE + (CMP_BLOCK - 1) <= tpos
    s = s + per_head(jnp.where(cvalid, 0.0, MASKED))
    e = jnp.exp(s - jnp.max(s, axis=-1, keepdims=True)) * per_head(jnp.where(cvalid, 1.0, 0.0))
    l = jnp.sum(e, axis=-1, keepdims=True)
    p = e / jnp.where(l > 0.0, l, 1.0)
    o_cmp = jnp.dot(p.astype(bf16), vc_ref[0, 0], preferred_element_type=f32)

    imp = p[0:tq]
    for g in range(1, NSA_GROUP):
        imp = imp + p[g * tq:(g + 1) * tq]
    hi = imp.astype(bf16)
    r1 = imp - hi.astype(f32)
    mid = r1.astype(bf16)
    lo = (r1 - mid.astype(f32)).astype(bf16)
    msel = msel_ref[...]
    score = (jnp.dot(hi, msel, preferred_element_type=f32) + jnp.dot(mid, msel, preferred_element_type=f32)
             + jnp.dot(lo, msel, preferred_element_type=f32))
    n_sel = score.shape[1]
    j = lax.broadcasted_iota(jnp.int32, (tq, n_sel), 1)
    cur = tpos // SEL_BLOCK
    forced = (j == 0) | (j == cur) | (j == cur - 1)
    score = jnp.where(forced, jnp.inf, jnp.where(j * SEL_BLOCK <= tpos, score, -jnp.inf))
    rank = jnp.zeros((tq, n_sel), f32)
    for jp in range(n_sel):
        col = score[:, jp:jp + 1]
        before = (col > score) | ((col == score) & (j > jp))
        rank = rank + jnp.where(before, 1.0, 0.0)
    sel01 = jnp.where(rank < SEL_TOP, 1.0, 0.0).astype(bf16)

    ck = NSA_CK
    rows = NSA_GROUP * tq

    def sel_chunk(c, carry):
        m, l, acc = carry
        k0 = pl.multiple_of(c * ck, ck)
        s = lax.dot_general(qs, ks_ref[0, 0, pl.ds(k0, ck), :], _NT, preferred_element_type=f32)
        chosen = jnp.dot(sel01, exp_ref[c], preferred_element_type=f32)
        kpos = k0 + lax.broadcasted_iota(jnp.int32, (tq, ck), 1)
        ok = (chosen > 0.5) & (kpos <= tpos)
        s = s + per_head(jnp.where(ok, 0.0, MASKED))
        m_new = jnp.maximum(m, jnp.max(s, axis=-1, keepdims=True))
        a = jnp.exp(m - m_new)
        pr = jnp.exp(s - m_new)
        l = a * l + jnp.sum(pr, axis=-1, keepdims=True)
        acc = a * acc + jnp.dot(pr.astype(bf16), vs_ref[0, 0, pl.ds(k0, ck), :], preferred_element_type=f32)
        return m_new, l, acc

    init = (jnp.full((rows, 1), MASKED, f32), jnp.zeros((rows, 1), f32), jnp.zeros((rows, NSA_HD), f32))
    n_chunks = (q0 + tq + ck - 1) // ck
    _, l_sel, acc_sel = lax.fori_loop(0, n_chunks, sel_chunk, init)
    o_sel = acc_sel / l_sel

    w0 = pl.multiple_of(jnp.maximum(q0 - WINDOW, 0), tq)
    wl = WINDOW + tq
    s = lax.dot_general(qs, kw_ref[0, 0, pl.ds(w0, wl), :], _NT, preferred_element_type=f32)
    diff = tpos - (w0 + lax.broadcasted_iota(jnp.int32, (tq, wl), 1))
    s = s + per_head(jnp.where((diff >= 0) & (diff < WINDOW), 0.0, MASKED))
    o_win = jnp.dot(_softmax_rows(s).astype(bf16), vw_ref[0, 0, pl.ds(w0, wl), :], preferred_element_type=f32)

    gates = jax.nn.sigmoid(zg_ref[0, 0])
    for g in range(NSA_GROUP):
        r = slice(g * tq, (g + 1) * tq)
        o_ref[0, :, g * NSA_HD:(g + 1) * NSA_HD] = (gates[:, 3 * g:3 * g + 1] * o_cmp[r]
                                                    + gates[:, 3 * g + 1:3 * g + 2] * o_sel[r]
                                                    + gates[:, 3 * g + 2:3 * g + 3] * o_win[r])


def nsa_prompt_attention(q, kc, vc, k_slc, v_slc, k_win, v_win, zg):
    B, S = q.shape[:2]
    bf16 = jnp.bfloat16
    assert S % NSA_CK == 0 and S % NSA_TQ == 0 and WINDOW % NSA_TQ == 0 and WINDOW + NSA_TQ <= S
    n_sel = S // SEL_BLOCK
    nb = kc.shape[1]
    assert nb <= 128

    def kv_major(a):
        return a.transpose(0, 2, 1, 3).astype(bf16)

    def pad_blocks(a):
        return jnp.pad(kv_major(a), ((0, 0), (0, 0), (0, 128 - nb), (0, 0)))

    qs = (q * ATTN_SCALE).reshape(B, S, NSA_QW).astype(bf16)
    zg4 = zg.reshape(B, S, NSA_KV_HEADS, 3 * NSA_GROUP).transpose(0, 2, 1, 3)
    c = np.arange(128)[:, None]
    jj = np.arange(n_sel)[None, :]
    ratio = SEL_BLOCK // CMP_STRIDE
    msel = ((c >= jj * ratio - (CMP_BLOCK // CMP_STRIDE - 1)) & (c <= jj * ratio + ratio - 1) & (c < nb))
    expand = (np.arange(S)[None, :] // SEL_BLOCK == np.arange(n_sel)[:, None])
    expand = expand.reshape(n_sel, S // NSA_CK, NSA_CK).transpose(1, 0, 2)
    row_spec = pl.BlockSpec((1, 1, S, NSA_HD), lambda b, n, i: (b, n, 0, 0))
    blk_spec = pl.BlockSpec((1, 1, 128, NSA_HD), lambda b, n, i: (b, n, 0, 0))
    return pl.pallas_call(
        _nsa_prompt_kernel,
        out_shape=jax.ShapeDtypeStruct((B, S, NSA_QW), jnp.float32),
        grid=(B, NSA_KV_HEADS, S // NSA_TQ),
        in_specs=[pl.BlockSpec((1, NSA_TQ, NSA_GROUP * NSA_HD), lambda b, n, i: (b, i, n)),
                  blk_spec, blk_spec, row_spec, row_spec, row_spec, row_spec,
                  pl.BlockSpec((1, 1, NSA_TQ, 3 * NSA_GROUP), lambda b, n, i: (b, n, i, 0)),
                  pl.BlockSpec((128, n_sel), lambda b, n, i: (0, 0)),
                  pl.BlockSpec((S // NSA_CK, n_sel, NSA_CK), lambda b, n, i: (0, 0, 0))],
        out_specs=pl.BlockSpec((1, NSA_TQ, NSA_GROUP * NSA_HD), lambda b, n, i: (b, i, n)),
        compiler_params=pltpu.CompilerParams(dimension_semantics=("arbitrary", "arbitrary", "arbitrary"),
                                             vmem_limit_bytes=VMEM_LIMIT_BYTES),
        name="nsa_prompt_attention",
    )(qs, pad_blocks(kc), pad_blocks(vc), kv_major(k_slc), kv_major(v_slc), kv_major(k_win), kv_major(v_win),
      zg4, jnp.asarray(msel, bf16), jnp.asarray(expand, bf16))


MLSTM_L = 128
CONV_HALO = 8


def _log_sigmoid(x):
    return -(jnp.maximum(-x, 0.0) + jnp.log1p(jnp.exp(-jnp.abs(x))))


def _mlstm_prompt_kernel(x_ref, xprev_ref, halo0_ref, v_ref, o_ref, gcol_ref, grow_ref, cw_ref, cb_ref,
                         out_ref, c_out, n_out, m_out, c_ref, n_ref, m_ref):
    f32, bf16 = jnp.float32, jnp.bfloat16
    c = pl.program_id(1)
    L = MLSTM_L

    @pl.when(c == 0)
    def _():
        c_ref[...] = jnp.zeros_like(c_ref)
        n_ref[...] = jnp.zeros_like(n_ref)
        m_ref[...] = jnp.zeros_like(m_ref)

    x = x_ref[0]
    halo = jnp.where(c == 0, halo0_ref[0], xprev_ref[0, L - CONV_HALO:L, :])
    ext = jnp.concatenate([halo, x], axis=0)
    conv = cb_ref[...]
    for j in range(CONV_W):
        o = CONV_HALO - (CONV_W - 1) + j
        conv = conv + ext[o:o + L] * cw_ref[j:j + 1, :]
    qk = conv * jax.nn.sigmoid(conv)

    t_id = lax.broadcasted_iota(jnp.int32, (L, L), 0)
    s_id = lax.broadcasted_iota(jnp.int32, (L, L), 1)
    causal = t_id >= s_id
    gcol = gcol_ref[0, 0]
    grow = grow_ref[0, 0]
    for h in range(M_HEADS):
        hd = slice(h * M_HD, (h + 1) * M_HD)
        q = qk[:, hd]
        k = qk[:, M_W + h * M_HD:M_W + (h + 1) * M_HD] * (M_HD ** -0.5)
        v = v_ref[0, :, hd].astype(bf16)
        ig_r = grow[h:h + 1, :]
        ig_c = gcol[:, h:h + 1]
        lf_r = _log_sigmoid(grow[M_HEADS + h:M_HEADS + h + 1, :])
        lf_c = _log_sigmoid(gcol[:, M_HEADS + h:M_HEADS + h + 1])
        b_c = jnp.sum(jnp.where(causal, lf_r, 0.0), axis=1, keepdims=True)
        b_r = jnp.sum(jnp.where(t_id <= s_id, lf_c, 0.0), axis=0, keepdims=True)
        m_prev = m_ref[h]
        dmat = jnp.where(causal, b_c - b_r + ig_r, -jnp.inf)
        inter = b_c + m_prev
        m_t = jnp.maximum(inter, jnp.max(dmat, axis=1, keepdims=True))
        w_intra = jnp.exp(dmat - m_t)
        w_inter = jnp.exp(inter - m_t)
        qb = q.astype(bf16)
        s = lax.dot_general(qb, k.astype(bf16), _NT, preferred_element_type=f32) * w_intra
        num = (jnp.dot(s.astype(bf16), v, preferred_element_type=f32)
               + w_inter * jnp.dot(qb, c_ref[h].astype(bf16), preferred_element_type=f32))
        den = jnp.sum(s, axis=1, keepdims=True) + w_inter * jnp.sum(q * n_ref[h], axis=1, keepdims=True)
        hh = num / jnp.maximum(jnp.abs(den), jnp.exp(-m_t))
        out_ref[0, :, hd] = jax.nn.sigmoid(o_ref[0, :, hd]) * hh
        m_new = m_t[L - 1:L]
        b_last = b_c[L - 1:L]
        w_s = jnp.exp(b_last - b_c + ig_c - m_new)
        w_p = jnp.exp(b_last + m_prev - m_new)
        kw = k * w_s
        c_ref[h] = w_p * c_ref[h] + jnp.dot(kw.T.astype(bf16), v, preferred_element_type=f32)
        n_ref[h] = w_p * n_ref[h] + jnp.sum(kw, axis=0, keepdims=True)
        m_ref[h] = m_new

    @pl.when(c == pl.num_programs(1) - 1)
    def _():
        c_out[0] = c_ref[...]
        n_out[0] = n_ref[...]
        m_out[0] = m_ref[...]


def mlstm_prompt(zqk, zv, zo, zif, conv_w, conv_b, b_if):
    B, T, _ = zqk.shape
    L = MLSTM_L
    assert T % L == 0
    nc = T // L
    f32 = jnp.float32
    gif = zif + b_if
    gcol = gif.reshape(B, nc, L, 2 * M_HEADS)
    grow = gcol.transpose(0, 1, 3, 2)
    halo0 = jnp.zeros((B, CONV_HALO, 2 * M_W), f32)
    out, C, n, m = pl.pallas_call(
        _mlstm_prompt_kernel,
        out_shape=(jax.ShapeDtypeStruct((B, T, M_W), f32),
                   jax.ShapeDtypeStruct((B, M_HEADS, M_HD, M_HD), f32),
                   jax.ShapeDtypeStruct((B, M_HEADS, 1, M_HD), f32),
                   jax.ShapeDtypeStruct((B, M_HEADS, 1, 1), f32)),
        grid=(B, nc),
        in_specs=[pl.BlockSpec((1, L, 2 * M_W), lambda b, c: (b, c, 0)),
                  pl.BlockSpec((1, L, 2 * M_W), lambda b, c: (b, jnp.maximum(c - 1, 0), 0)),
                  pl.BlockSpec((1, CONV_HALO, 2 * M_W), lambda b, c: (b, 0, 0)),
                  pl.BlockSpec((1, L, M_W), lambda b, c: (b, c, 0)),
                  pl.BlockSpec((1, L, M_W), lambda b, c: (b, c, 0)),
                  pl.BlockSpec((1, 1, L, 2 * M_HEADS), lambda b, c: (b, c, 0, 0)),
                  pl.BlockSpec((1, 1, 2 * M_HEADS, L), lambda b, c: (b, c, 0, 0)),
                  pl.BlockSpec((CONV_W, 2 * M_W), lambda b, c: (0, 0)),
                  pl.BlockSpec((1, 2 * M_W), lambda b, c: (0, 0))],
        out_specs=(pl.BlockSpec((1, L, M_W), lambda b, c: (b, c, 0)),
                   pl.BlockSpec((1, M_HEADS, M_HD, M_HD), lambda b, c: (b, 0, 0, 0)),
                   pl.BlockSpec((1, M_HEADS, 1, M_HD), lambda b, c: (b, 0, 0, 0)),
                   pl.BlockSpec((1, M_HEADS, 1, 1), lambda b, c: (b, 0, 0, 0))),
        scratch_shapes=[pltpu.VMEM((M_HEADS, M_HD, M_HD), f32), pltpu.VMEM((M_HEADS, 1, M_HD), f32),
                        pltpu.VMEM((M_HEADS, 1, 1), f32)],
        compiler_params=pltpu.CompilerParams(dimension_semantics=("arbitrary", "arbitrary"),
                                             vmem_limit_bytes=VMEM_LIMIT_BYTES),
        name="mlstm_prompt",
    )(zqk, zqk, halo0, zv, zo, gcol, grow, conv_w, conv_b[None])
    return out, C, n.reshape(B, M_HEADS, M_HD), m.reshape(B, M_HEADS)


def mlstm_chunk(carry, inp):
    C, n, m = carry
    q, k, v, ig, lf = inp
    L = q.shape[2]
    b = jnp.cumsum(lf, axis=-1)
    causal = jnp.tril(jnp.ones((L, L), dtype=bool))
    dmat = jnp.where(causal, b[..., :, None] - b[..., None, :] + ig[..., None, :], -jnp.inf)
    inter = b + m[..., None]
    m_t = jnp.maximum(inter, dmat.max(axis=-1))
    w_intra = jnp.exp(dmat - m_t[..., None])
    w_inter = jnp.exp(inter - m_t)
    s = jnp.einsum('bhtd,bhsd->bhts', q, k) * w_intra
    num = jnp.einsum('bhts,bhsv->bhtv', s, v) + w_inter[..., None] * jnp.einsum('bhtd,bhdv->bhtv', q, C)
    den = s.sum(-1) + w_inter * jnp.einsum('bhtd,bhd->bht', q, n)
    h = num / jnp.maximum(jnp.abs(den), jnp.exp(-m_t))[..., None]
    m_new = m_t[..., -1]
    w_s = jnp.exp(b[..., -1:] - b + ig - m_new[..., None])
    w_p = jnp.exp(b[..., -1] + m - m_new)
    C_new = w_p[..., None, None] * C + jnp.einsum('bhs,bhsd,bhsv->bhdv', w_s, k, v)
    n_new = w_p[..., None] * n + jnp.einsum('bhs,bhsd->bhd', w_s, k)
    return (C_new, n_new, m_new), h


def mlstm_mix(zqk, zv, zo, zif, buf0, C0, n0, m0, conv_w, conv_b, b_if, chunk):
    B, T, _ = zqk.shape
    full = jnp.concatenate([buf0, zqk], axis=1)
    conv = conv_b
    for j in range(CONV_W):
        conv = conv + full[:, j:j + T] * conv_w[j]
    qk = jax.nn.silu(conv)

    def heads(a):
        return a.reshape(B, T, M_HEADS, M_HD).transpose(0, 2, 1, 3)

    q = heads(qk[..., :M_W])
    k = heads(qk[..., M_W:]) * (M_HD ** -0.5)
    v = heads(zv)
    gif = zif + b_if
    ig = gif[..., :M_HEADS].transpose(0, 2, 1)
    lf = jax.nn.log_sigmoid(gif[..., M_HEADS:]).transpose(0, 2, 1)
    nc = T // chunk

    def to_chunks(a):
        return jnp.moveaxis(a.reshape(B, M_HEADS, nc, chunk, *a.shape[3:]), 2, 0)

    (C, n, m), h = lax.scan(mlstm_chunk, (C0, n0, m0),
                            (to_chunks(q), to_chunks(k), to_chunks(v), to_chunks(ig), to_chunks(lf)))
    h = jnp.moveaxis(h, 0, 2).reshape(B, M_HEADS, T, M_HD).transpose(0, 2, 1, 3).reshape(B, T, M_W)
    out = jax.nn.sigmoid(zo) * h
    return out, (C, n, m, full[:, T:])


PEER_COMBOS = 2 * PEER_HEADS
PEER_KEY_ROWS = 8
PEER_TILE = PEER_KEY_ROWS * N_KEYS
PEER_TS_ROWS = 24
LANES = 128
_NT = (((1,), (1,)), ((), ()))


def _peer_topk_kernel(q_ref, keys_ref, s_ref, e0_ref, e1_ref, tau_ref, ts_ref):
    c = pl.program_id(1)
    tt = q_ref.shape[0]
    s = lax.dot_general(keys_ref[0], q_ref[...].astype(jnp.bfloat16), _NT,
                        preferred_element_type=jnp.float32)
    s_ref[c] = s
    key_id = lax.broadcasted_iota(jnp.int32, s.shape, 0)
    work = s
    rows = []
    for _ in range(PEER_TOPK + 1):
        m = jnp.max(work, axis=0, keepdims=True)
        first = jnp.min(jnp.where(work == m, key_id, N_KEYS), axis=0, keepdims=True)
        work = jnp.where(key_id == first, -jnp.inf, work)
        rows.append(m)
    rows.append(jnp.full((PEER_TS_ROWS - PEER_TOPK - 1, tt), -jnp.inf, jnp.float32))
    ts_ref[c] = jnp.concatenate(rows, axis=0)

    @pl.when(c == PEER_COMBOS - 1)
    def _():
        for h in range(PEER_HEADS):
            t0 = ts_ref[2 * h]
            t1 = ts_ref[2 * h + 1]
            pieces = [t0[0:1] + t1] + [t0[a:a + 1] + t1[0:8] for a in range(1, 8)] + [t0[8:24] + t1[0:1]]
            cand = jnp.concatenate(pieces, axis=0)
            top = t0[0:1] + t1[0:1]
            v16 = top
            v17 = top
            z = jnp.zeros_like(top)
            seen = jnp.zeros_like(top)
            for _ in range(PEER_TOPK + 1):
                m = jnp.max(cand, axis=0, keepdims=True)
                eq = cand == m
                cnt = jnp.sum(jnp.where(eq, 1.0, 0.0), axis=0, keepdims=True)
                active = seen < PEER_TOPK
                take = jnp.minimum(cnt, PEER_TOPK - seen)
                v16 = jnp.where(active, m, v16)
                v17 = jnp.where(seen < PEER_TOPK + 1, m, v17)
                z = z + jnp.where(active, take * jnp.exp(m - top), 0.0)
                seen = seen + cnt
                cand = jnp.where(eq, -jnp.inf, cand)
            tau_ref[h:h + 1, :] = 0.5 * v16 + 0.5 * v17
            e0_ref[h] = jnp.exp(s_ref[2 * h] - t0[0:1]) / z
            e1_ref[h] = jnp.exp(s_ref[2 * h + 1] - t1[0:1])


def peer_scores(q, sub_keys, tt):
    n = q.shape[0]
    assert n % tt == 0
    keys = sub_keys.reshape(PEER_COMBOS, N_KEYS, PEER_QDIM // 2).astype(jnp.bfloat16)
    f32 = jnp.float32
    return pl.pallas_call(
        _peer_topk_kernel,
        out_shape=(jax.ShapeDtypeStruct((PEER_COMBOS, N_KEYS, n), f32),
                   jax.ShapeDtypeStruct((PEER_HEADS, N_KEYS, n), f32),
                   jax.ShapeDtypeStruct((PEER_HEADS, N_KEYS, n), f32),
                   jax.ShapeDtypeStruct((PEER_HEADS, n), f32)),
        grid=(n // tt, PEER_COMBOS),
        in_specs=[pl.BlockSpec((tt, PEER_QDIM // 2), lambda i, c: (i, c)),
                  pl.BlockSpec((1, N_KEYS, PEER_QDIM // 2), lambda i, c: (c, 0, 0))],
        out_specs=(pl.BlockSpec((PEER_COMBOS, N_KEYS, tt), lambda i, c: (0, 0, i)),
                   pl.BlockSpec((PEER_HEADS, N_KEYS, tt), lambda i, c: (0, 0, i)),
                   pl.BlockSpec((PEER_HEADS, N_KEYS, tt), lambda i, c: (0, 0, i)),
                   pl.BlockSpec((PEER_HEADS, tt), lambda i, c: (0, i))),
        scratch_shapes=[pltpu.VMEM((PEER_COMBOS, PEER_TS_ROWS, tt), f32)],
        compiler_params=pltpu.CompilerParams(dimension_semantics=("arbitrary", "arbitrary"),
                                             vmem_limit_bytes=VMEM_LIMIT_BYTES),
        name="peer_topk",
    )(q, keys)


def _peer_dense_kernel(xb_ref, h_ref, u_ref, vt_ref, s0_ref, ez_ref, s_ref, e1_ref, tau_ref, g_ref, b_ref,
                       o_ref, acc_ref, a_ref, w_ref):
    e = pl.program_id(1)
    tt = xb_ref.shape[0]

    @pl.when(e == 0)
    def _():
        acc_ref[...] = jnp.zeros_like(acc_ref)

    a_ref[...] = lax.dot_general(u_ref[...], xb_ref[...], _NT, preferred_element_type=jnp.float32)
    for r in range(PEER_KEY_ROWS):
        rows = slice(r * N_KEYS, (r + 1) * N_KEYS)
        for t in range(tt // LANES):
            tok = slice(t * LANES, (t + 1) * LANES)
            gate = jnp.zeros((N_KEYS, LANES), jnp.float32)
            for h in range(PEER_HEADS):
                need = tau_ref[h:h + 1, tok] - s0_ref[2 * h, r:r + 1, tok]
                picked = jnp.where(s_ref[2 * h + 1, :, tok] >= need, e1_ref[h, :, tok], 0.0)
                gate = gate + picked * ez_ref[h, r:r + 1, tok]
            ar = a_ref[rows, tok]
            act = 0.5 * ar * (1.0 + lax.erf(ar * (2.0 ** -0.5)))
            w_ref[rows, tok] = (gate * act).astype(jnp.bfloat16)
    acc_ref[...] += jnp.dot(vt_ref[...], w_ref[...], preferred_element_type=jnp.float32)

    @pl.when(e == pl.num_programs(1) - 1)
    def _():
        r = ALPHA * h_ref[...] + acc_ref[...].T
        mu = jnp.mean(r, axis=-1, keepdims=True)
        d = r - mu
        var = jnp.mean(d * d, axis=-1, keepdims=True)
        o_ref[...] = d * lax.rsqrt(var + LN_EPS) * g_ref[...] + b_ref[...]


def peer_tail(h, q, sub_keys, u_bf, vt_bf, ln_g, ln_b, tt):
    n, d = h.shape
    s, e0z, e1, tau = peer_scores(q, sub_keys, tt)
    n_exp = u_bf.shape[0]
    return pl.pallas_call(
        _peer_dense_kernel,
        out_shape=jax.ShapeDtypeStruct((n, d), jnp.float32),
        grid=(n // tt, n_exp // PEER_TILE),
        in_specs=[pl.BlockSpec((tt, d), lambda i, e: (i, 0)),
                  pl.BlockSpec((tt, d), lambda i, e: (i, 0)),
                  pl.BlockSpec((PEER_TILE, d), lambda i, e: (e, 0)),
                  pl.BlockSpec((d, PEER_TILE), lambda i, e: (0, e)),
                  pl.BlockSpec((PEER_COMBOS, PEER_KEY_ROWS, tt), lambda i, e: (0, e, i)),
                  pl.BlockSpec((PEER_HEADS, PEER_KEY_ROWS, tt), lambda i, e: (0, e, i)),
                  pl.BlockSpec((PEER_COMBOS, N_KEYS, tt), lambda i, e: (0, 0, i)),
                  pl.BlockSpec((PEER_HEADS, N_KEYS, tt), lambda i, e: (0, 0, i)),
                  pl.BlockSpec((PEER_HEADS, tt), lambda i, e: (0, i)),
                  pl.BlockSpec((1, d), lambda i, e: (0, 0)),
                  pl.BlockSpec((1, d), lambda i, e: (0, 0))],
        out_specs=pl.BlockSpec((tt, d), lambda i, e: (i, 0)),
        scratch_shapes=[pltpu.VMEM((d, tt), jnp.float32), pltpu.VMEM((PEER_TILE, tt), jnp.float32),
                        pltpu.VMEM((PEER_TILE, tt), jnp.bfloat16)],
        compiler_params=pltpu.CompilerParams(dimension_semantics=("arbitrary", "arbitrary"),
                                             vmem_limit_bytes=VMEM_LIMIT_BYTES),
        name="peer_dense",
    )(h.astype(jnp.bfloat16), h, u_bf, vt_bf, s, e0z, s, e1, tau, ln_g[None], ln_b[None])


def block_tail(x, mix, w_out, ln_g, ln_b, w_pq, sub_keys, u_bf, vt_bf, tt):
    lead = x.shape[:-1]
    h = layer_norm(ALPHA * x + mm3(mix, w_out), ln_g[0], ln_b[0]).reshape(-1, D_MODEL)
    q = pallas_matmul(h, w_pq)
    return peer_tail(h, q, sub_keys, u_bf, vt_bf, ln_g[1], ln_b[1], tt).reshape(*lead, D_MODEL)


def prompt_mix(x, w_in, pe, w1, b1, w2, conv_w, conv_b, b_if):
    B, S, _ = x.shape
    pos = jnp.arange(S)
    zq, zkv, zg, zqk, zv, zo, zif = split_in_proj(x, w_in)
    q, (k_cmp, v_cmp, k_slc, v_slc, k_win, v_win), gates = nsa_project(zq, zkv, zg, pos)
    chunk_w = CMP_STRIDE * NSA_KVW

    def compressed(rows, c):
        proj = chunk_projection(rows.reshape(B * (S // CMP_STRIDE), chunk_w), w1[c])
        return compress_from_projection(proj.reshape(B, S // CMP_STRIDE, -1), pe[c], w1[c], b1[c], w2[c])

    kc = compressed(k_cmp, 0)
    vc = compressed(v_cmp, 1)
    o_nsa = nsa_prompt_attention(q, kc, vc, k_slc, v_slc, k_win, v_win, zg)
    o_m, C, n, m = mlstm_prompt(zqk, zv, zo, zif, conv_w, conv_b, b_if)
    buf = zqk[:, S - (CONV_W - 1):]
    wl = min(WINDOW, S)
    mix = jnp.concatenate([o_nsa, o_m], axis=-1)
    return mix, (k_cmp, v_cmp, k_slc, v_slc, k_win[:, S - wl:], v_win[:, S - wl:], C, n, m, buf)


def sample_mix(x, kc_pool, vc_pool, ks_pool, vs_pool, kw_buf, vw_buf, C0, n0, m0, buf0, page_table,
               w_in, pe, w1, b1, w2, conv_w, conv_b, b_if):
    B, T, _ = x.shape
    past = page_table.shape[1] * PAGE_SIZE
    pos = past + jnp.arange(T)
    zq, zkv, zg, zqk, zv, zo, zif = split_in_proj(x, w_in)
    q, (k_cmp, v_cmp, k_slc, v_slc, k_win, v_win), gates = nsa_project(zq, zkv, zg, pos)

    def with_past(pool, new):
        old = pool[page_table].reshape(B, past, NSA_KV_HEADS, NSA_HD)
        return jnp.concatenate([old, new], axis=1)

    assert (past + T) // CMP_STRIDE == past // CMP_STRIDE

    def compressed(pool, c):
        rows = pool[page_table].reshape(B * (past // CMP_STRIDE), CMP_STRIDE * NSA_KVW)
        proj = chunk_projection(rows, w1[c])
        return compress_from_projection(proj.reshape(B, past // CMP_STRIDE, -1), pe[c], w1[c], b1[c], w2[c])

    o_cmp, p = cmp_attend(q, pos, compressed(kc_pool, 0), compressed(vc_pool, 1))
    n_sel = -(-(past + T) // SEL_BLOCK)
    sel = select_blocks(p, pos, n_sel)
    kb = to_blocks(with_past(ks_pool, k_slc), n_sel)
    vb = to_blocks(with_past(vs_pool, v_slc), n_sel)
    wb = kw_buf.shape[1]
    kw = jnp.concatenate([kw_buf, k_win], axis=1)
    vw = jnp.concatenate([vw_buf, v_win], axis=1)
    kpos = past - wb + jnp.arange(wb + T)
    o_sel = sel_attend(q, pos, sel, kb, vb)
    o_win = win_attend(q, pos, kw, vw, kpos)
    o_nsa = nsa_combine(gates, o_cmp, o_sel, o_win)
    o_m, (C, n, m, buf) = mlstm_mix(zqk, zv, zo, zif, buf0, C0, n0, m0, conv_w, conv_b, b_if, T)
    mix = jnp.concatenate([o_nsa, o_m], axis=-1)
    return mix, (k_cmp, v_cmp, k_slc, v_slc, kw[:, T:], vw[:, T:], C, n, m, buf)


def kernel(x_prompt, x_sample, cache_k_cmp, cache_v_cmp, cache_k_slc, cache_v_slc, cache_k_win, cache_v_win,
           state_C, state_n, state_m, state_conv, page_table, w_in, w_out, w_phi1, b_phi1, w_phi2, pe_cmp,
           conv_w, conv_b, b_if, ln_g, ln_b, w_pq, sub_keys, u_tab, v_tab):
    l = 0
    mix_p, st_p = prompt_mix(x_prompt, w_in[l], pe_cmp[l], w_phi1[l], b_phi1[l], w_phi2[l],
                             conv_w[l], conv_b[l], b_if[l])
    mix_s, st_s = sample_mix(x_sample, cache_k_cmp[l], cache_v_cmp[l], cache_k_slc[l], cache_v_slc[l],
                             cache_k_win[l], cache_v_win[l], state_C[l], state_n[l], state_m[l],
                             state_conv[l], page_table, w_in[l], pe_cmp[l], w_phi1[l], b_phi1[l],
                             w_phi2[l], conv_w[l], conv_b[l], b_if[l])
    u_bf = u_tab[l].astype(jnp.bfloat16)
    vt_bf = v_tab[l].astype(jnp.bfloat16).T
    xp = block_tail(x_prompt, mix_p, w_out[l], ln_g[l], ln_b[l], w_pq[l], sub_keys[l], u_bf, vt_bf, 512)
    xs = block_tail(x_sample, mix_s, w_out[l], ln_g[l], ln_b[l], w_pq[l], sub_keys[l], u_bf, vt_bf, 128)
    return (xp, xs) + tuple(a[None] for a in st_p) + tuple(a[None] for a in st_s)
```

```python
import jax
import jax.numpy as jnp
import numpy as np
from jax import lax
from jax.experimental import pallas as pl
from jax.experimental.pallas import tpu as pltpu

D_MODEL = 1024
DEPTH = 1
PAGE_SIZE = 128
NSA_HEADS = 8
NSA_KV_HEADS = 2
NSA_GROUP = NSA_HEADS // NSA_KV_HEADS
NSA_HD = 64
NSA_QW = NSA_HEADS * NSA_HD
NSA_KVW = NSA_KV_HEADS * NSA_HD
CMP_BLOCK = 32
CMP_STRIDE = 16
SEL_BLOCK = 64
SEL_TOP = 16
WINDOW = 512
Q_BLOCK = 64
ATTN_SCALE = NSA_HD ** -0.5
ROPE_THETA = 10000.0
M_HEADS = 4
M_HD = 128
M_W = M_HEADS * M_HD
M_CHUNK = 64
CONV_W = 4
PEER_HEADS = 8
N_KEYS = 128
PEER_TOPK = 16
PEER_QDIM = 256
PEER_BLOCK = 128
IN_SPLITS = (NSA_QW, 6 * NSA_KVW, 3 * NSA_HEADS, 2 * M_W, M_W, M_W, 2 * M_HEADS)
LN_EPS = 1e-5
ALPHA = (2 * DEPTH) ** 0.25

VMEM_LIMIT_BYTES = 56 * 1024 * 1024


def _mm_kernel(x_ref, w_ref, o_ref):
    o_ref[...] = jnp.dot(x_ref[...].astype(jnp.bfloat16), w_ref[...], preferred_element_type=jnp.float32)


def pallas_matmul(x, w, tm=512):
    M, K = x.shape
    N = w.shape[1]
    tm = min(tm, M)
    assert M % tm == 0
    return pl.pallas_call(
        _mm_kernel,
        out_shape=jax.ShapeDtypeStruct((M, N), jnp.float32),
        grid=(M // tm,),
        in_specs=[pl.BlockSpec((tm, K), lambda i: (i, 0)), pl.BlockSpec((K, N), lambda i: (0, 0))],
        out_specs=pl.BlockSpec((tm, N), lambda i: (i, 0)),
        compiler_params=pltpu.CompilerParams(dimension_semantics=("arbitrary",),
                                             vmem_limit_bytes=VMEM_LIMIT_BYTES),
        name="proj_matmul",
    )(x, w.astype(jnp.bfloat16))


def mm3(x, w):
    lead = x.shape[:-1]
    return pallas_matmul(x.reshape(-1, x.shape[-1]), w).reshape(*lead, w.shape[1])


def layer_norm(x, g, b):
    mu = x.mean(-1, keepdims=True)
    var = jnp.square(x - mu).mean(-1, keepdims=True)
    return (x - mu) * lax.rsqrt(var + LN_EPS) * g + b


def rope(x, pos):
    half = x.shape[-1] // 2
    inv = ROPE_THETA ** (-jnp.arange(half, dtype=jnp.float32) / half)
    ang = pos.astype(jnp.float32)[:, None] * inv[None, :]
    cos = jnp.cos(ang)[:, None, :]
    sin = jnp.sin(ang)[:, None, :]
    x1, x2 = x[..., :half], x[..., half:]
    return jnp.concatenate([x1 * cos - x2 * sin, x2 * cos + x1 * sin], axis=-1)


def split_in_proj(x, w_in):
    z = mm3(x, w_in)
    cuts = [int(c) for c in np.cumsum(IN_SPLITS)[:-1]]
    return jnp.split(z, cuts, axis=-1)


def nsa_project(zq, zkv, zg, pos):
    B, T, _ = zq.shape
    q = rope(zq.reshape(B, T, NSA_HEADS, NSA_HD), pos)
    kv = zkv.reshape(B, T, 6, NSA_KV_HEADS, NSA_HD)
    rows = (rope(kv[:, :, 0], pos), kv[:, :, 1], rope(kv[:, :, 2], pos), kv[:, :, 3],
            rope(kv[:, :, 4], pos), kv[:, :, 5])
    gates = jax.nn.sigmoid(zg).reshape(B, T, NSA_HEADS, 3)
    return q, rows, gates


def chunk_projection(chunks, w1):
    assert CMP_BLOCK == 2 * CMP_STRIDE
    return pallas_matmul(chunks, _expanded_w1(w1))


def _expanded_w1(w1):
    w1r = w1.reshape(2, CMP_STRIDE, NSA_HD, w1.shape[-1])
    wbig = jnp.einsum('hpdf,kn->pkdnhf', w1r, jnp.eye(NSA_KV_HEADS, dtype=w1.dtype))
    return wbig.reshape(CMP_STRIDE * NSA_KVW, 2 * NSA_KV_HEADS * w1.shape[-1])


def _page_projection_kernel(pg_ref, w_ref, o_ref, x_ref, t_ref):
    n_pages = pg_ref.shape[1]
    per_page = PAGE_SIZE // CMP_STRIDE

    def place(g, carry):
        t_ref[...] = pg_ref[0, g].reshape(NSA_KVW, PAGE_SIZE).T
        row0 = pl.multiple_of(g * per_page, per_page)
        for p in range(CMP_STRIDE):
            x_ref[pl.ds(row0, per_page), p * NSA_KVW:(p + 1) * NSA_KVW] = t_ref[pl.ds(p, per_page, stride=CMP_STRIDE), :]
        return carry

    lax.fori_loop(0, n_pages, place, 0)
    o_ref[0] = jnp.dot(x_ref[...].astype(jnp.bfloat16), w_ref[...], preferred_element_type=jnp.float32)


def page_projection(pages, w1):
    B, n_pages = pages.shape[:2]
    assert pages.shape[2:] == (NSA_KV_HEADS, NSA_HD, PAGE_SIZE) and NSA_KVW == LANES and PAGE_SIZE == LANES
    wbig = _expanded_w1(w1).astype(jnp.bfloat16)
    rows = n_pages * (PAGE_SIZE // CMP_STRIDE)
    return pl.pallas_call(
        _page_projection_kernel,
        out_shape=jax.ShapeDtypeStruct((B, rows, wbig.shape[1]), jnp.float32),
        grid=(B,),
        in_specs=[pl.BlockSpec((1, n_pages, NSA_KV_HEADS, NSA_HD, PAGE_SIZE), lambda b: (b, 0, 0, 0, 0)),
                  pl.BlockSpec(wbig.shape, lambda b: (0, 0))],
        out_specs=pl.BlockSpec((1, rows, wbig.shape[1]), lambda b: (b, 0, 0)),
        scratch_shapes=[pltpu.VMEM((rows, wbig.shape[0]), jnp.float32),
                        pltpu.VMEM((PAGE_SIZE, NSA_KVW), jnp.float32)],
        compiler_params=pltpu.CompilerParams(dimension_semantics=("arbitrary",),
                                             vmem_limit_bytes=VMEM_LIMIT_BYTES),
        name="page_projection",
    )(pages, wbig)


def compress_from_projection(proj, pe, w1, b1, w2):
    f = w1.shape[-1]
    bias = jnp.dot(pe.reshape(-1), w1, precision=lax.Precision.HIGHEST) + b1
    heads = []
    for n in range(NSA_KV_HEADS):
        first = proj[:, :-1, 2 * n * f:(2 * n + 1) * f]
        second = proj[:, 1:, (2 * n + 1) * f:(2 * n + 2) * f]
        heads.append(jax.nn.gelu(first + second + bias, approximate=False) @ w2)
    return jnp.stack(heads, axis=2)


def cmp_attend(q, qpos, kc, vc):
    B, T = q.shape[:2]
    qg = q.reshape(B, T, NSA_KV_HEADS, NSA_GROUP, NSA_HD)
    s = jnp.einsum('btngd,bcnd->btngc', qg, kc) * ATTN_SCALE
    nblk = kc.shape[1]
    blk_end = jnp.arange(nblk) * CMP_STRIDE + CMP_BLOCK - 1
    valid = (blk_end[None, :] <= qpos[:, None])[None, :, None, None, :]
    p = jax.nn.softmax(jnp.where(valid, s, -1e30), axis=-1) * valid
    o = jnp.einsum('btngc,bcnd->btngd', p, vc)
    return o.reshape(B, T, NSA_HEADS, NSA_HD), p


def select_blocks(p, qpos, n_sel):
    imp = p.sum(axis=3)
    R = SEL_BLOCK // CMP_STRIDE
    r = CMP_BLOCK // CMP_STRIDE
    nb = imp.shape[-1]
    right = n_sel * R + R - 1 - nb
    padded = jnp.pad(imp, ((0, 0), (0, 0), (0, 0), (r - 1, right)))
    score = padded[..., 0:(n_sel - 1) * R + 1:R]
    for o in range(1, R + r - 1):
        score = score + padded[..., o:o + (n_sel - 1) * R + 1:R]
    j = jnp.arange(n_sel)[None, :]
    cur = (qpos // SEL_BLOCK)[:, None]
    valid = (j * SEL_BLOCK <= qpos[:, None])[None, :, None, :]
    forced = ((j == 0) | (j == cur) | (j == cur - 1))[None, :, None, :]
    score = jnp.where(forced, jnp.inf, jnp.where(valid, score, -jnp.inf))
    idx = j[0]
    before = (score[..., None, :] > score[..., :, None]) | ((score[..., None, :] == score[..., :, None])
                                                          & (idx[None, :] < idx[:, None]))
    rank = before.sum(-1)
    hit = rank[..., None, :] == jnp.arange(min(SEL_TOP, n_sel))[:, None]
    return (hit * idx).sum(-1)


def to_blocks(rows, n_sel):
    B, L, KV, hd = rows.shape
    rows = jnp.pad(rows, ((0, 0), (0, n_sel * SEL_BLOCK - L), (0, 0), (0, 0)))
    return rows.reshape(B, n_sel, SEL_BLOCK, KV, hd).transpose(0, 3, 1, 2, 4)


def take_rows(table, idx):
    return table[idx]


def sel_attend(q, qpos, sel, kb, vb):
    B, Tq = q.shape[:2]
    k = sel.shape[-1]
    sel_t = sel.transpose(0, 2, 1, 3)
    gather = jax.vmap(jax.vmap(take_rows))
    kg = gather(kb, sel_t).reshape(B, NSA_KV_HEADS, Tq, k * SEL_BLOCK, NSA_HD)
    vg = gather(vb, sel_t).reshape(B, NSA_KV_HEADS, Tq, k * SEL_BLOCK, NSA_HD)
    kpos = (sel_t[..., None] * SEL_BLOCK + jnp.arange(SEL_BLOCK)).reshape(B, NSA_KV_HEADS, Tq, k * SEL_BLOCK)
    qg = q.reshape(B, Tq, NSA_KV_HEADS, NSA_GROUP, NSA_HD).transpose(0, 2, 1, 3, 4)
    s = jnp.einsum('bntgd,bntsd->bntgs', qg, kg) * ATTN_SCALE
    mask = kpos[:, :, :, None, :] <= qpos[None, None, :, None, None]
    pr = jax.nn.softmax(jnp.where(mask, s, -jnp.inf), axis=-1)
    o = jnp.einsum('bntgs,bntsd->bntgd', pr, vg)
    return o.transpose(0, 2, 1, 3, 4).reshape(B, Tq, NSA_HEADS, NSA_HD)


def win_attend(q, qpos, k, v, kpos):
    B, Tq = q.shape[:2]
    qg = q.reshape(B, Tq, NSA_KV_HEADS, NSA_GROUP, NSA_HD)
    s = jnp.einsum('btngd,bsnd->btngs', qg, k) * ATTN_SCALE
    diff = qpos[:, None] - kpos[None, :]
    mask = ((diff >= 0) & (diff < WINDOW) & (kpos[None, :] >= 0))[None, :, None, None, :]
    pr = jax.nn.softmax(jnp.where(mask, s, -jnp.inf), axis=-1)
    o = jnp.einsum('btngs,bsnd->btngd', pr, v)
    return o.reshape(B, Tq, NSA_HEADS, NSA_HD)


def nsa_combine(gates, o_cmp, o_sel, o_win):
    B, T = gates.shape[:2]
    o = gates[..., 0:1] * o_cmp + gates[..., 1:2] * o_sel + gates[..., 2:3] * o_win
    return o.reshape(B, T, NSA_QW)


NSA_TQ = 128
NSA_CK = 512
MASKED = -1e30


def _softmax_rows(s):
    m = jnp.max(s, axis=-1, keepdims=True)
    e = jnp.exp(s - m)
    return e / jnp.sum(e, axis=-1, keepdims=True)


def _nsa_prompt_kernel(q_ref, kc_ref, vc_ref, ks_ref, vs_ref, kw_ref, vw_ref, zg_ref, msel_ref, exp_ref, o_ref):
    f32, bf16 = jnp.float32, jnp.bfloat16
    tq = NSA_TQ
    q0 = pl.program_id(2) * tq
    qb = q_ref[0]
    qs = jnp.concatenate([qb[:, g * NSA_HD:(g + 1) * NSA_HD] for g in range(NSA_GROUP)], axis=0)
    tpos = q0 + lax.broadcasted_iota(jnp.int32, (tq, 1), 0)

    def per_head(a):
        return jnp.concatenate([a] * NSA_GROUP, axis=0)

    s = lax.dot_general(qs, kc_ref[0, 0], _NT, preferred_element_type=f32)
    cblk = lax.broadcasted_iota(jnp.int32, (tq, 128), 1)
    cvalid = cblk * CMP_STRIDE + (CMP_BLOCK - 1) <= tpos
    s = s + per_head(jnp.where(cvalid, 0.0, MASKED))
    e = jnp.exp(s - jnp.max(s, axis=-1, keepdims=True)) * per_head(jnp.where(cvalid, 1.0, 0.0))
    l = jnp.sum(e, axis=-1, keepdims=True)
    p = e / jnp.where(l > 0.0, l, 1.0)
    o_cmp = jnp.dot(p.astype(bf16), vc_ref[0, 0], preferred_element_type=f32)

    imp = p[0:tq]
    for g in range(1, NSA_GROUP):
        imp = imp + p[g * tq:(g + 1) * tq]
    hi = imp.astype(bf16)
    r1 = imp - hi.astype(f32)
    mid = r1.astype(bf16)
    lo = (r1 - mid.astype(f32)).astype(bf16)
    msel = msel_ref[...]
    score = (jnp.dot(hi, msel, preferred_element_type=f32) + jnp.dot(mid, msel, preferred_element_type=f32)
             + jnp.dot(lo, msel, preferred_element_type=f32))
    n_sel = score.shape[1]
    j = lax.broadcasted_iota(jnp.int32, (tq, n_sel), 1)
    cur = tpos // SEL_BLOCK
    forced = (j == 0) | (j == cur) | (j == cur - 1)
    score = jnp.where(forced, jnp.inf, jnp.where(j * SEL_BLOCK <= tpos, score, -jnp.inf))
    rank = jnp.zeros((tq, n_sel), f32)
    for jp in range(n_sel):
        col = score[:, jp:jp + 1]
        before = (col > score) | ((col == score) & (j > jp))
        rank = rank + jnp.where(before, 1.0, 0.0)
    sel01 = jnp.where(rank < SEL_TOP, 1.0, 0.0).astype(bf16)

    ck = NSA_CK
    rows = NSA_GROUP * tq

    def sel_chunk(c, carry):
        m, l, acc = carry
        k0 = pl.multiple_of(c * ck, ck)
        s = lax.dot_general(qs, ks_ref[0, 0, pl.ds(k0, ck), :], _NT, preferred_element_type=f32)
        chosen = jnp.dot(sel01, exp_ref[c], preferred_element_type=f32)
        kpos = k0 + lax.broadcasted_iota(jnp.int32, (tq, ck), 1)
        ok = (chosen > 0.5) & (kpos <= tpos)
        s = s + per_head(jnp.where(ok, 0.0, MASKED))
        m_new = jnp.maximum(m, jnp.max(s, axis=-1, keepdims=True))
        a = jnp.exp(m - m_new)
        pr = jnp.exp(s - m_new)
        l = a * l + jnp.sum(pr, axis=-1, keepdims=True)
        acc = a * acc + jnp.dot(pr.astype(bf16), vs_ref[0, 0, pl.ds(k0, ck), :], preferred_element_type=f32)
        return m_new, l, acc

    init = (jnp.full((rows, 1), MASKED, f32), jnp.zeros((rows, 1), f32), jnp.zeros((rows, NSA_HD), f32))
    n_chunks = (q0 + tq + ck - 1) // ck
    _, l_sel, acc_sel = lax.fori_loop(0, n_chunks, sel_chunk, init)
    o_sel = acc_sel / l_sel

    w0 = pl.multiple_of(jnp.maximum(q0 - WINDOW, 0), tq)
    wl = WINDOW + tq
    s = lax.dot_general(qs, kw_ref[0, 0, pl.ds(w0, wl), :], _NT, preferred_element_type=f32)
    diff = tpos - (w0 + lax.broadcasted_iota(jnp.int32, (tq, wl), 1))
    s = s + per_head(jnp.where((diff >= 0) & (diff < WINDOW), 0.0, MASKED))
    o_win = jnp.dot(_softmax_rows(s).astype(bf16), vw_ref[0, 0, pl.ds(w0, wl), :], preferred_element_type=f32)

    gates = jax.nn.sigmoid(zg_ref[0, 0])
    for g in range(NSA_GROUP):
        r = slice(g * tq, (g + 1) * tq)
        o_ref[0, :, g * NSA_HD:(g + 1) * NSA_HD] = (gates[:, 3 * g:3 * g + 1] * o_cmp[r]
                                                    + gates[:, 3 * g + 1:3 * g + 2] * o_sel[r]
                                                    + gates[:, 3 * g + 2:3 * g + 3] * o_win[r])


def nsa_prompt_attention(q, kc, vc, k_slc, v_slc, k_win, v_win, zg):
    B, S = q.shape[:2]
    bf16 = jnp.bfloat16
    assert S % NSA_CK == 0 and S % NSA_TQ == 0 and WINDOW % NSA_TQ == 0 and WINDOW + NSA_TQ <= S
    n_sel = S // SEL_BLOCK
    nb = kc.shape[1]
    assert nb <= 128

    def kv_major(a):
        return a.transpose(0, 2, 1, 3).astype(bf16)

    def pad_blocks(a):
        return jnp.pad(kv_major(a), ((0, 0), (0, 0), (0, 128 - nb), (0, 0)))

    qs = (q * ATTN_SCALE).reshape(B, S, NSA_QW).astype(bf16)
    zg4 = zg.reshape(B, S, NSA_KV_HEADS, 3 * NSA_GROUP).transpose(0, 2, 1, 3)
    c = np.arange(128)[:, None]
    jj = np.arange(n_sel)[None, :]
    ratio = SEL_BLOCK // CMP_STRIDE
    msel = ((c >= jj * ratio - (CMP_BLOCK // CMP_STRIDE - 1)) & (c <= jj * ratio + ratio - 1) & (c < nb))
    expand = (np.arange(S)[None, :] // SEL_BLOCK == np.arange(n_sel)[:, None])
    expand = expand.reshape(n_sel, S // NSA_CK, NSA_CK).transpose(1, 0, 2)
    row_spec = pl.BlockSpec((1, 1, S, NSA_HD), lambda b, n, i: (b, n, 0, 0))
    blk_spec = pl.BlockSpec((1, 1, 128, NSA_HD), lambda b, n, i: (b, n, 0, 0))
    return pl.pallas_call(
        _nsa_prompt_kernel,
        out_shape=jax.ShapeDtypeStruct((B, S, NSA_QW), jnp.float32),
        grid=(B, NSA_KV_HEADS, S // NSA_TQ),
        in_specs=[pl.BlockSpec((1, NSA_TQ, NSA_GROUP * NSA_HD), lambda b, n, i: (b, i, n)),
                  blk_spec, blk_spec, row_spec, row_spec, row_spec, row_spec,
                  pl.BlockSpec((1, 1, NSA_TQ, 3 * NSA_GROUP), lambda b, n, i: (b, n, i, 0)),
                  pl.BlockSpec((128, n_sel), lambda b, n, i: (0, 0)),
                  pl.BlockSpec((S // NSA_CK, n_sel, NSA_CK), lambda b, n, i: (0, 0, 0))],
        out_specs=pl.BlockSpec((1, NSA_TQ, NSA_GROUP * NSA_HD), lambda b, n, i: (b, i, n)),
        compiler_params=pltpu.CompilerParams(dimension_semantics=("arbitrary", "arbitrary", "arbitrary"),
                                             vmem_limit_bytes=VMEM_LIMIT_BYTES),
        name="nsa_prompt_attention",
    )(qs, pad_blocks(kc), pad_blocks(vc), kv_major(k_slc), kv_major(v_slc), kv_major(k_win), kv_major(v_win),
      zg4, jnp.asarray(msel, bf16), jnp.asarray(expand, bf16))


MLSTM_L = 128
CONV_HALO = 8


def _log_sigmoid(x):
    return -(jnp.maximum(-x, 0.0) + jnp.log1p(jnp.exp(-jnp.abs(x))))


def _mlstm_prompt_kernel(x_ref, xprev_ref, halo0_ref, v_ref, o_ref, gcol_ref, grow_ref, cw_ref, cb_ref,
                         out_ref, c_out, n_out, m_out, c_ref, n_ref, m_ref):
    f32, bf16 = jnp.float32, jnp.bfloat16
    c = pl.program_id(1)
    L = MLSTM_L

    @pl.when(c == 0)
    def _():
        c_ref[...] = jnp.zeros_like(c_ref)
        n_ref[...] = jnp.zeros_like(n_ref)
        m_ref[...] = jnp.zeros_like(m_ref)

    x = x_ref[0]
    halo = jnp.where(c == 0, halo0_ref[0], xprev_ref[0, L - CONV_HALO:L, :])
    ext = jnp.concatenate([halo, x], axis=0)
    conv = cb_ref[...]
    for j in range(CONV_W):
        o = CONV_HALO - (CONV_W - 1) + j
        conv = conv + ext[o:o + L] * cw_ref[j:j + 1, :]
    qk = conv * jax.nn.sigmoid(conv)

    t_id = lax.broadcasted_iota(jnp.int32, (L, L), 0)
    s_id = lax.broadcasted_iota(jnp.int32, (L, L), 1)
    causal = t_id >= s_id
    gcol = gcol_ref[0, 0]
    grow = grow_ref[0, 0]
    for h in range(M_HEADS):
        hd = slice(h * M_HD, (h + 1) * M_HD)
        q = qk[:, hd]
        k = qk[:, M_W + h * M_HD:M_W + (h + 1) * M_HD] * (M_HD ** -0.5)
        v = v_ref[0, :, hd].astype(bf16)
        ig_r = grow[h:h + 1, :]
        ig_c = gcol[:, h:h + 1]
        lf_r = _log_sigmoid(grow[M_HEADS + h:M_HEADS + h + 1, :])
        lf_c = _log_sigmoid(gcol[:, M_HEADS + h:M_HEADS + h + 1])
        b_c = jnp.sum(jnp.where(causal, lf_r, 0.0), axis=1, keepdims=True)
        b_r = jnp.sum(jnp.where(t_id <= s_id, lf_c, 0.0), axis=0, keepdims=True)
        m_prev = m_ref[h]
        dmat = jnp.where(causal, b_c - b_r + ig_r, -jnp.inf)
        inter = b_c + m_prev
        m_t = jnp.maximum(inter, jnp.max(dmat, axis=1, keepdims=True))
        w_intra = jnp.exp(dmat - m_t)
        w_inter = jnp.exp(inter - m_t)
        qb = q.astype(bf16)
        s = lax.dot_general(qb, k.astype(bf16), _NT, preferred_element_type=f32) * w_intra
        num = (jnp.dot(s.astype(bf16), v, preferred_element_type=f32)
               + w_inter * jnp.dot(qb, c_ref[h].astype(bf16), preferred_element_type=f32))
        den = jnp.sum(s, axis=1, keepdims=True) + w_inter * jnp.sum(q * n_ref[h], axis=1, keepdims=True)
        hh = num / jnp.maximum(jnp.abs(den), jnp.exp(-m_t))
        out_ref[0, :, hd] = jax.nn.sigmoid(o_ref[0, :, hd]) * hh
        m_new = m_t[L - 1:L]
        b_last = b_c[L - 1:L]
        w_s = jnp.exp(b_last - b_c + ig_c - m_new)
        w_p = jnp.exp(b_last + m_prev - m_new)
        kw = k * w_s
        c_ref[h] = w_p * c_ref[h] + jnp.dot(kw.T.astype(bf16), v, preferred_element_type=f32)
        n_ref[h] = w_p * n_ref[h] + jnp.sum(kw, axis=0, keepdims=True)
        m_ref[h] = m_new

    @pl.when(c == pl.num_programs(1) - 1)
    def _():
        c_out[0] = c_ref[...]
        n_out[0] = n_ref[...]
        m_out[0] = m_ref[...]


def mlstm_prompt(zqk, zv, zo, zif, conv_w, conv_b, b_if):
    B, T, _ = zqk.shape
    L = MLSTM_L
    assert T % L == 0
    nc = T // L
    f32 = jnp.float32
    gif = zif + b_if
    gcol = gif.reshape(B, nc, L, 2 * M_HEADS)
    grow = gcol.transpose(0, 1, 3, 2)
    halo0 = jnp.zeros((B, CONV_HALO, 2 * M_W), f32)
    out, C, n, m = pl.pallas_call(
        _mlstm_prompt_kernel,
        out_shape=(jax.ShapeDtypeStruct((B, T, M_W), f32),
                   jax.ShapeDtypeStruct((B, M_HEADS, M_HD, M_HD), f32),
                   jax.ShapeDtypeStruct((B, M_HEADS, 1, M_HD), f32),
                   jax.ShapeDtypeStruct((B, M_HEADS, 1, 1), f32)),
        grid=(B, nc),
        in_specs=[pl.BlockSpec((1, L, 2 * M_W), lambda b, c: (b, c, 0)),
                  pl.BlockSpec((1, L, 2 * M_W), lambda b, c: (b, jnp.maximum(c - 1, 0), 0)),
                  pl.BlockSpec((1, CONV_HALO, 2 * M_W), lambda b, c: (b, 0, 0)),
                  pl.BlockSpec((1, L, M_W), lambda b, c: (b, c, 0)),
                  pl.BlockSpec((1, L, M_W), lambda b, c: (b, c, 0)),
                  pl.BlockSpec((1, 1, L, 2 * M_HEADS), lambda b, c: (b, c, 0, 0)),
                  pl.BlockSpec((1, 1, 2 * M_HEADS, L), lambda b, c: (b, c, 0, 0)),
                  pl.BlockSpec((CONV_W, 2 * M_W), lambda b, c: (0, 0)),
                  pl.BlockSpec((1, 2 * M_W), lambda b, c: (0, 0))],
        out_specs=(pl.BlockSpec((1, L, M_W), lambda b, c: (b, c, 0)),
                   pl.BlockSpec((1, M_HEADS, M_HD, M_HD), lambda b, c: (b, 0, 0, 0)),
                   pl.BlockSpec((1, M_HEADS, 1, M_HD), lambda b, c: (b, 0, 0, 0)),
                   pl.BlockSpec((1, M_HEADS, 1, 1), lambda b, c: (b, 0, 0, 0))),
        scratch_shapes=[pltpu.VMEM((M_HEADS, M_HD, M_HD), f32), pltpu.VMEM((M_HEADS, 1, M_HD), f32),
                        pltpu.VMEM((M_HEADS, 1, 1), f32)],
        compiler_params=pltpu.CompilerParams(dimension_semantics=("arbitrary", "arbitrary"),
                                             vmem_limit_bytes=VMEM_LIMIT_BYTES),
        name="mlstm_prompt",
    )(zqk, zqk, halo0, zv, zo, gcol, grow, conv_w, conv_b[None])
    return out, C, n.reshape(B, M_HEADS, M_HD), m.reshape(B, M_HEADS)


def mlstm_chunk(carry, inp):
    C, n, m = carry
    q, k, v, ig, lf = inp
    L = q.shape[2]
    b = jnp.cumsum(lf, axis=-1)
    causal = jnp.tril(jnp.ones((L, L), dtype=bool))
    dmat = jnp.where(causal, b[..., :, None] - b[..., None, :] + ig[..., None, :], -jnp.inf)
    inter = b + m[..., None]
    m_t = jnp.maximum(inter, dmat.max(axis=-1))
    w_intra = jnp.exp(dmat - m_t[..., None])
    w_inter = jnp.exp(inter - m_t)
    s = jnp.einsum('bhtd,bhsd->bhts', q, k) * w_intra
    num = jnp.einsum('bhts,bhsv->bhtv', s, v) + w_inter[..., None] * jnp.einsum('bhtd,bhdv->bhtv', q, C)
    den = s.sum(-1) + w_inter * jnp.einsum('bhtd,bhd->bht', q, n)
    h = num / jnp.maximum(jnp.abs(den), jnp.exp(-m_t))[..., None]
    m_new = m_t[..., -1]
    w_s = jnp.exp(b[..., -1:] - b + ig - m_new[..., None])
    w_p = jnp.exp(b[..., -1] + m - m_new)
    C_new = w_p[..., None, None] * C + jnp.einsum('bhs,bhsd,bhsv->bhdv', w_s, k, v)
    n_new = w_p[..., None] * n + jnp.einsum('bhs,bhsd->bhd', w_s, k)
    return (C_new, n_new, m_new), h


def mlstm_mix(zqk, zv, zo, zif, buf0, C0, n0, m0, conv_w, conv_b, b_if, chunk):
    B, T, _ = zqk.shape
    full = jnp.concatenate([buf0, zqk], axis=1)
    conv = conv_b
    for j in range(CONV_W):
        conv = conv + full[:, j:j + T] * conv_w[j]
    qk = jax.nn.silu(conv)

    def heads(a):
        return a.reshape(B, T, M_HEADS, M_HD).transpose(0, 2, 1, 3)

    q = heads(qk[..., :M_W])
    k = heads(qk[..., M_W:]) * (M_HD ** -0.5)
    v = heads(zv)
    gif = zif + b_if
    ig = gif[..., :M_HEADS].transpose(0, 2, 1)
    lf = jax.nn.log_sigmoid(gif[..., M_HEADS:]).transpose(0, 2, 1)
    nc = T // chunk

    def to_chunks(a):
        return jnp.moveaxis(a.reshape(B, M_HEADS, nc, chunk, *a.shape[3:]), 2, 0)

    (C, n, m), h = lax.scan(mlstm_chunk, (C0, n0, m0),
                            (to_chunks(q), to_chunks(k), to_chunks(v), to_chunks(ig), to_chunks(lf)))
    h = jnp.moveaxis(h, 0, 2).reshape(B, M_HEADS, T, M_HD).transpose(0, 2, 1, 3).reshape(B, T, M_W)
    out = jax.nn.sigmoid(zo) * h
    return out, (C, n, m, full[:, T:])


PEER_COMBOS = 2 * PEER_HEADS
PEER_KEY_ROWS = 8
PEER_TILE = PEER_KEY_ROWS * N_KEYS
PEER_TS_ROWS = 24
LANES = 128
_NT = (((1,), (1,)), ((), ()))


def _peer_topk_kernel(q_ref, keys_ref, s_ref, e0_ref, e1_ref, tau_ref, ts_ref):
    c = pl.program_id(1)
    tt = q_ref.shape[0]
    s = lax.dot_general(keys_ref[0], q_ref[...].astype(jnp.bfloat16), _NT,
                        preferred_element_type=jnp.float32)
    s_ref[c] = s
    key_id = lax.broadcasted_iota(jnp.int32, s.shape, 0)
    work = s
    rows = []
    for _ in range(PEER_TOPK + 1):
        m = jnp.max(work, axis=0, keepdims=True)
        first = jnp.min(jnp.where(work == m, key_id, N_KEYS), axis=0, keepdims=True)
        work = jnp.where(key_id == first, -jnp.inf, work)
        rows.append(m)
    rows.append(jnp.full((PEER_TS_ROWS - PEER_TOPK - 1, tt), -jnp.inf, jnp.float32))
    ts_ref[c] = jnp.concatenate(rows, axis=0)

    @pl.when(c == PEER_COMBOS - 1)
    def _():
        for h in range(PEER_HEADS):
            t0 = ts_ref[2 * h]
            t1 = ts_ref[2 * h + 1]
            pieces = [t0[0:1] + t1] + [t0[a:a + 1] + t1[0:8] for a in range(1, 8)] + [t0[8:24] + t1[0:1]]
            cand = jnp.concatenate(pieces, axis=0)
            top = t0[0:1] + t1[0:1]
            v16 = top
            v17 = top
            z = jnp.zeros_like(top)
            seen = jnp.zeros_like(top)
            for _ in range(PEER_TOPK + 1):
                m = jnp.max(cand, axis=0, keepdims=True)
                eq = cand == m
                cnt = jnp.sum(jnp.where(eq, 1.0, 0.0), axis=0, keepdims=True)
                active = seen < PEER_TOPK
                take = jnp.minimum(cnt, PEER_TOPK - seen)
                v16 = jnp.where(active, m, v16)
                v17 = jnp.where(seen < PEER_TOPK + 1, m, v17)
                z = z + jnp.where(active, take * jnp.exp(m - top), 0.0)
                seen = seen + cnt
                cand = jnp.where(eq, -jnp.inf, cand)
            tau_ref[h:h + 1, :] = 0.5 * v16 + 0.5 * v17
            e0_ref[h] = jnp.exp(s_ref[2 * h] - t0[0:1]) / z
            e1_ref[h] = jnp.exp(s_ref[2 * h + 1] - t1[0:1])


def peer_scores(q, sub_keys, tt):
    n = q.shape[0]
    assert n % tt == 0
    keys = sub_keys.reshape(PEER_COMBOS, N_KEYS, PEER_QDIM // 2).astype(jnp.bfloat16)
    f32 = jnp.float32
    return pl.pallas_call(
        _peer_topk_kernel,
        out_shape=(jax.ShapeDtypeStruct((PEER_COMBOS, N_KEYS, n), f32),
                   jax.ShapeDtypeStruct((PEER_HEADS, N_KEYS, n), f32),
                   jax.ShapeDtypeStruct((PEER_HEADS, N_KEYS, n), f32),
                   jax.ShapeDtypeStruct((PEER_HEADS, n), f32)),
        grid=(n // tt, PEER_COMBOS),
        in_specs=[pl.BlockSpec((tt, PEER_QDIM // 2), lambda i, c: (i, c)),
                  pl.BlockSpec((1, N_KEYS, PEER_QDIM // 2), lambda i, c: (c, 0, 0))],
        out_specs=(pl.BlockSpec((PEER_COMBOS, N_KEYS, tt), lambda i, c: (0, 0, i)),
                   pl.BlockSpec((PEER_HEADS, N_KEYS, tt), lambda i, c: (0, 0, i)),
                   pl.BlockSpec((PEER_HEADS, N_KEYS, tt), lambda i, c: (0, 0, i)),
                   pl.BlockSpec((PEER_HEADS, tt), lambda i, c: (0, i))),
        scratch_shapes=[pltpu.VMEM((PEER_COMBOS, PEER_TS_ROWS, tt), f32)],
        compiler_params=pltpu.CompilerParams(dimension_semantics=("arbitrary", "arbitrary"),
                                             vmem_limit_bytes=VMEM_LIMIT_BYTES),
        name="peer_topk",
    )(q, keys)


def _peer_dense_kernel(xb_ref, h_ref, u_ref, vt_ref, s0_ref, ez_ref, s_ref, e1_ref, tau_ref, g_ref, b_ref,
                       o_ref, acc_ref, a_ref, w_ref):
    e = pl.program_id(1)
    tt = xb_ref.shape[0]

    @pl.when(e == 0)
    def _():
        acc_ref[...] = jnp.zeros_like(acc_ref)

    a_ref[...] = lax.dot_general(u_ref[...], xb_ref[...], _NT, preferred_element_type=jnp.float32)
    for r in range(PEER_KEY_ROWS):
        rows = slice(r * N_KEYS, (r + 1) * N_KEYS)
        for t in range(tt // LANES):
            tok = slice(t * LANES, (t + 1) * LANES)
            gate = jnp.zeros((N_KEYS, LANES), jnp.float32)
            for h in range(PEER_HEADS):
                need = tau_ref[h:h + 1, tok] - s0_ref[2 * h, r:r + 1, tok]
                picked = jnp.where(s_ref[2 * h + 1, :, tok] >= need, e1_ref[h, :, tok], 0.0)
                gate = gate + picked * ez_ref[h, r:r + 1, tok]
            ar = a_ref[rows, tok]
            act = 0.5 * ar * (1.0 + lax.erf(ar * (2.0 ** -0.5)))
            w_ref[rows, tok] = (gate * act).astype(jnp.bfloat16)
    acc_ref[...] += jnp.dot(vt_ref[...], w_ref[...], preferred_element_type=jnp.float32)

    @pl.when(e == pl.num_programs(1) - 1)
    def _():
        r = ALPHA * h_ref[...] + acc_ref[...].T
        mu = jnp.mean(r, axis=-1, keepdims=True)
        d = r - mu
        var = jnp.mean(d * d, axis=-1, keepdims=True)
        o_ref[...] = d * lax.rsqrt(var + LN_EPS) * g_ref[...] + b_ref[...]


def peer_tail(h, q, sub_keys, u_bf, vt_bf, ln_g, ln_b, tt):
    n, d = h.shape
    s, e0z, e1, tau = peer_scores(q, sub_keys, tt)
    n_exp = u_bf.shape[0]
    return pl.pallas_call(
        _peer_dense_kernel,
        out_shape=jax.ShapeDtypeStruct((n, d), jnp.float32),
        grid=(n // tt, n_exp // PEER_TILE),
        in_specs=[pl.BlockSpec((tt, d), lambda i, e: (i, 0)),
                  pl.BlockSpec((tt, d), lambda i, e: (i, 0)),
                  pl.BlockSpec((PEER_TILE, d), lambda i, e: (e, 0)),
                  pl.BlockSpec((d, PEER_TILE), lambda i, e: (0, e)),
                  pl.BlockSpec((PEER_COMBOS, PEER_KEY_ROWS, tt), lambda i, e: (0, e, i)),
                  pl.BlockSpec((PEER_HEADS, PEER_KEY_ROWS, tt), lambda i, e: (0, e, i)),
                  pl.BlockSpec((PEER_COMBOS, N_KEYS, tt), lambda i, e: (0, 0, i)),
                  pl.BlockSpec((PEER_HEADS, N_KEYS, tt), lambda i, e: (0, 0, i)),
                  pl.BlockSpec((PEER_HEADS, tt), lambda i, e: (0, i)),
                  pl.BlockSpec((1, d), lambda i, e: (0, 0)),
                  pl.BlockSpec((1, d), lambda i, e: (0, 0))],
        out_specs=pl.BlockSpec((tt, d), lambda i, e: (i, 0)),
        scratch_shapes=[pltpu.VMEM((d, tt), jnp.float32), pltpu.VMEM((PEER_TILE, tt), jnp.float32),
                        pltpu.VMEM((PEER_TILE, tt), jnp.bfloat16)],
        compiler_params=pltpu.CompilerParams(dimension_semantics=("arbitrary", "arbitrary"),
                                             vmem_limit_bytes=VMEM_LIMIT_BYTES),
        name="peer_dense",
    )(h.astype(jnp.bfloat16), h, u_bf, vt_bf, s, e0z, s, e1, tau, ln_g[None], ln_b[None])


def block_tail(x, mix, w_out, ln_g, ln_b, w_pq, sub_keys, u_bf, vt_bf, tt):
    lead = x.shape[:-1]
    h = layer_norm(ALPHA * x + mm3(mix, w_out), ln_g[0], ln_b[0]).reshape(-1, D_MODEL)
    q = pallas_matmul(h, w_pq)
    return peer_tail(h, q, sub_keys, u_bf, vt_bf, ln_g[1], ln_b[1], tt).reshape(*lead, D_MODEL)


def prompt_mix(x, w_in, pe, w1, b1, w2, conv_w, conv_b, b_if):
    B, S, _ = x.shape
    pos = jnp.arange(S)
    zq, zkv, zg, zqk, zv, zo, zif = split_in_proj(x, w_in)
    q, (k_cmp, v_cmp, k_slc, v_slc, k_win, v_win), gates = nsa_project(zq, zkv, zg, pos)
    chunk_w = CMP_STRIDE * NSA_KVW

    def compressed(rows, c):
        proj = chunk_projection(rows.reshape(B * (S // CMP_STRIDE), chunk_w), w1[c])
        return compress_from_projection(proj.reshape(B, S // CMP_STRIDE, -1), pe[c], w1[c], b1[c], w2[c])

    kc = compressed(k_cmp, 0)
    vc = compressed(v_cmp, 1)
    o_nsa = nsa_prompt_attention(q, kc, vc, k_slc, v_slc, k_win, v_win, zg)
    o_m, C, n, m = mlstm_prompt(zqk, zv, zo, zif, conv_w, conv_b, b_if)
    buf = zqk[:, S - (CONV_W - 1):]
    wl = min(WINDOW, S)
    mix = jnp.concatenate([o_nsa, o_m], axis=-1)
    return mix, (k_cmp, v_cmp, k_slc, v_slc, k_win[:, S - wl:], v_win[:, S - wl:], C, n, m, buf)


def sample_mix(x, kc_pool, vc_pool, ks_pool, vs_pool, kw_buf, vw_buf, C0, n0, m0, buf0, page_table,
               w_in, pe, w1, b1, w2, conv_w, conv_b, b_if):
    B, T, _ = x.shape
    past = page_table.shape[1] * PAGE_SIZE
    pos = past + jnp.arange(T)
    zq, zkv, zg, zqk, zv, zo, zif = split_in_proj(x, w_in)
    q, (k_cmp, v_cmp, k_slc, v_slc, k_win, v_win), gates = nsa_project(zq, zkv, zg, pos)

    def with_past(pool, new):
        old = pool[page_table].reshape(B, past, NSA_KV_HEADS, NSA_HD)
        return jnp.concatenate([old, new], axis=1)

    assert (past + T) // CMP_STRIDE == past // CMP_STRIDE

    def compressed(pool, c):
        pages = pool.transpose(0, 2, 3, 1)[page_table]
        return compress_from_projection(page_projection(pages, w1[c]), pe[c], w1[c], b1[c], w2[c])

    o_cmp, p = cmp_attend(q, pos, compressed(kc_pool, 0), compressed(vc_pool, 1))
    n_sel = -(-(past + T) // SEL_BLOCK)
    sel = select_blocks(p, pos, n_sel)
    kb = to_blocks(with_past(ks_pool, k_slc), n_sel)
    vb = to_blocks(with_past(vs_pool, v_slc), n_sel)
    wb = kw_buf.shape[1]
    kw = jnp.concatenate([kw_buf, k_win], axis=1)
    vw = jnp.concatenate([vw_buf, v_win], axis=1)
    kpos = past - wb + jnp.arange(wb + T)
    o_sel = sel_attend(q, pos, sel, kb, vb)
    o_win = win_attend(q, pos, kw, vw, kpos)
    o_nsa = nsa_combine(gates, o_cmp, o_sel, o_win)
    o_m, (C, n, m, buf) = mlstm_mix(zqk, zv, zo, zif, buf0, C0, n0, m0, conv_w, conv_b, b_if, T)
    mix = jnp.concatenate([o_nsa, o_m], axis=-1)
    return mix, (k_cmp, v_cmp, k_slc, v_slc, kw[:, T:], vw[:, T:], C, n, m, buf)


def kernel(x_prompt, x_sample, cache_k_cmp, cache_v_cmp, cache_k_slc, cache_v_slc, cache_k_win, cache_v_win,
           state_C, state_n, state_m, state_conv, page_table, w_in, w_out, w_phi1, b_phi1, w_phi2, pe_cmp,
           conv_w, conv_b, b_if, ln_g, ln_b, w_pq, sub_keys, u_tab, v_tab):
    l = 0
    mix_p, st_p = prompt_mix(x_prompt, w_in[l], pe_cmp[l], w_phi1[l], b_phi1[l], w_phi2[l],
                             conv_w[l], conv_b[l], b_if[l])
    mix_s, st_s = sample_mix(x_sample, cache_k_cmp[l], cache_v_cmp[l], cache_k_slc[l], cache_v_slc[l],
                             cache_k_win[l], cache_v_win[l], state_C[l], state_n[l], state_m[l],
                             state_conv[l], page_table, w_in[l], pe_cmp[l], w_phi1[l], b_phi1[l],
                             w_phi2[l], conv_w[l], conv_b[l], b_if[l])
    u_bf = u_tab[l].astype(jnp.bfloat16)
    vt_bf = v_tab[l].astype(jnp.bfloat16).T
    xp = block_tail(x_prompt, mix_p, w_out[l], ln_g[l], ln_b[l], w_pq[l], sub_keys[l], u_bf, vt_bf, 512)
    xs = block_tail(x_sample, mix_s, w_out[l], ln_g[l], ln_b[l], w_pq[l], sub_keys[l], u_bf, vt_bf, 128)
    return (xp, xs) + tuple(a[None] for a in st_p) + tuple(a[None] for a in st_s)
```

```python
import functools

import jax
import jax.numpy as jnp
import numpy as np
from jax import lax
from jax.experimental import pallas as pl
from jax.experimental.pallas import tpu as pltpu

D_MODEL = 1024
DEPTH = 1
PAGE_SIZE = 128
NSA_HEADS = 8
NSA_KV_HEADS = 2
NSA_GROUP = NSA_HEADS // NSA_KV_HEADS
NSA_HD = 64
NSA_QW = NSA_HEADS * NSA_HD
NSA_KVW = NSA_KV_HEADS * NSA_HD
CMP_BLOCK = 32
CMP_STRIDE = 16
SEL_BLOCK = 64
SEL_TOP = 16
WINDOW = 512
Q_BLOCK = 64
ATTN_SCALE = NSA_HD ** -0.5
ROPE_THETA = 10000.0
M_HEADS = 4
M_HD = 128
M_W = M_HEADS * M_HD
M_CHUNK = 64
CONV_W = 4
PEER_HEADS = 8
N_KEYS = 128
PEER_TOPK = 16
PEER_QDIM = 256
PEER_BLOCK = 128
IN_SPLITS = (NSA_QW, 6 * NSA_KVW, 3 * NSA_HEADS, 2 * M_W, M_W, M_W, 2 * M_HEADS)
LN_EPS = 1e-5
ALPHA = (2 * DEPTH) ** 0.25

VMEM_LIMIT_BYTES = 56 * 1024 * 1024


def _mm_kernel(x_ref, w_ref, o_ref):
    o_ref[...] = jnp.dot(x_ref[...].astype(jnp.bfloat16), w_ref[...], preferred_element_type=jnp.float32)


def pallas_matmul(x, w, tm=512):
    M, K = x.shape
    N = w.shape[1]
    tm = min(tm, M)
    assert M % tm == 0
    return pl.pallas_call(
        _mm_kernel,
        out_shape=jax.ShapeDtypeStruct((M, N), jnp.float32),
        grid=(M // tm,),
        in_specs=[pl.BlockSpec((tm, K), lambda i: (i, 0)), pl.BlockSpec((K, N), lambda i: (0, 0))],
        out_specs=pl.BlockSpec((tm, N), lambda i: (i, 0)),
        compiler_params=pltpu.CompilerParams(dimension_semantics=("arbitrary",),
                                             vmem_limit_bytes=VMEM_LIMIT_BYTES),
        name="proj_matmul",
    )(x, w.astype(jnp.bfloat16))


def mm3(x, w):
    lead = x.shape[:-1]
    return pallas_matmul(x.reshape(-1, x.shape[-1]), w).reshape(*lead, w.shape[1])


def layer_norm(x, g, b):
    mu = x.mean(-1, keepdims=True)
    var = jnp.square(x - mu).mean(-1, keepdims=True)
    return (x - mu) * lax.rsqrt(var + LN_EPS) * g + b


def rope(x, pos):
    half = x.shape[-1] // 2
    inv = ROPE_THETA ** (-jnp.arange(half, dtype=jnp.float32) / half)
    ang = pos.astype(jnp.float32)[:, None] * inv[None, :]
    cos = jnp.cos(ang)[:, None, :]
    sin = jnp.sin(ang)[:, None, :]
    x1, x2 = x[..., :half], x[..., half:]
    return jnp.concatenate([x1 * cos - x2 * sin, x2 * cos + x1 * sin], axis=-1)


def split_in_proj(x, w_in):
    z = mm3(x, w_in)
    cuts = [int(c) for c in np.cumsum(IN_SPLITS)[:-1]]
    return jnp.split(z, cuts, axis=-1)


_IN_OFF = np.concatenate([[0], np.cumsum(IN_SPLITS)])
_IN_ORDER = (0, 1, 3, 4, 5, 2, 6)
_N_KV_ROWS = 6
_KV_BF16 = (2, 3, 4, 5)
GATE_W = IN_SPLITS[2] + IN_SPLITS[6]


def _rope_pairs(x, cos, sin_signed):
    half = NSA_HD // 2
    lane = lax.broadcasted_iota(jnp.int32, x.shape, 1)
    partner = jnp.where(lane % NSA_HD < half, pltpu.roll(x, LANES - half, 1), pltpu.roll(x, half, 1))
    return x * cos + partner * sin_signed


def _in_proj_kernel(x_ref, w_ref, cos_ref, sin_ref, q_ref, *rest, kv_major):
    kv_refs = rest[:_N_KV_ROWS]
    rest = rest[_N_KV_ROWS:]
    if kv_major:
        bf_refs, rest = rest[:len(_KV_BF16)], rest[len(_KV_BF16):]
    zqk_ref, zv_ref, zo_ref, zgate_ref = rest
    z = jnp.dot(x_ref[...].astype(jnp.bfloat16), w_ref[...], preferred_element_type=jnp.float32)
    cos = cos_ref[...]
    sin = sin_ref[...]
    for g in range(NSA_QW // LANES):
        sl = slice(g * LANES, (g + 1) * LANES)
        q_ref[:, sl] = (_rope_pairs(z[:, sl], cos, sin) * ATTN_SCALE).astype(jnp.bfloat16)
    for r in range(_N_KV_ROWS):
        row = z[:, NSA_QW + r * NSA_KVW:NSA_QW + (r + 1) * NSA_KVW]
        if r % 2 == 0:
            row = _rope_pairs(row, cos, sin)
        kv_refs[r][...] = row
        if kv_major and r in _KV_BF16:
            dst = bf_refs[_KV_BF16.index(r)]
            for n in range(NSA_KV_HEADS):
                dst[0, n] = row[:, n * NSA_HD:(n + 1) * NSA_HD].astype(jnp.bfloat16)
    o = NSA_QW + _N_KV_ROWS * NSA_KVW
    zqk_ref[...] = z[:, o:o + 2 * M_W]
    zv_ref[...] = z[:, o + 2 * M_W:o + 3 * M_W]
    zo_ref[...] = z[:, o + 3 * M_W:o + 4 * M_W]
    zgate_ref[...] = z[:, o + 4 * M_W:o + 4 * M_W + GATE_W]


def in_proj_fused(x, w_in, pos, tm, kv_major):
    B, T, D = x.shape
    M = B * T
    assert M % tm == 0 and NSA_KVW == LANES and (not kv_major or T % tm == 0)
    f32, bf16 = jnp.float32, jnp.bfloat16
    w = jnp.concatenate([w_in[:, _IN_OFF[i]:_IN_OFF[i + 1]] for i in _IN_ORDER], axis=1).astype(bf16)
    half = NSA_HD // 2
    inv = ROPE_THETA ** (-jnp.arange(half, dtype=f32) / half)
    ang = pos.astype(f32)[:, None] * inv[None, :]
    cos = jnp.tile(jnp.cos(ang), (B, 2 * LANES // NSA_HD))
    sin = jnp.tile(jnp.concatenate([-jnp.sin(ang), jnp.sin(ang)], axis=1), (B, LANES // NSA_HD))
    n_w = w.shape[1]

    def rows(width):
        return pl.BlockSpec((tm, width), lambda i: (i, 0))

    out_shape = [jax.ShapeDtypeStruct((M, NSA_QW), bf16)] + [jax.ShapeDtypeStruct((M, NSA_KVW), f32)] * _N_KV_ROWS
    out_specs = [rows(NSA_QW)] + [rows(NSA_KVW)] * _N_KV_ROWS
    if kv_major:
        per_seq = T // tm
        out_shape += [jax.ShapeDtypeStruct((B, NSA_KV_HEADS, T, NSA_HD), bf16)] * len(_KV_BF16)
        out_specs += [pl.BlockSpec((1, NSA_KV_HEADS, tm, NSA_HD),
                                   lambda i: (i // per_seq, 0, i % per_seq, 0))] * len(_KV_BF16)
    out_shape += [jax.ShapeDtypeStruct((M, 2 * M_W), f32), jax.ShapeDtypeStruct((M, M_W), f32),
                  jax.ShapeDtypeStruct((M, M_W), f32), jax.ShapeDtypeStruct((M, GATE_W), f32)]
    out_specs += [rows(2 * M_W), rows(M_W), rows(M_W), rows(GATE_W)]
    outs = pl.pallas_call(
        functools.partial(_in_proj_kernel, kv_major=kv_major),
        out_shape=tuple(out_shape),
        grid=(M // tm,),
        in_specs=[rows(D), pl.BlockSpec((D, n_w), lambda i: (0, 0)), rows(LANES), rows(LANES)],
        out_specs=tuple(out_specs),
        compiler_params=pltpu.CompilerParams(dimension_semantics=("arbitrary",),
                                             vmem_limit_bytes=VMEM_LIMIT_BYTES),
        name="in_proj",
    )(x.reshape(M, D), w, cos, sin)
    names = ["q", "k_cmp", "v_cmp", "k_slc", "v_slc", "k_win", "v_win"]
    if kv_major:
        names += ["k_slc_bf", "v_slc_bf", "k_win_bf", "v_win_bf"]
    names += ["zqk", "zv", "zo", "zgate"]
    return dict(zip(names, outs))


def nsa_project(zq, zkv, zg, pos):
    B, T, _ = zq.shape
    q = rope(zq.reshape(B, T, NSA_HEADS, NSA_HD), pos)
    kv = zkv.reshape(B, T, 6, NSA_KV_HEADS, NSA_HD)
    rows = (rope(kv[:, :, 0], pos), kv[:, :, 1], rope(kv[:, :, 2], pos), kv[:, :, 3],
            rope(kv[:, :, 4], pos), kv[:, :, 5])
    gates = jax.nn.sigmoid(zg).reshape(B, T, NSA_HEADS, 3)
    return q, rows, gates


def chunk_projection(chunks, w1):
    assert CMP_BLOCK == 2 * CMP_STRIDE
    return pallas_matmul(chunks, _expanded_w1(w1))


def _expanded_w1(w1):
    w1r = w1.reshape(2, CMP_STRIDE, NSA_HD, w1.shape[-1])
    wbig = jnp.einsum('hpdf,kn->pkdnhf', w1r, jnp.eye(NSA_KV_HEADS, dtype=w1.dtype))
    return wbig.reshape(CMP_STRIDE * NSA_KVW, 2 * NSA_KV_HEADS * w1.shape[-1])


def _page_projection_kernel(pg_ref, w_ref, o_ref, x_ref, t_ref):
    n_pages = pg_ref.shape[1]
    per_page = PAGE_SIZE // CMP_STRIDE

    def place(g, carry):
        t_ref[...] = pg_ref[0, g].reshape(NSA_KVW, PAGE_SIZE).T
        row0 = pl.multiple_of(g * per_page, per_page)
        for p in range(CMP_STRIDE):
            x_ref[pl.ds(row0, per_page), p * NSA_KVW:(p + 1) * NSA_KVW] = t_ref[pl.ds(p, per_page, stride=CMP_STRIDE), :]
        return carry

    lax.fori_loop(0, n_pages, place, 0)
    o_ref[0] = jnp.dot(x_ref[...].astype(jnp.bfloat16), w_ref[...], preferred_element_type=jnp.float32)


def page_projection(pages, w1):
    B, n_pages = pages.shape[:2]
    assert pages.shape[2:] == (NSA_KV_HEADS, NSA_HD, PAGE_SIZE) and NSA_KVW == LANES and PAGE_SIZE == LANES
    wbig = _expanded_w1(w1).astype(jnp.bfloat16)
    rows = n_pages * (PAGE_SIZE // CMP_STRIDE)
    return pl.pallas_call(
        _page_projection_kernel,
        out_shape=jax.ShapeDtypeStruct((B, rows, wbig.shape[1]), jnp.float32),
        grid=(B,),
        in_specs=[pl.BlockSpec((1, n_pages, NSA_KV_HEADS, NSA_HD, PAGE_SIZE), lambda b: (b, 0, 0, 0, 0)),
                  pl.BlockSpec(wbig.shape, lambda b: (0, 0))],
        out_specs=pl.BlockSpec((1, rows, wbig.shape[1]), lambda b: (b, 0, 0)),
        scratch_shapes=[pltpu.VMEM((rows, wbig.shape[0]), jnp.float32),
                        pltpu.VMEM((PAGE_SIZE, NSA_KVW), jnp.float32)],
        compiler_params=pltpu.CompilerParams(dimension_semantics=("arbitrary",),
                                             vmem_limit_bytes=VMEM_LIMIT_BYTES),
        name="page_projection",
    )(pages, wbig)


def compress_from_projection(proj, pe, w1, b1, w2):
    f = w1.shape[-1]
    bias = jnp.dot(pe.reshape(-1), w1, precision=lax.Precision.HIGHEST) + b1
    heads = []
    for n in range(NSA_KV_HEADS):
        first = proj[:, :-1, 2 * n * f:(2 * n + 1) * f]
        second = proj[:, 1:, (2 * n + 1) * f:(2 * n + 2) * f]
        heads.append(jax.nn.gelu(first + second + bias, approximate=False) @ w2)
    return jnp.stack(heads, axis=2)


def cmp_attend(q, qpos, kc, vc):
    B, T = q.shape[:2]
    qg = q.reshape(B, T, NSA_KV_HEADS, NSA_GROUP, NSA_HD)
    s = jnp.einsum('btngd,bcnd->btngc', qg, kc) * ATTN_SCALE
    nblk = kc.shape[1]
    blk_end = jnp.arange(nblk) * CMP_STRIDE + CMP_BLOCK - 1
    valid = (blk_end[None, :] <= qpos[:, None])[None, :, None, None, :]
    p = jax.nn.softmax(jnp.where(valid, s, -1e30), axis=-1) * valid
    o = jnp.einsum('btngc,bcnd->btngd', p, vc)
    return o.reshape(B, T, NSA_HEADS, NSA_HD), p


def select_blocks(p, qpos, n_sel):
    imp = p.sum(axis=3)
    R = SEL_BLOCK // CMP_STRIDE
    r = CMP_BLOCK // CMP_STRIDE
    nb = imp.shape[-1]
    right = n_sel * R + R - 1 - nb
    padded = jnp.pad(imp, ((0, 0), (0, 0), (0, 0), (r - 1, right)))
    score = padded[..., 0:(n_sel - 1) * R + 1:R]
    for o in range(1, R + r - 1):
        score = score + padded[..., o:o + (n_sel - 1) * R + 1:R]
    j = jnp.arange(n_sel)[None, :]
    cur = (qpos // SEL_BLOCK)[:, None]
    valid = (j * SEL_BLOCK <= qpos[:, None])[None, :, None, :]
    forced = ((j == 0) | (j == cur) | (j == cur - 1))[None, :, None, :]
    score = jnp.where(forced, jnp.inf, jnp.where(valid, score, -jnp.inf))
    idx = j[0]
    before = (score[..., None, :] > score[..., :, None]) | ((score[..., None, :] == score[..., :, None])
                                                          & (idx[None, :] < idx[:, None]))
    rank = before.sum(-1)
    hit = rank[..., None, :] == jnp.arange(min(SEL_TOP, n_sel))[:, None]
    return (hit * idx).sum(-1)


def to_blocks(rows, n_sel):
    B, L, KV, hd = rows.shape
    rows = jnp.pad(rows, ((0, 0), (0, n_sel * SEL_BLOCK - L), (0, 0), (0, 0)))
    return rows.reshape(B, n_sel, SEL_BLOCK, KV, hd).transpose(0, 3, 1, 2, 4)


def take_rows(table, idx):
    return table[idx]


def sel_attend(q, qpos, sel, kb, vb):
    B, Tq = q.shape[:2]
    k = sel.shape[-1]
    sel_t = sel.transpose(0, 2, 1, 3)
    gather = jax.vmap(jax.vmap(take_rows))
    kg = gather(kb, sel_t).reshape(B, NSA_KV_HEADS, Tq, k * SEL_BLOCK, NSA_HD)
    vg = gather(vb, sel_t).reshape(B, NSA_KV_HEADS, Tq, k * SEL_BLOCK, NSA_HD)
    kpos = (sel_t[..., None] * SEL_BLOCK + jnp.arange(SEL_BLOCK)).reshape(B, NSA_KV_HEADS, Tq, k * SEL_BLOCK)
    qg = q.reshape(B, Tq, NSA_KV_HEADS, NSA_GROUP, NSA_HD).transpose(0, 2, 1, 3, 4)
    s = jnp.einsum('bntgd,bntsd->bntgs', qg, kg) * ATTN_SCALE
    mask = kpos[:, :, :, None, :] <= qpos[None, None, :, None, None]
    pr = jax.nn.softmax(jnp.where(mask, s, -jnp.inf), axis=-1)
    o = jnp.einsum('bntgs,bntsd->bntgd', pr, vg)
    return o.transpose(0, 2, 1, 3, 4).reshape(B, Tq, NSA_HEADS, NSA_HD)


def win_attend(q, qpos, k, v, kpos):
    B, Tq = q.shape[:2]
    qg = q.reshape(B, Tq, NSA_KV_HEADS, NSA_GROUP, NSA_HD)
    s = jnp.einsum('btngd,bsnd->btngs', qg, k) * ATTN_SCALE
    diff = qpos[:, None] - kpos[None, :]
    mask = ((diff >= 0) & (diff < WINDOW) & (kpos[None, :] >= 0))[None, :, None, None, :]
    pr = jax.nn.softmax(jnp.where(mask, s, -jnp.inf), axis=-1)
    o = jnp.einsum('btngs,bsnd->btngd', pr, v)
    return o.reshape(B, Tq, NSA_HEADS, NSA_HD)


def nsa_combine(gates, o_cmp, o_sel, o_win):
    B, T = gates.shape[:2]
    o = gates[..., 0:1] * o_cmp + gates[..., 1:2] * o_sel + gates[..., 2:3] * o_win
    return o.reshape(B, T, NSA_QW)


NSA_TQ = 128
NSA_CK = 512
MASKED = -1e30


def _softmax_rows(s):
    m = jnp.max(s, axis=-1, keepdims=True)
    e = jnp.exp(s - m)
    return e / jnp.sum(e, axis=-1, keepdims=True)


def _nsa_prompt_kernel(q_ref, kc_ref, vc_ref, ks_ref, vs_ref, kw_ref, vw_ref, zg_ref, msel_ref, exp_ref, o_ref):
    f32, bf16 = jnp.float32, jnp.bfloat16
    tq = NSA_TQ
    q0 = pl.program_id(2) * tq
    qb = q_ref[0]
    qs = jnp.concatenate([qb[:, g * NSA_HD:(g + 1) * NSA_HD] for g in range(NSA_GROUP)], axis=0)
    tpos = q0 + lax.broadcasted_iota(jnp.int32, (tq, 1), 0)

    def per_head(a):
        return jnp.concatenate([a] * NSA_GROUP, axis=0)

    s = lax.dot_general(qs, kc_ref[0, 0], _NT, preferred_element_type=f32)
    cblk = lax.broadcasted_iota(jnp.int32, (tq, 128), 1)
    cvalid = cblk * CMP_STRIDE + (CMP_BLOCK - 1) <= tpos
    s = s + per_head(jnp.where(cvalid, 0.0, MASKED))
    e = jnp.exp(s - jnp.max(s, axis=-1, keepdims=True)) * per_head(jnp.where(cvalid, 1.0, 0.0))
    l = jnp.sum(e, axis=-1, keepdims=True)
    p = e / jnp.where(l > 0.0, l, 1.0)
    o_cmp = jnp.dot(p.astype(bf16), vc_ref[0, 0], preferred_element_type=f32)

    imp = p[0:tq]
    for g in range(1, NSA_GROUP):
        imp = imp + p[g * tq:(g + 1) * tq]
    hi = imp.astype(bf16)
    r1 = imp - hi.astype(f32)
    mid = r1.astype(bf16)
    lo = (r1 - mid.astype(f32)).astype(bf16)
    msel = msel_ref[...]
    score = (jnp.dot(hi, msel, preferred_element_type=f32) + jnp.dot(mid, msel, preferred_element_type=f32)
             + jnp.dot(lo, msel, preferred_element_type=f32))
    n_sel = score.shape[1]
    j = lax.broadcasted_iota(jnp.int32, (tq, n_sel), 1)
    cur = tpos // SEL_BLOCK
    forced = (j == 0) | (j == cur) | (j == cur - 1)
    score = jnp.where(forced, jnp.inf, jnp.where(j * SEL_BLOCK <= tpos, score, -jnp.inf))
    rank = jnp.zeros((tq, n_sel), f32)
    for jp in range(n_sel):
        col = score[:, jp:jp + 1]
        before = (col > score) | ((col == score) & (j > jp))
        rank = rank + jnp.where(before, 1.0, 0.0)
    sel01 = jnp.where(rank < SEL_TOP, 1.0, 0.0).astype(bf16)

    ck = NSA_CK
    rows = NSA_GROUP * tq

    def sel_chunk(c, carry):
        m, l, acc = carry
        k0 = pl.multiple_of(c * ck, ck)
        s = lax.dot_general(qs, ks_ref[0, 0, pl.ds(k0, ck), :], _NT, preferred_element_type=f32)
        chosen = jnp.dot(sel01, exp_ref[c], preferred_element_type=f32)
        kpos = k0 + lax.broadcasted_iota(jnp.int32, (tq, ck), 1)
        ok = (chosen > 0.5) & (kpos <= tpos)
        s = s + per_head(jnp.where(ok, 0.0, MASKED))
        m_new = jnp.maximum(m, jnp.max(s, axis=-1, keepdims=True))
        a = jnp.exp(m - m_new)
        pr = jnp.exp(s - m_new)
        l = a * l + jnp.sum(pr, axis=-1, keepdims=True)
        acc = a * acc + jnp.dot(pr.astype(bf16), vs_ref[0, 0, pl.ds(k0, ck), :], preferred_element_type=f32)
        return m_new, l, acc

    init = (jnp.full((rows, 1), MASKED, f32), jnp.zeros((rows, 1), f32), jnp.zeros((rows, NSA_HD), f32))
    n_chunks = (q0 + tq + ck - 1) // ck
    _, l_sel, acc_sel = lax.fori_loop(0, n_chunks, sel_chunk, init)
    o_sel = acc_sel / l_sel

    w0 = pl.multiple_of(jnp.maximum(q0 - WINDOW, 0), tq)
    wl = WINDOW + tq
    s = lax.dot_general(qs, kw_ref[0, 0, pl.ds(w0, wl), :], _NT, preferred_element_type=f32)
    diff = tpos - (w0 + lax.broadcasted_iota(jnp.int32, (tq, wl), 1))
    s = s + per_head(jnp.where((diff >= 0) & (diff < WINDOW), 0.0, MASKED))
    o_win = jnp.dot(_softmax_rows(s).astype(bf16), vw_ref[0, 0, pl.ds(w0, wl), :], preferred_element_type=f32)

    gates = jax.nn.sigmoid(zg_ref[0, 0])
    for g in range(NSA_GROUP):
        r = slice(g * tq, (g + 1) * tq)
        o_ref[0, :, g * NSA_HD:(g + 1) * NSA_HD] = (gates[:, 3 * g:3 * g + 1] * o_cmp[r]
                                                    + gates[:, 3 * g + 1:3 * g + 2] * o_sel[r]
                                                    + gates[:, 3 * g + 2:3 * g + 3] * o_win[r])


def nsa_prompt_attention(qs, kc, vc, k_slc, v_slc, k_win, v_win, zg):
    B, S = qs.shape[:2]
    bf16 = jnp.bfloat16
    assert S % NSA_CK == 0 and S % NSA_TQ == 0 and WINDOW % NSA_TQ == 0 and WINDOW + NSA_TQ <= S
    n_sel = S // SEL_BLOCK
    nb = kc.shape[1]
    assert nb <= 128

    def pad_blocks(a):
        return jnp.pad(a.transpose(0, 2, 1, 3).astype(bf16), ((0, 0), (0, 0), (0, 128 - nb), (0, 0)))

    zg4 = zg.reshape(B, S, NSA_KV_HEADS, 3 * NSA_GROUP).transpose(0, 2, 1, 3)
    c = np.arange(128)[:, None]
    jj = np.arange(n_sel)[None, :]
    ratio = SEL_BLOCK // CMP_STRIDE
    msel = ((c >= jj * ratio - (CMP_BLOCK // CMP_STRIDE - 1)) & (c <= jj * ratio + ratio - 1) & (c < nb))
    expand = (np.arange(S)[None, :] // SEL_BLOCK == np.arange(n_sel)[:, None])
    expand = expand.reshape(n_sel, S // NSA_CK, NSA_CK).transpose(1, 0, 2)
    row_spec = pl.BlockSpec((1, 1, S, NSA_HD), lambda b, n, i: (b, n, 0, 0))
    blk_spec = pl.BlockSpec((1, 1, 128, NSA_HD), lambda b, n, i: (b, n, 0, 0))
    return pl.pallas_call(
        _nsa_prompt_kernel,
        out_shape=jax.ShapeDtypeStruct((B, S, NSA_QW), jnp.float32),
        grid=(B, NSA_KV_HEADS, S // NSA_TQ),
        in_specs=[pl.BlockSpec((1, NSA_TQ, NSA_GROUP * NSA_HD), lambda b, n, i: (b, i, n)),
                  blk_spec, blk_spec, row_spec, row_spec, row_spec, row_spec,
                  pl.BlockSpec((1, 1, NSA_TQ, 3 * NSA_GROUP), lambda b, n, i: (b, n, i, 0)),
                  pl.BlockSpec((128, n_sel), lambda b, n, i: (0, 0)),
                  pl.BlockSpec((S // NSA_CK, n_sel, NSA_CK), lambda b, n, i: (0, 0, 0))],
        out_specs=pl.BlockSpec((1, NSA_TQ, NSA_GROUP * NSA_HD), lambda b, n, i: (b, i, n)),
        compiler_params=pltpu.CompilerParams(dimension_semantics=("arbitrary", "arbitrary", "arbitrary"),
                                             vmem_limit_bytes=VMEM_LIMIT_BYTES),
        name="nsa_prompt_attention",
    )(qs, pad_blocks(kc), pad_blocks(vc), k_slc, v_slc, k_win, v_win,
      zg4, jnp.asarray(msel, bf16), jnp.asarray(expand, bf16))


MLSTM_L = 128
CONV_HALO = 8


def _log_sigmoid(x):
    return -(jnp.maximum(-x, 0.0) + jnp.log1p(jnp.exp(-jnp.abs(x))))


def _mlstm_prompt_kernel(x_ref, xprev_ref, halo0_ref, v_ref, o_ref, gcol_ref, grow_ref, cw_ref, cb_ref,
                         out_ref, c_out, n_out, m_out, c_ref, n_ref, m_ref):
    f32, bf16 = jnp.float32, jnp.bfloat16
    c = pl.program_id(1)
    L = MLSTM_L

    @pl.when(c == 0)
    def _():
        c_ref[...] = jnp.zeros_like(c_ref)
        n_ref[...] = jnp.zeros_like(n_ref)
        m_ref[...] = jnp.zeros_like(m_ref)

    x = x_ref[0]
    halo = jnp.where(c == 0, halo0_ref[0], xprev_ref[0, L - CONV_HALO:L, :])
    ext = jnp.concatenate([halo, x], axis=0)
    conv = cb_ref[...]
    for j in range(CONV_W):
        o = CONV_HALO - (CONV_W - 1) + j
        conv = conv + ext[o:o + L] * cw_ref[j:j + 1, :]
    qk = conv * jax.nn.sigmoid(conv)

    t_id = lax.broadcasted_iota(jnp.int32, (L, L), 0)
    s_id = lax.broadcasted_iota(jnp.int32, (L, L), 1)
    causal = t_id >= s_id
    gcol = gcol_ref[0, 0]
    grow = grow_ref[0, 0]
    for h in range(M_HEADS):
        hd = slice(h * M_HD, (h + 1) * M_HD)
        q = qk[:, hd]
        k = qk[:, M_W + h * M_HD:M_W + (h + 1) * M_HD] * (M_HD ** -0.5)
        v = v_ref[0, :, hd].astype(bf16)
        ig_r = grow[h:h + 1, :]
        ig_c = gcol[:, h:h + 1]
        lf_r = _log_sigmoid(grow[M_HEADS + h:M_HEADS + h + 1, :])
        lf_c = _log_sigmoid(gcol[:, M_HEADS + h:M_HEADS + h + 1])
        b_c = jnp.sum(jnp.where(causal, lf_r, 0.0), axis=1, keepdims=True)
        b_r = jnp.sum(jnp.where(t_id <= s_id, lf_c, 0.0), axis=0, keepdims=True)
        m_prev = m_ref[h]
        dmat = jnp.where(causal, b_c - b_r + ig_r, -jnp.inf)
        inter = b_c + m_prev
        m_t = jnp.maximum(inter, jnp.max(dmat, axis=1, keepdims=True))
        w_intra = jnp.exp(dmat - m_t)
        w_inter = jnp.exp(inter - m_t)
        qb = q.astype(bf16)
        s = lax.dot_general(qb, k.astype(bf16), _NT, preferred_element_type=f32) * w_intra
        num = (jnp.dot(s.astype(bf16), v, preferred_element_type=f32)
               + w_inter * jnp.dot(qb, c_ref[h].astype(bf16), preferred_element_type=f32))
        den = jnp.sum(s, axis=1, keepdims=True) + w_inter * jnp.sum(q * n_ref[h], axis=1, keepdims=True)
        hh = num / jnp.maximum(jnp.abs(den), jnp.exp(-m_t))
        out_ref[0, :, hd] = jax.nn.sigmoid(o_ref[0, :, hd]) * hh
        m_new = m_t[L - 1:L]
        b_last = b_c[L - 1:L]
        w_s = jnp.exp(b_last - b_c + ig_c - m_new)
        w_p = jnp.exp(b_last + m_prev - m_new)
        kw = k * w_s
        c_ref[h] = w_p * c_ref[h] + jnp.dot(kw.T.astype(bf16), v, preferred_element_type=f32)
        n_ref[h] = w_p * n_ref[h] + jnp.sum(kw, axis=0, keepdims=True)
        m_ref[h] = m_new

    @pl.when(c == pl.num_programs(1) - 1)
    def _():
        c_out[0] = c_ref[...]
        n_out[0] = n_ref[...]
        m_out[0] = m_ref[...]


def mlstm_prompt(zqk, zv, zo, zif, conv_w, conv_b, b_if):
    B, T, _ = zqk.shape
    L = MLSTM_L
    assert T % L == 0
    nc = T // L
    f32 = jnp.float32
    gif = zif + b_if
    gcol = gif.reshape(B, nc, L, 2 * M_HEADS)
    grow = gcol.transpose(0, 1, 3, 2)
    halo0 = jnp.zeros((B, CONV_HALO, 2 * M_W), f32)
    out, C, n, m = pl.pallas_call(
        _mlstm_prompt_kernel,
        out_shape=(jax.ShapeDtypeStruct((B, T, M_W), f32),
                   jax.ShapeDtypeStruct((B, M_HEADS, M_HD, M_HD), f32),
                   jax.ShapeDtypeStruct((B, M_HEADS, 1, M_HD), f32),
                   jax.ShapeDtypeStruct((B, M_HEADS, 1, 1), f32)),
        grid=(B, nc),
        in_specs=[pl.BlockSpec((1, L, 2 * M_W), lambda b, c: (b, c, 0)),
                  pl.BlockSpec((1, L, 2 * M_W), lambda b, c: (b, jnp.maximum(c - 1, 0), 0)),
                  pl.BlockSpec((1, CONV_HALO, 2 * M_W), lambda b, c: (b, 0, 0)),
                  pl.BlockSpec((1, L, M_W), lambda b, c: (b, c, 0)),
                  pl.BlockSpec((1, L, M_W), lambda b, c: (b, c, 0)),
                  pl.BlockSpec((1, 1, L, 2 * M_HEADS), lambda b, c: (b, c, 0, 0)),
                  pl.BlockSpec((1, 1, 2 * M_HEADS, L), lambda b, c: (b, c, 0, 0)),
                  pl.BlockSpec((CONV_W, 2 * M_W), lambda b, c: (0, 0)),
                  pl.BlockSpec((1, 2 * M_W), lambda b, c: (0, 0))],
        out_specs=(pl.BlockSpec((1, L, M_W), lambda b, c: (b, c, 0)),
                   pl.BlockSpec((1, M_HEADS, M_HD, M_HD), lambda b, c: (b, 0, 0, 0)),
                   pl.BlockSpec((1, M_HEADS, 1, M_HD), lambda b, c: (b, 0, 0, 0)),
                   pl.BlockSpec((1, M_HEADS, 1, 1), lambda b, c: (b, 0, 0, 0))),
        scratch_shapes=[pltpu.VMEM((M_HEADS, M_HD, M_HD), f32), pltpu.VMEM((M_HEADS, 1, M_HD), f32),
                        pltpu.VMEM((M_HEADS, 1, 1), f32)],
        compiler_params=pltpu.CompilerParams(dimension_semantics=("arbitrary", "arbitrary"),
                                             vmem_limit_bytes=VMEM_LIMIT_BYTES),
        name="mlstm_prompt",
    )(zqk, zqk, halo0, zv, zo, gcol, grow, conv_w, conv_b[None])
    return out, C, n.reshape(B, M_HEADS, M_HD), m.reshape(B, M_HEADS)


def mlstm_chunk(carry, inp):
    C, n, m = carry
    q, k, v, ig, lf = inp
    L = q.shape[2]
    b = jnp.cumsum(lf, axis=-1)
    causal = jnp.tril(jnp.ones((L, L), dtype=bool))
    dmat = jnp.where(causal, b[..., :, None] - b[..., None, :] + ig[..., None, :], -jnp.inf)
    inter = b + m[..., None]
    m_t = jnp.maximum(inter, dmat.max(axis=-1))
    w_intra = jnp.exp(dmat - m_t[..., None])
    w_inter = jnp.exp(inter - m_t)
    s = jnp.einsum('bhtd,bhsd->bhts', q, k) * w_intra
    num = jnp.einsum('bhts,bhsv->bhtv', s, v) + w_inter[..., None] * jnp.einsum('bhtd,bhdv->bhtv', q, C)
    den = s.sum(-1) + w_inter * jnp.einsum('bhtd,bhd->bht', q, n)
    h = num / jnp.maximum(jnp.abs(den), jnp.exp(-m_t))[..., None]
    m_new = m_t[..., -1]
    w_s = jnp.exp(b[..., -1:] - b + ig - m_new[..., None])
    w_p = jnp.exp(b[..., -1] + m - m_new)
    C_new = w_p[..., None, None] * C + jnp.einsum('bhs,bhsd,bhsv->bhdv', w_s, k, v)
    n_new = w_p[..., None] * n + jnp.einsum('bhs,bhsd->bhd', w_s, k)
    return (C_new, n_new, m_new), h


def mlstm_mix(zqk, zv, zo, zif, buf0, C0, n0, m0, conv_w, conv_b, b_if, chunk):
    B, T, _ = zqk.shape
    full = jnp.concatenate([buf0, zqk], axis=1)
    conv = conv_b
    for j in range(CONV_W):
        conv = conv + full[:, j:j + T] * conv_w[j]
    qk = jax.nn.silu(conv)

    def heads(a):
        return a.reshape(B, T, M_HEADS, M_HD).transpose(0, 2, 1, 3)

    q = heads(qk[..., :M_W])
    k = heads(qk[..., M_W:]) * (M_HD ** -0.5)
    v = heads(zv)
    gif = zif + b_if
    ig = gif[..., :M_HEADS].transpose(0, 2, 1)
    lf = jax.nn.log_sigmoid(gif[..., M_HEADS:]).transpose(0, 2, 1)
    nc = T // chunk

    def to_chunks(a):
        return jnp.moveaxis(a.reshape(B, M_HEADS, nc, chunk, *a.shape[3:]), 2, 0)

    (C, n, m), h = lax.scan(mlstm_chunk, (C0, n0, m0),
                            (to_chunks(q), to_chunks(k), to_chunks(v), to_chunks(ig), to_chunks(lf)))
    h = jnp.moveaxis(h, 0, 2).reshape(B, M_HEADS, T, M_HD).transpose(0, 2, 1, 3).reshape(B, T, M_W)
    out = jax.nn.sigmoid(zo) * h
    return out, (C, n, m, full[:, T:])


PEER_COMBOS = 2 * PEER_HEADS
PEER_KEY_ROWS = 8
PEER_TILE = PEER_KEY_ROWS * N_KEYS
PEER_TS_ROWS = 24
LANES = 128
_NT = (((1,), (1,)), ((), ()))


def _peer_topk_kernel(q_ref, keys_ref, s_ref, e0_ref, e1_ref, tau_ref, ts_ref):
    c = pl.program_id(1)
    tt = q_ref.shape[0]
    s = lax.dot_general(keys_ref[0], q_ref[...].astype(jnp.bfloat16), _NT,
                        preferred_element_type=jnp.float32)
    s_ref[c] = s
    key_id = lax.broadcasted_iota(jnp.int32, s.shape, 0)
    work = s
    rows = []
    for _ in range(PEER_TOPK + 1):
        m = jnp.max(work, axis=0, keepdims=True)
        first = jnp.min(jnp.where(work == m, key_id, N_KEYS), axis=0, keepdims=True)
        work = jnp.where(key_id == first, -jnp.inf, work)
        rows.append(m)
    rows.append(jnp.full((PEER_TS_ROWS - PEER_TOPK - 1, tt), -jnp.inf, jnp.float32))
    ts_ref[c] = jnp.concatenate(rows, axis=0)

    @pl.when(c == PEER_COMBOS - 1)
    def _():
        for h in range(PEER_HEADS):
            t0 = ts_ref[2 * h]
            t1 = ts_ref[2 * h + 1]
            pieces = [t0[0:1] + t1] + [t0[a:a + 1] + t1[0:8] for a in range(1, 8)] + [t0[8:24] + t1[0:1]]
            cand = jnp.concatenate(pieces, axis=0)
            top = t0[0:1] + t1[0:1]
            v16 = top
            v17 = top
            z = jnp.zeros_like(top)
            seen = jnp.zeros_like(top)
            for _ in range(PEER_TOPK + 1):
                m = jnp.max(cand, axis=0, keepdims=True)
                eq = cand == m
                cnt = jnp.sum(jnp.where(eq, 1.0, 0.0), axis=0, keepdims=True)
                active = seen < PEER_TOPK
                take = jnp.minimum(cnt, PEER_TOPK - seen)
                v16 = jnp.where(active, m, v16)
                v17 = jnp.where(seen < PEER_TOPK + 1, m, v17)
                z = z + jnp.where(active, take * jnp.exp(m - top), 0.0)
                seen = seen + cnt
                cand = jnp.where(eq, -jnp.inf, cand)
            tau_ref[h:h + 1, :] = 0.5 * v16 + 0.5 * v17
            e0_ref[h] = jnp.exp(s_ref[2 * h] - t0[0:1]) / z
            e1_ref[h] = jnp.exp(s_ref[2 * h + 1] - t1[0:1])


def peer_scores(q, sub_keys, tt):
    n = q.shape[0]
    assert n % tt == 0
    keys = sub_keys.reshape(PEER_COMBOS, N_KEYS, PEER_QDIM // 2).astype(jnp.bfloat16)
    f32 = jnp.float32
    return pl.pallas_call(
        _peer_topk_kernel,
        out_shape=(jax.ShapeDtypeStruct((PEER_COMBOS, N_KEYS, n), f32),
                   jax.ShapeDtypeStruct((PEER_HEADS, N_KEYS, n), f32),
                   jax.ShapeDtypeStruct((PEER_HEADS, N_KEYS, n), f32),
                   jax.ShapeDtypeStruct((PEER_HEADS, n), f32)),
        grid=(n // tt, PEER_COMBOS),
        in_specs=[pl.BlockSpec((tt, PEER_QDIM // 2), lambda i, c: (i, c)),
                  pl.BlockSpec((1, N_KEYS, PEER_QDIM // 2), lambda i, c: (c, 0, 0))],
        out_specs=(pl.BlockSpec((PEER_COMBOS, N_KEYS, tt), lambda i, c: (0, 0, i)),
                   pl.BlockSpec((PEER_HEADS, N_KEYS, tt), lambda i, c: (0, 0, i)),
                   pl.BlockSpec((PEER_HEADS, N_KEYS, tt), lambda i, c: (0, 0, i)),
                   pl.BlockSpec((PEER_HEADS, tt), lambda i, c: (0, i))),
        scratch_shapes=[pltpu.VMEM((PEER_COMBOS, PEER_TS_ROWS, tt), f32)],
        compiler_params=pltpu.CompilerParams(dimension_semantics=("arbitrary", "arbitrary"),
                                             vmem_limit_bytes=VMEM_LIMIT_BYTES),
        name="peer_topk",
    )(q, keys)


def _peer_dense_kernel(xb_ref, h_ref, u_ref, vt_ref, s0_ref, ez_ref, s_ref, e1_ref, tau_ref, g_ref, b_ref,
                       o_ref, acc_ref, a_ref, w_ref):
    e = pl.program_id(1)
    tt = xb_ref.shape[0]

    @pl.when(e == 0)
    def _():
        acc_ref[...] = jnp.zeros_like(acc_ref)

    a_ref[...] = lax.dot_general(u_ref[...], xb_ref[...], _NT, preferred_element_type=jnp.float32)
    for r in range(PEER_KEY_ROWS):
        rows = slice(r * N_KEYS, (r + 1) * N_KEYS)
        for t in range(tt // LANES):
            tok = slice(t * LANES, (t + 1) * LANES)
            gate = jnp.zeros((N_KEYS, LANES), jnp.float32)
            for h in range(PEER_HEADS):
                need = tau_ref[h:h + 1, tok] - s0_ref[2 * h, r:r + 1, tok]
                picked = jnp.where(s_ref[2 * h + 1, :, tok] >= need, e1_ref[h, :, tok], 0.0)
                gate = gate + picked * ez_ref[h, r:r + 1, tok]
            ar = a_ref[rows, tok]
            act = 0.5 * ar * (1.0 + lax.erf(ar * (2.0 ** -0.5)))
            w_ref[rows, tok] = (gate * act).astype(jnp.bfloat16)
    acc_ref[...] += jnp.dot(vt_ref[...], w_ref[...], preferred_element_type=jnp.float32)

    @pl.when(e == pl.num_programs(1) - 1)
    def _():
        r = ALPHA * h_ref[...] + acc_ref[...].T
        mu = jnp.mean(r, axis=-1, keepdims=True)
        d = r - mu
        var = jnp.mean(d * d, axis=-1, keepdims=True)
        o_ref[...] = d * lax.rsqrt(var + LN_EPS) * g_ref[...] + b_ref[...]


def peer_tail(h, hb, q, sub_keys, u_bf, vt_bf, ln_g, ln_b, tt):
    n, d = h.shape
    s, e0z, e1, tau = peer_scores(q, sub_keys, tt)
    n_exp = u_bf.shape[0]
    return pl.pallas_call(
        _peer_dense_kernel,
        out_shape=jax.ShapeDtypeStruct((n, d), jnp.float32),
        grid=(n // tt, n_exp // PEER_TILE),
        in_specs=[pl.BlockSpec((tt, d), lambda i, e: (i, 0)),
                  pl.BlockSpec((tt, d), lambda i, e: (i, 0)),
                  pl.BlockSpec((PEER_TILE, d), lambda i, e: (e, 0)),
                  pl.BlockSpec((d, PEER_TILE), lambda i, e: (0, e)),
                  pl.BlockSpec((PEER_COMBOS, PEER_KEY_ROWS, tt), lambda i, e: (0, e, i)),
                  pl.BlockSpec((PEER_HEADS, PEER_KEY_ROWS, tt), lambda i, e: (0, e, i)),
                  pl.BlockSpec((PEER_COMBOS, N_KEYS, tt), lambda i, e: (0, 0, i)),
                  pl.BlockSpec((PEER_HEADS, N_KEYS, tt), lambda i, e: (0, 0, i)),
                  pl.BlockSpec((PEER_HEADS, tt), lambda i, e: (0, i)),
                  pl.BlockSpec((1, d), lambda i, e: (0, 0)),
                  pl.BlockSpec((1, d), lambda i, e: (0, 0))],
        out_specs=pl.BlockSpec((tt, d), lambda i, e: (i, 0)),
        scratch_shapes=[pltpu.VMEM((d, tt), jnp.float32), pltpu.VMEM((PEER_TILE, tt), jnp.float32),
                        pltpu.VMEM((PEER_TILE, tt), jnp.bfloat16)],
        compiler_params=pltpu.CompilerParams(dimension_semantics=("arbitrary", "arbitrary"),
                                             vmem_limit_bytes=VMEM_LIMIT_BYTES),
        name="peer_dense",
    )(hb, h, u_bf, vt_bf, s, e0z, s, e1, tau, ln_g[None], ln_b[None])


def _out_proj_kernel(x_ref, nsa_ref, m_ref, wn_ref, wm_ref, g_ref, b_ref, wq_ref, h_ref, hb_ref, q_ref):
    f32, bf16 = jnp.float32, jnp.bfloat16
    r = (ALPHA * x_ref[...] + jnp.dot(nsa_ref[...].astype(bf16), wn_ref[...], preferred_element_type=f32)
         + jnp.dot(m_ref[...].astype(bf16), wm_ref[...], preferred_element_type=f32))
    mu = jnp.mean(r, axis=-1, keepdims=True)
    d = r - mu
    var = jnp.mean(d * d, axis=-1, keepdims=True)
    h = d * lax.rsqrt(var + LN_EPS) * g_ref[...] + b_ref[...]
    h_ref[...] = h
    hb = h.astype(bf16)
    hb_ref[...] = hb
    q_ref[...] = jnp.dot(hb, wq_ref[...], preferred_element_type=f32)


def out_proj_fused(x, o_nsa, o_m, w_out, ln_g, ln_b, w_pq, tm):
    n, d = x.shape
    assert n % tm == 0
    bf16 = jnp.bfloat16
    nq = w_pq.shape[1]

    def rows(width):
        return pl.BlockSpec((tm, width), lambda i: (i, 0))

    def whole(a):
        return pl.BlockSpec(a.shape, lambda i: (0, 0))

    wn = w_out[:NSA_QW].astype(bf16)
    wm = w_out[NSA_QW:].astype(bf16)
    wq = w_pq.astype(bf16)
    g, b = ln_g[None], ln_b[None]
    return pl.pallas_call(
        _out_proj_kernel,
        out_shape=(jax.ShapeDtypeStruct((n, d), jnp.float32), jax.ShapeDtypeStruct((n, d), bf16),
                   jax.ShapeDtypeStruct((n, nq), jnp.float32)),
        grid=(n // tm,),
        in_specs=[rows(d), rows(NSA_QW), rows(M_W), whole(wn), whole(wm), whole(g), whole(b), whole(wq)],
        out_specs=(rows(d), rows(d), rows(nq)),
        compiler_params=pltpu.CompilerParams(dimension_semantics=("arbitrary",),
                                             vmem_limit_bytes=VMEM_LIMIT_BYTES),
        name="out_proj",
    )(x, o_nsa, o_m, wn, wm, g, b, wq)


def block_tail(x, o_nsa, o_m, w_out, ln_g, ln_b, w_pq, sub_keys, u_bf, vt_bf, tt):
    lead = x.shape[:-1]
    h, hb, q = out_proj_fused(x.reshape(-1, D_MODEL), o_nsa.reshape(-1, NSA_QW), o_m.reshape(-1, M_W),
                              w_out, ln_g[0], ln_b[0], w_pq, tt)
    return peer_tail(h, hb, q, sub_keys, u_bf, vt_bf, ln_g[1], ln_b[1], tt).reshape(*lead, D_MODEL)


def prompt_mix(x, w_in, pe, w1, b1, w2, conv_w, conv_b, b_if):
    B, S, _ = x.shape
    z = in_proj_fused(x, w_in, jnp.arange(S), 512, True)
    chunk_w = CMP_STRIDE * NSA_KVW

    def compressed(rows, c):
        proj = chunk_projection(rows.reshape(B * (S // CMP_STRIDE), chunk_w), w1[c])
        return compress_from_projection(proj.reshape(B, S // CMP_STRIDE, -1), pe[c], w1[c], b1[c], w2[c])

    kc = compressed(z["k_cmp"], 0)
    vc = compressed(z["v_cmp"], 1)
    zgate = z["zgate"].reshape(B, S, GATE_W)
    o_nsa = nsa_prompt_attention(z["q"].reshape(B, S, NSA_QW), kc, vc, z["k_slc_bf"], z["v_slc_bf"],
                                 z["k_win_bf"], z["v_win_bf"], zgate[..., :IN_SPLITS[2]])
    zqk = z["zqk"].reshape(B, S, 2 * M_W)
    o_m, C, n, m = mlstm_prompt(zqk, z["zv"].reshape(B, S, M_W), z["zo"].reshape(B, S, M_W),
                                zgate[..., IN_SPLITS[2]:], conv_w, conv_b, b_if)
    buf = zqk[:, S - (CONV_W - 1):]
    wl = min(WINDOW, S)
    k_cmp, v_cmp, k_slc, v_slc, k_win, v_win = [
        z[k].reshape(B, S, NSA_KV_HEADS, NSA_HD) for k in ("k_cmp", "v_cmp", "k_slc", "v_slc", "k_win", "v_win")]
    return (o_nsa, o_m), (k_cmp, v_cmp, k_slc, v_slc, k_win[:, S - wl:], v_win[:, S - wl:], C, n, m, buf)


def sample_mix(x, kc_pool, vc_pool, ks_pool, vs_pool, kw_buf, vw_buf, C0, n0, m0, buf0, page_table,
               w_in, pe, w1, b1, w2, conv_w, conv_b, b_if):
    B, T, _ = x.shape
    past = page_table.shape[1] * PAGE_SIZE
    pos = past + jnp.arange(T)
    z = in_proj_fused(x, w_in, pos, B * T, False)
    q = z["q"].astype(jnp.float32).reshape(B, T, NSA_HEADS, NSA_HD) * (1.0 / ATTN_SCALE)
    k_cmp, v_cmp, k_slc, v_slc, k_win, v_win = [
        z[k].reshape(B, T, NSA_KV_HEADS, NSA_HD) for k in ("k_cmp", "v_cmp", "k_slc", "v_slc", "k_win", "v_win")]
    zgate = z["zgate"].reshape(B, T, GATE_W)
    gates = jax.nn.sigmoid(zgate[..., :IN_SPLITS[2]]).reshape(B, T, NSA_HEADS, 3)
    zqk, zv, zo, zif = (z["zqk"].reshape(B, T, 2 * M_W), z["zv"].reshape(B, T, M_W), z["zo"].reshape(B, T, M_W),
                        zgate[..., IN_SPLITS[2]:])

    def with_past(pool, new):
        old = pool[page_table].reshape(B, past, NSA_KV_HEADS, NSA_HD)
        return jnp.concatenate([old, new], axis=1)

    assert (past + T) // CMP_STRIDE == past // CMP_STRIDE

    def compressed(pool, c):
        pages = pool.transpose(0, 2, 3, 1)[page_table]
        return compress_from_projection(page_projection(pages, w1[c]), pe[c], w1[c], b1[c], w2[c])

    o_cmp, p = cmp_attend(q, pos, compressed(kc_pool, 0), compressed(vc_pool, 1))
    n_sel = -(-(past + T) // SEL_BLOCK)
    sel = select_blocks(p, pos, n_sel)
    kb = to_blocks(with_past(ks_pool, k_slc), n_sel)
    vb = to_blocks(with_past(vs_pool, v_slc), n_sel)
    wb = kw_buf.shape[1]
    kw = jnp.concatenate([kw_buf, k_win], axis=1)
    vw = jnp.concatenate([vw_buf, v_win], axis=1)
    kpos = past - wb + jnp.arange(wb + T)
    o_sel = sel_attend(q, pos, sel, kb, vb)
    o_win = win_attend(q, pos, kw, vw, kpos)
    o_nsa = nsa_combine(gates, o_cmp, o_sel, o_win)
    o_m, (C, n, m, buf) = mlstm_mix(zqk, zv, zo, zif, buf0, C0, n0, m0, conv_w, conv_b, b_if, T)
    return (o_nsa, o_m), (k_cmp, v_cmp, k_slc, v_slc, kw[:, T:], vw[:, T:], C, n, m, buf)


def kernel(x_prompt, x_sample, cache_k_cmp, cache_v_cmp, cache_k_slc, cache_v_slc, cache_k_win, cache_v_win,
           state_C, state_n, state_m, state_conv, page_table, w_in, w_out, w_phi1, b_phi1, w_phi2, pe_cmp,
           conv_w, conv_b, b_if, ln_g, ln_b, w_pq, sub_keys, u_tab, v_tab):
    l = 0
    mix_p, st_p = prompt_mix(x_prompt, w_in[l], pe_cmp[l], w_phi1[l], b_phi1[l], w_phi2[l],
                             conv_w[l], conv_b[l], b_if[l])
    mix_s, st_s = sample_mix(x_sample, cache_k_cmp[l], cache_v_cmp[l], cache_k_slc[l], cache_v_slc[l],
                             cache_k_win[l], cache_v_win[l], state_C[l], state_n[l], state_m[l],
                             state_conv[l], page_table, w_in[l], pe_cmp[l], w_phi1[l], b_phi1[l],
                             w_phi2[l], conv_w[l], conv_b[l], b_if[l])
    u_bf = u_tab[l].astype(jnp.bfloat16)
    vt_bf = v_tab[l].astype(jnp.bfloat16).T
    xp = block_tail(x_prompt, *mix_p, w_out[l], ln_g[l], ln_b[l], w_pq[l], sub_keys[l], u_bf, vt_bf, 512)
    xs = block_tail(x_sample, *mix_s, w_out[l], ln_g[l], ln_b[l], w_pq[l], sub_keys[l], u_bf, vt_bf, 128)
    return (xp, xs) + tuple(a[None] for a in st_p) + tuple(a[None] for a in st_s)
```

```python
import functools

import jax
import jax.numpy as jnp
import numpy as np
from jax import lax
from jax.experimental import pallas as pl
from jax.experimental.pallas import tpu as pltpu

D_MODEL = 1024
DEPTH = 1
PAGE_SIZE = 128
NSA_HEADS = 8
NSA_KV_HEADS = 2
NSA_GROUP = NSA_HEADS // NSA_KV_HEADS
NSA_HD = 64
NSA_QW = NSA_HEADS * NSA_HD
NSA_KVW = NSA_KV_HEADS * NSA_HD
CMP_BLOCK = 32
CMP_STRIDE = 16
SEL_BLOCK = 64
SEL_TOP = 16
WINDOW = 512
Q_BLOCK = 64
ATTN_SCALE = NSA_HD ** -0.5
ROPE_THETA = 10000.0
M_HEADS = 4
M_HD = 128
M_W = M_HEADS * M_HD
M_CHUNK = 64
CONV_W = 4
PEER_HEADS = 8
N_KEYS = 128
PEER_TOPK = 16
PEER_QDIM = 256
PEER_BLOCK = 128
IN_SPLITS = (NSA_QW, 6 * NSA_KVW, 3 * NSA_HEADS, 2 * M_W, M_W, M_W, 2 * M_HEADS)
LN_EPS = 1e-5
ALPHA = (2 * DEPTH) ** 0.25

VMEM_LIMIT_BYTES = 56 * 1024 * 1024


def _mm_kernel(x_ref, w_ref, o_ref):
    o_ref[...] = jnp.dot(x_ref[...].astype(jnp.bfloat16), w_ref[...], preferred_element_type=jnp.float32)


def pallas_matmul(x, w, tm=512):
    M, K = x.shape
    N = w.shape[1]
    tm = min(tm, M)
    assert M % tm == 0
    return pl.pallas_call(
        _mm_kernel,
        out_shape=jax.ShapeDtypeStruct((M, N), jnp.float32),
        grid=(M // tm,),
        in_specs=[pl.BlockSpec((tm, K), lambda i: (i, 0)), pl.BlockSpec((K, N), lambda i: (0, 0))],
        out_specs=pl.BlockSpec((tm, N), lambda i: (i, 0)),
        compiler_params=pltpu.CompilerParams(dimension_semantics=("arbitrary",),
                                             vmem_limit_bytes=VMEM_LIMIT_BYTES),
        name="proj_matmul",
    )(x, w.astype(jnp.bfloat16))


def mm3(x, w):
    lead = x.shape[:-1]
    return pallas_matmul(x.reshape(-1, x.shape[-1]), w).reshape(*lead, w.shape[1])


def layer_norm(x, g, b):
    mu = x.mean(-1, keepdims=True)
    var = jnp.square(x - mu).mean(-1, keepdims=True)
    return (x - mu) * lax.rsqrt(var + LN_EPS) * g + b


def rope(x, pos):
    half = x.shape[-1] // 2
    inv = ROPE_THETA ** (-jnp.arange(half, dtype=jnp.float32) / half)
    ang = pos.astype(jnp.float32)[:, None] * inv[None, :]
    cos = jnp.cos(ang)[:, None, :]
    sin = jnp.sin(ang)[:, None, :]
    x1, x2 = x[..., :half], x[..., half:]
    return jnp.concatenate([x1 * cos - x2 * sin, x2 * cos + x1 * sin], axis=-1)


def split_in_proj(x, w_in):
    z = mm3(x, w_in)
    cuts = [int(c) for c in np.cumsum(IN_SPLITS)[:-1]]
    return jnp.split(z, cuts, axis=-1)


_IN_OFF = np.concatenate([[0], np.cumsum(IN_SPLITS)])
_IN_ORDER = (0, 1, 3, 4, 5, 2, 6)
_N_KV_ROWS = 6
_KV_BF16 = (2, 3, 4, 5)
GATE_W = IN_SPLITS[2] + IN_SPLITS[6]


def _rope_pairs(x, cos, sin_signed):
    half = NSA_HD // 2
    lane = lax.broadcasted_iota(jnp.int32, x.shape, 1)
    partner = jnp.where(lane % NSA_HD < half, pltpu.roll(x, LANES - half, 1), pltpu.roll(x, half, 1))
    return x * cos + partner * sin_signed


def _in_proj_kernel(x_ref, w_ref, cos_ref, sin_ref, q_ref, *rest, kv_major):
    kv_refs = rest[:_N_KV_ROWS]
    rest = rest[_N_KV_ROWS:]
    if kv_major:
        bf_refs, rest = rest[:len(_KV_BF16)], rest[len(_KV_BF16):]
    zqk_ref, zv_ref, zo_ref, zgate_ref = rest
    z = jnp.dot(x_ref[...].astype(jnp.bfloat16), w_ref[...], preferred_element_type=jnp.float32)
    cos = cos_ref[...]
    sin = sin_ref[...]
    for g in range(NSA_QW // LANES):
        sl = slice(g * LANES, (g + 1) * LANES)
        q_ref[:, sl] = (_rope_pairs(z[:, sl], cos, sin) * ATTN_SCALE).astype(jnp.bfloat16)
    for r in range(_N_KV_ROWS):
        row = z[:, NSA_QW + r * NSA_KVW:NSA_QW + (r + 1) * NSA_KVW]
        if r % 2 == 0:
            row = _rope_pairs(row, cos, sin)
        kv_refs[r][...] = row
        if kv_major and r in _KV_BF16:
            dst = bf_refs[_KV_BF16.index(r)]
            for n in range(NSA_KV_HEADS):
                dst[0, n] = row[:, n * NSA_HD:(n + 1) * NSA_HD].astype(jnp.bfloat16)
    o = NSA_QW + _N_KV_ROWS * NSA_KVW
    zqk_ref[...] = z[:, o:o + 2 * M_W]
    zv_ref[...] = z[:, o + 2 * M_W:o + 3 * M_W]
    zo_ref[...] = z[:, o + 3 * M_W:o + 4 * M_W]
    zgate_ref[...] = z[:, o + 4 * M_W:o + 4 * M_W + GATE_W]


def in_proj_fused(x, w_in, pos, tm, kv_major):
    B, T, D = x.shape
    M = B * T
    assert M % tm == 0 and NSA_KVW == LANES and (not kv_major or T % tm == 0)
    f32, bf16 = jnp.float32, jnp.bfloat16
    w = jnp.concatenate([w_in[:, _IN_OFF[i]:_IN_OFF[i + 1]] for i in _IN_ORDER], axis=1).astype(bf16)
    half = NSA_HD // 2
    inv = ROPE_THETA ** (-jnp.arange(half, dtype=f32) / half)
    ang = pos.astype(f32)[:, None] * inv[None, :]
    cos = jnp.tile(jnp.cos(ang), (B, 2 * LANES // NSA_HD))
    sin = jnp.tile(jnp.concatenate([-jnp.sin(ang), jnp.sin(ang)], axis=1), (B, LANES // NSA_HD))
    n_w = w.shape[1]

    def rows(width):
        return pl.BlockSpec((tm, width), lambda i: (i, 0))

    out_shape = [jax.ShapeDtypeStruct((M, NSA_QW), bf16)] + [jax.ShapeDtypeStruct((M, NSA_KVW), f32)] * _N_KV_ROWS
    out_specs = [rows(NSA_QW)] + [rows(NSA_KVW)] * _N_KV_ROWS
    if kv_major:
        per_seq = T // tm
        out_shape += [jax.ShapeDtypeStruct((B, NSA_KV_HEADS, T, NSA_HD), bf16)] * len(_KV_BF16)
        out_specs += [pl.BlockSpec((1, NSA_KV_HEADS, tm, NSA_HD),
                                   lambda i: (i // per_seq, 0, i % per_seq, 0))] * len(_KV_BF16)
    out_shape += [jax.ShapeDtypeStruct((M, 2 * M_W), f32), jax.ShapeDtypeStruct((M, M_W), f32),
                  jax.ShapeDtypeStruct((M, M_W), f32), jax.ShapeDtypeStruct((M, GATE_W), f32)]
    out_specs += [rows(2 * M_W), rows(M_W), rows(M_W), rows(GATE_W)]
    outs = pl.pallas_call(
        functools.partial(_in_proj_kernel, kv_major=kv_major),
        out_shape=tuple(out_shape),
        grid=(M // tm,),
        in_specs=[rows(D), pl.BlockSpec((D, n_w), lambda i: (0, 0)), rows(LANES), rows(LANES)],
        out_specs=tuple(out_specs),
        compiler_params=pltpu.CompilerParams(dimension_semantics=("arbitrary",),
                                             vmem_limit_bytes=VMEM_LIMIT_BYTES),
        name="in_proj",
    )(x.reshape(M, D), w, cos, sin)
    names = ["q", "k_cmp", "v_cmp", "k_slc", "v_slc", "k_win", "v_win"]
    if kv_major:
        names += ["k_slc_bf", "v_slc_bf", "k_win_bf", "v_win_bf"]
    names += ["zqk", "zv", "zo", "zgate"]
    return dict(zip(names, outs))


def nsa_project(zq, zkv, zg, pos):
    B, T, _ = zq.shape
    q = rope(zq.reshape(B, T, NSA_HEADS, NSA_HD), pos)
    kv = zkv.reshape(B, T, 6, NSA_KV_HEADS, NSA_HD)
    rows = (rope(kv[:, :, 0], pos), kv[:, :, 1], rope(kv[:, :, 2], pos), kv[:, :, 3],
            rope(kv[:, :, 4], pos), kv[:, :, 5])
    gates = jax.nn.sigmoid(zg).reshape(B, T, NSA_HEADS, 3)
    return q, rows, gates


def chunk_projection(chunks, w1):
    assert CMP_BLOCK == 2 * CMP_STRIDE
    return pallas_matmul(chunks, _expanded_w1(w1))


def _expanded_w1(w1):
    w1r = w1.reshape(2, CMP_STRIDE, NSA_HD, w1.shape[-1])
    wbig = jnp.einsum('hpdf,kn->pkdnhf', w1r, jnp.eye(NSA_KV_HEADS, dtype=w1.dtype))
    return wbig.reshape(CMP_STRIDE * NSA_KVW, 2 * NSA_KV_HEADS * w1.shape[-1])


def _page_projection_kernel(pg_ref, w_ref, o_ref, x_ref, t_ref):
    n_pages = pg_ref.shape[1]
    per_page = PAGE_SIZE // CMP_STRIDE

    def place(g, carry):
        t_ref[...] = pg_ref[0, g].reshape(NSA_KVW, PAGE_SIZE).T
        row0 = pl.multiple_of(g * per_page, per_page)
        for p in range(CMP_STRIDE):
            x_ref[pl.ds(row0, per_page), p * NSA_KVW:(p + 1) * NSA_KVW] = t_ref[pl.ds(p, per_page, stride=CMP_STRIDE), :]
        return carry

    lax.fori_loop(0, n_pages, place, 0)
    o_ref[0] = jnp.dot(x_ref[...].astype(jnp.bfloat16), w_ref[...], preferred_element_type=jnp.float32)


def page_projection(pages, w1):
    B, n_pages = pages.shape[:2]
    assert pages.shape[2:] == (NSA_KV_HEADS, NSA_HD, PAGE_SIZE) and NSA_KVW == LANES and PAGE_SIZE == LANES
    wbig = _expanded_w1(w1).astype(jnp.bfloat16)
    rows = n_pages * (PAGE_SIZE // CMP_STRIDE)
    return pl.pallas_call(
        _page_projection_kernel,
        out_shape=jax.ShapeDtypeStruct((B, rows, wbig.shape[1]), jnp.float32),
        grid=(B,),
        in_specs=[pl.BlockSpec((1, n_pages, NSA_KV_HEADS, NSA_HD, PAGE_SIZE), lambda b: (b, 0, 0, 0, 0)),
                  pl.BlockSpec(wbig.shape, lambda b: (0, 0))],
        out_specs=pl.BlockSpec((1, rows, wbig.shape[1]), lambda b: (b, 0, 0)),
        scratch_shapes=[pltpu.VMEM((rows, wbig.shape[0]), jnp.float32),
                        pltpu.VMEM((PAGE_SIZE, NSA_KVW), jnp.float32)],
        compiler_params=pltpu.CompilerParams(dimension_semantics=("arbitrary",),
                                             vmem_limit_bytes=VMEM_LIMIT_BYTES),
        name="page_projection",
    )(pages, wbig)


def compress_from_projection(proj, pe, w1, b1, w2):
    f = w1.shape[-1]
    bias = jnp.dot(pe.reshape(-1), w1, precision=lax.Precision.HIGHEST) + b1
    heads = []
    for n in range(NSA_KV_HEADS):
        first = proj[:, :-1, 2 * n * f:(2 * n + 1) * f]
        second = proj[:, 1:, (2 * n + 1) * f:(2 * n + 2) * f]
        heads.append(jax.nn.gelu(first + second + bias, approximate=False) @ w2)
    return jnp.stack(heads, axis=2)


def cmp_attend(q, qpos, kc, vc):
    B, T = q.shape[:2]
    qg = q.reshape(B, T, NSA_KV_HEADS, NSA_GROUP, NSA_HD)
    s = jnp.einsum('btngd,bcnd->btngc', qg, kc) * ATTN_SCALE
    nblk = kc.shape[1]
    blk_end = jnp.arange(nblk) * CMP_STRIDE + CMP_BLOCK - 1
    valid = (blk_end[None, :] <= qpos[:, None])[None, :, None, None, :]
    p = jax.nn.softmax(jnp.where(valid, s, -1e30), axis=-1) * valid
    o = jnp.einsum('btngc,bcnd->btngd', p, vc)
    return o.reshape(B, T, NSA_HEADS, NSA_HD), p


def select_blocks(p, qpos, n_sel):
    imp = p.sum(axis=3)
    R = SEL_BLOCK // CMP_STRIDE
    r = CMP_BLOCK // CMP_STRIDE
    nb = imp.shape[-1]
    right = n_sel * R + R - 1 - nb
    padded = jnp.pad(imp, ((0, 0), (0, 0), (0, 0), (r - 1, right)))
    score = padded[..., 0:(n_sel - 1) * R + 1:R]
    for o in range(1, R + r - 1):
        score = score + padded[..., o:o + (n_sel - 1) * R + 1:R]
    j = jnp.arange(n_sel)[None, :]
    cur = (qpos // SEL_BLOCK)[:, None]
    valid = (j * SEL_BLOCK <= qpos[:, None])[None, :, None, :]
    forced = ((j == 0) | (j == cur) | (j == cur - 1))[None, :, None, :]
    score = jnp.where(forced, jnp.inf, jnp.where(valid, score, -jnp.inf))
    idx = j[0]
    before = (score[..., None, :] > score[..., :, None]) | ((score[..., None, :] == score[..., :, None])
                                                          & (idx[None, :] < idx[:, None]))
    return before.sum(-1) < min(SEL_TOP, n_sel)


def sample_selected_attention(q, qpos, member, k_pool, v_pool, k_new, v_new, page_table):
    B, T = q.shape[:2]
    n_pages = page_table.shape[1]
    per_page = PAGE_SIZE // SEL_BLOCK
    assert member.shape[-1] == n_pages * per_page + 1 and T <= SEL_BLOCK
    kp = k_pool.transpose(0, 2, 3, 1)[page_table]
    vp = v_pool.transpose(0, 2, 3, 1)[page_table]
    qg = q.reshape(B, T, NSA_KV_HEADS, NSA_GROUP, NSA_HD)
    s_past = jnp.einsum('btngd,bpndk->bntgpk', qg, kp) * ATTN_SCALE
    s_new = jnp.einsum('btngd,bsnd->bntgs', qg, k_new) * ATTN_SCALE
    m = member.transpose(0, 2, 1, 3)
    m_past = jnp.repeat(m[..., :-1].reshape(B, NSA_KV_HEADS, T, n_pages, per_page), SEL_BLOCK, axis=-1)
    kpos = (jnp.arange(n_pages) * PAGE_SIZE)[:, None] + jnp.arange(PAGE_SIZE)[None, :]
    m_past = m_past & (kpos[None, None, None] <= qpos[None, None, :, None, None])
    new_pos = n_pages * PAGE_SIZE + jnp.arange(T)
    m_new = m[..., -1:] & (new_pos[None, None, None, :] <= qpos[None, None, :, None])
    logits = jnp.concatenate(
        [jnp.where(m_past[:, :, :, None], s_past, -jnp.inf).reshape(B, NSA_KV_HEADS, T, NSA_GROUP, -1),
         jnp.where(m_new[:, :, :, None], s_new, -jnp.inf)], axis=-1)
    pr = jax.nn.softmax(logits, axis=-1)
    pr_past = pr[..., :n_pages * PAGE_SIZE].reshape(B, NSA_KV_HEADS, T, NSA_GROUP, n_pages, PAGE_SIZE)
    o = (jnp.einsum('bntgpk,bpndk->bntgd', pr_past, vp)
         + jnp.einsum('bntgs,bsnd->bntgd', pr[..., n_pages * PAGE_SIZE:], v_new))
    return o.transpose(0, 2, 1, 3, 4).reshape(B, T, NSA_HEADS, NSA_HD)


def to_blocks(rows, n_sel):
    B, L, KV, hd = rows.shape
    rows = jnp.pad(rows, ((0, 0), (0, n_sel * SEL_BLOCK - L), (0, 0), (0, 0)))
    return rows.reshape(B, n_sel, SEL_BLOCK, KV, hd).transpose(0, 3, 1, 2, 4)


def take_rows(table, idx):
    return table[idx]


def sel_attend(q, qpos, sel, kb, vb):
    B, Tq = q.shape[:2]
    k = sel.shape[-1]
    sel_t = sel.transpose(0, 2, 1, 3)
    gather = jax.vmap(jax.vmap(take_rows))
    kg = gather(kb, sel_t).reshape(B, NSA_KV_HEADS, Tq, k * SEL_BLOCK, NSA_HD)
    vg = gather(vb, sel_t).reshape(B, NSA_KV_HEADS, Tq, k * SEL_BLOCK, NSA_HD)
    kpos = (sel_t[..., None] * SEL_BLOCK + jnp.arange(SEL_BLOCK)).reshape(B, NSA_KV_HEADS, Tq, k * SEL_BLOCK)
    qg = q.reshape(B, Tq, NSA_KV_HEADS, NSA_GROUP, NSA_HD).transpose(0, 2, 1, 3, 4)
    s = jnp.einsum('bntgd,bntsd->bntgs', qg, kg) * ATTN_SCALE
    mask = kpos[:, :, :, None, :] <= qpos[None, None, :, None, None]
    pr = jax.nn.softmax(jnp.where(mask, s, -jnp.inf), axis=-1)
    o = jnp.einsum('bntgs,bntsd->bntgd', pr, vg)
    return o.transpose(0, 2, 1, 3, 4).reshape(B, Tq, NSA_HEADS, NSA_HD)


def win_attend(q, qpos, k, v, kpos):
    B, Tq = q.shape[:2]
    qg = q.reshape(B, Tq, NSA_KV_HEADS, NSA_GROUP, NSA_HD)
    s = jnp.einsum('btngd,bsnd->btngs', qg, k) * ATTN_SCALE
    diff = qpos[:, None] - kpos[None, :]
    mask = ((diff >= 0) & (diff < WINDOW) & (kpos[None, :] >= 0))[None, :, None, None, :]
    pr = jax.nn.softmax(jnp.where(mask, s, -jnp.inf), axis=-1)
    o = jnp.einsum('btngs,bsnd->btngd', pr, v)
    return o.reshape(B, Tq, NSA_HEADS, NSA_HD)


def nsa_combine(gates, o_cmp, o_sel, o_win):
    B, T = gates.shape[:2]
    o = gates[..., 0:1] * o_cmp + gates[..., 1:2] * o_sel + gates[..., 2:3] * o_win
    return o.reshape(B, T, NSA_QW)


NSA_TQ = 128
NSA_CK = 512
MASKED = -1e30


def _softmax_rows(s):
    m = jnp.max(s, axis=-1, keepdims=True)
    e = jnp.exp(s - m)
    return e / jnp.sum(e, axis=-1, keepdims=True)


def _nsa_prompt_kernel(q_ref, kc_ref, vc_ref, ks_ref, vs_ref, kw_ref, vw_ref, zg_ref, msel_ref, exp_ref, o_ref):
    f32, bf16 = jnp.float32, jnp.bfloat16
    tq = NSA_TQ
    q0 = pl.program_id(2) * tq
    qb = q_ref[0]
    qs = jnp.concatenate([qb[:, g * NSA_HD:(g + 1) * NSA_HD] for g in range(NSA_GROUP)], axis=0)
    tpos = q0 + lax.broadcasted_iota(jnp.int32, (tq, 1), 0)

    def per_head(a):
        return jnp.concatenate([a] * NSA_GROUP, axis=0)

    s = lax.dot_general(qs, kc_ref[0, 0], _NT, preferred_element_type=f32)
    cblk = lax.broadcasted_iota(jnp.int32, (tq, 128), 1)
    cvalid = cblk * CMP_STRIDE + (CMP_BLOCK - 1) <= tpos
    s = s + per_head(jnp.where(cvalid, 0.0, MASKED))
    e = jnp.exp(s - jnp.max(s, axis=-1, keepdims=True)) * per_head(jnp.where(cvalid, 1.0, 0.0))
    l = jnp.sum(e, axis=-1, keepdims=True)
    p = e / jnp.where(l > 0.0, l, 1.0)
    o_cmp = jnp.dot(p.astype(bf16), vc_ref[0, 0], preferred_element_type=f32)

    imp = p[0:tq]
    for g in range(1, NSA_GROUP):
        imp = imp + p[g * tq:(g + 1) * tq]
    hi = imp.astype(bf16)
    r1 = imp - hi.astype(f32)
    mid = r1.astype(bf16)
    lo = (r1 - mid.astype(f32)).astype(bf16)
    msel = msel_ref[...]
    score = (jnp.dot(hi, msel, preferred_element_type=f32) + jnp.dot(mid, msel, preferred_element_type=f32)
             + jnp.dot(lo, msel, preferred_element_type=f32))
    n_sel = score.shape[1]
    j = lax.broadcasted_iota(jnp.int32, (tq, n_sel), 1)
    cur = tpos // SEL_BLOCK
    forced = (j == 0) | (j == cur) | (j == cur - 1)
    score = jnp.where(forced, jnp.inf, jnp.where(j * SEL_BLOCK <= tpos, score, -jnp.inf))
    rank = jnp.zeros((tq, n_sel), f32)
    for jp in range(n_sel):
        col = score[:, jp:jp + 1]
        before = (col > score) | ((col == score) & (j > jp))
        rank = rank + jnp.where(before, 1.0, 0.0)
    sel01 = jnp.where(rank < SEL_TOP, 1.0, 0.0).astype(bf16)

    ck = NSA_CK
    rows = NSA_GROUP * tq

    def sel_chunk(c, carry):
        m, l, acc = carry
        k0 = pl.multiple_of(c * ck, ck)
        s = lax.dot_general(qs, ks_ref[0, 0, pl.ds(k0, ck), :], _NT, preferred_element_type=f32)
        chosen = jnp.dot(sel01, exp_ref[c], preferred_element_type=f32)
        kpos = k0 + lax.broadcasted_iota(jnp.int32, (tq, ck), 1)
        ok = (chosen > 0.5) & (kpos <= tpos)
        s = s + per_head(jnp.where(ok, 0.0, MASKED))
        m_new = jnp.maximum(m, jnp.max(s, axis=-1, keepdims=True))
        a = jnp.exp(m - m_new)
        pr = jnp.exp(s - m_new)
        l = a * l + jnp.sum(pr, axis=-1, keepdims=True)
        acc = a * acc + jnp.dot(pr.astype(bf16), vs_ref[0, 0, pl.ds(k0, ck), :], preferred_element_type=f32)
        return m_new, l, acc

    init = (jnp.full((rows, 1), MASKED, f32), jnp.zeros((rows, 1), f32), jnp.zeros((rows, NSA_HD), f32))
    n_chunks = (q0 + tq + ck - 1) // ck
    _, l_sel, acc_sel = lax.fori_loop(0, n_chunks, sel_chunk, init)
    o_sel = acc_sel / l_sel

    w0 = pl.multiple_of(jnp.maximum(q0 - WINDOW, 0), tq)
    wl = WINDOW + tq
    s = lax.dot_general(qs, kw_ref[0, 0, pl.ds(w0, wl), :], _NT, preferred_element_type=f32)
    diff = tpos - (w0 + lax.broadcasted_iota(jnp.int32, (tq, wl), 1))
    s = s + per_head(jnp.where((diff >= 0) & (diff < WINDOW), 0.0, MASKED))
    o_win = jnp.dot(_softmax_rows(s).astype(bf16), vw_ref[0, 0, pl.ds(w0, wl), :], preferred_element_type=f32)

    gates = jax.nn.sigmoid(zg_ref[0, 0])
    for g in range(NSA_GROUP):
        r = slice(g * tq, (g + 1) * tq)
        o_ref[0, :, g * NSA_HD:(g + 1) * NSA_HD] = (gates[:, 3 * g:3 * g + 1] * o_cmp[r]
                                                    + gates[:, 3 * g + 1:3 * g + 2] * o_sel[r]
                                                    + gates[:, 3 * g + 2:3 * g + 3] * o_win[r])


def nsa_prompt_attention(qs, kc, vc, k_slc, v_slc, k_win, v_win, zg):
    B, S = qs.shape[:2]
    bf16 = jnp.bfloat16
    assert S % NSA_CK == 0 and S % NSA_TQ == 0 and WINDOW % NSA_TQ == 0 and WINDOW + NSA_TQ <= S
    n_sel = S // SEL_BLOCK
    nb = kc.shape[1]
    assert nb <= 128

    def pad_blocks(a):
        return jnp.pad(a.transpose(0, 2, 1, 3).astype(bf16), ((0, 0), (0, 0), (0, 128 - nb), (0, 0)))

    zg4 = zg.reshape(B, S, NSA_KV_HEADS, 3 * NSA_GROUP).transpose(0, 2, 1, 3)
    c = np.arange(128)[:, None]
    jj = np.arange(n_sel)[None, :]
    ratio = SEL_BLOCK // CMP_STRIDE
    msel = ((c >= jj * ratio - (CMP_BLOCK // CMP_STRIDE - 1)) & (c <= jj * ratio + ratio - 1) & (c < nb))
    expand = (np.arange(S)[None, :] // SEL_BLOCK == np.arange(n_sel)[:, None])
    expand = expand.reshape(n_sel, S // NSA_CK, NSA_CK).transpose(1, 0, 2)
    row_spec = pl.BlockSpec((1, 1, S, NSA_HD), lambda b, n, i: (b, n, 0, 0))
    blk_spec = pl.BlockSpec((1, 1, 128, NSA_HD), lambda b, n, i: (b, n, 0, 0))
    return pl.pallas_call(
        _nsa_prompt_kernel,
        out_shape=jax.ShapeDtypeStruct((B, S, NSA_QW), jnp.float32),
        grid=(B, NSA_KV_HEADS, S // NSA_TQ),
        in_specs=[pl.BlockSpec((1, NSA_TQ, NSA_GROUP * NSA_HD), lambda b, n, i: (b, i, n)),
                  blk_spec, blk_spec, row_spec, row_spec, row_spec, row_spec,
                  pl.BlockSpec((1, 1, NSA_TQ, 3 * NSA_GROUP), lambda b, n, i: (b, n, i, 0)),
                  pl.BlockSpec((128, n_sel), lambda b, n, i: (0, 0)),
                  pl.BlockSpec((S // NSA_CK, n_sel, NSA_CK), lambda b, n, i: (0, 0, 0))],
        out_specs=pl.BlockSpec((1, NSA_TQ, NSA_GROUP * NSA_HD), lambda b, n, i: (b, i, n)),
        compiler_params=pltpu.CompilerParams(dimension_semantics=("arbitrary", "arbitrary", "arbitrary"),
                                             vmem_limit_bytes=VMEM_LIMIT_BYTES),
        name="nsa_prompt_attention",
    )(qs, pad_blocks(kc), pad_blocks(vc), k_slc, v_slc, k_win, v_win,
      zg4, jnp.asarray(msel, bf16), jnp.asarray(expand, bf16))


MLSTM_L = 128
CONV_HALO = 8


def _log_sigmoid(x):
    return -(jnp.maximum(-x, 0.0) + jnp.log1p(jnp.exp(-jnp.abs(x))))


def _mlstm_prompt_kernel(x_ref, xprev_ref, halo0_ref, v_ref, o_ref, gcol_ref, grow_ref, cw_ref, cb_ref,
                         out_ref, c_out, n_out, m_out, c_ref, n_ref, m_ref):
    f32, bf16 = jnp.float32, jnp.bfloat16
    c = pl.program_id(1)
    L = MLSTM_L

    @pl.when(c == 0)
    def _():
        c_ref[...] = jnp.zeros_like(c_ref)
        n_ref[...] = jnp.zeros_like(n_ref)
        m_ref[...] = jnp.zeros_like(m_ref)

    x = x_ref[0]
    halo = jnp.where(c == 0, halo0_ref[0], xprev_ref[0, L - CONV_HALO:L, :])
    ext = jnp.concatenate([halo, x], axis=0)
    conv = cb_ref[...]
    for j in range(CONV_W):
        o = CONV_HALO - (CONV_W - 1) + j
        conv = conv + ext[o:o + L] * cw_ref[j:j + 1, :]
    qk = conv * jax.nn.sigmoid(conv)

    t_id = lax.broadcasted_iota(jnp.int32, (L, L), 0)
    s_id = lax.broadcasted_iota(jnp.int32, (L, L), 1)
    causal = t_id >= s_id
    gcol = gcol_ref[0, 0]
    grow = grow_ref[0, 0]
    for h in range(M_HEADS):
        hd = slice(h * M_HD, (h + 1) * M_HD)
        q = qk[:, hd]
        k = qk[:, M_W + h * M_HD:M_W + (h + 1) * M_HD] * (M_HD ** -0.5)
        v = v_ref[0, :, hd].astype(bf16)
        ig_r = grow[h:h + 1, :]
        ig_c = gcol[:, h:h + 1]
        lf_r = _log_sigmoid(grow[M_HEADS + h:M_HEADS + h + 1, :])
        lf_c = _log_sigmoid(gcol[:, M_HEADS + h:M_HEADS + h + 1])
        b_c = jnp.sum(jnp.where(causal, lf_r, 0.0), axis=1, keepdims=True)
        b_r = jnp.sum(jnp.where(t_id <= s_id, lf_c, 0.0), axis=0, keepdims=True)
        m_prev = m_ref[h]
        dmat = jnp.where(causal, b_c - b_r + ig_r, -jnp.inf)
        inter = b_c + m_prev
        m_t = jnp.maximum(inter, jnp.max(dmat, axis=1, keepdims=True))
        w_intra = jnp.exp(dmat - m_t)
        w_inter = jnp.exp(inter - m_t)
        qb = q.astype(bf16)
        s = lax.dot_general(qb, k.astype(bf16), _NT, preferred_element_type=f32) * w_intra
        num = (jnp.dot(s.astype(bf16), v, preferred_element_type=f32)
               + w_inter * jnp.dot(qb, c_ref[h].astype(bf16), preferred_element_type=f32))
        den = jnp.sum(s, axis=1, keepdims=True) + w_inter * jnp.sum(q * n_ref[h], axis=1, keepdims=True)
        hh = num / jnp.maximum(jnp.abs(den), jnp.exp(-m_t))
        out_ref[0, :, hd] = jax.nn.sigmoid(o_ref[0, :, hd]) * hh
        m_new = m_t[L - 1:L]
        b_last = b_c[L - 1:L]
        w_s = jnp.exp(b_last - b_c + ig_c - m_new)
        w_p = jnp.exp(b_last + m_prev - m_new)
        kw = k * w_s
        c_ref[h] = w_p * c_ref[h] + jnp.dot(kw.T.astype(bf16), v, preferred_element_type=f32)
        n_ref[h] = w_p * n_ref[h] + jnp.sum(kw, axis=0, keepdims=True)
        m_ref[h] = m_new

    @pl.when(c == pl.num_programs(1) - 1)
    def _():
        c_out[0] = c_ref[...]
        n_out[0] = n_ref[...]
        m_out[0] = m_ref[...]


def mlstm_prompt(zqk, zv, zo, zif, conv_w, conv_b, b_if):
    B, T, _ = zqk.shape
    L = MLSTM_L
    assert T % L == 0
    nc = T // L
    f32 = jnp.float32
    gif = zif + b_if
    gcol = gif.reshape(B, nc, L, 2 * M_HEADS)
    grow = gcol.transpose(0, 1, 3, 2)
    halo0 = jnp.zeros((B, CONV_HALO, 2 * M_W), f32)
    out, C, n, m = pl.pallas_call(
        _mlstm_prompt_kernel,
        out_shape=(jax.ShapeDtypeStruct((B, T, M_W), f32),
                   jax.ShapeDtypeStruct((B, M_HEADS, M_HD, M_HD), f32),
                   jax.ShapeDtypeStruct((B, M_HEADS, 1, M_HD), f32),
                   jax.ShapeDtypeStruct((B, M_HEADS, 1, 1), f32)),
        grid=(B, nc),
        in_specs=[pl.BlockSpec((1, L, 2 * M_W), lambda b, c: (b, c, 0)),
                  pl.BlockSpec((1, L, 2 * M_W), lambda b, c: (b, jnp.maximum(c - 1, 0), 0)),
                  pl.BlockSpec((1, CONV_HALO, 2 * M_W), lambda b, c: (b, 0, 0)),
                  pl.BlockSpec((1, L, M_W), lambda b, c: (b, c, 0)),
                  pl.BlockSpec((1, L, M_W), lambda b, c: (b, c, 0)),
                  pl.BlockSpec((1, 1, L, 2 * M_HEADS), lambda b, c: (b, c, 0, 0)),
                  pl.BlockSpec((1, 1, 2 * M_HEADS, L), lambda b, c: (b, c, 0, 0)),
                  pl.BlockSpec((CONV_W, 2 * M_W), lambda b, c: (0, 0)),
                  pl.BlockSpec((1, 2 * M_W), lambda b, c: (0, 0))],
        out_specs=(pl.BlockSpec((1, L, M_W), lambda b, c: (b, c, 0)),
                   pl.BlockSpec((1, M_HEADS, M_HD, M_HD), lambda b, c: (b, 0, 0, 0)),
                   pl.BlockSpec((1, M_HEADS, 1, M_HD), lambda b, c: (b, 0, 0, 0)),
                   pl.BlockSpec((1, M_HEADS, 1, 1), lambda b, c: (b, 0, 0, 0))),
        scratch_shapes=[pltpu.VMEM((M_HEADS, M_HD, M_HD), f32), pltpu.VMEM((M_HEADS, 1, M_HD), f32),
                        pltpu.VMEM((M_HEADS, 1, 1), f32)],
        compiler_params=pltpu.CompilerParams(dimension_semantics=("arbitrary", "arbitrary"),
                                             vmem_limit_bytes=VMEM_LIMIT_BYTES),
        name="mlstm_prompt",
    )(zqk, zqk, halo0, zv, zo, gcol, grow, conv_w, conv_b[None])
    return out, C, n.reshape(B, M_HEADS, M_HD), m.reshape(B, M_HEADS)


def mlstm_chunk(carry, inp):
    C, n, m = carry
    q, k, v, ig, lf = inp
    L = q.shape[2]
    b = jnp.cumsum(lf, axis=-1)
    causal = jnp.tril(jnp.ones((L, L), dtype=bool))
    dmat = jnp.where(causal, b[..., :, None] - b[..., None, :] + ig[..., None, :], -jnp.inf)
    inter = b + m[..., None]
    m_t = jnp.maximum(inter, dmat.max(axis=-1))
    w_intra = jnp.exp(dmat - m_t[..., None])
    w_inter = jnp.exp(inter - m_t)
    s = jnp.einsum('bhtd,bhsd->bhts', q, k) * w_intra
    num = jnp.einsum('bhts,bhsv->bhtv', s, v) + w_inter[..., None] * jnp.einsum('bhtd,bhdv->bhtv', q, C)
    den = s.sum(-1) + w_inter * jnp.einsum('bhtd,bhd->bht', q, n)
    h = num / jnp.maximum(jnp.abs(den), jnp.exp(-m_t))[..., None]
    m_new = m_t[..., -1]
    w_s = jnp.exp(b[..., -1:] - b + ig - m_new[..., None])
    w_p = jnp.exp(b[..., -1] + m - m_new)
    C_new = w_p[..., None, None] * C + jnp.einsum('bhs,bhsd,bhsv->bhdv', w_s, k, v)
    n_new = w_p[..., None] * n + jnp.einsum('bhs,bhsd->bhd', w_s, k)
    return (C_new, n_new, m_new), h


def mlstm_mix(zqk, zv, zo, zif, buf0, C0, n0, m0, conv_w, conv_b, b_if, chunk):
    B, T, _ = zqk.shape
    full = jnp.concatenate([buf0, zqk], axis=1)
    conv = conv_b
    for j in range(CONV_W):
        conv = conv + full[:, j:j + T] * conv_w[j]
    qk = jax.nn.silu(conv)

    def heads(a):
        return a.reshape(B, T, M_HEADS, M_HD).transpose(0, 2, 1, 3)

    q = heads(qk[..., :M_W])
    k = heads(qk[..., M_W:]) * (M_HD ** -0.5)
    v = heads(zv)
    gif = zif + b_if
    ig = gif[..., :M_HEADS].transpose(0, 2, 1)
    lf = jax.nn.log_sigmoid(gif[..., M_HEADS:]).transpose(0, 2, 1)
    nc = T // chunk

    def to_chunks(a):
        return jnp.moveaxis(a.reshape(B, M_HEADS, nc, chunk, *a.shape[3:]), 2, 0)

    (C, n, m), h = lax.scan(mlstm_chunk, (C0, n0, m0),
                            (to_chunks(q), to_chunks(k), to_chunks(v), to_chunks(ig), to_chunks(lf)))
    h = jnp.moveaxis(h, 0, 2).reshape(B, M_HEADS, T, M_HD).transpose(0, 2, 1, 3).reshape(B, T, M_W)
    out = jax.nn.sigmoid(zo) * h
    return out, (C, n, m, full[:, T:])


PEER_COMBOS = 2 * PEER_HEADS
PEER_KEY_ROWS = 8
PEER_TILE = PEER_KEY_ROWS * N_KEYS
PEER_TS_ROWS = 24
LANES = 128
_NT = (((1,), (1,)), ((), ()))


def _peer_topk_kernel(q_ref, keys_ref, s_ref, e0_ref, e1_ref, tau_ref, ts_ref):
    c = pl.program_id(1)
    tt = q_ref.shape[0]
    s = lax.dot_general(keys_ref[0], q_ref[...].astype(jnp.bfloat16), _NT,
                        preferred_element_type=jnp.float32)
    s_ref[c] = s
    key_id = lax.broadcasted_iota(jnp.int32, s.shape, 0)
    work = s
    rows = []
    for _ in range(PEER_TOPK + 1):
        m = jnp.max(work, axis=0, keepdims=True)
        first = jnp.min(jnp.where(work == m, key_id, N_KEYS), axis=0, keepdims=True)
        work = jnp.where(key_id == first, -jnp.inf, work)
        rows.append(m)
    rows.append(jnp.full((PEER_TS_ROWS - PEER_TOPK - 1, tt), -jnp.inf, jnp.float32))
    ts_ref[c] = jnp.concatenate(rows, axis=0)

    @pl.when(c == PEER_COMBOS - 1)
    def _():
        for h in range(PEER_HEADS):
            t0 = ts_ref[2 * h]
            t1 = ts_ref[2 * h + 1]
            pieces = [t0[0:1] + t1] + [t0[a:a + 1] + t1[0:8] for a in range(1, 8)] + [t0[8:24] + t1[0:1]]
            cand = jnp.concatenate(pieces, axis=0)
            top = t0[0:1] + t1[0:1]
            v16 = top
            v17 = top
            z = jnp.zeros_like(top)
            seen = jnp.zeros_like(top)
            for _ in range(PEER_TOPK + 1):
                m = jnp.max(cand, axis=0, keepdims=True)
                eq = cand == m
                cnt = jnp.sum(jnp.where(eq, 1.0, 0.0), axis=0, keepdims=True)
                active = seen < PEER_TOPK
                take = jnp.minimum(cnt, PEER_TOPK - seen)
                v16 = jnp.where(active, m, v16)
                v17 = jnp.where(seen < PEER_TOPK + 1, m, v17)
                z = z + jnp.where(active, take * jnp.exp(m - top), 0.0)
                seen = seen + cnt
                cand = jnp.where(eq, -jnp.inf, cand)
            tau_ref[h:h + 1, :] = 0.5 * v16 + 0.5 * v17
            e0_ref[h] = jnp.exp(s_ref[2 * h] - t0[0:1]) / z
            e1_ref[h] = jnp.exp(s_ref[2 * h + 1] - t1[0:1])


def peer_scores(q, sub_keys, tt):
    n = q.shape[0]
    assert n % tt == 0
    keys = sub_keys.reshape(PEER_COMBOS, N_KEYS, PEER_QDIM // 2).astype(jnp.bfloat16)
    f32 = jnp.float32
    return pl.pallas_call(
        _peer_topk_kernel,
        out_shape=(jax.ShapeDtypeStruct((PEER_COMBOS, N_KEYS, n), f32),
                   jax.ShapeDtypeStruct((PEER_HEADS, N_KEYS, n), f32),
                   jax.ShapeDtypeStruct((PEER_HEADS, N_KEYS, n), f32),
                   jax.ShapeDtypeStruct((PEER_HEADS, n), f32)),
        grid=(n // tt, PEER_COMBOS),
        in_specs=[pl.BlockSpec((tt, PEER_QDIM // 2), lambda i, c: (i, c)),
                  pl.BlockSpec((1, N_KEYS, PEER_QDIM // 2), lambda i, c: (c, 0, 0))],
        out_specs=(pl.BlockSpec((PEER_COMBOS, N_KEYS, tt), lambda i, c: (0, 0, i)),
                   pl.BlockSpec((PEER_HEADS, N_KEYS, tt), lambda i, c: (0, 0, i)),
                   pl.BlockSpec((PEER_HEADS, N_KEYS, tt), lambda i, c: (0, 0, i)),
                   pl.BlockSpec((PEER_HEADS, tt), lambda i, c: (0, i))),
        scratch_shapes=[pltpu.VMEM((PEER_COMBOS, PEER_TS_ROWS, tt), f32)],
        compiler_params=pltpu.CompilerParams(dimension_semantics=("arbitrary", "arbitrary"),
                                             vmem_limit_bytes=VMEM_LIMIT_BYTES),
        name="peer_topk",
    )(q, keys)


def _peer_dense_kernel(xb_ref, h_ref, u_ref, vt_ref, s0_ref, ez_ref, s_ref, e1_ref, tau_ref, g_ref, b_ref,
                       o_ref, acc_ref, a_ref, w_ref):
    e = pl.program_id(1)
    tt = xb_ref.shape[0]

    @pl.when(e == 0)
    def _():
        acc_ref[...] = jnp.zeros_like(acc_ref)

    a_ref[...] = lax.dot_general(u_ref[...], xb_ref[...], _NT, preferred_element_type=jnp.float32)
    for r in range(PEER_KEY_ROWS):
        rows = slice(r * N_KEYS, (r + 1) * N_KEYS)
        for t in range(tt // LANES):
            tok = slice(t * LANES, (t + 1) * LANES)
            gate = jnp.zeros((N_KEYS, LANES), jnp.float32)
            for h in range(PEER_HEADS):
                need = tau_ref[h:h + 1, tok] - s0_ref[2 * h, r:r + 1, tok]
                picked = jnp.where(s_ref[2 * h + 1, :, tok] >= need, e1_ref[h, :, tok], 0.0)
                gate = gate + picked * ez_ref[h, r:r + 1, tok]
            ar = a_ref[rows, tok]
            act = 0.5 * ar * (1.0 + lax.erf(ar * (2.0 ** -0.5)))
            w_ref[rows, tok] = (gate * act).astype(jnp.bfloat16)
    acc_ref[...] += jnp.dot(vt_ref[...], w_ref[...], preferred_element_type=jnp.float32)

    @pl.when(e == pl.num_programs(1) - 1)
    def _():
        r = ALPHA * h_ref[...] + acc_ref[...].T
        mu = jnp.mean(r, axis=-1, keepdims=True)
        d = r - mu
        var = jnp.mean(d * d, axis=-1, keepdims=True)
        o_ref[...] = d * lax.rsqrt(var + LN_EPS) * g_ref[...] + b_ref[...]


def peer_tail(h, hb, q, sub_keys, u_bf, vt_bf, ln_g, ln_b, tt):
    n, d = h.shape
    s, e0z, e1, tau = peer_scores(q, sub_keys, tt)
    n_exp = u_bf.shape[0]
    return pl.pallas_call(
        _peer_dense_kernel,
        out_shape=jax.ShapeDtypeStruct((n, d), jnp.float32),
        grid=(n // tt, n_exp // PEER_TILE),
        in_specs=[pl.BlockSpec((tt, d), lambda i, e: (i, 0)),
                  pl.BlockSpec((tt, d), lambda i, e: (i, 0)),
                  pl.BlockSpec((PEER_TILE, d), lambda i, e: (e, 0)),
                  pl.BlockSpec((d, PEER_TILE), lambda i, e: (0, e)),
                  pl.BlockSpec((PEER_COMBOS, PEER_KEY_ROWS, tt), lambda i, e: (0, e, i)),
                  pl.BlockSpec((PEER_HEADS, PEER_KEY_ROWS, tt), lambda i, e: (0, e, i)),
                  pl.BlockSpec((PEER_COMBOS, N_KEYS, tt), lambda i, e: (0, 0, i)),
                  pl.BlockSpec((PEER_HEADS, N_KEYS, tt), lambda i, e: (0, 0, i)),
                  pl.BlockSpec((PEER_HEADS, tt), lambda i, e: (0, i)),
                  pl.BlockSpec((1, d), lambda i, e: (0, 0)),
                  pl.BlockSpec((1, d), lambda i, e: (0, 0))],
        out_specs=pl.BlockSpec((tt, d), lambda i, e: (i, 0)),
        scratch_shapes=[pltpu.VMEM((d, tt), jnp.float32), pltpu.VMEM((PEER_TILE, tt), jnp.float32),
                        pltpu.VMEM((PEER_TILE, tt), jnp.bfloat16)],
        compiler_params=pltpu.CompilerParams(dimension_semantics=("arbitrary", "arbitrary"),
                                             vmem_limit_bytes=VMEM_LIMIT_BYTES),
        name="peer_dense",
    )(hb, h, u_bf, vt_bf, s, e0z, s, e1, tau, ln_g[None], ln_b[None])


def _out_proj_kernel(x_ref, nsa_ref, m_ref, wn_ref, wm_ref, g_ref, b_ref, wq_ref, h_ref, hb_ref, q_ref):
    f32, bf16 = jnp.float32, jnp.bfloat16
    r = (ALPHA * x_ref[...] + jnp.dot(nsa_ref[...].astype(bf16), wn_ref[...], preferred_element_type=f32)
         + jnp.dot(m_ref[...].astype(bf16), wm_ref[...], preferred_element_type=f32))
    mu = jnp.mean(r, axis=-1, keepdims=True)
    d = r - mu
    var = jnp.mean(d * d, axis=-1, keepdims=True)
    h = d * lax.rsqrt(var + LN_EPS) * g_ref[...] + b_ref[...]
    h_ref[...] = h
    hb = h.astype(bf16)
    hb_ref[...] = hb
    q_ref[...] = jnp.dot(hb, wq_ref[...], preferred_element_type=f32)


def out_proj_fused(x, o_nsa, o_m, w_out, ln_g, ln_b, w_pq, tm):
    n, d = x.shape
    assert n % tm == 0
    bf16 = jnp.bfloat16
    nq = w_pq.shape[1]

    def rows(width):
        return pl.BlockSpec((tm, width), lambda i: (i, 0))

    def whole(a):
        return pl.BlockSpec(a.shape, lambda i: (0, 0))

    wn = w_out[:NSA_QW].astype(bf16)
    wm = w_out[NSA_QW:].astype(bf16)
    wq = w_pq.astype(bf16)
    g, b = ln_g[None], ln_b[None]
    return pl.pallas_call(
        _out_proj_kernel,
        out_shape=(jax.ShapeDtypeStruct((n, d), jnp.float32), jax.ShapeDtypeStruct((n, d), bf16),
                   jax.ShapeDtypeStruct((n, nq), jnp.float32)),
        grid=(n // tm,),
        in_specs=[rows(d), rows(NSA_QW), rows(M_W), whole(wn), whole(wm), whole(g), whole(b), whole(wq)],
        out_specs=(rows(d), rows(d), rows(nq)),
        compiler_params=pltpu.CompilerParams(dimension_semantics=("arbitrary",),
                                             vmem_limit_bytes=VMEM_LIMIT_BYTES),
        name="out_proj",
    )(x, o_nsa, o_m, wn, wm, g, b, wq)


def block_tail(x, o_nsa, o_m, w_out, ln_g, ln_b, w_pq, sub_keys, u_bf, vt_bf, tt):
    lead = x.shape[:-1]
    h, hb, q = out_proj_fused(x.reshape(-1, D_MODEL), o_nsa.reshape(-1, NSA_QW), o_m.reshape(-1, M_W),
                              w_out, ln_g[0], ln_b[0], w_pq, tt)
    return peer_tail(h, hb, q, sub_keys, u_bf, vt_bf, ln_g[1], ln_b[1], tt).reshape(*lead, D_MODEL)


def prompt_mix(x, w_in, pe, w1, b1, w2, conv_w, conv_b, b_if):
    B, S, _ = x.shape
    z = in_proj_fused(x, w_in, jnp.arange(S), 512, True)
    chunk_w = CMP_STRIDE * NSA_KVW

    def compressed(rows, c):
        proj = chunk_projection(rows.reshape(B * (S // CMP_STRIDE), chunk_w), w1[c])
        return compress_from_projection(proj.reshape(B, S // CMP_STRIDE, -1), pe[c], w1[c], b1[c], w2[c])

    kc = compressed(z["k_cmp"], 0)
    vc = compressed(z["v_cmp"], 1)
    zgate = z["zgate"].reshape(B, S, GATE_W)
    o_nsa = nsa_prompt_attention(z["q"].reshape(B, S, NSA_QW), kc, vc, z["k_slc_bf"], z["v_slc_bf"],
                                 z["k_win_bf"], z["v_win_bf"], zgate[..., :IN_SPLITS[2]])
    zqk = z["zqk"].reshape(B, S, 2 * M_W)
    o_m, C, n, m = mlstm_prompt(zqk, z["zv"].reshape(B, S, M_W), z["zo"].reshape(B, S, M_W),
                                zgate[..., IN_SPLITS[2]:], conv_w, conv_b, b_if)
    buf = zqk[:, S - (CONV_W - 1):]
    wl = min(WINDOW, S)
    k_cmp, v_cmp, k_slc, v_slc, k_win, v_win = [
        z[k].reshape(B, S, NSA_KV_HEADS, NSA_HD) for k in ("k_cmp", "v_cmp", "k_slc", "v_slc", "k_win", "v_win")]
    return (o_nsa, o_m), (k_cmp, v_cmp, k_slc, v_slc, k_win[:, S - wl:], v_win[:, S - wl:], C, n, m, buf)


def sample_mix(x, kc_pool, vc_pool, ks_pool, vs_pool, kw_buf, vw_buf, C0, n0, m0, buf0, page_table,
               w_in, pe, w1, b1, w2, conv_w, conv_b, b_if):
    B, T, _ = x.shape
    past = page_table.shape[1] * PAGE_SIZE
    pos = past + jnp.arange(T)
    z = in_proj_fused(x, w_in, pos, B * T, False)
    q = z["q"].astype(jnp.float32).reshape(B, T, NSA_HEADS, NSA_HD) * (1.0 / ATTN_SCALE)
    k_cmp, v_cmp, k_slc, v_slc, k_win, v_win = [
        z[k].reshape(B, T, NSA_KV_HEADS, NSA_HD) for k in ("k_cmp", "v_cmp", "k_slc", "v_slc", "k_win", "v_win")]
    zgate = z["zgate"].reshape(B, T, GATE_W)
    gates = jax.nn.sigmoid(zgate[..., :IN_SPLITS[2]]).reshape(B, T, NSA_HEADS, 3)
    zqk, zv, zo, zif = (z["zqk"].reshape(B, T, 2 * M_W), z["zv"].reshape(B, T, M_W), z["zo"].reshape(B, T, M_W),
                        zgate[..., IN_SPLITS[2]:])

    assert (past + T) // CMP_STRIDE == past // CMP_STRIDE

    def compressed(pool, c):
        pages = pool.transpose(0, 2, 3, 1)[page_table]
        return compress_from_projection(page_projection(pages, w1[c]), pe[c], w1[c], b1[c], w2[c])

    o_cmp, p = cmp_attend(q, pos, compressed(kc_pool, 0), compressed(vc_pool, 1))
    n_sel = -(-(past + T) // SEL_BLOCK)
    member = select_blocks(p, pos, n_sel)
    o_sel = sample_selected_attention(q, pos, member, ks_pool, vs_pool, k_slc, v_slc, page_table)
    wb = kw_buf.shape[1]
    kw = jnp.concatenate([kw_buf, k_win], axis=1)
    vw = jnp.concatenate([vw_buf, v_win], axis=1)
    kpos = past - wb + jnp.arange(wb + T)
    o_win = win_attend(q, pos, kw, vw, kpos)
    o_nsa = nsa_combine(gates, o_cmp, o_sel, o_win)
    o_m, (C, n, m, buf) = mlstm_mix(zqk, zv, zo, zif, buf0, C0, n0, m0, conv_w, conv_b, b_if, T)
    return (o_nsa, o_m), (k_cmp, v_cmp, k_slc, v_slc, kw[:, T:], vw[:, T:], C, n, m, buf)


def kernel(x_prompt, x_sample, cache_k_cmp, cache_v_cmp, cache_k_slc, cache_v_slc, cache_k_win, cache_v_win,
           state_C, state_n, state_m, state_conv, page_table, w_in, w_out, w_phi1, b_phi1, w_phi2, pe_cmp,
           conv_w, conv_b, b_if, ln_g, ln_b, w_pq, sub_keys, u_tab, v_tab):
    l = 0
    mix_p, st_p = prompt_mix(x_prompt, w_in[l], pe_cmp[l], w_phi1[l], b_phi1[l], w_phi2[l],
                             conv_w[l], conv_b[l], b_if[l])
    mix_s, st_s = sample_mix(x_sample, cache_k_cmp[l], cache_v_cmp[l], cache_k_slc[l], cache_v_slc[l],
                             cache_k_win[l], cache_v_win[l], state_C[l], state_n[l], state_m[l],
                             state_conv[l], page_table, w_in[l], pe_cmp[l], w_phi1[l], b_phi1[l],
                             w_phi2[l], conv_w[l], conv_b[l], b_if[l])
    u_bf = u_tab[l].astype(jnp.bfloat16)
    vt_bf = v_tab[l].astype(jnp.bfloat16).T
    xp = block_tail(x_prompt, *mix_p, w_out[l], ln_g[l], ln_b[l], w_pq[l], sub_keys[l], u_bf, vt_bf, 512)
    xs = block_tail(x_sample, *mix_s, w_out[l], ln_g[l], ln_b[l], w_pq[l], sub_keys[l], u_bf, vt_bf, 128)
    return (xp, xs) + tuple(a[None] for a in st_p) + tuple(a[None] for a in st_s)
```

```python
import functools

import jax
import jax.numpy as jnp
import numpy as np
from jax import lax
from jax.experimental import pallas as pl
from jax.experimental.pallas import tpu as pltpu

D_MODEL = 1024
DEPTH = 1
PAGE_SIZE = 128
NSA_HEADS = 8
NSA_KV_HEADS = 2
NSA_GROUP = NSA_HEADS // NSA_KV_HEADS
NSA_HD = 64
NSA_QW = NSA_HEADS * NSA_HD
NSA_KVW = NSA_KV_HEADS * NSA_HD
CMP_BLOCK = 32
CMP_STRIDE = 16
SEL_BLOCK = 64
SEL_TOP = 16
WINDOW = 512
Q_BLOCK = 64
ATTN_SCALE = NSA_HD ** -0.5
ROPE_THETA = 10000.0
M_HEADS = 4
M_HD = 128
M_W = M_HEADS * M_HD
M_CHUNK = 64
CONV_W = 4
PEER_HEADS = 8
N_KEYS = 128
PEER_TOPK = 16
PEER_QDIM = 256
PEER_BLOCK = 128
IN_SPLITS = (NSA_QW, 6 * NSA_KVW, 3 * NSA_HEADS, 2 * M_W, M_W, M_W, 2 * M_HEADS)
LN_EPS = 1e-5
ALPHA = (2 * DEPTH) ** 0.25

VMEM_LIMIT_BYTES = 56 * 1024 * 1024


def _mm_kernel(x_ref, w_ref, o_ref):
    o_ref[...] = jnp.dot(x_ref[...].astype(jnp.bfloat16), w_ref[...], preferred_element_type=jnp.float32)


def pallas_matmul(x, w, tm=512):
    M, K = x.shape
    N = w.shape[1]
    tm = min(tm, M)
    assert M % tm == 0
    return pl.pallas_call(
        _mm_kernel,
        out_shape=jax.ShapeDtypeStruct((M, N), jnp.float32),
        grid=(M // tm,),
        in_specs=[pl.BlockSpec((tm, K), lambda i: (i, 0)), pl.BlockSpec((K, N), lambda i: (0, 0))],
        out_specs=pl.BlockSpec((tm, N), lambda i: (i, 0)),
        compiler_params=pltpu.CompilerParams(dimension_semantics=("arbitrary",),
                                             vmem_limit_bytes=VMEM_LIMIT_BYTES),
        name="proj_matmul",
    )(x, w.astype(jnp.bfloat16))


def mm3(x, w):
    lead = x.shape[:-1]
    return pallas_matmul(x.reshape(-1, x.shape[-1]), w).reshape(*lead, w.shape[1])


def layer_norm(x, g, b):
    mu = x.mean(-1, keepdims=True)
    var = jnp.square(x - mu).mean(-1, keepdims=True)
    return (x - mu) * lax.rsqrt(var + LN_EPS) * g + b


def rope(x, pos):
    half = x.shape[-1] // 2
    inv = ROPE_THETA ** (-jnp.arange(half, dtype=jnp.float32) / half)
    ang = pos.astype(jnp.float32)[:, None] * inv[None, :]
    cos = jnp.cos(ang)[:, None, :]
    sin = jnp.sin(ang)[:, None, :]
    x1, x2 = x[..., :half], x[..., half:]
    return jnp.concatenate([x1 * cos - x2 * sin, x2 * cos + x1 * sin], axis=-1)


def split_in_proj(x, w_in):
    z = mm3(x, w_in)
    cuts = [int(c) for c in np.cumsum(IN_SPLITS)[:-1]]
    return jnp.split(z, cuts, axis=-1)


_IN_OFF = np.concatenate([[0], np.cumsum(IN_SPLITS)])
_IN_ORDER = (0, 1, 3, 4, 5, 2, 6)
_N_KV_ROWS = 6
_KV_BF16 = (2, 3, 4, 5)
GATE_W = IN_SPLITS[2] + IN_SPLITS[6]


def _rope_pairs(x, cos, sin_signed):
    half = NSA_HD // 2
    lane = lax.broadcasted_iota(jnp.int32, x.shape, 1)
    partner = jnp.where(lane % NSA_HD < half, pltpu.roll(x, LANES - half, 1), pltpu.roll(x, half, 1))
    return x * cos + partner * sin_signed


def _in_proj_kernel(x_ref, w_ref, cos_ref, sin_ref, q_ref, *rest, kv_major):
    kv_refs = rest[:_N_KV_ROWS]
    rest = rest[_N_KV_ROWS:]
    if kv_major:
        bf_refs, rest = rest[:len(_KV_BF16)], rest[len(_KV_BF16):]
    zqk_ref, zv_ref, zo_ref, zgate_ref = rest
    z = jnp.dot(x_ref[...].astype(jnp.bfloat16), w_ref[...], preferred_element_type=jnp.float32)
    cos = cos_ref[...]
    sin = sin_ref[...]
    for g in range(NSA_QW // LANES):
        sl = slice(g * LANES, (g + 1) * LANES)
        q_ref[:, sl] = (_rope_pairs(z[:, sl], cos, sin) * ATTN_SCALE).astype(jnp.bfloat16)
    for r in range(_N_KV_ROWS):
        row = z[:, NSA_QW + r * NSA_KVW:NSA_QW + (r + 1) * NSA_KVW]
        if r % 2 == 0:
            row = _rope_pairs(row, cos, sin)
        kv_refs[r][...] = row
        if kv_major and r in _KV_BF16:
            dst = bf_refs[_KV_BF16.index(r)]
            for n in range(NSA_KV_HEADS):
                dst[0, n] = row[:, n * NSA_HD:(n + 1) * NSA_HD].astype(jnp.bfloat16)
    o = NSA_QW + _N_KV_ROWS * NSA_KVW
    zqk_ref[...] = z[:, o:o + 2 * M_W]
    zv_ref[...] = z[:, o + 2 * M_W:o + 3 * M_W]
    zo_ref[...] = z[:, o + 3 * M_W:o + 4 * M_W]
    zgate_ref[...] = z[:, o + 4 * M_W:o + 4 * M_W + GATE_W]


def in_proj_fused(x, w_in, pos, tm, kv_major):
    B, T, D = x.shape
    M = B * T
    assert M % tm == 0 and NSA_KVW == LANES and (not kv_major or T % tm == 0)
    f32, bf16 = jnp.float32, jnp.bfloat16
    w = jnp.concatenate([w_in[:, _IN_OFF[i]:_IN_OFF[i + 1]] for i in _IN_ORDER], axis=1).astype(bf16)
    half = NSA_HD // 2
    inv = ROPE_THETA ** (-jnp.arange(half, dtype=f32) / half)
    ang = pos.astype(f32)[:, None] * inv[None, :]
    cos = jnp.tile(jnp.cos(ang), (B, 2 * LANES // NSA_HD))
    sin = jnp.tile(jnp.concatenate([-jnp.sin(ang), jnp.sin(ang)], axis=1), (B, LANES // NSA_HD))
    n_w = w.shape[1]

    def rows(width):
        return pl.BlockSpec((tm, width), lambda i: (i, 0))

    out_shape = [jax.ShapeDtypeStruct((M, NSA_QW), bf16)] + [jax.ShapeDtypeStruct((M, NSA_KVW), f32)] * _N_KV_ROWS
    out_specs = [rows(NSA_QW)] + [rows(NSA_KVW)] * _N_KV_ROWS
    if kv_major:
        per_seq = T // tm
        out_shape += [jax.ShapeDtypeStruct((B, NSA_KV_HEADS, T, NSA_HD), bf16)] * len(_KV_BF16)
        out_specs += [pl.BlockSpec((1, NSA_KV_HEADS, tm, NSA_HD),
                                   lambda i: (i // per_seq, 0, i % per_seq, 0))] * len(_KV_BF16)
    out_shape += [jax.ShapeDtypeStruct((M, 2 * M_W), f32), jax.ShapeDtypeStruct((M, M_W), f32),
                  jax.ShapeDtypeStruct((M, M_W), f32), jax.ShapeDtypeStruct((M, GATE_W), f32)]
    out_specs += [rows(2 * M_W), rows(M_W), rows(M_W), rows(GATE_W)]
    outs = pl.pallas_call(
        functools.partial(_in_proj_kernel, kv_major=kv_major),
        out_shape=tuple(out_shape),
        grid=(M // tm,),
        in_specs=[rows(D), pl.BlockSpec((D, n_w), lambda i: (0, 0)), rows(LANES), rows(LANES)],
        out_specs=tuple(out_specs),
        compiler_params=pltpu.CompilerParams(dimension_semantics=("arbitrary",),
                                             vmem_limit_bytes=VMEM_LIMIT_BYTES),
        name="in_proj",
    )(x.reshape(M, D), w, cos, sin)
    names = ["q", "k_cmp", "v_cmp", "k_slc", "v_slc", "k_win", "v_win"]
    if kv_major:
        names += ["k_slc_bf", "v_slc_bf", "k_win_bf", "v_win_bf"]
    names += ["zqk", "zv", "zo", "zgate"]
    return dict(zip(names, outs))


def nsa_project(zq, zkv, zg, pos):
    B, T, _ = zq.shape
    q = rope(zq.reshape(B, T, NSA_HEADS, NSA_HD), pos)
    kv = zkv.reshape(B, T, 6, NSA_KV_HEADS, NSA_HD)
    rows = (rope(kv[:, :, 0], pos), kv[:, :, 1], rope(kv[:, :, 2], pos), kv[:, :, 3],
            rope(kv[:, :, 4], pos), kv[:, :, 5])
    gates = jax.nn.sigmoid(zg).reshape(B, T, NSA_HEADS, 3)
    return q, rows, gates


def chunk_projection(chunks, w1):
    assert CMP_BLOCK == 2 * CMP_STRIDE
    return pallas_matmul(chunks, _expanded_w1(w1))


def _expanded_w1(w1):
    w1r = w1.reshape(2, CMP_STRIDE, NSA_HD, w1.shape[-1])
    wbig = jnp.einsum('hpdf,kn->pkdnhf', w1r, jnp.eye(NSA_KV_HEADS, dtype=w1.dtype))
    return wbig.reshape(CMP_STRIDE * NSA_KVW, 2 * NSA_KV_HEADS * w1.shape[-1])


PAGE_GROUP = 4


def _page_projection_kernel(pg_ref, w_ref, o_ref, x_ref, t_ref):
    n_pages = pg_ref.shape[1]
    per_page = PAGE_SIZE // CMP_STRIDE
    group = PAGE_GROUP

    def place(i, carry):
        for u in range(group):
            g = i * group + u
            t_ref[u] = pg_ref[0, g].reshape(NSA_KVW, PAGE_SIZE).T
            row0 = pl.multiple_of(g * per_page, per_page)
            for p in range(CMP_STRIDE):
                x_ref[pl.ds(row0, per_page), p * NSA_KVW:(p + 1) * NSA_KVW] = (
                    t_ref.at[u][pl.ds(p, per_page, stride=CMP_STRIDE), :])
        return carry

    lax.fori_loop(0, n_pages // group, place, 0)
    o_ref[0] = jnp.dot(x_ref[...].astype(jnp.bfloat16), w_ref[...], preferred_element_type=jnp.float32)


def page_projection(pages, w1):
    B, n_pages = pages.shape[:2]
    assert pages.shape[2:] == (NSA_KV_HEADS, NSA_HD, PAGE_SIZE) and NSA_KVW == LANES and PAGE_SIZE == LANES
    assert n_pages % PAGE_GROUP == 0
    wbig = _expanded_w1(w1).astype(jnp.bfloat16)
    rows = n_pages * (PAGE_SIZE // CMP_STRIDE)
    return pl.pallas_call(
        _page_projection_kernel,
        out_shape=jax.ShapeDtypeStruct((B, rows, wbig.shape[1]), jnp.float32),
        grid=(B,),
        in_specs=[pl.BlockSpec((1, n_pages, NSA_KV_HEADS, NSA_HD, PAGE_SIZE), lambda b: (b, 0, 0, 0, 0)),
                  pl.BlockSpec(wbig.shape, lambda b: (0, 0))],
        out_specs=pl.BlockSpec((1, rows, wbig.shape[1]), lambda b: (b, 0, 0)),
        scratch_shapes=[pltpu.VMEM((rows, wbig.shape[0]), jnp.float32),
                        pltpu.VMEM((PAGE_GROUP, PAGE_SIZE, NSA_KVW), jnp.float32)],
        compiler_params=pltpu.CompilerParams(dimension_semantics=("arbitrary",),
                                             vmem_limit_bytes=VMEM_LIMIT_BYTES),
        name="page_projection",
    )(pages, wbig)


def compress_from_projection(proj, pe, w1, b1, w2):
    f = w1.shape[-1]
    bias = jnp.dot(pe.reshape(-1), w1, precision=lax.Precision.HIGHEST) + b1
    heads = []
    for n in range(NSA_KV_HEADS):
        first = proj[:, :-1, 2 * n * f:(2 * n + 1) * f]
        second = proj[:, 1:, (2 * n + 1) * f:(2 * n + 2) * f]
        heads.append(jax.nn.gelu(first + second + bias, approximate=False) @ w2)
    return jnp.stack(heads, axis=2)


def cmp_attend(q, qpos, kc, vc):
    B, T = q.shape[:2]
    qg = q.reshape(B, T, NSA_KV_HEADS, NSA_GROUP, NSA_HD)
    s = jnp.einsum('btngd,bcnd->btngc', qg, kc) * ATTN_SCALE
    nblk = kc.shape[1]
    blk_end = jnp.arange(nblk) * CMP_STRIDE + CMP_BLOCK - 1
    valid = (blk_end[None, :] <= qpos[:, None])[None, :, None, None, :]
    p = jax.nn.softmax(jnp.where(valid, s, -1e30), axis=-1) * valid
    o = jnp.einsum('btngc,bcnd->btngd', p, vc)
    return o.reshape(B, T, NSA_HEADS, NSA_HD), p


def select_blocks(p, qpos, n_sel):
    imp = p.sum(axis=3)
    R = SEL_BLOCK // CMP_STRIDE
    r = CMP_BLOCK // CMP_STRIDE
    nb = imp.shape[-1]
    right = n_sel * R + R - 1 - nb
    padded = jnp.pad(imp, ((0, 0), (0, 0), (0, 0), (r - 1, right)))
    score = padded[..., 0:(n_sel - 1) * R + 1:R]
    for o in range(1, R + r - 1):
        score = score + padded[..., o:o + (n_sel - 1) * R + 1:R]
    j = jnp.arange(n_sel)[None, :]
    cur = (qpos // SEL_BLOCK)[:, None]
    valid = (j * SEL_BLOCK <= qpos[:, None])[None, :, None, :]
    forced = ((j == 0) | (j == cur) | (j == cur - 1))[None, :, None, :]
    score = jnp.where(forced, jnp.inf, jnp.where(valid, score, -jnp.inf))
    idx = j[0]
    before = (score[..., None, :] > score[..., :, None]) | ((score[..., None, :] == score[..., :, None])
                                                          & (idx[None, :] < idx[:, None]))
    return before.sum(-1) < min(SEL_TOP, n_sel)


def sample_selected_attention(q, qpos, member, k_pool, v_pool, k_new, v_new, page_table):
    B, T = q.shape[:2]
    n_pages = page_table.shape[1]
    per_page = PAGE_SIZE // SEL_BLOCK
    assert member.shape[-1] == n_pages * per_page + 1 and T <= SEL_BLOCK
    kp = k_pool.transpose(0, 2, 3, 1)[page_table]
    vp = v_pool.transpose(0, 2, 3, 1)[page_table]
    qg = q.reshape(B, T, NSA_KV_HEADS, NSA_GROUP, NSA_HD)
    s_past = jnp.einsum('btngd,bpndk->bntgpk', qg, kp) * ATTN_SCALE
    s_new = jnp.einsum('btngd,bsnd->bntgs', qg, k_new) * ATTN_SCALE
    m = member.transpose(0, 2, 1, 3)
    m_past = jnp.repeat(m[..., :-1].reshape(B, NSA_KV_HEADS, T, n_pages, per_page), SEL_BLOCK, axis=-1)
    kpos = (jnp.arange(n_pages) * PAGE_SIZE)[:, None] + jnp.arange(PAGE_SIZE)[None, :]
    m_past = m_past & (kpos[None, None, None] <= qpos[None, None, :, None, None])
    new_pos = n_pages * PAGE_SIZE + jnp.arange(T)
    m_new = m[..., -1:] & (new_pos[None, None, None, :] <= qpos[None, None, :, None])
    logits = jnp.concatenate(
        [jnp.where(m_past[:, :, :, None], s_past, -jnp.inf).reshape(B, NSA_KV_HEADS, T, NSA_GROUP, -1),
         jnp.where(m_new[:, :, :, None], s_new, -jnp.inf)], axis=-1)
    pr = jax.nn.softmax(logits, axis=-1)
    pr_past = pr[..., :n_pages * PAGE_SIZE].reshape(B, NSA_KV_HEADS, T, NSA_GROUP, n_pages, PAGE_SIZE)
    o = (jnp.einsum('bntgpk,bpndk->bntgd', pr_past, vp)
         + jnp.einsum('bntgs,bsnd->bntgd', pr[..., n_pages * PAGE_SIZE:], v_new))
    return o.transpose(0, 2, 1, 3, 4).reshape(B, T, NSA_HEADS, NSA_HD)


def to_blocks(rows, n_sel):
    B, L, KV, hd = rows.shape
    rows = jnp.pad(rows, ((0, 0), (0, n_sel * SEL_BLOCK - L), (0, 0), (0, 0)))
    return rows.reshape(B, n_sel, SEL_BLOCK, KV, hd).transpose(0, 3, 1, 2, 4)


def take_rows(table, idx):
    return table[idx]


def sel_attend(q, qpos, sel, kb, vb):
    B, Tq = q.shape[:2]
    k = sel.shape[-1]
    sel_t = sel.transpose(0, 2, 1, 3)
    gather = jax.vmap(jax.vmap(take_rows))
    kg = gather(kb, sel_t).reshape(B, NSA_KV_HEADS, Tq, k * SEL_BLOCK, NSA_HD)
    vg = gather(vb, sel_t).reshape(B, NSA_KV_HEADS, Tq, k * SEL_BLOCK, NSA_HD)
    kpos = (sel_t[..., None] * SEL_BLOCK + jnp.arange(SEL_BLOCK)).reshape(B, NSA_KV_HEADS, Tq, k * SEL_BLOCK)
    qg = q.reshape(B, Tq, NSA_KV_HEADS, NSA_GROUP, NSA_HD).transpose(0, 2, 1, 3, 4)
    s = jnp.einsum('bntgd,bntsd->bntgs', qg, kg) * ATTN_SCALE
    mask = kpos[:, :, :, None, :] <= qpos[None, None, :, None, None]
    pr = jax.nn.softmax(jnp.where(mask, s, -jnp.inf), axis=-1)
    o = jnp.einsum('bntgs,bntsd->bntgd', pr, vg)
    return o.transpose(0, 2, 1, 3, 4).reshape(B, Tq, NSA_HEADS, NSA_HD)


def win_attend(q, qpos, k, v, kpos):
    B, Tq = q.shape[:2]
    qg = q.reshape(B, Tq, NSA_KV_HEADS, NSA_GROUP, NSA_HD)
    s = jnp.einsum('btngd,bsnd->btngs', qg, k) * ATTN_SCALE
    diff = qpos[:, None] - kpos[None, :]
    mask = ((diff >= 0) & (diff < WINDOW) & (kpos[None, :] >= 0))[None, :, None, None, :]
    pr = jax.nn.softmax(jnp.where(mask, s, -jnp.inf), axis=-1)
    o = jnp.einsum('btngs,bsnd->btngd', pr, v)
    return o.reshape(B, Tq, NSA_HEADS, NSA_HD)


def nsa_combine(gates, o_cmp, o_sel, o_win):
    B, T = gates.shape[:2]
    o = gates[..., 0:1] * o_cmp + gates[..., 1:2] * o_sel + gates[..., 2:3] * o_win
    return o.reshape(B, T, NSA_QW)


NSA_TQ = 128
NSA_CK = 512
MASKED = -1e30


def _softmax_rows(s):
    m = jnp.max(s, axis=-1, keepdims=True)
    e = jnp.exp(s - m)
    return e / jnp.sum(e, axis=-1, keepdims=True)


def _nsa_prompt_kernel(q_ref, kc_ref, vc_ref, ks_ref, vs_ref, kw_ref, vw_ref, zg_ref, msel_ref, exp_ref, o_ref):
    f32, bf16 = jnp.float32, jnp.bfloat16
    tq = NSA_TQ
    q0 = pl.program_id(2) * tq
    qb = q_ref[0]
    qs = jnp.concatenate([qb[:, g * NSA_HD:(g + 1) * NSA_HD] for g in range(NSA_GROUP)], axis=0)
    tpos = q0 + lax.broadcasted_iota(jnp.int32, (tq, 1), 0)

    def per_head(a):
        return jnp.concatenate([a] * NSA_GROUP, axis=0)

    s = lax.dot_general(qs, kc_ref[0, 0], _NT, preferred_element_type=f32)
    cblk = lax.broadcasted_iota(jnp.int32, (tq, 128), 1)
    cvalid = cblk * CMP_STRIDE + (CMP_BLOCK - 1) <= tpos
    s = s + per_head(jnp.where(cvalid, 0.0, MASKED))
    e = jnp.exp(s - jnp.max(s, axis=-1, keepdims=True)) * per_head(jnp.where(cvalid, 1.0, 0.0))
    l = jnp.sum(e, axis=-1, keepdims=True)
    p = e / jnp.where(l > 0.0, l, 1.0)
    o_cmp = jnp.dot(p.astype(bf16), vc_ref[0, 0], preferred_element_type=f32)

    imp = p[0:tq]
    for g in range(1, NSA_GROUP):
        imp = imp + p[g * tq:(g + 1) * tq]
    hi = imp.astype(bf16)
    r1 = imp - hi.astype(f32)
    mid = r1.astype(bf16)
    lo = (r1 - mid.astype(f32)).astype(bf16)
    msel = msel_ref[...]
    score = (jnp.dot(hi, msel, preferred_element_type=f32) + jnp.dot(mid, msel, preferred_element_type=f32)
             + jnp.dot(lo, msel, preferred_element_type=f32))
    n_sel = score.shape[1]
    j = lax.broadcasted_iota(jnp.int32, (tq, n_sel), 1)
    cur = tpos // SEL_BLOCK
    forced = (j == 0) | (j == cur) | (j == cur - 1)
    score = jnp.where(forced, jnp.inf, jnp.where(j * SEL_BLOCK <= tpos, score, -jnp.inf))
    rank = jnp.zeros((tq, n_sel), f32)
    for jp in range(n_sel):
        col = score[:, jp:jp + 1]
        before = (col > score) | ((col == score) & (j > jp))
        rank = rank + jnp.where(before, 1.0, 0.0)
    sel01 = jnp.where(rank < SEL_TOP, 1.0, 0.0).astype(bf16)

    ck = NSA_CK
    rows = NSA_GROUP * tq

    def sel_chunk(c, carry):
        m, l, acc = carry
        k0 = pl.multiple_of(c * ck, ck)
        s = lax.dot_general(qs, ks_ref[0, 0, pl.ds(k0, ck), :], _NT, preferred_element_type=f32)
        chosen = jnp.dot(sel01, exp_ref[c], preferred_element_type=f32)
        kpos = k0 + lax.broadcasted_iota(jnp.int32, (tq, ck), 1)
        ok = (chosen > 0.5) & (kpos <= tpos)
        s = s + per_head(jnp.where(ok, 0.0, MASKED))
        m_new = jnp.maximum(m, jnp.max(s, axis=-1, keepdims=True))
        a = jnp.exp(m - m_new)
        pr = jnp.exp(s - m_new)
        l = a * l + jnp.sum(pr, axis=-1, keepdims=True)
        acc = a * acc + jnp.dot(pr.astype(bf16), vs_ref[0, 0, pl.ds(k0, ck), :], preferred_element_type=f32)
        return m_new, l, acc

    init = (jnp.full((rows, 1), MASKED, f32), jnp.zeros((rows, 1), f32), jnp.zeros((rows, NSA_HD), f32))
    n_chunks = (q0 + tq + ck - 1) // ck
    _, l_sel, acc_sel = lax.fori_loop(0, n_chunks, sel_chunk, init)
    o_sel = acc_sel / l_sel

    w0 = pl.multiple_of(jnp.maximum(q0 - WINDOW, 0), tq)
    wl = WINDOW + tq
    s = lax.dot_general(qs, kw_ref[0, 0, pl.ds(w0, wl), :], _NT, preferred_element_type=f32)
    diff = tpos - (w0 + lax.broadcasted_iota(jnp.int32, (tq, wl), 1))
    s = s + per_head(jnp.where((diff >= 0) & (diff < WINDOW), 0.0, MASKED))
    o_win = jnp.dot(_softmax_rows(s).astype(bf16), vw_ref[0, 0, pl.ds(w0, wl), :], preferred_element_type=f32)

    gates = jax.nn.sigmoid(zg_ref[0, 0])
    for g in range(NSA_GROUP):
        r = slice(g * tq, (g + 1) * tq)
        o_ref[0, :, g * NSA_HD:(g + 1) * NSA_HD] = (gates[:, 3 * g:3 * g + 1] * o_cmp[r]
                                                    + gates[:, 3 * g + 1:3 * g + 2] * o_sel[r]
                                                    + gates[:, 3 * g + 2:3 * g + 3] * o_win[r])


def nsa_prompt_attention(qs, kc, vc, k_slc, v_slc, k_win, v_win, zg):
    B, S = qs.shape[:2]
    bf16 = jnp.bfloat16
    assert S % NSA_CK == 0 and S % NSA_TQ == 0 and WINDOW % NSA_TQ == 0 and WINDOW + NSA_TQ <= S
    n_sel = S // SEL_BLOCK
    nb = kc.shape[1]
    assert nb <= 128

    def pad_blocks(a):
        return jnp.pad(a.transpose(0, 2, 1, 3).astype(bf16), ((0, 0), (0, 0), (0, 128 - nb), (0, 0)))

    zg4 = zg.reshape(B, S, NSA_KV_HEADS, 3 * NSA_GROUP).transpose(0, 2, 1, 3)
    c = np.arange(128)[:, None]
    jj = np.arange(n_sel)[None, :]
    ratio = SEL_BLOCK // CMP_STRIDE
    msel = ((c >= jj * ratio - (CMP_BLOCK // CMP_STRIDE - 1)) & (c <= jj * ratio + ratio - 1) & (c < nb))
    expand = (np.arange(S)[None, :] // SEL_BLOCK == np.arange(n_sel)[:, None])
    expand = expand.reshape(n_sel, S // NSA_CK, NSA_CK).transpose(1, 0, 2)
    row_spec = pl.BlockSpec((1, 1, S, NSA_HD), lambda b, n, i: (b, n, 0, 0))
    blk_spec = pl.BlockSpec((1, 1, 128, NSA_HD), lambda b, n, i: (b, n, 0, 0))
    return pl.pallas_call(
        _nsa_prompt_kernel,
        out_shape=jax.ShapeDtypeStruct((B, S, NSA_QW), jnp.float32),
        grid=(B, NSA_KV_HEADS, S // NSA_TQ),
        in_specs=[pl.BlockSpec((1, NSA_TQ, NSA_GROUP * NSA_HD), lambda b, n, i: (b, i, n)),
                  blk_spec, blk_spec, row_spec, row_spec, row_spec, row_spec,
                  pl.BlockSpec((1, 1, NSA_TQ, 3 * NSA_GROUP), lambda b, n, i: (b, n, i, 0)),
                  pl.BlockSpec((128, n_sel), lambda b, n, i: (0, 0)),
                  pl.BlockSpec((S // NSA_CK, n_sel, NSA_CK), lambda b, n, i: (0, 0, 0))],
        out_specs=pl.BlockSpec((1, NSA_TQ, NSA_GROUP * NSA_HD), lambda b, n, i: (b, i, n)),
        compiler_params=pltpu.CompilerParams(dimension_semantics=("arbitrary", "arbitrary", "arbitrary"),
                                             vmem_limit_bytes=VMEM_LIMIT_BYTES),
        name="nsa_prompt_attention",
    )(qs, pad_blocks(kc), pad_blocks(vc), k_slc, v_slc, k_win, v_win,
      zg4, jnp.asarray(msel, bf16), jnp.asarray(expand, bf16))


MLSTM_L = 128
CONV_HALO = 8


def _log_sigmoid(x):
    return -(jnp.maximum(-x, 0.0) + jnp.log1p(jnp.exp(-jnp.abs(x))))


def _mlstm_prompt_kernel(x_ref, xprev_ref, halo0_ref, v_ref, o_ref, gcol_ref, grow_ref, cw_ref, cb_ref,
                         out_ref, c_out, n_out, m_out, c_ref, n_ref, m_ref):
    f32, bf16 = jnp.float32, jnp.bfloat16
    c = pl.program_id(1)
    L = MLSTM_L

    @pl.when(c == 0)
    def _():
        c_ref[...] = jnp.zeros_like(c_ref)
        n_ref[...] = jnp.zeros_like(n_ref)
        m_ref[...] = jnp.zeros_like(m_ref)

    x = x_ref[0]
    halo = jnp.where(c == 0, halo0_ref[0], xprev_ref[0, L - CONV_HALO:L, :])
    ext = jnp.concatenate([halo, x], axis=0)
    conv = cb_ref[...]
    for j in range(CONV_W):
        o = CONV_HALO - (CONV_W - 1) + j
        conv = conv + ext[o:o + L] * cw_ref[j:j + 1, :]
    qk = conv * jax.nn.sigmoid(conv)

    t_id = lax.broadcasted_iota(jnp.int32, (L, L), 0)
    s_id = lax.broadcasted_iota(jnp.int32, (L, L), 1)
    causal = t_id >= s_id
    gcol = gcol_ref[0, 0]
    grow = grow_ref[0, 0]
    for h in range(M_HEADS):
        hd = slice(h * M_HD, (h + 1) * M_HD)
        q = qk[:, hd]
        k = qk[:, M_W + h * M_HD:M_W + (h + 1) * M_HD] * (M_HD ** -0.5)
        v = v_ref[0, :, hd].astype(bf16)
        ig_r = grow[h:h + 1, :]
        ig_c = gcol[:, h:h + 1]
        lf_r = _log_sigmoid(grow[M_HEADS + h:M_HEADS + h + 1, :])
        lf_c = _log_sigmoid(gcol[:, M_HEADS + h:M_HEADS + h + 1])
        b_c = jnp.sum(jnp.where(causal, lf_r, 0.0), axis=1, keepdims=True)
        b_r = jnp.sum(jnp.where(t_id <= s_id, lf_c, 0.0), axis=0, keepdims=True)
        m_prev = m_ref[h]
        dmat = jnp.where(causal, b_c - b_r + ig_r, -jnp.inf)
        inter = b_c + m_prev
        m_t = jnp.maximum(inter, jnp.max(dmat, axis=1, keepdims=True))
        w_intra = jnp.exp(dmat - m_t)
        w_inter = jnp.exp(inter - m_t)
        qb = q.astype(bf16)
        s = lax.dot_general(qb, k.astype(bf16), _NT, preferred_element_type=f32) * w_intra
        num = (jnp.dot(s.astype(bf16), v, preferred_element_type=f32)
               + w_inter * jnp.dot(qb, c_ref[h].astype(bf16), preferred_element_type=f32))
        den = jnp.sum(s, axis=1, keepdims=True) + w_inter * jnp.sum(q * n_ref[h], axis=1, keepdims=True)
        hh = num / jnp.maximum(jnp.abs(den), jnp.exp(-m_t))
        out_ref[0, :, hd] = jax.nn.sigmoid(o_ref[0, :, hd]) * hh
        m_new = m_t[L - 1:L]
        b_last = b_c[L - 1:L]
        w_s = jnp.exp(b_last - b_c + ig_c - m_new)
        w_p = jnp.exp(b_last + m_prev - m_new)
        kw = k * w_s
        c_ref[h] = w_p * c_ref[h] + jnp.dot(kw.T.astype(bf16), v, preferred_element_type=f32)
        n_ref[h] = w_p * n_ref[h] + jnp.sum(kw, axis=0, keepdims=True)
        m_ref[h] = m_new

    @pl.when(c == pl.num_programs(1) - 1)
    def _():
        c_out[0] = c_ref[...]
        n_out[0] = n_ref[...]
        m_out[0] = m_ref[...]


def mlstm_prompt(zqk, zv, zo, zif, conv_w, conv_b, b_if):
    B, T, _ = zqk.shape
    L = MLSTM_L
    assert T % L == 0
    nc = T // L
    f32 = jnp.float32
    gif = zif + b_if
    gcol = gif.reshape(B, nc, L, 2 * M_HEADS)
    grow = gcol.transpose(0, 1, 3, 2)
    halo0 = jnp.zeros((B, CONV_HALO, 2 * M_W), f32)
    out, C, n, m = pl.pallas_call(
        _mlstm_prompt_kernel,
        out_shape=(jax.ShapeDtypeStruct((B, T, M_W), f32),
                   jax.ShapeDtypeStruct((B, M_HEADS, M_HD, M_HD), f32),
                   jax.ShapeDtypeStruct((B, M_HEADS, 1, M_HD), f32),
                   jax.ShapeDtypeStruct((B, M_HEADS, 1, 1), f32)),
        grid=(B, nc),
        in_specs=[pl.BlockSpec((1, L, 2 * M_W), lambda b, c: (b, c, 0)),
                  pl.BlockSpec((1, L, 2 * M_W), lambda b, c: (b, jnp.maximum(c - 1, 0), 0)),
                  pl.BlockSpec((1, CONV_HALO, 2 * M_W), lambda b, c: (b, 0, 0)),
                  pl.BlockSpec((1, L, M_W), lambda b, c: (b, c, 0)),
                  pl.BlockSpec((1, L, M_W), lambda b, c: (b, c, 0)),
                  pl.BlockSpec((1, 1, L, 2 * M_HEADS), lambda b, c: (b, c, 0, 0)),
                  pl.BlockSpec((1, 1, 2 * M_HEADS, L), lambda b, c: (b, c, 0, 0)),
                  pl.BlockSpec((CONV_W, 2 * M_W), lambda b, c: (0, 0)),
                  pl.BlockSpec((1, 2 * M_W), lambda b, c: (0, 0))],
        out_specs=(pl.BlockSpec((1, L, M_W), lambda b, c: (b, c, 0)),
                   pl.BlockSpec((1, M_HEADS, M_HD, M_HD), lambda b, c: (b, 0, 0, 0)),
                   pl.BlockSpec((1, M_HEADS, 1, M_HD), lambda b, c: (b, 0, 0, 0)),
                   pl.BlockSpec((1, M_HEADS, 1, 1), lambda b, c: (b, 0, 0, 0))),
        scratch_shapes=[pltpu.VMEM((M_HEADS, M_HD, M_HD), f32), pltpu.VMEM((M_HEADS, 1, M_HD), f32),
                        pltpu.VMEM((M_HEADS, 1, 1), f32)],
        compiler_params=pltpu.CompilerParams(dimension_semantics=("arbitrary", "arbitrary"),
                                             vmem_limit_bytes=VMEM_LIMIT_BYTES),
        name="mlstm_prompt",
    )(zqk, zqk, halo0, zv, zo, gcol, grow, conv_w, conv_b[None])
    return out, C, n.reshape(B, M_HEADS, M_HD), m.reshape(B, M_HEADS)


def mlstm_chunk(carry, inp):
    C, n, m = carry
    q, k, v, ig, lf = inp
    L = q.shape[2]
    b = jnp.cumsum(lf, axis=-1)
    causal = jnp.tril(jnp.ones((L, L), dtype=bool))
    dmat = jnp.where(causal, b[..., :, None] - b[..., None, :] + ig[..., None, :], -jnp.inf)
    inter = b + m[..., None]
    m_t = jnp.maximum(inter, dmat.max(axis=-1))
    w_intra = jnp.exp(dmat - m_t[..., None])
    w_inter = jnp.exp(inter - m_t)
    s = jnp.einsum('bhtd,bhsd->bhts', q, k) * w_intra
    num = jnp.einsum('bhts,bhsv->bhtv', s, v) + w_inter[..., None] * jnp.einsum('bhtd,bhdv->bhtv', q, C)
    den = s.sum(-1) + w_inter * jnp.einsum('bhtd,bhd->bht', q, n)
    h = num / jnp.maximum(jnp.abs(den), jnp.exp(-m_t))[..., None]
    m_new = m_t[..., -1]
    w_s = jnp.exp(b[..., -1:] - b + ig - m_new[..., None])
    w_p = jnp.exp(b[..., -1] + m - m_new)
    C_new = w_p[..., None, None] * C + jnp.einsum('bhs,bhsd,bhsv->bhdv', w_s, k, v)
    n_new = w_p[..., None] * n + jnp.einsum('bhs,bhsd->bhd', w_s, k)
    return (C_new, n_new, m_new), h


def mlstm_mix(zqk, zv, zo, zif, buf0, C0, n0, m0, conv_w, conv_b, b_if, chunk):
    B, T, _ = zqk.shape
    full = jnp.concatenate([buf0, zqk], axis=1)
    conv = conv_b
    for j in range(CONV_W):
        conv = conv + full[:, j:j + T] * conv_w[j]
    qk = jax.nn.silu(conv)

    def heads(a):
        return a.reshape(B, T, M_HEADS, M_HD).transpose(0, 2, 1, 3)

    q = heads(qk[..., :M_W])
    k = heads(qk[..., M_W:]) * (M_HD ** -0.5)
    v = heads(zv)
    gif = zif + b_if
    ig = gif[..., :M_HEADS].transpose(0, 2, 1)
    lf = jax.nn.log_sigmoid(gif[..., M_HEADS:]).transpose(0, 2, 1)
    nc = T // chunk

    def to_chunks(a):
        return jnp.moveaxis(a.reshape(B, M_HEADS, nc, chunk, *a.shape[3:]), 2, 0)

    (C, n, m), h = lax.scan(mlstm_chunk, (C0, n0, m0),
                            (to_chunks(q), to_chunks(k), to_chunks(v), to_chunks(ig), to_chunks(lf)))
    h = jnp.moveaxis(h, 0, 2).reshape(B, M_HEADS, T, M_HD).transpose(0, 2, 1, 3).reshape(B, T, M_W)
    out = jax.nn.sigmoid(zo) * h
    return out, (C, n, m, full[:, T:])


PEER_COMBOS = 2 * PEER_HEADS
PEER_KEY_ROWS = 8
PEER_TILE = PEER_KEY_ROWS * N_KEYS
PEER_TS_ROWS = 24
LANES = 128
_NT = (((1,), (1,)), ((), ()))


def _peer_topk_kernel(q_ref, keys_ref, s_ref, e0_ref, e1_ref, tau_ref, ts_ref):
    c = pl.program_id(1)
    tt = q_ref.shape[0]
    s = lax.dot_general(keys_ref[0], q_ref[...].astype(jnp.bfloat16), _NT,
                        preferred_element_type=jnp.float32)
    s_ref[c] = s
    key_id = lax.broadcasted_iota(jnp.int32, s.shape, 0)
    work = s
    rows = []
    for _ in range(PEER_TOPK + 1):
        m = jnp.max(work, axis=0, keepdims=True)
        first = jnp.min(jnp.where(work == m, key_id, N_KEYS), axis=0, keepdims=True)
        work = jnp.where(key_id == first, -jnp.inf, work)
        rows.append(m)
    rows.append(jnp.full((PEER_TS_ROWS - PEER_TOPK - 1, tt), -jnp.inf, jnp.float32))
    ts_ref[c] = jnp.concatenate(rows, axis=0)

    @pl.when(c == PEER_COMBOS - 1)
    def _():
        for h in range(PEER_HEADS):
            t0 = ts_ref[2 * h]
            t1 = ts_ref[2 * h + 1]
            pieces = [t0[0:1] + t1] + [t0[a:a + 1] + t1[0:8] for a in range(1, 8)] + [t0[8:24] + t1[0:1]]
            cand = jnp.concatenate(pieces, axis=0)
            top = t0[0:1] + t1[0:1]
            v16 = top
            v17 = top
            z = jnp.zeros_like(top)
            seen = jnp.zeros_like(top)
            for _ in range(PEER_TOPK + 1):
                m = jnp.max(cand, axis=0, keepdims=True)
                eq = cand == m
                cnt = jnp.sum(jnp.where(eq, 1.0, 0.0), axis=0, keepdims=True)
                active = seen < PEER_TOPK
                take = jnp.minimum(cnt, PEER_TOPK - seen)
                v16 = jnp.where(active, m, v16)
                v17 = jnp.where(seen < PEER_TOPK + 1, m, v17)
                z = z + jnp.where(active, take * jnp.exp(m - top), 0.0)
                seen = seen + cnt
                cand = jnp.where(eq, -jnp.inf, cand)
            tau_ref[h:h + 1, :] = 0.5 * v16 + 0.5 * v17
            e0_ref[h] = jnp.exp(s_ref[2 * h] - t0[0:1]) / z
            e1_ref[h] = jnp.exp(s_ref[2 * h + 1] - t1[0:1])


def peer_scores(q, sub_keys, tt):
    n = q.shape[0]
    assert n % tt == 0
    keys = sub_keys.reshape(PEER_COMBOS, N_KEYS, PEER_QDIM // 2).astype(jnp.bfloat16)
    f32 = jnp.float32
    return pl.pallas_call(
        _peer_topk_kernel,
        out_shape=(jax.ShapeDtypeStruct((PEER_COMBOS, N_KEYS, n), f32),
                   jax.ShapeDtypeStruct((PEER_HEADS, N_KEYS, n), f32),
                   jax.ShapeDtypeStruct((PEER_HEADS, N_KEYS, n), f32),
                   jax.ShapeDtypeStruct((PEER_HEADS, n), f32)),
        grid=(n // tt, PEER_COMBOS),
        in_specs=[pl.BlockSpec((tt, PEER_QDIM // 2), lambda i, c: (i, c)),
                  pl.BlockSpec((1, N_KEYS, PEER_QDIM // 2), lambda i, c: (c, 0, 0))],
        out_specs=(pl.BlockSpec((PEER_COMBOS, N_KEYS, tt), lambda i, c: (0, 0, i)),
                   pl.BlockSpec((PEER_HEADS, N_KEYS, tt), lambda i, c: (0, 0, i)),
                   pl.BlockSpec((PEER_HEADS, N_KEYS, tt), lambda i, c: (0, 0, i)),
                   pl.BlockSpec((PEER_HEADS, tt), lambda i, c: (0, i))),
        scratch_shapes=[pltpu.VMEM((PEER_COMBOS, PEER_TS_ROWS, tt), f32)],
        compiler_params=pltpu.CompilerParams(dimension_semantics=("arbitrary", "arbitrary"),
                                             vmem_limit_bytes=VMEM_LIMIT_BYTES),
        name="peer_topk",
    )(q, keys)


def _peer_dense_kernel(xb_ref, h_ref, u_ref, vt_ref, s0_ref, ez_ref, s_ref, e1_ref, tau_ref, g_ref, b_ref,
                       o_ref, acc_ref, a_ref, w_ref):
    e = pl.program_id(1)
    tt = xb_ref.shape[0]

    @pl.when(e == 0)
    def _():
        acc_ref[...] = jnp.zeros_like(acc_ref)

    a_ref[...] = lax.dot_general(u_ref[...], xb_ref[...], _NT, preferred_element_type=jnp.float32)
    for r in range(PEER_KEY_ROWS):
        rows = slice(r * N_KEYS, (r + 1) * N_KEYS)
        for t in range(tt // LANES):
            tok = slice(t * LANES, (t + 1) * LANES)
            gate = jnp.zeros((N_KEYS, LANES), jnp.float32)
            for h in range(PEER_HEADS):
                need = tau_ref[h:h + 1, tok] - s0_ref[2 * h, r:r + 1, tok]
                picked = jnp.where(s_ref[2 * h + 1, :, tok] >= need, e1_ref[h, :, tok], 0.0)
                gate = gate + picked * ez_ref[h, r:r + 1, tok]
            ar = a_ref[rows, tok]
            act = 0.5 * ar * (1.0 + lax.erf(ar * (2.0 ** -0.5)))
            w_ref[rows, tok] = (gate * act).astype(jnp.bfloat16)
    acc_ref[...] += jnp.dot(vt_ref[...], w_ref[...], preferred_element_type=jnp.float32)

    @pl.when(e == pl.num_programs(1) - 1)
    def _():
        r = ALPHA * h_ref[...] + acc_ref[...].T
        mu = jnp.mean(r, axis=-1, keepdims=True)
        d = r - mu
        var = jnp.mean(d * d, axis=-1, keepdims=True)
        o_ref[...] = d * lax.rsqrt(var + LN_EPS) * g_ref[...] + b_ref[...]


def peer_tail(h, hb, q, sub_keys, u_bf, vt_bf, ln_g, ln_b, tt):
    n, d = h.shape
    s, e0z, e1, tau = peer_scores(q, sub_keys, tt)
    n_exp = u_bf.shape[0]
    return pl.pallas_call(
        _peer_dense_kernel,
        out_shape=jax.ShapeDtypeStruct((n, d), jnp.float32),
        grid=(n // tt, n_exp // PEER_TILE),
        in_specs=[pl.BlockSpec((tt, d), lambda i, e: (i, 0)),
                  pl.BlockSpec((tt, d), lambda i, e: (i, 0)),
                  pl.BlockSpec((PEER_TILE, d), lambda i, e: (e, 0)),
                  pl.BlockSpec((d, PEER_TILE), lambda i, e: (0, e)),
                  pl.BlockSpec((PEER_COMBOS, PEER_KEY_ROWS, tt), lambda i, e: (0, e, i)),
                  pl.BlockSpec((PEER_HEADS, PEER_KEY_ROWS, tt), lambda i, e: (0, e, i)),
                  pl.BlockSpec((PEER_COMBOS, N_KEYS, tt), lambda i, e: (0, 0, i)),
                  pl.BlockSpec((PEER_HEADS, N_KEYS, tt), lambda i, e: (0, 0, i)),
                  pl.BlockSpec((PEER_HEADS, tt), lambda i, e: (0, i)),
                  pl.BlockSpec((1, d), lambda i, e: (0, 0)),
                  pl.BlockSpec((1, d), lambda i, e: (0, 0))],
        out_specs=pl.BlockSpec((tt, d), lambda i, e: (i, 0)),
        scratch_shapes=[pltpu.VMEM((d, tt), jnp.float32), pltpu.VMEM((PEER_TILE, tt), jnp.float32),
                        pltpu.VMEM((PEER_TILE, tt), jnp.bfloat16)],
        compiler_params=pltpu.CompilerParams(dimension_semantics=("arbitrary", "arbitrary"),
                                             vmem_limit_bytes=VMEM_LIMIT_BYTES),
        name="peer_dense",
    )(hb, h, u_bf, vt_bf, s, e0z, s, e1, tau, ln_g[None], ln_b[None])


def _out_proj_kernel(x_ref, nsa_ref, m_ref, wn_ref, wm_ref, g_ref, b_ref, wq_ref, h_ref, hb_ref, q_ref):
    f32, bf16 = jnp.float32, jnp.bfloat16
    r = (ALPHA * x_ref[...] + jnp.dot(nsa_ref[...].astype(bf16), wn_ref[...], preferred_element_type=f32)
         + jnp.dot(m_ref[...].astype(bf16), wm_ref[...], preferred_element_type=f32))
    mu = jnp.mean(r, axis=-1, keepdims=True)
    d = r - mu
    var = jnp.mean(d * d, axis=-1, keepdims=True)
    h = d * lax.rsqrt(var + LN_EPS) * g_ref[...] + b_ref[...]
    h_ref[...] = h
    hb = h.astype(bf16)
    hb_ref[...] = hb
    q_ref[...] = jnp.dot(hb, wq_ref[...], preferred_element_type=f32)


def out_proj_fused(x, o_nsa, o_m, w_out, ln_g, ln_b, w_pq, tm):
    n, d = x.shape
    assert n % tm == 0
    bf16 = jnp.bfloat16
    nq = w_pq.shape[1]

    def rows(width):
        return pl.BlockSpec((tm, width), lambda i: (i, 0))

    def whole(a):
        return pl.BlockSpec(a.shape, lambda i: (0, 0))

    wn = w_out[:NSA_QW].astype(bf16)
    wm = w_out[NSA_QW:].astype(bf16)
    wq = w_pq.astype(bf16)
    g, b = ln_g[None], ln_b[None]
    return pl.pallas_call(
        _out_proj_kernel,
        out_shape=(jax.ShapeDtypeStruct((n, d), jnp.float32), jax.ShapeDtypeStruct((n, d), bf16),
                   jax.ShapeDtypeStruct((n, nq), jnp.float32)),
        grid=(n // tm,),
        in_specs=[rows(d), rows(NSA_QW), rows(M_W), whole(wn), whole(wm), whole(g), whole(b), whole(wq)],
        out_specs=(rows(d), rows(d), rows(nq)),
        compiler_params=pltpu.CompilerParams(dimension_semantics=("arbitrary",),
                                             vmem_limit_bytes=VMEM_LIMIT_BYTES),
        name="out_proj",
    )(x, o_nsa, o_m, wn, wm, g, b, wq)


def block_tail(x, o_nsa, o_m, w_out, ln_g, ln_b, w_pq, sub_keys, u_bf, vt_bf, tt):
    lead = x.shape[:-1]
    h, hb, q = out_proj_fused(x.reshape(-1, D_MODEL), o_nsa.reshape(-1, NSA_QW), o_m.reshape(-1, M_W),
                              w_out, ln_g[0], ln_b[0], w_pq, tt)
    return peer_tail(h, hb, q, sub_keys, u_bf, vt_bf, ln_g[1], ln_b[1], tt).reshape(*lead, D_MODEL)


def prompt_mix(x, w_in, pe, w1, b1, w2, conv_w, conv_b, b_if):
    B, S, _ = x.shape
    z = in_proj_fused(x, w_in, jnp.arange(S), 512, True)
    chunk_w = CMP_STRIDE * NSA_KVW

    def compressed(rows, c):
        proj = chunk_projection(rows.reshape(B * (S // CMP_STRIDE), chunk_w), w1[c])
        return compress_from_projection(proj.reshape(B, S // CMP_STRIDE, -1), pe[c], w1[c], b1[c], w2[c])

    kc = compressed(z["k_cmp"], 0)
    vc = compressed(z["v_cmp"], 1)
    zgate = z["zgate"].reshape(B, S, GATE_W)
    o_nsa = nsa_prompt_attention(z["q"].reshape(B, S, NSA_QW), kc, vc, z["k_slc_bf"], z["v_slc_bf"],
                                 z["k_win_bf"], z["v_win_bf"], zgate[..., :IN_SPLITS[2]])
    zqk = z["zqk"].reshape(B, S, 2 * M_W)
    o_m, C, n, m = mlstm_prompt(zqk, z["zv"].reshape(B, S, M_W), z["zo"].reshape(B, S, M_W),
                                zgate[..., IN_SPLITS[2]:], conv_w, conv_b, b_if)
    buf = zqk[:, S - (CONV_W - 1):]
    wl = min(WINDOW, S)
    k_cmp, v_cmp, k_slc, v_slc, k_win, v_win = [
        z[k].reshape(B, S, NSA_KV_HEADS, NSA_HD) for k in ("k_cmp", "v_cmp", "k_slc", "v_slc", "k_win", "v_win")]
    return (o_nsa, o_m), (k_cmp, v_cmp, k_slc, v_slc, k_win[:, S - wl:], v_win[:, S - wl:], C, n, m, buf)


def sample_mix(x, kc_pool, vc_pool, ks_pool, vs_pool, kw_buf, vw_buf, C0, n0, m0, buf0, page_table,
               w_in, pe, w1, b1, w2, conv_w, conv_b, b_if):
    B, T, _ = x.shape
    past = page_table.shape[1] * PAGE_SIZE
    pos = past + jnp.arange(T)
    z = in_proj_fused(x, w_in, pos, B * T, False)
    q = z["q"].astype(jnp.float32).reshape(B, T, NSA_HEADS, NSA_HD) * (1.0 / ATTN_SCALE)
    k_cmp, v_cmp, k_slc, v_slc, k_win, v_win = [
        z[k].reshape(B, T, NSA_KV_HEADS, NSA_HD) for k in ("k_cmp", "v_cmp", "k_slc", "v_slc", "k_win", "v_win")]
    zgate = z["zgate"].reshape(B, T, GATE_W)
    gates = jax.nn.sigmoid(zgate[..., :IN_SPLITS[2]]).reshape(B, T, NSA_HEADS, 3)
    zqk, zv, zo, zif = (z["zqk"].reshape(B, T, 2 * M_W), z["zv"].reshape(B, T, M_W), z["zo"].reshape(B, T, M_W),
                        zgate[..., IN_SPLITS[2]:])

    assert (past + T) // CMP_STRIDE == past // CMP_STRIDE

    def compressed(pool, c):
        pages = pool.transpose(0, 2, 3, 1)[page_table]
        return compress_from_projection(page_projection(pages, w1[c]), pe[c], w1[c], b1[c], w2[c])

    o_cmp, p = cmp_attend(q, pos, compressed(kc_pool, 0), compressed(vc_pool, 1))
    n_sel = -(-(past + T) // SEL_BLOCK)
    member = select_blocks(p, pos, n_sel)
    o_sel = sample_selected_attention(q, pos, member, ks_pool, vs_pool, k_slc, v_slc, page_table)
    wb = kw_buf.shape[1]
    kw = jnp.concatenate([kw_buf, k_win], axis=1)
    vw = jnp.concatenate([vw_buf, v_win], axis=1)
    kpos = past - wb + jnp.arange(wb + T)
    o_win = win_attend(q, pos, kw, vw, kpos)
    o_nsa = nsa_combine(gates, o_cmp, o_sel, o_win)
    o_m, (C, n, m, buf) = mlstm_mix(zqk, zv, zo, zif, buf0, C0, n0, m0, conv_w, conv_b, b_if, T)
    return (o_nsa, o_m), (k_cmp, v_cmp, k_slc, v_slc, kw[:, T:], vw[:, T:], C, n, m, buf)


def kernel(x_prompt, x_sample, cache_k_cmp, cache_v_cmp, cache_k_slc, cache_v_slc, cache_k_win, cache_v_win,
           state_C, state_n, state_m, state_conv, page_table, w_in, w_out, w_phi1, b_phi1, w_phi2, pe_cmp,
           conv_w, conv_b, b_if, ln_g, ln_b, w_pq, sub_keys, u_tab, v_tab):
    l = 0
    mix_p, st_p = prompt_mix(x_prompt, w_in[l], pe_cmp[l], w_phi1[l], b_phi1[l], w_phi2[l],
                             conv_w[l], conv_b[l], b_if[l])
    mix_s, st_s = sample_mix(x_sample, cache_k_cmp[l], cache_v_cmp[l], cache_k_slc[l], cache_v_slc[l],
                             cache_k_win[l], cache_v_win[l], state_C[l], state_n[l], state_m[l],
                             state_conv[l], page_table, w_in[l], pe_cmp[l], w_phi1[l], b_phi1[l],
                             w_phi2[l], conv_w[l], conv_b[l], b_if[l])
    u_bf = u_tab[l].astype(jnp.bfloat16)
    vt_bf = v_tab[l].astype(jnp.bfloat16).T
    xp = block_tail(x_prompt, *mix_p, w_out[l], ln_g[l], ln_b[l], w_pq[l], sub_keys[l], u_bf, vt_bf, 512)
    xs = block_tail(x_sample, *mix_s, w_out[l], ln_g[l], ln_b[l], w_pq[l], sub_keys[l], u_bf, vt_bf, 128)
    return (xp, xs) + tuple(a[None] for a in st_p) + tuple(a[None] for a in st_s)
```

```python
import functools

import jax
import jax.numpy as jnp
import numpy as np
from jax import lax
from jax.experimental import pallas as pl
from jax.experimental.pallas import tpu as pltpu

D_MODEL = 1024
DEPTH = 1
PAGE_SIZE = 128
NSA_HEADS = 8
NSA_KV_HEADS = 2
NSA_GROUP = NSA_HEADS // NSA_KV_HEADS
NSA_HD = 64
NSA_QW = NSA_HEADS * NSA_HD
NSA_KVW = NSA_KV_HEADS * NSA_HD
CMP_BLOCK = 32
CMP_STRIDE = 16
SEL_BLOCK = 64
SEL_TOP = 16
WINDOW = 512
Q_BLOCK = 64
ATTN_SCALE = NSA_HD ** -0.5
ROPE_THETA = 10000.0
M_HEADS = 4
M_HD = 128
M_W = M_HEADS * M_HD
M_CHUNK = 64
CONV_W = 4
PEER_HEADS = 8
N_KEYS = 128
PEER_TOPK = 16
PEER_QDIM = 256
PEER_BLOCK = 128
IN_SPLITS = (NSA_QW, 6 * NSA_KVW, 3 * NSA_HEADS, 2 * M_W, M_W, M_W, 2 * M_HEADS)
LN_EPS = 1e-5
ALPHA = (2 * DEPTH) ** 0.25

VMEM_LIMIT_BYTES = 56 * 1024 * 1024


def _mm_kernel(x_ref, w_ref, o_ref):
    o_ref[...] = jnp.dot(x_ref[...].astype(jnp.bfloat16), w_ref[...], preferred_element_type=jnp.float32)


def pallas_matmul(x, w, tm=512):
    M, K = x.shape
    N = w.shape[1]
    tm = min(tm, M)
    assert M % tm == 0
    return pl.pallas_call(
        _mm_kernel,
        out_shape=jax.ShapeDtypeStruct((M, N), jnp.float32),
        grid=(M // tm,),
        in_specs=[pl.BlockSpec((tm, K), lambda i: (i, 0)), pl.BlockSpec((K, N), lambda i: (0, 0))],
        out_specs=pl.BlockSpec((tm, N), lambda i: (i, 0)),
        compiler_params=pltpu.CompilerParams(dimension_semantics=("arbitrary",),
                                             vmem_limit_bytes=VMEM_LIMIT_BYTES),
        name="proj_matmul",
    )(x, w.astype(jnp.bfloat16))


def mm3(x, w):
    lead = x.shape[:-1]
    return pallas_matmul(x.reshape(-1, x.shape[-1]), w).reshape(*lead, w.shape[1])


def layer_norm(x, g, b):
    mu = x.mean(-1, keepdims=True)
    var = jnp.square(x - mu).mean(-1, keepdims=True)
    return (x - mu) * lax.rsqrt(var + LN_EPS) * g + b


def rope(x, pos):
    half = x.shape[-1] // 2
    inv = ROPE_THETA ** (-jnp.arange(half, dtype=jnp.float32) / half)
    ang = pos.astype(jnp.float32)[:, None] * inv[None, :]
    cos = jnp.cos(ang)[:, None, :]
    sin = jnp.sin(ang)[:, None, :]
    x1, x2 = x[..., :half], x[..., half:]
    return jnp.concatenate([x1 * cos - x2 * sin, x2 * cos + x1 * sin], axis=-1)


def split_in_proj(x, w_in):
    z = mm3(x, w_in)
    cuts = [int(c) for c in np.cumsum(IN_SPLITS)[:-1]]
    return jnp.split(z, cuts, axis=-1)


_IN_OFF = np.concatenate([[0], np.cumsum(IN_SPLITS)])
_IN_ORDER = (0, 1, 3, 4, 5, 2, 6)
_N_KV_ROWS = 6
_KV_BF16 = (2, 3, 4, 5)
GATE_W = IN_SPLITS[2] + IN_SPLITS[6]


def _rope_pairs(x, cos, sin_signed):
    half = NSA_HD // 2
    lane = lax.broadcasted_iota(jnp.int32, x.shape, 1)
    partner = jnp.where(lane % NSA_HD < half, pltpu.roll(x, LANES - half, 1), pltpu.roll(x, half, 1))
    return x * cos + partner * sin_signed


def _in_proj_kernel(x_ref, w_ref, cos_ref, sin_ref, q_ref, *rest, kv_major):
    kv_refs = rest[:_N_KV_ROWS]
    rest = rest[_N_KV_ROWS:]
    if kv_major:
        bf_refs, rest = rest[:len(_KV_BF16)], rest[len(_KV_BF16):]
    zqk_ref, zv_ref, zo_ref, zgate_ref = rest
    z = jnp.dot(x_ref[...].astype(jnp.bfloat16), w_ref[...], preferred_element_type=jnp.float32)
    cos = cos_ref[...]
    sin = sin_ref[...]
    for g in range(NSA_QW // LANES):
        sl = slice(g * LANES, (g + 1) * LANES)
        q_ref[:, sl] = (_rope_pairs(z[:, sl], cos, sin) * ATTN_SCALE).astype(jnp.bfloat16)
    for r in range(_N_KV_ROWS):
        row = z[:, NSA_QW + r * NSA_KVW:NSA_QW + (r + 1) * NSA_KVW]
        if r % 2 == 0:
            row = _rope_pairs(row, cos, sin)
        kv_refs[r][...] = row
        if kv_major and r in _KV_BF16:
            dst = bf_refs[_KV_BF16.index(r)]
            for n in range(NSA_KV_HEADS):
                dst[0, n] = row[:, n * NSA_HD:(n + 1) * NSA_HD].astype(jnp.bfloat16)
    o = NSA_QW + _N_KV_ROWS * NSA_KVW
    zqk_ref[...] = z[:, o:o + 2 * M_W]
    zv_ref[...] = z[:, o + 2 * M_W:o + 3 * M_W]
    zo_ref[...] = z[:, o + 3 * M_W:o + 4 * M_W]
    zgate_ref[...] = z[:, o + 4 * M_W:o + 4 * M_W + GATE_W]


def in_proj_fused(x, w_in, pos, tm, kv_major):
    B, T, D = x.shape
    M = B * T
    assert M % tm == 0 and NSA_KVW == LANES and (not kv_major or T % tm == 0)
    f32, bf16 = jnp.float32, jnp.bfloat16
    w = jnp.concatenate([w_in[:, _IN_OFF[i]:_IN_OFF[i + 1]] for i in _IN_ORDER], axis=1).astype(bf16)
    half = NSA_HD // 2
    inv = ROPE_THETA ** (-jnp.arange(half, dtype=f32) / half)
    ang = pos.astype(f32)[:, None] * inv[None, :]
    cos = jnp.tile(jnp.cos(ang), (B, 2 * LANES // NSA_HD))
    sin = jnp.tile(jnp.concatenate([-jnp.sin(ang), jnp.sin(ang)], axis=1), (B, LANES // NSA_HD))
    n_w = w.shape[1]

    def rows(width):
        return pl.BlockSpec((tm, width), lambda i: (i, 0))

    out_shape = [jax.ShapeDtypeStruct((M, NSA_QW), bf16)] + [jax.ShapeDtypeStruct((M, NSA_KVW), f32)] * _N_KV_ROWS
    out_specs = [rows(NSA_QW)] + [rows(NSA_KVW)] * _N_KV_ROWS
    if kv_major:
        per_seq = T // tm
        out_shape += [jax.ShapeDtypeStruct((B, NSA_KV_HEADS, T, NSA_HD), bf16)] * len(_KV_BF16)
        out_specs += [pl.BlockSpec((1, NSA_KV_HEADS, tm, NSA_HD),
                                   lambda i: (i // per_seq, 0, i % per_seq, 0))] * len(_KV_BF16)
    out_shape += [jax.ShapeDtypeStruct((M, 2 * M_W), f32), jax.ShapeDtypeStruct((M, M_W), f32),
                  jax.ShapeDtypeStruct((M, M_W), f32), jax.ShapeDtypeStruct((M, GATE_W), f32)]
    out_specs += [rows(2 * M_W), rows(M_W), rows(M_W), rows(GATE_W)]
    outs = pl.pallas_call(
        functools.partial(_in_proj_kernel, kv_major=kv_major),
        out_shape=tuple(out_shape),
        grid=(M // tm,),
        in_specs=[rows(D), pl.BlockSpec((D, n_w), lambda i: (0, 0)), rows(LANES), rows(LANES)],
        out_specs=tuple(out_specs),
        compiler_params=pltpu.CompilerParams(dimension_semantics=("arbitrary",),
                                             vmem_limit_bytes=VMEM_LIMIT_BYTES),
        name="in_proj",
    )(x.reshape(M, D), w, cos, sin)
    names = ["q", "k_cmp", "v_cmp", "k_slc", "v_slc", "k_win", "v_win"]
    if kv_major:
        names += ["k_slc_bf", "v_slc_bf", "k_win_bf", "v_win_bf"]
    names += ["zqk", "zv", "zo", "zgate"]
    return dict(zip(names, outs))


def nsa_project(zq, zkv, zg, pos):
    B, T, _ = zq.shape
    q = rope(zq.reshape(B, T, NSA_HEADS, NSA_HD), pos)
    kv = zkv.reshape(B, T, 6, NSA_KV_HEADS, NSA_HD)
    rows = (rope(kv[:, :, 0], pos), kv[:, :, 1], rope(kv[:, :, 2], pos), kv[:, :, 3],
            rope(kv[:, :, 4], pos), kv[:, :, 5])
    gates = jax.nn.sigmoid(zg).reshape(B, T, NSA_HEADS, 3)
    return q, rows, gates


def _expanded_w1(w1):
    assert CMP_BLOCK == 2 * CMP_STRIDE
    w1r = w1.reshape(2, CMP_STRIDE, NSA_HD, w1.shape[-1])
    wbig = jnp.einsum('hpdf,kn->pkdnhf', w1r, jnp.eye(NSA_KV_HEADS, dtype=w1.dtype))
    return wbig.reshape(CMP_STRIDE * NSA_KVW, 2 * NSA_KV_HEADS * w1.shape[-1])


def _compress_rows(x, w_ref, bias_ref, w2_ref, o_ref):
    f32, bf16 = jnp.float32, jnp.bfloat16
    rows = x.shape[0]
    f = w2_ref.shape[0]
    proj = jnp.dot(x.astype(bf16), w_ref[...], preferred_element_type=f32)
    for n in range(NSA_KV_HEADS):
        first = proj[:, 2 * n * f:(2 * n + 1) * f]
        second = pltpu.roll(proj[:, (2 * n + 1) * f:(2 * n + 2) * f], rows - 1, 0)
        pre = first + second + bias_ref[...]
        hid = 0.5 * pre * (1.0 + lax.erf(pre * (2.0 ** -0.5)))
        o_ref[:, n * NSA_HD:(n + 1) * NSA_HD] = jnp.dot(hid.astype(bf16), w2_ref[...], preferred_element_type=f32)


def _compress_chunks_kernel(x_ref, w_ref, bias_ref, w2_ref, o_ref):
    _compress_rows(x_ref[...], w_ref, bias_ref, w2_ref, o_ref)


def _compress_weights(pe, w1, b1, w2):
    bf16 = jnp.bfloat16
    bias = jnp.dot(pe.reshape(-1), w1, precision=lax.Precision.HIGHEST) + b1
    return _expanded_w1(w1).astype(bf16), bias[None], w2.astype(bf16)


def compress_chunks(rows, per_seq, pe, w1, b1, w2, tm=512):
    chunks = rows.reshape(-1, CMP_STRIDE * NSA_KVW)
    n = chunks.shape[0]
    tm = min(tm, n)
    assert n % tm == 0 and tm % per_seq == 0
    wbig, bias, w2b = _compress_weights(pe, w1, b1, w2)

    def whole(a):
        return pl.BlockSpec(a.shape, lambda i: (0, 0))

    out = pl.pallas_call(
        _compress_chunks_kernel,
        out_shape=jax.ShapeDtypeStruct((n, NSA_KVW), jnp.float32),
        grid=(n // tm,),
        in_specs=[pl.BlockSpec((tm, chunks.shape[1]), lambda i: (i, 0)), whole(wbig), whole(bias), whole(w2b)],
        out_specs=pl.BlockSpec((tm, NSA_KVW), lambda i: (i, 0)),
        compiler_params=pltpu.CompilerParams(dimension_semantics=("arbitrary",),
                                             vmem_limit_bytes=VMEM_LIMIT_BYTES),
        name="compress_chunks",
    )(chunks, wbig, bias, w2b)
    return out.reshape(n // per_seq, per_seq, NSA_KV_HEADS, NSA_HD)[:, :-1]


PAGE_GROUP = 4


def _compress_pages_kernel(pg_ref, w_ref, bias_ref, w2_ref, o_ref, x_ref, t_ref):
    n_pages = pg_ref.shape[1]
    per_page = PAGE_SIZE // CMP_STRIDE
    group = PAGE_GROUP

    def place(i, carry):
        for u in range(group):
            g = i * group + u
            t_ref[u] = pg_ref[0, g].reshape(NSA_KVW, PAGE_SIZE).T
            row0 = pl.multiple_of(g * per_page, per_page)
            for p in range(CMP_STRIDE):
                x_ref[pl.ds(row0, per_page), p * NSA_KVW:(p + 1) * NSA_KVW] = (
                    t_ref.at[u][pl.ds(p, per_page, stride=CMP_STRIDE), :])
        return carry

    lax.fori_loop(0, n_pages // group, place, 0)
    _compress_rows(x_ref[...], w_ref, bias_ref, w2_ref, o_ref.at[0])


def compress_pages(pages, pe, w1, b1, w2):
    B, n_pages = pages.shape[:2]
    assert pages.shape[2:] == (NSA_KV_HEADS, NSA_HD, PAGE_SIZE) and NSA_KVW == LANES and PAGE_SIZE == LANES
    assert n_pages % PAGE_GROUP == 0
    wbig, bias, w2b = _compress_weights(pe, w1, b1, w2)
    rows = n_pages * (PAGE_SIZE // CMP_STRIDE)

    def whole(a):
        return pl.BlockSpec(a.shape, lambda b: (0, 0))

    out = pl.pallas_call(
        _compress_pages_kernel,
        out_shape=jax.ShapeDtypeStruct((B, rows, NSA_KVW), jnp.float32),
        grid=(B,),
        in_specs=[pl.BlockSpec((1, n_pages, NSA_KV_HEADS, NSA_HD, PAGE_SIZE), lambda b: (b, 0, 0, 0, 0)),
                  whole(wbig), whole(bias), whole(w2b)],
        out_specs=pl.BlockSpec((1, rows, NSA_KVW), lambda b: (b, 0, 0)),
        scratch_shapes=[pltpu.VMEM((rows, wbig.shape[0]), jnp.float32),
                        pltpu.VMEM((PAGE_GROUP, PAGE_SIZE, NSA_KVW), jnp.float32)],
        compiler_params=pltpu.CompilerParams(dimension_semantics=("arbitrary",),
                                             vmem_limit_bytes=VMEM_LIMIT_BYTES),
        name="compress_pages",
    )(pages, wbig, bias, w2b)
    return out.reshape(B, rows, NSA_KV_HEADS, NSA_HD)[:, :-1]


def cmp_attend(q, qpos, kc, vc):
    B, T = q.shape[:2]
    qg = q.reshape(B, T, NSA_KV_HEADS, NSA_GROUP, NSA_HD)
    s = jnp.einsum('btngd,bcnd->btngc', qg, kc) * ATTN_SCALE
    nblk = kc.shape[1]
    blk_end = jnp.arange(nblk) * CMP_STRIDE + CMP_BLOCK - 1
    valid = (blk_end[None, :] <= qpos[:, None])[None, :, None, None, :]
    p = jax.nn.softmax(jnp.where(valid, s, -1e30), axis=-1) * valid
    o = jnp.einsum('btngc,bcnd->btngd', p, vc)
    return o.reshape(B, T, NSA_HEADS, NSA_HD), p


def select_blocks(p, qpos, n_sel):
    imp = p.sum(axis=3)
    R = SEL_BLOCK // CMP_STRIDE
    r = CMP_BLOCK // CMP_STRIDE
    nb = imp.shape[-1]
    right = n_sel * R + R - 1 - nb
    padded = jnp.pad(imp, ((0, 0), (0, 0), (0, 0), (r - 1, right)))
    score = padded[..., 0:(n_sel - 1) * R + 1:R]
    for o in range(1, R + r - 1):
        score = score + padded[..., o:o + (n_sel - 1) * R + 1:R]
    j = jnp.arange(n_sel)[None, :]
    cur = (qpos // SEL_BLOCK)[:, None]
    valid = (j * SEL_BLOCK <= qpos[:, None])[None, :, None, :]
    forced = ((j == 0) | (j == cur) | (j == cur - 1))[None, :, None, :]
    score = jnp.where(forced, jnp.inf, jnp.where(valid, score, -jnp.inf))
    idx = j[0]
    before = (score[..., None, :] > score[..., :, None]) | ((score[..., None, :] == score[..., :, None])
                                                          & (idx[None, :] < idx[:, None]))
    return before.sum(-1) < min(SEL_TOP, n_sel)


def sample_selected_attention(q, qpos, member, k_pool, v_pool, k_new, v_new, page_table):
    B, T = q.shape[:2]
    n_pages = page_table.shape[1]
    per_page = PAGE_SIZE // SEL_BLOCK
    assert member.shape[-1] == n_pages * per_page + 1 and T <= SEL_BLOCK
    kp = k_pool.transpose(0, 2, 3, 1)[page_table]
    vp = v_pool.transpose(0, 2, 3, 1)[page_table]
    qg = q.reshape(B, T, NSA_KV_HEADS, NSA_GROUP, NSA_HD)
    s_past = jnp.einsum('btngd,bpndk->bntgpk', qg, kp) * ATTN_SCALE
    s_new = jnp.einsum('btngd,bsnd->bntgs', qg, k_new) * ATTN_SCALE
    m = member.transpose(0, 2, 1, 3)
    m_past = jnp.repeat(m[..., :-1].reshape(B, NSA_KV_HEADS, T, n_pages, per_page), SEL_BLOCK, axis=-1)
    kpos = (jnp.arange(n_pages) * PAGE_SIZE)[:, None] + jnp.arange(PAGE_SIZE)[None, :]
    m_past = m_past & (kpos[None, None, None] <= qpos[None, None, :, None, None])
    new_pos = n_pages * PAGE_SIZE + jnp.arange(T)
    m_new = m[..., -1:] & (new_pos[None, None, None, :] <= qpos[None, None, :, None])
    logits = jnp.concatenate(
        [jnp.where(m_past[:, :, :, None], s_past, -jnp.inf).reshape(B, NSA_KV_HEADS, T, NSA_GROUP, -1),
         jnp.where(m_new[:, :, :, None], s_new, -jnp.inf)], axis=-1)
    pr = jax.nn.softmax(logits, axis=-1)
    pr_past = pr[..., :n_pages * PAGE_SIZE].reshape(B, NSA_KV_HEADS, T, NSA_GROUP, n_pages, PAGE_SIZE)
    o = (jnp.einsum('bntgpk,bpndk->bntgd', pr_past, vp)
         + jnp.einsum('bntgs,bsnd->bntgd', pr[..., n_pages * PAGE_SIZE:], v_new))
    return o.transpose(0, 2, 1, 3, 4).reshape(B, T, NSA_HEADS, NSA_HD)


def to_blocks(rows, n_sel):
    B, L, KV, hd = rows.shape
    rows = jnp.pad(rows, ((0, 0), (0, n_sel * SEL_BLOCK - L), (0, 0), (0, 0)))
    return rows.reshape(B, n_sel, SEL_BLOCK, KV, hd).transpose(0, 3, 1, 2, 4)


def take_rows(table, idx):
    return table[idx]


def sel_attend(q, qpos, sel, kb, vb):
    B, Tq = q.shape[:2]
    k = sel.shape[-1]
    sel_t = sel.transpose(0, 2, 1, 3)
    gather = jax.vmap(jax.vmap(take_rows))
    kg = gather(kb, sel_t).reshape(B, NSA_KV_HEADS, Tq, k * SEL_BLOCK, NSA_HD)
    vg = gather(vb, sel_t).reshape(B, NSA_KV_HEADS, Tq, k * SEL_BLOCK, NSA_HD)
    kpos = (sel_t[..., None] * SEL_BLOCK + jnp.arange(SEL_BLOCK)).reshape(B, NSA_KV_HEADS, Tq, k * SEL_BLOCK)
    qg = q.reshape(B, Tq, NSA_KV_HEADS, NSA_GROUP, NSA_HD).transpose(0, 2, 1, 3, 4)
    s = jnp.einsum('bntgd,bntsd->bntgs', qg, kg) * ATTN_SCALE
    mask = kpos[:, :, :, None, :] <= qpos[None, None, :, None, None]
    pr = jax.nn.softmax(jnp.where(mask, s, -jnp.inf), axis=-1)
    o = jnp.einsum('bntgs,bntsd->bntgd', pr, vg)
    return o.transpose(0, 2, 1, 3, 4).reshape(B, Tq, NSA_HEADS, NSA_HD)


def win_attend(q, qpos, k, v, kpos):
    B, Tq = q.shape[:2]
    qg = q.reshape(B, Tq, NSA_KV_HEADS, NSA_GROUP, NSA_HD)
    s = jnp.einsum('btngd,bsnd->btngs', qg, k) * ATTN_SCALE
    diff = qpos[:, None] - kpos[None, :]
    mask = ((diff >= 0) & (diff < WINDOW) & (kpos[None, :] >= 0))[None, :, None, None, :]
    pr = jax.nn.softmax(jnp.where(mask, s, -jnp.inf), axis=-1)
    o = jnp.einsum('btngs,bsnd->btngd', pr, v)
    return o.reshape(B, Tq, NSA_HEADS, NSA_HD)


def nsa_combine(gates, o_cmp, o_sel, o_win):
    B, T = gates.shape[:2]
    o = gates[..., 0:1] * o_cmp + gates[..., 1:2] * o_sel + gates[..., 2:3] * o_win
    return o.reshape(B, T, NSA_QW)


NSA_TQ = 128
NSA_CK = 512
MASKED = -1e30


def _softmax_rows(s):
    m = jnp.max(s, axis=-1, keepdims=True)
    e = jnp.exp(s - m)
    return e / jnp.sum(e, axis=-1, keepdims=True)


def _nsa_prompt_kernel(q_ref, kc_ref, vc_ref, ks_ref, vs_ref, kw_ref, vw_ref, zg_ref, msel_ref, exp_ref, o_ref):
    f32, bf16 = jnp.float32, jnp.bfloat16
    tq = NSA_TQ
    q0 = pl.program_id(2) * tq
    qb = q_ref[0]
    qs = jnp.concatenate([qb[:, g * NSA_HD:(g + 1) * NSA_HD] for g in range(NSA_GROUP)], axis=0)
    tpos = q0 + lax.broadcasted_iota(jnp.int32, (tq, 1), 0)

    def per_head(a):
        return jnp.concatenate([a] * NSA_GROUP, axis=0)

    s = lax.dot_general(qs, kc_ref[0, 0], _NT, preferred_element_type=f32)
    cblk = lax.broadcasted_iota(jnp.int32, (tq, 128), 1)
    cvalid = cblk * CMP_STRIDE + (CMP_BLOCK - 1) <= tpos
    s = s + per_head(jnp.where(cvalid, 0.0, MASKED))
    e = jnp.exp(s - jnp.max(s, axis=-1, keepdims=True)) * per_head(jnp.where(cvalid, 1.0, 0.0))
    l = jnp.sum(e, axis=-1, keepdims=True)
    p = e / jnp.where(l > 0.0, l, 1.0)
    o_cmp = jnp.dot(p.astype(bf16), vc_ref[0, 0], preferred_element_type=f32)

    imp = p[0:tq]
    for g in range(1, NSA_GROUP):
        imp = imp + p[g * tq:(g + 1) * tq]
    hi = imp.astype(bf16)
    r1 = imp - hi.astype(f32)
    mid = r1.astype(bf16)
    lo = (r1 - mid.astype(f32)).astype(bf16)
    msel = msel_ref[...]
    score = (lax.dot_general(msel, hi, _NT, preferred_element_type=f32)
             + lax.dot_general(msel, mid, _NT, preferred_element_type=f32)
             + lax.dot_general(msel, lo, _NT, preferred_element_type=f32))
    n_sel = score.shape[0]
    j = lax.broadcasted_iota(jnp.int32, (n_sel, tq), 0)
    tok = q0 + lax.broadcasted_iota(jnp.int32, (n_sel, tq), 1)
    cur = tok // SEL_BLOCK
    forced = (j == 0) | (j == cur) | (j == cur - 1)
    score = jnp.where(forced, jnp.inf, jnp.where(j * SEL_BLOCK <= tok, score, -jnp.inf))
    rank = jnp.zeros((n_sel, tq), f32)
    for jp in range(n_sel):
        row = score[jp:jp + 1, :]
        before = (row > score) | ((row == score) & (j > jp))
        rank = rank + jnp.where(before, 1.0, 0.0)
    chosen_t = jnp.where(rank < SEL_TOP, 1.0, 0.0)
    sel01 = jnp.concatenate([chosen_t, jnp.zeros((LANES - n_sel, tq), f32)], axis=0).T.astype(bf16)

    ck = NSA_CK
    rows = NSA_GROUP * tq

    def sel_chunk(c, carry):
        m, l, acc = carry
        k0 = pl.multiple_of(c * ck, ck)
        s = lax.dot_general(qs, ks_ref[0, 0, pl.ds(k0, ck), :], _NT, preferred_element_type=f32)
        chosen = jnp.dot(sel01, exp_ref[c], preferred_element_type=f32)
        kpos = k0 + lax.broadcasted_iota(jnp.int32, (tq, ck), 1)
        ok = (chosen > 0.5) & (kpos <= tpos)
        s = s + per_head(jnp.where(ok, 0.0, MASKED))
        m_new = jnp.maximum(m, jnp.max(s, axis=-1, keepdims=True))
        a = jnp.exp(m - m_new)
        pr = jnp.exp(s - m_new)
        l = a * l + jnp.sum(pr, axis=-1, keepdims=True)
        acc = a * acc + jnp.dot(pr.astype(bf16), vs_ref[0, 0, pl.ds(k0, ck), :], preferred_element_type=f32)
        return m_new, l, acc

    init = (jnp.full((rows, 1), MASKED, f32), jnp.zeros((rows, 1), f32), jnp.zeros((rows, NSA_HD), f32))
    n_chunks = (q0 + tq + ck - 1) // ck
    _, l_sel, acc_sel = lax.fori_loop(0, n_chunks, sel_chunk, init)
    o_sel = acc_sel / l_sel

    w0 = pl.multiple_of(jnp.maximum(q0 - WINDOW, 0), tq)
    wl = WINDOW + tq
    s = lax.dot_general(qs, kw_ref[0, 0, pl.ds(w0, wl), :], _NT, preferred_element_type=f32)
    diff = tpos - (w0 + lax.broadcasted_iota(jnp.int32, (tq, wl), 1))
    s = s + per_head(jnp.where((diff >= 0) & (diff < WINDOW), 0.0, MASKED))
    o_win = jnp.dot(_softmax_rows(s).astype(bf16), vw_ref[0, 0, pl.ds(w0, wl), :], preferred_element_type=f32)

    gates = jax.nn.sigmoid(zg_ref[0, 0])
    for g in range(NSA_GROUP):
        r = slice(g * tq, (g + 1) * tq)
        o_ref[0, :, g * NSA_HD:(g + 1) * NSA_HD] = (gates[:, 3 * g:3 * g + 1] * o_cmp[r]
                                                    + gates[:, 3 * g + 1:3 * g + 2] * o_sel[r]
                                                    + gates[:, 3 * g + 2:3 * g + 3] * o_win[r])


def nsa_prompt_attention(qs, kc, vc, k_slc, v_slc, k_win, v_win, zg):
    B, S = qs.shape[:2]
    bf16 = jnp.bfloat16
    assert S % NSA_CK == 0 and S % NSA_TQ == 0 and WINDOW % NSA_TQ == 0 and WINDOW + NSA_TQ <= S
    n_sel = S // SEL_BLOCK
    nb = kc.shape[1]
    assert nb <= 128

    def pad_blocks(a):
        return jnp.pad(a.transpose(0, 2, 1, 3).astype(bf16), ((0, 0), (0, 0), (0, 128 - nb), (0, 0)))

    zg4 = zg.reshape(B, S, NSA_KV_HEADS, 3 * NSA_GROUP).transpose(0, 2, 1, 3)
    c = np.arange(128)[:, None]
    jj = np.arange(n_sel)[None, :]
    ratio = SEL_BLOCK // CMP_STRIDE
    msel = ((c >= jj * ratio - (CMP_BLOCK // CMP_STRIDE - 1)) & (c <= jj * ratio + ratio - 1) & (c < nb))
    assert n_sel <= LANES and NSA_TQ == LANES
    expand = (np.arange(S)[None, :] // SEL_BLOCK == np.arange(LANES)[:, None])
    expand = expand.reshape(LANES, S // NSA_CK, NSA_CK).transpose(1, 0, 2)
    row_spec = pl.BlockSpec((1, 1, S, NSA_HD), lambda b, n, i: (b, n, 0, 0))
    blk_spec = pl.BlockSpec((1, 1, 128, NSA_HD), lambda b, n, i: (b, n, 0, 0))
    return pl.pallas_call(
        _nsa_prompt_kernel,
        out_shape=jax.ShapeDtypeStruct((B, S, NSA_QW), jnp.float32),
        grid=(B, NSA_KV_HEADS, S // NSA_TQ),
        in_specs=[pl.BlockSpec((1, NSA_TQ, NSA_GROUP * NSA_HD), lambda b, n, i: (b, i, n)),
                  blk_spec, blk_spec, row_spec, row_spec, row_spec, row_spec,
                  pl.BlockSpec((1, 1, NSA_TQ, 3 * NSA_GROUP), lambda b, n, i: (b, n, i, 0)),
                  pl.BlockSpec((n_sel, 128), lambda b, n, i: (0, 0)),
                  pl.BlockSpec((S // NSA_CK, LANES, NSA_CK), lambda b, n, i: (0, 0, 0))],
        out_specs=pl.BlockSpec((1, NSA_TQ, NSA_GROUP * NSA_HD), lambda b, n, i: (b, i, n)),
        compiler_params=pltpu.CompilerParams(dimension_semantics=("arbitrary", "arbitrary", "arbitrary"),
                                             vmem_limit_bytes=VMEM_LIMIT_BYTES),
        name="nsa_prompt_attention",
    )(qs, pad_blocks(kc), pad_blocks(vc), k_slc, v_slc, k_win, v_win,
      zg4, jnp.asarray(msel.T, bf16), jnp.asarray(expand, bf16))


MLSTM_L = 128
CONV_HALO = 8


def _log_sigmoid(x):
    return -(jnp.maximum(-x, 0.0) + jnp.log1p(jnp.exp(-jnp.abs(x))))


def _mlstm_prompt_kernel(x_ref, xprev_ref, halo0_ref, v_ref, o_ref, gcol_ref, grow_ref, cw_ref, cb_ref,
                         out_ref, c_out, n_out, m_out, c_ref, n_ref, m_ref):
    f32, bf16 = jnp.float32, jnp.bfloat16
    c = pl.program_id(1)
    L = MLSTM_L

    @pl.when(c == 0)
    def _():
        c_ref[...] = jnp.zeros_like(c_ref)
        n_ref[...] = jnp.zeros_like(n_ref)
        m_ref[...] = jnp.zeros_like(m_ref)

    x = x_ref[0]
    halo = jnp.where(c == 0, halo0_ref[0], xprev_ref[0, L - CONV_HALO:L, :])
    ext = jnp.concatenate([halo, x], axis=0)
    conv = cb_ref[...]
    for j in range(CONV_W):
        o = CONV_HALO - (CONV_W - 1) + j
        conv = conv + ext[o:o + L] * cw_ref[j:j + 1, :]
    qk = conv * jax.nn.sigmoid(conv)

    t_id = lax.broadcasted_iota(jnp.int32, (L, L), 0)
    s_id = lax.broadcasted_iota(jnp.int32, (L, L), 1)
    causal = t_id >= s_id
    gcol = gcol_ref[0, 0]
    grow = grow_ref[0, 0]
    for h in range(M_HEADS):
        hd = slice(h * M_HD, (h + 1) * M_HD)
        q = qk[:, hd]
        k = qk[:, M_W + h * M_HD:M_W + (h + 1) * M_HD] * (M_HD ** -0.5)
        v = v_ref[0, :, hd].astype(bf16)
        ig_r = grow[h:h + 1, :]
        ig_c = gcol[:, h:h + 1]
        lf_r = _log_sigmoid(grow[M_HEADS + h:M_HEADS + h + 1, :])
        lf_c = _log_sigmoid(gcol[:, M_HEADS + h:M_HEADS + h + 1])
        b_c = jnp.sum(jnp.where(causal, lf_r, 0.0), axis=1, keepdims=True)
        b_r = jnp.sum(jnp.where(t_id <= s_id, lf_c, 0.0), axis=0, keepdims=True)
        m_prev = m_ref[h]
        dmat = jnp.where(causal, b_c - b_r + ig_r, -jnp.inf)
        inter = b_c + m_prev
        m_t = jnp.maximum(inter, jnp.max(dmat, axis=1, keepdims=True))
        w_intra = jnp.exp(dmat - m_t)
        w_inter = jnp.exp(inter - m_t)
        qb = q.astype(bf16)
        s = lax.dot_general(qb, k.astype(bf16), _NT, preferred_element_type=f32) * w_intra
        num = (jnp.dot(s.astype(bf16), v, preferred_element_type=f32)
               + w_inter * jnp.dot(qb, c_ref[h].astype(bf16), preferred_element_type=f32))
        den = jnp.sum(s, axis=1, keepdims=True) + w_inter * jnp.sum(q * n_ref[h], axis=1, keepdims=True)
        hh = num / jnp.maximum(jnp.abs(den), jnp.exp(-m_t))
        out_ref[0, :, hd] = jax.nn.sigmoid(o_ref[0, :, hd]) * hh
        m_new = m_t[L - 1:L]
        b_last = b_c[L - 1:L]
        w_s = jnp.exp(b_last - b_c + ig_c - m_new)
        w_p = jnp.exp(b_last + m_prev - m_new)
        kw = k * w_s
        c_ref[h] = w_p * c_ref[h] + jnp.dot(kw.T.astype(bf16), v, preferred_element_type=f32)
        n_ref[h] = w_p * n_ref[h] + jnp.sum(kw, axis=0, keepdims=True)
        m_ref[h] = m_new

    @pl.when(c == pl.num_programs(1) - 1)
    def _():
        c_out[0] = c_ref[...]
        n_out[0] = n_ref[...]
        m_out[0] = m_ref[...]


def mlstm_prompt(zqk, zv, zo, zif, conv_w, conv_b, b_if):
    B, T, _ = zqk.shape
    L = MLSTM_L
    assert T % L == 0
    nc = T // L
    f32 = jnp.float32
    gif = zif + b_if
    gcol = gif.reshape(B, nc, L, 2 * M_HEADS)
    grow = gcol.transpose(0, 1, 3, 2)
    halo0 = jnp.zeros((B, CONV_HALO, 2 * M_W), f32)
    out, C, n, m = pl.pallas_call(
        _mlstm_prompt_kernel,
        out_shape=(jax.ShapeDtypeStruct((B, T, M_W), f32),
                   jax.ShapeDtypeStruct((B, M_HEADS, M_HD, M_HD), f32),
                   jax.ShapeDtypeStruct((B, M_HEADS, 1, M_HD), f32),
                   jax.ShapeDtypeStruct((B, M_HEADS, 1, 1), f32)),
        grid=(B, nc),
        in_specs=[pl.BlockSpec((1, L, 2 * M_W), lambda b, c: (b, c, 0)),
                  pl.BlockSpec((1, L, 2 * M_W), lambda b, c: (b, jnp.maximum(c - 1, 0), 0)),
                  pl.BlockSpec((1, CONV_HALO, 2 * M_W), lambda b, c: (b, 0, 0)),
                  pl.BlockSpec((1, L, M_W), lambda b, c: (b, c, 0)),
                  pl.BlockSpec((1, L, M_W), lambda b, c: (b, c, 0)),
                  pl.BlockSpec((1, 1, L, 2 * M_HEADS), lambda b, c: (b, c, 0, 0)),
                  pl.BlockSpec((1, 1, 2 * M_HEADS, L), lambda b, c: (b, c, 0, 0)),
                  pl.BlockSpec((CONV_W, 2 * M_W), lambda b, c: (0, 0)),
                  pl.BlockSpec((1, 2 * M_W), lambda b, c: (0, 0))],
        out_specs=(pl.BlockSpec((1, L, M_W), lambda b, c: (b, c, 0)),
                   pl.BlockSpec((1, M_HEADS, M_HD, M_HD), lambda b, c: (b, 0, 0, 0)),
                   pl.BlockSpec((1, M_HEADS, 1, M_HD), lambda b, c: (b, 0, 0, 0)),
                   pl.BlockSpec((1, M_HEADS, 1, 1), lambda b, c: (b, 0, 0, 0))),
        scratch_shapes=[pltpu.VMEM((M_HEADS, M_HD, M_HD), f32), pltpu.VMEM((M_HEADS, 1, M_HD), f32),
                        pltpu.VMEM((M_HEADS, 1, 1), f32)],
        compiler_params=pltpu.CompilerParams(dimension_semantics=("arbitrary", "arbitrary"),
                                             vmem_limit_bytes=VMEM_LIMIT_BYTES),
        name="mlstm_prompt",
    )(zqk, zqk, halo0, zv, zo, gcol, grow, conv_w, conv_b[None])
    return out, C, n.reshape(B, M_HEADS, M_HD), m.reshape(B, M_HEADS)


def mlstm_chunk(carry, inp):
    C, n, m = carry
    q, k, v, ig, lf = inp
    L = q.shape[2]
    b = jnp.cumsum(lf, axis=-1)
    causal = jnp.tril(jnp.ones((L, L), dtype=bool))
    dmat = jnp.where(causal, b[..., :, None] - b[..., None, :] + ig[..., None, :], -jnp.inf)
    inter = b + m[..., None]
    m_t = jnp.maximum(inter, dmat.max(axis=-1))
    w_intra = jnp.exp(dmat - m_t[..., None])
    w_inter = jnp.exp(inter - m_t)
    s = jnp.einsum('bhtd,bhsd->bhts', q, k) * w_intra
    num = jnp.einsum('bhts,bhsv->bhtv', s, v) + w_inter[..., None] * jnp.einsum('bhtd,bhdv->bhtv', q, C)
    den = s.sum(-1) + w_inter * jnp.einsum('bhtd,bhd->bht', q, n)
    h = num / jnp.maximum(jnp.abs(den), jnp.exp(-m_t))[..., None]
    m_new = m_t[..., -1]
    w_s = jnp.exp(b[..., -1:] - b + ig - m_new[..., None])
    w_p = jnp.exp(b[..., -1] + m - m_new)
    C_new = w_p[..., None, None] * C + jnp.einsum('bhs,bhsd,bhsv->bhdv', w_s, k, v)
    n_new = w_p[..., None] * n + jnp.einsum('bhs,bhsd->bhd', w_s, k)
    return (C_new, n_new, m_new), h


def mlstm_mix(zqk, zv, zo, zif, buf0, C0, n0, m0, conv_w, conv_b, b_if, chunk):
    B, T, _ = zqk.shape
    full = jnp.concatenate([buf0, zqk], axis=1)
    conv = conv_b
    for j in range(CONV_W):
        conv = conv + full[:, j:j + T] * conv_w[j]
    qk = jax.nn.silu(conv)

    def heads(a):
        return a.reshape(B, T, M_HEADS, M_HD).transpose(0, 2, 1, 3)

    q = heads(qk[..., :M_W])
    k = heads(qk[..., M_W:]) * (M_HD ** -0.5)
    v = heads(zv)
    gif = zif + b_if
    ig = gif[..., :M_HEADS].transpose(0, 2, 1)
    lf = jax.nn.log_sigmoid(gif[..., M_HEADS:]).transpose(0, 2, 1)
    nc = T // chunk

    def to_chunks(a):
        return jnp.moveaxis(a.reshape(B, M_HEADS, nc, chunk, *a.shape[3:]), 2, 0)

    (C, n, m), h = lax.scan(mlstm_chunk, (C0, n0, m0),
                            (to_chunks(q), to_chunks(k), to_chunks(v), to_chunks(ig), to_chunks(lf)))
    h = jnp.moveaxis(h, 0, 2).reshape(B, M_HEADS, T, M_HD).transpose(0, 2, 1, 3).reshape(B, T, M_W)
    out = jax.nn.sigmoid(zo) * h
    return out, (C, n, m, full[:, T:])


PEER_COMBOS = 2 * PEER_HEADS
PEER_KEY_ROWS = 8
PEER_TILE = PEER_KEY_ROWS * N_KEYS
PEER_TS_ROWS = 24
LANES = 128
_NT = (((1,), (1,)), ((), ()))


def _peer_topk_kernel(q_ref, keys_ref, s_ref, e0_ref, e1_ref, tau_ref, ts_ref):
    c = pl.program_id(1)
    tt = q_ref.shape[0]
    s = lax.dot_general(keys_ref[0], q_ref[...].astype(jnp.bfloat16), _NT,
                        preferred_element_type=jnp.float32)
    s_ref[c] = s
    key_id = lax.broadcasted_iota(jnp.int32, s.shape, 0)
    work = s
    rows = []
    for _ in range(PEER_TOPK + 1):
        m = jnp.max(work, axis=0, keepdims=True)
        first = jnp.min(jnp.where(work == m, key_id, N_KEYS), axis=0, keepdims=True)
        work = jnp.where(key_id == first, -jnp.inf, work)
        rows.append(m)
    rows.append(jnp.full((PEER_TS_ROWS - PEER_TOPK - 1, tt), -jnp.inf, jnp.float32))
    ts_ref[c] = jnp.concatenate(rows, axis=0)

    @pl.when(c == PEER_COMBOS - 1)
    def _():
        for h in range(PEER_HEADS):
            t0 = ts_ref[2 * h]
            t1 = ts_ref[2 * h + 1]
            pieces = [t0[0:1] + t1] + [t0[a:a + 1] + t1[0:8] for a in range(1, 8)] + [t0[8:24] + t1[0:1]]
            cand = jnp.concatenate(pieces, axis=0)
            top = t0[0:1] + t1[0:1]
            v16 = top
            v17 = top
            z = jnp.zeros_like(top)
            seen = jnp.zeros_like(top)
            for _ in range(PEER_TOPK + 1):
                m = jnp.max(cand, axis=0, keepdims=True)
                eq = cand == m
                cnt = jnp.sum(jnp.where(eq, 1.0, 0.0), axis=0, keepdims=True)
                active = seen < PEER_TOPK
                take = jnp.minimum(cnt, PEER_TOPK - seen)
                v16 = jnp.where(active, m, v16)
                v17 = jnp.where(seen < PEER_TOPK + 1, m, v17)
                z = z + jnp.where(active, take * jnp.exp(m - top), 0.0)
                seen = seen + cnt
                cand = jnp.where(eq, -jnp.inf, cand)
            tau_ref[h:h + 1, :] = 0.5 * v16 + 0.5 * v17
            e0_ref[h] = jnp.exp(s_ref[2 * h] - t0[0:1]) / z
            e1_ref[h] = jnp.exp(s_ref[2 * h + 1] - t1[0:1])


def peer_scores(q, sub_keys, tt):
    n = q.shape[0]
    assert n % tt == 0
    keys = sub_keys.reshape(PEER_COMBOS, N_KEYS, PEER_QDIM // 2).astype(jnp.bfloat16)
    f32 = jnp.float32
    return pl.pallas_call(
        _peer_topk_kernel,
        out_shape=(jax.ShapeDtypeStruct((PEER_COMBOS, N_KEYS, n), f32),
                   jax.ShapeDtypeStruct((PEER_HEADS, N_KEYS, n), f32),
                   jax.ShapeDtypeStruct((PEER_HEADS, N_KEYS, n), f32),
                   jax.ShapeDtypeStruct((PEER_HEADS, n), f32)),
        grid=(n // tt, PEER_COMBOS),
        in_specs=[pl.BlockSpec((tt, PEER_QDIM // 2), lambda i, c: (i, c)),
                  pl.BlockSpec((1, N_KEYS, PEER_QDIM // 2), lambda i, c: (c, 0, 0))],
        out_specs=(pl.BlockSpec((PEER_COMBOS, N_KEYS, tt), lambda i, c: (0, 0, i)),
                   pl.BlockSpec((PEER_HEADS, N_KEYS, tt), lambda i, c: (0, 0, i)),
                   pl.BlockSpec((PEER_HEADS, N_KEYS, tt), lambda i, c: (0, 0, i)),
                   pl.BlockSpec((PEER_HEADS, tt), lambda i, c: (0, i))),
        scratch_shapes=[pltpu.VMEM((PEER_COMBOS, PEER_TS_ROWS, tt), f32)],
        compiler_params=pltpu.CompilerParams(dimension_semantics=("arbitrary", "arbitrary"),
                                             vmem_limit_bytes=VMEM_LIMIT_BYTES),
        name="peer_topk",
    )(q, keys)


def _peer_dense_kernel(xb_ref, h_ref, u_ref, vt_ref, s0_ref, ez_ref, s_ref, e1_ref, tau_ref, g_ref, b_ref,
                       o_ref, acc_ref, a_ref, w_ref):
    e = pl.program_id(1)
    tt = xb_ref.shape[0]

    @pl.when(e == 0)
    def _():
        acc_ref[...] = jnp.zeros_like(acc_ref)

    a_ref[...] = lax.dot_general(u_ref[...], xb_ref[...], _NT, preferred_element_type=jnp.float32)
    for r in range(PEER_KEY_ROWS):
        rows = slice(r * N_KEYS, (r + 1) * N_KEYS)
        for t in range(tt // LANES):
            tok = slice(t * LANES, (t + 1) * LANES)
            gate = jnp.zeros((N_KEYS, LANES), jnp.float32)
            for h in range(PEER_HEADS):
                need = tau_ref[h:h + 1, tok] - s0_ref[2 * h, r:r + 1, tok]
                picked = jnp.where(s_ref[2 * h + 1, :, tok] >= need, e1_ref[h, :, tok], 0.0)
                gate = gate + picked * ez_ref[h, r:r + 1, tok]
            ar = a_ref[rows, tok]
            act = 0.5 * ar * (1.0 + lax.erf(ar * (2.0 ** -0.5)))
            w_ref[rows, tok] = (gate * act).astype(jnp.bfloat16)
    acc_ref[...] += jnp.dot(vt_ref[...], w_ref[...], preferred_element_type=jnp.float32)

    @pl.when(e == pl.num_programs(1) - 1)
    def _():
        r = ALPHA * h_ref[...] + acc_ref[...].T
        mu = jnp.mean(r, axis=-1, keepdims=True)
        d = r - mu
        var = jnp.mean(d * d, axis=-1, keepdims=True)
        o_ref[...] = d * lax.rsqrt(var + LN_EPS) * g_ref[...] + b_ref[...]


def peer_tail(h, hb, q, sub_keys, u_bf, vt_bf, ln_g, ln_b, tt):
    n, d = h.shape
    s, e0z, e1, tau = peer_scores(q, sub_keys, tt)
    n_exp = u_bf.shape[0]
    return pl.pallas_call(
        _peer_dense_kernel,
        out_shape=jax.ShapeDtypeStruct((n, d), jnp.float32),
        grid=(n // tt, n_exp // PEER_TILE),
        in_specs=[pl.BlockSpec((tt, d), lambda i, e: (i, 0)),
                  pl.BlockSpec((tt, d), lambda i, e: (i, 0)),
                  pl.BlockSpec((PEER_TILE, d), lambda i, e: (e, 0)),
                  pl.BlockSpec((d, PEER_TILE), lambda i, e: (0, e)),
                  pl.BlockSpec((PEER_COMBOS, PEER_KEY_ROWS, tt), lambda i, e: (0, e, i)),
                  pl.BlockSpec((PEER_HEADS, PEER_KEY_ROWS, tt), lambda i, e: (0, e, i)),
                  pl.BlockSpec((PEER_COMBOS, N_KEYS, tt), lambda i, e: (0, 0, i)),
                  pl.BlockSpec((PEER_HEADS, N_KEYS, tt), lambda i, e: (0, 0, i)),
                  pl.BlockSpec((PEER_HEADS, tt), lambda i, e: (0, i)),
                  pl.BlockSpec((1, d), lambda i, e: (0, 0)),
                  pl.BlockSpec((1, d), lambda i, e: (0, 0))],
        out_specs=pl.BlockSpec((tt, d), lambda i, e: (i, 0)),
        scratch_shapes=[pltpu.VMEM((d, tt), jnp.float32), pltpu.VMEM((PEER_TILE, tt), jnp.float32),
                        pltpu.VMEM((PEER_TILE, tt), jnp.bfloat16)],
        compiler_params=pltpu.CompilerParams(dimension_semantics=("arbitrary", "arbitrary"),
                                             vmem_limit_bytes=VMEM_LIMIT_BYTES),
        name="peer_dense",
    )(hb, h, u_bf, vt_bf, s, e0z, s, e1, tau, ln_g[None], ln_b[None])


def _out_proj_kernel(x_ref, nsa_ref, m_ref, wn_ref, wm_ref, g_ref, b_ref, wq_ref, h_ref, hb_ref, q_ref):
    f32, bf16 = jnp.float32, jnp.bfloat16
    r = (ALPHA * x_ref[...] + jnp.dot(nsa_ref[...].astype(bf16), wn_ref[...], preferred_element_type=f32)
         + jnp.dot(m_ref[...].astype(bf16), wm_ref[...], preferred_element_type=f32))
    mu = jnp.mean(r, axis=-1, keepdims=True)
    d = r - mu
    var = jnp.mean(d * d, axis=-1, keepdims=True)
    h = d * lax.rsqrt(var + LN_EPS) * g_ref[...] + b_ref[...]
    h_ref[...] = h
    hb = h.astype(bf16)
    hb_ref[...] = hb
    q_ref[...] = jnp.dot(hb, wq_ref[...], preferred_element_type=f32)


def out_proj_fused(x, o_nsa, o_m, w_out, ln_g, ln_b, w_pq, tm):
    n, d = x.shape
    assert n % tm == 0
    bf16 = jnp.bfloat16
    nq = w_pq.shape[1]

    def rows(width):
        return pl.BlockSpec((tm, width), lambda i: (i, 0))

    def whole(a):
        return pl.BlockSpec(a.shape, lambda i: (0, 0))

    wn = w_out[:NSA_QW].astype(bf16)
    wm = w_out[NSA_QW:].astype(bf16)
    wq = w_pq.astype(bf16)
    g, b = ln_g[None], ln_b[None]
    return pl.pallas_call(
        _out_proj_kernel,
        out_shape=(jax.ShapeDtypeStruct((n, d), jnp.float32), jax.ShapeDtypeStruct((n, d), bf16),
                   jax.ShapeDtypeStruct((n, nq), jnp.float32)),
        grid=(n // tm,),
        in_specs=[rows(d), rows(NSA_QW), rows(M_W), whole(wn), whole(wm), whole(g), whole(b), whole(wq)],
        out_specs=(rows(d), rows(d), rows(nq)),
        compiler_params=pltpu.CompilerParams(dimension_semantics=("arbitrary",),
                                             vmem_limit_bytes=VMEM_LIMIT_BYTES),
        name="out_proj",
    )(x, o_nsa, o_m, wn, wm, g, b, wq)


def block_tail(x, o_nsa, o_m, w_out, ln_g, ln_b, w_pq, sub_keys, u_bf, vt_bf, tt):
    lead = x.shape[:-1]
    h, hb, q = out_proj_fused(x.reshape(-1, D_MODEL), o_nsa.reshape(-1, NSA_QW), o_m.reshape(-1, M_W),
                              w_out, ln_g[0], ln_b[0], w_pq, tt)
    return peer_tail(h, hb, q, sub_keys, u_bf, vt_bf, ln_g[1], ln_b[1], tt).reshape(*lead, D_MODEL)


def prompt_mix(x, w_in, pe, w1, b1, w2, conv_w, conv_b, b_if):
    B, S, _ = x.shape
    z = in_proj_fused(x, w_in, jnp.arange(S), 512, True)
    kc = compress_chunks(z["k_cmp"], S // CMP_STRIDE, pe[0], w1[0], b1[0], w2[0])
    vc = compress_chunks(z["v_cmp"], S // CMP_STRIDE, pe[1], w1[1], b1[1], w2[1])
    zgate = z["zgate"].reshape(B, S, GATE_W)
    o_nsa = nsa_prompt_attention(z["q"].reshape(B, S, NSA_QW), kc, vc, z["k_slc_bf"], z["v_slc_bf"],
                                 z["k_win_bf"], z["v_win_bf"], zgate[..., :IN_SPLITS[2]])
    zqk = z["zqk"].reshape(B, S, 2 * M_W)
    o_m, C, n, m = mlstm_prompt(zqk, z["zv"].reshape(B, S, M_W), z["zo"].reshape(B, S, M_W),
                                zgate[..., IN_SPLITS[2]:], conv_w, conv_b, b_if)
    buf = zqk[:, S - (CONV_W - 1):]
    wl = min(WINDOW, S)
    k_cmp, v_cmp, k_slc, v_slc, k_win, v_win = [
        z[k].reshape(B, S, NSA_KV_HEADS, NSA_HD) for k in ("k_cmp", "v_cmp", "k_slc", "v_slc", "k_win", "v_win")]
    return (o_nsa, o_m), (k_cmp, v_cmp, k_slc, v_slc, k_win[:, S - wl:], v_win[:, S - wl:], C, n, m, buf)


def sample_mix(x, kc_pool, vc_pool, ks_pool, vs_pool, kw_buf, vw_buf, C0, n0, m0, buf0, page_table,
               w_in, pe, w1, b1, w2, conv_w, conv_b, b_if):
    B, T, _ = x.shape
    past = page_table.shape[1] * PAGE_SIZE
    pos = past + jnp.arange(T)
    z = in_proj_fused(x, w_in, pos, B * T, False)
    q = z["q"].astype(jnp.float32).reshape(B, T, NSA_HEADS, NSA_HD) * (1.0 / ATTN_SCALE)
    k_cmp, v_cmp, k_slc, v_slc, k_win, v_win = [
        z[k].reshape(B, T, NSA_KV_HEADS, NSA_HD) for k in ("k_cmp", "v_cmp", "k_slc", "v_slc", "k_win", "v_win")]
    zgate = z["zgate"].reshape(B, T, GATE_W)
    gates = jax.nn.sigmoid(zgate[..., :IN_SPLITS[2]]).reshape(B, T, NSA_HEADS, 3)
    zqk, zv, zo, zif = (z["zqk"].reshape(B, T, 2 * M_W), z["zv"].reshape(B, T, M_W), z["zo"].reshape(B, T, M_W),
                        zgate[..., IN_SPLITS[2]:])

    assert (past + T) // CMP_STRIDE == past // CMP_STRIDE

    def compressed(pool, c):
        pages = pool.transpose(0, 2, 3, 1)[page_table]
        return compress_pages(pages, pe[c], w1[c], b1[c], w2[c])

    o_cmp, p = cmp_attend(q, pos, compressed(kc_pool, 0), compressed(vc_pool, 1))
    n_sel = -(-(past + T) // SEL_BLOCK)
    member = select_blocks(p, pos, n_sel)
    o_sel = sample_selected_attention(q, pos, member, ks_pool, vs_pool, k_slc, v_slc, page_table)
    wb = kw_buf.shape[1]
    kw = jnp.concatenate([kw_buf, k_win], axis=1)
    vw = jnp.concatenate([vw_buf, v_win], axis=1)
    kpos = past - wb + jnp.arange(wb + T)
    o_win = win_attend(q, pos, kw, vw, kpos)
    o_nsa = nsa_combine(gates, o_cmp, o_sel, o_win)
    o_m, (C, n, m, buf) = mlstm_mix(zqk, zv, zo, zif, buf0, C0, n0, m0, conv_w, conv_b, b_if, T)
    return (o_nsa, o_m), (k_cmp, v_cmp, k_slc, v_slc, kw[:, T:], vw[:, T:], C, n, m, buf)


def kernel(x_prompt, x_sample, cache_k_cmp, cache_v_cmp, cache_k_slc, cache_v_slc, cache_k_win, cache_v_win,
           state_C, state_n, state_m, state_conv, page_table, w_in, w_out, w_phi1, b_phi1, w_phi2, pe_cmp,
           conv_w, conv_b, b_if, ln_g, ln_b, w_pq, sub_keys, u_tab, v_tab):
    l = 0
    mix_p, st_p = prompt_mix(x_prompt, w_in[l], pe_cmp[l], w_phi1[l], b_phi1[l], w_phi2[l],
                             conv_w[l], conv_b[l], b_if[l])
    mix_s, st_s = sample_mix(x_sample, cache_k_cmp[l], cache_v_cmp[l], cache_k_slc[l], cache_v_slc[l],
                             cache_k_win[l], cache_v_win[l], state_C[l], state_n[l], state_m[l],
                             state_conv[l], page_table, w_in[l], pe_cmp[l], w_phi1[l], b_phi1[l],
                             w_phi2[l], conv_w[l], conv_b[l], b_if[l])
    u_bf = u_tab[l].astype(jnp.bfloat16)
    vt_bf = v_tab[l].astype(jnp.bfloat16).T
    xp = block_tail(x_prompt, *mix_p, w_out[l], ln_g[l], ln_b[l], w_pq[l], sub_keys[l], u_bf, vt_bf, 512)
    xs = block_tail(x_sample, *mix_s, w_out[l], ln_g[l], ln_b[l], w_pq[l], sub_keys[l], u_bf, vt_bf, 128)
    return (xp, xs) + tuple(a[None] for a in st_p) + tuple(a[None] for a in st_s)
```

```python
import functools

import jax
import jax.numpy as jnp
import numpy as np
from jax import lax
from jax.experimental import pallas as pl
from jax.experimental.pallas import tpu as pltpu

D_MODEL = 1024
DEPTH = 1
PAGE_SIZE = 128
NSA_HEADS = 8
NSA_KV_HEADS = 2
NSA_GROUP = NSA_HEADS // NSA_KV_HEADS
NSA_HD = 64
NSA_QW = NSA_HEADS * NSA_HD
NSA_KVW = NSA_KV_HEADS * NSA_HD
CMP_BLOCK = 32
CMP_STRIDE = 16
SEL_BLOCK = 64
SEL_TOP = 16
WINDOW = 512
Q_BLOCK = 64
ATTN_SCALE = NSA_HD ** -0.5
ROPE_THETA = 10000.0
M_HEADS = 4
M_HD = 128
M_W = M_HEADS * M_HD
M_CHUNK = 64
CONV_W = 4
PEER_HEADS = 8
N_KEYS = 128
PEER_TOPK = 16
PEER_QDIM = 256
PEER_BLOCK = 128
IN_SPLITS = (NSA_QW, 6 * NSA_KVW, 3 * NSA_HEADS, 2 * M_W, M_W, M_W, 2 * M_HEADS)
LN_EPS = 1e-5
ALPHA = (2 * DEPTH) ** 0.25

VMEM_LIMIT_BYTES = 56 * 1024 * 1024


def _mm_kernel(x_ref, w_ref, o_ref):
    o_ref[...] = jnp.dot(x_ref[...].astype(jnp.bfloat16), w_ref[...], preferred_element_type=jnp.float32)


def pallas_matmul(x, w, tm=512):
    M, K = x.shape
    N = w.shape[1]
    tm = min(tm, M)
    assert M % tm == 0
    return pl.pallas_call(
        _mm_kernel,
        out_shape=jax.ShapeDtypeStruct((M, N), jnp.float32),
        grid=(M // tm,),
        in_specs=[pl.BlockSpec((tm, K), lambda i: (i, 0)), pl.BlockSpec((K, N), lambda i: (0, 0))],
        out_specs=pl.BlockSpec((tm, N), lambda i: (i, 0)),
        compiler_params=pltpu.CompilerParams(dimension_semantics=("arbitrary",),
                                             vmem_limit_bytes=VMEM_LIMIT_BYTES),
        name="proj_matmul",
    )(x, w.astype(jnp.bfloat16))


def mm3(x, w):
    lead = x.shape[:-1]
    return pallas_matmul(x.reshape(-1, x.shape[-1]), w).reshape(*lead, w.shape[1])


def layer_norm(x, g, b):
    mu = x.mean(-1, keepdims=True)
    var = jnp.square(x - mu).mean(-1, keepdims=True)
    return (x - mu) * lax.rsqrt(var + LN_EPS) * g + b


def rope(x, pos):
    half = x.shape[-1] // 2
    inv = ROPE_THETA ** (-jnp.arange(half, dtype=jnp.float32) / half)
    ang = pos.astype(jnp.float32)[:, None] * inv[None, :]
    cos = jnp.cos(ang)[:, None, :]
    sin = jnp.sin(ang)[:, None, :]
    x1, x2 = x[..., :half], x[..., half:]
    return jnp.concatenate([x1 * cos - x2 * sin, x2 * cos + x1 * sin], axis=-1)


def split_in_proj(x, w_in):
    z = mm3(x, w_in)
    cuts = [int(c) for c in np.cumsum(IN_SPLITS)[:-1]]
    return jnp.split(z, cuts, axis=-1)


_IN_OFF = np.concatenate([[0], np.cumsum(IN_SPLITS)])
_IN_ORDER = (0, 1, 3, 4, 5, 2, 6)
_N_KV_ROWS = 6
_KV_BF16 = (2, 3, 4, 5)
GATE_W = IN_SPLITS[2] + IN_SPLITS[6]


def _rope_pairs(x, cos, sin_signed):
    half = NSA_HD // 2
    lane = lax.broadcasted_iota(jnp.int32, x.shape, 1)
    partner = jnp.where(lane % NSA_HD < half, pltpu.roll(x, LANES - half, 1), pltpu.roll(x, half, 1))
    return x * cos + partner * sin_signed


def _in_proj_kernel(x_ref, w_ref, cos_ref, sin_ref, q_ref, *rest, kv_major):
    kv_refs = rest[:_N_KV_ROWS]
    rest = rest[_N_KV_ROWS:]
    if kv_major:
        bf_refs, rest = rest[:len(_KV_BF16)], rest[len(_KV_BF16):]
    zqk_ref, zv_ref, zo_ref, zgate_ref = rest
    z = jnp.dot(x_ref[...].astype(jnp.bfloat16), w_ref[...], preferred_element_type=jnp.float32)
    cos = cos_ref[...]
    sin = sin_ref[...]
    for g in range(NSA_QW // LANES):
        sl = slice(g * LANES, (g + 1) * LANES)
        q_ref[:, sl] = (_rope_pairs(z[:, sl], cos, sin) * ATTN_SCALE).astype(jnp.bfloat16)
    for r in range(_N_KV_ROWS):
        row = z[:, NSA_QW + r * NSA_KVW:NSA_QW + (r + 1) * NSA_KVW]
        if r % 2 == 0:
            row = _rope_pairs(row, cos, sin)
        kv_refs[r][...] = row
        if kv_major and r in _KV_BF16:
            dst = bf_refs[_KV_BF16.index(r)]
            for n in range(NSA_KV_HEADS):
                dst[0, n] = row[:, n * NSA_HD:(n + 1) * NSA_HD].astype(jnp.bfloat16)
    o = NSA_QW + _N_KV_ROWS * NSA_KVW
    zqk_ref[...] = z[:, o:o + 2 * M_W]
    zv_ref[...] = z[:, o + 2 * M_W:o + 3 * M_W]
    zo_ref[...] = z[:, o + 3 * M_W:o + 4 * M_W]
    zgate_ref[...] = z[:, o + 4 * M_W:o + 4 * M_W + GATE_W]


def in_proj_fused(x, w_in, pos, tm, kv_major):
    B, T, D = x.shape
    M = B * T
    assert M % tm == 0 and NSA_KVW == LANES and (not kv_major or T % tm == 0)
    f32, bf16 = jnp.float32, jnp.bfloat16
    w = jnp.concatenate([w_in[:, _IN_OFF[i]:_IN_OFF[i + 1]] for i in _IN_ORDER], axis=1).astype(bf16)
    half = NSA_HD // 2
    inv = ROPE_THETA ** (-jnp.arange(half, dtype=f32) / half)
    ang = pos.astype(f32)[:, None] * inv[None, :]
    cos = jnp.tile(jnp.cos(ang), (B, 2 * LANES // NSA_HD))
    sin = jnp.tile(jnp.concatenate([-jnp.sin(ang), jnp.sin(ang)], axis=1), (B, LANES // NSA_HD))
    n_w = w.shape[1]

    def rows(width):
        return pl.BlockSpec((tm, width), lambda i: (i, 0))

    out_shape = [jax.ShapeDtypeStruct((M, NSA_QW), bf16)] + [jax.ShapeDtypeStruct((M, NSA_KVW), f32)] * _N_KV_ROWS
    out_specs = [rows(NSA_QW)] + [rows(NSA_KVW)] * _N_KV_ROWS
    if kv_major:
        per_seq = T // tm
        out_shape += [jax.ShapeDtypeStruct((B, NSA_KV_HEADS, T, NSA_HD), bf16)] * len(_KV_BF16)
        out_specs += [pl.BlockSpec((1, NSA_KV_HEADS, tm, NSA_HD),
                                   lambda i: (i // per_seq, 0, i % per_seq, 0))] * len(_KV_BF16)
    out_shape += [jax.ShapeDtypeStruct((M, 2 * M_W), f32), jax.ShapeDtypeStruct((M, M_W), f32),
                  jax.ShapeDtypeStruct((M, M_W), f32), jax.ShapeDtypeStruct((M, GATE_W), f32)]
    out_specs += [rows(2 * M_W), rows(M_W), rows(M_W), rows(GATE_W)]
    outs = pl.pallas_call(
        functools.partial(_in_proj_kernel, kv_major=kv_major),
        out_shape=tuple(out_shape),
        grid=(M // tm,),
        in_specs=[rows(D), pl.BlockSpec((D, n_w), lambda i: (0, 0)), rows(LANES), rows(LANES)],
        out_specs=tuple(out_specs),
        compiler_params=pltpu.CompilerParams(dimension_semantics=("arbitrary",),
                                             vmem_limit_bytes=VMEM_LIMIT_BYTES),
        name="in_proj",
    )(x.reshape(M, D), w, cos, sin)
    names = ["q", "k_cmp", "v_cmp", "k_slc", "v_slc", "k_win", "v_win"]
    if kv_major:
        names += ["k_slc_bf", "v_slc_bf", "k_win_bf", "v_win_bf"]
    names += ["zqk", "zv", "zo", "zgate"]
    return dict(zip(names, outs))


def nsa_project(zq, zkv, zg, pos):
    B, T, _ = zq.shape
    q = rope(zq.reshape(B, T, NSA_HEADS, NSA_HD), pos)
    kv = zkv.reshape(B, T, 6, NSA_KV_HEADS, NSA_HD)
    rows = (rope(kv[:, :, 0], pos), kv[:, :, 1], rope(kv[:, :, 2], pos), kv[:, :, 3],
            rope(kv[:, :, 4], pos), kv[:, :, 5])
    gates = jax.nn.sigmoid(zg).reshape(B, T, NSA_HEADS, 3)
    return q, rows, gates


def _expanded_w1(w1):
    assert CMP_BLOCK == 2 * CMP_STRIDE
    w1r = w1.reshape(2, CMP_STRIDE, NSA_HD, w1.shape[-1])
    wbig = jnp.einsum('hpdf,kn->pkdnhf', w1r, jnp.eye(NSA_KV_HEADS, dtype=w1.dtype))
    return wbig.reshape(CMP_STRIDE * NSA_KVW, 2 * NSA_KV_HEADS * w1.shape[-1])


def _compress_rows(x, w_ref, bias_ref, w2_ref, o_ref):
    f32, bf16 = jnp.float32, jnp.bfloat16
    rows = x.shape[0]
    f = w2_ref.shape[0]
    proj = jnp.dot(x.astype(bf16), w_ref[...], preferred_element_type=f32)
    for n in range(NSA_KV_HEADS):
        first = proj[:, 2 * n * f:(2 * n + 1) * f]
        second = pltpu.roll(proj[:, (2 * n + 1) * f:(2 * n + 2) * f], rows - 1, 0)
        pre = first + second + bias_ref[...]
        hid = 0.5 * pre * (1.0 + lax.erf(pre * (2.0 ** -0.5)))
        o_ref[:, n * NSA_HD:(n + 1) * NSA_HD] = jnp.dot(hid.astype(bf16), w2_ref[...], preferred_element_type=f32)


def _compress_chunks_kernel(x_ref, w_ref, bias_ref, w2_ref, o_ref):
    _compress_rows(x_ref[...], w_ref, bias_ref, w2_ref, o_ref)


def _compress_weights(pe, w1, b1, w2):
    bf16 = jnp.bfloat16
    bias = jnp.dot(pe.reshape(-1), w1, precision=lax.Precision.HIGHEST) + b1
    return _expanded_w1(w1).astype(bf16), bias[None], w2.astype(bf16)


def compress_chunks(rows, per_seq, pe, w1, b1, w2, tm=512):
    chunks = rows.reshape(-1, CMP_STRIDE * NSA_KVW)
    n = chunks.shape[0]
    tm = min(tm, n)
    assert n % tm == 0 and tm % per_seq == 0
    wbig, bias, w2b = _compress_weights(pe, w1, b1, w2)

    def whole(a):
        return pl.BlockSpec(a.shape, lambda i: (0, 0))

    out = pl.pallas_call(
        _compress_chunks_kernel,
        out_shape=jax.ShapeDtypeStruct((n, NSA_KVW), jnp.float32),
        grid=(n // tm,),
        in_specs=[pl.BlockSpec((tm, chunks.shape[1]), lambda i: (i, 0)), whole(wbig), whole(bias), whole(w2b)],
        out_specs=pl.BlockSpec((tm, NSA_KVW), lambda i: (i, 0)),
        compiler_params=pltpu.CompilerParams(dimension_semantics=("arbitrary",),
                                             vmem_limit_bytes=VMEM_LIMIT_BYTES),
        name="compress_chunks",
    )(chunks, wbig, bias, w2b)
    return out.reshape(n // per_seq, per_seq, NSA_KV_HEADS, NSA_HD)[:, :-1]


PAGE_GROUP = 4


def _compress_pages_kernel(pg_ref, w_ref, bias_ref, w2_ref, o_ref, x_ref, t_ref):
    n_pages = pg_ref.shape[1]
    per_page = PAGE_SIZE // CMP_STRIDE
    group = PAGE_GROUP

    def place(i, carry):
        for u in range(group):
            g = i * group + u
            t_ref[u] = pg_ref[0, g].reshape(NSA_KVW, PAGE_SIZE).T
            row0 = pl.multiple_of(g * per_page, per_page)
            for p in range(CMP_STRIDE):
                x_ref[pl.ds(row0, per_page), p * NSA_KVW:(p + 1) * NSA_KVW] = (
                    t_ref.at[u][pl.ds(p, per_page, stride=CMP_STRIDE), :])
        return carry

    lax.fori_loop(0, n_pages // group, place, 0)
    _compress_rows(x_ref[...], w_ref, bias_ref, w2_ref, o_ref.at[0])


def compress_pages(pages, pe, w1, b1, w2):
    B, n_pages = pages.shape[:2]
    assert pages.shape[2:] == (NSA_KV_HEADS, NSA_HD, PAGE_SIZE) and NSA_KVW == LANES and PAGE_SIZE == LANES
    assert n_pages % PAGE_GROUP == 0
    wbig, bias, w2b = _compress_weights(pe, w1, b1, w2)
    rows = n_pages * (PAGE_SIZE // CMP_STRIDE)

    def whole(a):
        return pl.BlockSpec(a.shape, lambda b: (0, 0))

    out = pl.pallas_call(
        _compress_pages_kernel,
        out_shape=jax.ShapeDtypeStruct((B, rows, NSA_KVW), jnp.float32),
        grid=(B,),
        in_specs=[pl.BlockSpec((1, n_pages, NSA_KV_HEADS, NSA_HD, PAGE_SIZE), lambda b: (b, 0, 0, 0, 0)),
                  whole(wbig), whole(bias), whole(w2b)],
        out_specs=pl.BlockSpec((1, rows, NSA_KVW), lambda b: (b, 0, 0)),
        scratch_shapes=[pltpu.VMEM((rows, wbig.shape[0]), jnp.float32),
                        pltpu.VMEM((PAGE_GROUP, PAGE_SIZE, NSA_KVW), jnp.float32)],
        compiler_params=pltpu.CompilerParams(dimension_semantics=("arbitrary",),
                                             vmem_limit_bytes=VMEM_LIMIT_BYTES),
        name="compress_pages",
    )(pages, wbig, bias, w2b)
    return out.reshape(B, rows, NSA_KV_HEADS, NSA_HD)[:, :-1]


def cmp_attend(q, qpos, kc, vc):
    B, T = q.shape[:2]
    qg = q.reshape(B, T, NSA_KV_HEADS, NSA_GROUP, NSA_HD)
    s = jnp.einsum('btngd,bcnd->btngc', qg, kc) * ATTN_SCALE
    nblk = kc.shape[1]
    blk_end = jnp.arange(nblk) * CMP_STRIDE + CMP_BLOCK - 1
    valid = (blk_end[None, :] <= qpos[:, None])[None, :, None, None, :]
    p = jax.nn.softmax(jnp.where(valid, s, -1e30), axis=-1) * valid
    o = jnp.einsum('btngc,bcnd->btngd', p, vc)
    return o.reshape(B, T, NSA_HEADS, NSA_HD), p


def select_blocks(p, qpos, n_sel):
    imp = p.sum(axis=3)
    R = SEL_BLOCK // CMP_STRIDE
    r = CMP_BLOCK // CMP_STRIDE
    nb = imp.shape[-1]
    right = n_sel * R + R - 1 - nb
    padded = jnp.pad(imp, ((0, 0), (0, 0), (0, 0), (r - 1, right)))
    score = padded[..., 0:(n_sel - 1) * R + 1:R]
    for o in range(1, R + r - 1):
        score = score + padded[..., o:o + (n_sel - 1) * R + 1:R]
    j = jnp.arange(n_sel)[None, :]
    cur = (qpos // SEL_BLOCK)[:, None]
    valid = (j * SEL_BLOCK <= qpos[:, None])[None, :, None, :]
    forced = ((j == 0) | (j == cur) | (j == cur - 1))[None, :, None, :]
    score = jnp.where(forced, jnp.inf, jnp.where(valid, score, -jnp.inf))
    idx = j[0]
    before = (score[..., None, :] > score[..., :, None]) | ((score[..., None, :] == score[..., :, None])
                                                          & (idx[None, :] < idx[:, None]))
    return before.sum(-1) < min(SEL_TOP, n_sel)


def sample_selected_attention(q, qpos, member, k_pool, v_pool, k_new, v_new, page_table):
    B, T = q.shape[:2]
    n_pages = page_table.shape[1]
    per_page = PAGE_SIZE // SEL_BLOCK
    assert member.shape[-1] == n_pages * per_page + 1 and T <= SEL_BLOCK
    kp = k_pool.transpose(0, 2, 3, 1)[page_table]
    vp = v_pool.transpose(0, 2, 3, 1)[page_table]
    qg = q.reshape(B, T, NSA_KV_HEADS, NSA_GROUP, NSA_HD)
    s_past = jnp.einsum('btngd,bpndk->bntgpk', qg, kp) * ATTN_SCALE
    s_new = jnp.einsum('btngd,bsnd->bntgs', qg, k_new) * ATTN_SCALE
    m = member.transpose(0, 2, 1, 3)
    m_past = jnp.repeat(m[..., :-1].reshape(B, NSA_KV_HEADS, T, n_pages, per_page), SEL_BLOCK, axis=-1)
    kpos = (jnp.arange(n_pages) * PAGE_SIZE)[:, None] + jnp.arange(PAGE_SIZE)[None, :]
    m_past = m_past & (kpos[None, None, None] <= qpos[None, None, :, None, None])
    new_pos = n_pages * PAGE_SIZE + jnp.arange(T)
    m_new = m[..., -1:] & (new_pos[None, None, None, :] <= qpos[None, None, :, None])
    logits = jnp.concatenate(
        [jnp.where(m_past[:, :, :, None], s_past, -jnp.inf).reshape(B, NSA_KV_HEADS, T, NSA_GROUP, -1),
         jnp.where(m_new[:, :, :, None], s_new, -jnp.inf)], axis=-1)
    pr = jax.nn.softmax(logits, axis=-1)
    pr_past = pr[..., :n_pages * PAGE_SIZE].reshape(B, NSA_KV_HEADS, T, NSA_GROUP, n_pages, PAGE_SIZE)
    o = (jnp.einsum('bntgpk,bpndk->bntgd', pr_past, vp)
         + jnp.einsum('bntgs,bsnd->bntgd', pr[..., n_pages * PAGE_SIZE:], v_new))
    return o.transpose(0, 2, 1, 3, 4).reshape(B, T, NSA_HEADS, NSA_HD)


def to_blocks(rows, n_sel):
    B, L, KV, hd = rows.shape
    rows = jnp.pad(rows, ((0, 0), (0, n_sel * SEL_BLOCK - L), (0, 0), (0, 0)))
    return rows.reshape(B, n_sel, SEL_BLOCK, KV, hd).transpose(0, 3, 1, 2, 4)


def take_rows(table, idx):
    return table[idx]


def sel_attend(q, qpos, sel, kb, vb):
    B, Tq = q.shape[:2]
    k = sel.shape[-1]
    sel_t = sel.transpose(0, 2, 1, 3)
    gather = jax.vmap(jax.vmap(take_rows))
    kg = gather(kb, sel_t).reshape(B, NSA_KV_HEADS, Tq, k * SEL_BLOCK, NSA_HD)
    vg = gather(vb, sel_t).reshape(B, NSA_KV_HEADS, Tq, k * SEL_BLOCK, NSA_HD)
    kpos = (sel_t[..., None] * SEL_BLOCK + jnp.arange(SEL_BLOCK)).reshape(B, NSA_KV_HEADS, Tq, k * SEL_BLOCK)
    qg = q.reshape(B, Tq, NSA_KV_HEADS, NSA_GROUP, NSA_HD).transpose(0, 2, 1, 3, 4)
    s = jnp.einsum('bntgd,bntsd->bntgs', qg, kg) * ATTN_SCALE
    mask = kpos[:, :, :, None, :] <= qpos[None, None, :, None, None]
    pr = jax.nn.softmax(jnp.where(mask, s, -jnp.inf), axis=-1)
    o = jnp.einsum('bntgs,bntsd->bntgd', pr, vg)
    return o.transpose(0, 2, 1, 3, 4).reshape(B, Tq, NSA_HEADS, NSA_HD)


def win_attend(q, qpos, k, v, kpos):
    B, Tq = q.shape[:2]
    qg = q.reshape(B, Tq, NSA_KV_HEADS, NSA_GROUP, NSA_HD)
    s = jnp.einsum('btngd,bsnd->btngs', qg, k) * ATTN_SCALE
    diff = qpos[:, None] - kpos[None, :]
    mask = ((diff >= 0) & (diff < WINDOW) & (kpos[None, :] >= 0))[None, :, None, None, :]
    pr = jax.nn.softmax(jnp.where(mask, s, -jnp.inf), axis=-1)
    o = jnp.einsum('btngs,bsnd->btngd', pr, v)
    return o.reshape(B, Tq, NSA_HEADS, NSA_HD)


def nsa_combine(gates, o_cmp, o_sel, o_win):
    B, T = gates.shape[:2]
    o = gates[..., 0:1] * o_cmp + gates[..., 1:2] * o_sel + gates[..., 2:3] * o_win
    return o.reshape(B, T, NSA_QW)


NSA_TQ = 128
NSA_CK = 512
MASKED = -1e30


def _softmax_rows(s):
    m = jnp.max(s, axis=-1, keepdims=True)
    e = jnp.exp(s - m)
    return e / jnp.sum(e, axis=-1, keepdims=True)


def _nsa_prompt_kernel(q_ref, kc_ref, vc_ref, ks_ref, vs_ref, kw_ref, vw_ref, zg_ref, msel_ref, exp_ref, o_ref):
    f32, bf16 = jnp.float32, jnp.bfloat16
    tq = NSA_TQ
    q0 = pl.program_id(2) * tq
    qb = q_ref[0]
    qs = jnp.concatenate([qb[:, g * NSA_HD:(g + 1) * NSA_HD] for g in range(NSA_GROUP)], axis=0)
    tpos = q0 + lax.broadcasted_iota(jnp.int32, (tq, 1), 0)

    def per_head(a):
        return jnp.concatenate([a] * NSA_GROUP, axis=0)

    s = lax.dot_general(qs, kc_ref[0, 0], _NT, preferred_element_type=f32)
    cblk = lax.broadcasted_iota(jnp.int32, (tq, 128), 1)
    cvalid = cblk * CMP_STRIDE + (CMP_BLOCK - 1) <= tpos
    s = s + per_head(jnp.where(cvalid, 0.0, MASKED))
    e = jnp.exp(s - jnp.max(s, axis=-1, keepdims=True)) * per_head(jnp.where(cvalid, 1.0, 0.0))
    l = jnp.sum(e, axis=-1, keepdims=True)
    p = e / jnp.where(l > 0.0, l, 1.0)
    o_cmp = jnp.dot(p.astype(bf16), vc_ref[0, 0], preferred_element_type=f32)

    imp = p[0:tq]
    for g in range(1, NSA_GROUP):
        imp = imp + p[g * tq:(g + 1) * tq]
    hi = imp.astype(bf16)
    r1 = imp - hi.astype(f32)
    mid = r1.astype(bf16)
    lo = (r1 - mid.astype(f32)).astype(bf16)
    msel = msel_ref[...]
    score = (lax.dot_general(msel, hi, _NT, preferred_element_type=f32)
             + lax.dot_general(msel, mid, _NT, preferred_element_type=f32)
             + lax.dot_general(msel, lo, _NT, preferred_element_type=f32))
    n_sel = score.shape[0]
    j = lax.broadcasted_iota(jnp.int32, (n_sel, tq), 0)
    tok = q0 + lax.broadcasted_iota(jnp.int32, (n_sel, tq), 1)
    cur = tok // SEL_BLOCK
    forced = (j == 0) | (j == cur) | (j == cur - 1)
    score = jnp.where(forced, jnp.inf, jnp.where(j * SEL_BLOCK <= tok, score, -jnp.inf))
    rank = jnp.zeros((n_sel, tq), f32)
    for jp in range(n_sel):
        row = score[jp:jp + 1, :]
        before = (row > score) | ((row == score) & (j > jp))
        rank = rank + jnp.where(before, 1.0, 0.0)
    chosen_t = jnp.where(rank < SEL_TOP, 1.0, 0.0)
    sel01 = jnp.concatenate([chosen_t, jnp.zeros((LANES - n_sel, tq), f32)], axis=0).T.astype(bf16)

    ck = NSA_CK
    rows = NSA_GROUP * tq

    def sel_chunk(c, carry):
        m, l, acc = carry
        k0 = pl.multiple_of(c * ck, ck)
        s = lax.dot_general(qs, ks_ref[0, 0, pl.ds(k0, ck), :], _NT, preferred_element_type=f32)
        chosen = jnp.dot(sel01, exp_ref[c], preferred_element_type=f32)
        kpos = k0 + lax.broadcasted_iota(jnp.int32, (tq, ck), 1)
        ok = (chosen > 0.5) & (kpos <= tpos)
        s = s + per_head(jnp.where(ok, 0.0, MASKED))
        m_new = jnp.maximum(m, jnp.max(s, axis=-1, keepdims=True))
        a = jnp.exp(m - m_new)
        pr = jnp.exp(s - m_new)
        l = a * l + jnp.sum(pr, axis=-1, keepdims=True)
        acc = a * acc + jnp.dot(pr.astype(bf16), vs_ref[0, 0, pl.ds(k0, ck), :], preferred_element_type=f32)
        return m_new, l, acc

    init = (jnp.full((rows, 1), MASKED, f32), jnp.zeros((rows, 1), f32), jnp.zeros((rows, NSA_HD), f32))
    n_chunks = (q0 + tq + ck - 1) // ck
    _, l_sel, acc_sel = lax.fori_loop(0, n_chunks, sel_chunk, init)
    o_sel = acc_sel / l_sel

    w0 = pl.multiple_of(jnp.maximum(q0 - WINDOW, 0), tq)
    wl = WINDOW + tq
    s = lax.dot_general(qs, kw_ref[0, 0, pl.ds(w0, wl), :], _NT, preferred_element_type=f32)
    diff = tpos - (w0 + lax.broadcasted_iota(jnp.int32, (tq, wl), 1))
    s = s + per_head(jnp.where((diff >= 0) & (diff < WINDOW), 0.0, MASKED))
    o_win = jnp.dot(_softmax_rows(s).astype(bf16), vw_ref[0, 0, pl.ds(w0, wl), :], preferred_element_type=f32)

    gates = jax.nn.sigmoid(zg_ref[0, 0])
    for g in range(NSA_GROUP):
        r = slice(g * tq, (g + 1) * tq)
        o_ref[0, :, g * NSA_HD:(g + 1) * NSA_HD] = (gates[:, 3 * g:3 * g + 1] * o_cmp[r]
                                                    + gates[:, 3 * g + 1:3 * g + 2] * o_sel[r]
                                                    + gates[:, 3 * g + 2:3 * g + 3] * o_win[r])


def nsa_prompt_attention(qs, kc, vc, k_slc, v_slc, k_win, v_win, zg):
    B, S = qs.shape[:2]
    bf16 = jnp.bfloat16
    assert S % NSA_CK == 0 and S % NSA_TQ == 0 and WINDOW % NSA_TQ == 0 and WINDOW + NSA_TQ <= S
    n_sel = S // SEL_BLOCK
    nb = kc.shape[1]
    assert nb <= 128

    def pad_blocks(a):
        return jnp.pad(a.transpose(0, 2, 1, 3).astype(bf16), ((0, 0), (0, 0), (0, 128 - nb), (0, 0)))

    zg4 = zg.reshape(B, S, NSA_KV_HEADS, 3 * NSA_GROUP).transpose(0, 2, 1, 3)
    c = np.arange(128)[:, None]
    jj = np.arange(n_sel)[None, :]
    ratio = SEL_BLOCK // CMP_STRIDE
    msel = ((c >= jj * ratio - (CMP_BLOCK // CMP_STRIDE - 1)) & (c <= jj * ratio + ratio - 1) & (c < nb))
    assert n_sel <= LANES and NSA_TQ == LANES
    expand = (np.arange(S)[None, :] // SEL_BLOCK == np.arange(LANES)[:, None])
    expand = expand.reshape(LANES, S // NSA_CK, NSA_CK).transpose(1, 0, 2)
    row_spec = pl.BlockSpec((1, 1, S, NSA_HD), lambda b, n, i: (b, n, 0, 0))
    blk_spec = pl.BlockSpec((1, 1, 128, NSA_HD), lambda b, n, i: (b, n, 0, 0))
    return pl.pallas_call(
        _nsa_prompt_kernel,
        out_shape=jax.ShapeDtypeStruct((B, S, NSA_QW), jnp.float32),
        grid=(B, NSA_KV_HEADS, S // NSA_TQ),
        in_specs=[pl.BlockSpec((1, NSA_TQ, NSA_GROUP * NSA_HD), lambda b, n, i: (b, i, n)),
                  blk_spec, blk_spec, row_spec, row_spec, row_spec, row_spec,
                  pl.BlockSpec((1, 1, NSA_TQ, 3 * NSA_GROUP), lambda b, n, i: (b, n, i, 0)),
                  pl.BlockSpec((n_sel, 128), lambda b, n, i: (0, 0)),
                  pl.BlockSpec((S // NSA_CK, LANES, NSA_CK), lambda b, n, i: (0, 0, 0))],
        out_specs=pl.BlockSpec((1, NSA_TQ, NSA_GROUP * NSA_HD), lambda b, n, i: (b, i, n)),
        compiler_params=pltpu.CompilerParams(dimension_semantics=("arbitrary", "arbitrary", "arbitrary"),
                                             vmem_limit_bytes=VMEM_LIMIT_BYTES),
        name="nsa_prompt_attention",
    )(qs, pad_blocks(kc), pad_blocks(vc), k_slc, v_slc, k_win, v_win,
      zg4, jnp.asarray(msel.T, bf16), jnp.asarray(expand, bf16))


MLSTM_L = 128
CONV_HALO = 8


def _log_sigmoid(x):
    return -(jnp.maximum(-x, 0.0) + jnp.log1p(jnp.exp(-jnp.abs(x))))


def _mlstm_prompt_kernel(x_ref, xprev_ref, halo0_ref, v_ref, o_ref, gcol_ref, grow_ref, cw_ref, cb_ref,
                         out_ref, c_out, n_out, m_out, c_ref, n_ref, m_ref):
    f32, bf16 = jnp.float32, jnp.bfloat16
    c = pl.program_id(1)
    L = MLSTM_L

    @pl.when(c == 0)
    def _():
        c_ref[...] = jnp.zeros_like(c_ref)
        n_ref[...] = jnp.zeros_like(n_ref)
        m_ref[...] = jnp.zeros_like(m_ref)

    x = x_ref[0]
    halo = jnp.where(c == 0, halo0_ref[0], xprev_ref[0, L - CONV_HALO:L, :])
    ext = jnp.concatenate([halo, x], axis=0)
    conv = cb_ref[...]
    for j in range(CONV_W):
        o = CONV_HALO - (CONV_W - 1) + j
        conv = conv + ext[o:o + L] * cw_ref[j:j + 1, :]
    qk = conv * jax.nn.sigmoid(conv)

    t_id = lax.broadcasted_iota(jnp.int32, (L, L), 0)
    s_id = lax.broadcasted_iota(jnp.int32, (L, L), 1)
    causal = t_id >= s_id
    gcol = gcol_ref[0, 0]
    grow = grow_ref[0, 0]
    for h in range(M_HEADS):
        hd = slice(h * M_HD, (h + 1) * M_HD)
        q = qk[:, hd]
        k = qk[:, M_W + h * M_HD:M_W + (h + 1) * M_HD] * (M_HD ** -0.5)
        v = v_ref[0, :, hd].astype(bf16)
        ig_r = grow[h:h + 1, :]
        ig_c = gcol[:, h:h + 1]
        lf_r = _log_sigmoid(grow[M_HEADS + h:M_HEADS + h + 1, :])
        lf_c = _log_sigmoid(gcol[:, M_HEADS + h:M_HEADS + h + 1])
        b_c = jnp.sum(jnp.where(causal, lf_r, 0.0), axis=1, keepdims=True)
        b_r = jnp.sum(jnp.where(t_id <= s_id, lf_c, 0.0), axis=0, keepdims=True)
        m_prev = m_ref[h]
        dmat = jnp.where(causal, b_c - b_r + ig_r, -jnp.inf)
        inter = b_c + m_prev
        m_t = jnp.maximum(inter, jnp.max(dmat, axis=1, keepdims=True))
        w_intra = jnp.exp(dmat - m_t)
        w_inter = jnp.exp(inter - m_t)
        qb = q.astype(bf16)
        s = lax.dot_general(qb, k.astype(bf16), _NT, preferred_element_type=f32) * w_intra
        num = (jnp.dot(s.astype(bf16), v, preferred_element_type=f32)
               + w_inter * jnp.dot(qb, c_ref[h].astype(bf16), preferred_element_type=f32))
        den = jnp.sum(s, axis=1, keepdims=True) + w_inter * jnp.sum(q * n_ref[h], axis=1, keepdims=True)
        hh = num / jnp.maximum(jnp.abs(den), jnp.exp(-m_t))
        out_ref[0, :, hd] = jax.nn.sigmoid(o_ref[0, :, hd]) * hh
        m_new = m_t[L - 1:L]
        b_last = b_c[L - 1:L]
        w_s = jnp.exp(b_last - b_c + ig_c - m_new)
        w_p = jnp.exp(b_last + m_prev - m_new)
        kw = k * w_s
        c_ref[h] = w_p * c_ref[h] + jnp.dot(kw.T.astype(bf16), v, preferred_element_type=f32)
        n_ref[h] = w_p * n_ref[h] + jnp.sum(kw, axis=0, keepdims=True)
        m_ref[h] = m_new

    @pl.when(c == pl.num_programs(1) - 1)
    def _():
        c_out[0] = c_ref[...]
        n_out[0] = n_ref[...]
        m_out[0] = m_ref[...]


def mlstm_prompt(zqk, zv, zo, zif, conv_w, conv_b, b_if):
    B, T, _ = zqk.shape
    L = MLSTM_L
    assert T % L == 0
    nc = T // L
    f32 = jnp.float32
    gif = zif + b_if
    gcol = gif.reshape(B, nc, L, 2 * M_HEADS)
    grow = gcol.transpose(0, 1, 3, 2)
    halo0 = jnp.zeros((B, CONV_HALO, 2 * M_W), f32)
    out, C, n, m = pl.pallas_call(
        _mlstm_prompt_kernel,
        out_shape=(jax.ShapeDtypeStruct((B, T, M_W), f32),
                   jax.ShapeDtypeStruct((B, M_HEADS, M_HD, M_HD), f32),
                   jax.ShapeDtypeStruct((B, M_HEADS, 1, M_HD), f32),
                   jax.ShapeDtypeStruct((B, M_HEADS, 1, 1), f32)),
        grid=(B, nc),
        in_specs=[pl.BlockSpec((1, L, 2 * M_W), lambda b, c: (b, c, 0)),
                  pl.BlockSpec((1, L, 2 * M_W), lambda b, c: (b, jnp.maximum(c - 1, 0), 0)),
                  pl.BlockSpec((1, CONV_HALO, 2 * M_W), lambda b, c: (b, 0, 0)),
                  pl.BlockSpec((1, L, M_W), lambda b, c: (b, c, 0)),
                  pl.BlockSpec((1, L, M_W), lambda b, c: (b, c, 0)),
                  pl.BlockSpec((1, 1, L, 2 * M_HEADS), lambda b, c: (b, c, 0, 0)),
                  pl.BlockSpec((1, 1, 2 * M_HEADS, L), lambda b, c: (b, c, 0, 0)),
                  pl.BlockSpec((CONV_W, 2 * M_W), lambda b, c: (0, 0)),
                  pl.BlockSpec((1, 2 * M_W), lambda b, c: (0, 0))],
        out_specs=(pl.BlockSpec((1, L, M_W), lambda b, c: (b, c, 0)),
                   pl.BlockSpec((1, M_HEADS, M_HD, M_HD), lambda b, c: (b, 0, 0, 0)),
                   pl.BlockSpec((1, M_HEADS, 1, M_HD), lambda b, c: (b, 0, 0, 0)),
                   pl.BlockSpec((1, M_HEADS, 1, 1), lambda b, c: (b, 0, 0, 0))),
        scratch_shapes=[pltpu.VMEM((M_HEADS, M_HD, M_HD), f32), pltpu.VMEM((M_HEADS, 1, M_HD), f32),
                        pltpu.VMEM((M_HEADS, 1, 1), f32)],
        compiler_params=pltpu.CompilerParams(dimension_semantics=("arbitrary", "arbitrary"),
                                             vmem_limit_bytes=VMEM_LIMIT_BYTES),
        name="mlstm_prompt",
    )(zqk, zqk, halo0, zv, zo, gcol, grow, conv_w, conv_b[None])
    return out, C, n.reshape(B, M_HEADS, M_HD), m.reshape(B, M_HEADS)


def mlstm_chunk(carry, inp):
    C, n, m = carry
    q, k, v, ig, lf = inp
    L = q.shape[2]
    b = jnp.cumsum(lf, axis=-1)
    causal = jnp.tril(jnp.ones((L, L), dtype=bool))
    dmat = jnp.where(causal, b[..., :, None] - b[..., None, :] + ig[..., None, :], -jnp.inf)
    inter = b + m[..., None]
    m_t = jnp.maximum(inter, dmat.max(axis=-1))
    w_intra = jnp.exp(dmat - m_t[..., None])
    w_inter = jnp.exp(inter - m_t)
    s = jnp.einsum('bhtd,bhsd->bhts', q, k) * w_intra
    num = jnp.einsum('bhts,bhsv->bhtv', s, v) + w_inter[..., None] * jnp.einsum('bhtd,bhdv->bhtv', q, C)
    den = s.sum(-1) + w_inter * jnp.einsum('bhtd,bhd->bht', q, n)
    h = num / jnp.maximum(jnp.abs(den), jnp.exp(-m_t))[..., None]
    m_new = m_t[..., -1]
    w_s = jnp.exp(b[..., -1:] - b + ig - m_new[..., None])
    w_p = jnp.exp(b[..., -1] + m - m_new)
    C_new = w_p[..., None, None] * C + jnp.einsum('bhs,bhsd,bhsv->bhdv', w_s, k, v)
    n_new = w_p[..., None] * n + jnp.einsum('bhs,bhsd->bhd', w_s, k)
    return (C_new, n_new, m_new), h


def mlstm_mix(zqk, zv, zo, zif, buf0, C0, n0, m0, conv_w, conv_b, b_if, chunk):
    B, T, _ = zqk.shape
    full = jnp.concatenate([buf0, zqk], axis=1)
    conv = conv_b
    for j in range(CONV_W):
        conv = conv + full[:, j:j + T] * conv_w[j]
    qk = jax.nn.silu(conv)

    def heads(a):
        return a.reshape(B, T, M_HEADS, M_HD).transpose(0, 2, 1, 3)

    q = heads(qk[..., :M_W])
    k = heads(qk[..., M_W:]) * (M_HD ** -0.5)
    v = heads(zv)
    gif = zif + b_if
    ig = gif[..., :M_HEADS].transpose(0, 2, 1)
    lf = jax.nn.log_sigmoid(gif[..., M_HEADS:]).transpose(0, 2, 1)
    nc = T // chunk

    def to_chunks(a):
        return jnp.moveaxis(a.reshape(B, M_HEADS, nc, chunk, *a.shape[3:]), 2, 0)

    (C, n, m), h = lax.scan(mlstm_chunk, (C0, n0, m0),
                            (to_chunks(q), to_chunks(k), to_chunks(v), to_chunks(ig), to_chunks(lf)))
    h = jnp.moveaxis(h, 0, 2).reshape(B, M_HEADS, T, M_HD).transpose(0, 2, 1, 3).reshape(B, T, M_W)
    out = jax.nn.sigmoid(zo) * h
    return out, (C, n, m, full[:, T:])


PEER_COMBOS = 2 * PEER_HEADS
PEER_KEY_ROWS = 8
PEER_TILE = PEER_KEY_ROWS * N_KEYS
PEER_TS_ROWS = 24
LANES = 128
_NT = (((1,), (1,)), ((), ()))


PEER_UNRANKED = 127.0


def _peer_topk_kernel(q_ref, keys_ref, e0_ref, cnt_ref, e1_ref, rk_ref, s_ref, ts_ref):
    c = pl.program_id(1)
    tt = q_ref.shape[0]
    s = lax.dot_general(keys_ref[0], q_ref[...].astype(jnp.bfloat16), _NT,
                        preferred_element_type=jnp.float32)
    s_ref[c] = s
    key_id = lax.broadcasted_iota(jnp.int32, s.shape, 0)
    work = s
    rank = jnp.full(s.shape, PEER_UNRANKED, jnp.float32)
    rows = []
    for r in range(PEER_TOPK + 1):
        m = jnp.max(work, axis=0, keepdims=True)
        first = jnp.min(jnp.where(work == m, key_id, N_KEYS), axis=0, keepdims=True)
        hit = key_id == first
        work = jnp.where(hit, -jnp.inf, work)
        rank = jnp.where(hit, float(r), rank)
        rows.append(m)
    rows.append(jnp.full((PEER_TS_ROWS - PEER_TOPK - 1, tt), -jnp.inf, jnp.float32))
    ts_ref[c] = jnp.concatenate(rows, axis=0)

    @pl.when(c % 2 == 1)
    def _():
        rk_ref[c // 2] = rank.astype(jnp.bfloat16)

    @pl.when(c == PEER_COMBOS - 1)
    def _():
        for h in range(PEER_HEADS):
            t0 = ts_ref[2 * h]
            t1 = ts_ref[2 * h + 1]
            pieces = [t0[0:1] + t1] + [t0[a:a + 1] + t1[0:8] for a in range(1, 8)] + [t0[8:24] + t1[0:1]]
            cand = jnp.concatenate(pieces, axis=0)
            top = t0[0:1] + t1[0:1]
            v16 = top
            v17 = top
            z = jnp.zeros_like(top)
            seen = jnp.zeros_like(top)
            for _ in range(PEER_TOPK + 1):
                m = jnp.max(cand, axis=0, keepdims=True)
                eq = cand == m
                cnt = jnp.sum(jnp.where(eq, 1.0, 0.0), axis=0, keepdims=True)
                active = seen < PEER_TOPK
                take = jnp.minimum(cnt, PEER_TOPK - seen)
                v16 = jnp.where(active, m, v16)
                v17 = jnp.where(seen < PEER_TOPK + 1, m, v17)
                z = z + jnp.where(active, take * jnp.exp(m - top), 0.0)
                seen = seen + cnt
                cand = jnp.where(eq, -jnp.inf, cand)
            tau = 0.5 * v16 + 0.5 * v17
            s0 = s_ref[2 * h]
            need = tau - s0
            cnt = jnp.zeros_like(need)
            for b in range(PEER_TOPK // 2):
                cnt = cnt + jnp.where(t1[b:b + 1] >= need, 1.0, 0.0)
            need_best = tau - t0[0:1]
            extra = jnp.zeros_like(tau)
            for b in range(PEER_TOPK // 2, PEER_TOPK + 1):
                extra = extra + jnp.where(t1[b:b + 1] >= need_best, 1.0, 0.0)
            cnt_ref[h] = cnt + jnp.where(s0 == t0[0:1], extra, 0.0)
            e0_ref[h] = jnp.exp(s_ref[2 * h] - t0[0:1]) / z
            e1_ref[h] = jnp.exp(s_ref[2 * h + 1] - t1[0:1]).astype(jnp.bfloat16)


def peer_scores(q, sub_keys, tt):
    n = q.shape[0]
    assert n % tt == 0
    keys = sub_keys.reshape(PEER_COMBOS, N_KEYS, PEER_QDIM // 2).astype(jnp.bfloat16)
    f32, bf16 = jnp.float32, jnp.bfloat16
    per_head = pl.BlockSpec((PEER_HEADS, N_KEYS, tt), lambda i, c: (0, 0, i))
    return pl.pallas_call(
        _peer_topk_kernel,
        out_shape=(jax.ShapeDtypeStruct((PEER_HEADS, N_KEYS, n), f32),
                   jax.ShapeDtypeStruct((PEER_HEADS, N_KEYS, n), f32),
                   jax.ShapeDtypeStruct((PEER_HEADS, N_KEYS, n), bf16),
                   jax.ShapeDtypeStruct((PEER_HEADS, N_KEYS, n), bf16)),
        grid=(n // tt, PEER_COMBOS),
        in_specs=[pl.BlockSpec((tt, PEER_QDIM // 2), lambda i, c: (i, c)),
                  pl.BlockSpec((1, N_KEYS, PEER_QDIM // 2), lambda i, c: (c, 0, 0))],
        out_specs=(per_head, per_head, per_head, per_head),
        scratch_shapes=[pltpu.VMEM((PEER_COMBOS, N_KEYS, tt), f32),
                        pltpu.VMEM((PEER_COMBOS, PEER_TS_ROWS, tt), f32)],
        compiler_params=pltpu.CompilerParams(dimension_semantics=("arbitrary", "arbitrary"),
                                             vmem_limit_bytes=VMEM_LIMIT_BYTES),
        name="peer_topk",
    )(q, keys)


def _peer_dense_kernel(xb_ref, h_ref, u_ref, vt_ref, cnt_ref, ez_ref, rk_ref, e1_ref, g_ref, b_ref,
                       o_ref, acc_ref, a_ref, w_ref):
    bf16 = jnp.bfloat16
    e = pl.program_id(1)
    tt = xb_ref.shape[0]

    @pl.when(e == 0)
    def _():
        acc_ref[...] = jnp.zeros_like(acc_ref)

    a_ref[...] = lax.dot_general(u_ref[...], xb_ref[...], _NT, preferred_element_type=jnp.float32)
    for r in range(PEER_KEY_ROWS):
        rows = slice(r * N_KEYS, (r + 1) * N_KEYS)
        for t in range(tt // LANES):
            tok = slice(t * LANES, (t + 1) * LANES)
            gate = jnp.zeros((N_KEYS, LANES), bf16)
            for h in range(PEER_HEADS):
                cnt = cnt_ref[h, r:r + 1, tok].astype(bf16)
                picked = jnp.where(rk_ref[h, :, tok] < cnt, e1_ref[h, :, tok], jnp.zeros((), bf16))
                gate = gate + picked * ez_ref[h, r:r + 1, tok].astype(bf16)
            ar = a_ref[rows, tok]
            act = 0.5 * ar * (1.0 + lax.erf(ar * (2.0 ** -0.5)))
            w_ref[rows, tok] = gate * act.astype(bf16)
    acc_ref[...] += jnp.dot(vt_ref[...], w_ref[...], preferred_element_type=jnp.float32)

    @pl.when(e == pl.num_programs(1) - 1)
    def _():
        r = ALPHA * h_ref[...] + acc_ref[...].T
        mu = jnp.mean(r, axis=-1, keepdims=True)
        d = r - mu
        var = jnp.mean(d * d, axis=-1, keepdims=True)
        o_ref[...] = d * lax.rsqrt(var + LN_EPS) * g_ref[...] + b_ref[...]


def peer_tail(h, hb, q, sub_keys, u_bf, vt_bf, ln_g, ln_b, tt):
    n, d = h.shape
    e0z, cnt, e1, rank1 = peer_scores(q, sub_keys, tt)
    n_exp = u_bf.shape[0]
    key_rows = pl.BlockSpec((PEER_HEADS, PEER_KEY_ROWS, tt), lambda i, e: (0, e, i))
    all_keys = pl.BlockSpec((PEER_HEADS, N_KEYS, tt), lambda i, e: (0, 0, i))
    return pl.pallas_call(
        _peer_dense_kernel,
        out_shape=jax.ShapeDtypeStruct((n, d), jnp.float32),
        grid=(n // tt, n_exp // PEER_TILE),
        in_specs=[pl.BlockSpec((tt, d), lambda i, e: (i, 0)),
                  pl.BlockSpec((tt, d), lambda i, e: (i, 0)),
                  pl.BlockSpec((PEER_TILE, d), lambda i, e: (e, 0)),
                  pl.BlockSpec((d, PEER_TILE), lambda i, e: (0, e)),
                  key_rows, key_rows, all_keys, all_keys,
                  pl.BlockSpec((1, d), lambda i, e: (0, 0)),
                  pl.BlockSpec((1, d), lambda i, e: (0, 0))],
        out_specs=pl.BlockSpec((tt, d), lambda i, e: (i, 0)),
        scratch_shapes=[pltpu.VMEM((d, tt), jnp.float32), pltpu.VMEM((PEER_TILE, tt), jnp.float32),
                        pltpu.VMEM((PEER_TILE, tt), jnp.bfloat16)],
        compiler_params=pltpu.CompilerParams(dimension_semantics=("arbitrary", "arbitrary"),
                                             vmem_limit_bytes=VMEM_LIMIT_BYTES),
        name="peer_dense",
    )(hb, h, u_bf, vt_bf, cnt, e0z, rank1, e1, ln_g[None], ln_b[None])


def _out_proj_kernel(x_ref, nsa_ref, m_ref, wn_ref, wm_ref, g_ref, b_ref, wq_ref, h_ref, hb_ref, q_ref):
    f32, bf16 = jnp.float32, jnp.bfloat16
    r = (ALPHA * x_ref[...] + jnp.dot(nsa_ref[...].astype(bf16), wn_ref[...], preferred_element_type=f32)
         + jnp.dot(m_ref[...].astype(bf16), wm_ref[...], preferred_element_type=f32))
    mu = jnp.mean(r, axis=-1, keepdims=True)
    d = r - mu
    var = jnp.mean(d * d, axis=-1, keepdims=True)
    h = d * lax.rsqrt(var + LN_EPS) * g_ref[...] + b_ref[...]
    h_ref[...] = h
    hb = h.astype(bf16)
    hb_ref[...] = hb
    q_ref[...] = jnp.dot(hb, wq_ref[...], preferred_element_type=f32)


def out_proj_fused(x, o_nsa, o_m, w_out, ln_g, ln_b, w_pq, tm):
    n, d = x.shape
    assert n % tm == 0
    bf16 = jnp.bfloat16
    nq = w_pq.shape[1]

    def rows(width):
        return pl.BlockSpec((tm, width), lambda i: (i, 0))

    def whole(a):
        return pl.BlockSpec(a.shape, lambda i: (0, 0))

    wn = w_out[:NSA_QW].astype(bf16)
    wm = w_out[NSA_QW:].astype(bf16)
    wq = w_pq.astype(bf16)
    g, b = ln_g[None], ln_b[None]
    return pl.pallas_call(
        _out_proj_kernel,
        out_shape=(jax.ShapeDtypeStruct((n, d), jnp.float32), jax.ShapeDtypeStruct((n, d), bf16),
                   jax.ShapeDtypeStruct((n, nq), jnp.float32)),
        grid=(n // tm,),
        in_specs=[rows(d), rows(NSA_QW), rows(M_W), whole(wn), whole(wm), whole(g), whole(b), whole(wq)],
        out_specs=(rows(d), rows(d), rows(nq)),
        compiler_params=pltpu.CompilerParams(dimension_semantics=("arbitrary",),
                                             vmem_limit_bytes=VMEM_LIMIT_BYTES),
        name="out_proj",
    )(x, o_nsa, o_m, wn, wm, g, b, wq)


def block_tail(x, o_nsa, o_m, w_out, ln_g, ln_b, w_pq, sub_keys, u_bf, vt_bf, tt):
    lead = x.shape[:-1]
    h, hb, q = out_proj_fused(x.reshape(-1, D_MODEL), o_nsa.reshape(-1, NSA_QW), o_m.reshape(-1, M_W),
                              w_out, ln_g[0], ln_b[0], w_pq, tt)
    return peer_tail(h, hb, q, sub_keys, u_bf, vt_bf, ln_g[1], ln_b[1], tt).reshape(*lead, D_MODEL)


def prompt_mix(x, w_in, pe, w1, b1, w2, conv_w, conv_b, b_if):
    B, S, _ = x.shape
    z = in_proj_fused(x, w_in, jnp.arange(S), 512, True)
    kc = compress_chunks(z["k_cmp"], S // CMP_STRIDE, pe[0], w1[0], b1[0], w2[0])
    vc = compress_chunks(z["v_cmp"], S // CMP_STRIDE, pe[1], w1[1], b1[1], w2[1])
    zgate = z["zgate"].reshape(B, S, GATE_W)
    o_nsa = nsa_prompt_attention(z["q"].reshape(B, S, NSA_QW), kc, vc, z["k_slc_bf"], z["v_slc_bf"],
                                 z["k_win_bf"], z["v_win_bf"], zgate[..., :IN_SPLITS[2]])
    zqk = z["zqk"].reshape(B, S, 2 * M_W)
    o_m, C, n, m = mlstm_prompt(zqk, z["zv"].reshape(B, S, M_W), z["zo"].reshape(B, S, M_W),
                                zgate[..., IN_SPLITS[2]:], conv_w, conv_b, b_if)
    buf = zqk[:, S - (CONV_W - 1):]
    wl = min(WINDOW, S)
    k_cmp, v_cmp, k_slc, v_slc, k_win, v_win = [
        z[k].reshape(B, S, NSA_KV_HEADS, NSA_HD) for k in ("k_cmp", "v_cmp", "k_slc", "v_slc", "k_win", "v_win")]
    return (o_nsa, o_m), (k_cmp, v_cmp, k_slc, v_slc, k_win[:, S - wl:], v_win[:, S - wl:], C, n, m, buf)


def sample_mix(x, kc_pool, vc_pool, ks_pool, vs_pool, kw_buf, vw_buf, C0, n0, m0, buf0, page_table,
               w_in, pe, w1, b1, w2, conv_w, conv_b, b_if):
    B, T, _ = x.shape
    past = page_table.shape[1] * PAGE_SIZE
    pos = past + jnp.arange(T)
    z = in_proj_fused(x, w_in, pos, B * T, False)
    q = z["q"].astype(jnp.float32).reshape(B, T, NSA_HEADS, NSA_HD) * (1.0 / ATTN_SCALE)
    k_cmp, v_cmp, k_slc, v_slc, k_win, v_win = [
        z[k].reshape(B, T, NSA_KV_HEADS, NSA_HD) for k in ("k_cmp", "v_cmp", "k_slc", "v_slc", "k_win", "v_win")]
    zgate = z["zgate"].reshape(B, T, GATE_W)
    gates = jax.nn.sigmoid(zgate[..., :IN_SPLITS[2]]).reshape(B, T, NSA_HEADS, 3)
    zqk, zv, zo, zif = (z["zqk"].reshape(B, T, 2 * M_W), z["zv"].reshape(B, T, M_W), z["zo"].reshape(B, T, M_W),
                        zgate[..., IN_SPLITS[2]:])

    assert (past + T) // CMP_STRIDE == past // CMP_STRIDE

    def compressed(pool, c):
        pages = pool.transpose(0, 2, 3, 1)[page_table]
        return compress_pages(pages, pe[c], w1[c], b1[c], w2[c])

    o_cmp, p = cmp_attend(q, pos, compressed(kc_pool, 0), compressed(vc_pool, 1))
    n_sel = -(-(past + T) // SEL_BLOCK)
    member = select_blocks(p, pos, n_sel)
    o_sel = sample_selected_attention(q, pos, member, ks_pool, vs_pool, k_slc, v_slc, page_table)
    wb = kw_buf.shape[1]
    kw = jnp.concatenate([kw_buf, k_win], axis=1)
    vw = jnp.concatenate([vw_buf, v_win], axis=1)
    kpos = past - wb + jnp.arange(wb + T)
    o_win = win_attend(q, pos, kw, vw, kpos)
    o_nsa = nsa_combine(gates, o_cmp, o_sel, o_win)
    o_m, (C, n, m, buf) = mlstm_mix(zqk, zv, zo, zif, buf0, C0, n0, m0, conv_w, conv_b, b_if, T)
    return (o_nsa, o_m), (k_cmp, v_cmp, k_slc, v_slc, kw[:, T:], vw[:, T:], C, n, m, buf)


def kernel(x_prompt, x_sample, cache_k_cmp, cache_v_cmp, cache_k_slc, cache_v_slc, cache_k_win, cache_v_win,
           state_C, state_n, state_m, state_conv, page_table, w_in, w_out, w_phi1, b_phi1, w_phi2, pe_cmp,
           conv_w, conv_b, b_if, ln_g, ln_b, w_pq, sub_keys, u_tab, v_tab):
    l = 0
    mix_p, st_p = prompt_mix(x_prompt, w_in[l], pe_cmp[l], w_phi1[l], b_phi1[l], w_phi2[l],
                             conv_w[l], conv_b[l], b_if[l])
    mix_s, st_s = sample_mix(x_sample, cache_k_cmp[l], cache_v_cmp[l], cache_k_slc[l], cache_v_slc[l],
                             cache_k_win[l], cache_v_win[l], state_C[l], state_n[l], state_m[l],
                             state_conv[l], page_table, w_in[l], pe_cmp[l], w_phi1[l], b_phi1[l],
                             w_phi2[l], conv_w[l], conv_b[l], b_if[l])
    u_bf = u_tab[l].astype(jnp.bfloat16)
    vt_bf = v_tab[l].astype(jnp.bfloat16).T
    xp = block_tail(x_prompt, *mix_p, w_out[l], ln_g[l], ln_b[l], w_pq[l], sub_keys[l], u_bf, vt_bf, 512)
    xs = block_tail(x_sample, *mix_s, w_out[l], ln_g[l], ln_b[l], w_pq[l], sub_keys[l], u_bf, vt_bf, 128)
    return (xp, xs) + tuple(a[None] for a in st_p) + tuple(a[None] for a in st_s)
```

```python
import functools

import jax
import jax.numpy as jnp
import numpy as np
from jax import lax
from jax.experimental import pallas as pl
from jax.experimental.pallas import tpu as pltpu

D_MODEL = 1024
DEPTH = 1
PAGE_SIZE = 128
NSA_HEADS = 8
NSA_KV_HEADS = 2
NSA_GROUP = NSA_HEADS // NSA_KV_HEADS
NSA_HD = 64
NSA_QW = NSA_HEADS * NSA_HD
NSA_KVW = NSA_KV_HEADS * NSA_HD
CMP_BLOCK = 32
CMP_STRIDE = 16
SEL_BLOCK = 64
SEL_TOP = 16
WINDOW = 512
Q_BLOCK = 64
ATTN_SCALE = NSA_HD ** -0.5
ROPE_THETA = 10000.0
M_HEADS = 4
M_HD = 128
M_W = M_HEADS * M_HD
M_CHUNK = 64
CONV_W = 4
PEER_HEADS = 8
N_KEYS = 128
PEER_TOPK = 16
PEER_QDIM = 256
PEER_BLOCK = 128
IN_SPLITS = (NSA_QW, 6 * NSA_KVW, 3 * NSA_HEADS, 2 * M_W, M_W, M_W, 2 * M_HEADS)
LN_EPS = 1e-5
ALPHA = (2 * DEPTH) ** 0.25

VMEM_LIMIT_BYTES = 56 * 1024 * 1024


def _mm_kernel(x_ref, w_ref, o_ref):
    o_ref[...] = jnp.dot(x_ref[...].astype(jnp.bfloat16), w_ref[...], preferred_element_type=jnp.float32)


def pallas_matmul(x, w, tm=512):
    M, K = x.shape
    N = w.shape[1]
    tm = min(tm, M)
    assert M % tm == 0
    return pl.pallas_call(
        _mm_kernel,
        out_shape=jax.ShapeDtypeStruct((M, N), jnp.float32),
        grid=(M // tm,),
        in_specs=[pl.BlockSpec((tm, K), lambda i: (i, 0)), pl.BlockSpec((K, N), lambda i: (0, 0))],
        out_specs=pl.BlockSpec((tm, N), lambda i: (i, 0)),
        compiler_params=pltpu.CompilerParams(dimension_semantics=("arbitrary",),
                                             vmem_limit_bytes=VMEM_LIMIT_BYTES),
        name="proj_matmul",
    )(x, w.astype(jnp.bfloat16))


def mm3(x, w):
    lead = x.shape[:-1]
    return pallas_matmul(x.reshape(-1, x.shape[-1]), w).reshape(*lead, w.shape[1])


def layer_norm(x, g, b):
    mu = x.mean(-1, keepdims=True)
    var = jnp.square(x - mu).mean(-1, keepdims=True)
    return (x - mu) * lax.rsqrt(var + LN_EPS) * g + b


def rope(x, pos):
    half = x.shape[-1] // 2
    inv = ROPE_THETA ** (-jnp.arange(half, dtype=jnp.float32) / half)
    ang = pos.astype(jnp.float32)[:, None] * inv[None, :]
    cos = jnp.cos(ang)[:, None, :]
    sin = jnp.sin(ang)[:, None, :]
    x1, x2 = x[..., :half], x[..., half:]
    return jnp.concatenate([x1 * cos - x2 * sin, x2 * cos + x1 * sin], axis=-1)


def split_in_proj(x, w_in):
    z = mm3(x, w_in)
    cuts = [int(c) for c in np.cumsum(IN_SPLITS)[:-1]]
    return jnp.split(z, cuts, axis=-1)


_IN_OFF = np.concatenate([[0], np.cumsum(IN_SPLITS)])
_IN_ORDER = (0, 1, 3, 4, 5, 2, 6)
_N_KV_ROWS = 6
_KV_BF16 = (2, 3, 4, 5)
GATE_W = IN_SPLITS[2] + IN_SPLITS[6]


def _rope_pairs(x, cos, sin_signed):
    half = NSA_HD // 2
    lane = lax.broadcasted_iota(jnp.int32, x.shape, 1)
    partner = jnp.where(lane % NSA_HD < half, pltpu.roll(x, LANES - half, 1), pltpu.roll(x, half, 1))
    return x * cos + partner * sin_signed


def _in_proj_kernel(x_ref, w_ref, cos_ref, sin_ref, q_ref, *rest, kv_major):
    kv_refs = rest[:_N_KV_ROWS]
    rest = rest[_N_KV_ROWS:]
    if kv_major:
        bf_refs, rest = rest[:len(_KV_BF16)], rest[len(_KV_BF16):]
    zqk_ref, zv_ref, zo_ref, zgate_ref = rest
    z = jnp.dot(x_ref[...].astype(jnp.bfloat16), w_ref[...], preferred_element_type=jnp.float32)
    cos = cos_ref[...]
    sin = sin_ref[...]
    for g in range(NSA_QW // LANES):
        sl = slice(g * LANES, (g + 1) * LANES)
        q_ref[:, sl] = (_rope_pairs(z[:, sl], cos, sin) * ATTN_SCALE).astype(jnp.bfloat16)
    for r in range(_N_KV_ROWS):
        row = z[:, NSA_QW + r * NSA_KVW:NSA_QW + (r + 1) * NSA_KVW]
        if r % 2 == 0:
            row = _rope_pairs(row, cos, sin)
        kv_refs[r][...] = row
        if kv_major and r in _KV_BF16:
            dst = bf_refs[_KV_BF16.index(r)]
            for n in range(NSA_KV_HEADS):
                dst[0, n] = row[:, n * NSA_HD:(n + 1) * NSA_HD].astype(jnp.bfloat16)
    o = NSA_QW + _N_KV_ROWS * NSA_KVW
    zqk_ref[...] = z[:, o:o + 2 * M_W]
    zv_ref[...] = z[:, o + 2 * M_W:o + 3 * M_W]
    zo_ref[...] = z[:, o + 3 * M_W:o + 4 * M_W]
    zgate_ref[...] = z[:, o + 4 * M_W:o + 4 * M_W + GATE_W]


def in_proj_fused(x, w_in, pos, tm, kv_major):
    B, T, D = x.shape
    M = B * T
    assert M % tm == 0 and NSA_KVW == LANES and (not kv_major or T % tm == 0)
    f32, bf16 = jnp.float32, jnp.bfloat16
    w = jnp.concatenate([w_in[:, _IN_OFF[i]:_IN_OFF[i + 1]] for i in _IN_ORDER], axis=1).astype(bf16)
    half = NSA_HD // 2
    inv = ROPE_THETA ** (-jnp.arange(half, dtype=f32) / half)
    ang = pos.astype(f32)[:, None] * inv[None, :]
    cos = jnp.tile(jnp.cos(ang), (B, 2 * LANES // NSA_HD))
    sin = jnp.tile(jnp.concatenate([-jnp.sin(ang), jnp.sin(ang)], axis=1), (B, LANES // NSA_HD))
    n_w = w.shape[1]

    def rows(width):
        return pl.BlockSpec((tm, width), lambda i: (i, 0))

    out_shape = [jax.ShapeDtypeStruct((M, NSA_QW), bf16)] + [jax.ShapeDtypeStruct((M, NSA_KVW), f32)] * _N_KV_ROWS
    out_specs = [rows(NSA_QW)] + [rows(NSA_KVW)] * _N_KV_ROWS
    if kv_major:
        per_seq = T // tm
        out_shape += [jax.ShapeDtypeStruct((B, NSA_KV_HEADS, T, NSA_HD), bf16)] * len(_KV_BF16)
        out_specs += [pl.BlockSpec((1, NSA_KV_HEADS, tm, NSA_HD),
                                   lambda i: (i // per_seq, 0, i % per_seq, 0))] * len(_KV_BF16)
    out_shape += [jax.ShapeDtypeStruct((M, 2 * M_W), f32), jax.ShapeDtypeStruct((M, M_W), f32),
                  jax.ShapeDtypeStruct((M, M_W), f32), jax.ShapeDtypeStruct((M, GATE_W), f32)]
    out_specs += [rows(2 * M_W), rows(M_W), rows(M_W), rows(GATE_W)]
    outs = pl.pallas_call(
        functools.partial(_in_proj_kernel, kv_major=kv_major),
        out_shape=tuple(out_shape),
        grid=(M // tm,),
        in_specs=[rows(D), pl.BlockSpec((D, n_w), lambda i: (0, 0)), rows(LANES), rows(LANES)],
        out_specs=tuple(out_specs),
        compiler_params=pltpu.CompilerParams(dimension_semantics=("arbitrary",),
                                             vmem_limit_bytes=VMEM_LIMIT_BYTES),
        name="in_proj",
    )(x.reshape(M, D), w, cos, sin)
    names = ["q", "k_cmp", "v_cmp", "k_slc", "v_slc", "k_win", "v_win"]
    if kv_major:
        names += ["k_slc_bf", "v_slc_bf", "k_win_bf", "v_win_bf"]
    names += ["zqk", "zv", "zo", "zgate"]
    return dict(zip(names, outs))


def nsa_project(zq, zkv, zg, pos):
    B, T, _ = zq.shape
    q = rope(zq.reshape(B, T, NSA_HEADS, NSA_HD), pos)
    kv = zkv.reshape(B, T, 6, NSA_KV_HEADS, NSA_HD)
    rows = (rope(kv[:, :, 0], pos), kv[:, :, 1], rope(kv[:, :, 2], pos), kv[:, :, 3],
            rope(kv[:, :, 4], pos), kv[:, :, 5])
    gates = jax.nn.sigmoid(zg).reshape(B, T, NSA_HEADS, 3)
    return q, rows, gates


def _expanded_w1(w1):
    assert CMP_BLOCK == 2 * CMP_STRIDE
    w1r = w1.reshape(2, CMP_STRIDE, NSA_HD, w1.shape[-1])
    wbig = jnp.einsum('hpdf,kn->pkdnhf', w1r, jnp.eye(NSA_KV_HEADS, dtype=w1.dtype))
    return wbig.reshape(CMP_STRIDE * NSA_KVW, 2 * NSA_KV_HEADS * w1.shape[-1])


def _compress_rows(x, w_ref, bias_ref, w2_ref, o_ref):
    f32, bf16 = jnp.float32, jnp.bfloat16
    rows = x.shape[0]
    f = w2_ref.shape[0]
    proj = jnp.dot(x.astype(bf16), w_ref[...], preferred_element_type=f32)
    for n in range(NSA_KV_HEADS):
        first = proj[:, 2 * n * f:(2 * n + 1) * f]
        second = pltpu.roll(proj[:, (2 * n + 1) * f:(2 * n + 2) * f], rows - 1, 0)
        pre = first + second + bias_ref[...]
        hid = 0.5 * pre * (1.0 + lax.erf(pre * (2.0 ** -0.5)))
        o_ref[:, n * NSA_HD:(n + 1) * NSA_HD] = jnp.dot(hid.astype(bf16), w2_ref[...], preferred_element_type=f32)


def _compress_chunks_kernel(x_ref, w_ref, bias_ref, w2_ref, o_ref):
    _compress_rows(x_ref[...], w_ref, bias_ref, w2_ref, o_ref)


def _compress_weights(pe, w1, b1, w2):
    bf16 = jnp.bfloat16
    bias = jnp.dot(pe.reshape(-1), w1, precision=lax.Precision.HIGHEST) + b1
    return _expanded_w1(w1).astype(bf16), bias[None], w2.astype(bf16)


def compress_chunks(rows, per_seq, pe, w1, b1, w2, tm=512):
    chunks = rows.reshape(-1, CMP_STRIDE * NSA_KVW)
    n = chunks.shape[0]
    tm = min(tm, n)
    assert n % tm == 0 and tm % per_seq == 0
    wbig, bias, w2b = _compress_weights(pe, w1, b1, w2)

    def whole(a):
        return pl.BlockSpec(a.shape, lambda i: (0, 0))

    out = pl.pallas_call(
        _compress_chunks_kernel,
        out_shape=jax.ShapeDtypeStruct((n, NSA_KVW), jnp.float32),
        grid=(n // tm,),
        in_specs=[pl.BlockSpec((tm, chunks.shape[1]), lambda i: (i, 0)), whole(wbig), whole(bias), whole(w2b)],
        out_specs=pl.BlockSpec((tm, NSA_KVW), lambda i: (i, 0)),
        compiler_params=pltpu.CompilerParams(dimension_semantics=("arbitrary",),
                                             vmem_limit_bytes=VMEM_LIMIT_BYTES),
        name="compress_chunks",
    )(chunks, wbig, bias, w2b)
    return out.reshape(n // per_seq, per_seq, NSA_KV_HEADS, NSA_HD)[:, :-1]


PAGE_GROUP = 4


def _compress_pages_kernel(pg_ref, w_ref, bias_ref, w2_ref, o_ref, x_ref, t_ref):
    n_pages = pg_ref.shape[1]
    per_page = PAGE_SIZE // CMP_STRIDE
    group = PAGE_GROUP

    def place(i, carry):
        for u in range(group):
            g = i * group + u
            t_ref[u] = pg_ref[0, g].reshape(NSA_KVW, PAGE_SIZE).T
            row0 = pl.multiple_of(g * per_page, per_page)
            for p in range(CMP_STRIDE):
                x_ref[pl.ds(row0, per_page), p * NSA_KVW:(p + 1) * NSA_KVW] = (
                    t_ref.at[u][pl.ds(p, per_page, stride=CMP_STRIDE), :])
        return carry

    lax.fori_loop(0, n_pages // group, place, 0)
    _compress_rows(x_ref[...], w_ref, bias_ref, w2_ref, o_ref.at[0])


def compress_pages(pages, pe, w1, b1, w2):
    B, n_pages = pages.shape[:2]
    assert pages.shape[2:] == (NSA_KV_HEADS, NSA_HD, PAGE_SIZE) and NSA_KVW == LANES and PAGE_SIZE == LANES
    assert n_pages % PAGE_GROUP == 0
    wbig, bias, w2b = _compress_weights(pe, w1, b1, w2)
    rows = n_pages * (PAGE_SIZE // CMP_STRIDE)

    def whole(a):
        return pl.BlockSpec(a.shape, lambda b: (0, 0))

    out = pl.pallas_call(
        _compress_pages_kernel,
        out_shape=jax.ShapeDtypeStruct((B, rows, NSA_KVW), jnp.float32),
        grid=(B,),
        in_specs=[pl.BlockSpec((1, n_pages, NSA_KV_HEADS, NSA_HD, PAGE_SIZE), lambda b: (b, 0, 0, 0, 0)),
                  whole(wbig), whole(bias), whole(w2b)],
        out_specs=pl.BlockSpec((1, rows, NSA_KVW), lambda b: (b, 0, 0)),
        scratch_shapes=[pltpu.VMEM((rows, wbig.shape[0]), jnp.float32),
                        pltpu.VMEM((PAGE_GROUP, PAGE_SIZE, NSA_KVW), jnp.float32)],
        compiler_params=pltpu.CompilerParams(dimension_semantics=("arbitrary",),
                                             vmem_limit_bytes=VMEM_LIMIT_BYTES),
        name="compress_pages",
    )(pages, wbig, bias, w2b)
    return out.reshape(B, rows, NSA_KV_HEADS, NSA_HD)[:, :-1]


def cmp_attend(q, qpos, kc, vc):
    B, T = q.shape[:2]
    qg = q.reshape(B, T, NSA_KV_HEADS, NSA_GROUP, NSA_HD)
    s = jnp.einsum('btngd,bcnd->btngc', qg, kc) * ATTN_SCALE
    nblk = kc.shape[1]
    blk_end = jnp.arange(nblk) * CMP_STRIDE + CMP_BLOCK - 1
    valid = (blk_end[None, :] <= qpos[:, None])[None, :, None, None, :]
    p = jax.nn.softmax(jnp.where(valid, s, -1e30), axis=-1) * valid
    o = jnp.einsum('btngc,bcnd->btngd', p, vc)
    return o.reshape(B, T, NSA_HEADS, NSA_HD), p


def select_blocks(p, qpos, n_sel):
    imp = p.sum(axis=3)
    R = SEL_BLOCK // CMP_STRIDE
    r = CMP_BLOCK // CMP_STRIDE
    nb = imp.shape[-1]
    right = n_sel * R + R - 1 - nb
    padded = jnp.pad(imp, ((0, 0), (0, 0), (0, 0), (r - 1, right)))
    score = padded[..., 0:(n_sel - 1) * R + 1:R]
    for o in range(1, R + r - 1):
        score = score + padded[..., o:o + (n_sel - 1) * R + 1:R]
    j = jnp.arange(n_sel)[None, :]
    cur = (qpos // SEL_BLOCK)[:, None]
    valid = (j * SEL_BLOCK <= qpos[:, None])[None, :, None, :]
    forced = ((j == 0) | (j == cur) | (j == cur - 1))[None, :, None, :]
    score = jnp.where(forced, jnp.inf, jnp.where(valid, score, -jnp.inf))
    idx = j[0]
    before = (score[..., None, :] > score[..., :, None]) | ((score[..., None, :] == score[..., :, None])
                                                          & (idx[None, :] < idx[:, None]))
    return before.sum(-1) < min(SEL_TOP, n_sel)


def sample_selected_attention(q, qpos, member, k_pool, v_pool, k_new, v_new, page_table):
    B, T = q.shape[:2]
    n_pages = page_table.shape[1]
    per_page = PAGE_SIZE // SEL_BLOCK
    assert member.shape[-1] == n_pages * per_page + 1 and T <= SEL_BLOCK
    kp = k_pool.transpose(0, 2, 3, 1)[page_table]
    vp = v_pool.transpose(0, 2, 3, 1)[page_table]
    qg = q.reshape(B, T, NSA_KV_HEADS, NSA_GROUP, NSA_HD)
    s_past = jnp.einsum('btngd,bpndk->bntgpk', qg, kp) * ATTN_SCALE
    s_new = jnp.einsum('btngd,bsnd->bntgs', qg, k_new) * ATTN_SCALE
    m = member.transpose(0, 2, 1, 3)
    m_past = jnp.repeat(m[..., :-1].reshape(B, NSA_KV_HEADS, T, n_pages, per_page), SEL_BLOCK, axis=-1)
    kpos = (jnp.arange(n_pages) * PAGE_SIZE)[:, None] + jnp.arange(PAGE_SIZE)[None, :]
    m_past = m_past & (kpos[None, None, None] <= qpos[None, None, :, None, None])
    new_pos = n_pages * PAGE_SIZE + jnp.arange(T)
    m_new = m[..., -1:] & (new_pos[None, None, None, :] <= qpos[None, None, :, None])
    logits = jnp.concatenate(
        [jnp.where(m_past[:, :, :, None], s_past, -jnp.inf).reshape(B, NSA_KV_HEADS, T, NSA_GROUP, -1),
         jnp.where(m_new[:, :, :, None], s_new, -jnp.inf)], axis=-1)
    pr = jax.nn.softmax(logits, axis=-1)
    pr_past = pr[..., :n_pages * PAGE_SIZE].reshape(B, NSA_KV_HEADS, T, NSA_GROUP, n_pages, PAGE_SIZE)
    o = (jnp.einsum('bntgpk,bpndk->bntgd', pr_past, vp)
         + jnp.einsum('bntgs,bsnd->bntgd', pr[..., n_pages * PAGE_SIZE:], v_new))
    return o.transpose(0, 2, 1, 3, 4).reshape(B, T, NSA_HEADS, NSA_HD)


def to_blocks(rows, n_sel):
    B, L, KV, hd = rows.shape
    rows = jnp.pad(rows, ((0, 0), (0, n_sel * SEL_BLOCK - L), (0, 0), (0, 0)))
    return rows.reshape(B, n_sel, SEL_BLOCK, KV, hd).transpose(0, 3, 1, 2, 4)


def take_rows(table, idx):
    return table[idx]


def sel_attend(q, qpos, sel, kb, vb):
    B, Tq = q.shape[:2]
    k = sel.shape[-1]
    sel_t = sel.transpose(0, 2, 1, 3)
    gather = jax.vmap(jax.vmap(take_rows))
    kg = gather(kb, sel_t).reshape(B, NSA_KV_HEADS, Tq, k * SEL_BLOCK, NSA_HD)
    vg = gather(vb, sel_t).reshape(B, NSA_KV_HEADS, Tq, k * SEL_BLOCK, NSA_HD)
    kpos = (sel_t[..., None] * SEL_BLOCK + jnp.arange(SEL_BLOCK)).reshape(B, NSA_KV_HEADS, Tq, k * SEL_BLOCK)
    qg = q.reshape(B, Tq, NSA_KV_HEADS, NSA_GROUP, NSA_HD).transpose(0, 2, 1, 3, 4)
    s = jnp.einsum('bntgd,bntsd->bntgs', qg, kg) * ATTN_SCALE
    mask = kpos[:, :, :, None, :] <= qpos[None, None, :, None, None]
    pr = jax.nn.softmax(jnp.where(mask, s, -jnp.inf), axis=-1)
    o = jnp.einsum('bntgs,bntsd->bntgd', pr, vg)
    return o.transpose(0, 2, 1, 3, 4).reshape(B, Tq, NSA_HEADS, NSA_HD)


def win_attend(q, qpos, k, v, kpos):
    B, Tq = q.shape[:2]
    qg = q.reshape(B, Tq, NSA_KV_HEADS, NSA_GROUP, NSA_HD)
    s = jnp.einsum('btngd,bsnd->btngs', qg, k) * ATTN_SCALE
    diff = qpos[:, None] - kpos[None, :]
    mask = ((diff >= 0) & (diff < WINDOW) & (kpos[None, :] >= 0))[None, :, None, None, :]
    pr = jax.nn.softmax(jnp.where(mask, s, -jnp.inf), axis=-1)
    o = jnp.einsum('btngs,bsnd->btngd', pr, v)
    return o.reshape(B, Tq, NSA_HEADS, NSA_HD)


def nsa_combine(gates, o_cmp, o_sel, o_win):
    B, T = gates.shape[:2]
    o = gates[..., 0:1] * o_cmp + gates[..., 1:2] * o_sel + gates[..., 2:3] * o_win
    return o.reshape(B, T, NSA_QW)


NSA_TQ = 128
NSA_CK = 512
MASKED = -1e30


def _softmax_rows(s):
    m = jnp.max(s, axis=-1, keepdims=True)
    e = jnp.exp(s - m)
    return e / jnp.sum(e, axis=-1, keepdims=True)


def _nsa_prompt_kernel(q_ref, kc_ref, vc_ref, ks_ref, vs_ref, kw_ref, vw_ref, zg_ref, msel_ref, exp_ref, o_ref):
    f32, bf16 = jnp.float32, jnp.bfloat16
    tq = NSA_TQ
    q0 = pl.program_id(2) * tq
    qb = q_ref[0]
    qs = jnp.concatenate([qb[:, g * NSA_HD:(g + 1) * NSA_HD] for g in range(NSA_GROUP)], axis=0)
    tpos = q0 + lax.broadcasted_iota(jnp.int32, (tq, 1), 0)

    def per_head(a):
        return jnp.concatenate([a] * NSA_GROUP, axis=0)

    s = lax.dot_general(qs, kc_ref[0, 0], _NT, preferred_element_type=f32)
    cblk = lax.broadcasted_iota(jnp.int32, (tq, 128), 1)
    cvalid = cblk * CMP_STRIDE + (CMP_BLOCK - 1) <= tpos
    s = s + per_head(jnp.where(cvalid, 0.0, MASKED))
    e = jnp.exp(s - jnp.max(s, axis=-1, keepdims=True)) * per_head(jnp.where(cvalid, 1.0, 0.0))
    l = jnp.sum(e, axis=-1, keepdims=True)
    p = e / jnp.where(l > 0.0, l, 1.0)
    o_cmp = jnp.dot(p.astype(bf16), vc_ref[0, 0], preferred_element_type=f32)

    imp = p[0:tq]
    for g in range(1, NSA_GROUP):
        imp = imp + p[g * tq:(g + 1) * tq]
    hi = imp.astype(bf16)
    r1 = imp - hi.astype(f32)
    mid = r1.astype(bf16)
    lo = (r1 - mid.astype(f32)).astype(bf16)
    msel = msel_ref[...]
    score = (lax.dot_general(msel, hi, _NT, preferred_element_type=f32)
             + lax.dot_general(msel, mid, _NT, preferred_element_type=f32)
             + lax.dot_general(msel, lo, _NT, preferred_element_type=f32))
    n_sel = score.shape[0]
    j = lax.broadcasted_iota(jnp.int32, (n_sel, tq), 0)
    tok = q0 + lax.broadcasted_iota(jnp.int32, (n_sel, tq), 1)
    cur = tok // SEL_BLOCK
    forced = (j == 0) | (j == cur) | (j == cur - 1)
    score = jnp.where(forced, jnp.inf, jnp.where(j * SEL_BLOCK <= tok, score, -jnp.inf))
    rank = jnp.zeros((n_sel, tq), f32)
    for jp in range(n_sel):
        row = score[jp:jp + 1, :]
        before = (row > score) | ((row == score) & (j > jp))
        rank = rank + jnp.where(before, 1.0, 0.0)
    chosen_t = jnp.where(rank < SEL_TOP, 1.0, 0.0)
    sel01 = jnp.concatenate([chosen_t, jnp.zeros((LANES - n_sel, tq), f32)], axis=0).T.astype(bf16)

    ck = NSA_CK
    rows = NSA_GROUP * tq

    def sel_chunk(c, carry):
        m, l, acc = carry
        k0 = pl.multiple_of(c * ck, ck)
        s = lax.dot_general(qs, ks_ref[0, 0, pl.ds(k0, ck), :], _NT, preferred_element_type=f32)
        chosen = jnp.dot(sel01, exp_ref[c], preferred_element_type=f32)
        kpos = k0 + lax.broadcasted_iota(jnp.int32, (tq, ck), 1)
        ok = (chosen > 0.5) & (kpos <= tpos)
        s = s + per_head(jnp.where(ok, 0.0, MASKED))
        m_new = jnp.maximum(m, jnp.max(s, axis=-1, keepdims=True))
        a = jnp.exp(m - m_new)
        pr = jnp.exp(s - m_new)
        l = a * l + jnp.sum(pr, axis=-1, keepdims=True)
        acc = a * acc + jnp.dot(pr.astype(bf16), vs_ref[0, 0, pl.ds(k0, ck), :], preferred_element_type=f32)
        return m_new, l, acc

    init = (jnp.full((rows, 1), MASKED, f32), jnp.zeros((rows, 1), f32), jnp.zeros((rows, NSA_HD), f32))
    n_chunks = (q0 + tq + ck - 1) // ck
    _, l_sel, acc_sel = lax.fori_loop(0, n_chunks, sel_chunk, init)
    o_sel = acc_sel / l_sel

    w0 = pl.multiple_of(jnp.maximum(q0 - WINDOW, 0), tq)
    wl = WINDOW + tq
    s = lax.dot_general(qs, kw_ref[0, 0, pl.ds(w0, wl), :], _NT, preferred_element_type=f32)
    diff = tpos - (w0 + lax.broadcasted_iota(jnp.int32, (tq, wl), 1))
    s = s + per_head(jnp.where((diff >= 0) & (diff < WINDOW), 0.0, MASKED))
    o_win = jnp.dot(_softmax_rows(s).astype(bf16), vw_ref[0, 0, pl.ds(w0, wl), :], preferred_element_type=f32)

    gates = jax.nn.sigmoid(zg_ref[0, 0])
    for g in range(NSA_GROUP):
        r = slice(g * tq, (g + 1) * tq)
        o_ref[0, :, g * NSA_HD:(g + 1) * NSA_HD] = (gates[:, 3 * g:3 * g + 1] * o_cmp[r]
                                                    + gates[:, 3 * g + 1:3 * g + 2] * o_sel[r]
                                                    + gates[:, 3 * g + 2:3 * g + 3] * o_win[r])


def nsa_prompt_attention(qs, kc, vc, k_slc, v_slc, k_win, v_win, zg):
    B, S = qs.shape[:2]
    bf16 = jnp.bfloat16
    assert S % NSA_CK == 0 and S % NSA_TQ == 0 and WINDOW % NSA_TQ == 0 and WINDOW + NSA_TQ <= S
    n_sel = S // SEL_BLOCK
    nb = kc.shape[1]
    assert nb <= 128

    def pad_blocks(a):
        return jnp.pad(a.transpose(0, 2, 1, 3).astype(bf16), ((0, 0), (0, 0), (0, 128 - nb), (0, 0)))

    zg4 = zg.reshape(B, S, NSA_KV_HEADS, 3 * NSA_GROUP).transpose(0, 2, 1, 3)
    c = np.arange(128)[:, None]
    jj = np.arange(n_sel)[None, :]
    ratio = SEL_BLOCK // CMP_STRIDE
    msel = ((c >= jj * ratio - (CMP_BLOCK // CMP_STRIDE - 1)) & (c <= jj * ratio + ratio - 1) & (c < nb))
    assert n_sel <= LANES and NSA_TQ == LANES
    expand = (np.arange(S)[None, :] // SEL_BLOCK == np.arange(LANES)[:, None])
    expand = expand.reshape(LANES, S // NSA_CK, NSA_CK).transpose(1, 0, 2)
    row_spec = pl.BlockSpec((1, 1, S, NSA_HD), lambda b, n, i: (b, n, 0, 0))
    blk_spec = pl.BlockSpec((1, 1, 128, NSA_HD), lambda b, n, i: (b, n, 0, 0))
    return pl.pallas_call(
        _nsa_prompt_kernel,
        out_shape=jax.ShapeDtypeStruct((B, S, NSA_QW), jnp.float32),
        grid=(B, NSA_KV_HEADS, S // NSA_TQ),
        in_specs=[pl.BlockSpec((1, NSA_TQ, NSA_GROUP * NSA_HD), lambda b, n, i: (b, i, n)),
                  blk_spec, blk_spec, row_spec, row_spec, row_spec, row_spec,
                  pl.BlockSpec((1, 1, NSA_TQ, 3 * NSA_GROUP), lambda b, n, i: (b, n, i, 0)),
                  pl.BlockSpec((n_sel, 128), lambda b, n, i: (0, 0)),
                  pl.BlockSpec((S // NSA_CK, LANES, NSA_CK), lambda b, n, i: (0, 0, 0))],
        out_specs=pl.BlockSpec((1, NSA_TQ, NSA_GROUP * NSA_HD), lambda b, n, i: (b, i, n)),
        compiler_params=pltpu.CompilerParams(dimension_semantics=("arbitrary", "arbitrary", "arbitrary"),
                                             vmem_limit_bytes=VMEM_LIMIT_BYTES),
        name="nsa_prompt_attention",
    )(qs, pad_blocks(kc), pad_blocks(vc), k_slc, v_slc, k_win, v_win,
      zg4, jnp.asarray(msel.T, bf16), jnp.asarray(expand, bf16))


MLSTM_L = 128
CONV_HALO = 8


def _log_sigmoid(x):
    return -(jnp.maximum(-x, 0.0) + jnp.log1p(jnp.exp(-jnp.abs(x))))


def _mlstm_prompt_kernel(x_ref, xprev_ref, halo0_ref, v_ref, o_ref, gcol_ref, grow_ref, cw_ref, cb_ref,
                         out_ref, c_out, n_out, m_out, c_ref, n_ref, m_ref):
    f32, bf16 = jnp.float32, jnp.bfloat16
    c = pl.program_id(1)
    L = MLSTM_L

    @pl.when(c == 0)
    def _():
        c_ref[...] = jnp.zeros_like(c_ref)
        n_ref[...] = jnp.zeros_like(n_ref)
        m_ref[...] = jnp.zeros_like(m_ref)

    x = x_ref[0]
    halo = jnp.where(c == 0, halo0_ref[0], xprev_ref[0, L - CONV_HALO:L, :])
    ext = jnp.concatenate([halo, x], axis=0)
    conv = cb_ref[...]
    for j in range(CONV_W):
        o = CONV_HALO - (CONV_W - 1) + j
        conv = conv + ext[o:o + L] * cw_ref[j:j + 1, :]
    qk = conv * jax.nn.sigmoid(conv)

    t_id = lax.broadcasted_iota(jnp.int32, (L, L), 0)
    s_id = lax.broadcasted_iota(jnp.int32, (L, L), 1)
    causal = t_id >= s_id
    gcol = gcol_ref[0, 0]
    grow = grow_ref[0, 0]
    for h in range(M_HEADS):
        hd = slice(h * M_HD, (h + 1) * M_HD)
        q = qk[:, hd]
        k = qk[:, M_W + h * M_HD:M_W + (h + 1) * M_HD] * (M_HD ** -0.5)
        v = v_ref[0, :, hd].astype(bf16)
        ig_r = grow[h:h + 1, :]
        ig_c = gcol[:, h:h + 1]
        lf_r = _log_sigmoid(grow[M_HEADS + h:M_HEADS + h + 1, :])
        lf_c = _log_sigmoid(gcol[:, M_HEADS + h:M_HEADS + h + 1])
        b_c = jnp.sum(jnp.where(causal, lf_r, 0.0), axis=1, keepdims=True)
        b_r = jnp.sum(jnp.where(t_id <= s_id, lf_c, 0.0), axis=0, keepdims=True)
        m_prev = m_ref[h]
        dmat = jnp.where(causal, b_c - b_r + ig_r, -jnp.inf)
        inter = b_c + m_prev
        m_t = jnp.maximum(inter, jnp.max(dmat, axis=1, keepdims=True))
        w_intra = jnp.exp(dmat - m_t)
        w_inter = jnp.exp(inter - m_t)
        qb = q.astype(bf16)
        s = lax.dot_general(qb, k.astype(bf16), _NT, preferred_element_type=f32) * w_intra
        num = (jnp.dot(s.astype(bf16), v, preferred_element_type=f32)
               + w_inter * jnp.dot(qb, c_ref[h].astype(bf16), preferred_element_type=f32))
        den = jnp.sum(s, axis=1, keepdims=True) + w_inter * jnp.sum(q * n_ref[h], axis=1, keepdims=True)
        hh = num / jnp.maximum(jnp.abs(den), jnp.exp(-m_t))
        out_ref[0, :, hd] = jax.nn.sigmoid(o_ref[0, :, hd]) * hh
        m_new = m_t[L - 1:L]
        b_last = b_c[L - 1:L]
        w_s = jnp.exp(b_last - b_c + ig_c - m_new)
        w_p = jnp.exp(b_last + m_prev - m_new)
        kw = k * w_s
        c_ref[h] = w_p * c_ref[h] + jnp.dot(kw.T.astype(bf16), v, preferred_element_type=f32)
        n_ref[h] = w_p * n_ref[h] + jnp.sum(kw, axis=0, keepdims=True)
        m_ref[h] = m_new

    @pl.when(c == pl.num_programs(1) - 1)
    def _():
        c_out[0] = c_ref[...]
        n_out[0] = n_ref[...]
        m_out[0] = m_ref[...]


def mlstm_prompt(zqk, zv, zo, zif, conv_w, conv_b, b_if):
    B, T, _ = zqk.shape
    L = MLSTM_L
    assert T % L == 0
    nc = T // L
    f32 = jnp.float32
    gif = zif + b_if
    gcol = gif.reshape(B, nc, L, 2 * M_HEADS)
    grow = gcol.transpose(0, 1, 3, 2)
    halo0 = jnp.zeros((B, CONV_HALO, 2 * M_W), f32)
    out, C, n, m = pl.pallas_call(
        _mlstm_prompt_kernel,
        out_shape=(jax.ShapeDtypeStruct((B, T, M_W), f32),
                   jax.ShapeDtypeStruct((B, M_HEADS, M_HD, M_HD), f32),
                   jax.ShapeDtypeStruct((B, M_HEADS, 1, M_HD), f32),
                   jax.ShapeDtypeStruct((B, M_HEADS, 1, 1), f32)),
        grid=(B, nc),
        in_specs=[pl.BlockSpec((1, L, 2 * M_W), lambda b, c: (b, c, 0)),
                  pl.BlockSpec((1, L, 2 * M_W), lambda b, c: (b, jnp.maximum(c - 1, 0), 0)),
                  pl.BlockSpec((1, CONV_HALO, 2 * M_W), lambda b, c: (b, 0, 0)),
                  pl.BlockSpec((1, L, M_W), lambda b, c: (b, c, 0)),
                  pl.BlockSpec((1, L, M_W), lambda b, c: (b, c, 0)),
                  pl.BlockSpec((1, 1, L, 2 * M_HEADS), lambda b, c: (b, c, 0, 0)),
                  pl.BlockSpec((1, 1, 2 * M_HEADS, L), lambda b, c: (b, c, 0, 0)),
                  pl.BlockSpec((CONV_W, 2 * M_W), lambda b, c: (0, 0)),
                  pl.BlockSpec((1, 2 * M_W), lambda b, c: (0, 0))],
        out_specs=(pl.BlockSpec((1, L, M_W), lambda b, c: (b, c, 0)),
                   pl.BlockSpec((1, M_HEADS, M_HD, M_HD), lambda b, c: (b, 0, 0, 0)),
                   pl.BlockSpec((1, M_HEADS, 1, M_HD), lambda b, c: (b, 0, 0, 0)),
                   pl.BlockSpec((1, M_HEADS, 1, 1), lambda b, c: (b, 0, 0, 0))),
        scratch_shapes=[pltpu.VMEM((M_HEADS, M_HD, M_HD), f32), pltpu.VMEM((M_HEADS, 1, M_HD), f32),
                        pltpu.VMEM((M_HEADS, 1, 1), f32)],
        compiler_params=pltpu.CompilerParams(dimension_semantics=("arbitrary", "arbitrary"),
                                             vmem_limit_bytes=VMEM_LIMIT_BYTES),
        name="mlstm_prompt",
    )(zqk, zqk, halo0, zv, zo, gcol, grow, conv_w, conv_b[None])
    return out, C, n.reshape(B, M_HEADS, M_HD), m.reshape(B, M_HEADS)


def mlstm_chunk(carry, inp):
    C, n, m = carry
    q, k, v, ig, lf = inp
    L = q.shape[2]
    b = jnp.cumsum(lf, axis=-1)
    causal = jnp.tril(jnp.ones((L, L), dtype=bool))
    dmat = jnp.where(causal, b[..., :, None] - b[..., None, :] + ig[..., None, :], -jnp.inf)
    inter = b + m[..., None]
    m_t = jnp.maximum(inter, dmat.max(axis=-1))
    w_intra = jnp.exp(dmat - m_t[..., None])
    w_inter = jnp.exp(inter - m_t)
    s = jnp.einsum('bhtd,bhsd->bhts', q, k) * w_intra
    num = jnp.einsum('bhts,bhsv->bhtv', s, v) + w_inter[..., None] * jnp.einsum('bhtd,bhdv->bhtv', q, C)
    den = s.sum(-1) + w_inter * jnp.einsum('bhtd,bhd->bht', q, n)
    h = num / jnp.maximum(jnp.abs(den), jnp.exp(-m_t))[..., None]
    m_new = m_t[..., -1]
    w_s = jnp.exp(b[..., -1:] - b + ig - m_new[..., None])
    w_p = jnp.exp(b[..., -1] + m - m_new)
    C_new = w_p[..., None, None] * C + jnp.einsum('bhs,bhsd,bhsv->bhdv', w_s, k, v)
    n_new = w_p[..., None] * n + jnp.einsum('bhs,bhsd->bhd', w_s, k)
    return (C_new, n_new, m_new), h


def mlstm_mix(zqk, zv, zo, zif, buf0, C0, n0, m0, conv_w, conv_b, b_if, chunk):
    B, T, _ = zqk.shape
    full = jnp.concatenate([buf0, zqk], axis=1)
    conv = conv_b
    for j in range(CONV_W):
        conv = conv + full[:, j:j + T] * conv_w[j]
    qk = jax.nn.silu(conv)

    def heads(a):
        return a.reshape(B, T, M_HEADS, M_HD).transpose(0, 2, 1, 3)

    q = heads(qk[..., :M_W])
    k = heads(qk[..., M_W:]) * (M_HD ** -0.5)
    v = heads(zv)
    gif = zif + b_if
    ig = gif[..., :M_HEADS].transpose(0, 2, 1)
    lf = jax.nn.log_sigmoid(gif[..., M_HEADS:]).transpose(0, 2, 1)
    nc = T // chunk

    def to_chunks(a):
        return jnp.moveaxis(a.reshape(B, M_HEADS, nc, chunk, *a.shape[3:]), 2, 0)

    (C, n, m), h = lax.scan(mlstm_chunk, (C0, n0, m0),
                            (to_chunks(q), to_chunks(k), to_chunks(v), to_chunks(ig), to_chunks(lf)))
    h = jnp.moveaxis(h, 0, 2).reshape(B, M_HEADS, T, M_HD).transpose(0, 2, 1, 3).reshape(B, T, M_W)
    out = jax.nn.sigmoid(zo) * h
    return out, (C, n, m, full[:, T:])


PEER_COMBOS = 2 * PEER_HEADS
PEER_KEY_ROWS = 8
PEER_TILE = PEER_KEY_ROWS * N_KEYS
PEER_TS_ROWS = 24
LANES = 128
_NT = (((1,), (1,)), ((), ()))


PEER_UNRANKED = 127.0


def _peer_topk_kernel(q_ref, keys_ref, e0_ref, cnt_ref, e1_ref, rk_ref, s_ref, ts_ref):
    c = pl.program_id(1)
    tt = q_ref.shape[0]
    s = lax.dot_general(keys_ref[0], q_ref[...].astype(jnp.bfloat16), _NT,
                        preferred_element_type=jnp.float32)
    s_ref[c] = s
    key_id = lax.broadcasted_iota(jnp.int32, s.shape, 0)
    work = s
    rank = jnp.full(s.shape, PEER_UNRANKED, jnp.float32)
    rows = []
    for r in range(PEER_TOPK + 1):
        m = jnp.max(work, axis=0, keepdims=True)
        first = jnp.min(jnp.where(work == m, key_id, N_KEYS), axis=0, keepdims=True)
        hit = key_id == first
        work = jnp.where(hit, -jnp.inf, work)
        rank = jnp.where(hit, float(r), rank)
        rows.append(m)
    rows.append(jnp.full((PEER_TS_ROWS - PEER_TOPK - 1, tt), -jnp.inf, jnp.float32))
    ts_ref[c] = jnp.concatenate(rows, axis=0)

    @pl.when(c % 2 == 1)
    def _():
        rk_ref[c // 2] = rank.astype(jnp.bfloat16)

    @pl.when(c == PEER_COMBOS - 1)
    def _():
        for h in range(PEER_HEADS):
            t0 = ts_ref[2 * h]
            t1 = ts_ref[2 * h + 1]
            pieces = [t0[0:1] + t1] + [t0[a:a + 1] + t1[0:8] for a in range(1, 8)] + [t0[8:24] + t1[0:1]]
            cand = jnp.concatenate(pieces, axis=0)
            top = t0[0:1] + t1[0:1]
            v16 = top
            v17 = top
            z = jnp.zeros_like(top)
            seen = jnp.zeros_like(top)
            for _ in range(PEER_TOPK + 1):
                m = jnp.max(cand, axis=0, keepdims=True)
                eq = cand == m
                cnt = jnp.sum(jnp.where(eq, 1.0, 0.0), axis=0, keepdims=True)
                active = seen < PEER_TOPK
                take = jnp.minimum(cnt, PEER_TOPK - seen)
                v16 = jnp.where(active, m, v16)
                v17 = jnp.where(seen < PEER_TOPK + 1, m, v17)
                z = z + jnp.where(active, take * jnp.exp(m - top), 0.0)
                seen = seen + cnt
                cand = jnp.where(eq, -jnp.inf, cand)
            tau = 0.5 * v16 + 0.5 * v17
            s0 = s_ref[2 * h]
            need = tau - s0
            cnt = jnp.zeros_like(need)
            for b in range(PEER_TOPK // 2):
                cnt = cnt + jnp.where(t1[b:b + 1] >= need, 1.0, 0.0)
            need_best = tau - t0[0:1]
            extra = jnp.zeros_like(tau)
            for b in range(PEER_TOPK // 2, PEER_TOPK + 1):
                extra = extra + jnp.where(t1[b:b + 1] >= need_best, 1.0, 0.0)
            cnt_ref[h] = cnt + jnp.where(s0 == t0[0:1], extra, 0.0)
            e0_ref[h] = jnp.exp(s_ref[2 * h] - t0[0:1]) / z
            e1_ref[h] = jnp.exp(s_ref[2 * h + 1] - t1[0:1]).astype(jnp.bfloat16)


def peer_scores(q, sub_keys, tt):
    n = q.shape[0]
    assert n % tt == 0
    keys = sub_keys.reshape(PEER_COMBOS, N_KEYS, PEER_QDIM // 2).astype(jnp.bfloat16)
    f32, bf16 = jnp.float32, jnp.bfloat16
    per_head = pl.BlockSpec((PEER_HEADS, N_KEYS, tt), lambda i, c: (0, 0, i))
    return pl.pallas_call(
        _peer_topk_kernel,
        out_shape=(jax.ShapeDtypeStruct((PEER_HEADS, N_KEYS, n), f32),
                   jax.ShapeDtypeStruct((PEER_HEADS, N_KEYS, n), f32),
                   jax.ShapeDtypeStruct((PEER_HEADS, N_KEYS, n), bf16),
                   jax.ShapeDtypeStruct((PEER_HEADS, N_KEYS, n), bf16)),
        grid=(n // tt, PEER_COMBOS),
        in_specs=[pl.BlockSpec((tt, PEER_QDIM // 2), lambda i, c: (i, c)),
                  pl.BlockSpec((1, N_KEYS, PEER_QDIM // 2), lambda i, c: (c, 0, 0))],
        out_specs=(per_head, per_head, per_head, per_head),
        scratch_shapes=[pltpu.VMEM((PEER_COMBOS, N_KEYS, tt), f32),
                        pltpu.VMEM((PEER_COMBOS, PEER_TS_ROWS, tt), f32)],
        compiler_params=pltpu.CompilerParams(dimension_semantics=("arbitrary", "arbitrary"),
                                             vmem_limit_bytes=VMEM_LIMIT_BYTES),
        name="peer_topk",
    )(q, keys)


def _peer_dense_kernel(xb_ref, h_ref, u_ref, vt_ref, cnt_ref, ez_ref, rk_ref, e1_ref, g_ref, b_ref,
                       o_ref, acc_ref, a_ref, w_ref):
    bf16 = jnp.bfloat16
    e = pl.program_id(1)
    tt = xb_ref.shape[0]

    @pl.when(e == 0)
    def _():
        acc_ref[...] = jnp.zeros_like(acc_ref)

    a_ref[...] = lax.dot_general(u_ref[...], xb_ref[...], _NT, preferred_element_type=jnp.float32)
    for r in range(PEER_KEY_ROWS):
        rows = slice(r * N_KEYS, (r + 1) * N_KEYS)
        for t in range(tt // LANES):
            tok = slice(t * LANES, (t + 1) * LANES)
            gate = jnp.zeros((N_KEYS, LANES), bf16)
            for h in range(PEER_HEADS):
                cnt = cnt_ref[h, r:r + 1, tok].astype(bf16)
                picked = jnp.where(rk_ref[h, :, tok] < cnt, e1_ref[h, :, tok], jnp.zeros((), bf16))
                gate = gate + picked * ez_ref[h, r:r + 1, tok].astype(bf16)
            ar = a_ref[rows, tok]
            act = 0.5 * ar * (1.0 + lax.erf(ar * (2.0 ** -0.5)))
            w_ref[rows, tok] = gate * act.astype(bf16)
    acc_ref[...] += jnp.dot(vt_ref[0], w_ref[...], preferred_element_type=jnp.float32)

    @pl.when(e == pl.num_programs(1) - 1)
    def _():
        r = ALPHA * h_ref[...] + acc_ref[...].T
        mu = jnp.mean(r, axis=-1, keepdims=True)
        d = r - mu
        var = jnp.mean(d * d, axis=-1, keepdims=True)
        o_ref[...] = d * lax.rsqrt(var + LN_EPS) * g_ref[...] + b_ref[...]


def peer_tail(h, hb, q, sub_keys, u_bf, vt_bf, ln_g, ln_b, tt):
    n, d = h.shape
    e0z, cnt, e1, rank1 = peer_scores(q, sub_keys, tt)
    n_exp = u_bf.shape[0]
    key_rows = pl.BlockSpec((PEER_HEADS, PEER_KEY_ROWS, tt), lambda i, e: (0, e, i))
    all_keys = pl.BlockSpec((PEER_HEADS, N_KEYS, tt), lambda i, e: (0, 0, i))
    return pl.pallas_call(
        _peer_dense_kernel,
        out_shape=jax.ShapeDtypeStruct((n, d), jnp.float32),
        grid=(n // tt, n_exp // PEER_TILE),
        in_specs=[pl.BlockSpec((tt, d), lambda i, e: (i, 0)),
                  pl.BlockSpec((tt, d), lambda i, e: (i, 0)),
                  pl.BlockSpec((PEER_TILE, d), lambda i, e: (e, 0)),
                  pl.BlockSpec((1, d, PEER_TILE), lambda i, e: (e, 0, 0)),
                  key_rows, key_rows, all_keys, all_keys,
                  pl.BlockSpec((1, d), lambda i, e: (0, 0)),
                  pl.BlockSpec((1, d), lambda i, e: (0, 0))],
        out_specs=pl.BlockSpec((tt, d), lambda i, e: (i, 0)),
        scratch_shapes=[pltpu.VMEM((d, tt), jnp.float32), pltpu.VMEM((PEER_TILE, tt), jnp.float32),
                        pltpu.VMEM((PEER_TILE, tt), jnp.bfloat16)],
        compiler_params=pltpu.CompilerParams(dimension_semantics=("arbitrary", "arbitrary"),
                                             vmem_limit_bytes=VMEM_LIMIT_BYTES),
        name="peer_dense",
    )(hb, h, u_bf, vt_bf, cnt, e0z, rank1, e1, ln_g[None], ln_b[None])


def _out_proj_kernel(x_ref, nsa_ref, m_ref, wn_ref, wm_ref, g_ref, b_ref, wq_ref, h_ref, hb_ref, q_ref):
    f32, bf16 = jnp.float32, jnp.bfloat16
    r = (ALPHA * x_ref[...] + jnp.dot(nsa_ref[...].astype(bf16), wn_ref[...], preferred_element_type=f32)
         + jnp.dot(m_ref[...].astype(bf16), wm_ref[...], preferred_element_type=f32))
    mu = jnp.mean(r, axis=-1, keepdims=True)
    d = r - mu
    var = jnp.mean(d * d, axis=-1, keepdims=True)
    h = d * lax.rsqrt(var + LN_EPS) * g_ref[...] + b_ref[...]
    h_ref[...] = h
    hb = h.astype(bf16)
    hb_ref[...] = hb
    q_ref[...] = jnp.dot(hb, wq_ref[...], preferred_element_type=f32)


def out_proj_fused(x, o_nsa, o_m, w_out, ln_g, ln_b, w_pq, tm):
    n, d = x.shape
    assert n % tm == 0
    bf16 = jnp.bfloat16
    nq = w_pq.shape[1]

    def rows(width):
        return pl.BlockSpec((tm, width), lambda i: (i, 0))

    def whole(a):
        return pl.BlockSpec(a.shape, lambda i: (0, 0))

    wn = w_out[:NSA_QW].astype(bf16)
    wm = w_out[NSA_QW:].astype(bf16)
    wq = w_pq.astype(bf16)
    g, b = ln_g[None], ln_b[None]
    return pl.pallas_call(
        _out_proj_kernel,
        out_shape=(jax.ShapeDtypeStruct((n, d), jnp.float32), jax.ShapeDtypeStruct((n, d), bf16),
                   jax.ShapeDtypeStruct((n, nq), jnp.float32)),
        grid=(n // tm,),
        in_specs=[rows(d), rows(NSA_QW), rows(M_W), whole(wn), whole(wm), whole(g), whole(b), whole(wq)],
        out_specs=(rows(d), rows(d), rows(nq)),
        compiler_params=pltpu.CompilerParams(dimension_semantics=("arbitrary",),
                                             vmem_limit_bytes=VMEM_LIMIT_BYTES),
        name="out_proj",
    )(x, o_nsa, o_m, wn, wm, g, b, wq)


def block_tail(x, o_nsa, o_m, w_out, ln_g, ln_b, w_pq, sub_keys, u_bf, vt_bf, tt):
    lead = x.shape[:-1]
    h, hb, q = out_proj_fused(x.reshape(-1, D_MODEL), o_nsa.reshape(-1, NSA_QW), o_m.reshape(-1, M_W),
                              w_out, ln_g[0], ln_b[0], w_pq, tt)
    return peer_tail(h, hb, q, sub_keys, u_bf, vt_bf, ln_g[1], ln_b[1], tt).reshape(*lead, D_MODEL)


def prompt_mix(x, w_in, pe, w1, b1, w2, conv_w, conv_b, b_if):
    B, S, _ = x.shape
    z = in_proj_fused(x, w_in, jnp.arange(S), 512, True)
    kc = compress_chunks(z["k_cmp"], S // CMP_STRIDE, pe[0], w1[0], b1[0], w2[0])
    vc = compress_chunks(z["v_cmp"], S // CMP_STRIDE, pe[1], w1[1], b1[1], w2[1])
    zgate = z["zgate"].reshape(B, S, GATE_W)
    o_nsa = nsa_prompt_attention(z["q"].reshape(B, S, NSA_QW), kc, vc, z["k_slc_bf"], z["v_slc_bf"],
                                 z["k_win_bf"], z["v_win_bf"], zgate[..., :IN_SPLITS[2]])
    zqk = z["zqk"].reshape(B, S, 2 * M_W)
    o_m, C, n, m = mlstm_prompt(zqk, z["zv"].reshape(B, S, M_W), z["zo"].reshape(B, S, M_W),
                                zgate[..., IN_SPLITS[2]:], conv_w, conv_b, b_if)
    buf = zqk[:, S - (CONV_W - 1):]
    wl = min(WINDOW, S)
    k_cmp, v_cmp, k_slc, v_slc, k_win, v_win = [
        z[k].reshape(B, S, NSA_KV_HEADS, NSA_HD) for k in ("k_cmp", "v_cmp", "k_slc", "v_slc", "k_win", "v_win")]
    return (o_nsa, o_m), (k_cmp, v_cmp, k_slc, v_slc, k_win[:, S - wl:], v_win[:, S - wl:], C, n, m, buf)


def sample_mix(x, kc_pool, vc_pool, ks_pool, vs_pool, kw_buf, vw_buf, C0, n0, m0, buf0, page_table,
               w_in, pe, w1, b1, w2, conv_w, conv_b, b_if):
    B, T, _ = x.shape
    past = page_table.shape[1] * PAGE_SIZE
    pos = past + jnp.arange(T)
    z = in_proj_fused(x, w_in, pos, B * T, False)
    q = z["q"].astype(jnp.float32).reshape(B, T, NSA_HEADS, NSA_HD) * (1.0 / ATTN_SCALE)
    k_cmp, v_cmp, k_slc, v_slc, k_win, v_win = [
        z[k].reshape(B, T, NSA_KV_HEADS, NSA_HD) for k in ("k_cmp", "v_cmp", "k_slc", "v_slc", "k_win", "v_win")]
    zgate = z["zgate"].reshape(B, T, GATE_W)
    gates = jax.nn.sigmoid(zgate[..., :IN_SPLITS[2]]).reshape(B, T, NSA_HEADS, 3)
    zqk, zv, zo, zif = (z["zqk"].reshape(B, T, 2 * M_W), z["zv"].reshape(B, T, M_W), z["zo"].reshape(B, T, M_W),
                        zgate[..., IN_SPLITS[2]:])

    assert (past + T) // CMP_STRIDE == past // CMP_STRIDE

    def compressed(pool, c):
        pages = pool.transpose(0, 2, 3, 1)[page_table]
        return compress_pages(pages, pe[c], w1[c], b1[c], w2[c])

    o_cmp, p = cmp_attend(q, pos, compressed(kc_pool, 0), compressed(vc_pool, 1))
    n_sel = -(-(past + T) // SEL_BLOCK)
    member = select_blocks(p, pos, n_sel)
    o_sel = sample_selected_attention(q, pos, member, ks_pool, vs_pool, k_slc, v_slc, page_table)
    wb = kw_buf.shape[1]
    kw = jnp.concatenate([kw_buf, k_win], axis=1)
    vw = jnp.concatenate([vw_buf, v_win], axis=1)
    kpos = past - wb + jnp.arange(wb + T)
    o_win = win_attend(q, pos, kw, vw, kpos)
    o_nsa = nsa_combine(gates, o_cmp, o_sel, o_win)
    o_m, (C, n, m, buf) = mlstm_mix(zqk, zv, zo, zif, buf0, C0, n0, m0, conv_w, conv_b, b_if, T)
    return (o_nsa, o_m), (k_cmp, v_cmp, k_slc, v_slc, kw[:, T:], vw[:, T:], C, n, m, buf)


def kernel(x_prompt, x_sample, cache_k_cmp, cache_v_cmp, cache_k_slc, cache_v_slc, cache_k_win, cache_v_win,
           state_C, state_n, state_m, state_conv, page_table, w_in, w_out, w_phi1, b_phi1, w_phi2, pe_cmp,
           conv_w, conv_b, b_if, ln_g, ln_b, w_pq, sub_keys, u_tab, v_tab):
    l = 0
    mix_p, st_p = prompt_mix(x_prompt, w_in[l], pe_cmp[l], w_phi1[l], b_phi1[l], w_phi2[l],
                             conv_w[l], conv_b[l], b_if[l])
    mix_s, st_s = sample_mix(x_sample, cache_k_cmp[l], cache_v_cmp[l], cache_k_slc[l], cache_v_slc[l],
                             cache_k_win[l], cache_v_win[l], state_C[l], state_n[l], state_m[l],
                             state_conv[l], page_table, w_in[l], pe_cmp[l], w_phi1[l], b_phi1[l],
                             w_phi2[l], conv_w[l], conv_b[l], b_if[l])
    u_bf = u_tab[l].astype(jnp.bfloat16)
    vt_bf = v_tab[l].astype(jnp.bfloat16).reshape(-1, PEER_TILE, D_MODEL).transpose(0, 2, 1)
    xp = block_tail(x_prompt, *mix_p, w_out[l], ln_g[l], ln_b[l], w_pq[l], sub_keys[l], u_bf, vt_bf, 512)
    xs = block_tail(x_sample, *mix_s, w_out[l], ln_g[l], ln_b[l], w_pq[l], sub_keys[l], u_bf, vt_bf, 128)
    return (xp, xs) + tuple(a[None] for a in st_p) + tuple(a[None] for a in st_s)
```

```python
import functools

import jax
import jax.numpy as jnp
import numpy as np
from jax import lax
from jax.experimental import pallas as pl
from jax.experimental.pallas import tpu as pltpu

D_MODEL = 1024
DEPTH = 1
PAGE_SIZE = 128
NSA_HEADS = 8
NSA_KV_HEADS = 2
NSA_GROUP = NSA_HEADS // NSA_KV_HEADS
NSA_HD = 64
NSA_QW = NSA_HEADS * NSA_HD
NSA_KVW = NSA_KV_HEADS * NSA_HD
CMP_BLOCK = 32
CMP_STRIDE = 16
SEL_BLOCK = 64
SEL_TOP = 16
WINDOW = 512
Q_BLOCK = 64
ATTN_SCALE = NSA_HD ** -0.5
ROPE_THETA = 10000.0
M_HEADS = 4
M_HD = 128
M_W = M_HEADS * M_HD
M_CHUNK = 64
CONV_W = 4
PEER_HEADS = 8
N_KEYS = 128
PEER_TOPK = 16
PEER_QDIM = 256
PEER_BLOCK = 128
IN_SPLITS = (NSA_QW, 6 * NSA_KVW, 3 * NSA_HEADS, 2 * M_W, M_W, M_W, 2 * M_HEADS)
LN_EPS = 1e-5
ALPHA = (2 * DEPTH) ** 0.25

VMEM_LIMIT_BYTES = 56 * 1024 * 1024


def _mm_kernel(x_ref, w_ref, o_ref):
    o_ref[...] = jnp.dot(x_ref[...].astype(jnp.bfloat16), w_ref[...], preferred_element_type=jnp.float32)


def pallas_matmul(x, w, tm=512):
    M, K = x.shape
    N = w.shape[1]
    tm = min(tm, M)
    assert M % tm == 0
    return pl.pallas_call(
        _mm_kernel,
        out_shape=jax.ShapeDtypeStruct((M, N), jnp.float32),
        grid=(M // tm,),
        in_specs=[pl.BlockSpec((tm, K), lambda i: (i, 0)), pl.BlockSpec((K, N), lambda i: (0, 0))],
        out_specs=pl.BlockSpec((tm, N), lambda i: (i, 0)),
        compiler_params=pltpu.CompilerParams(dimension_semantics=("arbitrary",),
                                             vmem_limit_bytes=VMEM_LIMIT_BYTES),
        name="proj_matmul",
    )(x, w.astype(jnp.bfloat16))


def mm3(x, w):
    lead = x.shape[:-1]
    return pallas_matmul(x.reshape(-1, x.shape[-1]), w).reshape(*lead, w.shape[1])


def layer_norm(x, g, b):
    mu = x.mean(-1, keepdims=True)
    var = jnp.square(x - mu).mean(-1, keepdims=True)
    return (x - mu) * lax.rsqrt(var + LN_EPS) * g + b


def rope(x, pos):
    half = x.shape[-1] // 2
    inv = ROPE_THETA ** (-jnp.arange(half, dtype=jnp.float32) / half)
    ang = pos.astype(jnp.float32)[:, None] * inv[None, :]
    cos = jnp.cos(ang)[:, None, :]
    sin = jnp.sin(ang)[:, None, :]
    x1, x2 = x[..., :half], x[..., half:]
    return jnp.concatenate([x1 * cos - x2 * sin, x2 * cos + x1 * sin], axis=-1)


def split_in_proj(x, w_in):
    z = mm3(x, w_in)
    cuts = [int(c) for c in np.cumsum(IN_SPLITS)[:-1]]
    return jnp.split(z, cuts, axis=-1)


_IN_OFF = np.concatenate([[0], np.cumsum(IN_SPLITS)])
_IN_ORDER = (0, 1, 3, 4, 5, 2, 6)
_N_KV_ROWS = 6
_KV_BF16 = (2, 3, 4, 5)
GATE_W = IN_SPLITS[2] + IN_SPLITS[6]


def _rope_pairs(x, cos, sin_signed):
    half = NSA_HD // 2
    lane = lax.broadcasted_iota(jnp.int32, x.shape, 1)
    partner = jnp.where(lane % NSA_HD < half, pltpu.roll(x, LANES - half, 1), pltpu.roll(x, half, 1))
    return x * cos + partner * sin_signed


def _in_proj_kernel(x_ref, w_ref, cos_ref, sin_ref, q_ref, *rest, kv_major):
    kv_refs = rest[:_N_KV_ROWS]
    rest = rest[_N_KV_ROWS:]
    if kv_major:
        bf_refs, rest = rest[:len(_KV_BF16)], rest[len(_KV_BF16):]
    zqk_ref, zv_ref, zo_ref, zgate_ref = rest
    z = jnp.dot(x_ref[...].astype(jnp.bfloat16), w_ref[...], preferred_element_type=jnp.float32)
    cos = cos_ref[...]
    sin = sin_ref[...]
    for g in range(NSA_QW // LANES):
        sl = slice(g * LANES, (g + 1) * LANES)
        q_ref[:, sl] = (_rope_pairs(z[:, sl], cos, sin) * ATTN_SCALE).astype(jnp.bfloat16)
    for r in range(_N_KV_ROWS):
        row = z[:, NSA_QW + r * NSA_KVW:NSA_QW + (r + 1) * NSA_KVW]
        if r % 2 == 0:
            row = _rope_pairs(row, cos, sin)
        kv_refs[r][...] = row
        if kv_major and r in _KV_BF16:
            dst = bf_refs[_KV_BF16.index(r)]
            for n in range(NSA_KV_HEADS):
                dst[0, n] = row[:, n * NSA_HD:(n + 1) * NSA_HD].astype(jnp.bfloat16)
    o = NSA_QW + _N_KV_ROWS * NSA_KVW
    zqk_ref[...] = z[:, o:o + 2 * M_W]
    zv_ref[...] = z[:, o + 2 * M_W:o + 3 * M_W]
    zo_ref[...] = z[:, o + 3 * M_W:o + 4 * M_W]
    zgate_ref[...] = z[:, o + 4 * M_W:o + 4 * M_W + GATE_W]


def in_proj_fused(x, w_in, pos, tm, kv_major):
    B, T, D = x.shape
    M = B * T
    assert M % tm == 0 and NSA_KVW == LANES and (not kv_major or T % tm == 0)
    f32, bf16 = jnp.float32, jnp.bfloat16
    w = jnp.concatenate([w_in[:, _IN_OFF[i]:_IN_OFF[i + 1]] for i in _IN_ORDER], axis=1).astype(bf16)
    half = NSA_HD // 2
    inv = ROPE_THETA ** (-jnp.arange(half, dtype=f32) / half)
    ang = pos.astype(f32)[:, None] * inv[None, :]
    cos = jnp.tile(jnp.cos(ang), (B, 2 * LANES // NSA_HD))
    sin = jnp.tile(jnp.concatenate([-jnp.sin(ang), jnp.sin(ang)], axis=1), (B, LANES // NSA_HD))
    n_w = w.shape[1]

    def rows(width):
        return pl.BlockSpec((tm, width), lambda i: (i, 0))

    out_shape = [jax.ShapeDtypeStruct((M, NSA_QW), bf16)] + [jax.ShapeDtypeStruct((M, NSA_KVW), f32)] * _N_KV_ROWS
    out_specs = [rows(NSA_QW)] + [rows(NSA_KVW)] * _N_KV_ROWS
    if kv_major:
        per_seq = T // tm
        out_shape += [jax.ShapeDtypeStruct((B, NSA_KV_HEADS, T, NSA_HD), bf16)] * len(_KV_BF16)
        out_specs += [pl.BlockSpec((1, NSA_KV_HEADS, tm, NSA_HD),
                                   lambda i: (i // per_seq, 0, i % per_seq, 0))] * len(_KV_BF16)
    out_shape += [jax.ShapeDtypeStruct((M, 2 * M_W), f32), jax.ShapeDtypeStruct((M, M_W), f32),
                  jax.ShapeDtypeStruct((M, M_W), f32), jax.ShapeDtypeStruct((M, GATE_W), f32)]
    out_specs += [rows(2 * M_W), rows(M_W), rows(M_W), rows(GATE_W)]
    outs = pl.pallas_call(
        functools.partial(_in_proj_kernel, kv_major=kv_major),
        out_shape=tuple(out_shape),
        grid=(M // tm,),
        in_specs=[rows(D), pl.BlockSpec((D, n_w), lambda i: (0, 0)), rows(LANES), rows(LANES)],
        out_specs=tuple(out_specs),
        compiler_params=pltpu.CompilerParams(dimension_semantics=("arbitrary",),
                                             vmem_limit_bytes=VMEM_LIMIT_BYTES),
        name="in_proj",
    )(x.reshape(M, D), w, cos, sin)
    names = ["q", "k_cmp", "v_cmp", "k_slc", "v_slc", "k_win", "v_win"]
    if kv_major:
        names += ["k_slc_bf", "v_slc_bf", "k_win_bf", "v_win_bf"]
    names += ["zqk", "zv", "zo", "zgate"]
    return dict(zip(names, outs))


def nsa_project(zq, zkv, zg, pos):
    B, T, _ = zq.shape
    q = rope(zq.reshape(B, T, NSA_HEADS, NSA_HD), pos)
    kv = zkv.reshape(B, T, 6, NSA_KV_HEADS, NSA_HD)
    rows = (rope(kv[:, :, 0], pos), kv[:, :, 1], rope(kv[:, :, 2], pos), kv[:, :, 3],
            rope(kv[:, :, 4], pos), kv[:, :, 5])
    gates = jax.nn.sigmoid(zg).reshape(B, T, NSA_HEADS, 3)
    return q, rows, gates


def _expanded_w1(w1):
    assert CMP_BLOCK == 2 * CMP_STRIDE
    w1r = w1.reshape(2, CMP_STRIDE, NSA_HD, w1.shape[-1])
    wbig = jnp.einsum('hpdf,kn->pkdnhf', w1r, jnp.eye(NSA_KV_HEADS, dtype=w1.dtype))
    return wbig.reshape(CMP_STRIDE * NSA_KVW, 2 * NSA_KV_HEADS * w1.shape[-1])


def _compress_rows(x, w_ref, bias_ref, w2_ref, o_ref):
    f32, bf16 = jnp.float32, jnp.bfloat16
    rows = x.shape[0]
    f = w2_ref.shape[0]
    proj = jnp.dot(x.astype(bf16), w_ref[...], preferred_element_type=f32)
    for n in range(NSA_KV_HEADS):
        first = proj[:, 2 * n * f:(2 * n + 1) * f]
        second = pltpu.roll(proj[:, (2 * n + 1) * f:(2 * n + 2) * f], rows - 1, 0)
        pre = first + second + bias_ref[...]
        hid = 0.5 * pre * (1.0 + lax.erf(pre * (2.0 ** -0.5)))
        o_ref[:, n * NSA_HD:(n + 1) * NSA_HD] = jnp.dot(hid.astype(bf16), w2_ref[...], preferred_element_type=f32)


def _compress_chunks_kernel(x_ref, w_ref, bias_ref, w2_ref, o_ref):
    _compress_rows(x_ref[...], w_ref, bias_ref, w2_ref, o_ref)


def _compress_weights(pe, w1, b1, w2):
    bf16 = jnp.bfloat16
    bias = jnp.dot(pe.reshape(-1), w1, precision=lax.Precision.HIGHEST) + b1
    return _expanded_w1(w1).astype(bf16), bias[None], w2.astype(bf16)


def compress_chunks(rows, per_seq, pe, w1, b1, w2, tm=512):
    chunks = rows.reshape(-1, CMP_STRIDE * NSA_KVW)
    n = chunks.shape[0]
    tm = min(tm, n)
    assert n % tm == 0 and tm % per_seq == 0
    wbig, bias, w2b = _compress_weights(pe, w1, b1, w2)

    def whole(a):
        return pl.BlockSpec(a.shape, lambda i: (0, 0))

    out = pl.pallas_call(
        _compress_chunks_kernel,
        out_shape=jax.ShapeDtypeStruct((n, NSA_KVW), jnp.float32),
        grid=(n // tm,),
        in_specs=[pl.BlockSpec((tm, chunks.shape[1]), lambda i: (i, 0)), whole(wbig), whole(bias), whole(w2b)],
        out_specs=pl.BlockSpec((tm, NSA_KVW), lambda i: (i, 0)),
        compiler_params=pltpu.CompilerParams(dimension_semantics=("arbitrary",),
                                             vmem_limit_bytes=VMEM_LIMIT_BYTES),
        name="compress_chunks",
    )(chunks, wbig, bias, w2b)
    return out.reshape(n // per_seq, per_seq, NSA_KV_HEADS, NSA_HD)[:, :-1]


PAGE_GROUP = 4


def _compress_pages_kernel(pg_ref, w_ref, bias_ref, w2_ref, o_ref, x_ref, t_ref):
    n_pages = pg_ref.shape[1]
    per_page = PAGE_SIZE // CMP_STRIDE
    group = PAGE_GROUP

    def place(i, carry):
        for u in range(group):
            g = i * group + u
            t_ref[u] = pg_ref[0, g].reshape(NSA_KVW, PAGE_SIZE).T
            row0 = pl.multiple_of(g * per_page, per_page)
            for p in range(CMP_STRIDE):
                x_ref[pl.ds(row0, per_page), p * NSA_KVW:(p + 1) * NSA_KVW] = (
                    t_ref.at[u][pl.ds(p, per_page, stride=CMP_STRIDE), :])
        return carry

    lax.fori_loop(0, n_pages // group, place, 0)
    _compress_rows(x_ref[...], w_ref, bias_ref, w2_ref, o_ref.at[0])


def compress_pages(pages, pe, w1, b1, w2):
    B, n_pages = pages.shape[:2]
    assert pages.shape[2:] == (NSA_KV_HEADS, NSA_HD, PAGE_SIZE) and NSA_KVW == LANES and PAGE_SIZE == LANES
    assert n_pages % PAGE_GROUP == 0
    wbig, bias, w2b = _compress_weights(pe, w1, b1, w2)
    rows = n_pages * (PAGE_SIZE // CMP_STRIDE)

    def whole(a):
        return pl.BlockSpec(a.shape, lambda b: (0, 0))

    out = pl.pallas_call(
        _compress_pages_kernel,
        out_shape=jax.ShapeDtypeStruct((B, rows, NSA_KVW), jnp.float32),
        grid=(B,),
        in_specs=[pl.BlockSpec((1, n_pages, NSA_KV_HEADS, NSA_HD, PAGE_SIZE), lambda b: (b, 0, 0, 0, 0)),
                  whole(wbig), whole(bias), whole(w2b)],
        out_specs=pl.BlockSpec((1, rows, NSA_KVW), lambda b: (b, 0, 0)),
        scratch_shapes=[pltpu.VMEM((rows, wbig.shape[0]), jnp.float32),
                        pltpu.VMEM((PAGE_GROUP, PAGE_SIZE, NSA_KVW), jnp.float32)],
        compiler_params=pltpu.CompilerParams(dimension_semantics=("arbitrary",),
                                             vmem_limit_bytes=VMEM_LIMIT_BYTES),
        name="compress_pages",
    )(pages, wbig, bias, w2b)
    return out.reshape(B, rows, NSA_KV_HEADS, NSA_HD)[:, :-1]


def cmp_attend(q, qpos, kc, vc):
    B, T = q.shape[:2]
    qg = q.reshape(B, T, NSA_KV_HEADS, NSA_GROUP, NSA_HD)
    s = jnp.einsum('btngd,bcnd->btngc', qg, kc) * ATTN_SCALE
    nblk = kc.shape[1]
    blk_end = jnp.arange(nblk) * CMP_STRIDE + CMP_BLOCK - 1
    valid = (blk_end[None, :] <= qpos[:, None])[None, :, None, None, :]
    p = jax.nn.softmax(jnp.where(valid, s, -1e30), axis=-1) * valid
    o = jnp.einsum('btngc,bcnd->btngd', p, vc)
    return o.reshape(B, T, NSA_HEADS, NSA_HD), p


def select_blocks(p, qpos, n_sel):
    imp = p.sum(axis=3)
    R = SEL_BLOCK // CMP_STRIDE
    r = CMP_BLOCK // CMP_STRIDE
    nb = imp.shape[-1]
    right = n_sel * R + R - 1 - nb
    padded = jnp.pad(imp, ((0, 0), (0, 0), (0, 0), (r - 1, right)))
    score = padded[..., 0:(n_sel - 1) * R + 1:R]
    for o in range(1, R + r - 1):
        score = score + padded[..., o:o + (n_sel - 1) * R + 1:R]
    j = jnp.arange(n_sel)[None, :]
    cur = (qpos // SEL_BLOCK)[:, None]
    valid = (j * SEL_BLOCK <= qpos[:, None])[None, :, None, :]
    forced = ((j == 0) | (j == cur) | (j == cur - 1))[None, :, None, :]
    score = jnp.where(forced, jnp.inf, jnp.where(valid, score, -jnp.inf))
    idx = j[0]
    before = (score[..., None, :] > score[..., :, None]) | ((score[..., None, :] == score[..., :, None])
                                                          & (idx[None, :] < idx[:, None]))
    return before.sum(-1) < min(SEL_TOP, n_sel)


def sample_selected_attention(q, qpos, member, k_pool, v_pool, k_new, v_new, page_table):
    B, T = q.shape[:2]
    n_pages = page_table.shape[1]
    per_page = PAGE_SIZE // SEL_BLOCK
    assert member.shape[-1] == n_pages * per_page + 1 and T <= SEL_BLOCK
    kp = k_pool.transpose(0, 2, 3, 1)[page_table]
    vp = v_pool.transpose(0, 2, 3, 1)[page_table]
    qg = q.reshape(B, T, NSA_KV_HEADS, NSA_GROUP, NSA_HD)
    s_past = jnp.einsum('btngd,bpndk->bntgpk', qg, kp) * ATTN_SCALE
    s_new = jnp.einsum('btngd,bsnd->bntgs', qg, k_new) * ATTN_SCALE
    m = member.transpose(0, 2, 1, 3)
    m_past = jnp.repeat(m[..., :-1].reshape(B, NSA_KV_HEADS, T, n_pages, per_page), SEL_BLOCK, axis=-1)
    kpos = (jnp.arange(n_pages) * PAGE_SIZE)[:, None] + jnp.arange(PAGE_SIZE)[None, :]
    m_past = m_past & (kpos[None, None, None] <= qpos[None, None, :, None, None])
    new_pos = n_pages * PAGE_SIZE + jnp.arange(T)
    m_new = m[..., -1:] & (new_pos[None, None, None, :] <= qpos[None, None, :, None])
    logits = jnp.concatenate(
        [jnp.where(m_past[:, :, :, None], s_past, -jnp.inf).reshape(B, NSA_KV_HEADS, T, NSA_GROUP, -1),
         jnp.where(m_new[:, :, :, None], s_new, -jnp.inf)], axis=-1)
    pr = jax.nn.softmax(logits, axis=-1)
    pr_past = pr[..., :n_pages * PAGE_SIZE].reshape(B, NSA_KV_HEADS, T, NSA_GROUP, n_pages, PAGE_SIZE)
    o = (jnp.einsum('bntgpk,bpndk->bntgd', pr_past, vp)
         + jnp.einsum('bntgs,bsnd->bntgd', pr[..., n_pages * PAGE_SIZE:], v_new))
    return o.transpose(0, 2, 1, 3, 4).reshape(B, T, NSA_HEADS, NSA_HD)


def to_blocks(rows, n_sel):
    B, L, KV, hd = rows.shape
    rows = jnp.pad(rows, ((0, 0), (0, n_sel * SEL_BLOCK - L), (0, 0), (0, 0)))
    return rows.reshape(B, n_sel, SEL_BLOCK, KV, hd).transpose(0, 3, 1, 2, 4)


def take_rows(table, idx):
    return table[idx]


def sel_attend(q, qpos, sel, kb, vb):
    B, Tq = q.shape[:2]
    k = sel.shape[-1]
    sel_t = sel.transpose(0, 2, 1, 3)
    gather = jax.vmap(jax.vmap(take_rows))
    kg = gather(kb, sel_t).reshape(B, NSA_KV_HEADS, Tq, k * SEL_BLOCK, NSA_HD)
    vg = gather(vb, sel_t).reshape(B, NSA_KV_HEADS, Tq, k * SEL_BLOCK, NSA_HD)
    kpos = (sel_t[..., None] * SEL_BLOCK + jnp.arange(SEL_BLOCK)).reshape(B, NSA_KV_HEADS, Tq, k * SEL_BLOCK)
    qg = q.reshape(B, Tq, NSA_KV_HEADS, NSA_GROUP, NSA_HD).transpose(0, 2, 1, 3, 4)
    s = jnp.einsum('bntgd,bntsd->bntgs', qg, kg) * ATTN_SCALE
    mask = kpos[:, :, :, None, :] <= qpos[None, None, :, None, None]
    pr = jax.nn.softmax(jnp.where(mask, s, -jnp.inf), axis=-1)
    o = jnp.einsum('bntgs,bntsd->bntgd', pr, vg)
    return o.transpose(0, 2, 1, 3, 4).reshape(B, Tq, NSA_HEADS, NSA_HD)


def win_attend(q, qpos, k, v, kpos):
    B, Tq = q.shape[:2]
    qg = q.reshape(B, Tq, NSA_KV_HEADS, NSA_GROUP, NSA_HD)
    s = jnp.einsum('btngd,bsnd->btngs', qg, k) * ATTN_SCALE
    diff = qpos[:, None] - kpos[None, :]
    mask = ((diff >= 0) & (diff < WINDOW) & (kpos[None, :] >= 0))[None, :, None, None, :]
    pr = jax.nn.softmax(jnp.where(mask, s, -jnp.inf), axis=-1)
    o = jnp.einsum('btngs,bsnd->btngd', pr, v)
    return o.reshape(B, Tq, NSA_HEADS, NSA_HD)


def nsa_combine(gates, o_cmp, o_sel, o_win):
    B, T = gates.shape[:2]
    o = gates[..., 0:1] * o_cmp + gates[..., 1:2] * o_sel + gates[..., 2:3] * o_win
    return o.reshape(B, T, NSA_QW)


NSA_TQ = 128
NSA_CK = 512
MASKED = -1e30


def _softmax_rows(s):
    m = jnp.max(s, axis=-1, keepdims=True)
    e = jnp.exp(s - m)
    return e / jnp.sum(e, axis=-1, keepdims=True)


def _nsa_prompt_kernel(q_ref, kc_ref, vc_ref, ks_ref, vs_ref, kw_ref, vw_ref, zg_ref, msel_ref, exp_ref, o_ref):
    f32, bf16 = jnp.float32, jnp.bfloat16
    tq = NSA_TQ
    q0 = pl.program_id(2) * tq
    qb = q_ref[0]
    qs = jnp.concatenate([qb[:, g * NSA_HD:(g + 1) * NSA_HD] for g in range(NSA_GROUP)], axis=0)
    tpos = q0 + lax.broadcasted_iota(jnp.int32, (tq, 1), 0)

    def per_head(a):
        return jnp.concatenate([a] * NSA_GROUP, axis=0)

    s = lax.dot_general(qs, kc_ref[0, 0], _NT, preferred_element_type=f32)
    cblk = lax.broadcasted_iota(jnp.int32, (tq, 128), 1)
    cvalid = cblk * CMP_STRIDE + (CMP_BLOCK - 1) <= tpos
    s = s + per_head(jnp.where(cvalid, 0.0, MASKED))
    e = jnp.exp(s - jnp.max(s, axis=-1, keepdims=True)) * per_head(jnp.where(cvalid, 1.0, 0.0))
    l = jnp.sum(e, axis=-1, keepdims=True)
    p = e / jnp.where(l > 0.0, l, 1.0)
    o_cmp = jnp.dot(p.astype(bf16), vc_ref[0, 0], preferred_element_type=f32)

    imp = p[0:tq]
    for g in range(1, NSA_GROUP):
        imp = imp + p[g * tq:(g + 1) * tq]
    hi = imp.astype(bf16)
    r1 = imp - hi.astype(f32)
    mid = r1.astype(bf16)
    lo = (r1 - mid.astype(f32)).astype(bf16)
    msel = msel_ref[...]
    score = (lax.dot_general(msel, hi, _NT, preferred_element_type=f32)
             + lax.dot_general(msel, mid, _NT, preferred_element_type=f32)
             + lax.dot_general(msel, lo, _NT, preferred_element_type=f32))
    n_sel = score.shape[0]
    j = lax.broadcasted_iota(jnp.int32, (n_sel, tq), 0)
    tok = q0 + lax.broadcasted_iota(jnp.int32, (n_sel, tq), 1)
    cur = tok // SEL_BLOCK
    forced = (j == 0) | (j == cur) | (j == cur - 1)
    score = jnp.where(forced, jnp.inf, jnp.where(j * SEL_BLOCK <= tok, score, -jnp.inf))
    rank = jnp.zeros((n_sel, tq), f32)
    for jp in range(n_sel):
        row = score[jp:jp + 1, :]
        before = (row > score) | ((row == score) & (j > jp))
        rank = rank + jnp.where(before, 1.0, 0.0)
    chosen_t = jnp.where(rank < SEL_TOP, 1.0, 0.0)
    sel01 = jnp.concatenate([chosen_t, jnp.zeros((LANES - n_sel, tq), f32)], axis=0).T.astype(bf16)

    ck = NSA_CK
    rows = NSA_GROUP * tq

    def sel_chunk(c, carry):
        m, l, acc = carry
        k0 = pl.multiple_of(c * ck, ck)
        s = lax.dot_general(qs, ks_ref[0, 0, pl.ds(k0, ck), :], _NT, preferred_element_type=f32)
        chosen = jnp.dot(sel01, exp_ref[c], preferred_element_type=f32)
        kpos = k0 + lax.broadcasted_iota(jnp.int32, (tq, ck), 1)
        ok = (chosen > 0.5) & (kpos <= tpos)
        s = s + per_head(jnp.where(ok, 0.0, MASKED))
        m_new = jnp.maximum(m, jnp.max(s, axis=-1, keepdims=True))
        a = jnp.exp(m - m_new)
        pr = jnp.exp(s - m_new)
        l = a * l + jnp.sum(pr, axis=-1, keepdims=True)
        acc = a * acc + jnp.dot(pr.astype(bf16), vs_ref[0, 0, pl.ds(k0, ck), :], preferred_element_type=f32)
        return m_new, l, acc

    init = (jnp.full((rows, 1), MASKED, f32), jnp.zeros((rows, 1), f32), jnp.zeros((rows, NSA_HD), f32))
    n_chunks = (q0 + tq + ck - 1) // ck
    _, l_sel, acc_sel = lax.fori_loop(0, n_chunks, sel_chunk, init)
    o_sel = acc_sel / l_sel

    w0 = pl.multiple_of(jnp.maximum(q0 - WINDOW, 0), tq)
    wl = WINDOW + tq
    s = lax.dot_general(qs, kw_ref[0, 0, pl.ds(w0, wl), :], _NT, preferred_element_type=f32)
    diff = tpos - (w0 + lax.broadcasted_iota(jnp.int32, (tq, wl), 1))
    s = s + per_head(jnp.where((diff >= 0) & (diff < WINDOW), 0.0, MASKED))
    o_win = jnp.dot(_softmax_rows(s).astype(bf16), vw_ref[0, 0, pl.ds(w0, wl), :], preferred_element_type=f32)

    gates = jax.nn.sigmoid(zg_ref[0, 0])
    for g in range(NSA_GROUP):
        r = slice(g * tq, (g + 1) * tq)
        o_ref[0, :, g * NSA_HD:(g + 1) * NSA_HD] = (gates[:, 3 * g:3 * g + 1] * o_cmp[r]
                                                    + gates[:, 3 * g + 1:3 * g + 2] * o_sel[r]
                                                    + gates[:, 3 * g + 2:3 * g + 3] * o_win[r])


def nsa_prompt_attention(qs, kc, vc, k_slc, v_slc, k_win, v_win, zg):
    B, S = qs.shape[:2]
    bf16 = jnp.bfloat16
    assert S % NSA_CK == 0 and S % NSA_TQ == 0 and WINDOW % NSA_TQ == 0 and WINDOW + NSA_TQ <= S
    n_sel = S // SEL_BLOCK
    nb = kc.shape[1]
    assert nb <= 128

    def pad_blocks(a):
        return jnp.pad(a.transpose(0, 2, 1, 3).astype(bf16), ((0, 0), (0, 0), (0, 128 - nb), (0, 0)))

    zg4 = zg.reshape(B, S, NSA_KV_HEADS, 3 * NSA_GROUP).transpose(0, 2, 1, 3)
    c = np.arange(128)[:, None]
    jj = np.arange(n_sel)[None, :]
    ratio = SEL_BLOCK // CMP_STRIDE
    msel = ((c >= jj * ratio - (CMP_BLOCK // CMP_STRIDE - 1)) & (c <= jj * ratio + ratio - 1) & (c < nb))
    assert n_sel <= LANES and NSA_TQ == LANES
    expand = (np.arange(S)[None, :] // SEL_BLOCK == np.arange(LANES)[:, None])
    expand = expand.reshape(LANES, S // NSA_CK, NSA_CK).transpose(1, 0, 2)
    row_spec = pl.BlockSpec((1, 1, S, NSA_HD), lambda b, n, i: (b, n, 0, 0))
    blk_spec = pl.BlockSpec((1, 1, 128, NSA_HD), lambda b, n, i: (b, n, 0, 0))
    return pl.pallas_call(
        _nsa_prompt_kernel,
        out_shape=jax.ShapeDtypeStruct((B, S, NSA_QW), jnp.float32),
        grid=(B, NSA_KV_HEADS, S // NSA_TQ),
        in_specs=[pl.BlockSpec((1, NSA_TQ, NSA_GROUP * NSA_HD), lambda b, n, i: (b, i, n)),
                  blk_spec, blk_spec, row_spec, row_spec, row_spec, row_spec,
                  pl.BlockSpec((1, 1, NSA_TQ, 3 * NSA_GROUP), lambda b, n, i: (b, n, i, 0)),
                  pl.BlockSpec((n_sel, 128), lambda b, n, i: (0, 0)),
                  pl.BlockSpec((S // NSA_CK, LANES, NSA_CK), lambda b, n, i: (0, 0, 0))],
        out_specs=pl.BlockSpec((1, NSA_TQ, NSA_GROUP * NSA_HD), lambda b, n, i: (b, i, n)),
        compiler_params=pltpu.CompilerParams(dimension_semantics=("arbitrary", "arbitrary", "arbitrary"),
                                             vmem_limit_bytes=VMEM_LIMIT_BYTES),
        name="nsa_prompt_attention",
    )(qs, pad_blocks(kc), pad_blocks(vc), k_slc, v_slc, k_win, v_win,
      zg4, jnp.asarray(msel.T, bf16), jnp.asarray(expand, bf16))


MLSTM_L = 128
CONV_HALO = 8


def _log_sigmoid(x):
    return -(jnp.maximum(-x, 0.0) + jnp.log1p(jnp.exp(-jnp.abs(x))))


def _mlstm_prompt_kernel(x_ref, xprev_ref, halo0_ref, v_ref, o_ref, gcol_ref, grow_ref, cw_ref, cb_ref,
                         out_ref, c_out, n_out, m_out, c_ref, n_ref, m_ref):
    f32, bf16 = jnp.float32, jnp.bfloat16
    c = pl.program_id(1)
    L = MLSTM_L

    @pl.when(c == 0)
    def _():
        c_ref[...] = jnp.zeros_like(c_ref)
        n_ref[...] = jnp.zeros_like(n_ref)
        m_ref[...] = jnp.zeros_like(m_ref)

    x = x_ref[0]
    halo = jnp.where(c == 0, halo0_ref[0], xprev_ref[0, L - CONV_HALO:L, :])
    ext = jnp.concatenate([halo, x], axis=0)
    conv = cb_ref[...]
    for j in range(CONV_W):
        o = CONV_HALO - (CONV_W - 1) + j
        conv = conv + ext[o:o + L] * cw_ref[j:j + 1, :]
    qk = conv * jax.nn.sigmoid(conv)

    t_id = lax.broadcasted_iota(jnp.int32, (L, L), 0)
    s_id = lax.broadcasted_iota(jnp.int32, (L, L), 1)
    causal = t_id >= s_id
    gcol = gcol_ref[0, 0]
    grow = grow_ref[0, 0]
    for h in range(M_HEADS):
        hd = slice(h * M_HD, (h + 1) * M_HD)
        q = qk[:, hd]
        k = qk[:, M_W + h * M_HD:M_W + (h + 1) * M_HD] * (M_HD ** -0.5)
        v = v_ref[0, :, hd].astype(bf16)
        ig_r = grow[h:h + 1, :]
        ig_c = gcol[:, h:h + 1]
        lf_r = _log_sigmoid(grow[M_HEADS + h:M_HEADS + h + 1, :])
        lf_c = _log_sigmoid(gcol[:, M_HEADS + h:M_HEADS + h + 1])
        b_c = jnp.sum(jnp.where(causal, lf_r, 0.0), axis=1, keepdims=True)
        b_r = jnp.sum(jnp.where(t_id <= s_id, lf_c, 0.0), axis=0, keepdims=True)
        m_prev = m_ref[h]
        dmat = jnp.where(causal, b_c - b_r + ig_r, -jnp.inf)
        inter = b_c + m_prev
        m_t = jnp.maximum(inter, jnp.max(dmat, axis=1, keepdims=True))
        w_intra = jnp.exp(dmat - m_t)
        w_inter = jnp.exp(inter - m_t)
        qb = q.astype(bf16)
        s = lax.dot_general(qb, k.astype(bf16), _NT, preferred_element_type=f32) * w_intra
        num = (jnp.dot(s.astype(bf16), v, preferred_element_type=f32)
               + w_inter * jnp.dot(qb, c_ref[h].astype(bf16), preferred_element_type=f32))
        den = jnp.sum(s, axis=1, keepdims=True) + w_inter * jnp.sum(q * n_ref[h], axis=1, keepdims=True)
        hh = num / jnp.maximum(jnp.abs(den), jnp.exp(-m_t))
        out_ref[0, :, hd] = jax.nn.sigmoid(o_ref[0, :, hd]) * hh
        m_new = m_t[L - 1:L]
        b_last = b_c[L - 1:L]
        w_s = jnp.exp(b_last - b_c + ig_c - m_new)
        w_p = jnp.exp(b_last + m_prev - m_new)
        kw = k * w_s
        c_ref[h] = w_p * c_ref[h] + jnp.dot(kw.T.astype(bf16), v, preferred_element_type=f32)
        n_ref[h] = w_p * n_ref[h] + jnp.sum(kw, axis=0, keepdims=True)
        m_ref[h] = m_new

    @pl.when(c == pl.num_programs(1) - 1)
    def _():
        c_out[0] = c_ref[...]
        n_out[0] = n_ref[...]
        m_out[0] = m_ref[...]


def mlstm_prompt(zqk, zv, zo, zif, conv_w, conv_b, b_if):
    B, T, _ = zqk.shape
    L = MLSTM_L
    assert T % L == 0
    nc = T // L
    f32 = jnp.float32
    gif = zif + b_if
    gcol = gif.reshape(B, nc, L, 2 * M_HEADS)
    grow = gcol.transpose(0, 1, 3, 2)
    halo0 = jnp.zeros((B, CONV_HALO, 2 * M_W), f32)
    out, C, n, m = pl.pallas_call(
        _mlstm_prompt_kernel,
        out_shape=(jax.ShapeDtypeStruct((B, T, M_W), f32),
                   jax.ShapeDtypeStruct((B, M_HEADS, M_HD, M_HD), f32),
                   jax.ShapeDtypeStruct((B, M_HEADS, 1, M_HD), f32),
                   jax.ShapeDtypeStruct((B, M_HEADS, 1, 1), f32)),
        grid=(B, nc),
        in_specs=[pl.BlockSpec((1, L, 2 * M_W), lambda b, c: (b, c, 0)),
                  pl.BlockSpec((1, L, 2 * M_W), lambda b, c: (b, jnp.maximum(c - 1, 0), 0)),
                  pl.BlockSpec((1, CONV_HALO, 2 * M_W), lambda b, c: (b, 0, 0)),
                  pl.BlockSpec((1, L, M_W), lambda b, c: (b, c, 0)),
                  pl.BlockSpec((1, L, M_W), lambda b, c: (b, c, 0)),
                  pl.BlockSpec((1, 1, L, 2 * M_HEADS), lambda b, c: (b, c, 0, 0)),
                  pl.BlockSpec((1, 1, 2 * M_HEADS, L), lambda b, c: (b, c, 0, 0)),
                  pl.BlockSpec((CONV_W, 2 * M_W), lambda b, c: (0, 0)),
                  pl.BlockSpec((1, 2 * M_W), lambda b, c: (0, 0))],
        out_specs=(pl.BlockSpec((1, L, M_W), lambda b, c: (b, c, 0)),
                   pl.BlockSpec((1, M_HEADS, M_HD, M_HD), lambda b, c: (b, 0, 0, 0)),
                   pl.BlockSpec((1, M_HEADS, 1, M_HD), lambda b, c: (b, 0, 0, 0)),
                   pl.BlockSpec((1, M_HEADS, 1, 1), lambda b, c: (b, 0, 0, 0))),
        scratch_shapes=[pltpu.VMEM((M_HEADS, M_HD, M_HD), f32), pltpu.VMEM((M_HEADS, 1, M_HD), f32),
                        pltpu.VMEM((M_HEADS, 1, 1), f32)],
        compiler_params=pltpu.CompilerParams(dimension_semantics=("arbitrary", "arbitrary"),
                                             vmem_limit_bytes=VMEM_LIMIT_BYTES),
        name="mlstm_prompt",
    )(zqk, zqk, halo0, zv, zo, gcol, grow, conv_w, conv_b[None])
    return out, C, n.reshape(B, M_HEADS, M_HD), m.reshape(B, M_HEADS)


def mlstm_chunk(carry, inp):
    C, n, m = carry
    q, k, v, ig, lf = inp
    L = q.shape[2]
    b = jnp.cumsum(lf, axis=-1)
    causal = jnp.tril(jnp.ones((L, L), dtype=bool))
    dmat = jnp.where(causal, b[..., :, None] - b[..., None, :] + ig[..., None, :], -jnp.inf)
    inter = b + m[..., None]
    m_t = jnp.maximum(inter, dmat.max(axis=-1))
    w_intra = jnp.exp(dmat - m_t[..., None])
    w_inter = jnp.exp(inter - m_t)
    s = jnp.einsum('bhtd,bhsd->bhts', q, k) * w_intra
    num = jnp.einsum('bhts,bhsv->bhtv', s, v) + w_inter[..., None] * jnp.einsum('bhtd,bhdv->bhtv', q, C)
    den = s.sum(-1) + w_inter * jnp.einsum('bhtd,bhd->bht', q, n)
    h = num / jnp.maximum(jnp.abs(den), jnp.exp(-m_t))[..., None]
    m_new = m_t[..., -1]
    w_s = jnp.exp(b[..., -1:] - b + ig - m_new[..., None])
    w_p = jnp.exp(b[..., -1] + m - m_new)
    C_new = w_p[..., None, None] * C + jnp.einsum('bhs,bhsd,bhsv->bhdv', w_s, k, v)
    n_new = w_p[..., None] * n + jnp.einsum('bhs,bhsd->bhd', w_s, k)
    return (C_new, n_new, m_new), h


def mlstm_mix(zqk, zv, zo, zif, buf0, C0, n0, m0, conv_w, conv_b, b_if, chunk):
    B, T, _ = zqk.shape
    full = jnp.concatenate([buf0, zqk], axis=1)
    conv = conv_b
    for j in range(CONV_W):
        conv = conv + full[:, j:j + T] * conv_w[j]
    qk = jax.nn.silu(conv)

    def heads(a):
        return a.reshape(B, T, M_HEADS, M_HD).transpose(0, 2, 1, 3)

    q = heads(qk[..., :M_W])
    k = heads(qk[..., M_W:]) * (M_HD ** -0.5)
    v = heads(zv)
    gif = zif + b_if
    ig = gif[..., :M_HEADS].transpose(0, 2, 1)
    lf = jax.nn.log_sigmoid(gif[..., M_HEADS:]).transpose(0, 2, 1)
    nc = T // chunk

    def to_chunks(a):
        return jnp.moveaxis(a.reshape(B, M_HEADS, nc, chunk, *a.shape[3:]), 2, 0)

    (C, n, m), h = lax.scan(mlstm_chunk, (C0, n0, m0),
                            (to_chunks(q), to_chunks(k), to_chunks(v), to_chunks(ig), to_chunks(lf)))
    h = jnp.moveaxis(h, 0, 2).reshape(B, M_HEADS, T, M_HD).transpose(0, 2, 1, 3).reshape(B, T, M_W)
    out = jax.nn.sigmoid(zo) * h
    return out, (C, n, m, full[:, T:])


PEER_COMBOS = 2 * PEER_HEADS
PEER_KEY_ROWS = 8
PEER_TILE = PEER_KEY_ROWS * N_KEYS
PEER_TS_ROWS = 24
LANES = 128
_NT = (((1,), (1,)), ((), ()))


def _peer_topk_kernel(q_ref, keys_ref, s_ref, e0_ref, e1_ref, tau_ref, ts_ref):
    c = pl.program_id(1)
    tt = q_ref.shape[0]
    s = lax.dot_general(keys_ref[0], q_ref[...].astype(jnp.bfloat16), _NT,
                        preferred_element_type=jnp.float32)
    s_ref[c] = s
    key_id = lax.broadcasted_iota(jnp.int32, s.shape, 0)
    work = s
    rows = []
    for _ in range(PEER_TOPK + 1):
        m = jnp.max(work, axis=0, keepdims=True)
        first = jnp.min(jnp.where(work == m, key_id, N_KEYS), axis=0, keepdims=True)
        work = jnp.where(key_id == first, -jnp.inf, work)
        rows.append(m)
    rows.append(jnp.full((PEER_TS_ROWS - PEER_TOPK - 1, tt), -jnp.inf, jnp.float32))
    ts_ref[c] = jnp.concatenate(rows, axis=0)

    @pl.when(c == PEER_COMBOS - 1)
    def _():
        for h in range(PEER_HEADS):
            t0 = ts_ref[2 * h]
            t1 = ts_ref[2 * h + 1]
            pieces = [t0[0:1] + t1] + [t0[a:a + 1] + t1[0:8] for a in range(1, 8)] + [t0[8:24] + t1[0:1]]
            cand = jnp.concatenate(pieces, axis=0)
            top = t0[0:1] + t1[0:1]
            v16 = top
            v17 = top
            z = jnp.zeros_like(top)
            seen = jnp.zeros_like(top)
            for _ in range(PEER_TOPK + 1):
                m = jnp.max(cand, axis=0, keepdims=True)
                eq = cand == m
                cnt = jnp.sum(jnp.where(eq, 1.0, 0.0), axis=0, keepdims=True)
                active = seen < PEER_TOPK
                take = jnp.minimum(cnt, PEER_TOPK - seen)
                v16 = jnp.where(active, m, v16)
                v17 = jnp.where(seen < PEER_TOPK + 1, m, v17)
                z = z + jnp.where(active, take * jnp.exp(m - top), 0.0)
                seen = seen + cnt
                cand = jnp.where(eq, -jnp.inf, cand)
            tau_ref[h:h + 1, :] = 0.5 * v16 + 0.5 * v17
            e0_ref[h] = jnp.exp(s_ref[2 * h] - t0[0:1]) / z
            e1_ref[h] = jnp.exp(s_ref[2 * h + 1] - t1[0:1])


def peer_scores(q, sub_keys, tt):
    n = q.shape[0]
    assert n % tt == 0
    keys = sub_keys.reshape(PEER_COMBOS, N_KEYS, PEER_QDIM // 2).astype(jnp.bfloat16)
    f32 = jnp.float32
    return pl.pallas_call(
        _peer_topk_kernel,
        out_shape=(jax.ShapeDtypeStruct((PEER_COMBOS, N_KEYS, n), f32),
                   jax.ShapeDtypeStruct((PEER_HEADS, N_KEYS, n), f32),
                   jax.ShapeDtypeStruct((PEER_HEADS, N_KEYS, n), f32),
                   jax.ShapeDtypeStruct((PEER_HEADS, n), f32)),
        grid=(n // tt, PEER_COMBOS),
        in_specs=[pl.BlockSpec((tt, PEER_QDIM // 2), lambda i, c: (i, c)),
                  pl.BlockSpec((1, N_KEYS, PEER_QDIM // 2), lambda i, c: (c, 0, 0))],
        out_specs=(pl.BlockSpec((PEER_COMBOS, N_KEYS, tt), lambda i, c: (0, 0, i)),
                   pl.BlockSpec((PEER_HEADS, N_KEYS, tt), lambda i, c: (0, 0, i)),
                   pl.BlockSpec((PEER_HEADS, N_KEYS, tt), lambda i, c: (0, 0, i)),
                   pl.BlockSpec((PEER_HEADS, tt), lambda i, c: (0, i))),
        scratch_shapes=[pltpu.VMEM((PEER_COMBOS, PEER_TS_ROWS, tt), f32)],
        compiler_params=pltpu.CompilerParams(dimension_semantics=("arbitrary", "arbitrary"),
                                             vmem_limit_bytes=VMEM_LIMIT_BYTES),
        name="peer_topk",
    )(q, keys)


def _peer_dense_kernel(xt_ref, h_ref, u_ref, vt_ref, s0_ref, ez_ref, s_ref, e1_ref, tau_ref, g_ref, b_ref,
                       o_ref, acc_ref, a_ref, w_ref):
    e = pl.program_id(1)
    tt = xt_ref.shape[1]

    @pl.when(e == 0)
    def _():
        acc_ref[...] = jnp.zeros_like(acc_ref)

    a_ref[...] = jnp.dot(u_ref[...], xt_ref[...], preferred_element_type=jnp.float32)
    for r in range(PEER_KEY_ROWS):
        rows = slice(r * N_KEYS, (r + 1) * N_KEYS)
        for t in range(tt // LANES):
            tok = slice(t * LANES, (t + 1) * LANES)
            gate = jnp.zeros((N_KEYS, LANES), jnp.float32)
            for h in range(PEER_HEADS):
                need = tau_ref[h:h + 1, tok] - s0_ref[2 * h, r:r + 1, tok]
                picked = jnp.where(s_ref[2 * h + 1, :, tok] >= need, e1_ref[h, :, tok], 0.0)
                gate = gate + picked * ez_ref[h, r:r + 1, tok]
            ar = a_ref[rows, tok]
            act = 0.5 * ar * (1.0 + lax.erf(ar * (2.0 ** -0.5)))
            w_ref[rows, tok] = (gate * act).astype(jnp.bfloat16)
    acc_ref[...] += jnp.dot(vt_ref[0], w_ref[...], preferred_element_type=jnp.float32)

    @pl.when(e == pl.num_programs(1) - 1)
    def _():
        r = ALPHA * h_ref[...] + acc_ref[...].T
        mu = jnp.mean(r, axis=-1, keepdims=True)
        d = r - mu
        var = jnp.mean(d * d, axis=-1, keepdims=True)
        o_ref[...] = d * lax.rsqrt(var + LN_EPS) * g_ref[...] + b_ref[...]


def peer_tail(h, ht, q, sub_keys, u_bf, vt_bf, ln_g, ln_b, tt):
    n, d = h.shape
    s, e0z, e1, tau = peer_scores(q, sub_keys, tt)
    n_exp = u_bf.shape[0]
    return pl.pallas_call(
        _peer_dense_kernel,
        out_shape=jax.ShapeDtypeStruct((n, d), jnp.float32),
        grid=(n // tt, n_exp // PEER_TILE),
        in_specs=[pl.BlockSpec((d, tt), lambda i, e: (0, i)),
                  pl.BlockSpec((tt, d), lambda i, e: (i, 0)),
                  pl.BlockSpec((PEER_TILE, d), lambda i, e: (e, 0)),
                  pl.BlockSpec((1, d, PEER_TILE), lambda i, e: (e, 0, 0)),
                  pl.BlockSpec((PEER_COMBOS, PEER_KEY_ROWS, tt), lambda i, e: (0, e, i)),
                  pl.BlockSpec((PEER_HEADS, PEER_KEY_ROWS, tt), lambda i, e: (0, e, i)),
                  pl.BlockSpec((PEER_COMBOS, N_KEYS, tt), lambda i, e: (0, 0, i)),
                  pl.BlockSpec((PEER_HEADS, N_KEYS, tt), lambda i, e: (0, 0, i)),
                  pl.BlockSpec((PEER_HEADS, tt), lambda i, e: (0, i)),
                  pl.BlockSpec((1, d), lambda i, e: (0, 0)),
                  pl.BlockSpec((1, d), lambda i, e: (0, 0))],
        out_specs=pl.BlockSpec((tt, d), lambda i, e: (i, 0)),
        scratch_shapes=[pltpu.VMEM((d, tt), jnp.float32), pltpu.VMEM((PEER_TILE, tt), jnp.float32),
                        pltpu.VMEM((PEER_TILE, tt), jnp.bfloat16)],
        compiler_params=pltpu.CompilerParams(dimension_semantics=("arbitrary", "arbitrary"),
                                             vmem_limit_bytes=VMEM_LIMIT_BYTES),
        name="peer_dense",
    )(ht, h, u_bf, vt_bf, s, e0z, s, e1, tau, ln_g[None], ln_b[None])


def _out_proj_kernel(x_ref, nsa_ref, m_ref, wn_ref, wm_ref, g_ref, b_ref, wq_ref, h_ref, ht_ref, q_ref):
    f32, bf16 = jnp.float32, jnp.bfloat16
    r = (ALPHA * x_ref[...] + jnp.dot(nsa_ref[...].astype(bf16), wn_ref[...], preferred_element_type=f32)
         + jnp.dot(m_ref[...].astype(bf16), wm_ref[...], preferred_element_type=f32))
    mu = jnp.mean(r, axis=-1, keepdims=True)
    d = r - mu
    var = jnp.mean(d * d, axis=-1, keepdims=True)
    h = d * lax.rsqrt(var + LN_EPS) * g_ref[...] + b_ref[...]
    h_ref[...] = h
    ht_ref[...] = h.T.astype(bf16)
    q_ref[...] = jnp.dot(h.astype(bf16), wq_ref[...], preferred_element_type=f32)


def out_proj_fused(x, o_nsa, o_m, w_out, ln_g, ln_b, w_pq, tm):
    n, d = x.shape
    assert n % tm == 0
    bf16 = jnp.bfloat16
    nq = w_pq.shape[1]

    def rows(width):
        return pl.BlockSpec((tm, width), lambda i: (i, 0))

    def whole(a):
        return pl.BlockSpec(a.shape, lambda i: (0, 0))

    wn = w_out[:NSA_QW].astype(bf16)
    wm = w_out[NSA_QW:].astype(bf16)
    wq = w_pq.astype(bf16)
    g, b = ln_g[None], ln_b[None]
    return pl.pallas_call(
        _out_proj_kernel,
        out_shape=(jax.ShapeDtypeStruct((n, d), jnp.float32), jax.ShapeDtypeStruct((d, n), bf16),
                   jax.ShapeDtypeStruct((n, nq), jnp.float32)),
        grid=(n // tm,),
        in_specs=[rows(d), rows(NSA_QW), rows(M_W), whole(wn), whole(wm), whole(g), whole(b), whole(wq)],
        out_specs=(rows(d), pl.BlockSpec((d, tm), lambda i: (0, i)), rows(nq)),
        compiler_params=pltpu.CompilerParams(dimension_semantics=("arbitrary",),
                                             vmem_limit_bytes=VMEM_LIMIT_BYTES),
        name="out_proj",
    )(x, o_nsa, o_m, wn, wm, g, b, wq)


def block_tail(x, o_nsa, o_m, w_out, ln_g, ln_b, w_pq, sub_keys, u_bf, vt_bf, tt):
    lead = x.shape[:-1]
    h, ht, q = out_proj_fused(x.reshape(-1, D_MODEL), o_nsa.reshape(-1, NSA_QW), o_m.reshape(-1, M_W),
                              w_out, ln_g[0], ln_b[0], w_pq, tt)
    return peer_tail(h, ht, q, sub_keys, u_bf, vt_bf, ln_g[1], ln_b[1], tt).reshape(*lead, D_MODEL)


def prompt_mix(x, w_in, pe, w1, b1, w2, conv_w, conv_b, b_if):
    B, S, _ = x.shape
    z = in_proj_fused(x, w_in, jnp.arange(S), 512, True)
    kc = compress_chunks(z["k_cmp"], S // CMP_STRIDE, pe[0], w1[0], b1[0], w2[0])
    vc = compress_chunks(z["v_cmp"], S // CMP_STRIDE, pe[1], w1[1], b1[1], w2[1])
    zgate = z["zgate"].reshape(B, S, GATE_W)
    o_nsa = nsa_prompt_attention(z["q"].reshape(B, S, NSA_QW), kc, vc, z["k_slc_bf"], z["v_slc_bf"],
                                 z["k_win_bf"], z["v_win_bf"], zgate[..., :IN_SPLITS[2]])
    zqk = z["zqk"].reshape(B, S, 2 * M_W)
    o_m, C, n, m = mlstm_prompt(zqk, z["zv"].reshape(B, S, M_W), z["zo"].reshape(B, S, M_W),
                                zgate[..., IN_SPLITS[2]:], conv_w, conv_b, b_if)
    buf = zqk[:, S - (CONV_W - 1):]
    wl = min(WINDOW, S)
    k_cmp, v_cmp, k_slc, v_slc, k_win, v_win = [
        z[k].reshape(B, S, NSA_KV_HEADS, NSA_HD) for k in ("k_cmp", "v_cmp", "k_slc", "v_slc", "k_win", "v_win")]
    return (o_nsa, o_m), (k_cmp, v_cmp, k_slc, v_slc, k_win[:, S - wl:], v_win[:, S - wl:], C, n, m, buf)


def sample_mix(x, kc_pool, vc_pool, ks_pool, vs_pool, kw_buf, vw_buf, C0, n0, m0, buf0, page_table,
               w_in, pe, w1, b1, w2, conv_w, conv_b, b_if):
    B, T, _ = x.shape
    past = page_table.shape[1] * PAGE_SIZE
    pos = past + jnp.arange(T)
    z = in_proj_fused(x, w_in, pos, B * T, False)
    q = z["q"].astype(jnp.float32).reshape(B, T, NSA_HEADS, NSA_HD) * (1.0 / ATTN_SCALE)
    k_cmp, v_cmp, k_slc, v_slc, k_win, v_win = [
        z[k].reshape(B, T, NSA_KV_HEADS, NSA_HD) for k in ("k_cmp", "v_cmp", "k_slc", "v_slc", "k_win", "v_win")]
    zgate = z["zgate"].reshape(B, T, GATE_W)
    gates = jax.nn.sigmoid(zgate[..., :IN_SPLITS[2]]).reshape(B, T, NSA_HEADS, 3)
    zqk, zv, zo, zif = (z["zqk"].reshape(B, T, 2 * M_W), z["zv"].reshape(B, T, M_W), z["zo"].reshape(B, T, M_W),
                        zgate[..., IN_SPLITS[2]:])

    assert (past + T) // CMP_STRIDE == past // CMP_STRIDE

    def compressed(pool, c):
        pages = pool.transpose(0, 2, 3, 1)[page_table]
        return compress_pages(pages, pe[c], w1[c], b1[c], w2[c])

    o_cmp, p = cmp_attend(q, pos, compressed(kc_pool, 0), compressed(vc_pool, 1))
    n_sel = -(-(past + T) // SEL_BLOCK)
    member = select_blocks(p, pos, n_sel)
    o_sel = sample_selected_attention(q, pos, member, ks_pool, vs_pool, k_slc, v_slc, page_table)
    wb = kw_buf.shape[1]
    kw = jnp.concatenate([kw_buf, k_win], axis=1)
    vw = jnp.concatenate([vw_buf, v_win], axis=1)
    kpos = past - wb + jnp.arange(wb + T)
    o_win = win_attend(q, pos, kw, vw, kpos)
    o_nsa = nsa_combine(gates, o_cmp, o_sel, o_win)
    o_m, (C, n, m, buf) = mlstm_mix(zqk, zv, zo, zif, buf0, C0, n0, m0, conv_w, conv_b, b_if, T)
    return (o_nsa, o_m), (k_cmp, v_cmp, k_slc, v_slc, kw[:, T:], vw[:, T:], C, n, m, buf)


def kernel(x_prompt, x_sample, cache_k_cmp, cache_v_cmp, cache_k_slc, cache_v_slc, cache_k_win, cache_v_win,
           state_C, state_n, state_m, state_conv, page_table, w_in, w_out, w_phi1, b_phi1, w_phi2, pe_cmp,
           conv_w, conv_b, b_if, ln_g, ln_b, w_pq, sub_keys, u_tab, v_tab):
    l = 0
    mix_p, st_p = prompt_mix(x_prompt, w_in[l], pe_cmp[l], w_phi1[l], b_phi1[l], w_phi2[l],
                             conv_w[l], conv_b[l], b_if[l])
    mix_s, st_s = sample_mix(x_sample, cache_k_cmp[l], cache_v_cmp[l], cache_k_slc[l], cache_v_slc[l],
                             cache_k_win[l], cache_v_win[l], state_C[l], state_n[l], state_m[l],
                             state_conv[l], page_table, w_in[l], pe_cmp[l], w_phi1[l], b_phi1[l],
                             w_phi2[l], conv_w[l], conv_b[l], b_if[l])
    u_bf = u_tab[l].astype(jnp.bfloat16)
    vt_bf = v_tab[l].astype(jnp.bfloat16).reshape(-1, PEER_TILE, D_MODEL).transpose(0, 2, 1)
    xp = block_tail(x_prompt, *mix_p, w_out[l], ln_g[l], ln_b[l], w_pq[l], sub_keys[l], u_bf, vt_bf, 512)
    xs = block_tail(x_sample, *mix_s, w_out[l], ln_g[l], ln_b[l], w_pq[l], sub_keys[l], u_bf, vt_bf, 128)
    return (xp, xs) + tuple(a[None] for a in st_p) + tuple(a[None] for a in st_s)
```

```python
import functools

import jax
import jax.numpy as jnp
import numpy as np
from jax import lax
from jax.experimental import pallas as pl
from jax.experimental.pallas import tpu as pltpu

D_MODEL = 1024
DEPTH = 1
PAGE_SIZE = 128
NSA_HEADS = 8
NSA_KV_HEADS = 2
NSA_GROUP = NSA_HEADS // NSA_KV_HEADS
NSA_HD = 64
NSA_QW = NSA_HEADS * NSA_HD
NSA_KVW = NSA_KV_HEADS * NSA_HD
CMP_BLOCK = 32
CMP_STRIDE = 16
SEL_BLOCK = 64
SEL_TOP = 16
WINDOW = 512
Q_BLOCK = 64
ATTN_SCALE = NSA_HD ** -0.5
ROPE_THETA = 10000.0
M_HEADS = 4
M_HD = 128
M_W = M_HEADS * M_HD
M_CHUNK = 64
CONV_W = 4
PEER_HEADS = 8
N_KEYS = 128
PEER_TOPK = 16
PEER_QDIM = 256
PEER_BLOCK = 128
IN_SPLITS = (NSA_QW, 6 * NSA_KVW, 3 * NSA_HEADS, 2 * M_W, M_W, M_W, 2 * M_HEADS)
LN_EPS = 1e-5
ALPHA = (2 * DEPTH) ** 0.25

VMEM_LIMIT_BYTES = 56 * 1024 * 1024


def _mm_kernel(x_ref, w_ref, o_ref):
    o_ref[...] = jnp.dot(x_ref[...].astype(jnp.bfloat16), w_ref[...], preferred_element_type=jnp.float32)


def pallas_matmul(x, w, tm=512):
    M, K = x.shape
    N = w.shape[1]
    tm = min(tm, M)
    assert M % tm == 0
    return pl.pallas_call(
        _mm_kernel,
        out_shape=jax.ShapeDtypeStruct((M, N), jnp.float32),
        grid=(M // tm,),
        in_specs=[pl.BlockSpec((tm, K), lambda i: (i, 0)), pl.BlockSpec((K, N), lambda i: (0, 0))],
        out_specs=pl.BlockSpec((tm, N), lambda i: (i, 0)),
        compiler_params=pltpu.CompilerParams(dimension_semantics=("arbitrary",),
                                             vmem_limit_bytes=VMEM_LIMIT_BYTES),
        name="proj_matmul",
    )(x, w.astype(jnp.bfloat16))


def mm3(x, w):
    lead = x.shape[:-1]
    return pallas_matmul(x.reshape(-1, x.shape[-1]), w).reshape(*lead, w.shape[1])


def layer_norm(x, g, b):
    mu = x.mean(-1, keepdims=True)
    var = jnp.square(x - mu).mean(-1, keepdims=True)
    return (x - mu) * lax.rsqrt(var + LN_EPS) * g + b


def rope(x, pos):
    half = x.shape[-1] // 2
    inv = ROPE_THETA ** (-jnp.arange(half, dtype=jnp.float32) / half)
    ang = pos.astype(jnp.float32)[:, None] * inv[None, :]
    cos = jnp.cos(ang)[:, None, :]
    sin = jnp.sin(ang)[:, None, :]
    x1, x2 = x[..., :half], x[..., half:]
    return jnp.concatenate([x1 * cos - x2 * sin, x2 * cos + x1 * sin], axis=-1)


def split_in_proj(x, w_in):
    z = mm3(x, w_in)
    cuts = [int(c) for c in np.cumsum(IN_SPLITS)[:-1]]
    return jnp.split(z, cuts, axis=-1)


_IN_OFF = np.concatenate([[0], np.cumsum(IN_SPLITS)])
_IN_ORDER = (0, 1, 3, 4, 5, 2, 6)
_N_KV_ROWS = 6
_KV_BF16 = (2, 3, 4, 5)
_KV_CACHE = (0, 1, 2, 3, 4, 5)
GATE_W = IN_SPLITS[2] + IN_SPLITS[6]


def _rope_pairs(x, cos, sin_signed):
    half = NSA_HD // 2
    lane = lax.broadcasted_iota(jnp.int32, x.shape, 1)
    partner = jnp.where(lane % NSA_HD < half, pltpu.roll(x, LANES - half, 1), pltpu.roll(x, half, 1))
    return x * cos + partner * sin_signed


def _in_proj_kernel(x_ref, w_ref, cos_ref, sin_ref, q_ref, *rest, kv_major):
    kv_refs = rest[:_N_KV_ROWS]
    rest = rest[_N_KV_ROWS:]
    if kv_major:
        bf_refs, rest = rest[:len(_KV_BF16)], rest[len(_KV_BF16):]
        cache_refs, rest = rest[:len(_KV_CACHE)], rest[len(_KV_CACHE):]
    zqk_ref, zv_ref, zo_ref, zgate_ref = rest
    z = jnp.dot(x_ref[...].astype(jnp.bfloat16), w_ref[...], preferred_element_type=jnp.float32)
    cos = cos_ref[...]
    sin = sin_ref[...]
    for g in range(NSA_QW // LANES):
        sl = slice(g * LANES, (g + 1) * LANES)
        q_ref[:, sl] = (_rope_pairs(z[:, sl], cos, sin) * ATTN_SCALE).astype(jnp.bfloat16)
    for r in range(_N_KV_ROWS):
        row = z[:, NSA_QW + r * NSA_KVW:NSA_QW + (r + 1) * NSA_KVW]
        if r % 2 == 0:
            row = _rope_pairs(row, cos, sin)
        kv_refs[r][...] = row
        if kv_major and r in _KV_CACHE:
            dst = cache_refs[_KV_CACHE.index(r)]
            dst[0] = row.T
        if kv_major and r in _KV_BF16:
            dst = bf_refs[_KV_BF16.index(r)]
            for n in range(NSA_KV_HEADS):
                dst[0, n] = row[:, n * NSA_HD:(n + 1) * NSA_HD].astype(jnp.bfloat16)
    o = NSA_QW + _N_KV_ROWS * NSA_KVW
    zqk_ref[...] = z[:, o:o + 2 * M_W]
    zv_ref[...] = z[:, o + 2 * M_W:o + 3 * M_W]
    zo_ref[...] = z[:, o + 3 * M_W:o + 4 * M_W]
    zgate_ref[...] = z[:, o + 4 * M_W:o + 4 * M_W + GATE_W]


def in_proj_fused(x, w_in, pos, tm, kv_major):
    B, T, D = x.shape
    M = B * T
    assert M % tm == 0 and NSA_KVW == LANES and (not kv_major or T % tm == 0)
    f32, bf16 = jnp.float32, jnp.bfloat16
    w = jnp.concatenate([w_in[:, _IN_OFF[i]:_IN_OFF[i + 1]] for i in _IN_ORDER], axis=1).astype(bf16)
    half = NSA_HD // 2
    inv = ROPE_THETA ** (-jnp.arange(half, dtype=f32) / half)
    ang = pos.astype(f32)[:, None] * inv[None, :]
    cos = jnp.tile(jnp.cos(ang), (B, 2 * LANES // NSA_HD))
    sin = jnp.tile(jnp.concatenate([-jnp.sin(ang), jnp.sin(ang)], axis=1), (B, LANES // NSA_HD))
    n_w = w.shape[1]

    def rows(width):
        return pl.BlockSpec((tm, width), lambda i: (i, 0))

    out_shape = [jax.ShapeDtypeStruct((M, NSA_QW), bf16)] + [jax.ShapeDtypeStruct((M, NSA_KVW), f32)] * _N_KV_ROWS
    out_specs = [rows(NSA_QW)] + [rows(NSA_KVW)] * _N_KV_ROWS
    if kv_major:
        per_seq = T // tm
        out_shape += [jax.ShapeDtypeStruct((B, NSA_KV_HEADS, T, NSA_HD), bf16)] * len(_KV_BF16)
        out_specs += [pl.BlockSpec((1, NSA_KV_HEADS, tm, NSA_HD),
                                   lambda i: (i // per_seq, 0, i % per_seq, 0))] * len(_KV_BF16)
        out_shape += [jax.ShapeDtypeStruct((B, NSA_KVW, T), f32)] * len(_KV_CACHE)
        out_specs += [pl.BlockSpec((1, NSA_KVW, tm), lambda i: (i // per_seq, 0, i % per_seq))] * len(_KV_CACHE)
    out_shape += [jax.ShapeDtypeStruct((M, 2 * M_W), f32), jax.ShapeDtypeStruct((M, M_W), f32),
                  jax.ShapeDtypeStruct((M, M_W), f32), jax.ShapeDtypeStruct((M, GATE_W), f32)]
    out_specs += [rows(2 * M_W), rows(M_W), rows(M_W), rows(GATE_W)]
    outs = pl.pallas_call(
        functools.partial(_in_proj_kernel, kv_major=kv_major),
        out_shape=tuple(out_shape),
        grid=(M // tm,),
        in_specs=[rows(D), pl.BlockSpec((D, n_w), lambda i: (0, 0)), rows(LANES), rows(LANES)],
        out_specs=tuple(out_specs),
        compiler_params=pltpu.CompilerParams(dimension_semantics=("arbitrary",),
                                             vmem_limit_bytes=VMEM_LIMIT_BYTES),
        name="in_proj",
    )(x.reshape(M, D), w, cos, sin)
    names = ["q", "k_cmp", "v_cmp", "k_slc", "v_slc", "k_win", "v_win"]
    if kv_major:
        names += ["k_slc_bf", "v_slc_bf", "k_win_bf", "v_win_bf"]
        names += ["k_cmp_cache", "v_cmp_cache", "k_slc_cache", "v_slc_cache", "k_win_cache", "v_win_cache"]
    names += ["zqk", "zv", "zo", "zgate"]
    return dict(zip(names, outs))


def nsa_project(zq, zkv, zg, pos):
    B, T, _ = zq.shape
    q = rope(zq.reshape(B, T, NSA_HEADS, NSA_HD), pos)
    kv = zkv.reshape(B, T, 6, NSA_KV_HEADS, NSA_HD)
    rows = (rope(kv[:, :, 0], pos), kv[:, :, 1], rope(kv[:, :, 2], pos), kv[:, :, 3],
            rope(kv[:, :, 4], pos), kv[:, :, 5])
    gates = jax.nn.sigmoid(zg).reshape(B, T, NSA_HEADS, 3)
    return q, rows, gates


def _expanded_w1(w1):
    assert CMP_BLOCK == 2 * CMP_STRIDE
    w1r = w1.reshape(2, CMP_STRIDE, NSA_HD, w1.shape[-1])
    wbig = jnp.einsum('hpdf,kn->pkdnhf', w1r, jnp.eye(NSA_KV_HEADS, dtype=w1.dtype))
    return wbig.reshape(CMP_STRIDE * NSA_KVW, 2 * NSA_KV_HEADS * w1.shape[-1])


def _compress_rows(x, w_ref, bias_ref, w2_ref, o_ref):
    f32, bf16 = jnp.float32, jnp.bfloat16
    rows = x.shape[0]
    f = w2_ref.shape[0]
    proj = jnp.dot(x.astype(bf16), w_ref[...], preferred_element_type=f32)
    for n in range(NSA_KV_HEADS):
        first = proj[:, 2 * n * f:(2 * n + 1) * f]
        second = pltpu.roll(proj[:, (2 * n + 1) * f:(2 * n + 2) * f], rows - 1, 0)
        pre = first + second + bias_ref[...]
        hid = 0.5 * pre * (1.0 + lax.erf(pre * (2.0 ** -0.5)))
        o_ref[:, n * NSA_HD:(n + 1) * NSA_HD] = jnp.dot(hid.astype(bf16), w2_ref[...], preferred_element_type=f32)


def _compress_chunks_kernel(x_ref, w_ref, bias_ref, w2_ref, o_ref):
    _compress_rows(x_ref[...], w_ref, bias_ref, w2_ref, o_ref)


def _compress_weights(pe, w1, b1, w2):
    bf16 = jnp.bfloat16
    bias = jnp.dot(pe.reshape(-1), w1, precision=lax.Precision.HIGHEST) + b1
    return _expanded_w1(w1).astype(bf16), bias[None], w2.astype(bf16)


def compress_chunks(rows, per_seq, pe, w1, b1, w2, tm=512):
    chunks = rows.reshape(-1, CMP_STRIDE * NSA_KVW)
    n = chunks.shape[0]
    tm = min(tm, n)
    assert n % tm == 0 and tm % per_seq == 0
    wbig, bias, w2b = _compress_weights(pe, w1, b1, w2)

    def whole(a):
        return pl.BlockSpec(a.shape, lambda i: (0, 0))

    out = pl.pallas_call(
        _compress_chunks_kernel,
        out_shape=jax.ShapeDtypeStruct((n, NSA_KVW), jnp.float32),
        grid=(n // tm,),
        in_specs=[pl.BlockSpec((tm, chunks.shape[1]), lambda i: (i, 0)), whole(wbig), whole(bias), whole(w2b)],
        out_specs=pl.BlockSpec((tm, NSA_KVW), lambda i: (i, 0)),
        compiler_params=pltpu.CompilerParams(dimension_semantics=("arbitrary",),
                                             vmem_limit_bytes=VMEM_LIMIT_BYTES),
        name="compress_chunks",
    )(chunks, wbig, bias, w2b)
    return out.reshape(n // per_seq, per_seq, NSA_KV_HEADS, NSA_HD)[:, :-1]


PAGE_GROUP = 4


def _compress_pages_kernel(pg_ref, w_ref, bias_ref, w2_ref, o_ref, x_ref, t_ref):
    n_pages = pg_ref.shape[1]
    per_page = PAGE_SIZE // CMP_STRIDE
    group = PAGE_GROUP

    def place(i, carry):
        for u in range(group):
            g = i * group + u
            t_ref[u] = pg_ref[0, g].reshape(NSA_KVW, PAGE_SIZE).T
            row0 = pl.multiple_of(g * per_page, per_page)
            for p in range(CMP_STRIDE):
                x_ref[pl.ds(row0, per_page), p * NSA_KVW:(p + 1) * NSA_KVW] = (
                    t_ref.at[u][pl.ds(p, per_page, stride=CMP_STRIDE), :])
        return carry

    lax.fori_loop(0, n_pages // group, place, 0)
    _compress_rows(x_ref[...], w_ref, bias_ref, w2_ref, o_ref.at[0])


def compress_pages(pages, pe, w1, b1, w2):
    B, n_pages = pages.shape[:2]
    assert pages.shape[2:] == (NSA_KV_HEADS, NSA_HD, PAGE_SIZE) and NSA_KVW == LANES and PAGE_SIZE == LANES
    assert n_pages % PAGE_GROUP == 0
    wbig, bias, w2b = _compress_weights(pe, w1, b1, w2)
    rows = n_pages * (PAGE_SIZE // CMP_STRIDE)

    def whole(a):
        return pl.BlockSpec(a.shape, lambda b: (0, 0))

    out = pl.pallas_call(
        _compress_pages_kernel,
        out_shape=jax.ShapeDtypeStruct((B, rows, NSA_KVW), jnp.float32),
        grid=(B,),
        in_specs=[pl.BlockSpec((1, n_pages, NSA_KV_HEADS, NSA_HD, PAGE_SIZE), lambda b: (b, 0, 0, 0, 0)),
                  whole(wbig), whole(bias), whole(w2b)],
        out_specs=pl.BlockSpec((1, rows, NSA_KVW), lambda b: (b, 0, 0)),
        scratch_shapes=[pltpu.VMEM((rows, wbig.shape[0]), jnp.float32),
                        pltpu.VMEM((PAGE_GROUP, PAGE_SIZE, NSA_KVW), jnp.float32)],
        compiler_params=pltpu.CompilerParams(dimension_semantics=("arbitrary",),
                                             vmem_limit_bytes=VMEM_LIMIT_BYTES),
        name="compress_pages",
    )(pages, wbig, bias, w2b)
    return out.reshape(B, rows, NSA_KV_HEADS, NSA_HD)[:, :-1]


def cmp_attend(q, qpos, kc, vc):
    B, T = q.shape[:2]
    qg = q.reshape(B, T, NSA_KV_HEADS, NSA_GROUP, NSA_HD)
    s = jnp.einsum('btngd,bcnd->btngc', qg, kc) * ATTN_SCALE
    nblk = kc.shape[1]
    blk_end = jnp.arange(nblk) * CMP_STRIDE + CMP_BLOCK - 1
    valid = (blk_end[None, :] <= qpos[:, None])[None, :, None, None, :]
    p = jax.nn.softmax(jnp.where(valid, s, -1e30), axis=-1) * valid
    o = jnp.einsum('btngc,bcnd->btngd', p, vc)
    return o.reshape(B, T, NSA_HEADS, NSA_HD), p


def select_blocks(p, qpos, n_sel):
    imp = p.sum(axis=3)
    R = SEL_BLOCK // CMP_STRIDE
    r = CMP_BLOCK // CMP_STRIDE
    nb = imp.shape[-1]
    right = n_sel * R + R - 1 - nb
    padded = jnp.pad(imp, ((0, 0), (0, 0), (0, 0), (r - 1, right)))
    score = padded[..., 0:(n_sel - 1) * R + 1:R]
    for o in range(1, R + r - 1):
        score = score + padded[..., o:o + (n_sel - 1) * R + 1:R]
    j = jnp.arange(n_sel)[None, :]
    cur = (qpos // SEL_BLOCK)[:, None]
    valid = (j * SEL_BLOCK <= qpos[:, None])[None, :, None, :]
    forced = ((j == 0) | (j == cur) | (j == cur - 1))[None, :, None, :]
    score = jnp.where(forced, jnp.inf, jnp.where(valid, score, -jnp.inf))
    idx = j[0]
    before = (score[..., None, :] > score[..., :, None]) | ((score[..., None, :] == score[..., :, None])
                                                          & (idx[None, :] < idx[:, None]))
    return before.sum(-1) < min(SEL_TOP, n_sel)


def sample_selected_attention(q, qpos, member, k_pool, v_pool, k_new, v_new, page_table):
    B, T = q.shape[:2]
    n_pages = page_table.shape[1]
    per_page = PAGE_SIZE // SEL_BLOCK
    assert member.shape[-1] == n_pages * per_page + 1 and T <= SEL_BLOCK
    kp = k_pool.transpose(0, 2, 3, 1)[page_table]
    vp = v_pool.transpose(0, 2, 3, 1)[page_table]
    qg = q.reshape(B, T, NSA_KV_HEADS, NSA_GROUP, NSA_HD)
    s_past = jnp.einsum('btngd,bpndk->bntgpk', qg, kp) * ATTN_SCALE
    s_new = jnp.einsum('btngd,bsnd->bntgs', qg, k_new) * ATTN_SCALE
    m = member.transpose(0, 2, 1, 3)
    m_past = jnp.repeat(m[..., :-1].reshape(B, NSA_KV_HEADS, T, n_pages, per_page), SEL_BLOCK, axis=-1)
    kpos = (jnp.arange(n_pages) * PAGE_SIZE)[:, None] + jnp.arange(PAGE_SIZE)[None, :]
    m_past = m_past & (kpos[None, None, None] <= qpos[None, None, :, None, None])
    new_pos = n_pages * PAGE_SIZE + jnp.arange(T)
    m_new = m[..., -1:] & (new_pos[None, None, None, :] <= qpos[None, None, :, None])
    logits = jnp.concatenate(
        [jnp.where(m_past[:, :, :, None], s_past, -jnp.inf).reshape(B, NSA_KV_HEADS, T, NSA_GROUP, -1),
         jnp.where(m_new[:, :, :, None], s_new, -jnp.inf)], axis=-1)
    pr = jax.nn.softmax(logits, axis=-1)
    pr_past = pr[..., :n_pages * PAGE_SIZE].reshape(B, NSA_KV_HEADS, T, NSA_GROUP, n_pages, PAGE_SIZE)
    o = (jnp.einsum('bntgpk,bpndk->bntgd', pr_past, vp)
         + jnp.einsum('bntgs,bsnd->bntgd', pr[..., n_pages * PAGE_SIZE:], v_new))
    return o.transpose(0, 2, 1, 3, 4).reshape(B, T, NSA_HEADS, NSA_HD)


def to_blocks(rows, n_sel):
    B, L, KV, hd = rows.shape
    rows = jnp.pad(rows, ((0, 0), (0, n_sel * SEL_BLOCK - L), (0, 0), (0, 0)))
    return rows.reshape(B, n_sel, SEL_BLOCK, KV, hd).transpose(0, 3, 1, 2, 4)


def take_rows(table, idx):
    return table[idx]


def sel_attend(q, qpos, sel, kb, vb):
    B, Tq = q.shape[:2]
    k = sel.shape[-1]
    sel_t = sel.transpose(0, 2, 1, 3)
    gather = jax.vmap(jax.vmap(take_rows))
    kg = gather(kb, sel_t).reshape(B, NSA_KV_HEADS, Tq, k * SEL_BLOCK, NSA_HD)
    vg = gather(vb, sel_t).reshape(B, NSA_KV_HEADS, Tq, k * SEL_BLOCK, NSA_HD)
    kpos = (sel_t[..., None] * SEL_BLOCK + jnp.arange(SEL_BLOCK)).reshape(B, NSA_KV_HEADS, Tq, k * SEL_BLOCK)
    qg = q.reshape(B, Tq, NSA_KV_HEADS, NSA_GROUP, NSA_HD).transpose(0, 2, 1, 3, 4)
    s = jnp.einsum('bntgd,bntsd->bntgs', qg, kg) * ATTN_SCALE
    mask = kpos[:, :, :, None, :] <= qpos[None, None, :, None, None]
    pr = jax.nn.softmax(jnp.where(mask, s, -jnp.inf), axis=-1)
    o = jnp.einsum('bntgs,bntsd->bntgd', pr, vg)
    return o.transpose(0, 2, 1, 3, 4).reshape(B, Tq, NSA_HEADS, NSA_HD)


def win_attend(q, qpos, k, v, kpos):
    B, Tq = q.shape[:2]
    qg = q.reshape(B, Tq, NSA_KV_HEADS, NSA_GROUP, NSA_HD)
    s = jnp.einsum('btngd,bsnd->btngs', qg, k) * ATTN_SCALE
    diff = qpos[:, None] - kpos[None, :]
    mask = ((diff >= 0) & (diff < WINDOW) & (kpos[None, :] >= 0))[None, :, None, None, :]
    pr = jax.nn.softmax(jnp.where(mask, s, -jnp.inf), axis=-1)
    o = jnp.einsum('btngs,bsnd->btngd', pr, v)
    return o.reshape(B, Tq, NSA_HEADS, NSA_HD)


def nsa_combine(gates, o_cmp, o_sel, o_win):
    B, T = gates.shape[:2]
    o = gates[..., 0:1] * o_cmp + gates[..., 1:2] * o_sel + gates[..., 2:3] * o_win
    return o.reshape(B, T, NSA_QW)


NSA_TQ = 128
NSA_CK = 512
MASKED = -1e30


def _softmax_rows(s):
    m = jnp.max(s, axis=-1, keepdims=True)
    e = jnp.exp(s - m)
    return e / jnp.sum(e, axis=-1, keepdims=True)


def _nsa_prompt_kernel(q_ref, kc_ref, vc_ref, ks_ref, vs_ref, kw_ref, vw_ref, zg_ref, msel_ref, exp_ref, o_ref):
    f32, bf16 = jnp.float32, jnp.bfloat16
    tq = NSA_TQ
    q0 = pl.program_id(2) * tq
    qb = q_ref[0]
    qs = jnp.concatenate([qb[:, g * NSA_HD:(g + 1) * NSA_HD] for g in range(NSA_GROUP)], axis=0)
    tpos = q0 + lax.broadcasted_iota(jnp.int32, (tq, 1), 0)

    def per_head(a):
        return jnp.concatenate([a] * NSA_GROUP, axis=0)

    s = lax.dot_general(qs, kc_ref[0, 0], _NT, preferred_element_type=f32)
    cblk = lax.broadcasted_iota(jnp.int32, (tq, 128), 1)
    cvalid = cblk * CMP_STRIDE + (CMP_BLOCK - 1) <= tpos
    s = s + per_head(jnp.where(cvalid, 0.0, MASKED))
    e = jnp.exp(s - jnp.max(s, axis=-1, keepdims=True)) * per_head(jnp.where(cvalid, 1.0, 0.0))
    l = jnp.sum(e, axis=-1, keepdims=True)
    p = e / jnp.where(l > 0.0, l, 1.0)
    o_cmp = jnp.dot(p.astype(bf16), vc_ref[0, 0], preferred_element_type=f32)

    imp = p[0:tq]
    for g in range(1, NSA_GROUP):
        imp = imp + p[g * tq:(g + 1) * tq]
    hi = imp.astype(bf16)
    r1 = imp - hi.astype(f32)
    mid = r1.astype(bf16)
    lo = (r1 - mid.astype(f32)).astype(bf16)
    msel = msel_ref[...]
    score = (lax.dot_general(msel, hi, _NT, preferred_element_type=f32)
             + lax.dot_general(msel, mid, _NT, preferred_element_type=f32)
             + lax.dot_general(msel, lo, _NT, preferred_element_type=f32))
    n_sel = score.shape[0]
    j = lax.broadcasted_iota(jnp.int32, (n_sel, tq), 0)
    tok = q0 + lax.broadcasted_iota(jnp.int32, (n_sel, tq), 1)
    cur = tok // SEL_BLOCK
    forced = (j == 0) | (j == cur) | (j == cur - 1)
    score = jnp.where(forced, jnp.inf, jnp.where(j * SEL_BLOCK <= tok, score, -jnp.inf))
    rank = jnp.zeros((n_sel, tq), f32)
    for jp in range(n_sel):
        row = score[jp:jp + 1, :]
        before = (row > score) | ((row == score) & (j > jp))
        rank = rank + jnp.where(before, 1.0, 0.0)
    chosen_t = jnp.where(rank < SEL_TOP, 1.0, 0.0)
    sel01 = jnp.concatenate([chosen_t, jnp.zeros((LANES - n_sel, tq), f32)], axis=0).T.astype(bf16)

    ck = NSA_CK
    rows = NSA_GROUP * tq

    def sel_chunk(c, carry):
        m, l, acc = carry
        k0 = pl.multiple_of(c * ck, ck)
        s = lax.dot_general(qs, ks_ref[0, 0, pl.ds(k0, ck), :], _NT, preferred_element_type=f32)
        chosen = jnp.dot(sel01, exp_ref[c], preferred_element_type=f32)
        kpos = k0 + lax.broadcasted_iota(jnp.int32, (tq, ck), 1)
        ok = (chosen > 0.5) & (kpos <= tpos)
        s = s + per_head(jnp.where(ok, 0.0, MASKED))
        m_new = jnp.maximum(m, jnp.max(s, axis=-1, keepdims=True))
        a = jnp.exp(m - m_new)
        pr = jnp.exp(s - m_new)
        l = a * l + jnp.sum(pr, axis=-1, keepdims=True)
        acc = a * acc + jnp.dot(pr.astype(bf16), vs_ref[0, 0, pl.ds(k0, ck), :], preferred_element_type=f32)
        return m_new, l, acc

    init = (jnp.full((rows, 1), MASKED, f32), jnp.zeros((rows, 1), f32), jnp.zeros((rows, NSA_HD), f32))
    n_chunks = (q0 + tq + ck - 1) // ck
    _, l_sel, acc_sel = lax.fori_loop(0, n_chunks, sel_chunk, init)
    o_sel = acc_sel / l_sel

    w0 = pl.multiple_of(jnp.maximum(q0 - WINDOW, 0), tq)
    wl = WINDOW + tq
    s = lax.dot_general(qs, kw_ref[0, 0, pl.ds(w0, wl), :], _NT, preferred_element_type=f32)
    diff = tpos - (w0 + lax.broadcasted_iota(jnp.int32, (tq, wl), 1))
    s = s + per_head(jnp.where((diff >= 0) & (diff < WINDOW), 0.0, MASKED))
    o_win = jnp.dot(_softmax_rows(s).astype(bf16), vw_ref[0, 0, pl.ds(w0, wl), :], preferred_element_type=f32)

    gates = jax.nn.sigmoid(zg_ref[0, 0])
    for g in range(NSA_GROUP):
        r = slice(g * tq, (g + 1) * tq)
        o_ref[0, :, g * NSA_HD:(g + 1) * NSA_HD] = (gates[:, 3 * g:3 * g + 1] * o_cmp[r]
                                                    + gates[:, 3 * g + 1:3 * g + 2] * o_sel[r]
                                                    + gates[:, 3 * g + 2:3 * g + 3] * o_win[r])


def nsa_prompt_attention(qs, kc, vc, k_slc, v_slc, k_win, v_win, zg):
    B, S = qs.shape[:2]
    bf16 = jnp.bfloat16
    assert S % NSA_CK == 0 and S % NSA_TQ == 0 and WINDOW % NSA_TQ == 0 and WINDOW + NSA_TQ <= S
    n_sel = S // SEL_BLOCK
    nb = kc.shape[1]
    assert nb <= 128

    def pad_blocks(a):
        return jnp.pad(a.transpose(0, 2, 1, 3).astype(bf16), ((0, 0), (0, 0), (0, 128 - nb), (0, 0)))

    zg4 = zg.reshape(B, S, NSA_KV_HEADS, 3 * NSA_GROUP).transpose(0, 2, 1, 3)
    c = np.arange(128)[:, None]
    jj = np.arange(n_sel)[None, :]
    ratio = SEL_BLOCK // CMP_STRIDE
    msel = ((c >= jj * ratio - (CMP_BLOCK // CMP_STRIDE - 1)) & (c <= jj * ratio + ratio - 1) & (c < nb))
    assert n_sel <= LANES and NSA_TQ == LANES
    expand = (np.arange(S)[None, :] // SEL_BLOCK == np.arange(LANES)[:, None])
    expand = expand.reshape(LANES, S // NSA_CK, NSA_CK).transpose(1, 0, 2)
    row_spec = pl.BlockSpec((1, 1, S, NSA_HD), lambda b, n, i: (b, n, 0, 0))
    blk_spec = pl.BlockSpec((1, 1, 128, NSA_HD), lambda b, n, i: (b, n, 0, 0))
    return pl.pallas_call(
        _nsa_prompt_kernel,
        out_shape=jax.ShapeDtypeStruct((B, S, NSA_QW), jnp.float32),
        grid=(B, NSA_KV_HEADS, S // NSA_TQ),
        in_specs=[pl.BlockSpec((1, NSA_TQ, NSA_GROUP * NSA_HD), lambda b, n, i: (b, i, n)),
                  blk_spec, blk_spec, row_spec, row_spec, row_spec, row_spec,
                  pl.BlockSpec((1, 1, NSA_TQ, 3 * NSA_GROUP), lambda b, n, i: (b, n, i, 0)),
                  pl.BlockSpec((n_sel, 128), lambda b, n, i: (0, 0)),
                  pl.BlockSpec((S // NSA_CK, LANES, NSA_CK), lambda b, n, i: (0, 0, 0))],
        out_specs=pl.BlockSpec((1, NSA_TQ, NSA_GROUP * NSA_HD), lambda b, n, i: (b, i, n)),
        compiler_params=pltpu.CompilerParams(dimension_semantics=("arbitrary", "arbitrary", "arbitrary"),
                                             vmem_limit_bytes=VMEM_LIMIT_BYTES),
        name="nsa_prompt_attention",
    )(qs, pad_blocks(kc), pad_blocks(vc), k_slc, v_slc, k_win, v_win,
      zg4, jnp.asarray(msel.T, bf16), jnp.asarray(expand, bf16))


MLSTM_L = 128
CONV_HALO = 8


def _log_sigmoid(x):
    return -(jnp.maximum(-x, 0.0) + jnp.log1p(jnp.exp(-jnp.abs(x))))


def _mlstm_prompt_kernel(x_ref, xprev_ref, halo0_ref, v_ref, o_ref, gcol_ref, grow_ref, cw_ref, cb_ref,
                         out_ref, c_out, n_out, m_out, c_ref, n_ref, m_ref):
    f32, bf16 = jnp.float32, jnp.bfloat16
    c = pl.program_id(1)
    L = MLSTM_L

    @pl.when(c == 0)
    def _():
        c_ref[...] = jnp.zeros_like(c_ref)
        n_ref[...] = jnp.zeros_like(n_ref)
        m_ref[...] = jnp.zeros_like(m_ref)

    x = x_ref[0]
    halo = jnp.where(c == 0, halo0_ref[0], xprev_ref[0, L - CONV_HALO:L, :])
    ext = jnp.concatenate([halo, x], axis=0)
    conv = cb_ref[...]
    for j in range(CONV_W):
        o = CONV_HALO - (CONV_W - 1) + j
        conv = conv + ext[o:o + L] * cw_ref[j:j + 1, :]
    qk = conv * jax.nn.sigmoid(conv)

    t_id = lax.broadcasted_iota(jnp.int32, (L, L), 0)
    s_id = lax.broadcasted_iota(jnp.int32, (L, L), 1)
    causal = t_id >= s_id
    gcol = gcol_ref[0, 0]
    grow = grow_ref[0, 0]
    for h in range(M_HEADS):
        hd = slice(h * M_HD, (h + 1) * M_HD)
        q = qk[:, hd]
        k = qk[:, M_W + h * M_HD:M_W + (h + 1) * M_HD] * (M_HD ** -0.5)
        v = v_ref[0, :, hd].astype(bf16)
        ig_r = grow[h:h + 1, :]
        ig_c = gcol[:, h:h + 1]
        lf_r = _log_sigmoid(grow[M_HEADS + h:M_HEADS + h + 1, :])
        lf_c = _log_sigmoid(gcol[:, M_HEADS + h:M_HEADS + h + 1])
        b_c = jnp.sum(jnp.where(causal, lf_r, 0.0), axis=1, keepdims=True)
        b_r = jnp.sum(jnp.where(t_id <= s_id, lf_c, 0.0), axis=0, keepdims=True)
        m_prev = m_ref[h]
        dmat = jnp.where(causal, b_c - b_r + ig_r, -jnp.inf)
        inter = b_c + m_prev
        m_t = jnp.maximum(inter, jnp.max(dmat, axis=1, keepdims=True))
        w_intra = jnp.exp(dmat - m_t)
        w_inter = jnp.exp(inter - m_t)
        qb = q.astype(bf16)
        s = lax.dot_general(qb, k.astype(bf16), _NT, preferred_element_type=f32) * w_intra
        num = (jnp.dot(s.astype(bf16), v, preferred_element_type=f32)
               + w_inter * jnp.dot(qb, c_ref[h].astype(bf16), preferred_element_type=f32))
        den = jnp.sum(s, axis=1, keepdims=True) + w_inter * jnp.sum(q * n_ref[h], axis=1, keepdims=True)
        hh = num / jnp.maximum(jnp.abs(den), jnp.exp(-m_t))
        out_ref[0, :, hd] = jax.nn.sigmoid(o_ref[0, :, hd]) * hh
        m_new = m_t[L - 1:L]
        b_last = b_c[L - 1:L]
        w_s = jnp.exp(b_last - b_c + ig_c - m_new)
        w_p = jnp.exp(b_last + m_prev - m_new)
        kw = k * w_s
        c_ref[h] = w_p * c_ref[h] + jnp.dot(kw.T.astype(bf16), v, preferred_element_type=f32)
        n_ref[h] = w_p * n_ref[h] + jnp.sum(kw, axis=0, keepdims=True)
        m_ref[h] = m_new

    @pl.when(c == pl.num_programs(1) - 1)
    def _():
        c_out[0] = c_ref[...]
        n_out[0] = n_ref[...]
        m_out[0] = m_ref[...]


def mlstm_prompt(zqk, zv, zo, zif, conv_w, conv_b, b_if):
    B, T, _ = zqk.shape
    L = MLSTM_L
    assert T % L == 0
    nc = T // L
    f32 = jnp.float32
    gif = zif + b_if
    gcol = gif.reshape(B, nc, L, 2 * M_HEADS)
    grow = gcol.transpose(0, 1, 3, 2)
    halo0 = jnp.zeros((B, CONV_HALO, 2 * M_W), f32)
    out, C, n, m = pl.pallas_call(
        _mlstm_prompt_kernel,
        out_shape=(jax.ShapeDtypeStruct((B, T, M_W), f32),
                   jax.ShapeDtypeStruct((B, M_HEADS, M_HD, M_HD), f32),
                   jax.ShapeDtypeStruct((B, M_HEADS, 1, M_HD), f32),
                   jax.ShapeDtypeStruct((B, M_HEADS, 1, 1), f32)),
        grid=(B, nc),
        in_specs=[pl.BlockSpec((1, L, 2 * M_W), lambda b, c: (b, c, 0)),
                  pl.BlockSpec((1, L, 2 * M_W), lambda b, c: (b, jnp.maximum(c - 1, 0), 0)),
                  pl.BlockSpec((1, CONV_HALO, 2 * M_W), lambda b, c: (b, 0, 0)),
                  pl.BlockSpec((1, L, M_W), lambda b, c: (b, c, 0)),
                  pl.BlockSpec((1, L, M_W), lambda b, c: (b, c, 0)),
                  pl.BlockSpec((1, 1, L, 2 * M_HEADS), lambda b, c: (b, c, 0, 0)),
                  pl.BlockSpec((1, 1, 2 * M_HEADS, L), lambda b, c: (b, c, 0, 0)),
                  pl.BlockSpec((CONV_W, 2 * M_W), lambda b, c: (0, 0)),
                  pl.BlockSpec((1, 2 * M_W), lambda b, c: (0, 0))],
        out_specs=(pl.BlockSpec((1, L, M_W), lambda b, c: (b, c, 0)),
                   pl.BlockSpec((1, M_HEADS, M_HD, M_HD), lambda b, c: (b, 0, 0, 0)),
                   pl.BlockSpec((1, M_HEADS, 1, M_HD), lambda b, c: (b, 0, 0, 0)),
                   pl.BlockSpec((1, M_HEADS, 1, 1), lambda b, c: (b, 0, 0, 0))),
        scratch_shapes=[pltpu.VMEM((M_HEADS, M_HD, M_HD), f32), pltpu.VMEM((M_HEADS, 1, M_HD), f32),
                        pltpu.VMEM((M_HEADS, 1, 1), f32)],
        compiler_params=pltpu.CompilerParams(dimension_semantics=("arbitrary", "arbitrary"),
                                             vmem_limit_bytes=VMEM_LIMIT_BYTES),
        name="mlstm_prompt",
    )(zqk, zqk, halo0, zv, zo, gcol, grow, conv_w, conv_b[None])
    return out, C, n.reshape(B, M_HEADS, M_HD), m.reshape(B, M_HEADS)


def mlstm_chunk(carry, inp):
    C, n, m = carry
    q, k, v, ig, lf = inp
    L = q.shape[2]
    b = jnp.cumsum(lf, axis=-1)
    causal = jnp.tril(jnp.ones((L, L), dtype=bool))
    dmat = jnp.where(causal, b[..., :, None] - b[..., None, :] + ig[..., None, :], -jnp.inf)
    inter = b + m[..., None]
    m_t = jnp.maximum(inter, dmat.max(axis=-1))
    w_intra = jnp.exp(dmat - m_t[..., None])
    w_inter = jnp.exp(inter - m_t)
    s = jnp.einsum('bhtd,bhsd->bhts', q, k) * w_intra
    num = jnp.einsum('bhts,bhsv->bhtv', s, v) + w_inter[..., None] * jnp.einsum('bhtd,bhdv->bhtv', q, C)
    den = s.sum(-1) + w_inter * jnp.einsum('bhtd,bhd->bht', q, n)
    h = num / jnp.maximum(jnp.abs(den), jnp.exp(-m_t))[..., None]
    m_new = m_t[..., -1]
    w_s = jnp.exp(b[..., -1:] - b + ig - m_new[..., None])
    w_p = jnp.exp(b[..., -1] + m - m_new)
    C_new = w_p[..., None, None] * C + jnp.einsum('bhs,bhsd,bhsv->bhdv', w_s, k, v)
    n_new = w_p[..., None] * n + jnp.einsum('bhs,bhsd->bhd', w_s, k)
    return (C_new, n_new, m_new), h


def mlstm_mix(zqk, zv, zo, zif, buf0, C0, n0, m0, conv_w, conv_b, b_if, chunk):
    B, T, _ = zqk.shape
    full = jnp.concatenate([buf0, zqk], axis=1)
    conv = conv_b
    for j in range(CONV_W):
        conv = conv + full[:, j:j + T] * conv_w[j]
    qk = jax.nn.silu(conv)

    def heads(a):
        return a.reshape(B, T, M_HEADS, M_HD).transpose(0, 2, 1, 3)

    q = heads(qk[..., :M_W])
    k = heads(qk[..., M_W:]) * (M_HD ** -0.5)
    v = heads(zv)
    gif = zif + b_if
    ig = gif[..., :M_HEADS].transpose(0, 2, 1)
    lf = jax.nn.log_sigmoid(gif[..., M_HEADS:]).transpose(0, 2, 1)
    nc = T // chunk

    def to_chunks(a):
        return jnp.moveaxis(a.reshape(B, M_HEADS, nc, chunk, *a.shape[3:]), 2, 0)

    (C, n, m), h = lax.scan(mlstm_chunk, (C0, n0, m0),
                            (to_chunks(q), to_chunks(k), to_chunks(v), to_chunks(ig), to_chunks(lf)))
    h = jnp.moveaxis(h, 0, 2).reshape(B, M_HEADS, T, M_HD).transpose(0, 2, 1, 3).reshape(B, T, M_W)
    out = jax.nn.sigmoid(zo) * h
    return out, (C, n, m, full[:, T:])


PEER_COMBOS = 2 * PEER_HEADS
PEER_KEY_ROWS = 8
PEER_TILE = PEER_KEY_ROWS * N_KEYS
PEER_TS_ROWS = 24
LANES = 128
_NT = (((1,), (1,)), ((), ()))


def _peer_topk_kernel(q_ref, keys_ref, s_ref, e0_ref, e1_ref, tau_ref, ts_ref):
    c = pl.program_id(1)
    tt = q_ref.shape[0]
    s = lax.dot_general(keys_ref[0], q_ref[...].astype(jnp.bfloat16), _NT,
                        preferred_element_type=jnp.float32)
    s_ref[c] = s
    key_id = lax.broadcasted_iota(jnp.int32, s.shape, 0)
    work = s
    rows = []
    for _ in range(PEER_TOPK + 1):
        m = jnp.max(work, axis=0, keepdims=True)
        first = jnp.min(jnp.where(work == m, key_id, N_KEYS), axis=0, keepdims=True)
        work = jnp.where(key_id == first, -jnp.inf, work)
        rows.append(m)
    rows.append(jnp.full((PEER_TS_ROWS - PEER_TOPK - 1, tt), -jnp.inf, jnp.float32))
    ts_ref[c] = jnp.concatenate(rows, axis=0)

    @pl.when(c == PEER_COMBOS - 1)
    def _():
        for h in range(PEER_HEADS):
            t0 = ts_ref[2 * h]
            t1 = ts_ref[2 * h + 1]
            pieces = [t0[0:1] + t1] + [t0[a:a + 1] + t1[0:8] for a in range(1, 8)] + [t0[8:24] + t1[0:1]]
            cand = jnp.concatenate(pieces, axis=0)
            top = t0[0:1] + t1[0:1]
            v16 = top
            v17 = top
            z = jnp.zeros_like(top)
            seen = jnp.zeros_like(top)
            for _ in range(PEER_TOPK + 1):
                m = jnp.max(cand, axis=0, keepdims=True)
                eq = cand == m
                cnt = jnp.sum(jnp.where(eq, 1.0, 0.0), axis=0, keepdims=True)
                active = seen < PEER_TOPK
                take = jnp.minimum(cnt, PEER_TOPK - seen)
                v16 = jnp.where(active, m, v16)
                v17 = jnp.where(seen < PEER_TOPK + 1, m, v17)
                z = z + jnp.where(active, take * jnp.exp(m - top), 0.0)
                seen = seen + cnt
                cand = jnp.where(eq, -jnp.inf, cand)
            tau_ref[h:h + 1, :] = 0.5 * v16 + 0.5 * v17
            e0_ref[h] = jnp.exp(s_ref[2 * h] - t0[0:1]) / z
            e1_ref[h] = jnp.exp(s_ref[2 * h + 1] - t1[0:1])


def peer_scores(q, sub_keys, tt):
    n = q.shape[0]
    assert n % tt == 0
    keys = sub_keys.reshape(PEER_COMBOS, N_KEYS, PEER_QDIM // 2).astype(jnp.bfloat16)
    f32 = jnp.float32
    return pl.pallas_call(
        _peer_topk_kernel,
        out_shape=(jax.ShapeDtypeStruct((PEER_COMBOS, N_KEYS, n), f32),
                   jax.ShapeDtypeStruct((PEER_HEADS, N_KEYS, n), f32),
                   jax.ShapeDtypeStruct((PEER_HEADS, N_KEYS, n), f32),
                   jax.ShapeDtypeStruct((PEER_HEADS, n), f32)),
        grid=(n // tt, PEER_COMBOS),
        in_specs=[pl.BlockSpec((tt, PEER_QDIM // 2), lambda i, c: (i, c)),
                  pl.BlockSpec((1, N_KEYS, PEER_QDIM // 2), lambda i, c: (c, 0, 0))],
        out_specs=(pl.BlockSpec((PEER_COMBOS, N_KEYS, tt), lambda i, c: (0, 0, i)),
                   pl.BlockSpec((PEER_HEADS, N_KEYS, tt), lambda i, c: (0, 0, i)),
                   pl.BlockSpec((PEER_HEADS, N_KEYS, tt), lambda i, c: (0, 0, i)),
                   pl.BlockSpec((PEER_HEADS, tt), lambda i, c: (0, i))),
        scratch_shapes=[pltpu.VMEM((PEER_COMBOS, PEER_TS_ROWS, tt), f32)],
        compiler_params=pltpu.CompilerParams(dimension_semantics=("arbitrary", "arbitrary"),
                                             vmem_limit_bytes=VMEM_LIMIT_BYTES),
        name="peer_topk",
    )(q, keys)


def _peer_dense_kernel(xt_ref, h_ref, u_ref, vt_ref, s0_ref, ez_ref, s_ref, e1_ref, tau_ref, g_ref, b_ref,
                       o_ref, acc_ref, a_ref, w_ref):
    e = pl.program_id(1)
    tt = xt_ref.shape[1]

    @pl.when(e == 0)
    def _():
        acc_ref[...] = jnp.zeros_like(acc_ref)

    a_ref[...] = jnp.dot(u_ref[...], xt_ref[...], preferred_element_type=jnp.float32)
    for r in range(PEER_KEY_ROWS):
        rows = slice(r * N_KEYS, (r + 1) * N_KEYS)
        for t in range(tt // LANES):
            tok = slice(t * LANES, (t + 1) * LANES)
            gate = jnp.zeros((N_KEYS, LANES), jnp.float32)
            for h in range(PEER_HEADS):
                need = tau_ref[h:h + 1, tok] - s0_ref[2 * h, r:r + 1, tok]
                picked = jnp.where(s_ref[2 * h + 1, :, tok] >= need, e1_ref[h, :, tok], 0.0)
                gate = gate + picked * ez_ref[h, r:r + 1, tok]
            ar = a_ref[rows, tok]
            act = 0.5 * ar * (1.0 + lax.erf(ar * (2.0 ** -0.5)))
            w_ref[rows, tok] = (gate * act).astype(jnp.bfloat16)
    acc_ref[...] += jnp.dot(vt_ref[0], w_ref[...], preferred_element_type=jnp.float32)

    @pl.when(e == pl.num_programs(1) - 1)
    def _():
        r = ALPHA * h_ref[...] + acc_ref[...].T
        mu = jnp.mean(r, axis=-1, keepdims=True)
        d = r - mu
        var = jnp.mean(d * d, axis=-1, keepdims=True)
        o_ref[...] = d * lax.rsqrt(var + LN_EPS) * g_ref[...] + b_ref[...]


def peer_tail(h, ht, q, sub_keys, u_bf, vt_bf, ln_g, ln_b, tt):
    n, d = h.shape
    s, e0z, e1, tau = peer_scores(q, sub_keys, tt)
    n_exp = u_bf.shape[0]
    return pl.pallas_call(
        _peer_dense_kernel,
        out_shape=jax.ShapeDtypeStruct((n, d), jnp.float32),
        grid=(n // tt, n_exp // PEER_TILE),
        in_specs=[pl.BlockSpec((d, tt), lambda i, e: (0, i)),
                  pl.BlockSpec((tt, d), lambda i, e: (i, 0)),
                  pl.BlockSpec((PEER_TILE, d), lambda i, e: (e, 0)),
                  pl.BlockSpec((1, d, PEER_TILE), lambda i, e: (e, 0, 0)),
                  pl.BlockSpec((PEER_COMBOS, PEER_KEY_ROWS, tt), lambda i, e: (0, e, i)),
                  pl.BlockSpec((PEER_HEADS, PEER_KEY_ROWS, tt), lambda i, e: (0, e, i)),
                  pl.BlockSpec((PEER_COMBOS, N_KEYS, tt), lambda i, e: (0, 0, i)),
                  pl.BlockSpec((PEER_HEADS, N_KEYS, tt), lambda i, e: (0, 0, i)),
                  pl.BlockSpec((PEER_HEADS, tt), lambda i, e: (0, i)),
                  pl.BlockSpec((1, d), lambda i, e: (0, 0)),
                  pl.BlockSpec((1, d), lambda i, e: (0, 0))],
        out_specs=pl.BlockSpec((tt, d), lambda i, e: (i, 0)),
        scratch_shapes=[pltpu.VMEM((d, tt), jnp.float32), pltpu.VMEM((PEER_TILE, tt), jnp.float32),
                        pltpu.VMEM((PEER_TILE, tt), jnp.bfloat16)],
        compiler_params=pltpu.CompilerParams(dimension_semantics=("arbitrary", "arbitrary"),
                                             vmem_limit_bytes=VMEM_LIMIT_BYTES),
        name="peer_dense",
    )(ht, h, u_bf, vt_bf, s, e0z, s, e1, tau, ln_g[None], ln_b[None])


def _out_proj_kernel(x_ref, nsa_ref, m_ref, wn_ref, wm_ref, g_ref, b_ref, wq_ref, h_ref, ht_ref, q_ref):
    f32, bf16 = jnp.float32, jnp.bfloat16
    r = (ALPHA * x_ref[...] + jnp.dot(nsa_ref[...].astype(bf16), wn_ref[...], preferred_element_type=f32)
         + jnp.dot(m_ref[...].astype(bf16), wm_ref[...], preferred_element_type=f32))
    mu = jnp.mean(r, axis=-1, keepdims=True)
    d = r - mu
    var = jnp.mean(d * d, axis=-1, keepdims=True)
    h = d * lax.rsqrt(var + LN_EPS) * g_ref[...] + b_ref[...]
    h_ref[...] = h
    ht_ref[...] = h.T.astype(bf16)
    q_ref[...] = jnp.dot(h.astype(bf16), wq_ref[...], preferred_element_type=f32)


def out_proj_fused(x, o_nsa, o_m, w_out, ln_g, ln_b, w_pq, tm):
    n, d = x.shape
    assert n % tm == 0
    bf16 = jnp.bfloat16
    nq = w_pq.shape[1]

    def rows(width):
        return pl.BlockSpec((tm, width), lambda i: (i, 0))

    def whole(a):
        return pl.BlockSpec(a.shape, lambda i: (0, 0))

    wn = w_out[:NSA_QW].astype(bf16)
    wm = w_out[NSA_QW:].astype(bf16)
    wq = w_pq.astype(bf16)
    g, b = ln_g[None], ln_b[None]
    return pl.pallas_call(
        _out_proj_kernel,
        out_shape=(jax.ShapeDtypeStruct((n, d), jnp.float32), jax.ShapeDtypeStruct((d, n), bf16),
                   jax.ShapeDtypeStruct((n, nq), jnp.float32)),
        grid=(n // tm,),
        in_specs=[rows(d), rows(NSA_QW), rows(M_W), whole(wn), whole(wm), whole(g), whole(b), whole(wq)],
        out_specs=(rows(d), pl.BlockSpec((d, tm), lambda i: (0, i)), rows(nq)),
        compiler_params=pltpu.CompilerParams(dimension_semantics=("arbitrary",),
                                             vmem_limit_bytes=VMEM_LIMIT_BYTES),
        name="out_proj",
    )(x, o_nsa, o_m, wn, wm, g, b, wq)


def block_tail(x, o_nsa, o_m, w_out, ln_g, ln_b, w_pq, sub_keys, u_bf, vt_bf, tt):
    lead = x.shape[:-1]
    h, ht, q = out_proj_fused(x.reshape(-1, D_MODEL), o_nsa.reshape(-1, NSA_QW), o_m.reshape(-1, M_W),
                              w_out, ln_g[0], ln_b[0], w_pq, tt)
    return peer_tail(h, ht, q, sub_keys, u_bf, vt_bf, ln_g[1], ln_b[1], tt).reshape(*lead, D_MODEL)


def prompt_mix(x, w_in, pe, w1, b1, w2, conv_w, conv_b, b_if):
    B, S, _ = x.shape
    z = in_proj_fused(x, w_in, jnp.arange(S), 512, True)
    kc = compress_chunks(z["k_cmp"], S // CMP_STRIDE, pe[0], w1[0], b1[0], w2[0])
    vc = compress_chunks(z["v_cmp"], S // CMP_STRIDE, pe[1], w1[1], b1[1], w2[1])
    zgate = z["zgate"].reshape(B, S, GATE_W)
    o_nsa = nsa_prompt_attention(z["q"].reshape(B, S, NSA_QW), kc, vc, z["k_slc_bf"], z["v_slc_bf"],
                                 z["k_win_bf"], z["v_win_bf"], zgate[..., :IN_SPLITS[2]])
    zqk = z["zqk"].reshape(B, S, 2 * M_W)
    o_m, C, n, m = mlstm_prompt(zqk, z["zv"].reshape(B, S, M_W), z["zo"].reshape(B, S, M_W),
                                zgate[..., IN_SPLITS[2]:], conv_w, conv_b, b_if)
    buf = zqk[:, S - (CONV_W - 1):]
    wl = min(WINDOW, S)
    k_cmp, v_cmp, k_slc, v_slc, k_win, v_win = [
        z[k + "_cache"].reshape(B, NSA_KV_HEADS, NSA_HD, S).transpose(0, 3, 1, 2)
        for k in ("k_cmp", "v_cmp", "k_slc", "v_slc", "k_win", "v_win")]
    return (o_nsa, o_m), (k_cmp, v_cmp, k_slc, v_slc, k_win[:, S - wl:], v_win[:, S - wl:], C, n, m, buf)


def sample_mix(x, kc_pool, vc_pool, ks_pool, vs_pool, kw_buf, vw_buf, C0, n0, m0, buf0, page_table,
               w_in, pe, w1, b1, w2, conv_w, conv_b, b_if):
    B, T, _ = x.shape
    past = page_table.shape[1] * PAGE_SIZE
    pos = past + jnp.arange(T)
    z = in_proj_fused(x, w_in, pos, B * T, False)
    q = z["q"].astype(jnp.float32).reshape(B, T, NSA_HEADS, NSA_HD) * (1.0 / ATTN_SCALE)
    k_cmp, v_cmp, k_slc, v_slc, k_win, v_win = [
        z[k].reshape(B, T, NSA_KV_HEADS, NSA_HD) for k in ("k_cmp", "v_cmp", "k_slc", "v_slc", "k_win", "v_win")]
    zgate = z["zgate"].reshape(B, T, GATE_W)
    gates = jax.nn.sigmoid(zgate[..., :IN_SPLITS[2]]).reshape(B, T, NSA_HEADS, 3)
    zqk, zv, zo, zif = (z["zqk"].reshape(B, T, 2 * M_W), z["zv"].reshape(B, T, M_W), z["zo"].reshape(B, T, M_W),
                        zgate[..., IN_SPLITS[2]:])

    assert (past + T) // CMP_STRIDE == past // CMP_STRIDE

    def compressed(pool, c):
        pages = pool.transpose(0, 2, 3, 1)[page_table]
        return compress_pages(pages, pe[c], w1[c], b1[c], w2[c])

    o_cmp, p = cmp_attend(q, pos, compressed(kc_pool, 0), compressed(vc_pool, 1))
    n_sel = -(-(past + T) // SEL_BLOCK)
    member = select_blocks(p, pos, n_sel)
    o_sel = sample_selected_attention(q, pos, member, ks_pool, vs_pool, k_slc, v_slc, page_table)
    wb = kw_buf.shape[1]
    kw = jnp.concatenate([kw_buf, k_win], axis=1)
    vw = jnp.concatenate([vw_buf, v_win], axis=1)
    kpos = past - wb + jnp.arange(wb + T)
    o_win = win_attend(q, pos, kw, vw, kpos)
    o_nsa = nsa_combine(gates, o_cmp, o_sel, o_win)
    o_m, (C, n, m, buf) = mlstm_mix(zqk, zv, zo, zif, buf0, C0, n0, m0, conv_w, conv_b, b_if, T)
    return (o_nsa, o_m), (k_cmp, v_cmp, k_slc, v_slc, kw[:, T:], vw[:, T:], C, n, m, buf)


def kernel(x_prompt, x_sample, cache_k_cmp, cache_v_cmp, cache_k_slc, cache_v_slc, cache_k_win, cache_v_win,
           state_C, state_n, state_m, state_conv, page_table, w_in, w_out, w_phi1, b_phi1, w_phi2, pe_cmp,
           conv_w, conv_b, b_if, ln_g, ln_b, w_pq, sub_keys, u_tab, v_tab):
    l = 0
    mix_p, st_p = prompt_mix(x_prompt, w_in[l], pe_cmp[l], w_phi1[l], b_phi1[l], w_phi2[l],
                             conv_w[l], conv_b[l], b_if[l])
    mix_s, st_s = sample_mix(x_sample, cache_k_cmp[l], cache_v_cmp[l], cache_k_slc[l], cache_v_slc[l],
                             cache_k_win[l], cache_v_win[l], state_C[l], state_n[l], state_m[l],
                             state_conv[l], page_table, w_in[l], pe_cmp[l], w_phi1[l], b_phi1[l],
                             w_phi2[l], conv_w[l], conv_b[l], b_if[l])
    u_bf = u_tab[l].astype(jnp.bfloat16)
    vt_bf = v_tab[l].astype(jnp.bfloat16).reshape(-1, PEER_TILE, D_MODEL).transpose(0, 2, 1)
    xp = block_tail(x_prompt, *mix_p, w_out[l], ln_g[l], ln_b[l], w_pq[l], sub_keys[l], u_bf, vt_bf, 512)
    xs = block_tail(x_sample, *mix_s, w_out[l], ln_g[l], ln_b[l], w_pq[l], sub_keys[l], u_bf, vt_bf, 128)
    return (xp, xs) + tuple(a[None] for a in st_p) + tuple(a[None] for a in st_s)
```

```python
import functools

import jax
import jax.numpy as jnp
import numpy as np
from jax import lax
from jax.experimental import pallas as pl
from jax.experimental.pallas import tpu as pltpu

D_MODEL = 1024
DEPTH = 1
PAGE_SIZE = 128
NSA_HEADS = 8
NSA_KV_HEADS = 2
NSA_GROUP = NSA_HEADS // NSA_KV_HEADS
NSA_HD = 64
NSA_QW = NSA_HEADS * NSA_HD
NSA_KVW = NSA_KV_HEADS * NSA_HD
CMP_BLOCK = 32
CMP_STRIDE = 16
SEL_BLOCK = 64
SEL_TOP = 16
WINDOW = 512
Q_BLOCK = 64
ATTN_SCALE = NSA_HD ** -0.5
ROPE_THETA = 10000.0
M_HEADS = 4
M_HD = 128
M_W = M_HEADS * M_HD
M_CHUNK = 64
CONV_W = 4
PEER_HEADS = 8
N_KEYS = 128
PEER_TOPK = 16
PEER_QDIM = 256
PEER_BLOCK = 128
IN_SPLITS = (NSA_QW, 6 * NSA_KVW, 3 * NSA_HEADS, 2 * M_W, M_W, M_W, 2 * M_HEADS)
LN_EPS = 1e-5
ALPHA = (2 * DEPTH) ** 0.25

VMEM_LIMIT_BYTES = 56 * 1024 * 1024


def _mm_kernel(x_ref, w_ref, o_ref):
    o_ref[...] = jnp.dot(x_ref[...].astype(jnp.bfloat16), w_ref[...], preferred_element_type=jnp.float32)


def pallas_matmul(x, w, tm=512):
    M, K = x.shape
    N = w.shape[1]
    tm = min(tm, M)
    assert M % tm == 0
    return pl.pallas_call(
        _mm_kernel,
        out_shape=jax.ShapeDtypeStruct((M, N), jnp.float32),
        grid=(M // tm,),
        in_specs=[pl.BlockSpec((tm, K), lambda i: (i, 0)), pl.BlockSpec((K, N), lambda i: (0, 0))],
        out_specs=pl.BlockSpec((tm, N), lambda i: (i, 0)),
        compiler_params=pltpu.CompilerParams(dimension_semantics=("arbitrary",),
                                             vmem_limit_bytes=VMEM_LIMIT_BYTES),
        name="proj_matmul",
    )(x, w.astype(jnp.bfloat16))


def mm3(x, w):
    lead = x.shape[:-1]
    return pallas_matmul(x.reshape(-1, x.shape[-1]), w).reshape(*lead, w.shape[1])


def layer_norm(x, g, b):
    mu = x.mean(-1, keepdims=True)
    var = jnp.square(x - mu).mean(-1, keepdims=True)
    return (x - mu) * lax.rsqrt(var + LN_EPS) * g + b


def rope(x, pos):
    half = x.shape[-1] // 2
    inv = ROPE_THETA ** (-jnp.arange(half, dtype=jnp.float32) / half)
    ang = pos.astype(jnp.float32)[:, None] * inv[None, :]
    cos = jnp.cos(ang)[:, None, :]
    sin = jnp.sin(ang)[:, None, :]
    x1, x2 = x[..., :half], x[..., half:]
    return jnp.concatenate([x1 * cos - x2 * sin, x2 * cos + x1 * sin], axis=-1)


def split_in_proj(x, w_in):
    z = mm3(x, w_in)
    cuts = [int(c) for c in np.cumsum(IN_SPLITS)[:-1]]
    return jnp.split(z, cuts, axis=-1)


_IN_OFF = np.concatenate([[0], np.cumsum(IN_SPLITS)])
_IN_ORDER = (0, 1, 3, 4, 5, 2, 6)
_N_KV_ROWS = 6
_KV_BF16 = (2, 3, 4, 5)
_KV_CACHE = (0, 1, 2, 3, 4, 5)
GATE_W = IN_SPLITS[2] + IN_SPLITS[6]


def _rope_pairs(x, cos, sin_signed):
    half = NSA_HD // 2
    lane = lax.broadcasted_iota(jnp.int32, x.shape, 1)
    partner = jnp.where(lane % NSA_HD < half, pltpu.roll(x, LANES - half, 1), pltpu.roll(x, half, 1))
    return x * cos + partner * sin_signed


def _in_proj_kernel(x_ref, w_ref, cos_ref, sin_ref, q_ref, *rest, kv_major):
    kv_refs = rest[:_N_KV_ROWS]
    rest = rest[_N_KV_ROWS:]
    if kv_major:
        bf_refs, rest = rest[:len(_KV_BF16)], rest[len(_KV_BF16):]
        cache_refs, rest = rest[:len(_KV_CACHE)], rest[len(_KV_CACHE):]
    zqk_ref, zv_ref, zo_ref, zgate_ref = rest
    z = jnp.dot(x_ref[...].astype(jnp.bfloat16), w_ref[...], preferred_element_type=jnp.float32)
    cos = cos_ref[...]
    sin = sin_ref[...]
    for g in range(NSA_QW // LANES):
        sl = slice(g * LANES, (g + 1) * LANES)
        q_ref[:, sl] = (_rope_pairs(z[:, sl], cos, sin) * ATTN_SCALE).astype(jnp.bfloat16)
    for r in range(_N_KV_ROWS):
        row = z[:, NSA_QW + r * NSA_KVW:NSA_QW + (r + 1) * NSA_KVW]
        if r % 2 == 0:
            row = _rope_pairs(row, cos, sin)
        kv_refs[r][...] = row
        if kv_major and r in _KV_CACHE:
            dst = cache_refs[_KV_CACHE.index(r)]
            dst[0] = row.T
        if kv_major and r in _KV_BF16:
            dst = bf_refs[_KV_BF16.index(r)]
            for n in range(NSA_KV_HEADS):
                dst[0, n] = row[:, n * NSA_HD:(n + 1) * NSA_HD].astype(jnp.bfloat16)
    o = NSA_QW + _N_KV_ROWS * NSA_KVW
    zqk_ref[...] = z[:, o:o + 2 * M_W]
    zv_ref[...] = z[:, o + 2 * M_W:o + 3 * M_W]
    zo_ref[...] = z[:, o + 3 * M_W:o + 4 * M_W]
    zgate_ref[...] = z[:, o + 4 * M_W:o + 4 * M_W + GATE_W]


def in_proj_fused(x, w_in, pos, tm, kv_major):
    B, T, D = x.shape
    M = B * T
    assert M % tm == 0 and NSA_KVW == LANES and (not kv_major or T % tm == 0)
    f32, bf16 = jnp.float32, jnp.bfloat16
    w = jnp.concatenate([w_in[:, _IN_OFF[i]:_IN_OFF[i + 1]] for i in _IN_ORDER], axis=1).astype(bf16)
    half = NSA_HD // 2
    inv = ROPE_THETA ** (-jnp.arange(half, dtype=f32) / half)
    ang = pos.astype(f32)[:, None] * inv[None, :]
    cos = jnp.tile(jnp.cos(ang), (B, 2 * LANES // NSA_HD))
    sin = jnp.tile(jnp.concatenate([-jnp.sin(ang), jnp.sin(ang)], axis=1), (B, LANES // NSA_HD))
    n_w = w.shape[1]

    def rows(width):
        return pl.BlockSpec((tm, width), lambda i: (i, 0))

    out_shape = [jax.ShapeDtypeStruct((M, NSA_QW), bf16)] + [jax.ShapeDtypeStruct((M, NSA_KVW), f32)] * _N_KV_ROWS
    out_specs = [rows(NSA_QW)] + [rows(NSA_KVW)] * _N_KV_ROWS
    if kv_major:
        per_seq = T // tm
        out_shape += [jax.ShapeDtypeStruct((B, NSA_KV_HEADS, T, NSA_HD), bf16)] * len(_KV_BF16)
        out_specs += [pl.BlockSpec((1, NSA_KV_HEADS, tm, NSA_HD),
                                   lambda i: (i // per_seq, 0, i % per_seq, 0))] * len(_KV_BF16)
        out_shape += [jax.ShapeDtypeStruct((B, NSA_KVW, T), f32)] * len(_KV_CACHE)
        out_specs += [pl.BlockSpec((1, NSA_KVW, tm), lambda i: (i // per_seq, 0, i % per_seq))] * len(_KV_CACHE)
    out_shape += [jax.ShapeDtypeStruct((M, 2 * M_W), f32), jax.ShapeDtypeStruct((M, M_W), f32),
                  jax.ShapeDtypeStruct((M, M_W), f32), jax.ShapeDtypeStruct((M, GATE_W), f32)]
    out_specs += [rows(2 * M_W), rows(M_W), rows(M_W), rows(GATE_W)]
    outs = pl.pallas_call(
        functools.partial(_in_proj_kernel, kv_major=kv_major),
        out_shape=tuple(out_shape),
        grid=(M // tm,),
        in_specs=[rows(D), pl.BlockSpec((D, n_w), lambda i: (0, 0)), rows(LANES), rows(LANES)],
        out_specs=tuple(out_specs),
        compiler_params=pltpu.CompilerParams(dimension_semantics=("arbitrary",),
                                             vmem_limit_bytes=VMEM_LIMIT_BYTES),
        name="in_proj",
    )(x.reshape(M, D), w, cos, sin)
    names = ["q", "k_cmp", "v_cmp", "k_slc", "v_slc", "k_win", "v_win"]
    if kv_major:
        names += ["k_slc_bf", "v_slc_bf", "k_win_bf", "v_win_bf"]
        names += ["k_cmp_cache", "v_cmp_cache", "k_slc_cache", "v_slc_cache", "k_win_cache", "v_win_cache"]
    names += ["zqk", "zv", "zo", "zgate"]
    return dict(zip(names, outs))


def nsa_project(zq, zkv, zg, pos):
    B, T, _ = zq.shape
    q = rope(zq.reshape(B, T, NSA_HEADS, NSA_HD), pos)
    kv = zkv.reshape(B, T, 6, NSA_KV_HEADS, NSA_HD)
    rows = (rope(kv[:, :, 0], pos), kv[:, :, 1], rope(kv[:, :, 2], pos), kv[:, :, 3],
            rope(kv[:, :, 4], pos), kv[:, :, 5])
    gates = jax.nn.sigmoid(zg).reshape(B, T, NSA_HEADS, 3)
    return q, rows, gates


def _expanded_w1(w1):
    assert CMP_BLOCK == 2 * CMP_STRIDE
    w1r = w1.reshape(2, CMP_STRIDE, NSA_HD, w1.shape[-1])
    wbig = jnp.einsum('hpdf,kn->pkdnhf', w1r, jnp.eye(NSA_KV_HEADS, dtype=w1.dtype))
    return wbig.reshape(CMP_STRIDE * NSA_KVW, 2 * NSA_KV_HEADS * w1.shape[-1])


def _compress_rows(x, w_ref, bias_ref, w2_ref, o_ref):
    f32, bf16 = jnp.float32, jnp.bfloat16
    rows = x.shape[0]
    f = w2_ref.shape[0]
    proj = jnp.dot(x.astype(bf16), w_ref[...], preferred_element_type=f32)
    for n in range(NSA_KV_HEADS):
        first = proj[:, 2 * n * f:(2 * n + 1) * f]
        second = pltpu.roll(proj[:, (2 * n + 1) * f:(2 * n + 2) * f], rows - 1, 0)
        pre = first + second + bias_ref[...]
        hid = 0.5 * pre * (1.0 + lax.erf(pre * (2.0 ** -0.5)))
        o_ref[:, n * NSA_HD:(n + 1) * NSA_HD] = jnp.dot(hid.astype(bf16), w2_ref[...], preferred_element_type=f32)


def _compress_chunks_kernel(x_ref, w_ref, bias_ref, w2_ref, o_ref):
    _compress_rows(x_ref[...], w_ref, bias_ref, w2_ref, o_ref)


def _compress_weights(pe, w1, b1, w2):
    bf16 = jnp.bfloat16
    bias = jnp.dot(pe.reshape(-1), w1, precision=lax.Precision.HIGHEST) + b1
    return _expanded_w1(w1).astype(bf16), bias[None], w2.astype(bf16)


def compress_chunks(rows, per_seq, pe, w1, b1, w2, tm=512):
    chunks = rows.reshape(-1, CMP_STRIDE * NSA_KVW)
    n = chunks.shape[0]
    tm = min(tm, n)
    assert n % tm == 0 and tm % per_seq == 0
    wbig, bias, w2b = _compress_weights(pe, w1, b1, w2)

    def whole(a):
        return pl.BlockSpec(a.shape, lambda i: (0, 0))

    out = pl.pallas_call(
        _compress_chunks_kernel,
        out_shape=jax.ShapeDtypeStruct((n, NSA_KVW), jnp.float32),
        grid=(n // tm,),
        in_specs=[pl.BlockSpec((tm, chunks.shape[1]), lambda i: (i, 0)), whole(wbig), whole(bias), whole(w2b)],
        out_specs=pl.BlockSpec((tm, NSA_KVW), lambda i: (i, 0)),
        compiler_params=pltpu.CompilerParams(dimension_semantics=("arbitrary",),
                                             vmem_limit_bytes=VMEM_LIMIT_BYTES),
        name="compress_chunks",
    )(chunks, wbig, bias, w2b)
    return out.reshape(n // per_seq, per_seq, NSA_KV_HEADS, NSA_HD)[:, :-1]


PAGE_GROUP = 4


def _compress_pages_kernel(pg_ref, w_ref, bias_ref, w2_ref, o_ref, x_ref, t_ref):
    n_pages = pg_ref.shape[1]
    per_page = PAGE_SIZE // CMP_STRIDE
    group = PAGE_GROUP

    def place(i, carry):
        for u in range(group):
            g = i * group + u
            t_ref[u] = pg_ref[0, g].reshape(NSA_KVW, PAGE_SIZE).T
            row0 = pl.multiple_of(g * per_page, per_page)
            for p in range(CMP_STRIDE):
                x_ref[pl.ds(row0, per_page), p * NSA_KVW:(p + 1) * NSA_KVW] = (
                    t_ref.at[u][pl.ds(p, per_page, stride=CMP_STRIDE), :])
        return carry

    lax.fori_loop(0, n_pages // group, place, 0)
    _compress_rows(x_ref[...], w_ref, bias_ref, w2_ref, o_ref.at[0])


def compress_pages(pages, pe, w1, b1, w2):
    B, n_pages = pages.shape[:2]
    assert pages.shape[2:] == (NSA_KV_HEADS, NSA_HD, PAGE_SIZE) and NSA_KVW == LANES and PAGE_SIZE == LANES
    assert n_pages % PAGE_GROUP == 0
    wbig, bias, w2b = _compress_weights(pe, w1, b1, w2)
    rows = n_pages * (PAGE_SIZE // CMP_STRIDE)

    def whole(a):
        return pl.BlockSpec(a.shape, lambda b: (0, 0))

    out = pl.pallas_call(
        _compress_pages_kernel,
        out_shape=jax.ShapeDtypeStruct((B, rows, NSA_KVW), jnp.float32),
        grid=(B,),
        in_specs=[pl.BlockSpec((1, n_pages, NSA_KV_HEADS, NSA_HD, PAGE_SIZE), lambda b: (b, 0, 0, 0, 0)),
                  whole(wbig), whole(bias), whole(w2b)],
        out_specs=pl.BlockSpec((1, rows, NSA_KVW), lambda b: (b, 0, 0)),
        scratch_shapes=[pltpu.VMEM((rows, wbig.shape[0]), jnp.float32),
                        pltpu.VMEM((PAGE_GROUP, PAGE_SIZE, NSA_KVW), jnp.float32)],
        compiler_params=pltpu.CompilerParams(dimension_semantics=("arbitrary",),
                                             vmem_limit_bytes=VMEM_LIMIT_BYTES),
        name="compress_pages",
    )(pages, wbig, bias, w2b)
    return out.reshape(B, rows, NSA_KV_HEADS, NSA_HD)[:, :-1]


def cmp_attend(q, qpos, kc, vc):
    B, T = q.shape[:2]
    qg = q.reshape(B, T, NSA_KV_HEADS, NSA_GROUP, NSA_HD)
    s = jnp.einsum('btngd,bcnd->btngc', qg, kc) * ATTN_SCALE
    nblk = kc.shape[1]
    blk_end = jnp.arange(nblk) * CMP_STRIDE + CMP_BLOCK - 1
    valid = (blk_end[None, :] <= qpos[:, None])[None, :, None, None, :]
    p = jax.nn.softmax(jnp.where(valid, s, -1e30), axis=-1) * valid
    o = jnp.einsum('btngc,bcnd->btngd', p, vc)
    return o.reshape(B, T, NSA_HEADS, NSA_HD), p


def select_blocks(p, qpos, n_sel):
    imp = p.sum(axis=3)
    R = SEL_BLOCK // CMP_STRIDE
    r = CMP_BLOCK // CMP_STRIDE
    nb = imp.shape[-1]
    right = n_sel * R + R - 1 - nb
    padded = jnp.pad(imp, ((0, 0), (0, 0), (0, 0), (r - 1, right)))
    score = padded[..., 0:(n_sel - 1) * R + 1:R]
    for o in range(1, R + r - 1):
        score = score + padded[..., o:o + (n_sel - 1) * R + 1:R]
    j = jnp.arange(n_sel)[None, :]
    cur = (qpos // SEL_BLOCK)[:, None]
    valid = (j * SEL_BLOCK <= qpos[:, None])[None, :, None, :]
    forced = ((j == 0) | (j == cur) | (j == cur - 1))[None, :, None, :]
    score = jnp.where(forced, jnp.inf, jnp.where(valid, score, -jnp.inf))
    idx = j[0]
    before = (score[..., None, :] > score[..., :, None]) | ((score[..., None, :] == score[..., :, None])
                                                          & (idx[None, :] < idx[:, None]))
    return before.sum(-1) < min(SEL_TOP, n_sel)


def sample_selected_attention(q, qpos, member, k_pool, v_pool, k_new, v_new, page_table):
    B, T = q.shape[:2]
    n_pages = page_table.shape[1]
    per_page = PAGE_SIZE // SEL_BLOCK
    assert member.shape[-1] == n_pages * per_page + 1 and T <= SEL_BLOCK
    kp = k_pool.transpose(0, 2, 3, 1)[page_table]
    vp = v_pool.transpose(0, 2, 3, 1)[page_table]
    qg = q.reshape(B, T, NSA_KV_HEADS, NSA_GROUP, NSA_HD)
    s_past = jnp.einsum('btngd,bpndk->bntgpk', qg, kp) * ATTN_SCALE
    s_new = jnp.einsum('btngd,bsnd->bntgs', qg, k_new) * ATTN_SCALE
    m = member.transpose(0, 2, 1, 3)
    m_past = jnp.repeat(m[..., :-1].reshape(B, NSA_KV_HEADS, T, n_pages, per_page), SEL_BLOCK, axis=-1)
    kpos = (jnp.arange(n_pages) * PAGE_SIZE)[:, None] + jnp.arange(PAGE_SIZE)[None, :]
    m_past = m_past & (kpos[None, None, None] <= qpos[None, None, :, None, None])
    new_pos = n_pages * PAGE_SIZE + jnp.arange(T)
    m_new = m[..., -1:] & (new_pos[None, None, None, :] <= qpos[None, None, :, None])
    logits = jnp.concatenate(
        [jnp.where(m_past[:, :, :, None], s_past, -jnp.inf).reshape(B, NSA_KV_HEADS, T, NSA_GROUP, -1),
         jnp.where(m_new[:, :, :, None], s_new, -jnp.inf)], axis=-1)
    pr = jax.nn.softmax(logits, axis=-1)
    pr_past = pr[..., :n_pages * PAGE_SIZE].reshape(B, NSA_KV_HEADS, T, NSA_GROUP, n_pages, PAGE_SIZE)
    o = (jnp.einsum('bntgpk,bpndk->bntgd', pr_past, vp)
         + jnp.einsum('bntgs,bsnd->bntgd', pr[..., n_pages * PAGE_SIZE:], v_new))
    return o.transpose(0, 2, 1, 3, 4).reshape(B, T, NSA_HEADS, NSA_HD)


def to_blocks(rows, n_sel):
    B, L, KV, hd = rows.shape
    rows = jnp.pad(rows, ((0, 0), (0, n_sel * SEL_BLOCK - L), (0, 0), (0, 0)))
    return rows.reshape(B, n_sel, SEL_BLOCK, KV, hd).transpose(0, 3, 1, 2, 4)


def take_rows(table, idx):
    return table[idx]


def sel_attend(q, qpos, sel, kb, vb):
    B, Tq = q.shape[:2]
    k = sel.shape[-1]
    sel_t = sel.transpose(0, 2, 1, 3)
    gather = jax.vmap(jax.vmap(take_rows))
    kg = gather(kb, sel_t).reshape(B, NSA_KV_HEADS, Tq, k * SEL_BLOCK, NSA_HD)
    vg = gather(vb, sel_t).reshape(B, NSA_KV_HEADS, Tq, k * SEL_BLOCK, NSA_HD)
    kpos = (sel_t[..., None] * SEL_BLOCK + jnp.arange(SEL_BLOCK)).reshape(B, NSA_KV_HEADS, Tq, k * SEL_BLOCK)
    qg = q.reshape(B, Tq, NSA_KV_HEADS, NSA_GROUP, NSA_HD).transpose(0, 2, 1, 3, 4)
    s = jnp.einsum('bntgd,bntsd->bntgs', qg, kg) * ATTN_SCALE
    mask = kpos[:, :, :, None, :] <= qpos[None, None, :, None, None]
    pr = jax.nn.softmax(jnp.where(mask, s, -jnp.inf), axis=-1)
    o = jnp.einsum('bntgs,bntsd->bntgd', pr, vg)
    return o.transpose(0, 2, 1, 3, 4).reshape(B, Tq, NSA_HEADS, NSA_HD)


def win_attend(q, qpos, k, v, kpos):
    B, Tq = q.shape[:2]
    qg = q.reshape(B, Tq, NSA_KV_HEADS, NSA_GROUP, NSA_HD)
    s = jnp.einsum('btngd,bsnd->btngs', qg, k) * ATTN_SCALE
    diff = qpos[:, None] - kpos[None, :]
    mask = ((diff >= 0) & (diff < WINDOW) & (kpos[None, :] >= 0))[None, :, None, None, :]
    pr = jax.nn.softmax(jnp.where(mask, s, -jnp.inf), axis=-1)
    o = jnp.einsum('btngs,bsnd->btngd', pr, v)
    return o.reshape(B, Tq, NSA_HEADS, NSA_HD)


def nsa_combine(gates, o_cmp, o_sel, o_win):
    B, T = gates.shape[:2]
    o = gates[..., 0:1] * o_cmp + gates[..., 1:2] * o_sel + gates[..., 2:3] * o_win
    return o.reshape(B, T, NSA_QW)


NSA_TQ = 128
NSA_CK = 512
MASKED = -1e30


def _softmax_rows(s):
    m = jnp.max(s, axis=-1, keepdims=True)
    e = jnp.exp(s - m)
    return e / jnp.sum(e, axis=-1, keepdims=True)


def _nsa_prompt_kernel(q_ref, kc_ref, vc_ref, ks_ref, vs_ref, kw_ref, vw_ref, zg_ref, msel_ref, exp_ref, o_ref):
    f32, bf16 = jnp.float32, jnp.bfloat16
    tq = NSA_TQ
    q0 = pl.program_id(2) * tq
    qb = q_ref[0]
    qs = jnp.concatenate([qb[:, g * NSA_HD:(g + 1) * NSA_HD] for g in range(NSA_GROUP)], axis=0)
    tpos = q0 + lax.broadcasted_iota(jnp.int32, (tq, 1), 0)

    def per_head(a):
        return jnp.concatenate([a] * NSA_GROUP, axis=0)

    s = lax.dot_general(qs, kc_ref[0, 0], _NT, preferred_element_type=f32)
    cblk = lax.broadcasted_iota(jnp.int32, (tq, 128), 1)
    cvalid = cblk * CMP_STRIDE + (CMP_BLOCK - 1) <= tpos
    s = s + per_head(jnp.where(cvalid, 0.0, MASKED))
    e = jnp.exp(s - jnp.max(s, axis=-1, keepdims=True)) * per_head(jnp.where(cvalid, 1.0, 0.0))
    l = jnp.sum(e, axis=-1, keepdims=True)
    p = e / jnp.where(l > 0.0, l, 1.0)
    o_cmp = jnp.dot(p.astype(bf16), vc_ref[0, 0], preferred_element_type=f32)

    imp = p[0:tq]
    for g in range(1, NSA_GROUP):
        imp = imp + p[g * tq:(g + 1) * tq]
    hi = imp.astype(bf16)
    r1 = imp - hi.astype(f32)
    mid = r1.astype(bf16)
    lo = (r1 - mid.astype(f32)).astype(bf16)
    msel = msel_ref[...]
    score = (lax.dot_general(msel, hi, _NT, preferred_element_type=f32)
             + lax.dot_general(msel, mid, _NT, preferred_element_type=f32)
             + lax.dot_general(msel, lo, _NT, preferred_element_type=f32))
    n_sel = score.shape[0]
    j = lax.broadcasted_iota(jnp.int32, (n_sel, tq), 0)
    tok = q0 + lax.broadcasted_iota(jnp.int32, (n_sel, tq), 1)
    cur = tok // SEL_BLOCK
    forced = (j == 0) | (j == cur) | (j == cur - 1)
    score = jnp.where(forced, jnp.inf, jnp.where(j * SEL_BLOCK <= tok, score, -jnp.inf))
    rank = jnp.zeros((n_sel, tq), f32)
    for jp in range(n_sel):
        row = score[jp:jp + 1, :]
        before = (row > score) | ((row == score) & (j > jp))
        rank = rank + jnp.where(before, 1.0, 0.0)
    chosen_t = jnp.where(rank < SEL_TOP, 1.0, 0.0)
    sel01 = jnp.concatenate([chosen_t, jnp.zeros((LANES - n_sel, tq), f32)], axis=0).T.astype(bf16)

    ck = NSA_CK
    rows = NSA_GROUP * tq

    def sel_chunk(c, carry):
        m, l, acc = carry
        k0 = pl.multiple_of(c * ck, ck)
        s = lax.dot_general(qs, ks_ref[0, 0, pl.ds(k0, ck), :], _NT, preferred_element_type=f32)
        chosen = jnp.dot(sel01, exp_ref[c], preferred_element_type=f32)
        kpos = k0 + lax.broadcasted_iota(jnp.int32, (tq, ck), 1)
        ok = (chosen > 0.5) & (kpos <= tpos)
        s = s + per_head(jnp.where(ok, 0.0, MASKED))
        m_new = jnp.maximum(m, jnp.max(s, axis=-1, keepdims=True))
        a = jnp.exp(m - m_new)
        pr = jnp.exp(s - m_new)
        l = a * l + jnp.sum(pr, axis=-1, keepdims=True)
        acc = a * acc + jnp.dot(pr.astype(bf16), vs_ref[0, 0, pl.ds(k0, ck), :], preferred_element_type=f32)
        return m_new, l, acc

    init = (jnp.full((rows, 1), MASKED, f32), jnp.zeros((rows, 1), f32), jnp.zeros((rows, NSA_HD), f32))
    n_chunks = (q0 + tq + ck - 1) // ck
    _, l_sel, acc_sel = lax.fori_loop(0, n_chunks, sel_chunk, init)
    o_sel = acc_sel / l_sel

    w0 = pl.multiple_of(jnp.maximum(q0 - WINDOW, 0), tq)
    wl = WINDOW + tq
    s = lax.dot_general(qs, kw_ref[0, 0, pl.ds(w0, wl), :], _NT, preferred_element_type=f32)
    diff = tpos - (w0 + lax.broadcasted_iota(jnp.int32, (tq, wl), 1))
    s = s + per_head(jnp.where((diff >= 0) & (diff < WINDOW), 0.0, MASKED))
    o_win = jnp.dot(_softmax_rows(s).astype(bf16), vw_ref[0, 0, pl.ds(w0, wl), :], preferred_element_type=f32)

    gates = jax.nn.sigmoid(zg_ref[0, 0])
    for g in range(NSA_GROUP):
        r = slice(g * tq, (g + 1) * tq)
        o_ref[0, :, g * NSA_HD:(g + 1) * NSA_HD] = (gates[:, 3 * g:3 * g + 1] * o_cmp[r]
                                                    + gates[:, 3 * g + 1:3 * g + 2] * o_sel[r]
                                                    + gates[:, 3 * g + 2:3 * g + 3] * o_win[r])


def nsa_prompt_attention(qs, kc, vc, k_slc, v_slc, k_win, v_win, zg):
    B, S = qs.shape[:2]
    bf16 = jnp.bfloat16
    assert S % NSA_CK == 0 and S % NSA_TQ == 0 and WINDOW % NSA_TQ == 0 and WINDOW + NSA_TQ <= S
    n_sel = S // SEL_BLOCK
    nb = kc.shape[1]
    assert nb <= 128

    def pad_blocks(a):
        return jnp.pad(a.transpose(0, 2, 1, 3).astype(bf16), ((0, 0), (0, 0), (0, 128 - nb), (0, 0)))

    zg4 = zg.reshape(B, S, NSA_KV_HEADS, 3 * NSA_GROUP).transpose(0, 2, 1, 3)
    c = np.arange(128)[:, None]
    jj = np.arange(n_sel)[None, :]
    ratio = SEL_BLOCK // CMP_STRIDE
    msel = ((c >= jj * ratio - (CMP_BLOCK // CMP_STRIDE - 1)) & (c <= jj * ratio + ratio - 1) & (c < nb))
    assert n_sel <= LANES and NSA_TQ == LANES
    expand = (np.arange(S)[None, :] // SEL_BLOCK == np.arange(LANES)[:, None])
    expand = expand.reshape(LANES, S // NSA_CK, NSA_CK).transpose(1, 0, 2)
    row_spec = pl.BlockSpec((1, 1, S, NSA_HD), lambda b, n, i: (b, n, 0, 0))
    blk_spec = pl.BlockSpec((1, 1, 128, NSA_HD), lambda b, n, i: (b, n, 0, 0))
    return pl.pallas_call(
        _nsa_prompt_kernel,
        out_shape=jax.ShapeDtypeStruct((B, S, NSA_QW), jnp.float32),
        grid=(B, NSA_KV_HEADS, S // NSA_TQ),
        in_specs=[pl.BlockSpec((1, NSA_TQ, NSA_GROUP * NSA_HD), lambda b, n, i: (b, i, n)),
                  blk_spec, blk_spec, row_spec, row_spec, row_spec, row_spec,
                  pl.BlockSpec((1, 1, NSA_TQ, 3 * NSA_GROUP), lambda b, n, i: (b, n, i, 0)),
                  pl.BlockSpec((n_sel, 128), lambda b, n, i: (0, 0)),
                  pl.BlockSpec((S // NSA_CK, LANES, NSA_CK), lambda b, n, i: (0, 0, 0))],
        out_specs=pl.BlockSpec((1, NSA_TQ, NSA_GROUP * NSA_HD), lambda b, n, i: (b, i, n)),
        compiler_params=pltpu.CompilerParams(dimension_semantics=("arbitrary", "arbitrary", "arbitrary"),
                                             vmem_limit_bytes=VMEM_LIMIT_BYTES),
        name="nsa_prompt_attention",
    )(qs, pad_blocks(kc), pad_blocks(vc), k_slc, v_slc, k_win, v_win,
      zg4, jnp.asarray(msel.T, bf16), jnp.asarray(expand, bf16))


MLSTM_L = 128
CONV_HALO = 8


def _log_sigmoid(x):
    return -(jnp.maximum(-x, 0.0) + jnp.log1p(jnp.exp(-jnp.abs(x))))


def _mlstm_prompt_kernel(x_ref, xprev_ref, halo0_ref, v_ref, o_ref, gcol_ref, grow_ref, cw_ref, cb_ref,
                         out_ref, c_out, n_out, m_out, c_ref, n_ref, m_ref):
    f32, bf16 = jnp.float32, jnp.bfloat16
    c = pl.program_id(1)
    L = MLSTM_L

    @pl.when(c == 0)
    def _():
        c_ref[...] = jnp.zeros_like(c_ref)
        n_ref[...] = jnp.zeros_like(n_ref)
        m_ref[...] = jnp.zeros_like(m_ref)

    x = x_ref[0]
    halo = jnp.where(c == 0, halo0_ref[0], xprev_ref[0, L - CONV_HALO:L, :])
    ext = jnp.concatenate([halo, x], axis=0)
    conv = cb_ref[...]
    for j in range(CONV_W):
        o = CONV_HALO - (CONV_W - 1) + j
        conv = conv + ext[o:o + L] * cw_ref[j:j + 1, :]
    qk = conv * jax.nn.sigmoid(conv)

    t_id = lax.broadcasted_iota(jnp.int32, (L, L), 0)
    s_id = lax.broadcasted_iota(jnp.int32, (L, L), 1)
    causal = t_id >= s_id
    gcol = gcol_ref[0, 0]
    grow = grow_ref[0, 0]
    for h in range(M_HEADS):
        hd = slice(h * M_HD, (h + 1) * M_HD)
        q = qk[:, hd]
        k = qk[:, M_W + h * M_HD:M_W + (h + 1) * M_HD] * (M_HD ** -0.5)
        v = v_ref[0, :, hd].astype(bf16)
        ig_r = grow[h:h + 1, :]
        ig_c = gcol[:, h:h + 1]
        lf_r = _log_sigmoid(grow[M_HEADS + h:M_HEADS + h + 1, :])
        lf_c = _log_sigmoid(gcol[:, M_HEADS + h:M_HEADS + h + 1])
        b_c = jnp.sum(jnp.where(causal, lf_r, 0.0), axis=1, keepdims=True)
        b_r = jnp.sum(jnp.where(t_id <= s_id, lf_c, 0.0), axis=0, keepdims=True)
        m_prev = m_ref[h]
        dmat = jnp.where(causal, b_c - b_r + ig_r, -jnp.inf)
        inter = b_c + m_prev
        m_t = jnp.maximum(inter, jnp.max(dmat, axis=1, keepdims=True))
        w_intra = jnp.exp(dmat - m_t)
        w_inter = jnp.exp(inter - m_t)
        qb = q.astype(bf16)
        s = lax.dot_general(qb, k.astype(bf16), _NT, preferred_element_type=f32) * w_intra
        num = (jnp.dot(s.astype(bf16), v, preferred_element_type=f32)
               + w_inter * jnp.dot(qb, c_ref[h].astype(bf16), preferred_element_type=f32))
        den = jnp.sum(s, axis=1, keepdims=True) + w_inter * jnp.sum(q * n_ref[h], axis=1, keepdims=True)
        hh = num / jnp.maximum(jnp.abs(den), jnp.exp(-m_t))
        out_ref[0, :, hd] = jax.nn.sigmoid(o_ref[0, :, hd]) * hh
        m_new = m_t[L - 1:L]
        b_last = b_c[L - 1:L]
        w_s = jnp.exp(b_last - b_c + ig_c - m_new)
        w_p = jnp.exp(b_last + m_prev - m_new)
        kw = k * w_s
        c_ref[h] = w_p * c_ref[h] + jnp.dot(kw.T.astype(bf16), v, preferred_element_type=f32)
        n_ref[h] = w_p * n_ref[h] + jnp.sum(kw, axis=0, keepdims=True)
        m_ref[h] = m_new

    @pl.when(c == pl.num_programs(1) - 1)
    def _():
        c_out[0] = c_ref[...]
        n_out[0] = n_ref[...]
        m_out[0] = m_ref[...]


def mlstm_prompt(zqk, zv, zo, zif, conv_w, conv_b, b_if):
    B, T, _ = zqk.shape
    L = MLSTM_L
    assert T % L == 0
    nc = T // L
    f32 = jnp.float32
    gif = zif + b_if
    gcol = gif.reshape(B, nc, L, 2 * M_HEADS)
    grow = gcol.transpose(0, 1, 3, 2)
    halo0 = jnp.zeros((B, CONV_HALO, 2 * M_W), f32)
    out, C, n, m = pl.pallas_call(
        _mlstm_prompt_kernel,
        out_shape=(jax.ShapeDtypeStruct((B, T, M_W), f32),
                   jax.ShapeDtypeStruct((B, M_HEADS, M_HD, M_HD), f32),
                   jax.ShapeDtypeStruct((B, M_HEADS, 1, M_HD), f32),
                   jax.ShapeDtypeStruct((B, M_HEADS, 1, 1), f32)),
        grid=(B, nc),
        in_specs=[pl.BlockSpec((1, L, 2 * M_W), lambda b, c: (b, c, 0)),
                  pl.BlockSpec((1, L, 2 * M_W), lambda b, c: (b, jnp.maximum(c - 1, 0), 0)),
                  pl.BlockSpec((1, CONV_HALO, 2 * M_W), lambda b, c: (b, 0, 0)),
                  pl.BlockSpec((1, L, M_W), lambda b, c: (b, c, 0)),
                  pl.BlockSpec((1, L, M_W), lambda b, c: (b, c, 0)),
                  pl.BlockSpec((1, 1, L, 2 * M_HEADS), lambda b, c: (b, c, 0, 0)),
                  pl.BlockSpec((1, 1, 2 * M_HEADS, L), lambda b, c: (b, c, 0, 0)),
                  pl.BlockSpec((CONV_W, 2 * M_W), lambda b, c: (0, 0)),
                  pl.BlockSpec((1, 2 * M_W), lambda b, c: (0, 0))],
        out_specs=(pl.BlockSpec((1, L, M_W), lambda b, c: (b, c, 0)),
                   pl.BlockSpec((1, M_HEADS, M_HD, M_HD), lambda b, c: (b, 0, 0, 0)),
                   pl.BlockSpec((1, M_HEADS, 1, M_HD), lambda b, c: (b, 0, 0, 0)),
                   pl.BlockSpec((1, M_HEADS, 1, 1), lambda b, c: (b, 0, 0, 0))),
        scratch_shapes=[pltpu.VMEM((M_HEADS, M_HD, M_HD), f32), pltpu.VMEM((M_HEADS, 1, M_HD), f32),
                        pltpu.VMEM((M_HEADS, 1, 1), f32)],
        compiler_params=pltpu.CompilerParams(dimension_semantics=("arbitrary", "arbitrary"),
                                             vmem_limit_bytes=VMEM_LIMIT_BYTES),
        name="mlstm_prompt",
    )(zqk, zqk, halo0, zv, zo, gcol, grow, conv_w, conv_b[None])
    return out, C, n.reshape(B, M_HEADS, M_HD), m.reshape(B, M_HEADS)


def mlstm_chunk(carry, inp):
    C, n, m = carry
    q, k, v, ig, lf = inp
    L = q.shape[2]
    b = jnp.cumsum(lf, axis=-1)
    causal = jnp.tril(jnp.ones((L, L), dtype=bool))
    dmat = jnp.where(causal, b[..., :, None] - b[..., None, :] + ig[..., None, :], -jnp.inf)
    inter = b + m[..., None]
    m_t = jnp.maximum(inter, dmat.max(axis=-1))
    w_intra = jnp.exp(dmat - m_t[..., None])
    w_inter = jnp.exp(inter - m_t)
    s = jnp.einsum('bhtd,bhsd->bhts', q, k) * w_intra
    num = jnp.einsum('bhts,bhsv->bhtv', s, v) + w_inter[..., None] * jnp.einsum('bhtd,bhdv->bhtv', q, C)
    den = s.sum(-1) + w_inter * jnp.einsum('bhtd,bhd->bht', q, n)
    h = num / jnp.maximum(jnp.abs(den), jnp.exp(-m_t))[..., None]
    m_new = m_t[..., -1]
    w_s = jnp.exp(b[..., -1:] - b + ig - m_new[..., None])
    w_p = jnp.exp(b[..., -1] + m - m_new)
    C_new = w_p[..., None, None] * C + jnp.einsum('bhs,bhsd,bhsv->bhdv', w_s, k, v)
    n_new = w_p[..., None] * n + jnp.einsum('bhs,bhsd->bhd', w_s, k)
    return (C_new, n_new, m_new), h


def mlstm_mix(zqk, zv, zo, zif, buf0, C0, n0, m0, conv_w, conv_b, b_if, chunk):
    B, T, _ = zqk.shape
    full = jnp.concatenate([buf0, zqk], axis=1)
    conv = conv_b
    for j in range(CONV_W):
        conv = conv + full[:, j:j + T] * conv_w[j]
    qk = jax.nn.silu(conv)

    def heads(a):
        return a.reshape(B, T, M_HEADS, M_HD).transpose(0, 2, 1, 3)

    q = heads(qk[..., :M_W])
    k = heads(qk[..., M_W:]) * (M_HD ** -0.5)
    v = heads(zv)
    gif = zif + b_if
    ig = gif[..., :M_HEADS].transpose(0, 2, 1)
    lf = jax.nn.log_sigmoid(gif[..., M_HEADS:]).transpose(0, 2, 1)
    nc = T // chunk

    def to_chunks(a):
        return jnp.moveaxis(a.reshape(B, M_HEADS, nc, chunk, *a.shape[3:]), 2, 0)

    (C, n, m), h = lax.scan(mlstm_chunk, (C0, n0, m0),
                            (to_chunks(q), to_chunks(k), to_chunks(v), to_chunks(ig), to_chunks(lf)))
    h = jnp.moveaxis(h, 0, 2).reshape(B, M_HEADS, T, M_HD).transpose(0, 2, 1, 3).reshape(B, T, M_W)
    out = jax.nn.sigmoid(zo) * h
    return out, (C, n, m, full[:, T:])


PEER_COMBOS = 2 * PEER_HEADS
PEER_KEY_ROWS = 8
PEER_TILE = PEER_KEY_ROWS * N_KEYS
PEER_TS_ROWS = 24
LANES = 128
_NT = (((1,), (1,)), ((), ()))


def _peer_topk_kernel(q_ref, keys_ref, s_ref, e0_ref, e1_ref, tau_ref, ts_ref):
    head = pl.program_id(1)
    tt = q_ref.shape[0]
    half_w = PEER_QDIM // 2
    for half in range(2):
        c = 2 * head + half
        qh = q_ref[:, half * half_w:(half + 1) * half_w].astype(jnp.bfloat16)
        s = lax.dot_general(keys_ref[half], qh, _NT, preferred_element_type=jnp.float32)
        s_ref[c] = s
        key_id = lax.broadcasted_iota(jnp.int32, s.shape, 0)
        work = s
        rows = []
        for _ in range(PEER_TOPK + 1):
            m = jnp.max(work, axis=0, keepdims=True)
            first = jnp.min(jnp.where(work == m, key_id, N_KEYS), axis=0, keepdims=True)
            work = jnp.where(key_id == first, -jnp.inf, work)
            rows.append(m)
        rows.append(jnp.full((PEER_TS_ROWS - PEER_TOPK - 1, tt), -jnp.inf, jnp.float32))
        ts_ref[c] = jnp.concatenate(rows, axis=0)

    @pl.when(head == PEER_HEADS - 1)
    def _():
        for h in range(PEER_HEADS):
            t0 = ts_ref[2 * h]
            t1 = ts_ref[2 * h + 1]
            pieces = [t0[0:1] + t1] + [t0[a:a + 1] + t1[0:8] for a in range(1, 8)] + [t0[8:24] + t1[0:1]]
            cand = jnp.concatenate(pieces, axis=0)
            top = t0[0:1] + t1[0:1]
            v16 = top
            v17 = top
            z = jnp.zeros_like(top)
            seen = jnp.zeros_like(top)
            for _ in range(PEER_TOPK + 1):
                m = jnp.max(cand, axis=0, keepdims=True)
                eq = cand == m
                cnt = jnp.sum(jnp.where(eq, 1.0, 0.0), axis=0, keepdims=True)
                active = seen < PEER_TOPK
                take = jnp.minimum(cnt, PEER_TOPK - seen)
                v16 = jnp.where(active, m, v16)
                v17 = jnp.where(seen < PEER_TOPK + 1, m, v17)
                z = z + jnp.where(active, take * jnp.exp(m - top), 0.0)
                seen = seen + cnt
                cand = jnp.where(eq, -jnp.inf, cand)
            tau_ref[h:h + 1, :] = 0.5 * v16 + 0.5 * v17
            e0_ref[h] = jnp.exp(s_ref[2 * h] - t0[0:1]) / z
            e1_ref[h] = jnp.exp(s_ref[2 * h + 1] - t1[0:1])


def peer_scores(q, sub_keys, tt):
    n = q.shape[0]
    assert n % tt == 0
    keys = sub_keys.reshape(PEER_COMBOS, N_KEYS, PEER_QDIM // 2).astype(jnp.bfloat16)
    f32 = jnp.float32
    return pl.pallas_call(
        _peer_topk_kernel,
        out_shape=(jax.ShapeDtypeStruct((PEER_COMBOS, N_KEYS, n), f32),
                   jax.ShapeDtypeStruct((PEER_HEADS, N_KEYS, n), f32),
                   jax.ShapeDtypeStruct((PEER_HEADS, N_KEYS, n), f32),
                   jax.ShapeDtypeStruct((PEER_HEADS, n), f32)),
        grid=(n // tt, PEER_HEADS),
        in_specs=[pl.BlockSpec((tt, PEER_QDIM), lambda i, c: (i, c)),
                  pl.BlockSpec((2, N_KEYS, PEER_QDIM // 2), lambda i, c: (c, 0, 0))],
        out_specs=(pl.BlockSpec((PEER_COMBOS, N_KEYS, tt), lambda i, c: (0, 0, i)),
                   pl.BlockSpec((PEER_HEADS, N_KEYS, tt), lambda i, c: (0, 0, i)),
                   pl.BlockSpec((PEER_HEADS, N_KEYS, tt), lambda i, c: (0, 0, i)),
                   pl.BlockSpec((PEER_HEADS, tt), lambda i, c: (0, i))),
        scratch_shapes=[pltpu.VMEM((PEER_COMBOS, PEER_TS_ROWS, tt), f32)],
        compiler_params=pltpu.CompilerParams(dimension_semantics=("arbitrary", "arbitrary"),
                                             vmem_limit_bytes=VMEM_LIMIT_BYTES),
        name="peer_topk",
    )(q, keys)


def _peer_dense_kernel(xt_ref, h_ref, u_ref, vt_ref, s0_ref, ez_ref, s_ref, e1_ref, tau_ref, g_ref, b_ref,
                       o_ref, acc_ref, a_ref, w_ref):
    e = pl.program_id(1)
    tt = xt_ref.shape[1]

    @pl.when(e == 0)
    def _():
        acc_ref[...] = jnp.zeros_like(acc_ref)

    a_ref[...] = jnp.dot(u_ref[...], xt_ref[...], preferred_element_type=jnp.float32)
    for r in range(PEER_KEY_ROWS):
        rows = slice(r * N_KEYS, (r + 1) * N_KEYS)
        for t in range(tt // LANES):
            tok = slice(t * LANES, (t + 1) * LANES)
            gate = jnp.zeros((N_KEYS, LANES), jnp.float32)
            for h in range(PEER_HEADS):
                need = tau_ref[h:h + 1, tok] - s0_ref[2 * h, r:r + 1, tok]
                picked = jnp.where(s_ref[2 * h + 1, :, tok] >= need, e1_ref[h, :, tok], 0.0)
                gate = gate + picked * ez_ref[h, r:r + 1, tok]
            ar = a_ref[rows, tok]
            act = 0.5 * ar * (1.0 + lax.erf(ar * (2.0 ** -0.5)))
            w_ref[rows, tok] = (gate * act).astype(jnp.bfloat16)
    acc_ref[...] += jnp.dot(vt_ref[0], w_ref[...], preferred_element_type=jnp.float32)

    @pl.when(e == pl.num_programs(1) - 1)
    def _():
        r = ALPHA * h_ref[...] + acc_ref[...].T
        mu = jnp.mean(r, axis=-1, keepdims=True)
        d = r - mu
        var = jnp.mean(d * d, axis=-1, keepdims=True)
        o_ref[...] = d * lax.rsqrt(var + LN_EPS) * g_ref[...] + b_ref[...]


def peer_tail(h, ht, q, sub_keys, u_bf, vt_bf, ln_g, ln_b, tt):
    n, d = h.shape
    s, e0z, e1, tau = peer_scores(q, sub_keys, tt)
    n_exp = u_bf.shape[0]
    return pl.pallas_call(
        _peer_dense_kernel,
        out_shape=jax.ShapeDtypeStruct((n, d), jnp.float32),
        grid=(n // tt, n_exp // PEER_TILE),
        in_specs=[pl.BlockSpec((d, tt), lambda i, e: (0, i)),
                  pl.BlockSpec((tt, d), lambda i, e: (i, 0)),
                  pl.BlockSpec((PEER_TILE, d), lambda i, e: (e, 0)),
                  pl.BlockSpec((1, d, PEER_TILE), lambda i, e: (e, 0, 0)),
                  pl.BlockSpec((PEER_COMBOS, PEER_KEY_ROWS, tt), lambda i, e: (0, e, i)),
                  pl.BlockSpec((PEER_HEADS, PEER_KEY_ROWS, tt), lambda i, e: (0, e, i)),
                  pl.BlockSpec((PEER_COMBOS, N_KEYS, tt), lambda i, e: (0, 0, i)),
                  pl.BlockSpec((PEER_HEADS, N_KEYS, tt), lambda i, e: (0, 0, i)),
                  pl.BlockSpec((PEER_HEADS, tt), lambda i, e: (0, i)),
                  pl.BlockSpec((1, d), lambda i, e: (0, 0)),
                  pl.BlockSpec((1, d), lambda i, e: (0, 0))],
        out_specs=pl.BlockSpec((tt, d), lambda i, e: (i, 0)),
        scratch_shapes=[pltpu.VMEM((d, tt), jnp.float32), pltpu.VMEM((PEER_TILE, tt), jnp.float32),
                        pltpu.VMEM((PEER_TILE, tt), jnp.bfloat16)],
        compiler_params=pltpu.CompilerParams(dimension_semantics=("arbitrary", "arbitrary"),
                                             vmem_limit_bytes=VMEM_LIMIT_BYTES),
        name="peer_dense",
    )(ht, h, u_bf, vt_bf, s, e0z, s, e1, tau, ln_g[None], ln_b[None])


def _out_proj_kernel(x_ref, nsa_ref, m_ref, wn_ref, wm_ref, g_ref, b_ref, wq_ref, h_ref, ht_ref, q_ref):
    f32, bf16 = jnp.float32, jnp.bfloat16
    r = (ALPHA * x_ref[...] + jnp.dot(nsa_ref[...].astype(bf16), wn_ref[...], preferred_element_type=f32)
         + jnp.dot(m_ref[...].astype(bf16), wm_ref[...], preferred_element_type=f32))
    mu = jnp.mean(r, axis=-1, keepdims=True)
    d = r - mu
    var = jnp.mean(d * d, axis=-1, keepdims=True)
    h = d * lax.rsqrt(var + LN_EPS) * g_ref[...] + b_ref[...]
    h_ref[...] = h
    ht_ref[...] = h.T.astype(bf16)
    q_ref[...] = jnp.dot(h.astype(bf16), wq_ref[...], preferred_element_type=f32)


def out_proj_fused(x, o_nsa, o_m, w_out, ln_g, ln_b, w_pq, tm):
    n, d = x.shape
    assert n % tm == 0
    bf16 = jnp.bfloat16
    nq = w_pq.shape[1]

    def rows(width):
        return pl.BlockSpec((tm, width), lambda i: (i, 0))

    def whole(a):
        return pl.BlockSpec(a.shape, lambda i: (0, 0))

    wn = w_out[:NSA_QW].astype(bf16)
    wm = w_out[NSA_QW:].astype(bf16)
    wq = w_pq.astype(bf16)
    g, b = ln_g[None], ln_b[None]
    return pl.pallas_call(
        _out_proj_kernel,
        out_shape=(jax.ShapeDtypeStruct((n, d), jnp.float32), jax.ShapeDtypeStruct((d, n), bf16),
                   jax.ShapeDtypeStruct((n, nq), jnp.float32)),
        grid=(n // tm,),
        in_specs=[rows(d), rows(NSA_QW), rows(M_W), whole(wn), whole(wm), whole(g), whole(b), whole(wq)],
        out_specs=(rows(d), pl.BlockSpec((d, tm), lambda i: (0, i)), rows(nq)),
        compiler_params=pltpu.CompilerParams(dimension_semantics=("arbitrary",),
                                             vmem_limit_bytes=VMEM_LIMIT_BYTES),
        name="out_proj",
    )(x, o_nsa, o_m, wn, wm, g, b, wq)


def block_tail(x, o_nsa, o_m, w_out, ln_g, ln_b, w_pq, sub_keys, u_bf, vt_bf, tt):
    lead = x.shape[:-1]
    h, ht, q = out_proj_fused(x.reshape(-1, D_MODEL), o_nsa.reshape(-1, NSA_QW), o_m.reshape(-1, M_W),
                              w_out, ln_g[0], ln_b[0], w_pq, tt)
    return peer_tail(h, ht, q, sub_keys, u_bf, vt_bf, ln_g[1], ln_b[1], tt).reshape(*lead, D_MODEL)


def prompt_mix(x, w_in, pe, w1, b1, w2, conv_w, conv_b, b_if):
    B, S, _ = x.shape
    z = in_proj_fused(x, w_in, jnp.arange(S), 512, True)
    kc = compress_chunks(z["k_cmp"], S // CMP_STRIDE, pe[0], w1[0], b1[0], w2[0])
    vc = compress_chunks(z["v_cmp"], S // CMP_STRIDE, pe[1], w1[1], b1[1], w2[1])
    zgate = z["zgate"].reshape(B, S, GATE_W)
    o_nsa = nsa_prompt_attention(z["q"].reshape(B, S, NSA_QW), kc, vc, z["k_slc_bf"], z["v_slc_bf"],
                                 z["k_win_bf"], z["v_win_bf"], zgate[..., :IN_SPLITS[2]])
    zqk = z["zqk"].reshape(B, S, 2 * M_W)
    o_m, C, n, m = mlstm_prompt(zqk, z["zv"].reshape(B, S, M_W), z["zo"].reshape(B, S, M_W),
                                zgate[..., IN_SPLITS[2]:], conv_w, conv_b, b_if)
    buf = zqk[:, S - (CONV_W - 1):]
    wl = min(WINDOW, S)
    k_cmp, v_cmp, k_slc, v_slc, k_win, v_win = [
        z[k + "_cache"].reshape(B, NSA_KV_HEADS, NSA_HD, S).transpose(0, 3, 1, 2)
        for k in ("k_cmp", "v_cmp", "k_slc", "v_slc", "k_win", "v_win")]
    return (o_nsa, o_m), (k_cmp, v_cmp, k_slc, v_slc, k_win[:, S - wl:], v_win[:, S - wl:], C, n, m, buf)


def sample_mix(x, kc_pool, vc_pool, ks_pool, vs_pool, kw_buf, vw_buf, C0, n0, m0, buf0, page_table,
               w_in, pe, w1, b1, w2, conv_w, conv_b, b_if):
    B, T, _ = x.shape
    past = page_table.shape[1] * PAGE_SIZE
    pos = past + jnp.arange(T)
    z = in_proj_fused(x, w_in, pos, B * T, False)
    q = z["q"].astype(jnp.float32).reshape(B, T, NSA_HEADS, NSA_HD) * (1.0 / ATTN_SCALE)
    k_cmp, v_cmp, k_slc, v_slc, k_win, v_win = [
        z[k].reshape(B, T, NSA_KV_HEADS, NSA_HD) for k in ("k_cmp", "v_cmp", "k_slc", "v_slc", "k_win", "v_win")]
    zgate = z["zgate"].reshape(B, T, GATE_W)
    gates = jax.nn.sigmoid(zgate[..., :IN_SPLITS[2]]).reshape(B, T, NSA_HEADS, 3)
    zqk, zv, zo, zif = (z["zqk"].reshape(B, T, 2 * M_W), z["zv"].reshape(B, T, M_W), z["zo"].reshape(B, T, M_W),
                        zgate[..., IN_SPLITS[2]:])

    assert (past + T) // CMP_STRIDE == past // CMP_STRIDE

    def compressed(pool, c):
        pages = pool.transpose(0, 2, 3, 1)[page_table]
        return compress_pages(pages, pe[c], w1[c], b1[c], w2[c])

    o_cmp, p = cmp_attend(q, pos, compressed(kc_pool, 0), compressed(vc_pool, 1))
    n_sel = -(-(past + T) // SEL_BLOCK)
    member = select_blocks(p, pos, n_sel)
    o_sel = sample_selected_attention(q, pos, member, ks_pool, vs_pool, k_slc, v_slc, page_table)
    wb = kw_buf.shape[1]
    kw = jnp.concatenate([kw_buf, k_win], axis=1)
    vw = jnp.concatenate([vw_buf, v_win], axis=1)
    kpos = past - wb + jnp.arange(wb + T)
    o_win = win_attend(q, pos, kw, vw, kpos)
    o_nsa = nsa_combine(gates, o_cmp, o_sel, o_win)
    o_m, (C, n, m, buf) = mlstm_mix(zqk, zv, zo, zif, buf0, C0, n0, m0, conv_w, conv_b, b_if, T)
    return (o_nsa, o_m), (k_cmp, v_cmp, k_slc, v_slc, kw[:, T:], vw[:, T:], C, n, m, buf)


def kernel(x_prompt, x_sample, cache_k_cmp, cache_v_cmp, cache_k_slc, cache_v_slc, cache_k_win, cache_v_win,
           state_C, state_n, state_m, state_conv, page_table, w_in, w_out, w_phi1, b_phi1, w_phi2, pe_cmp,
           conv_w, conv_b, b_if, ln_g, ln_b, w_pq, sub_keys, u_tab, v_tab):
    l = 0
    mix_p, st_p = prompt_mix(x_prompt, w_in[l], pe_cmp[l], w_phi1[l], b_phi1[l], w_phi2[l],
                             conv_w[l], conv_b[l], b_if[l])
    mix_s, st_s = sample_mix(x_sample, cache_k_cmp[l], cache_v_cmp[l], cache_k_slc[l], cache_v_slc[l],
                             cache_k_win[l], cache_v_win[l], state_C[l], state_n[l], state_m[l],
                             state_conv[l], page_table, w_in[l], pe_cmp[l], w_phi1[l], b_phi1[l],
                             w_phi2[l], conv_w[l], conv_b[l], b_if[l])
    u_bf = u_tab[l].astype(jnp.bfloat16)
    vt_bf = v_tab[l].astype(jnp.bfloat16).reshape(-1, PEER_TILE, D_MODEL).transpose(0, 2, 1)
    xp = block_tail(x_prompt, *mix_p, w_out[l], ln_g[l], ln_b[l], w_pq[l], sub_keys[l], u_bf, vt_bf, 512)
    xs = block_tail(x_sample, *mix_s, w_out[l], ln_g[l], ln_b[l], w_pq[l], sub_keys[l], u_bf, vt_bf, 128)
    return (xp, xs) + tuple(a[None] for a in st_p) + tuple(a[None] for a in st_s)
```

```python
import functools

import jax
import jax.numpy as jnp
import numpy as np
from jax import lax
from jax.experimental import pallas as pl
from jax.experimental.pallas import tpu as pltpu

D_MODEL = 1024
DEPTH = 1
PAGE_SIZE = 128
NSA_HEADS = 8
NSA_KV_HEADS = 2
NSA_GROUP = NSA_HEADS // NSA_KV_HEADS
NSA_HD = 64
NSA_QW = NSA_HEADS * NSA_HD
NSA_KVW = NSA_KV_HEADS * NSA_HD
CMP_BLOCK = 32
CMP_STRIDE = 16
SEL_BLOCK = 64
SEL_TOP = 16
WINDOW = 512
Q_BLOCK = 64
ATTN_SCALE = NSA_HD ** -0.5
ROPE_THETA = 10000.0
M_HEADS = 4
M_HD = 128
M_W = M_HEADS * M_HD
M_CHUNK = 64
CONV_W = 4
PEER_HEADS = 8
N_KEYS = 128
PEER_TOPK = 16
PEER_QDIM = 256
PEER_BLOCK = 128
IN_SPLITS = (NSA_QW, 6 * NSA_KVW, 3 * NSA_HEADS, 2 * M_W, M_W, M_W, 2 * M_HEADS)
LN_EPS = 1e-5
ALPHA = (2 * DEPTH) ** 0.25

VMEM_LIMIT_BYTES = 56 * 1024 * 1024


def _mm_kernel(x_ref, w_ref, o_ref):
    o_ref[...] = jnp.dot(x_ref[...].astype(jnp.bfloat16), w_ref[...], preferred_element_type=jnp.float32)


def pallas_matmul(x, w, tm=512):
    M, K = x.shape
    N = w.shape[1]
    tm = min(tm, M)
    assert M % tm == 0
    return pl.pallas_call(
        _mm_kernel,
        out_shape=jax.ShapeDtypeStruct((M, N), jnp.float32),
        grid=(M // tm,),
        in_specs=[pl.BlockSpec((tm, K), lambda i: (i, 0)), pl.BlockSpec((K, N), lambda i: (0, 0))],
        out_specs=pl.BlockSpec((tm, N), lambda i: (i, 0)),
        compiler_params=pltpu.CompilerParams(dimension_semantics=("arbitrary",),
                                             vmem_limit_bytes=VMEM_LIMIT_BYTES),
        name="proj_matmul",
    )(x, w.astype(jnp.bfloat16))


def mm3(x, w):
    lead = x.shape[:-1]
    return pallas_matmul(x.reshape(-1, x.shape[-1]), w).reshape(*lead, w.shape[1])


def layer_norm(x, g, b):
    mu = x.mean(-1, keepdims=True)
    var = jnp.square(x - mu).mean(-1, keepdims=True)
    return (x - mu) * lax.rsqrt(var + LN_EPS) * g + b


def rope(x, pos):
    half = x.shape[-1] // 2
    inv = ROPE_THETA ** (-jnp.arange(half, dtype=jnp.float32) / half)
    ang = pos.astype(jnp.float32)[:, None] * inv[None, :]
    cos = jnp.cos(ang)[:, None, :]
    sin = jnp.sin(ang)[:, None, :]
    x1, x2 = x[..., :half], x[..., half:]
    return jnp.concatenate([x1 * cos - x2 * sin, x2 * cos + x1 * sin], axis=-1)


def split_in_proj(x, w_in):
    z = mm3(x, w_in)
    cuts = [int(c) for c in np.cumsum(IN_SPLITS)[:-1]]
    return jnp.split(z, cuts, axis=-1)


_IN_OFF = np.concatenate([[0], np.cumsum(IN_SPLITS)])
_IN_ORDER = (0, 1, 3, 4, 5, 2, 6)
_N_KV_ROWS = 6
_KV_BF16 = (2, 3, 4, 5)
_KV_CACHE = (0, 1, 2, 3, 4, 5)
GATE_W = IN_SPLITS[2] + IN_SPLITS[6]


def _rope_pairs(x, cos, sin_signed):
    half = NSA_HD // 2
    lane = lax.broadcasted_iota(jnp.int32, x.shape, 1)
    partner = jnp.where(lane % NSA_HD < half, pltpu.roll(x, LANES - half, 1), pltpu.roll(x, half, 1))
    return x * cos + partner * sin_signed


def _in_proj_kernel(x_ref, w_ref, cos_ref, sin_ref, q_ref, *rest, kv_major):
    kv_refs = rest[:_N_KV_ROWS]
    rest = rest[_N_KV_ROWS:]
    if kv_major:
        bf_refs, rest = rest[:len(_KV_BF16)], rest[len(_KV_BF16):]
        cache_refs, rest = rest[:len(_KV_CACHE)], rest[len(_KV_CACHE):]
    zqk_ref, zv_ref, zo_ref, zgate_ref = rest
    z = jnp.dot(x_ref[...].astype(jnp.bfloat16), w_ref[...], preferred_element_type=jnp.float32)
    cos = cos_ref[...]
    sin = sin_ref[...]
    for g in range(NSA_QW // LANES):
        sl = slice(g * LANES, (g + 1) * LANES)
        q_ref[:, sl] = (_rope_pairs(z[:, sl], cos, sin) * ATTN_SCALE).astype(jnp.bfloat16)
    for r in range(_N_KV_ROWS):
        row = z[:, NSA_QW + r * NSA_KVW:NSA_QW + (r + 1) * NSA_KVW]
        if r % 2 == 0:
            row = _rope_pairs(row, cos, sin)
        kv_refs[r][...] = row
        if kv_major and r in _KV_CACHE:
            dst = cache_refs[_KV_CACHE.index(r)]
            dst[0] = row.T
        if kv_major and r in _KV_BF16:
            dst = bf_refs[_KV_BF16.index(r)]
            for n in range(NSA_KV_HEADS):
                dst[0, n] = row[:, n * NSA_HD:(n + 1) * NSA_HD].astype(jnp.bfloat16)
    o = NSA_QW + _N_KV_ROWS * NSA_KVW
    zqk_ref[...] = z[:, o:o + 2 * M_W]
    zv_ref[...] = z[:, o + 2 * M_W:o + 3 * M_W]
    zo_ref[...] = z[:, o + 3 * M_W:o + 4 * M_W]
    zgate_ref[...] = z[:, o + 4 * M_W:o + 4 * M_W + GATE_W]


def in_proj_fused(x, w_in, pos, tm, kv_major):
    B, T, D = x.shape
    M = B * T
    assert M % tm == 0 and NSA_KVW == LANES and (not kv_major or T % tm == 0)
    f32, bf16 = jnp.float32, jnp.bfloat16
    w = jnp.concatenate([w_in[:, _IN_OFF[i]:_IN_OFF[i + 1]] for i in _IN_ORDER], axis=1).astype(bf16)
    half = NSA_HD // 2
    inv = ROPE_THETA ** (-jnp.arange(half, dtype=f32) / half)
    ang = pos.astype(f32)[:, None] * inv[None, :]
    cos = jnp.tile(jnp.cos(ang), (B, 2 * LANES // NSA_HD))
    sin = jnp.tile(jnp.concatenate([-jnp.sin(ang), jnp.sin(ang)], axis=1), (B, LANES // NSA_HD))
    n_w = w.shape[1]

    def rows(width):
        return pl.BlockSpec((tm, width), lambda i: (i, 0))

    out_shape = [jax.ShapeDtypeStruct((M, NSA_QW), bf16)] + [jax.ShapeDtypeStruct((M, NSA_KVW), f32)] * _N_KV_ROWS
    out_specs = [rows(NSA_QW)] + [rows(NSA_KVW)] * _N_KV_ROWS
    if kv_major:
        per_seq = T // tm
        out_shape += [jax.ShapeDtypeStruct((B, NSA_KV_HEADS, T, NSA_HD), bf16)] * len(_KV_BF16)
        out_specs += [pl.BlockSpec((1, NSA_KV_HEADS, tm, NSA_HD),
                                   lambda i: (i // per_seq, 0, i % per_seq, 0))] * len(_KV_BF16)
        out_shape += [jax.ShapeDtypeStruct((B, NSA_KVW, T), f32)] * len(_KV_CACHE)
        out_specs += [pl.BlockSpec((1, NSA_KVW, tm), lambda i: (i // per_seq, 0, i % per_seq))] * len(_KV_CACHE)
    out_shape += [jax.ShapeDtypeStruct((M, 2 * M_W), f32), jax.ShapeDtypeStruct((M, M_W), f32),
                  jax.ShapeDtypeStruct((M, M_W), f32), jax.ShapeDtypeStruct((M, GATE_W), f32)]
    out_specs += [rows(2 * M_W), rows(M_W), rows(M_W), rows(GATE_W)]
    outs = pl.pallas_call(
        functools.partial(_in_proj_kernel, kv_major=kv_major),
        out_shape=tuple(out_shape),
        grid=(M // tm,),
        in_specs=[rows(D), pl.BlockSpec((D, n_w), lambda i: (0, 0)), rows(LANES), rows(LANES)],
        out_specs=tuple(out_specs),
        compiler_params=pltpu.CompilerParams(dimension_semantics=("arbitrary",),
                                             vmem_limit_bytes=VMEM_LIMIT_BYTES),
        name="in_proj",
    )(x.reshape(M, D), w, cos, sin)
    names = ["q", "k_cmp", "v_cmp", "k_slc", "v_slc", "k_win", "v_win"]
    if kv_major:
        names += ["k_slc_bf", "v_slc_bf", "k_win_bf", "v_win_bf"]
        names += ["k_cmp_cache", "v_cmp_cache", "k_slc_cache", "v_slc_cache", "k_win_cache", "v_win_cache"]
    names += ["zqk", "zv", "zo", "zgate"]
    return dict(zip(names, outs))


def nsa_project(zq, zkv, zg, pos):
    B, T, _ = zq.shape
    q = rope(zq.reshape(B, T, NSA_HEADS, NSA_HD), pos)
    kv = zkv.reshape(B, T, 6, NSA_KV_HEADS, NSA_HD)
    rows = (rope(kv[:, :, 0], pos), kv[:, :, 1], rope(kv[:, :, 2], pos), kv[:, :, 3],
            rope(kv[:, :, 4], pos), kv[:, :, 5])
    gates = jax.nn.sigmoid(zg).reshape(B, T, NSA_HEADS, 3)
    return q, rows, gates


def _expanded_w1(w1):
    assert CMP_BLOCK == 2 * CMP_STRIDE
    w1r = w1.reshape(2, CMP_STRIDE, NSA_HD, w1.shape[-1])
    wbig = jnp.einsum('hpdf,kn->pkdnhf', w1r, jnp.eye(NSA_KV_HEADS, dtype=w1.dtype))
    return wbig.reshape(CMP_STRIDE * NSA_KVW, 2 * NSA_KV_HEADS * w1.shape[-1])


def _compress_rows(x, w_ref, bias_ref, w2_ref, o_ref):
    f32, bf16 = jnp.float32, jnp.bfloat16
    rows = x.shape[0]
    f = w2_ref.shape[0]
    proj = jnp.dot(x.astype(bf16), w_ref[...], preferred_element_type=f32)
    for n in range(NSA_KV_HEADS):
        first = proj[:, 2 * n * f:(2 * n + 1) * f]
        second = pltpu.roll(proj[:, (2 * n + 1) * f:(2 * n + 2) * f], rows - 1, 0)
        pre = first + second + bias_ref[...]
        hid = 0.5 * pre * (1.0 + lax.erf(pre * (2.0 ** -0.5)))
        o_ref[:, n * NSA_HD:(n + 1) * NSA_HD] = jnp.dot(hid.astype(bf16), w2_ref[...], preferred_element_type=f32)


def _compress_chunks_kernel(x_ref, w_ref, bias_ref, w2_ref, o_ref):
    _compress_rows(x_ref[...], w_ref, bias_ref, w2_ref, o_ref)


def _compress_weights(pe, w1, b1, w2):
    bf16 = jnp.bfloat16
    bias = jnp.dot(pe.reshape(-1), w1, precision=lax.Precision.HIGHEST) + b1
    return _expanded_w1(w1).astype(bf16), bias[None], w2.astype(bf16)


def compress_chunks(rows, per_seq, pe, w1, b1, w2, tm=512):
    chunks = rows.reshape(-1, CMP_STRIDE * NSA_KVW)
    n = chunks.shape[0]
    tm = min(tm, n)
    assert n % tm == 0 and tm % per_seq == 0
    wbig, bias, w2b = _compress_weights(pe, w1, b1, w2)

    def whole(a):
        return pl.BlockSpec(a.shape, lambda i: (0, 0))

    out = pl.pallas_call(
        _compress_chunks_kernel,
        out_shape=jax.ShapeDtypeStruct((n, NSA_KVW), jnp.float32),
        grid=(n // tm,),
        in_specs=[pl.BlockSpec((tm, chunks.shape[1]), lambda i: (i, 0)), whole(wbig), whole(bias), whole(w2b)],
        out_specs=pl.BlockSpec((tm, NSA_KVW), lambda i: (i, 0)),
        compiler_params=pltpu.CompilerParams(dimension_semantics=("arbitrary",),
                                             vmem_limit_bytes=VMEM_LIMIT_BYTES),
        name="compress_chunks",
    )(chunks, wbig, bias, w2b)
    return out.reshape(n // per_seq, per_seq, NSA_KV_HEADS, NSA_HD)[:, :-1]


PAGE_GROUP = 4


def _compress_pages_kernel(pg_ref, w_ref, bias_ref, w2_ref, o_ref, x_ref, t_ref):
    n_pages = pg_ref.shape[1]
    per_page = PAGE_SIZE // CMP_STRIDE
    group = PAGE_GROUP

    def place(i, carry):
        for u in range(group):
            g = i * group + u
            t_ref[u] = pg_ref[0, g].reshape(NSA_KVW, PAGE_SIZE).T
            row0 = pl.multiple_of(g * per_page, per_page)
            for p in range(CMP_STRIDE):
                x_ref[pl.ds(row0, per_page), p * NSA_KVW:(p + 1) * NSA_KVW] = (
                    t_ref.at[u][pl.ds(p, per_page, stride=CMP_STRIDE), :])
        return carry

    lax.fori_loop(0, n_pages // group, place, 0)
    _compress_rows(x_ref[...], w_ref, bias_ref, w2_ref, o_ref.at[0])


def compress_pages(pages, pe, w1, b1, w2):
    B, n_pages = pages.shape[:2]
    assert pages.shape[2:] == (NSA_KV_HEADS, NSA_HD, PAGE_SIZE) and NSA_KVW == LANES and PAGE_SIZE == LANES
    assert n_pages % PAGE_GROUP == 0
    wbig, bias, w2b = _compress_weights(pe, w1, b1, w2)
    rows = n_pages * (PAGE_SIZE // CMP_STRIDE)

    def whole(a):
        return pl.BlockSpec(a.shape, lambda b: (0, 0))

    out = pl.pallas_call(
        _compress_pages_kernel,
        out_shape=jax.ShapeDtypeStruct((B, rows, NSA_KVW), jnp.float32),
        grid=(B,),
        in_specs=[pl.BlockSpec((1, n_pages, NSA_KV_HEADS, NSA_HD, PAGE_SIZE), lambda b: (b, 0, 0, 0, 0)),
                  whole(wbig), whole(bias), whole(w2b)],
        out_specs=pl.BlockSpec((1, rows, NSA_KVW), lambda b: (b, 0, 0)),
        scratch_shapes=[pltpu.VMEM((rows, wbig.shape[0]), jnp.float32),
                        pltpu.VMEM((PAGE_GROUP, PAGE_SIZE, NSA_KVW), jnp.float32)],
        compiler_params=pltpu.CompilerParams(dimension_semantics=("arbitrary",),
                                             vmem_limit_bytes=VMEM_LIMIT_BYTES),
        name="compress_pages",
    )(pages, wbig, bias, w2b)
    return out.reshape(B, rows, NSA_KV_HEADS, NSA_HD)[:, :-1]


def cmp_attend(q, qpos, kc, vc):
    B, T = q.shape[:2]
    qg = q.reshape(B, T, NSA_KV_HEADS, NSA_GROUP, NSA_HD)
    s = jnp.einsum('btngd,bcnd->btngc', qg, kc) * ATTN_SCALE
    nblk = kc.shape[1]
    blk_end = jnp.arange(nblk) * CMP_STRIDE + CMP_BLOCK - 1
    valid = (blk_end[None, :] <= qpos[:, None])[None, :, None, None, :]
    p = jax.nn.softmax(jnp.where(valid, s, -1e30), axis=-1) * valid
    o = jnp.einsum('btngc,bcnd->btngd', p, vc)
    return o.reshape(B, T, NSA_HEADS, NSA_HD), p


def select_blocks(p, qpos, n_sel):
    imp = p.sum(axis=3)
    R = SEL_BLOCK // CMP_STRIDE
    r = CMP_BLOCK // CMP_STRIDE
    nb = imp.shape[-1]
    right = n_sel * R + R - 1 - nb
    padded = jnp.pad(imp, ((0, 0), (0, 0), (0, 0), (r - 1, right)))
    score = padded[..., 0:(n_sel - 1) * R + 1:R]
    for o in range(1, R + r - 1):
        score = score + padded[..., o:o + (n_sel - 1) * R + 1:R]
    j = jnp.arange(n_sel)[None, :]
    cur = (qpos // SEL_BLOCK)[:, None]
    valid = (j * SEL_BLOCK <= qpos[:, None])[None, :, None, :]
    forced = ((j == 0) | (j == cur) | (j == cur - 1))[None, :, None, :]
    score = jnp.where(forced, jnp.inf, jnp.where(valid, score, -jnp.inf))
    idx = j[0]
    before = (score[..., None, :] > score[..., :, None]) | ((score[..., None, :] == score[..., :, None])
                                                          & (idx[None, :] < idx[:, None]))
    return before.sum(-1) < min(SEL_TOP, n_sel)


def sample_selected_attention(q, qpos, member, k_pool, v_pool, k_new, v_new, page_table):
    B, T = q.shape[:2]
    n_pages = page_table.shape[1]
    per_page = PAGE_SIZE // SEL_BLOCK
    assert member.shape[-1] == n_pages * per_page + 1 and T <= SEL_BLOCK
    kp = k_pool.transpose(0, 2, 3, 1)[page_table]
    vp = v_pool.transpose(0, 2, 3, 1)[page_table]
    qg = q.reshape(B, T, NSA_KV_HEADS, NSA_GROUP, NSA_HD)
    s_past = jnp.einsum('btngd,bpndk->bntgpk', qg, kp) * ATTN_SCALE
    s_new = jnp.einsum('btngd,bsnd->bntgs', qg, k_new) * ATTN_SCALE
    m = member.transpose(0, 2, 1, 3)
    m_past = jnp.repeat(m[..., :-1].reshape(B, NSA_KV_HEADS, T, n_pages, per_page), SEL_BLOCK, axis=-1)
    kpos = (jnp.arange(n_pages) * PAGE_SIZE)[:, None] + jnp.arange(PAGE_SIZE)[None, :]
    m_past = m_past & (kpos[None, None, None] <= qpos[None, None, :, None, None])
    new_pos = n_pages * PAGE_SIZE + jnp.arange(T)
    m_new = m[..., -1:] & (new_pos[None, None, None, :] <= qpos[None, None, :, None])
    logits = jnp.concatenate(
        [jnp.where(m_past[:, :, :, None], s_past, -jnp.inf).reshape(B, NSA_KV_HEADS, T, NSA_GROUP, -1),
         jnp.where(m_new[:, :, :, None], s_new, -jnp.inf)], axis=-1)
    pr = jax.nn.softmax(logits, axis=-1)
    pr_past = pr[..., :n_pages * PAGE_SIZE].reshape(B, NSA_KV_HEADS, T, NSA_GROUP, n_pages, PAGE_SIZE)
    o = (jnp.einsum('bntgpk,bpndk->bntgd', pr_past, vp)
         + jnp.einsum('bntgs,bsnd->bntgd', pr[..., n_pages * PAGE_SIZE:], v_new))
    return o.transpose(0, 2, 1, 3, 4).reshape(B, T, NSA_HEADS, NSA_HD)


def to_blocks(rows, n_sel):
    B, L, KV, hd = rows.shape
    rows = jnp.pad(rows, ((0, 0), (0, n_sel * SEL_BLOCK - L), (0, 0), (0, 0)))
    return rows.reshape(B, n_sel, SEL_BLOCK, KV, hd).transpose(0, 3, 1, 2, 4)


def take_rows(table, idx):
    return table[idx]


def sel_attend(q, qpos, sel, kb, vb):
    B, Tq = q.shape[:2]
    k = sel.shape[-1]
    sel_t = sel.transpose(0, 2, 1, 3)
    gather = jax.vmap(jax.vmap(take_rows))
    kg = gather(kb, sel_t).reshape(B, NSA_KV_HEADS, Tq, k * SEL_BLOCK, NSA_HD)
    vg = gather(vb, sel_t).reshape(B, NSA_KV_HEADS, Tq, k * SEL_BLOCK, NSA_HD)
    kpos = (sel_t[..., None] * SEL_BLOCK + jnp.arange(SEL_BLOCK)).reshape(B, NSA_KV_HEADS, Tq, k * SEL_BLOCK)
    qg = q.reshape(B, Tq, NSA_KV_HEADS, NSA_GROUP, NSA_HD).transpose(0, 2, 1, 3, 4)
    s = jnp.einsum('bntgd,bntsd->bntgs', qg, kg) * ATTN_SCALE
    mask = kpos[:, :, :, None, :] <= qpos[None, None, :, None, None]
    pr = jax.nn.softmax(jnp.where(mask, s, -jnp.inf), axis=-1)
    o = jnp.einsum('bntgs,bntsd->bntgd', pr, vg)
    return o.transpose(0, 2, 1, 3, 4).reshape(B, Tq, NSA_HEADS, NSA_HD)


def win_attend(q, qpos, k, v, kpos):
    B, Tq = q.shape[:2]
    qg = q.reshape(B, Tq, NSA_KV_HEADS, NSA_GROUP, NSA_HD)
    s = jnp.einsum('btngd,bsnd->btngs', qg, k) * ATTN_SCALE
    diff = qpos[:, None] - kpos[None, :]
    mask = ((diff >= 0) & (diff < WINDOW) & (kpos[None, :] >= 0))[None, :, None, None, :]
    pr = jax.nn.softmax(jnp.where(mask, s, -jnp.inf), axis=-1)
    o = jnp.einsum('btngs,bsnd->btngd', pr, v)
    return o.reshape(B, Tq, NSA_HEADS, NSA_HD)


def nsa_combine(gates, o_cmp, o_sel, o_win):
    B, T = gates.shape[:2]
    o = gates[..., 0:1] * o_cmp + gates[..., 1:2] * o_sel + gates[..., 2:3] * o_win
    return o.reshape(B, T, NSA_QW)


NSA_TQ = 128
NSA_CK = 512
MASKED = -1e30


def _softmax_rows(s):
    m = jnp.max(s, axis=-1, keepdims=True)
    e = jnp.exp(s - m)
    return e / jnp.sum(e, axis=-1, keepdims=True)


def _nsa_prompt_kernel(q_ref, kc_ref, vc_ref, ks_ref, vs_ref, kw_ref, vw_ref, zg_ref, msel_ref, exp_ref, o_ref):
    f32, bf16 = jnp.float32, jnp.bfloat16
    tq = NSA_TQ
    q0 = pl.program_id(2) * tq
    qb = q_ref[0]
    qs = jnp.concatenate([qb[:, g * NSA_HD:(g + 1) * NSA_HD] for g in range(NSA_GROUP)], axis=0)
    tpos = q0 + lax.broadcasted_iota(jnp.int32, (tq, 1), 0)

    def per_head(a):
        return jnp.concatenate([a] * NSA_GROUP, axis=0)

    s = lax.dot_general(qs, kc_ref[0, 0], _NT, preferred_element_type=f32)
    cblk = lax.broadcasted_iota(jnp.int32, (tq, 128), 1)
    cvalid = cblk * CMP_STRIDE + (CMP_BLOCK - 1) <= tpos
    s = s + per_head(jnp.where(cvalid, 0.0, MASKED))
    e = jnp.exp(s - jnp.max(s, axis=-1, keepdims=True)) * per_head(jnp.where(cvalid, 1.0, 0.0))
    l = jnp.sum(e, axis=-1, keepdims=True)
    p = e / jnp.where(l > 0.0, l, 1.0)
    o_cmp = jnp.dot(p.astype(bf16), vc_ref[0, 0], preferred_element_type=f32)

    imp = p[0:tq]
    for g in range(1, NSA_GROUP):
        imp = imp + p[g * tq:(g + 1) * tq]
    hi = imp.astype(bf16)
    r1 = imp - hi.astype(f32)
    mid = r1.astype(bf16)
    lo = (r1 - mid.astype(f32)).astype(bf16)
    msel = msel_ref[...]
    score = (lax.dot_general(msel, hi, _NT, preferred_element_type=f32)
             + lax.dot_general(msel, mid, _NT, preferred_element_type=f32)
             + lax.dot_general(msel, lo, _NT, preferred_element_type=f32))
    n_sel = score.shape[0]
    j = lax.broadcasted_iota(jnp.int32, (n_sel, tq), 0)
    tok = q0 + lax.broadcasted_iota(jnp.int32, (n_sel, tq), 1)
    cur = tok // SEL_BLOCK
    forced = (j == 0) | (j == cur) | (j == cur - 1)
    score = jnp.where(forced, jnp.inf, jnp.where(j * SEL_BLOCK <= tok, score, -jnp.inf))
    rank = jnp.zeros((n_sel, tq), f32)
    for jp in range(n_sel):
        row = score[jp:jp + 1, :]
        before = (row > score) | ((row == score) & (j > jp))
        rank = rank + jnp.where(before, 1.0, 0.0)
    chosen_t = jnp.where(rank < SEL_TOP, 1.0, 0.0)
    sel01 = jnp.concatenate([chosen_t, jnp.zeros((LANES - n_sel, tq), f32)], axis=0).T.astype(bf16)

    ck = NSA_CK
    rows = NSA_GROUP * tq

    def sel_chunk(c, carry):
        m, l, acc = carry
        k0 = pl.multiple_of(c * ck, ck)
        s = lax.dot_general(qs, ks_ref[0, 0, pl.ds(k0, ck), :], _NT, preferred_element_type=f32)
        chosen = jnp.dot(sel01, exp_ref[c], preferred_element_type=f32)
        kpos = k0 + lax.broadcasted_iota(jnp.int32, (tq, ck), 1)
        ok = (chosen > 0.5) & (kpos <= tpos)
        s = s + per_head(jnp.where(ok, 0.0, MASKED))
        m_new = jnp.maximum(m, jnp.max(s, axis=-1, keepdims=True))
        a = jnp.exp(m - m_new)
        pr = jnp.exp(s - m_new)
        l = a * l + jnp.sum(pr, axis=-1, keepdims=True)
        acc = a * acc + jnp.dot(pr.astype(bf16), vs_ref[0, 0, pl.ds(k0, ck), :], preferred_element_type=f32)
        return m_new, l, acc

    init = (jnp.full((rows, 1), MASKED, f32), jnp.zeros((rows, 1), f32), jnp.zeros((rows, NSA_HD), f32))
    n_chunks = (q0 + tq + ck - 1) // ck
    _, l_sel, acc_sel = lax.fori_loop(0, n_chunks, sel_chunk, init)
    o_sel = acc_sel / l_sel

    w0 = pl.multiple_of(jnp.maximum(q0 - WINDOW, 0), tq)
    wl = WINDOW + tq
    s = lax.dot_general(qs, kw_ref[0, 0, pl.ds(w0, wl), :], _NT, preferred_element_type=f32)
    diff = tpos - (w0 + lax.broadcasted_iota(jnp.int32, (tq, wl), 1))
    s = s + per_head(jnp.where((diff >= 0) & (diff < WINDOW), 0.0, MASKED))
    o_win = jnp.dot(_softmax_rows(s).astype(bf16), vw_ref[0, 0, pl.ds(w0, wl), :], preferred_element_type=f32)

    gates = jax.nn.sigmoid(zg_ref[0, 0])
    for g in range(NSA_GROUP):
        r = slice(g * tq, (g + 1) * tq)
        o_ref[0, :, g * NSA_HD:(g + 1) * NSA_HD] = (gates[:, 3 * g:3 * g + 1] * o_cmp[r]
                                                    + gates[:, 3 * g + 1:3 * g + 2] * o_sel[r]
                                                    + gates[:, 3 * g + 2:3 * g + 3] * o_win[r])


def nsa_prompt_attention(qs, kc, vc, k_slc, v_slc, k_win, v_win, zg):
    B, S = qs.shape[:2]
    bf16 = jnp.bfloat16
    assert S % NSA_CK == 0 and S % NSA_TQ == 0 and WINDOW % NSA_TQ == 0 and WINDOW + NSA_TQ <= S
    n_sel = S // SEL_BLOCK
    nb = kc.shape[1]
    assert nb <= 128

    def pad_blocks(a):
        return jnp.pad(a.transpose(0, 2, 1, 3).astype(bf16), ((0, 0), (0, 0), (0, 128 - nb), (0, 0)))

    zg4 = zg.reshape(B, S, NSA_KV_HEADS, 3 * NSA_GROUP).transpose(0, 2, 1, 3)
    c = np.arange(128)[:, None]
    jj = np.arange(n_sel)[None, :]
    ratio = SEL_BLOCK // CMP_STRIDE
    msel = ((c >= jj * ratio - (CMP_BLOCK // CMP_STRIDE - 1)) & (c <= jj * ratio + ratio - 1) & (c < nb))
    assert n_sel <= LANES and NSA_TQ == LANES
    expand = (np.arange(S)[None, :] // SEL_BLOCK == np.arange(LANES)[:, None])
    expand = expand.reshape(LANES, S // NSA_CK, NSA_CK).transpose(1, 0, 2)
    row_spec = pl.BlockSpec((1, 1, S, NSA_HD), lambda b, n, i: (b, n, 0, 0))
    blk_spec = pl.BlockSpec((1, 1, 128, NSA_HD), lambda b, n, i: (b, n, 0, 0))
    return pl.pallas_call(
        _nsa_prompt_kernel,
        out_shape=jax.ShapeDtypeStruct((B, S, NSA_QW), jnp.float32),
        grid=(B, NSA_KV_HEADS, S // NSA_TQ),
        in_specs=[pl.BlockSpec((1, NSA_TQ, NSA_GROUP * NSA_HD), lambda b, n, i: (b, i, n)),
                  blk_spec, blk_spec, row_spec, row_spec, row_spec, row_spec,
                  pl.BlockSpec((1, 1, NSA_TQ, 3 * NSA_GROUP), lambda b, n, i: (b, n, i, 0)),
                  pl.BlockSpec((n_sel, 128), lambda b, n, i: (0, 0)),
                  pl.BlockSpec((S // NSA_CK, LANES, NSA_CK), lambda b, n, i: (0, 0, 0))],
        out_specs=pl.BlockSpec((1, NSA_TQ, NSA_GROUP * NSA_HD), lambda b, n, i: (b, i, n)),
        compiler_params=pltpu.CompilerParams(dimension_semantics=("arbitrary", "arbitrary", "arbitrary"),
                                             vmem_limit_bytes=VMEM_LIMIT_BYTES),
        name="nsa_prompt_attention",
    )(qs, pad_blocks(kc), pad_blocks(vc), k_slc, v_slc, k_win, v_win,
      zg4, jnp.asarray(msel.T, bf16), jnp.asarray(expand, bf16))


MLSTM_L = 128
CONV_HALO = 8


def _log_sigmoid(x):
    return -(jnp.maximum(-x, 0.0) + jnp.log1p(jnp.exp(-jnp.abs(x))))


def _mlstm_prompt_kernel(x_ref, xprev_ref, halo0_ref, v_ref, o_ref, gcol_ref, grow_ref, cw_ref, cb_ref,
                         out_ref, c_out, n_out, m_out, c_ref, n_ref, m_ref):
    f32, bf16 = jnp.float32, jnp.bfloat16
    c = pl.program_id(1)
    L = MLSTM_L

    @pl.when(c == 0)
    def _():
        c_ref[...] = jnp.zeros_like(c_ref)
        n_ref[...] = jnp.zeros_like(n_ref)
        m_ref[...] = jnp.zeros_like(m_ref)

    x = x_ref[0]
    halo = jnp.where(c == 0, halo0_ref[0], xprev_ref[0, L - CONV_HALO:L, :])
    ext = jnp.concatenate([halo, x], axis=0)
    conv = cb_ref[...]
    for j in range(CONV_W):
        o = CONV_HALO - (CONV_W - 1) + j
        conv = conv + ext[o:o + L] * cw_ref[j:j + 1, :]
    qk = conv * jax.nn.sigmoid(conv)

    t_id = lax.broadcasted_iota(jnp.int32, (L, L), 0)
    s_id = lax.broadcasted_iota(jnp.int32, (L, L), 1)
    causal = t_id >= s_id
    gcol = gcol_ref[0, 0]
    grow = grow_ref[0, 0]
    for h in range(M_HEADS):
        hd = slice(h * M_HD, (h + 1) * M_HD)
        q = qk[:, hd]
        k = qk[:, M_W + h * M_HD:M_W + (h + 1) * M_HD] * (M_HD ** -0.5)
        v = v_ref[0, :, hd].astype(bf16)
        ig_r = grow[h:h + 1, :]
        ig_c = gcol[:, h:h + 1]
        lf_r = _log_sigmoid(grow[M_HEADS + h:M_HEADS + h + 1, :])
        lf_c = _log_sigmoid(gcol[:, M_HEADS + h:M_HEADS + h + 1])
        b_c = jnp.sum(jnp.where(causal, lf_r, 0.0), axis=1, keepdims=True)
        b_r = jnp.sum(jnp.where(t_id <= s_id, lf_c, 0.0), axis=0, keepdims=True)
        m_prev = m_ref[h]
        dmat = jnp.where(causal, b_c - b_r + ig_r, -jnp.inf)
        inter = b_c + m_prev
        m_t = jnp.maximum(inter, jnp.max(dmat, axis=1, keepdims=True))
        w_intra = jnp.exp(dmat - m_t)
        w_inter = jnp.exp(inter - m_t)
        qb = q.astype(bf16)
        s = lax.dot_general(qb, k.astype(bf16), _NT, preferred_element_type=f32) * w_intra
        num = (jnp.dot(s.astype(bf16), v, preferred_element_type=f32)
               + w_inter * jnp.dot(qb, c_ref[h].astype(bf16), preferred_element_type=f32))
        den = jnp.sum(s, axis=1, keepdims=True) + w_inter * jnp.sum(q * n_ref[h], axis=1, keepdims=True)
        hh = num / jnp.maximum(jnp.abs(den), jnp.exp(-m_t))
        out_ref[0, :, hd] = jax.nn.sigmoid(o_ref[0, :, hd]) * hh
        m_new = m_t[L - 1:L]
        b_last = b_c[L - 1:L]
        w_s = jnp.exp(b_last - b_c + ig_c - m_new)
        w_p = jnp.exp(b_last + m_prev - m_new)
        kw = k * w_s
        c_ref[h] = w_p * c_ref[h] + jnp.dot(kw.T.astype(bf16), v, preferred_element_type=f32)
        n_ref[h] = w_p * n_ref[h] + jnp.sum(kw, axis=0, keepdims=True)
        m_ref[h] = m_new

    @pl.when(c == pl.num_programs(1) - 1)
    def _():
        c_out[0] = c_ref[...]
        n_out[0] = n_ref[...]
        m_out[0] = m_ref[...]


def mlstm_prompt(zqk, zv, zo, zif, conv_w, conv_b, b_if):
    B, T, _ = zqk.shape
    L = MLSTM_L
    assert T % L == 0
    nc = T // L
    f32 = jnp.float32
    gif = zif + b_if
    gcol = gif.reshape(B, nc, L, 2 * M_HEADS)
    grow = gcol.transpose(0, 1, 3, 2)
    halo0 = jnp.zeros((B, CONV_HALO, 2 * M_W), f32)
    out, C, n, m = pl.pallas_call(
        _mlstm_prompt_kernel,
        out_shape=(jax.ShapeDtypeStruct((B, T, M_W), f32),
                   jax.ShapeDtypeStruct((B, M_HEADS, M_HD, M_HD), f32),
                   jax.ShapeDtypeStruct((B, M_HEADS, 1, M_HD), f32),
                   jax.ShapeDtypeStruct((B, M_HEADS, 1, 1), f32)),
        grid=(B, nc),
        in_specs=[pl.BlockSpec((1, L, 2 * M_W), lambda b, c: (b, c, 0)),
                  pl.BlockSpec((1, L, 2 * M_W), lambda b, c: (b, jnp.maximum(c - 1, 0), 0)),
                  pl.BlockSpec((1, CONV_HALO, 2 * M_W), lambda b, c: (b, 0, 0)),
                  pl.BlockSpec((1, L, M_W), lambda b, c: (b, c, 0)),
                  pl.BlockSpec((1, L, M_W), lambda b, c: (b, c, 0)),
                  pl.BlockSpec((1, 1, L, 2 * M_HEADS), lambda b, c: (b, c, 0, 0)),
                  pl.BlockSpec((1, 1, 2 * M_HEADS, L), lambda b, c: (b, c, 0, 0)),
                  pl.BlockSpec((CONV_W, 2 * M_W), lambda b, c: (0, 0)),
                  pl.BlockSpec((1, 2 * M_W), lambda b, c: (0, 0))],
        out_specs=(pl.BlockSpec((1, L, M_W), lambda b, c: (b, c, 0)),
                   pl.BlockSpec((1, M_HEADS, M_HD, M_HD), lambda b, c: (b, 0, 0, 0)),
                   pl.BlockSpec((1, M_HEADS, 1, M_HD), lambda b, c: (b, 0, 0, 0)),
                   pl.BlockSpec((1, M_HEADS, 1, 1), lambda b, c: (b, 0, 0, 0))),
        scratch_shapes=[pltpu.VMEM((M_HEADS, M_HD, M_HD), f32), pltpu.VMEM((M_HEADS, 1, M_HD), f32),
                        pltpu.VMEM((M_HEADS, 1, 1), f32)],
        compiler_params=pltpu.CompilerParams(dimension_semantics=("arbitrary", "arbitrary"),
                                             vmem_limit_bytes=VMEM_LIMIT_BYTES),
        name="mlstm_prompt",
    )(zqk, zqk, halo0, zv, zo, gcol, grow, conv_w, conv_b[None])
    return out, C, n.reshape(B, M_HEADS, M_HD), m.reshape(B, M_HEADS)


def mlstm_chunk(carry, inp):
    C, n, m = carry
    q, k, v, ig, lf = inp
    L = q.shape[2]
    b = jnp.cumsum(lf, axis=-1)
    causal = jnp.tril(jnp.ones((L, L), dtype=bool))
    dmat = jnp.where(causal, b[..., :, None] - b[..., None, :] + ig[..., None, :], -jnp.inf)
    inter = b + m[..., None]
    m_t = jnp.maximum(inter, dmat.max(axis=-1))
    w_intra = jnp.exp(dmat - m_t[..., None])
    w_inter = jnp.exp(inter - m_t)
    s = jnp.einsum('bhtd,bhsd->bhts', q, k) * w_intra
    num = jnp.einsum('bhts,bhsv->bhtv', s, v) + w_inter[..., None] * jnp.einsum('bhtd,bhdv->bhtv', q, C)
    den = s.sum(-1) + w_inter * jnp.einsum('bhtd,bhd->bht', q, n)
    h = num / jnp.maximum(jnp.abs(den), jnp.exp(-m_t))[..., None]
    m_new = m_t[..., -1]
    w_s = jnp.exp(b[..., -1:] - b + ig - m_new[..., None])
    w_p = jnp.exp(b[..., -1] + m - m_new)
    C_new = w_p[..., None, None] * C + jnp.einsum('bhs,bhsd,bhsv->bhdv', w_s, k, v)
    n_new = w_p[..., None] * n + jnp.einsum('bhs,bhsd->bhd', w_s, k)
    return (C_new, n_new, m_new), h


def mlstm_mix(zqk, zv, zo, zif, buf0, C0, n0, m0, conv_w, conv_b, b_if, chunk):
    B, T, _ = zqk.shape
    full = jnp.concatenate([buf0, zqk], axis=1)
    conv = conv_b
    for j in range(CONV_W):
        conv = conv + full[:, j:j + T] * conv_w[j]
    qk = jax.nn.silu(conv)

    def heads(a):
        return a.reshape(B, T, M_HEADS, M_HD).transpose(0, 2, 1, 3)

    q = heads(qk[..., :M_W])
    k = heads(qk[..., M_W:]) * (M_HD ** -0.5)
    v = heads(zv)
    gif = zif + b_if
    ig = gif[..., :M_HEADS].transpose(0, 2, 1)
    lf = jax.nn.log_sigmoid(gif[..., M_HEADS:]).transpose(0, 2, 1)
    nc = T // chunk

    def to_chunks(a):
        return jnp.moveaxis(a.reshape(B, M_HEADS, nc, chunk, *a.shape[3:]), 2, 0)

    (C, n, m), h = lax.scan(mlstm_chunk, (C0, n0, m0),
                            (to_chunks(q), to_chunks(k), to_chunks(v), to_chunks(ig), to_chunks(lf)))
    h = jnp.moveaxis(h, 0, 2).reshape(B, M_HEADS, T, M_HD).transpose(0, 2, 1, 3).reshape(B, T, M_W)
    out = jax.nn.sigmoid(zo) * h
    return out, (C, n, m, full[:, T:])


PEER_COMBOS = 2 * PEER_HEADS
PEER_KEY_ROWS = 8
PEER_TILE = PEER_KEY_ROWS * N_KEYS
PEER_TS_ROWS = 24
LANES = 128
_NT = (((1,), (1,)), ((), ()))


def _peer_topk_kernel(q_ref, keys_ref, s_ref, e0_ref, e1_ref, tau_ref, ts_ref):
    head = pl.program_id(1)
    tt = q_ref.shape[0]
    half_w = PEER_QDIM // 2
    for half in range(2):
        c = 2 * head + half
        qh = q_ref[:, half * half_w:(half + 1) * half_w].astype(jnp.bfloat16)
        s = lax.dot_general(keys_ref[half], qh, _NT, preferred_element_type=jnp.float32)
        s_ref[c] = s
        pad_rows = jnp.full((PEER_TS_ROWS - PEER_TOPK - 1, tt), -jnp.inf, jnp.float32)
        work = s
        rows = []
        for _ in range(PEER_TOPK + 1):
            m = jnp.max(work, axis=0, keepdims=True)
            work = jnp.where(work == m, -jnp.inf, work)
            rows.append(m)
        ts_ref[c] = jnp.concatenate(rows + [pad_rows], axis=0)
        removed = jnp.sum(jnp.where(work == -jnp.inf, 1.0, 0.0), axis=0, keepdims=True)
        has_ties = jnp.max(removed) > PEER_TOPK + 1

        @pl.when(has_ties)
        def _(s=s, c=c):
            key_id = lax.broadcasted_iota(jnp.int32, s.shape, 0)
            work = s
            rows = []
            for _ in range(PEER_TOPK + 1):
                m = jnp.max(work, axis=0, keepdims=True)
                first = jnp.min(jnp.where(work == m, key_id, N_KEYS), axis=0, keepdims=True)
                work = jnp.where(key_id == first, -jnp.inf, work)
                rows.append(m)
            ts_ref[c] = jnp.concatenate(rows + [pad_rows], axis=0)

    @pl.when(head == PEER_HEADS - 1)
    def _():
        for h in range(PEER_HEADS):
            t0 = ts_ref[2 * h]
            t1 = ts_ref[2 * h + 1]
            pieces = [t0[0:1] + t1] + [t0[a:a + 1] + t1[0:8] for a in range(1, 8)] + [t0[8:24] + t1[0:1]]
            cand = jnp.concatenate(pieces, axis=0)
            top = t0[0:1] + t1[0:1]
            v16 = top
            v17 = top
            z = jnp.zeros_like(top)
            seen = jnp.zeros_like(top)
            for _ in range(PEER_TOPK + 1):
                m = jnp.max(cand, axis=0, keepdims=True)
                eq = cand == m
                cnt = jnp.sum(jnp.where(eq, 1.0, 0.0), axis=0, keepdims=True)
                active = seen < PEER_TOPK
                take = jnp.minimum(cnt, PEER_TOPK - seen)
                v16 = jnp.where(active, m, v16)
                v17 = jnp.where(seen < PEER_TOPK + 1, m, v17)
                z = z + jnp.where(active, take * jnp.exp(m - top), 0.0)
                seen = seen + cnt
                cand = jnp.where(eq, -jnp.inf, cand)
            tau_ref[h:h + 1, :] = 0.5 * v16 + 0.5 * v17
            e0_ref[h] = jnp.exp(s_ref[2 * h] - t0[0:1]) / z
            e1_ref[h] = jnp.exp(s_ref[2 * h + 1] - t1[0:1])


def peer_scores(q, sub_keys, tt):
    n = q.shape[0]
    assert n % tt == 0
    keys = sub_keys.reshape(PEER_COMBOS, N_KEYS, PEER_QDIM // 2).astype(jnp.bfloat16)
    f32 = jnp.float32
    return pl.pallas_call(
        _peer_topk_kernel,
        out_shape=(jax.ShapeDtypeStruct((PEER_COMBOS, N_KEYS, n), f32),
                   jax.ShapeDtypeStruct((PEER_HEADS, N_KEYS, n), f32),
                   jax.ShapeDtypeStruct((PEER_HEADS, N_KEYS, n), f32),
                   jax.ShapeDtypeStruct((PEER_HEADS, n), f32)),
        grid=(n // tt, PEER_HEADS),
        in_specs=[pl.BlockSpec((tt, PEER_QDIM), lambda i, c: (i, c)),
                  pl.BlockSpec((2, N_KEYS, PEER_QDIM // 2), lambda i, c: (c, 0, 0))],
        out_specs=(pl.BlockSpec((PEER_COMBOS, N_KEYS, tt), lambda i, c: (0, 0, i)),
                   pl.BlockSpec((PEER_HEADS, N_KEYS, tt), lambda i, c: (0, 0, i)),
                   pl.BlockSpec((PEER_HEADS, N_KEYS, tt), lambda i, c: (0, 0, i)),
                   pl.BlockSpec((PEER_HEADS, tt), lambda i, c: (0, i))),
        scratch_shapes=[pltpu.VMEM((PEER_COMBOS, PEER_TS_ROWS, tt), f32)],
        compiler_params=pltpu.CompilerParams(dimension_semantics=("arbitrary", "arbitrary"),
                                             vmem_limit_bytes=VMEM_LIMIT_BYTES),
        name="peer_topk",
    )(q, keys)


def _peer_dense_kernel(xt_ref, h_ref, u_ref, vt_ref, s0_ref, ez_ref, s_ref, e1_ref, tau_ref, g_ref, b_ref,
                       o_ref, acc_ref, a_ref, w_ref):
    e = pl.program_id(1)
    tt = xt_ref.shape[1]

    @pl.when(e == 0)
    def _():
        acc_ref[...] = jnp.zeros_like(acc_ref)

    a_ref[...] = jnp.dot(u_ref[...], xt_ref[...], preferred_element_type=jnp.float32)
    for r in range(PEER_KEY_ROWS):
        rows = slice(r * N_KEYS, (r + 1) * N_KEYS)
        for t in range(tt // LANES):
            tok = slice(t * LANES, (t + 1) * LANES)
            gate = jnp.zeros((N_KEYS, LANES), jnp.float32)
            for h in range(PEER_HEADS):
                need = tau_ref[h:h + 1, tok] - s0_ref[2 * h, r:r + 1, tok]
                picked = jnp.where(s_ref[2 * h + 1, :, tok] >= need, e1_ref[h, :, tok], 0.0)
                gate = gate + picked * ez_ref[h, r:r + 1, tok]
            ar = a_ref[rows, tok]
            act = 0.5 * ar * (1.0 + lax.erf(ar * (2.0 ** -0.5)))
            w_ref[rows, tok] = (gate * act).astype(jnp.bfloat16)
    acc_ref[...] += jnp.dot(vt_ref[0], w_ref[...], preferred_element_type=jnp.float32)

    @pl.when(e == pl.num_programs(1) - 1)
    def _():
        r = ALPHA * h_ref[...] + acc_ref[...].T
        mu = jnp.mean(r, axis=-1, keepdims=True)
        d = r - mu
        var = jnp.mean(d * d, axis=-1, keepdims=True)
        o_ref[...] = d * lax.rsqrt(var + LN_EPS) * g_ref[...] + b_ref[...]


def peer_tail(h, ht, q, sub_keys, u_bf, vt_bf, ln_g, ln_b, tt):
    n, d = h.shape
    s, e0z, e1, tau = peer_scores(q, sub_keys, tt)
    n_exp = u_bf.shape[0]
    return pl.pallas_call(
        _peer_dense_kernel,
        out_shape=jax.ShapeDtypeStruct((n, d), jnp.float32),
        grid=(n // tt, n_exp // PEER_TILE),
        in_specs=[pl.BlockSpec((d, tt), lambda i, e: (0, i)),
                  pl.BlockSpec((tt, d), lambda i, e: (i, 0)),
                  pl.BlockSpec((PEER_TILE, d), lambda i, e: (e, 0)),
                  pl.BlockSpec((1, d, PEER_TILE), lambda i, e: (e, 0, 0)),
                  pl.BlockSpec((PEER_COMBOS, PEER_KEY_ROWS, tt), lambda i, e: (0, e, i)),
                  pl.BlockSpec((PEER_HEADS, PEER_KEY_ROWS, tt), lambda i, e: (0, e, i)),
                  pl.BlockSpec((PEER_COMBOS, N_KEYS, tt), lambda i, e: (0, 0, i)),
                  pl.BlockSpec((PEER_HEADS, N_KEYS, tt), lambda i, e: (0, 0, i)),
                  pl.BlockSpec((PEER_HEADS, tt), lambda i, e: (0, i)),
                  pl.BlockSpec((1, d), lambda i, e: (0, 0)),
                  pl.BlockSpec((1, d), lambda i, e: (0, 0))],
        out_specs=pl.BlockSpec((tt, d), lambda i, e: (i, 0)),
        scratch_shapes=[pltpu.VMEM((d, tt), jnp.float32), pltpu.VMEM((PEER_TILE, tt), jnp.float32),
                        pltpu.VMEM((PEER_TILE, tt), jnp.bfloat16)],
        compiler_params=pltpu.CompilerParams(dimension_semantics=("arbitrary", "arbitrary"),
                                             vmem_limit_bytes=VMEM_LIMIT_BYTES),
        name="peer_dense",
    )(ht, h, u_bf, vt_bf, s, e0z, s, e1, tau, ln_g[None], ln_b[None])


def _out_proj_kernel(x_ref, nsa_ref, m_ref, wn_ref, wm_ref, g_ref, b_ref, wq_ref, h_ref, ht_ref, q_ref):
    f32, bf16 = jnp.float32, jnp.bfloat16
    r = (ALPHA * x_ref[...] + jnp.dot(nsa_ref[...].astype(bf16), wn_ref[...], preferred_element_type=f32)
         + jnp.dot(m_ref[...].astype(bf16), wm_ref[...], preferred_element_type=f32))
    mu = jnp.mean(r, axis=-1, keepdims=True)
    d = r - mu
    var = jnp.mean(d * d, axis=-1, keepdims=True)
    h = d * lax.rsqrt(var + LN_EPS) * g_ref[...] + b_ref[...]
    h_ref[...] = h
    ht_ref[...] = h.T.astype(bf16)
    q_ref[...] = jnp.dot(h.astype(bf16), wq_ref[...], preferred_element_type=f32)


def out_proj_fused(x, o_nsa, o_m, w_out, ln_g, ln_b, w_pq, tm):
    n, d = x.shape
    assert n % tm == 0
    bf16 = jnp.bfloat16
    nq = w_pq.shape[1]

    def rows(width):
        return pl.BlockSpec((tm, width), lambda i: (i, 0))

    def whole(a):
        return pl.BlockSpec(a.shape, lambda i: (0, 0))

    wn = w_out[:NSA_QW].astype(bf16)
    wm = w_out[NSA_QW:].astype(bf16)
    wq = w_pq.astype(bf16)
    g, b = ln_g[None], ln_b[None]
    return pl.pallas_call(
        _out_proj_kernel,
        out_shape=(jax.ShapeDtypeStruct((n, d), jnp.float32), jax.ShapeDtypeStruct((d, n), bf16),
                   jax.ShapeDtypeStruct((n, nq), jnp.float32)),
        grid=(n // tm,),
        in_specs=[rows(d), rows(NSA_QW), rows(M_W), whole(wn), whole(wm), whole(g), whole(b), whole(wq)],
        out_specs=(rows(d), pl.BlockSpec((d, tm), lambda i: (0, i)), rows(nq)),
        compiler_params=pltpu.CompilerParams(dimension_semantics=("arbitrary",),
                                             vmem_limit_bytes=VMEM_LIMIT_BYTES),
        name="out_proj",
    )(x, o_nsa, o_m, wn, wm, g, b, wq)


def block_tail(x, o_nsa, o_m, w_out, ln_g, ln_b, w_pq, sub_keys, u_bf, vt_bf, tt):
    lead = x.shape[:-1]
    h, ht, q = out_proj_fused(x.reshape(-1, D_MODEL), o_nsa.reshape(-1, NSA_QW), o_m.reshape(-1, M_W),
                              w_out, ln_g[0], ln_b[0], w_pq, tt)
    return peer_tail(h, ht, q, sub_keys, u_bf, vt_bf, ln_g[1], ln_b[1], tt).reshape(*lead, D_MODEL)


def prompt_mix(x, w_in, pe, w1, b1, w2, conv_w, conv_b, b_if):
    B, S, _ = x.shape
    z = in_proj_fused(x, w_in, jnp.arange(S), 512, True)
    kc = compress_chunks(z["k_cmp"], S // CMP_STRIDE, pe[0], w1[0], b1[0], w2[0])
    vc = compress_chunks(z["v_cmp"], S // CMP_STRIDE, pe[1], w1[1], b1[1], w2[1])
    zgate = z["zgate"].reshape(B, S, GATE_W)
    o_nsa = nsa_prompt_attention(z["q"].reshape(B, S, NSA_QW), kc, vc, z["k_slc_bf"], z["v_slc_bf"],
                                 z["k_win_bf"], z["v_win_bf"], zgate[..., :IN_SPLITS[2]])
    zqk = z["zqk"].reshape(B, S, 2 * M_W)
    o_m, C, n, m = mlstm_prompt(zqk, z["zv"].reshape(B, S, M_W), z["zo"].reshape(B, S, M_W),
                                zgate[..., IN_SPLITS[2]:], conv_w, conv_b, b_if)
    buf = zqk[:, S - (CONV_W - 1):]
    wl = min(WINDOW, S)
    k_cmp, v_cmp, k_slc, v_slc, k_win, v_win = [
        z[k + "_cache"].reshape(B, NSA_KV_HEADS, NSA_HD, S).transpose(0, 3, 1, 2)
        for k in ("k_cmp", "v_cmp", "k_slc", "v_slc", "k_win", "v_win")]
    return (o_nsa, o_m), (k_cmp, v_cmp, k_slc, v_slc, k_win[:, S - wl:], v_win[:, S - wl:], C, n, m, buf)


def sample_mix(x, kc_pool, vc_pool, ks_pool, vs_pool, kw_buf, vw_buf, C0, n0, m0, buf0, page_table,
               w_in, pe, w1, b1, w2, conv_w, conv_b, b_if):
    B, T, _ = x.shape
    past = page_table.shape[1] * PAGE_SIZE
    pos = past + jnp.arange(T)
    z = in_proj_fused(x, w_in, pos, B * T, False)
    q = z["q"].astype(jnp.float32).reshape(B, T, NSA_HEADS, NSA_HD) * (1.0 / ATTN_SCALE)
    k_cmp, v_cmp, k_slc, v_slc, k_win, v_win = [
        z[k].reshape(B, T, NSA_KV_HEADS, NSA_HD) for k in ("k_cmp", "v_cmp", "k_slc", "v_slc", "k_win", "v_win")]
    zgate = z["zgate"].reshape(B, T, GATE_W)
    gates = jax.nn.sigmoid(zgate[..., :IN_SPLITS[2]]).reshape(B, T, NSA_HEADS, 3)
    zqk, zv, zo, zif = (z["zqk"].reshape(B, T, 2 * M_W), z["zv"].reshape(B, T, M_W), z["zo"].reshape(B, T, M_W),
                        zgate[..., IN_SPLITS[2]:])

    assert (past + T) // CMP_STRIDE == past // CMP_STRIDE

    def compressed(pool, c):
        pages = pool.transpose(0, 2, 3, 1)[page_table]
        return compress_pages(pages, pe[c], w1[c], b1[c], w2[c])

    o_cmp, p = cmp_attend(q, pos, compressed(kc_pool, 0), compressed(vc_pool, 1))
    n_sel = -(-(past + T) // SEL_BLOCK)
    member = select_blocks(p, pos, n_sel)
    o_sel = sample_selected_attention(q, pos, member, ks_pool, vs_pool, k_slc, v_slc, page_table)
    wb = kw_buf.shape[1]
    kw = jnp.concatenate([kw_buf, k_win], axis=1)
    vw = jnp.concatenate([vw_buf, v_win], axis=1)
    kpos = past - wb + jnp.arange(wb + T)
    o_win = win_attend(q, pos, kw, vw, kpos)
    o_nsa = nsa_combine(gates, o_cmp, o_sel, o_win)
    o_m, (C, n, m, buf) = mlstm_mix(zqk, zv, zo, zif, buf0, C0, n0, m0, conv_w, conv_b, b_if, T)
    return (o_nsa, o_m), (k_cmp, v_cmp, k_slc, v_slc, kw[:, T:], vw[:, T:], C, n, m, buf)


def kernel(x_prompt, x_sample, cache_k_cmp, cache_v_cmp, cache_k_slc, cache_v_slc, cache_k_win, cache_v_win,
           state_C, state_n, state_m, state_conv, page_table, w_in, w_out, w_phi1, b_phi1, w_phi2, pe_cmp,
           conv_w, conv_b, b_if, ln_g, ln_b, w_pq, sub_keys, u_tab, v_tab):
    l = 0
    mix_p, st_p = prompt_mix(x_prompt, w_in[l], pe_cmp[l], w_phi1[l], b_phi1[l], w_phi2[l],
                             conv_w[l], conv_b[l], b_if[l])
    mix_s, st_s = sample_mix(x_sample, cache_k_cmp[l], cache_v_cmp[l], cache_k_slc[l], cache_v_slc[l],
                             cache_k_win[l], cache_v_win[l], state_C[l], state_n[l], state_m[l],
                             state_conv[l], page_table, w_in[l], pe_cmp[l], w_phi1[l], b_phi1[l],
                             w_phi2[l], conv_w[l], conv_b[l], b_if[l])
    u_bf = u_tab[l].astype(jnp.bfloat16)
    vt_bf = v_tab[l].astype(jnp.bfloat16).reshape(-1, PEER_TILE, D_MODEL).transpose(0, 2, 1)
    xp = block_tail(x_prompt, *mix_p, w_out[l], ln_g[l], ln_b[l], w_pq[l], sub_keys[l], u_bf, vt_bf, 512)
    xs = block_tail(x_sample, *mix_s, w_out[l], ln_g[l], ln_b[l], w_pq[l], sub_keys[l], u_bf, vt_bf, 128)
    return (xp, xs) + tuple(a[None] for a in st_p) + tuple(a[None] for a in st_s)
```

```python
import functools

import jax
import jax.numpy as jnp
import numpy as np
from jax import lax
from jax.experimental import pallas as pl
from jax.experimental.pallas import tpu as pltpu

D_MODEL = 1024
DEPTH = 1
PAGE_SIZE = 128
NSA_HEADS = 8
NSA_KV_HEADS = 2
NSA_GROUP = NSA_HEADS // NSA_KV_HEADS
NSA_HD = 64
NSA_QW = NSA_HEADS * NSA_HD
NSA_KVW = NSA_KV_HEADS * NSA_HD
CMP_BLOCK = 32
CMP_STRIDE = 16
SEL_BLOCK = 64
SEL_TOP = 16
WINDOW = 512
Q_BLOCK = 64
ATTN_SCALE = NSA_HD ** -0.5
ROPE_THETA = 10000.0
M_HEADS = 4
M_HD = 128
M_W = M_HEADS * M_HD
M_CHUNK = 64
CONV_W = 4
PEER_HEADS = 8
N_KEYS = 128
PEER_TOPK = 16
PEER_QDIM = 256
PEER_BLOCK = 128
IN_SPLITS = (NSA_QW, 6 * NSA_KVW, 3 * NSA_HEADS, 2 * M_W, M_W, M_W, 2 * M_HEADS)
LN_EPS = 1e-5
ALPHA = (2 * DEPTH) ** 0.25

VMEM_LIMIT_BYTES = 56 * 1024 * 1024


def _mm_kernel(x_ref, w_ref, o_ref):
    o_ref[...] = jnp.dot(x_ref[...].astype(jnp.bfloat16), w_ref[...], preferred_element_type=jnp.float32)


def pallas_matmul(x, w, tm=512):
    M, K = x.shape
    N = w.shape[1]
    tm = min(tm, M)
    assert M % tm == 0
    return pl.pallas_call(
        _mm_kernel,
        out_shape=jax.ShapeDtypeStruct((M, N), jnp.float32),
        grid=(M // tm,),
        in_specs=[pl.BlockSpec((tm, K), lambda i: (i, 0)), pl.BlockSpec((K, N), lambda i: (0, 0))],
        out_specs=pl.BlockSpec((tm, N), lambda i: (i, 0)),
        compiler_params=pltpu.CompilerParams(dimension_semantics=("arbitrary",),
                                             vmem_limit_bytes=VMEM_LIMIT_BYTES),
        name="proj_matmul",
    )(x, w.astype(jnp.bfloat16))


def mm3(x, w):
    lead = x.shape[:-1]
    return pallas_matmul(x.reshape(-1, x.shape[-1]), w).reshape(*lead, w.shape[1])


def layer_norm(x, g, b):
    mu = x.mean(-1, keepdims=True)
    var = jnp.square(x - mu).mean(-1, keepdims=True)
    return (x - mu) * lax.rsqrt(var + LN_EPS) * g + b


def rope(x, pos):
    half = x.shape[-1] // 2
    inv = ROPE_THETA ** (-jnp.arange(half, dtype=jnp.float32) / half)
    ang = pos.astype(jnp.float32)[:, None] * inv[None, :]
    cos = jnp.cos(ang)[:, None, :]
    sin = jnp.sin(ang)[:, None, :]
    x1, x2 = x[..., :half], x[..., half:]
    return jnp.concatenate([x1 * cos - x2 * sin, x2 * cos + x1 * sin], axis=-1)


def split_in_proj(x, w_in):
    z = mm3(x, w_in)
    cuts = [int(c) for c in np.cumsum(IN_SPLITS)[:-1]]
    return jnp.split(z, cuts, axis=-1)


_IN_OFF = np.concatenate([[0], np.cumsum(IN_SPLITS)])
_IN_ORDER = (0, 1, 3, 4, 5, 2, 6)
_N_KV_ROWS = 6
_KV_BF16 = (2, 3, 4, 5)
_KV_CACHE = (0, 1, 2, 3, 4, 5)
GATE_W = IN_SPLITS[2] + IN_SPLITS[6]


def _rope_pairs(x, cos, sin_signed):
    half = NSA_HD // 2
    lane = lax.broadcasted_iota(jnp.int32, x.shape, 1)
    partner = jnp.where(lane % NSA_HD < half, pltpu.roll(x, LANES - half, 1), pltpu.roll(x, half, 1))
    return x * cos + partner * sin_signed


def _in_proj_kernel(x_ref, w_ref, cos_ref, sin_ref, q_ref, *rest, kv_major):
    kv_refs = rest[:_N_KV_ROWS]
    rest = rest[_N_KV_ROWS:]
    if kv_major:
        bf_refs, rest = rest[:len(_KV_BF16)], rest[len(_KV_BF16):]
        cache_refs, rest = rest[:len(_KV_CACHE)], rest[len(_KV_CACHE):]
    zqk_ref, zv_ref, zo_ref, zgate_ref = rest
    z = jnp.dot(x_ref[...].astype(jnp.bfloat16), w_ref[...], preferred_element_type=jnp.float32)
    cos = cos_ref[...]
    sin = sin_ref[...]
    for g in range(NSA_QW // LANES):
        sl = slice(g * LANES, (g + 1) * LANES)
        q_ref[:, sl] = (_rope_pairs(z[:, sl], cos, sin) * ATTN_SCALE).astype(jnp.bfloat16)
    for r in range(_N_KV_ROWS):
        row = z[:, NSA_QW + r * NSA_KVW:NSA_QW + (r + 1) * NSA_KVW]
        if r % 2 == 0:
            row = _rope_pairs(row, cos, sin)
        kv_refs[r][...] = row
        if kv_major and r in _KV_CACHE:
            dst = cache_refs[_KV_CACHE.index(r)]
            dst[0] = row.T
        if kv_major and r in _KV_BF16:
            dst = bf_refs[_KV_BF16.index(r)]
            for n in range(NSA_KV_HEADS):
                dst[0, n] = row[:, n * NSA_HD:(n + 1) * NSA_HD].astype(jnp.bfloat16)
    o = NSA_QW + _N_KV_ROWS * NSA_KVW
    zqk_ref[...] = z[:, o:o + 2 * M_W]
    zv_ref[...] = z[:, o + 2 * M_W:o + 3 * M_W]
    zo_ref[...] = z[:, o + 3 * M_W:o + 4 * M_W]
    zgate_ref[...] = z[:, o + 4 * M_W:o + 4 * M_W + GATE_W]


def in_proj_fused(x, w_in, pos, tm, kv_major):
    B, T, D = x.shape
    M = B * T
    assert M % tm == 0 and NSA_KVW == LANES and (not kv_major or T % tm == 0)
    f32, bf16 = jnp.float32, jnp.bfloat16
    w = jnp.concatenate([w_in[:, _IN_OFF[i]:_IN_OFF[i + 1]] for i in _IN_ORDER], axis=1).astype(bf16)
    half = NSA_HD // 2
    inv = ROPE_THETA ** (-jnp.arange(half, dtype=f32) / half)
    ang = pos.astype(f32)[:, None] * inv[None, :]
    cos = jnp.tile(jnp.cos(ang), (B, 2 * LANES // NSA_HD))
    sin = jnp.tile(jnp.concatenate([-jnp.sin(ang), jnp.sin(ang)], axis=1), (B, LANES // NSA_HD))
    n_w = w.shape[1]

    def rows(width):
        return pl.BlockSpec((tm, width), lambda i: (i, 0))

    out_shape = [jax.ShapeDtypeStruct((M, NSA_QW), bf16)] + [jax.ShapeDtypeStruct((M, NSA_KVW), f32)] * _N_KV_ROWS
    out_specs = [rows(NSA_QW)] + [rows(NSA_KVW)] * _N_KV_ROWS
    if kv_major:
        per_seq = T // tm
        out_shape += [jax.ShapeDtypeStruct((B, NSA_KV_HEADS, T, NSA_HD), bf16)] * len(_KV_BF16)
        out_specs += [pl.BlockSpec((1, NSA_KV_HEADS, tm, NSA_HD),
                                   lambda i: (i // per_seq, 0, i % per_seq, 0))] * len(_KV_BF16)
        out_shape += [jax.ShapeDtypeStruct((B, NSA_KVW, T), f32)] * len(_KV_CACHE)
        out_specs += [pl.BlockSpec((1, NSA_KVW, tm), lambda i: (i // per_seq, 0, i % per_seq))] * len(_KV_CACHE)
    out_shape += [jax.ShapeDtypeStruct((M, 2 * M_W), f32), jax.ShapeDtypeStruct((M, M_W), f32),
                  jax.ShapeDtypeStruct((M, M_W), f32), jax.ShapeDtypeStruct((M, GATE_W), f32)]
    out_specs += [rows(2 * M_W), rows(M_W), rows(M_W), rows(GATE_W)]
    outs = pl.pallas_call(
        functools.partial(_in_proj_kernel, kv_major=kv_major),
        out_shape=tuple(out_shape),
        grid=(M // tm,),
        in_specs=[rows(D), pl.BlockSpec((D, n_w), lambda i: (0, 0)), rows(LANES), rows(LANES)],
        out_specs=tuple(out_specs),
        compiler_params=pltpu.CompilerParams(dimension_semantics=("arbitrary",),
                                             vmem_limit_bytes=VMEM_LIMIT_BYTES),
        name="in_proj",
    )(x.reshape(M, D), w, cos, sin)
    names = ["q", "k_cmp", "v_cmp", "k_slc", "v_slc", "k_win", "v_win"]
    if kv_major:
        names += ["k_slc_bf", "v_slc_bf", "k_win_bf", "v_win_bf"]
        names += ["k_cmp_cache", "v_cmp_cache", "k_slc_cache", "v_slc_cache", "k_win_cache", "v_win_cache"]
    names += ["zqk", "zv", "zo", "zgate"]
    return dict(zip(names, outs))


def nsa_project(zq, zkv, zg, pos):
    B, T, _ = zq.shape
    q = rope(zq.reshape(B, T, NSA_HEADS, NSA_HD), pos)
    kv = zkv.reshape(B, T, 6, NSA_KV_HEADS, NSA_HD)
    rows = (rope(kv[:, :, 0], pos), kv[:, :, 1], rope(kv[:, :, 2], pos), kv[:, :, 3],
            rope(kv[:, :, 4], pos), kv[:, :, 5])
    gates = jax.nn.sigmoid(zg).reshape(B, T, NSA_HEADS, 3)
    return q, rows, gates


def _expanded_w1(w1):
    assert CMP_BLOCK == 2 * CMP_STRIDE
    w1r = w1.reshape(2, CMP_STRIDE, NSA_HD, w1.shape[-1])
    wbig = jnp.einsum('hpdf,kn->pkdnhf', w1r, jnp.eye(NSA_KV_HEADS, dtype=w1.dtype))
    return wbig.reshape(CMP_STRIDE * NSA_KVW, 2 * NSA_KV_HEADS * w1.shape[-1])


def _compress_rows(x, w_ref, bias_ref, w2_ref, o_ref):
    f32, bf16 = jnp.float32, jnp.bfloat16
    rows = x.shape[0]
    f = w2_ref.shape[0]
    proj = jnp.dot(x.astype(bf16), w_ref[...], preferred_element_type=f32)
    for n in range(NSA_KV_HEADS):
        first = proj[:, 2 * n * f:(2 * n + 1) * f]
        second = pltpu.roll(proj[:, (2 * n + 1) * f:(2 * n + 2) * f], rows - 1, 0)
        pre = first + second + bias_ref[...]
        hid = 0.5 * pre * (1.0 + lax.erf(pre * (2.0 ** -0.5)))
        o_ref[:, n * NSA_HD:(n + 1) * NSA_HD] = jnp.dot(hid.astype(bf16), w2_ref[...], preferred_element_type=f32)


def _compress_chunks_kernel(x_ref, w_ref, bias_ref, w2_ref, o_ref):
    _compress_rows(x_ref[...], w_ref, bias_ref, w2_ref, o_ref)


def _compress_weights(pe, w1, b1, w2):
    bf16 = jnp.bfloat16
    bias = jnp.dot(pe.reshape(-1), w1, precision=lax.Precision.HIGHEST) + b1
    return _expanded_w1(w1).astype(bf16), bias[None], w2.astype(bf16)


def compress_chunks(rows, per_seq, pe, w1, b1, w2, tm=512):
    chunks = rows.reshape(-1, CMP_STRIDE * NSA_KVW)
    n = chunks.shape[0]
    tm = min(tm, n)
    assert n % tm == 0 and tm % per_seq == 0
    wbig, bias, w2b = _compress_weights(pe, w1, b1, w2)

    def whole(a):
        return pl.BlockSpec(a.shape, lambda i: (0, 0))

    out = pl.pallas_call(
        _compress_chunks_kernel,
        out_shape=jax.ShapeDtypeStruct((n, NSA_KVW), jnp.float32),
        grid=(n // tm,),
        in_specs=[pl.BlockSpec((tm, chunks.shape[1]), lambda i: (i, 0)), whole(wbig), whole(bias), whole(w2b)],
        out_specs=pl.BlockSpec((tm, NSA_KVW), lambda i: (i, 0)),
        compiler_params=pltpu.CompilerParams(dimension_semantics=("arbitrary",),
                                             vmem_limit_bytes=VMEM_LIMIT_BYTES),
        name="compress_chunks",
    )(chunks, wbig, bias, w2b)
    return out.reshape(n // per_seq, per_seq, NSA_KV_HEADS, NSA_HD)[:, :-1]


PAGE_GROUP = 4


def _compress_pages_kernel(pg_ref, w_ref, bias_ref, w2_ref, o_ref, x_ref, t_ref):
    n_pages = pg_ref.shape[1]
    per_page = PAGE_SIZE // CMP_STRIDE
    group = PAGE_GROUP

    def place(i, carry):
        for u in range(group):
            g = i * group + u
            t_ref[u] = pg_ref[0, g].reshape(NSA_KVW, PAGE_SIZE).T
            row0 = pl.multiple_of(g * per_page, per_page)
            for p in range(CMP_STRIDE):
                x_ref[pl.ds(row0, per_page), p * NSA_KVW:(p + 1) * NSA_KVW] = (
                    t_ref.at[u][pl.ds(p, per_page, stride=CMP_STRIDE), :])
        return carry

    lax.fori_loop(0, n_pages // group, place, 0)
    _compress_rows(x_ref[...], w_ref, bias_ref, w2_ref, o_ref.at[0])


def compress_pages(pages, pe, w1, b1, w2):
    B, n_pages = pages.shape[:2]
    assert pages.shape[2:] == (NSA_KV_HEADS, NSA_HD, PAGE_SIZE) and NSA_KVW == LANES and PAGE_SIZE == LANES
    assert n_pages % PAGE_GROUP == 0
    wbig, bias, w2b = _compress_weights(pe, w1, b1, w2)
    rows = n_pages * (PAGE_SIZE // CMP_STRIDE)

    def whole(a):
        return pl.BlockSpec(a.shape, lambda b: (0, 0))

    out = pl.pallas_call(
        _compress_pages_kernel,
        out_shape=jax.ShapeDtypeStruct((B, rows, NSA_KVW), jnp.float32),
        grid=(B,),
        in_specs=[pl.BlockSpec((1, n_pages, NSA_KV_HEADS, NSA_HD, PAGE_SIZE), lambda b: (b, 0, 0, 0, 0)),
                  whole(wbig), whole(bias), whole(w2b)],
        out_specs=pl.BlockSpec((1, rows, NSA_KVW), lambda b: (b, 0, 0)),
        scratch_shapes=[pltpu.VMEM((rows, wbig.shape[0]), jnp.float32),
                        pltpu.VMEM((PAGE_GROUP, PAGE_SIZE, NSA_KVW), jnp.float32)],
        compiler_params=pltpu.CompilerParams(dimension_semantics=("arbitrary",),
                                             vmem_limit_bytes=VMEM_LIMIT_BYTES),
        name="compress_pages",
    )(pages, wbig, bias, w2b)
    return out.reshape(B, rows, NSA_KV_HEADS, NSA_HD)[:, :-1]


def cmp_attend(q, qpos, kc, vc):
    B, T = q.shape[:2]
    qg = q.reshape(B, T, NSA_KV_HEADS, NSA_GROUP, NSA_HD)
    s = jnp.einsum('btngd,bcnd->btngc', qg, kc) * ATTN_SCALE
    nblk = kc.shape[1]
    blk_end = jnp.arange(nblk) * CMP_STRIDE + CMP_BLOCK - 1
    valid = (blk_end[None, :] <= qpos[:, None])[None, :, None, None, :]
    p = jax.nn.softmax(jnp.where(valid, s, -1e30), axis=-1) * valid
    o = jnp.einsum('btngc,bcnd->btngd', p, vc)
    return o.reshape(B, T, NSA_HEADS, NSA_HD), p


def select_blocks(p, qpos, n_sel):
    imp = p.sum(axis=3)
    R = SEL_BLOCK // CMP_STRIDE
    r = CMP_BLOCK // CMP_STRIDE
    nb = imp.shape[-1]
    right = n_sel * R + R - 1 - nb
    padded = jnp.pad(imp, ((0, 0), (0, 0), (0, 0), (r - 1, right)))
    score = padded[..., 0:(n_sel - 1) * R + 1:R]
    for o in range(1, R + r - 1):
        score = score + padded[..., o:o + (n_sel - 1) * R + 1:R]
    j = jnp.arange(n_sel)[None, :]
    cur = (qpos // SEL_BLOCK)[:, None]
    valid = (j * SEL_BLOCK <= qpos[:, None])[None, :, None, :]
    forced = ((j == 0) | (j == cur) | (j == cur - 1))[None, :, None, :]
    score = jnp.where(forced, jnp.inf, jnp.where(valid, score, -jnp.inf))
    idx = j[0]
    before = (score[..., None, :] > score[..., :, None]) | ((score[..., None, :] == score[..., :, None])
                                                          & (idx[None, :] < idx[:, None]))
    return before.sum(-1) < min(SEL_TOP, n_sel)


def sample_selected_attention(q, qpos, member, k_pool, v_pool, k_new, v_new, page_table):
    B, T = q.shape[:2]
    n_pages = page_table.shape[1]
    per_page = PAGE_SIZE // SEL_BLOCK
    assert member.shape[-1] == n_pages * per_page + 1 and T <= SEL_BLOCK
    kp = k_pool.transpose(0, 2, 3, 1)[page_table]
    vp = v_pool.transpose(0, 2, 3, 1)[page_table]
    qg = q.reshape(B, T, NSA_KV_HEADS, NSA_GROUP, NSA_HD)
    s_past = jnp.einsum('btngd,bpndk->bntgpk', qg, kp) * ATTN_SCALE
    s_new = jnp.einsum('btngd,bsnd->bntgs', qg, k_new) * ATTN_SCALE
    m = member.transpose(0, 2, 1, 3)
    m_past = jnp.repeat(m[..., :-1].reshape(B, NSA_KV_HEADS, T, n_pages, per_page), SEL_BLOCK, axis=-1)
    kpos = (jnp.arange(n_pages) * PAGE_SIZE)[:, None] + jnp.arange(PAGE_SIZE)[None, :]
    m_past = m_past & (kpos[None, None, None] <= qpos[None, None, :, None, None])
    new_pos = n_pages * PAGE_SIZE + jnp.arange(T)
    m_new = m[..., -1:] & (new_pos[None, None, None, :] <= qpos[None, None, :, None])
    logits = jnp.concatenate(
        [jnp.where(m_past[:, :, :, None], s_past, -jnp.inf).reshape(B, NSA_KV_HEADS, T, NSA_GROUP, -1),
         jnp.where(m_new[:, :, :, None], s_new, -jnp.inf)], axis=-1)
    pr = jax.nn.softmax(logits, axis=-1)
    pr_past = pr[..., :n_pages * PAGE_SIZE].reshape(B, NSA_KV_HEADS, T, NSA_GROUP, n_pages, PAGE_SIZE)
    o = (jnp.einsum('bntgpk,bpndk->bntgd', pr_past, vp)
         + jnp.einsum('bntgs,bsnd->bntgd', pr[..., n_pages * PAGE_SIZE:], v_new))
    return o.transpose(0, 2, 1, 3, 4).reshape(B, T, NSA_HEADS, NSA_HD)


def to_blocks(rows, n_sel):
    B, L, KV, hd = rows.shape
    rows = jnp.pad(rows, ((0, 0), (0, n_sel * SEL_BLOCK - L), (0, 0), (0, 0)))
    return rows.reshape(B, n_sel, SEL_BLOCK, KV, hd).transpose(0, 3, 1, 2, 4)


def take_rows(table, idx):
    return table[idx]


def sel_attend(q, qpos, sel, kb, vb):
    B, Tq = q.shape[:2]
    k = sel.shape[-1]
    sel_t = sel.transpose(0, 2, 1, 3)
    gather = jax.vmap(jax.vmap(take_rows))
    kg = gather(kb, sel_t).reshape(B, NSA_KV_HEADS, Tq, k * SEL_BLOCK, NSA_HD)
    vg = gather(vb, sel_t).reshape(B, NSA_KV_HEADS, Tq, k * SEL_BLOCK, NSA_HD)
    kpos = (sel_t[..., None] * SEL_BLOCK + jnp.arange(SEL_BLOCK)).reshape(B, NSA_KV_HEADS, Tq, k * SEL_BLOCK)
    qg = q.reshape(B, Tq, NSA_KV_HEADS, NSA_GROUP, NSA_HD).transpose(0, 2, 1, 3, 4)
    s = jnp.einsum('bntgd,bntsd->bntgs', qg, kg) * ATTN_SCALE
    mask = kpos[:, :, :, None, :] <= qpos[None, None, :, None, None]
    pr = jax.nn.softmax(jnp.where(mask, s, -jnp.inf), axis=-1)
    o = jnp.einsum('bntgs,bntsd->bntgd', pr, vg)
    return o.transpose(0, 2, 1, 3, 4).reshape(B, Tq, NSA_HEADS, NSA_HD)


def win_attend(q, qpos, k, v, kpos):
    B, Tq = q.shape[:2]
    qg = q.reshape(B, Tq, NSA_KV_HEADS, NSA_GROUP, NSA_HD)
    s = jnp.einsum('btngd,bsnd->btngs', qg, k) * ATTN_SCALE
    diff = qpos[:, None] - kpos[None, :]
    mask = ((diff >= 0) & (diff < WINDOW) & (kpos[None, :] >= 0))[None, :, None, None, :]
    pr = jax.nn.softmax(jnp.where(mask, s, -jnp.inf), axis=-1)
    o = jnp.einsum('btngs,bsnd->btngd', pr, v)
    return o.reshape(B, Tq, NSA_HEADS, NSA_HD)


def nsa_combine(gates, o_cmp, o_sel, o_win):
    B, T = gates.shape[:2]
    o = gates[..., 0:1] * o_cmp + gates[..., 1:2] * o_sel + gates[..., 2:3] * o_win
    return o.reshape(B, T, NSA_QW)


NSA_TQ = 128
NSA_CK = 512
MASKED = -1e30


def _softmax_rows(s):
    m = jnp.max(s, axis=-1, keepdims=True)
    e = jnp.exp(s - m)
    return e / jnp.sum(e, axis=-1, keepdims=True)


def _nsa_prompt_kernel(q_ref, kc_ref, vc_ref, ks_ref, vs_ref, kw_ref, vw_ref, zg_ref, msel_ref, exp_ref, o_ref):
    f32, bf16 = jnp.float32, jnp.bfloat16
    tq = NSA_TQ
    q0 = pl.program_id(2) * tq
    qb = q_ref[0]
    qs = jnp.concatenate([qb[:, g * NSA_HD:(g + 1) * NSA_HD] for g in range(NSA_GROUP)], axis=0)
    tpos = q0 + lax.broadcasted_iota(jnp.int32, (tq, 1), 0)

    def per_head(a):
        return jnp.concatenate([a] * NSA_GROUP, axis=0)

    s = lax.dot_general(qs, kc_ref[0, 0], _NT, preferred_element_type=f32)
    cblk = lax.broadcasted_iota(jnp.int32, (tq, 128), 1)
    cvalid = cblk * CMP_STRIDE + (CMP_BLOCK - 1) <= tpos
    s = s + per_head(jnp.where(cvalid, 0.0, MASKED))
    e = jnp.exp(s - jnp.max(s, axis=-1, keepdims=True)) * per_head(jnp.where(cvalid, 1.0, 0.0))
    l = jnp.sum(e, axis=-1, keepdims=True)
    p = e / jnp.where(l > 0.0, l, 1.0)
    o_cmp = jnp.dot(p.astype(bf16), vc_ref[0, 0], preferred_element_type=f32)

    imp = p[0:tq]
    for g in range(1, NSA_GROUP):
        imp = imp + p[g * tq:(g + 1) * tq]
    hi = imp.astype(bf16)
    r1 = imp - hi.astype(f32)
    mid = r1.astype(bf16)
    lo = (r1 - mid.astype(f32)).astype(bf16)
    msel = msel_ref[...]
    score = (lax.dot_general(msel, hi, _NT, preferred_element_type=f32)
             + lax.dot_general(msel, mid, _NT, preferred_element_type=f32)
             + lax.dot_general(msel, lo, _NT, preferred_element_type=f32))
    n_sel = score.shape[0]
    j = lax.broadcasted_iota(jnp.int32, (n_sel, tq), 0)
    tok = q0 + lax.broadcasted_iota(jnp.int32, (n_sel, tq), 1)
    cur = tok // SEL_BLOCK
    forced = (j == 0) | (j == cur) | (j == cur - 1)
    score = jnp.where(forced, jnp.inf, jnp.where(j * SEL_BLOCK <= tok, score, -jnp.inf))
    rank = jnp.zeros((n_sel, tq), f32)
    for jp in range(n_sel):
        row = score[jp:jp + 1, :]
        before = (row > score) | ((row == score) & (j > jp))
        rank = rank + jnp.where(before, 1.0, 0.0)
    chosen_t = jnp.where(rank < SEL_TOP, 1.0, 0.0)
    sel01 = jnp.concatenate([chosen_t, jnp.zeros((LANES - n_sel, tq), f32)], axis=0).T.astype(bf16)

    ck = NSA_CK
    rows = NSA_GROUP * tq

    def sel_chunk(c, carry):
        m, l, acc = carry
        k0 = pl.multiple_of(c * ck, ck)
        s = lax.dot_general(qs, ks_ref[0, 0, pl.ds(k0, ck), :], _NT, preferred_element_type=f32)
        chosen = jnp.dot(sel01, exp_ref[c], preferred_element_type=f32)
        kpos = k0 + lax.broadcasted_iota(jnp.int32, (tq, ck), 1)
        ok = (chosen > 0.5) & (kpos <= tpos)
        s = s + per_head(jnp.where(ok, 0.0, MASKED))
        m_new = jnp.maximum(m, jnp.max(s, axis=-1, keepdims=True))
        a = jnp.exp(m - m_new)
        pr = jnp.exp(s - m_new)
        l = a * l + jnp.sum(pr, axis=-1, keepdims=True)
        acc = a * acc + jnp.dot(pr.astype(bf16), vs_ref[0, 0, pl.ds(k0, ck), :], preferred_element_type=f32)
        return m_new, l, acc

    init = (jnp.full((rows, 1), MASKED, f32), jnp.zeros((rows, 1), f32), jnp.zeros((rows, NSA_HD), f32))
    n_chunks = (q0 + tq + ck - 1) // ck
    _, l_sel, acc_sel = lax.fori_loop(0, n_chunks, sel_chunk, init)
    o_sel = acc_sel / l_sel

    w0 = pl.multiple_of(jnp.maximum(q0 - WINDOW, 0), tq)
    wl = WINDOW + tq
    s = lax.dot_general(qs, kw_ref[0, 0, pl.ds(w0, wl), :], _NT, preferred_element_type=f32)
    diff = tpos - (w0 + lax.broadcasted_iota(jnp.int32, (tq, wl), 1))
    s = s + per_head(jnp.where((diff >= 0) & (diff < WINDOW), 0.0, MASKED))
    o_win = jnp.dot(_softmax_rows(s).astype(bf16), vw_ref[0, 0, pl.ds(w0, wl), :], preferred_element_type=f32)

    gates = jax.nn.sigmoid(zg_ref[0, 0])
    for g in range(NSA_GROUP):
        r = slice(g * tq, (g + 1) * tq)
        o_ref[0, :, g * NSA_HD:(g + 1) * NSA_HD] = (gates[:, 3 * g:3 * g + 1] * o_cmp[r]
                                                    + gates[:, 3 * g + 1:3 * g + 2] * o_sel[r]
                                                    + gates[:, 3 * g + 2:3 * g + 3] * o_win[r])


def nsa_prompt_attention(qs, kc, vc, k_slc, v_slc, k_win, v_win, zg):
    B, S = qs.shape[:2]
    bf16 = jnp.bfloat16
    assert S % NSA_CK == 0 and S % NSA_TQ == 0 and WINDOW % NSA_TQ == 0 and WINDOW + NSA_TQ <= S
    n_sel = S // SEL_BLOCK
    nb = kc.shape[1]
    assert nb <= 128

    def pad_blocks(a):
        return jnp.pad(a.transpose(0, 2, 1, 3).astype(bf16), ((0, 0), (0, 0), (0, 128 - nb), (0, 0)))

    zg4 = zg.reshape(B, S, NSA_KV_HEADS, 3 * NSA_GROUP).transpose(0, 2, 1, 3)
    c = np.arange(128)[:, None]
    jj = np.arange(n_sel)[None, :]
    ratio = SEL_BLOCK // CMP_STRIDE
    msel = ((c >= jj * ratio - (CMP_BLOCK // CMP_STRIDE - 1)) & (c <= jj * ratio + ratio - 1) & (c < nb))
    assert n_sel <= LANES and NSA_TQ == LANES
    expand = (np.arange(S)[None, :] // SEL_BLOCK == np.arange(LANES)[:, None])
    expand = expand.reshape(LANES, S // NSA_CK, NSA_CK).transpose(1, 0, 2)
    row_spec = pl.BlockSpec((1, 1, S, NSA_HD), lambda b, n, i: (b, n, 0, 0))
    blk_spec = pl.BlockSpec((1, 1, 128, NSA_HD), lambda b, n, i: (b, n, 0, 0))
    return pl.pallas_call(
        _nsa_prompt_kernel,
        out_shape=jax.ShapeDtypeStruct((B, S, NSA_QW), jnp.float32),
        grid=(B, NSA_KV_HEADS, S // NSA_TQ),
        in_specs=[pl.BlockSpec((1, NSA_TQ, NSA_GROUP * NSA_HD), lambda b, n, i: (b, i, n)),
                  blk_spec, blk_spec, row_spec, row_spec, row_spec, row_spec,
                  pl.BlockSpec((1, 1, NSA_TQ, 3 * NSA_GROUP), lambda b, n, i: (b, n, i, 0)),
                  pl.BlockSpec((n_sel, 128), lambda b, n, i: (0, 0)),
                  pl.BlockSpec((S // NSA_CK, LANES, NSA_CK), lambda b, n, i: (0, 0, 0))],
        out_specs=pl.BlockSpec((1, NSA_TQ, NSA_GROUP * NSA_HD), lambda b, n, i: (b, i, n)),
        compiler_params=pltpu.CompilerParams(dimension_semantics=("arbitrary", "arbitrary", "arbitrary"),
                                             vmem_limit_bytes=VMEM_LIMIT_BYTES),
        name="nsa_prompt_attention",
    )(qs, pad_blocks(kc), pad_blocks(vc), k_slc, v_slc, k_win, v_win,
      zg4, jnp.asarray(msel.T, bf16), jnp.asarray(expand, bf16))


MLSTM_L = 128
CONV_HALO = 8


def _log_sigmoid(x):
    return -(jnp.maximum(-x, 0.0) + jnp.log1p(jnp.exp(-jnp.abs(x))))


def _mlstm_prompt_kernel(x_ref, xprev_ref, halo0_ref, v_ref, o_ref, gcol_ref, grow_ref, cw_ref, cb_ref,
                         out_ref, c_out, n_out, m_out, c_ref, n_ref, m_ref):
    f32, bf16 = jnp.float32, jnp.bfloat16
    c = pl.program_id(1)
    L = MLSTM_L

    @pl.when(c == 0)
    def _():
        c_ref[...] = jnp.zeros_like(c_ref)
        n_ref[...] = jnp.zeros_like(n_ref)
        m_ref[...] = jnp.zeros_like(m_ref)

    x = x_ref[0]
    halo = jnp.where(c == 0, halo0_ref[0], xprev_ref[0, L - CONV_HALO:L, :])
    ext = jnp.concatenate([halo, x], axis=0)
    conv = cb_ref[...]
    for j in range(CONV_W):
        o = CONV_HALO - (CONV_W - 1) + j
        conv = conv + ext[o:o + L] * cw_ref[j:j + 1, :]
    qk = conv * jax.nn.sigmoid(conv)

    t_id = lax.broadcasted_iota(jnp.int32, (L, L), 0)
    s_id = lax.broadcasted_iota(jnp.int32, (L, L), 1)
    causal = t_id >= s_id
    gcol = gcol_ref[0, 0]
    grow = grow_ref[0, 0]
    for h in range(M_HEADS):
        hd = slice(h * M_HD, (h + 1) * M_HD)
        q = qk[:, hd]
        k = qk[:, M_W + h * M_HD:M_W + (h + 1) * M_HD] * (M_HD ** -0.5)
        v = v_ref[0, :, hd].astype(bf16)
        ig_r = grow[h:h + 1, :]
        ig_c = gcol[:, h:h + 1]
        lf_r = _log_sigmoid(grow[M_HEADS + h:M_HEADS + h + 1, :])
        lf_c = _log_sigmoid(gcol[:, M_HEADS + h:M_HEADS + h + 1])
        b_c = jnp.sum(jnp.where(causal, lf_r, 0.0), axis=1, keepdims=True)
        b_r = jnp.sum(jnp.where(t_id <= s_id, lf_c, 0.0), axis=0, keepdims=True)
        m_prev = m_ref[h]
        dmat = jnp.where(causal, b_c - b_r + ig_r, -jnp.inf)
        inter = b_c + m_prev
        m_t = jnp.maximum(inter, jnp.max(dmat, axis=1, keepdims=True))
        w_intra = jnp.exp(dmat - m_t)
        w_inter = jnp.exp(inter - m_t)
        qb = q.astype(bf16)
        s = lax.dot_general(qb, k.astype(bf16), _NT, preferred_element_type=f32) * w_intra
        num = (jnp.dot(s.astype(bf16), v, preferred_element_type=f32)
               + w_inter * jnp.dot(qb, c_ref[h].astype(bf16), preferred_element_type=f32))
        den = jnp.sum(s, axis=1, keepdims=True) + w_inter * jnp.sum(q * n_ref[h], axis=1, keepdims=True)
        hh = num / jnp.maximum(jnp.abs(den), jnp.exp(-m_t))
        out_ref[0, :, hd] = jax.nn.sigmoid(o_ref[0, :, hd]) * hh
        m_new = m_t[L - 1:L]
        b_last = b_c[L - 1:L]
        w_s = jnp.exp(b_last - b_c + ig_c - m_new)
        w_p = jnp.exp(b_last + m_prev - m_new)
        kw = k * w_s
        c_ref[h] = w_p * c_ref[h] + jnp.dot(kw.T.astype(bf16), v, preferred_element_type=f32)
        n_ref[h] = w_p * n_ref[h] + jnp.sum(kw, axis=0, keepdims=True)
        m_ref[h] = m_new

    @pl.when(c == pl.num_programs(1) - 1)
    def _():
        c_out[0] = c_ref[...]
        n_out[0] = n_ref[...]
        m_out[0] = m_ref[...]


def mlstm_prompt(zqk, zv, zo, zif, conv_w, conv_b, b_if):
    B, T, _ = zqk.shape
    L = MLSTM_L
    assert T % L == 0
    nc = T // L
    f32 = jnp.float32
    gif = zif + b_if
    gcol = gif.reshape(B, nc, L, 2 * M_HEADS)
    grow = gcol.transpose(0, 1, 3, 2)
    halo0 = jnp.zeros((B, CONV_HALO, 2 * M_W), f32)
    out, C, n, m = pl.pallas_call(
        _mlstm_prompt_kernel,
        out_shape=(jax.ShapeDtypeStruct((B, T, M_W), f32),
                   jax.ShapeDtypeStruct((B, M_HEADS, M_HD, M_HD), f32),
                   jax.ShapeDtypeStruct((B, M_HEADS, 1, M_HD), f32),
                   jax.ShapeDtypeStruct((B, M_HEADS, 1, 1), f32)),
        grid=(B, nc),
        in_specs=[pl.BlockSpec((1, L, 2 * M_W), lambda b, c: (b, c, 0)),
                  pl.BlockSpec((1, L, 2 * M_W), lambda b, c: (b, jnp.maximum(c - 1, 0), 0)),
                  pl.BlockSpec((1, CONV_HALO, 2 * M_W), lambda b, c: (b, 0, 0)),
                  pl.BlockSpec((1, L, M_W), lambda b, c: (b, c, 0)),
                  pl.BlockSpec((1, L, M_W), lambda b, c: (b, c, 0)),
                  pl.BlockSpec((1, 1, L, 2 * M_HEADS), lambda b, c: (b, c, 0, 0)),
                  pl.BlockSpec((1, 1, 2 * M_HEADS, L), lambda b, c: (b, c, 0, 0)),
                  pl.BlockSpec((CONV_W, 2 * M_W), lambda b, c: (0, 0)),
                  pl.BlockSpec((1, 2 * M_W), lambda b, c: (0, 0))],
        out_specs=(pl.BlockSpec((1, L, M_W), lambda b, c: (b, c, 0)),
                   pl.BlockSpec((1, M_HEADS, M_HD, M_HD), lambda b, c: (b, 0, 0, 0)),
                   pl.BlockSpec((1, M_HEADS, 1, M_HD), lambda b, c: (b, 0, 0, 0)),
                   pl.BlockSpec((1, M_HEADS, 1, 1), lambda b, c: (b, 0, 0, 0))),
        scratch_shapes=[pltpu.VMEM((M_HEADS, M_HD, M_HD), f32), pltpu.VMEM((M_HEADS, 1, M_HD), f32),
                        pltpu.VMEM((M_HEADS, 1, 1), f32)],
        compiler_params=pltpu.CompilerParams(dimension_semantics=("arbitrary", "arbitrary"),
                                             vmem_limit_bytes=VMEM_LIMIT_BYTES),
        name="mlstm_prompt",
    )(zqk, zqk, halo0, zv, zo, gcol, grow, conv_w, conv_b[None])
    return out, C, n.reshape(B, M_HEADS, M_HD), m.reshape(B, M_HEADS)


def mlstm_chunk(carry, inp):
    C, n, m = carry
    q, k, v, ig, lf = inp
    L = q.shape[2]
    b = jnp.cumsum(lf, axis=-1)
    causal = jnp.tril(jnp.ones((L, L), dtype=bool))
    dmat = jnp.where(causal, b[..., :, None] - b[..., None, :] + ig[..., None, :], -jnp.inf)
    inter = b + m[..., None]
    m_t = jnp.maximum(inter, dmat.max(axis=-1))
    w_intra = jnp.exp(dmat - m_t[..., None])
    w_inter = jnp.exp(inter - m_t)
    s = jnp.einsum('bhtd,bhsd->bhts', q, k) * w_intra
    num = jnp.einsum('bhts,bhsv->bhtv', s, v) + w_inter[..., None] * jnp.einsum('bhtd,bhdv->bhtv', q, C)
    den = s.sum(-1) + w_inter * jnp.einsum('bhtd,bhd->bht', q, n)
    h = num / jnp.maximum(jnp.abs(den), jnp.exp(-m_t))[..., None]
    m_new = m_t[..., -1]
    w_s = jnp.exp(b[..., -1:] - b + ig - m_new[..., None])
    w_p = jnp.exp(b[..., -1] + m - m_new)
    C_new = w_p[..., None, None] * C + jnp.einsum('bhs,bhsd,bhsv->bhdv', w_s, k, v)
    n_new = w_p[..., None] * n + jnp.einsum('bhs,bhsd->bhd', w_s, k)
    return (C_new, n_new, m_new), h


def mlstm_mix(zqk, zv, zo, zif, buf0, C0, n0, m0, conv_w, conv_b, b_if, chunk):
    B, T, _ = zqk.shape
    full = jnp.concatenate([buf0, zqk], axis=1)
    conv = conv_b
    for j in range(CONV_W):
        conv = conv + full[:, j:j + T] * conv_w[j]
    qk = jax.nn.silu(conv)

    def heads(a):
        return a.reshape(B, T, M_HEADS, M_HD).transpose(0, 2, 1, 3)

    q = heads(qk[..., :M_W])
    k = heads(qk[..., M_W:]) * (M_HD ** -0.5)
    v = heads(zv)
    gif = zif + b_if
    ig = gif[..., :M_HEADS].transpose(0, 2, 1)
    lf = jax.nn.log_sigmoid(gif[..., M_HEADS:]).transpose(0, 2, 1)
    nc = T // chunk

    def to_chunks(a):
        return jnp.moveaxis(a.reshape(B, M_HEADS, nc, chunk, *a.shape[3:]), 2, 0)

    (C, n, m), h = lax.scan(mlstm_chunk, (C0, n0, m0),
                            (to_chunks(q), to_chunks(k), to_chunks(v), to_chunks(ig), to_chunks(lf)))
    h = jnp.moveaxis(h, 0, 2).reshape(B, M_HEADS, T, M_HD).transpose(0, 2, 1, 3).reshape(B, T, M_W)
    out = jax.nn.sigmoid(zo) * h
    return out, (C, n, m, full[:, T:])


PEER_COMBOS = 2 * PEER_HEADS
PEER_KEY_ROWS = 16
PEER_TILE = PEER_KEY_ROWS * N_KEYS
PEER_TS_ROWS = 24
LANES = 128
_NT = (((1,), (1,)), ((), ()))


def _peer_topk_kernel(q_ref, keys_ref, s_ref, e0_ref, e1_ref, tau_ref, ts_ref):
    head = pl.program_id(1)
    tt = q_ref.shape[0]
    half_w = PEER_QDIM // 2
    for half in range(2):
        c = 2 * head + half
        qh = q_ref[:, half * half_w:(half + 1) * half_w].astype(jnp.bfloat16)
        s = lax.dot_general(keys_ref[half], qh, _NT, preferred_element_type=jnp.float32)
        s_ref[c] = s
        pad_rows = jnp.full((PEER_TS_ROWS - PEER_TOPK - 1, tt), -jnp.inf, jnp.float32)
        work = s
        rows = []
        for _ in range(PEER_TOPK + 1):
            m = jnp.max(work, axis=0, keepdims=True)
            work = jnp.where(work == m, -jnp.inf, work)
            rows.append(m)
        ts_ref[c] = jnp.concatenate(rows + [pad_rows], axis=0)
        removed = jnp.sum(jnp.where(work == -jnp.inf, 1.0, 0.0), axis=0, keepdims=True)
        has_ties = jnp.max(removed) > PEER_TOPK + 1

        @pl.when(has_ties)
        def _(s=s, c=c):
            key_id = lax.broadcasted_iota(jnp.int32, s.shape, 0)
            work = s
            rows = []
            for _ in range(PEER_TOPK + 1):
                m = jnp.max(work, axis=0, keepdims=True)
                first = jnp.min(jnp.where(work == m, key_id, N_KEYS), axis=0, keepdims=True)
                work = jnp.where(key_id == first, -jnp.inf, work)
                rows.append(m)
            ts_ref[c] = jnp.concatenate(rows + [pad_rows], axis=0)

    @pl.when(head == PEER_HEADS - 1)
    def _():
        for h in range(PEER_HEADS):
            t0 = ts_ref[2 * h]
            t1 = ts_ref[2 * h + 1]
            pieces = [t0[0:1] + t1] + [t0[a:a + 1] + t1[0:8] for a in range(1, 8)] + [t0[8:24] + t1[0:1]]
            cand = jnp.concatenate(pieces, axis=0)
            top = t0[0:1] + t1[0:1]
            v16 = top
            v17 = top
            z = jnp.zeros_like(top)
            seen = jnp.zeros_like(top)
            for _ in range(PEER_TOPK + 1):
                m = jnp.max(cand, axis=0, keepdims=True)
                eq = cand == m
                cnt = jnp.sum(jnp.where(eq, 1.0, 0.0), axis=0, keepdims=True)
                active = seen < PEER_TOPK
                take = jnp.minimum(cnt, PEER_TOPK - seen)
                v16 = jnp.where(active, m, v16)
                v17 = jnp.where(seen < PEER_TOPK + 1, m, v17)
                z = z + jnp.where(active, take * jnp.exp(m - top), 0.0)
                seen = seen + cnt
                cand = jnp.where(eq, -jnp.inf, cand)
            tau_ref[h:h + 1, :] = 0.5 * v16 + 0.5 * v17
            e0_ref[h] = jnp.exp(s_ref[2 * h] - t0[0:1]) / z
            e1_ref[h] = jnp.exp(s_ref[2 * h + 1] - t1[0:1])


def peer_scores(q, sub_keys, tt):
    n = q.shape[0]
    assert n % tt == 0
    keys = sub_keys.reshape(PEER_COMBOS, N_KEYS, PEER_QDIM // 2).astype(jnp.bfloat16)
    f32 = jnp.float32
    return pl.pallas_call(
        _peer_topk_kernel,
        out_shape=(jax.ShapeDtypeStruct((PEER_COMBOS, N_KEYS, n), f32),
                   jax.ShapeDtypeStruct((PEER_HEADS, N_KEYS, n), f32),
                   jax.ShapeDtypeStruct((PEER_HEADS, N_KEYS, n), f32),
                   jax.ShapeDtypeStruct((PEER_HEADS, n), f32)),
        grid=(n // tt, PEER_HEADS),
        in_specs=[pl.BlockSpec((tt, PEER_QDIM), lambda i, c: (i, c)),
                  pl.BlockSpec((2, N_KEYS, PEER_QDIM // 2), lambda i, c: (c, 0, 0))],
        out_specs=(pl.BlockSpec((PEER_COMBOS, N_KEYS, tt), lambda i, c: (0, 0, i)),
                   pl.BlockSpec((PEER_HEADS, N_KEYS, tt), lambda i, c: (0, 0, i)),
                   pl.BlockSpec((PEER_HEADS, N_KEYS, tt), lambda i, c: (0, 0, i)),
                   pl.BlockSpec((PEER_HEADS, tt), lambda i, c: (0, i))),
        scratch_shapes=[pltpu.VMEM((PEER_COMBOS, PEER_TS_ROWS, tt), f32)],
        compiler_params=pltpu.CompilerParams(dimension_semantics=("arbitrary", "arbitrary"),
                                             vmem_limit_bytes=VMEM_LIMIT_BYTES),
        name="peer_topk",
    )(q, keys)


def _peer_dense_kernel(xt_ref, h_ref, u_ref, vt_ref, s0_ref, ez_ref, s_ref, e1_ref, tau_ref, g_ref, b_ref,
                       o_ref, acc_ref, a_ref, w_ref):
    e = pl.program_id(1)
    tt = xt_ref.shape[1]

    @pl.when(e == 0)
    def _():
        acc_ref[...] = jnp.zeros_like(acc_ref)

    a_ref[...] = jnp.dot(u_ref[...], xt_ref[...], preferred_element_type=jnp.float32)
    for r in range(PEER_KEY_ROWS):
        rows = slice(r * N_KEYS, (r + 1) * N_KEYS)
        for t in range(tt // LANES):
            tok = slice(t * LANES, (t + 1) * LANES)
            gate = jnp.zeros((N_KEYS, LANES), jnp.float32)
            for h in range(PEER_HEADS):
                need = tau_ref[h:h + 1, tok] - s0_ref[2 * h, r:r + 1, tok]
                picked = jnp.where(s_ref[2 * h + 1, :, tok] >= need, e1_ref[h, :, tok], 0.0)
                gate = gate + picked * ez_ref[h, r:r + 1, tok]
            ar = a_ref[rows, tok]
            act = 0.5 * ar * (1.0 + lax.erf(ar * (2.0 ** -0.5)))
            w_ref[rows, tok] = (gate * act).astype(jnp.bfloat16)
    acc_ref[...] += jnp.dot(vt_ref[0], w_ref[...], preferred_element_type=jnp.float32)

    @pl.when(e == pl.num_programs(1) - 1)
    def _():
        r = ALPHA * h_ref[...] + acc_ref[...].T
        mu = jnp.mean(r, axis=-1, keepdims=True)
        d = r - mu
        var = jnp.mean(d * d, axis=-1, keepdims=True)
        o_ref[...] = d * lax.rsqrt(var + LN_EPS) * g_ref[...] + b_ref[...]


def peer_tail(h, ht, q, sub_keys, u_bf, vt_bf, ln_g, ln_b, tt):
    n, d = h.shape
    s, e0z, e1, tau = peer_scores(q, sub_keys, tt)
    n_exp = u_bf.shape[0]
    return pl.pallas_call(
        _peer_dense_kernel,
        out_shape=jax.ShapeDtypeStruct((n, d), jnp.float32),
        grid=(n // tt, n_exp // PEER_TILE),
        in_specs=[pl.BlockSpec((d, tt), lambda i, e: (0, i)),
                  pl.BlockSpec((tt, d), lambda i, e: (i, 0)),
                  pl.BlockSpec((PEER_TILE, d), lambda i, e: (e, 0)),
                  pl.BlockSpec((1, d, PEER_TILE), lambda i, e: (e, 0, 0)),
                  pl.BlockSpec((PEER_COMBOS, PEER_KEY_ROWS, tt), lambda i, e: (0, e, i)),
                  pl.BlockSpec((PEER_HEADS, PEER_KEY_ROWS, tt), lambda i, e: (0, e, i)),
                  pl.BlockSpec((PEER_COMBOS, N_KEYS, tt), lambda i, e: (0, 0, i)),
                  pl.BlockSpec((PEER_HEADS, N_KEYS, tt), lambda i, e: (0, 0, i)),
                  pl.BlockSpec((PEER_HEADS, tt), lambda i, e: (0, i)),
                  pl.BlockSpec((1, d), lambda i, e: (0, 0)),
                  pl.BlockSpec((1, d), lambda i, e: (0, 0))],
        out_specs=pl.BlockSpec((tt, d), lambda i, e: (i, 0)),
        scratch_shapes=[pltpu.VMEM((d, tt), jnp.float32), pltpu.VMEM((PEER_TILE, tt), jnp.float32),
                        pltpu.VMEM((PEER_TILE, tt), jnp.bfloat16)],
        compiler_params=pltpu.CompilerParams(dimension_semantics=("arbitrary", "arbitrary"),
                                             vmem_limit_bytes=VMEM_LIMIT_BYTES),
        name="peer_dense",
    )(ht, h, u_bf, vt_bf, s, e0z, s, e1, tau, ln_g[None], ln_b[None])


def _out_proj_kernel(x_ref, nsa_ref, m_ref, wn_ref, wm_ref, g_ref, b_ref, wq_ref, h_ref, ht_ref, q_ref):
    f32, bf16 = jnp.float32, jnp.bfloat16
    r = (ALPHA * x_ref[...] + jnp.dot(nsa_ref[...].astype(bf16), wn_ref[...], preferred_element_type=f32)
         + jnp.dot(m_ref[...].astype(bf16), wm_ref[...], preferred_element_type=f32))
    mu = jnp.mean(r, axis=-1, keepdims=True)
    d = r - mu
    var = jnp.mean(d * d, axis=-1, keepdims=True)
    h = d * lax.rsqrt(var + LN_EPS) * g_ref[...] + b_ref[...]
    h_ref[...] = h
    ht_ref[...] = h.T.astype(bf16)
    q_ref[...] = jnp.dot(h.astype(bf16), wq_ref[...], preferred_element_type=f32)


def out_proj_fused(x, o_nsa, o_m, w_out, ln_g, ln_b, w_pq, tm):
    n, d = x.shape
    assert n % tm == 0
    bf16 = jnp.bfloat16
    nq = w_pq.shape[1]

    def rows(width):
        return pl.BlockSpec((tm, width), lambda i: (i, 0))

    def whole(a):
        return pl.BlockSpec(a.shape, lambda i: (0, 0))

    wn = w_out[:NSA_QW].astype(bf16)
    wm = w_out[NSA_QW:].astype(bf16)
    wq = w_pq.astype(bf16)
    g, b = ln_g[None], ln_b[None]
    return pl.pallas_call(
        _out_proj_kernel,
        out_shape=(jax.ShapeDtypeStruct((n, d), jnp.float32), jax.ShapeDtypeStruct((d, n), bf16),
                   jax.ShapeDtypeStruct((n, nq), jnp.float32)),
        grid=(n // tm,),
        in_specs=[rows(d), rows(NSA_QW), rows(M_W), whole(wn), whole(wm), whole(g), whole(b), whole(wq)],
        out_specs=(rows(d), pl.BlockSpec((d, tm), lambda i: (0, i)), rows(nq)),
        compiler_params=pltpu.CompilerParams(dimension_semantics=("arbitrary",),
                                             vmem_limit_bytes=VMEM_LIMIT_BYTES),
        name="out_proj",
    )(x, o_nsa, o_m, wn, wm, g, b, wq)


def block_tail(x, o_nsa, o_m, w_out, ln_g, ln_b, w_pq, sub_keys, u_bf, vt_bf, tt):
    lead = x.shape[:-1]
    h, ht, q = out_proj_fused(x.reshape(-1, D_MODEL), o_nsa.reshape(-1, NSA_QW), o_m.reshape(-1, M_W),
                              w_out, ln_g[0], ln_b[0], w_pq, tt)
    return peer_tail(h, ht, q, sub_keys, u_bf, vt_bf, ln_g[1], ln_b[1], tt).reshape(*lead, D_MODEL)


def prompt_mix(x, w_in, pe, w1, b1, w2, conv_w, conv_b, b_if):
    B, S, _ = x.shape
    z = in_proj_fused(x, w_in, jnp.arange(S), 512, True)
    kc = compress_chunks(z["k_cmp"], S // CMP_STRIDE, pe[0], w1[0], b1[0], w2[0])
    vc = compress_chunks(z["v_cmp"], S // CMP_STRIDE, pe[1], w1[1], b1[1], w2[1])
    zgate = z["zgate"].reshape(B, S, GATE_W)
    o_nsa = nsa_prompt_attention(z["q"].reshape(B, S, NSA_QW), kc, vc, z["k_slc_bf"], z["v_slc_bf"],
                                 z["k_win_bf"], z["v_win_bf"], zgate[..., :IN_SPLITS[2]])
    zqk = z["zqk"].reshape(B, S, 2 * M_W)
    o_m, C, n, m = mlstm_prompt(zqk, z["zv"].reshape(B, S, M_W), z["zo"].reshape(B, S, M_W),
                                zgate[..., IN_SPLITS[2]:], conv_w, conv_b, b_if)
    buf = zqk[:, S - (CONV_W - 1):]
    wl = min(WINDOW, S)
    k_cmp, v_cmp, k_slc, v_slc, k_win, v_win = [
        z[k + "_cache"].reshape(B, NSA_KV_HEADS, NSA_HD, S).transpose(0, 3, 1, 2)
        for k in ("k_cmp", "v_cmp", "k_slc", "v_slc", "k_win", "v_win")]
    return (o_nsa, o_m), (k_cmp, v_cmp, k_slc, v_slc, k_win[:, S - wl:], v_win[:, S - wl:], C, n, m, buf)


def sample_mix(x, kc_pool, vc_pool, ks_pool, vs_pool, kw_buf, vw_buf, C0, n0, m0, buf0, page_table,
               w_in, pe, w1, b1, w2, conv_w, conv_b, b_if):
    B, T, _ = x.shape
    past = page_table.shape[1] * PAGE_SIZE
    pos = past + jnp.arange(T)
    z = in_proj_fused(x, w_in, pos, B * T, False)
    q = z["q"].astype(jnp.float32).reshape(B, T, NSA_HEADS, NSA_HD) * (1.0 / ATTN_SCALE)
    k_cmp, v_cmp, k_slc, v_slc, k_win, v_win = [
        z[k].reshape(B, T, NSA_KV_HEADS, NSA_HD) for k in ("k_cmp", "v_cmp", "k_slc", "v_slc", "k_win", "v_win")]
    zgate = z["zgate"].reshape(B, T, GATE_W)
    gates = jax.nn.sigmoid(zgate[..., :IN_SPLITS[2]]).reshape(B, T, NSA_HEADS, 3)
    zqk, zv, zo, zif = (z["zqk"].reshape(B, T, 2 * M_W), z["zv"].reshape(B, T, M_W), z["zo"].reshape(B, T, M_W),
                        zgate[..., IN_SPLITS[2]:])

    assert (past + T) // CMP_STRIDE == past // CMP_STRIDE

    def compressed(pool, c):
        pages = pool.transpose(0, 2, 3, 1)[page_table]
        return compress_pages(pages, pe[c], w1[c], b1[c], w2[c])

    o_cmp, p = cmp_attend(q, pos, compressed(kc_pool, 0), compressed(vc_pool, 1))
    n_sel = -(-(past + T) // SEL_BLOCK)
    member = select_blocks(p, pos, n_sel)
    o_sel = sample_selected_attention(q, pos, member, ks_pool, vs_pool, k_slc, v_slc, page_table)
    wb = kw_buf.shape[1]
    kw = jnp.concatenate([kw_buf, k_win], axis=1)
    vw = jnp.concatenate([vw_buf, v_win], axis=1)
    kpos = past - wb + jnp.arange(wb + T)
    o_win = win_attend(q, pos, kw, vw, kpos)
    o_nsa = nsa_combine(gates, o_cmp, o_sel, o_win)
    o_m, (C, n, m, buf) = mlstm_mix(zqk, zv, zo, zif, buf0, C0, n0, m0, conv_w, conv_b, b_if, T)
    return (o_nsa, o_m), (k_cmp, v_cmp, k_slc, v_slc, kw[:, T:], vw[:, T:], C, n, m, buf)


def kernel(x_prompt, x_sample, cache_k_cmp, cache_v_cmp, cache_k_slc, cache_v_slc, cache_k_win, cache_v_win,
           state_C, state_n, state_m, state_conv, page_table, w_in, w_out, w_phi1, b_phi1, w_phi2, pe_cmp,
           conv_w, conv_b, b_if, ln_g, ln_b, w_pq, sub_keys, u_tab, v_tab):
    l = 0
    mix_p, st_p = prompt_mix(x_prompt, w_in[l], pe_cmp[l], w_phi1[l], b_phi1[l], w_phi2[l],
                             conv_w[l], conv_b[l], b_if[l])
    mix_s, st_s = sample_mix(x_sample, cache_k_cmp[l], cache_v_cmp[l], cache_k_slc[l], cache_v_slc[l],
                             cache_k_win[l], cache_v_win[l], state_C[l], state_n[l], state_m[l],
                             state_conv[l], page_table, w_in[l], pe_cmp[l], w_phi1[l], b_phi1[l],
                             w_phi2[l], conv_w[l], conv_b[l], b_if[l])
    u_bf = u_tab[l].astype(jnp.bfloat16)
    vt_bf = v_tab[l].astype(jnp.bfloat16).reshape(-1, PEER_TILE, D_MODEL).transpose(0, 2, 1)
    xp = block_tail(x_prompt, *mix_p, w_out[l], ln_g[l], ln_b[l], w_pq[l], sub_keys[l], u_bf, vt_bf, 512)
    xs = block_tail(x_sample, *mix_s, w_out[l], ln_g[l], ln_b[l], w_pq[l], sub_keys[l], u_bf, vt_bf, 128)
    return (xp, xs) + tuple(a[None] for a in st_p) + tuple(a[None] for a in st_s)
```

```python
import functools

import jax
import jax.numpy as jnp
import numpy as np
from jax import lax
from jax.experimental import pallas as pl
from jax.experimental.pallas import tpu as pltpu

D_MODEL = 1024
DEPTH = 1
PAGE_SIZE = 128
NSA_HEADS = 8
NSA_KV_HEADS = 2
NSA_GROUP = NSA_HEADS // NSA_KV_HEADS
NSA_HD = 64
NSA_QW = NSA_HEADS * NSA_HD
NSA_KVW = NSA_KV_HEADS * NSA_HD
CMP_BLOCK = 32
CMP_STRIDE = 16
SEL_BLOCK = 64
SEL_TOP = 16
WINDOW = 512
Q_BLOCK = 64
ATTN_SCALE = NSA_HD ** -0.5
ROPE_THETA = 10000.0
M_HEADS = 4
M_HD = 128
M_W = M_HEADS * M_HD
M_CHUNK = 64
CONV_W = 4
PEER_HEADS = 8
N_KEYS = 128
PEER_TOPK = 16
PEER_QDIM = 256
PEER_BLOCK = 128
IN_SPLITS = (NSA_QW, 6 * NSA_KVW, 3 * NSA_HEADS, 2 * M_W, M_W, M_W, 2 * M_HEADS)
LN_EPS = 1e-5
ALPHA = (2 * DEPTH) ** 0.25

VMEM_LIMIT_BYTES = 56 * 1024 * 1024


def _mm_kernel(x_ref, w_ref, o_ref):
    o_ref[...] = jnp.dot(x_ref[...].astype(jnp.bfloat16), w_ref[...], preferred_element_type=jnp.float32)


def pallas_matmul(x, w, tm=512):
    M, K = x.shape
    N = w.shape[1]
    tm = min(tm, M)
    assert M % tm == 0
    return pl.pallas_call(
        _mm_kernel,
        out_shape=jax.ShapeDtypeStruct((M, N), jnp.float32),
        grid=(M // tm,),
        in_specs=[pl.BlockSpec((tm, K), lambda i: (i, 0)), pl.BlockSpec((K, N), lambda i: (0, 0))],
        out_specs=pl.BlockSpec((tm, N), lambda i: (i, 0)),
        compiler_params=pltpu.CompilerParams(dimension_semantics=("arbitrary",),
                                             vmem_limit_bytes=VMEM_LIMIT_BYTES),
        name="proj_matmul",
    )(x, w.astype(jnp.bfloat16))


def mm3(x, w):
    lead = x.shape[:-1]
    return pallas_matmul(x.reshape(-1, x.shape[-1]), w).reshape(*lead, w.shape[1])


def layer_norm(x, g, b):
    mu = x.mean(-1, keepdims=True)
    var = jnp.square(x - mu).mean(-1, keepdims=True)
    return (x - mu) * lax.rsqrt(var + LN_EPS) * g + b


def rope(x, pos):
    half = x.shape[-1] // 2
    inv = ROPE_THETA ** (-jnp.arange(half, dtype=jnp.float32) / half)
    ang = pos.astype(jnp.float32)[:, None] * inv[None, :]
    cos = jnp.cos(ang)[:, None, :]
    sin = jnp.sin(ang)[:, None, :]
    x1, x2 = x[..., :half], x[..., half:]
    return jnp.concatenate([x1 * cos - x2 * sin, x2 * cos + x1 * sin], axis=-1)


def split_in_proj(x, w_in):
    z = mm3(x, w_in)
    cuts = [int(c) for c in np.cumsum(IN_SPLITS)[:-1]]
    return jnp.split(z, cuts, axis=-1)


_IN_OFF = np.concatenate([[0], np.cumsum(IN_SPLITS)])
_IN_ORDER = (0, 1, 3, 4, 5, 2, 6)
_N_KV_ROWS = 6
_KV_BF16 = (2, 3, 4, 5)
_KV_CACHE = (0, 1, 2, 3, 4, 5)
GATE_W = IN_SPLITS[2] + IN_SPLITS[6]


def _rope_pairs(x, cos, sin_signed):
    half = NSA_HD // 2
    lane = lax.broadcasted_iota(jnp.int32, x.shape, 1)
    partner = jnp.where(lane % NSA_HD < half, pltpu.roll(x, LANES - half, 1), pltpu.roll(x, half, 1))
    return x * cos + partner * sin_signed


def _in_proj_kernel(x_ref, w_ref, cos_ref, sin_ref, q_ref, *rest, kv_major):
    kv_refs = rest[:_N_KV_ROWS]
    rest = rest[_N_KV_ROWS:]
    if kv_major:
        bf_refs, rest = rest[:len(_KV_BF16)], rest[len(_KV_BF16):]
        cache_refs, rest = rest[:len(_KV_CACHE)], rest[len(_KV_CACHE):]
    zqk_ref, zv_ref, zo_ref, zgate_ref = rest
    z = jnp.dot(x_ref[...].astype(jnp.bfloat16), w_ref[...], preferred_element_type=jnp.float32)
    cos = cos_ref[...]
    sin = sin_ref[...]
    for g in range(NSA_QW // LANES):
        sl = slice(g * LANES, (g + 1) * LANES)
        q_ref[:, sl] = (_rope_pairs(z[:, sl], cos, sin) * ATTN_SCALE).astype(jnp.bfloat16)
    for r in range(_N_KV_ROWS):
        row = z[:, NSA_QW + r * NSA_KVW:NSA_QW + (r + 1) * NSA_KVW]
        if r % 2 == 0:
            row = _rope_pairs(row, cos, sin)
        kv_refs[r][...] = row
        if kv_major and r in _KV_CACHE:
            dst = cache_refs[_KV_CACHE.index(r)]
            dst[0] = row.T
        if kv_major and r in _KV_BF16:
            dst = bf_refs[_KV_BF16.index(r)]
            for n in range(NSA_KV_HEADS):
                dst[0, n] = row[:, n * NSA_HD:(n + 1) * NSA_HD].astype(jnp.bfloat16)
    o = NSA_QW + _N_KV_ROWS * NSA_KVW
    zqk_ref[...] = z[:, o:o + 2 * M_W]
    zv_ref[...] = z[:, o + 2 * M_W:o + 3 * M_W]
    zo_ref[...] = z[:, o + 3 * M_W:o + 4 * M_W]
    zgate_ref[...] = z[:, o + 4 * M_W:o + 4 * M_W + GATE_W]


def in_proj_fused(x, w_in, pos, tm, kv_major):
    B, T, D = x.shape
    M = B * T
    assert M % tm == 0 and NSA_KVW == LANES and (not kv_major or T % tm == 0)
    f32, bf16 = jnp.float32, jnp.bfloat16
    w = jnp.concatenate([w_in[:, _IN_OFF[i]:_IN_OFF[i + 1]] for i in _IN_ORDER], axis=1).astype(bf16)
    half = NSA_HD // 2
    inv = ROPE_THETA ** (-jnp.arange(half, dtype=f32) / half)
    ang = pos.astype(f32)[:, None] * inv[None, :]
    cos = jnp.tile(jnp.cos(ang), (B, 2 * LANES // NSA_HD))
    sin = jnp.tile(jnp.concatenate([-jnp.sin(ang), jnp.sin(ang)], axis=1), (B, LANES // NSA_HD))
    n_w = w.shape[1]

    def rows(width):
        return pl.BlockSpec((tm, width), lambda i: (i, 0))

    out_shape = [jax.ShapeDtypeStruct((M, NSA_QW), bf16)] + [jax.ShapeDtypeStruct((M, NSA_KVW), f32)] * _N_KV_ROWS
    out_specs = [rows(NSA_QW)] + [rows(NSA_KVW)] * _N_KV_ROWS
    if kv_major:
        per_seq = T // tm
        out_shape += [jax.ShapeDtypeStruct((B, NSA_KV_HEADS, T, NSA_HD), bf16)] * len(_KV_BF16)
        out_specs += [pl.BlockSpec((1, NSA_KV_HEADS, tm, NSA_HD),
                                   lambda i: (i // per_seq, 0, i % per_seq, 0))] * len(_KV_BF16)
        out_shape += [jax.ShapeDtypeStruct((B, NSA_KVW, T), f32)] * len(_KV_CACHE)
        out_specs += [pl.BlockSpec((1, NSA_KVW, tm), lambda i: (i // per_seq, 0, i % per_seq))] * len(_KV_CACHE)
    out_shape += [jax.ShapeDtypeStruct((M, 2 * M_W), f32), jax.ShapeDtypeStruct((M, M_W), f32),
                  jax.ShapeDtypeStruct((M, M_W), f32), jax.ShapeDtypeStruct((M, GATE_W), f32)]
    out_specs += [rows(2 * M_W), rows(M_W), rows(M_W), rows(GATE_W)]
    outs = pl.pallas_call(
        functools.partial(_in_proj_kernel, kv_major=kv_major),
        out_shape=tuple(out_shape),
        grid=(M // tm,),
        in_specs=[rows(D), pl.BlockSpec((D, n_w), lambda i: (0, 0)), rows(LANES), rows(LANES)],
        out_specs=tuple(out_specs),
        compiler_params=pltpu.CompilerParams(dimension_semantics=("arbitrary",),
                                             vmem_limit_bytes=VMEM_LIMIT_BYTES),
        name="in_proj",
    )(x.reshape(M, D), w, cos, sin)
    names = ["q", "k_cmp", "v_cmp", "k_slc", "v_slc", "k_win", "v_win"]
    if kv_major:
        names += ["k_slc_bf", "v_slc_bf", "k_win_bf", "v_win_bf"]
        names += ["k_cmp_cache", "v_cmp_cache", "k_slc_cache", "v_slc_cache", "k_win_cache", "v_win_cache"]
    names += ["zqk", "zv", "zo", "zgate"]
    return dict(zip(names, outs))


def nsa_project(zq, zkv, zg, pos):
    B, T, _ = zq.shape
    q = rope(zq.reshape(B, T, NSA_HEADS, NSA_HD), pos)
    kv = zkv.reshape(B, T, 6, NSA_KV_HEADS, NSA_HD)
    rows = (rope(kv[:, :, 0], pos), kv[:, :, 1], rope(kv[:, :, 2], pos), kv[:, :, 3],
            rope(kv[:, :, 4], pos), kv[:, :, 5])
    gates = jax.nn.sigmoid(zg).reshape(B, T, NSA_HEADS, 3)
    return q, rows, gates


def _expanded_w1(w1):
    assert CMP_BLOCK == 2 * CMP_STRIDE
    w1r = w1.reshape(2, CMP_STRIDE, NSA_HD, w1.shape[-1])
    wbig = jnp.einsum('hpdf,kn->pkdnhf', w1r, jnp.eye(NSA_KV_HEADS, dtype=w1.dtype))
    return wbig.reshape(CMP_STRIDE * NSA_KVW, 2 * NSA_KV_HEADS * w1.shape[-1])


def _compress_rows(x, w_ref, bias_ref, w2_ref, o_ref):
    f32, bf16 = jnp.float32, jnp.bfloat16
    rows = x.shape[0]
    f = w2_ref.shape[0]
    proj = jnp.dot(x.astype(bf16), w_ref[...], preferred_element_type=f32)
    for n in range(NSA_KV_HEADS):
        first = proj[:, 2 * n * f:(2 * n + 1) * f]
        second = pltpu.roll(proj[:, (2 * n + 1) * f:(2 * n + 2) * f], rows - 1, 0)
        pre = first + second + bias_ref[...]
        hid = 0.5 * pre * (1.0 + lax.erf(pre * (2.0 ** -0.5)))
        o_ref[:, n * NSA_HD:(n + 1) * NSA_HD] = jnp.dot(hid.astype(bf16), w2_ref[...], preferred_element_type=f32)


def _compress_chunks_kernel(x_ref, w_ref, bias_ref, w2_ref, o_ref):
    _compress_rows(x_ref[...], w_ref, bias_ref, w2_ref, o_ref)


def _compress_weights(pe, w1, b1, w2):
    bf16 = jnp.bfloat16
    bias = jnp.dot(pe.reshape(-1), w1, precision=lax.Precision.HIGHEST) + b1
    return _expanded_w1(w1).astype(bf16), bias[None], w2.astype(bf16)


def compress_chunks(rows, per_seq, pe, w1, b1, w2, tm=512):
    chunks = rows.reshape(-1, CMP_STRIDE * NSA_KVW)
    n = chunks.shape[0]
    tm = min(tm, n)
    assert n % tm == 0 and tm % per_seq == 0
    wbig, bias, w2b = _compress_weights(pe, w1, b1, w2)

    def whole(a):
        return pl.BlockSpec(a.shape, lambda i: (0, 0))

    out = pl.pallas_call(
        _compress_chunks_kernel,
        out_shape=jax.ShapeDtypeStruct((n, NSA_KVW), jnp.float32),
        grid=(n // tm,),
        in_specs=[pl.BlockSpec((tm, chunks.shape[1]), lambda i: (i, 0)), whole(wbig), whole(bias), whole(w2b)],
        out_specs=pl.BlockSpec((tm, NSA_KVW), lambda i: (i, 0)),
        compiler_params=pltpu.CompilerParams(dimension_semantics=("arbitrary",),
                                             vmem_limit_bytes=VMEM_LIMIT_BYTES),
        name="compress_chunks",
    )(chunks, wbig, bias, w2b)
    return out.reshape(n // per_seq, per_seq, NSA_KV_HEADS, NSA_HD)[:, :-1]


PAGE_GROUP = 4


def _compress_pages_kernel(pg_ref, w_ref, bias_ref, w2_ref, o_ref, x_ref, t_ref):
    n_pages = pg_ref.shape[1]
    per_page = PAGE_SIZE // CMP_STRIDE
    group = PAGE_GROUP

    def place(i, carry):
        for u in range(group):
            g = i * group + u
            t_ref[u] = pg_ref[0, g].reshape(NSA_KVW, PAGE_SIZE).T
            row0 = pl.multiple_of(g * per_page, per_page)
            for p in range(CMP_STRIDE):
                x_ref[pl.ds(row0, per_page), p * NSA_KVW:(p + 1) * NSA_KVW] = (
                    t_ref.at[u][pl.ds(p, per_page, stride=CMP_STRIDE), :])
        return carry

    lax.fori_loop(0, n_pages // group, place, 0)
    _compress_rows(x_ref[...], w_ref, bias_ref, w2_ref, o_ref.at[0])


def compress_pages(pages, pe, w1, b1, w2):
    B, n_pages = pages.shape[:2]
    assert pages.shape[2:] == (NSA_KV_HEADS, NSA_HD, PAGE_SIZE) and NSA_KVW == LANES and PAGE_SIZE == LANES
    assert n_pages % PAGE_GROUP == 0
    wbig, bias, w2b = _compress_weights(pe, w1, b1, w2)
    rows = n_pages * (PAGE_SIZE // CMP_STRIDE)

    def whole(a):
        return pl.BlockSpec(a.shape, lambda b: (0, 0))

    out = pl.pallas_call(
        _compress_pages_kernel,
        out_shape=jax.ShapeDtypeStruct((B, rows, NSA_KVW), jnp.float32),
        grid=(B,),
        in_specs=[pl.BlockSpec((1, n_pages, NSA_KV_HEADS, NSA_HD, PAGE_SIZE), lambda b: (b, 0, 0, 0, 0)),
                  whole(wbig), whole(bias), whole(w2b)],
        out_specs=pl.BlockSpec((1, rows, NSA_KVW), lambda b: (b, 0, 0)),
        scratch_shapes=[pltpu.VMEM((rows, wbig.shape[0]), jnp.float32),
                        pltpu.VMEM((PAGE_GROUP, PAGE_SIZE, NSA_KVW), jnp.float32)],
        compiler_params=pltpu.CompilerParams(dimension_semantics=("arbitrary",),
                                             vmem_limit_bytes=VMEM_LIMIT_BYTES),
        name="compress_pages",
    )(pages, wbig, bias, w2b)
    return out.reshape(B, rows, NSA_KV_HEADS, NSA_HD)[:, :-1]


def cmp_attend(q, qpos, kc, vc):
    B, T = q.shape[:2]
    qg = q.reshape(B, T, NSA_KV_HEADS, NSA_GROUP, NSA_HD)
    s = jnp.einsum('btngd,bcnd->btngc', qg, kc) * ATTN_SCALE
    nblk = kc.shape[1]
    blk_end = jnp.arange(nblk) * CMP_STRIDE + CMP_BLOCK - 1
    valid = (blk_end[None, :] <= qpos[:, None])[None, :, None, None, :]
    p = jax.nn.softmax(jnp.where(valid, s, -1e30), axis=-1) * valid
    o = jnp.einsum('btngc,bcnd->btngd', p, vc)
    return o.reshape(B, T, NSA_HEADS, NSA_HD), p


def select_blocks(p, qpos, n_sel):
    imp = p.sum(axis=3)
    R = SEL_BLOCK // CMP_STRIDE
    r = CMP_BLOCK // CMP_STRIDE
    nb = imp.shape[-1]
    right = n_sel * R + R - 1 - nb
    padded = jnp.pad(imp, ((0, 0), (0, 0), (0, 0), (r - 1, right)))
    score = padded[..., 0:(n_sel - 1) * R + 1:R]
    for o in range(1, R + r - 1):
        score = score + padded[..., o:o + (n_sel - 1) * R + 1:R]
    j = jnp.arange(n_sel)[None, :]
    cur = (qpos // SEL_BLOCK)[:, None]
    valid = (j * SEL_BLOCK <= qpos[:, None])[None, :, None, :]
    forced = ((j == 0) | (j == cur) | (j == cur - 1))[None, :, None, :]
    score = jnp.where(forced, jnp.inf, jnp.where(valid, score, -jnp.inf))
    idx = j[0]
    before = (score[..., None, :] > score[..., :, None]) | ((score[..., None, :] == score[..., :, None])
                                                          & (idx[None, :] < idx[:, None]))
    return before.sum(-1) < min(SEL_TOP, n_sel)


def sample_selected_attention(q, qpos, member, k_pool, v_pool, k_new, v_new, page_table):
    B, T = q.shape[:2]
    n_pages = page_table.shape[1]
    per_page = PAGE_SIZE // SEL_BLOCK
    assert member.shape[-1] == n_pages * per_page + 1 and T <= SEL_BLOCK
    kp = k_pool.transpose(0, 2, 3, 1)[page_table]
    vp = v_pool.transpose(0, 2, 3, 1)[page_table]
    qg = q.reshape(B, T, NSA_KV_HEADS, NSA_GROUP, NSA_HD)
    s_past = jnp.einsum('btngd,bpndk->bntgpk', qg, kp) * ATTN_SCALE
    s_new = jnp.einsum('btngd,bsnd->bntgs', qg, k_new) * ATTN_SCALE
    m = member.transpose(0, 2, 1, 3)
    m_past = jnp.repeat(m[..., :-1].reshape(B, NSA_KV_HEADS, T, n_pages, per_page), SEL_BLOCK, axis=-1)
    kpos = (jnp.arange(n_pages) * PAGE_SIZE)[:, None] + jnp.arange(PAGE_SIZE)[None, :]
    m_past = m_past & (kpos[None, None, None] <= qpos[None, None, :, None, None])
    new_pos = n_pages * PAGE_SIZE + jnp.arange(T)
    m_new = m[..., -1:] & (new_pos[None, None, None, :] <= qpos[None, None, :, None])
    logits = jnp.concatenate(
        [jnp.where(m_past[:, :, :, None], s_past, -jnp.inf).reshape(B, NSA_KV_HEADS, T, NSA_GROUP, -1),
         jnp.where(m_new[:, :, :, None], s_new, -jnp.inf)], axis=-1)
    pr = jax.nn.softmax(logits, axis=-1)
    pr_past = pr[..., :n_pages * PAGE_SIZE].reshape(B, NSA_KV_HEADS, T, NSA_GROUP, n_pages, PAGE_SIZE)
    o = (jnp.einsum('bntgpk,bpndk->bntgd', pr_past, vp)
         + jnp.einsum('bntgs,bsnd->bntgd', pr[..., n_pages * PAGE_SIZE:], v_new))
    return o.transpose(0, 2, 1, 3, 4).reshape(B, T, NSA_HEADS, NSA_HD)


def to_blocks(rows, n_sel):
    B, L, KV, hd = rows.shape
    rows = jnp.pad(rows, ((0, 0), (0, n_sel * SEL_BLOCK - L), (0, 0), (0, 0)))
    return rows.reshape(B, n_sel, SEL_BLOCK, KV, hd).transpose(0, 3, 1, 2, 4)


def take_rows(table, idx):
    return table[idx]


def sel_attend(q, qpos, sel, kb, vb):
    B, Tq = q.shape[:2]
    k = sel.shape[-1]
    sel_t = sel.transpose(0, 2, 1, 3)
    gather = jax.vmap(jax.vmap(take_rows))
    kg = gather(kb, sel_t).reshape(B, NSA_KV_HEADS, Tq, k * SEL_BLOCK, NSA_HD)
    vg = gather(vb, sel_t).reshape(B, NSA_KV_HEADS, Tq, k * SEL_BLOCK, NSA_HD)
    kpos = (sel_t[..., None] * SEL_BLOCK + jnp.arange(SEL_BLOCK)).reshape(B, NSA_KV_HEADS, Tq, k * SEL_BLOCK)
    qg = q.reshape(B, Tq, NSA_KV_HEADS, NSA_GROUP, NSA_HD).transpose(0, 2, 1, 3, 4)
    s = jnp.einsum('bntgd,bntsd->bntgs', qg, kg) * ATTN_SCALE
    mask = kpos[:, :, :, None, :] <= qpos[None, None, :, None, None]
    pr = jax.nn.softmax(jnp.where(mask, s, -jnp.inf), axis=-1)
    o = jnp.einsum('bntgs,bntsd->bntgd', pr, vg)
    return o.transpose(0, 2, 1, 3, 4).reshape(B, Tq, NSA_HEADS, NSA_HD)


def win_attend(q, qpos, k, v, kpos):
    B, Tq = q.shape[:2]
    qg = q.reshape(B, Tq, NSA_KV_HEADS, NSA_GROUP, NSA_HD)
    s = jnp.einsum('btngd,bsnd->btngs', qg, k) * ATTN_SCALE
    diff = qpos[:, None] - kpos[None, :]
    mask = ((diff >= 0) & (diff < WINDOW) & (kpos[None, :] >= 0))[None, :, None, None, :]
    pr = jax.nn.softmax(jnp.where(mask, s, -jnp.inf), axis=-1)
    o = jnp.einsum('btngs,bsnd->btngd', pr, v)
    return o.reshape(B, Tq, NSA_HEADS, NSA_HD)


def nsa_combine(gates, o_cmp, o_sel, o_win):
    B, T = gates.shape[:2]
    o = gates[..., 0:1] * o_cmp + gates[..., 1:2] * o_sel + gates[..., 2:3] * o_win
    return o.reshape(B, T, NSA_QW)


NSA_TQ = 128
NSA_CK = 512
MASKED = -1e30


def _softmax_rows(s):
    m = jnp.max(s, axis=-1, keepdims=True)
    e = jnp.exp(s - m)
    return e / jnp.sum(e, axis=-1, keepdims=True)


def _nsa_prompt_kernel(q_ref, kc_ref, vc_ref, ks_ref, vs_ref, kw_ref, vw_ref, zg_ref, msel_ref, exp_ref, o_ref):
    f32, bf16 = jnp.float32, jnp.bfloat16
    tq = NSA_TQ
    q0 = pl.program_id(2) * tq
    qb = q_ref[0]
    qs = jnp.concatenate([qb[:, g * NSA_HD:(g + 1) * NSA_HD] for g in range(NSA_GROUP)], axis=0)
    tpos = q0 + lax.broadcasted_iota(jnp.int32, (tq, 1), 0)

    def per_head(a):
        return jnp.concatenate([a] * NSA_GROUP, axis=0)

    s = lax.dot_general(qs, kc_ref[0, 0], _NT, preferred_element_type=f32)
    cblk = lax.broadcasted_iota(jnp.int32, (tq, 128), 1)
    cvalid = cblk * CMP_STRIDE + (CMP_BLOCK - 1) <= tpos
    s = s + per_head(jnp.where(cvalid, 0.0, MASKED))
    e = jnp.exp(s - jnp.max(s, axis=-1, keepdims=True)) * per_head(jnp.where(cvalid, 1.0, 0.0))
    l = jnp.sum(e, axis=-1, keepdims=True)
    p = e / jnp.where(l > 0.0, l, 1.0)
    o_cmp = jnp.dot(p.astype(bf16), vc_ref[0, 0], preferred_element_type=f32)

    imp = p[0:tq]
    for g in range(1, NSA_GROUP):
        imp = imp + p[g * tq:(g + 1) * tq]
    hi = imp.astype(bf16)
    r1 = imp - hi.astype(f32)
    mid = r1.astype(bf16)
    lo = (r1 - mid.astype(f32)).astype(bf16)
    msel = msel_ref[...]
    score = (lax.dot_general(msel, hi, _NT, preferred_element_type=f32)
             + lax.dot_general(msel, mid, _NT, preferred_element_type=f32)
             + lax.dot_general(msel, lo, _NT, preferred_element_type=f32))
    n_sel = score.shape[0]
    j = lax.broadcasted_iota(jnp.int32, (n_sel, tq), 0)
    tok = q0 + lax.broadcasted_iota(jnp.int32, (n_sel, tq), 1)
    cur = tok // SEL_BLOCK
    forced = (j == 0) | (j == cur) | (j == cur - 1)
    score = jnp.where(forced, jnp.inf, jnp.where(j * SEL_BLOCK <= tok, score, -jnp.inf))
    rank = jnp.zeros((n_sel, tq), f32)
    for jp in range(n_sel):
        row = score[jp:jp + 1, :]
        before = (row > score) | ((row == score) & (j > jp))
        rank = rank + jnp.where(before, 1.0, 0.0)
    chosen_t = jnp.where(rank < SEL_TOP, 1.0, 0.0)
    sel01 = jnp.concatenate([chosen_t, jnp.zeros((LANES - n_sel, tq), f32)], axis=0).T.astype(bf16)

    ck = NSA_CK
    rows = NSA_GROUP * tq

    def sel_chunk(c, carry):
        m, l, acc = carry
        k0 = pl.multiple_of(c * ck, ck)
        s = lax.dot_general(qs, ks_ref[0, 0, pl.ds(k0, ck), :], _NT, preferred_element_type=f32)
        chosen = jnp.dot(sel01, exp_ref[c], preferred_element_type=f32)
        kpos = k0 + lax.broadcasted_iota(jnp.int32, (tq, ck), 1)
        ok = (chosen > 0.5) & (kpos <= tpos)
        s = s + per_head(jnp.where(ok, 0.0, MASKED))
        m_new = jnp.maximum(m, jnp.max(s, axis=-1, keepdims=True))
        a = jnp.exp(m - m_new)
        pr = jnp.exp(s - m_new)
        l = a * l + jnp.sum(pr, axis=-1, keepdims=True)
        acc = a * acc + jnp.dot(pr.astype(bf16), vs_ref[0, 0, pl.ds(k0, ck), :], preferred_element_type=f32)
        return m_new, l, acc

    init = (jnp.full((rows, 1), MASKED, f32), jnp.zeros((rows, 1), f32), jnp.zeros((rows, NSA_HD), f32))
    n_chunks = (q0 + tq + ck - 1) // ck
    _, l_sel, acc_sel = lax.fori_loop(0, n_chunks, sel_chunk, init)
    o_sel = acc_sel / l_sel

    w0 = pl.multiple_of(jnp.maximum(q0 - WINDOW, 0), tq)
    wl = WINDOW + tq
    s = lax.dot_general(qs, kw_ref[0, 0, pl.ds(w0, wl), :], _NT, preferred_element_type=f32)
    diff = tpos - (w0 + lax.broadcasted_iota(jnp.int32, (tq, wl), 1))
    s = s + per_head(jnp.where((diff >= 0) & (diff < WINDOW), 0.0, MASKED))
    o_win = jnp.dot(_softmax_rows(s).astype(bf16), vw_ref[0, 0, pl.ds(w0, wl), :], preferred_element_type=f32)

    gates = jax.nn.sigmoid(zg_ref[0, 0])
    for g in range(NSA_GROUP):
        r = slice(g * tq, (g + 1) * tq)
        o_ref[0, :, g * NSA_HD:(g + 1) * NSA_HD] = (gates[:, 3 * g:3 * g + 1] * o_cmp[r]
                                                    + gates[:, 3 * g + 1:3 * g + 2] * o_sel[r]
                                                    + gates[:, 3 * g + 2:3 * g + 3] * o_win[r])


def nsa_prompt_attention(qs, kc, vc, k_slc, v_slc, k_win, v_win, zg):
    B, S = qs.shape[:2]
    bf16 = jnp.bfloat16
    assert S % NSA_CK == 0 and S % NSA_TQ == 0 and WINDOW % NSA_TQ == 0 and WINDOW + NSA_TQ <= S
    n_sel = S // SEL_BLOCK
    nb = kc.shape[1]
    assert nb <= 128

    def pad_blocks(a):
        return jnp.pad(a.transpose(0, 2, 1, 3).astype(bf16), ((0, 0), (0, 0), (0, 128 - nb), (0, 0)))

    zg4 = zg.reshape(B, S, NSA_KV_HEADS, 3 * NSA_GROUP).transpose(0, 2, 1, 3)
    c = np.arange(128)[:, None]
    jj = np.arange(n_sel)[None, :]
    ratio = SEL_BLOCK // CMP_STRIDE
    msel = ((c >= jj * ratio - (CMP_BLOCK // CMP_STRIDE - 1)) & (c <= jj * ratio + ratio - 1) & (c < nb))
    assert n_sel <= LANES and NSA_TQ == LANES
    expand = (np.arange(S)[None, :] // SEL_BLOCK == np.arange(LANES)[:, None])
    expand = expand.reshape(LANES, S // NSA_CK, NSA_CK).transpose(1, 0, 2)
    row_spec = pl.BlockSpec((1, 1, S, NSA_HD), lambda b, n, i: (b, n, 0, 0))
    blk_spec = pl.BlockSpec((1, 1, 128, NSA_HD), lambda b, n, i: (b, n, 0, 0))
    return pl.pallas_call(
        _nsa_prompt_kernel,
        out_shape=jax.ShapeDtypeStruct((B, S, NSA_QW), jnp.float32),
        grid=(B, NSA_KV_HEADS, S // NSA_TQ),
        in_specs=[pl.BlockSpec((1, NSA_TQ, NSA_GROUP * NSA_HD), lambda b, n, i: (b, i, n)),
                  blk_spec, blk_spec, row_spec, row_spec, row_spec, row_spec,
                  pl.BlockSpec((1, 1, NSA_TQ, 3 * NSA_GROUP), lambda b, n, i: (b, n, i, 0)),
                  pl.BlockSpec((n_sel, 128), lambda b, n, i: (0, 0)),
                  pl.BlockSpec((S // NSA_CK, LANES, NSA_CK), lambda b, n, i: (0, 0, 0))],
        out_specs=pl.BlockSpec((1, NSA_TQ, NSA_GROUP * NSA_HD), lambda b, n, i: (b, i, n)),
        compiler_params=pltpu.CompilerParams(dimension_semantics=("arbitrary", "arbitrary", "arbitrary"),
                                             vmem_limit_bytes=VMEM_LIMIT_BYTES),
        name="nsa_prompt_attention",
    )(qs, pad_blocks(kc), pad_blocks(vc), k_slc, v_slc, k_win, v_win,
      zg4, jnp.asarray(msel.T, bf16), jnp.asarray(expand, bf16))


MLSTM_L = 128
CONV_HALO = 8


def _log_sigmoid(x):
    return -(jnp.maximum(-x, 0.0) + jnp.log1p(jnp.exp(-jnp.abs(x))))


def _mlstm_prompt_kernel(x_ref, xprev_ref, halo0_ref, v_ref, o_ref, gcol_ref, grow_ref, cw_ref, cb_ref,
                         out_ref, c_out, n_out, m_out, c_ref, n_ref, m_ref):
    f32, bf16 = jnp.float32, jnp.bfloat16
    c = pl.program_id(1)
    L = MLSTM_L

    @pl.when(c == 0)
    def _():
        c_ref[...] = jnp.zeros_like(c_ref)
        n_ref[...] = jnp.zeros_like(n_ref)
        m_ref[...] = jnp.zeros_like(m_ref)

    x = x_ref[0]
    halo = jnp.where(c == 0, halo0_ref[0], xprev_ref[0, L - CONV_HALO:L, :])
    ext = jnp.concatenate([halo, x], axis=0)
    conv = cb_ref[...]
    for j in range(CONV_W):
        o = CONV_HALO - (CONV_W - 1) + j
        conv = conv + ext[o:o + L] * cw_ref[j:j + 1, :]
    qk = conv * jax.nn.sigmoid(conv)

    t_id = lax.broadcasted_iota(jnp.int32, (L, L), 0)
    s_id = lax.broadcasted_iota(jnp.int32, (L, L), 1)
    causal = t_id >= s_id
    gcol = gcol_ref[0, 0]
    grow = grow_ref[0, 0]
    for h in range(M_HEADS):
        hd = slice(h * M_HD, (h + 1) * M_HD)
        q = qk[:, hd]
        k = qk[:, M_W + h * M_HD:M_W + (h + 1) * M_HD] * (M_HD ** -0.5)
        v = v_ref[0, :, hd].astype(bf16)
        ig_r = grow[h:h + 1, :]
        ig_c = gcol[:, h:h + 1]
        lf_r = _log_sigmoid(grow[M_HEADS + h:M_HEADS + h + 1, :])
        lf_c = _log_sigmoid(gcol[:, M_HEADS + h:M_HEADS + h + 1])
        b_c = jnp.sum(jnp.where(causal, lf_r, 0.0), axis=1, keepdims=True)
        b_r = jnp.sum(jnp.where(t_id <= s_id, lf_c, 0.0), axis=0, keepdims=True)
        m_prev = m_ref[h]
        dmat = jnp.where(causal, b_c - b_r + ig_r, -jnp.inf)
        inter = b_c + m_prev
        m_t = jnp.maximum(inter, jnp.max(dmat, axis=1, keepdims=True))
        w_intra = jnp.exp(dmat - m_t)
        w_inter = jnp.exp(inter - m_t)
        qb = q.astype(bf16)
        s = lax.dot_general(qb, k.astype(bf16), _NT, preferred_element_type=f32) * w_intra
        num = (jnp.dot(s.astype(bf16), v, preferred_element_type=f32)
               + w_inter * jnp.dot(qb, c_ref[h].astype(bf16), preferred_element_type=f32))
        den = jnp.sum(s, axis=1, keepdims=True) + w_inter * jnp.sum(q * n_ref[h], axis=1, keepdims=True)
        hh = num / jnp.maximum(jnp.abs(den), jnp.exp(-m_t))
        out_ref[0, :, hd] = jax.nn.sigmoid(o_ref[0, :, hd]) * hh
        m_new = m_t[L - 1:L]
        b_last = b_c[L - 1:L]
        w_s = jnp.exp(b_last - b_c + ig_c - m_new)
        w_p = jnp.exp(b_last + m_prev - m_new)
        kw = k * w_s
        c_ref[h] = w_p * c_ref[h] + jnp.dot(kw.T.astype(bf16), v, preferred_element_type=f32)
        n_ref[h] = w_p * n_ref[h] + jnp.sum(kw, axis=0, keepdims=True)
        m_ref[h] = m_new

    @pl.when(c == pl.num_programs(1) - 1)
    def _():
        c_out[0] = c_ref[...]
        n_out[0] = n_ref[...]
        m_out[0] = m_ref[...]


def mlstm_prompt(zqk, zv, zo, zif, conv_w, conv_b, b_if):
    B, T, _ = zqk.shape
    L = MLSTM_L
    assert T % L == 0
    nc = T // L
    f32 = jnp.float32
    gif = zif + b_if
    gcol = gif.reshape(B, nc, L, 2 * M_HEADS)
    grow = gcol.transpose(0, 1, 3, 2)
    halo0 = jnp.zeros((B, CONV_HALO, 2 * M_W), f32)
    out, C, n, m = pl.pallas_call(
        _mlstm_prompt_kernel,
        out_shape=(jax.ShapeDtypeStruct((B, T, M_W), f32),
                   jax.ShapeDtypeStruct((B, M_HEADS, M_HD, M_HD), f32),
                   jax.ShapeDtypeStruct((B, M_HEADS, 1, M_HD), f32),
                   jax.ShapeDtypeStruct((B, M_HEADS, 1, 1), f32)),
        grid=(B, nc),
        in_specs=[pl.BlockSpec((1, L, 2 * M_W), lambda b, c: (b, c, 0)),
                  pl.BlockSpec((1, L, 2 * M_W), lambda b, c: (b, jnp.maximum(c - 1, 0), 0)),
                  pl.BlockSpec((1, CONV_HALO, 2 * M_W), lambda b, c: (b, 0, 0)),
                  pl.BlockSpec((1, L, M_W), lambda b, c: (b, c, 0)),
                  pl.BlockSpec((1, L, M_W), lambda b, c: (b, c, 0)),
                  pl.BlockSpec((1, 1, L, 2 * M_HEADS), lambda b, c: (b, c, 0, 0)),
                  pl.BlockSpec((1, 1, 2 * M_HEADS, L), lambda b, c: (b, c, 0, 0)),
                  pl.BlockSpec((CONV_W, 2 * M_W), lambda b, c: (0, 0)),
                  pl.BlockSpec((1, 2 * M_W), lambda b, c: (0, 0))],
        out_specs=(pl.BlockSpec((1, L, M_W), lambda b, c: (b, c, 0)),
                   pl.BlockSpec((1, M_HEADS, M_HD, M_HD), lambda b, c: (b, 0, 0, 0)),
                   pl.BlockSpec((1, M_HEADS, 1, M_HD), lambda b, c: (b, 0, 0, 0)),
                   pl.BlockSpec((1, M_HEADS, 1, 1), lambda b, c: (b, 0, 0, 0))),
        scratch_shapes=[pltpu.VMEM((M_HEADS, M_HD, M_HD), f32), pltpu.VMEM((M_HEADS, 1, M_HD), f32),
                        pltpu.VMEM((M_HEADS, 1, 1), f32)],
        compiler_params=pltpu.CompilerParams(dimension_semantics=("arbitrary", "arbitrary"),
                                             vmem_limit_bytes=VMEM_LIMIT_BYTES),
        name="mlstm_prompt",
    )(zqk, zqk, halo0, zv, zo, gcol, grow, conv_w, conv_b[None])
    return out, C, n.reshape(B, M_HEADS, M_HD), m.reshape(B, M_HEADS)


def mlstm_chunk(carry, inp):
    C, n, m = carry
    q, k, v, ig, lf = inp
    L = q.shape[2]
    b = jnp.cumsum(lf, axis=-1)
    causal = jnp.tril(jnp.ones((L, L), dtype=bool))
    dmat = jnp.where(causal, b[..., :, None] - b[..., None, :] + ig[..., None, :], -jnp.inf)
    inter = b + m[..., None]
    m_t = jnp.maximum(inter, dmat.max(axis=-1))
    w_intra = jnp.exp(dmat - m_t[..., None])
    w_inter = jnp.exp(inter - m_t)
    s = jnp.einsum('bhtd,bhsd->bhts', q, k) * w_intra
    num = jnp.einsum('bhts,bhsv->bhtv', s, v) + w_inter[..., None] * jnp.einsum('bhtd,bhdv->bhtv', q, C)
    den = s.sum(-1) + w_inter * jnp.einsum('bhtd,bhd->bht', q, n)
    h = num / jnp.maximum(jnp.abs(den), jnp.exp(-m_t))[..., None]
    m_new = m_t[..., -1]
    w_s = jnp.exp(b[..., -1:] - b + ig - m_new[..., None])
    w_p = jnp.exp(b[..., -1] + m - m_new)
    C_new = w_p[..., None, None] * C + jnp.einsum('bhs,bhsd,bhsv->bhdv', w_s, k, v)
    n_new = w_p[..., None] * n + jnp.einsum('bhs,bhsd->bhd', w_s, k)
    return (C_new, n_new, m_new), h


def mlstm_mix(zqk, zv, zo, zif, buf0, C0, n0, m0, conv_w, conv_b, b_if, chunk):
    B, T, _ = zqk.shape
    full = jnp.concatenate([buf0, zqk], axis=1)
    conv = conv_b
    for j in range(CONV_W):
        conv = conv + full[:, j:j + T] * conv_w[j]
    qk = jax.nn.silu(conv)

    def heads(a):
        return a.reshape(B, T, M_HEADS, M_HD).transpose(0, 2, 1, 3)

    q = heads(qk[..., :M_W])
    k = heads(qk[..., M_W:]) * (M_HD ** -0.5)
    v = heads(zv)
    gif = zif + b_if
    ig = gif[..., :M_HEADS].transpose(0, 2, 1)
    lf = jax.nn.log_sigmoid(gif[..., M_HEADS:]).transpose(0, 2, 1)
    nc = T // chunk

    def to_chunks(a):
        return jnp.moveaxis(a.reshape(B, M_HEADS, nc, chunk, *a.shape[3:]), 2, 0)

    (C, n, m), h = lax.scan(mlstm_chunk, (C0, n0, m0),
                            (to_chunks(q), to_chunks(k), to_chunks(v), to_chunks(ig), to_chunks(lf)))
    h = jnp.moveaxis(h, 0, 2).reshape(B, M_HEADS, T, M_HD).transpose(0, 2, 1, 3).reshape(B, T, M_W)
    out = jax.nn.sigmoid(zo) * h
    return out, (C, n, m, full[:, T:])


PEER_COMBOS = 2 * PEER_HEADS
PEER_KEY_ROWS = 8
PEER_TILE = PEER_KEY_ROWS * N_KEYS
PEER_TS_ROWS = 24
LANES = 128
_NT = (((1,), (1,)), ((), ()))


def _peer_topk_kernel(q_ref, keys_ref, s_ref, e0_ref, e1_ref, tau_ref, ts_ref):
    head = pl.program_id(1)
    tt = q_ref.shape[0]
    half_w = PEER_QDIM // 2
    for half in range(2):
        c = 2 * head + half
        qh = q_ref[:, half * half_w:(half + 1) * half_w].astype(jnp.bfloat16)
        s = lax.dot_general(keys_ref[half], qh, _NT, preferred_element_type=jnp.float32)
        s_ref[c] = s
        pad_rows = jnp.full((PEER_TS_ROWS - PEER_TOPK - 1, tt), -jnp.inf, jnp.float32)
        work = s
        rows = []
        for _ in range(PEER_TOPK + 1):
            m = jnp.max(work, axis=0, keepdims=True)
            work = jnp.where(work == m, -jnp.inf, work)
            rows.append(m)
        ts_ref[c] = jnp.concatenate(rows + [pad_rows], axis=0)
        removed = jnp.sum(jnp.where(work == -jnp.inf, 1.0, 0.0), axis=0, keepdims=True)
        has_ties = jnp.max(removed) > PEER_TOPK + 1

        @pl.when(has_ties)
        def _(s=s, c=c):
            key_id = lax.broadcasted_iota(jnp.int32, s.shape, 0)
            work = s
            rows = []
            for _ in range(PEER_TOPK + 1):
                m = jnp.max(work, axis=0, keepdims=True)
                first = jnp.min(jnp.where(work == m, key_id, N_KEYS), axis=0, keepdims=True)
                work = jnp.where(key_id == first, -jnp.inf, work)
                rows.append(m)
            ts_ref[c] = jnp.concatenate(rows + [pad_rows], axis=0)

    @pl.when(head == PEER_HEADS - 1)
    def _():
        for h in range(PEER_HEADS):
            t0 = ts_ref[2 * h]
            t1 = ts_ref[2 * h + 1]
            pieces = [t0[0:1] + t1] + [t0[a:a + 1] + t1[0:8] for a in range(1, 8)] + [t0[8:24] + t1[0:1]]
            cand = jnp.concatenate(pieces, axis=0)
            top = t0[0:1] + t1[0:1]
            n_pad = 2 * (PEER_TS_ROWS - PEER_TOPK - 1)
            work = cand
            vals = []
            for _ in range(PEER_TOPK + 1):
                m = jnp.max(work, axis=0, keepdims=True)
                work = jnp.where(work == m, -jnp.inf, work)
                vals.append(m)
            z = jnp.exp(vals[0] - top)
            for r in range(1, PEER_TOPK):
                z = z + jnp.exp(vals[r] - top)
            tau_ref[h:h + 1, :] = 0.5 * vals[PEER_TOPK - 1] + 0.5 * vals[PEER_TOPK]
            e0_ref[h] = jnp.exp(s_ref[2 * h] - t0[0:1]) / z
            e1_ref[h] = jnp.exp(s_ref[2 * h + 1] - t1[0:1])
            removed = jnp.sum(jnp.where(work == -jnp.inf, 1.0, 0.0), axis=0, keepdims=True)

            @pl.when(jnp.max(removed) > PEER_TOPK + 1 + n_pad)
            def _(cand=cand, top=top, t0=t0, h=h):
                v16 = top
                v17 = top
                z = jnp.zeros_like(top)
                seen = jnp.zeros_like(top)
                for _ in range(PEER_TOPK + 1):
                    m = jnp.max(cand, axis=0, keepdims=True)
                    eq = cand == m
                    cnt = jnp.sum(jnp.where(eq, 1.0, 0.0), axis=0, keepdims=True)
                    active = seen < PEER_TOPK
                    take = jnp.minimum(cnt, PEER_TOPK - seen)
                    v16 = jnp.where(active, m, v16)
                    v17 = jnp.where(seen < PEER_TOPK + 1, m, v17)
                    z = z + jnp.where(active, take * jnp.exp(m - top), 0.0)
                    seen = seen + cnt
                    cand = jnp.where(eq, -jnp.inf, cand)
                tau_ref[h:h + 1, :] = 0.5 * v16 + 0.5 * v17
                e0_ref[h] = jnp.exp(s_ref[2 * h] - t0[0:1]) / z


def peer_scores(q, sub_keys, tt):
    n = q.shape[0]
    assert n % tt == 0
    keys = sub_keys.reshape(PEER_COMBOS, N_KEYS, PEER_QDIM // 2).astype(jnp.bfloat16)
    f32 = jnp.float32
    return pl.pallas_call(
        _peer_topk_kernel,
        out_shape=(jax.ShapeDtypeStruct((PEER_COMBOS, N_KEYS, n), f32),
                   jax.ShapeDtypeStruct((PEER_HEADS, N_KEYS, n), f32),
                   jax.ShapeDtypeStruct((PEER_HEADS, N_KEYS, n), f32),
                   jax.ShapeDtypeStruct((PEER_HEADS, n), f32)),
        grid=(n // tt, PEER_HEADS),
        in_specs=[pl.BlockSpec((tt, PEER_QDIM), lambda i, c: (i, c)),
                  pl.BlockSpec((2, N_KEYS, PEER_QDIM // 2), lambda i, c: (c, 0, 0))],
        out_specs=(pl.BlockSpec((PEER_COMBOS, N_KEYS, tt), lambda i, c: (0, 0, i)),
                   pl.BlockSpec((PEER_HEADS, N_KEYS, tt), lambda i, c: (0, 0, i)),
                   pl.BlockSpec((PEER_HEADS, N_KEYS, tt), lambda i, c: (0, 0, i)),
                   pl.BlockSpec((PEER_HEADS, tt), lambda i, c: (0, i))),
        scratch_shapes=[pltpu.VMEM((PEER_COMBOS, PEER_TS_ROWS, tt), f32)],
        compiler_params=pltpu.CompilerParams(dimension_semantics=("arbitrary", "arbitrary"),
                                             vmem_limit_bytes=VMEM_LIMIT_BYTES),
        name="peer_topk",
    )(q, keys)


def _peer_dense_kernel(xt_ref, h_ref, u_ref, vt_ref, s0_ref, ez_ref, s_ref, e1_ref, tau_ref, g_ref, b_ref,
                       o_ref, acc_ref, a_ref, w_ref):
    e = pl.program_id(1)
    tt = xt_ref.shape[1]

    @pl.when(e == 0)
    def _():
        acc_ref[...] = jnp.zeros_like(acc_ref)

    a_ref[...] = jnp.dot(u_ref[...], xt_ref[...], preferred_element_type=jnp.float32)
    for r in range(PEER_KEY_ROWS):
        rows = slice(r * N_KEYS, (r + 1) * N_KEYS)
        for t in range(tt // LANES):
            tok = slice(t * LANES, (t + 1) * LANES)
            gate = jnp.zeros((N_KEYS, LANES), jnp.float32)
            for h in range(PEER_HEADS):
                need = tau_ref[h:h + 1, tok] - s0_ref[2 * h, r:r + 1, tok]
                picked = jnp.where(s_ref[2 * h + 1, :, tok] >= need, e1_ref[h, :, tok], 0.0)
                gate = gate + picked * ez_ref[h, r:r + 1, tok]
            ar = a_ref[rows, tok]
            act = 0.5 * ar * (1.0 + lax.erf(ar * (2.0 ** -0.5)))
            w_ref[rows, tok] = (gate * act).astype(jnp.bfloat16)
    acc_ref[...] += jnp.dot(vt_ref[0], w_ref[...], preferred_element_type=jnp.float32)

    @pl.when(e == pl.num_programs(1) - 1)
    def _():
        r = ALPHA * h_ref[...] + acc_ref[...].T
        mu = jnp.mean(r, axis=-1, keepdims=True)
        d = r - mu
        var = jnp.mean(d * d, axis=-1, keepdims=True)
        o_ref[...] = d * lax.rsqrt(var + LN_EPS) * g_ref[...] + b_ref[...]


def peer_tail(h, ht, q, sub_keys, u_bf, vt_bf, ln_g, ln_b, tt):
    n, d = h.shape
    s, e0z, e1, tau = peer_scores(q, sub_keys, tt)
    n_exp = u_bf.shape[0]
    return pl.pallas_call(
        _peer_dense_kernel,
        out_shape=jax.ShapeDtypeStruct((n, d), jnp.float32),
        grid=(n // tt, n_exp // PEER_TILE),
        in_specs=[pl.BlockSpec((d, tt), lambda i, e: (0, i)),
                  pl.BlockSpec((tt, d), lambda i, e: (i, 0)),
                  pl.BlockSpec((PEER_TILE, d), lambda i, e: (e, 0)),
                  pl.BlockSpec((1, d, PEER_TILE), lambda i, e: (e, 0, 0)),
                  pl.BlockSpec((PEER_COMBOS, PEER_KEY_ROWS, tt), lambda i, e: (0, e, i)),
                  pl.BlockSpec((PEER_HEADS, PEER_KEY_ROWS, tt), lambda i, e: (0, e, i)),
                  pl.BlockSpec((PEER_COMBOS, N_KEYS, tt), lambda i, e: (0, 0, i)),
                  pl.BlockSpec((PEER_HEADS, N_KEYS, tt), lambda i, e: (0, 0, i)),
                  pl.BlockSpec((PEER_HEADS, tt), lambda i, e: (0, i)),
                  pl.BlockSpec((1, d), lambda i, e: (0, 0)),
                  pl.BlockSpec((1, d), lambda i, e: (0, 0))],
        out_specs=pl.BlockSpec((tt, d), lambda i, e: (i, 0)),
        scratch_shapes=[pltpu.VMEM((d, tt), jnp.float32), pltpu.VMEM((PEER_TILE, tt), jnp.float32),
                        pltpu.VMEM((PEER_TILE, tt), jnp.bfloat16)],
        compiler_params=pltpu.CompilerParams(dimension_semantics=("arbitrary", "arbitrary"),
                                             vmem_limit_bytes=VMEM_LIMIT_BYTES),
        name="peer_dense",
    )(ht, h, u_bf, vt_bf, s, e0z, s, e1, tau, ln_g[None], ln_b[None])


def _out_proj_kernel(x_ref, nsa_ref, m_ref, wn_ref, wm_ref, g_ref, b_ref, wq_ref, h_ref, ht_ref, q_ref):
    f32, bf16 = jnp.float32, jnp.bfloat16
    r = (ALPHA * x_ref[...] + jnp.dot(nsa_ref[...].astype(bf16), wn_ref[...], preferred_element_type=f32)
         + jnp.dot(m_ref[...].astype(bf16), wm_ref[...], preferred_element_type=f32))
    mu = jnp.mean(r, axis=-1, keepdims=True)
    d = r - mu
    var = jnp.mean(d * d, axis=-1, keepdims=True)
    h = d * lax.rsqrt(var + LN_EPS) * g_ref[...] + b_ref[...]
    h_ref[...] = h
    ht_ref[...] = h.T.astype(bf16)
    q_ref[...] = jnp.dot(h.astype(bf16), wq_ref[...], preferred_element_type=f32)


def out_proj_fused(x, o_nsa, o_m, w_out, ln_g, ln_b, w_pq, tm):
    n, d = x.shape
    assert n % tm == 0
    bf16 = jnp.bfloat16
    nq = w_pq.shape[1]

    def rows(width):
        return pl.BlockSpec((tm, width), lambda i: (i, 0))

    def whole(a):
        return pl.BlockSpec(a.shape, lambda i: (0, 0))

    wn = w_out[:NSA_QW].astype(bf16)
    wm = w_out[NSA_QW:].astype(bf16)
    wq = w_pq.astype(bf16)
    g, b = ln_g[None], ln_b[None]
    return pl.pallas_call(
        _out_proj_kernel,
        out_shape=(jax.ShapeDtypeStruct((n, d), jnp.float32), jax.ShapeDtypeStruct((d, n), bf16),
                   jax.ShapeDtypeStruct((n, nq), jnp.float32)),
        grid=(n // tm,),
        in_specs=[rows(d), rows(NSA_QW), rows(M_W), whole(wn), whole(wm), whole(g), whole(b), whole(wq)],
        out_specs=(rows(d), pl.BlockSpec((d, tm), lambda i: (0, i)), rows(nq)),
        compiler_params=pltpu.CompilerParams(dimension_semantics=("arbitrary",),
                                             vmem_limit_bytes=VMEM_LIMIT_BYTES),
        name="out_proj",
    )(x, o_nsa, o_m, wn, wm, g, b, wq)


def block_tail(x, o_nsa, o_m, w_out, ln_g, ln_b, w_pq, sub_keys, u_bf, vt_bf, tt):
    lead = x.shape[:-1]
    h, ht, q = out_proj_fused(x.reshape(-1, D_MODEL), o_nsa.reshape(-1, NSA_QW), o_m.reshape(-1, M_W),
                              w_out, ln_g[0], ln_b[0], w_pq, tt)
    return peer_tail(h, ht, q, sub_keys, u_bf, vt_bf, ln_g[1], ln_b[1], tt).reshape(*lead, D_MODEL)


def prompt_mix(x, w_in, pe, w1, b1, w2, conv_w, conv_b, b_if):
    B, S, _ = x.shape
    z = in_proj_fused(x, w_in, jnp.arange(S), 512, True)
    kc = compress_chunks(z["k_cmp"], S // CMP_STRIDE, pe[0], w1[0], b1[0], w2[0])
    vc = compress_chunks(z["v_cmp"], S // CMP_STRIDE, pe[1], w1[1], b1[1], w2[1])
    zgate = z["zgate"].reshape(B, S, GATE_W)
    o_nsa = nsa_prompt_attention(z["q"].reshape(B, S, NSA_QW), kc, vc, z["k_slc_bf"], z["v_slc_bf"],
                                 z["k_win_bf"], z["v_win_bf"], zgate[..., :IN_SPLITS[2]])
    zqk = z["zqk"].reshape(B, S, 2 * M_W)
    o_m, C, n, m = mlstm_prompt(zqk, z["zv"].reshape(B, S, M_W), z["zo"].reshape(B, S, M_W),
                                zgate[..., IN_SPLITS[2]:], conv_w, conv_b, b_if)
    buf = zqk[:, S - (CONV_W - 1):]
    wl = min(WINDOW, S)
    k_cmp, v_cmp, k_slc, v_slc, k_win, v_win = [
        z[k + "_cache"].reshape(B, NSA_KV_HEADS, NSA_HD, S).transpose(0, 3, 1, 2)
        for k in ("k_cmp", "v_cmp", "k_slc", "v_slc", "k_win", "v_win")]
    return (o_nsa, o_m), (k_cmp, v_cmp, k_slc, v_slc, k_win[:, S - wl:], v_win[:, S - wl:], C, n, m, buf)


def sample_mix(x, kc_pool, vc_pool, ks_pool, vs_pool, kw_buf, vw_buf, C0, n0, m0, buf0, page_table,
               w_in, pe, w1, b1, w2, conv_w, conv_b, b_if):
    B, T, _ = x.shape
    past = page_table.shape[1] * PAGE_SIZE
    pos = past + jnp.arange(T)
    z = in_proj_fused(x, w_in, pos, B * T, False)
    q = z["q"].astype(jnp.float32).reshape(B, T, NSA_HEADS, NSA_HD) * (1.0 / ATTN_SCALE)
    k_cmp, v_cmp, k_slc, v_slc, k_win, v_win = [
        z[k].reshape(B, T, NSA_KV_HEADS, NSA_HD) for k in ("k_cmp", "v_cmp", "k_slc", "v_slc", "k_win", "v_win")]
    zgate = z["zgate"].reshape(B, T, GATE_W)
    gates = jax.nn.sigmoid(zgate[..., :IN_SPLITS[2]]).reshape(B, T, NSA_HEADS, 3)
    zqk, zv, zo, zif = (z["zqk"].reshape(B, T, 2 * M_W), z["zv"].reshape(B, T, M_W), z["zo"].reshape(B, T, M_W),
                        zgate[..., IN_SPLITS[2]:])

    assert (past + T) // CMP_STRIDE == past // CMP_STRIDE

    def compressed(pool, c):
        pages = pool.transpose(0, 2, 3, 1)[page_table]
        return compress_pages(pages, pe[c], w1[c], b1[c], w2[c])

    o_cmp, p = cmp_attend(q, pos, compressed(kc_pool, 0), compressed(vc_pool, 1))
    n_sel = -(-(past + T) // SEL_BLOCK)
    member = select_blocks(p, pos, n_sel)
    o_sel = sample_selected_attention(q, pos, member, ks_pool, vs_pool, k_slc, v_slc, page_table)
    wb = kw_buf.shape[1]
    kw = jnp.concatenate([kw_buf, k_win], axis=1)
    vw = jnp.concatenate([vw_buf, v_win], axis=1)
    kpos = past - wb + jnp.arange(wb + T)
    o_win = win_attend(q, pos, kw, vw, kpos)
    o_nsa = nsa_combine(gates, o_cmp, o_sel, o_win)
    o_m, (C, n, m, buf) = mlstm_mix(zqk, zv, zo, zif, buf0, C0, n0, m0, conv_w, conv_b, b_if, T)
    return (o_nsa, o_m), (k_cmp, v_cmp, k_slc, v_slc, kw[:, T:], vw[:, T:], C, n, m, buf)


def kernel(x_prompt, x_sample, cache_k_cmp, cache_v_cmp, cache_k_slc, cache_v_slc, cache_k_win, cache_v_win,
           state_C, state_n, state_m, state_conv, page_table, w_in, w_out, w_phi1, b_phi1, w_phi2, pe_cmp,
           conv_w, conv_b, b_if, ln_g, ln_b, w_pq, sub_keys, u_tab, v_tab):
    l = 0
    mix_p, st_p = prompt_mix(x_prompt, w_in[l], pe_cmp[l], w_phi1[l], b_phi1[l], w_phi2[l],
                             conv_w[l], conv_b[l], b_if[l])
    mix_s, st_s = sample_mix(x_sample, cache_k_cmp[l], cache_v_cmp[l], cache_k_slc[l], cache_v_slc[l],
                             cache_k_win[l], cache_v_win[l], state_C[l], state_n[l], state_m[l],
                             state_conv[l], page_table, w_in[l], pe_cmp[l], w_phi1[l], b_phi1[l],
                             w_phi2[l], conv_w[l], conv_b[l], b_if[l])
    u_bf = u_tab[l].astype(jnp.bfloat16)
    vt_bf = v_tab[l].astype(jnp.bfloat16).reshape(-1, PEER_TILE, D_MODEL).transpose(0, 2, 1)
    xp = block_tail(x_prompt, *mix_p, w_out[l], ln_g[l], ln_b[l], w_pq[l], sub_keys[l], u_bf, vt_bf, 512)
    xs = block_tail(x_sample, *mix_s, w_out[l], ln_g[l], ln_b[l], w_pq[l], sub_keys[l], u_bf, vt_bf, 128)
    return (xp, xs) + tuple(a[None] for a in st_p) + tuple(a[None] for a in st_s)
```

```python
import functools

import jax
import jax.numpy as jnp
import numpy as np
from jax import lax
from jax.experimental import pallas as pl
from jax.experimental.pallas import tpu as pltpu

D_MODEL = 1024
DEPTH = 1
PAGE_SIZE = 128
NSA_HEADS = 8
NSA_KV_HEADS = 2
NSA_GROUP = NSA_HEADS // NSA_KV_HEADS
NSA_HD = 64
NSA_QW = NSA_HEADS * NSA_HD
NSA_KVW = NSA_KV_HEADS * NSA_HD
CMP_BLOCK = 32
CMP_STRIDE = 16
SEL_BLOCK = 64
SEL_TOP = 16
WINDOW = 512
Q_BLOCK = 64
ATTN_SCALE = NSA_HD ** -0.5
ROPE_THETA = 10000.0
M_HEADS = 4
M_HD = 128
M_W = M_HEADS * M_HD
M_CHUNK = 64
CONV_W = 4
PEER_HEADS = 8
N_KEYS = 128
PEER_TOPK = 16
PEER_QDIM = 256
PEER_BLOCK = 128
IN_SPLITS = (NSA_QW, 6 * NSA_KVW, 3 * NSA_HEADS, 2 * M_W, M_W, M_W, 2 * M_HEADS)
LN_EPS = 1e-5
ALPHA = (2 * DEPTH) ** 0.25

VMEM_LIMIT_BYTES = 56 * 1024 * 1024


def _mm_kernel(x_ref, w_ref, o_ref):
    o_ref[...] = jnp.dot(x_ref[...].astype(jnp.bfloat16), w_ref[...], preferred_element_type=jnp.float32)


def pallas_matmul(x, w, tm=512):
    M, K = x.shape
    N = w.shape[1]
    tm = min(tm, M)
    assert M % tm == 0
    return pl.pallas_call(
        _mm_kernel,
        out_shape=jax.ShapeDtypeStruct((M, N), jnp.float32),
        grid=(M // tm,),
        in_specs=[pl.BlockSpec((tm, K), lambda i: (i, 0)), pl.BlockSpec((K, N), lambda i: (0, 0))],
        out_specs=pl.BlockSpec((tm, N), lambda i: (i, 0)),
        compiler_params=pltpu.CompilerParams(dimension_semantics=("arbitrary",),
                                             vmem_limit_bytes=VMEM_LIMIT_BYTES),
        name="proj_matmul",
    )(x, w.astype(jnp.bfloat16))


def mm3(x, w):
    lead = x.shape[:-1]
    return pallas_matmul(x.reshape(-1, x.shape[-1]), w).reshape(*lead, w.shape[1])


def layer_norm(x, g, b):
    mu = x.mean(-1, keepdims=True)
    var = jnp.square(x - mu).mean(-1, keepdims=True)
    return (x - mu) * lax.rsqrt(var + LN_EPS) * g + b


def rope(x, pos):
    half = x.shape[-1] // 2
    inv = ROPE_THETA ** (-jnp.arange(half, dtype=jnp.float32) / half)
    ang = pos.astype(jnp.float32)[:, None] * inv[None, :]
    cos = jnp.cos(ang)[:, None, :]
    sin = jnp.sin(ang)[:, None, :]
    x1, x2 = x[..., :half], x[..., half:]
    return jnp.concatenate([x1 * cos - x2 * sin, x2 * cos + x1 * sin], axis=-1)


def split_in_proj(x, w_in):
    z = mm3(x, w_in)
    cuts = [int(c) for c in np.cumsum(IN_SPLITS)[:-1]]
    return jnp.split(z, cuts, axis=-1)


_IN_OFF = np.concatenate([[0], np.cumsum(IN_SPLITS)])
_IN_ORDER = (0, 1, 3, 4, 5, 2, 6)
_N_KV_ROWS = 6
_KV_BF16 = (2, 3, 4, 5)
_KV_CACHE = (0, 1, 2, 3, 4, 5)
GATE_W = IN_SPLITS[2] + IN_SPLITS[6]


def _rope_pairs(x, cos, sin_signed):
    half = NSA_HD // 2
    lane = lax.broadcasted_iota(jnp.int32, x.shape, 1)
    partner = jnp.where(lane % NSA_HD < half, pltpu.roll(x, LANES - half, 1), pltpu.roll(x, half, 1))
    return x * cos + partner * sin_signed


def _in_proj_kernel(x_ref, w_ref, cos_ref, sin_ref, q_ref, *rest, kv_major):
    kv_refs = rest[:_N_KV_ROWS]
    rest = rest[_N_KV_ROWS:]
    if kv_major:
        bf_refs, rest = rest[:len(_KV_BF16)], rest[len(_KV_BF16):]
        cache_refs, rest = rest[:len(_KV_CACHE)], rest[len(_KV_CACHE):]
    zqk_ref, zv_ref, zo_ref, zgate_ref = rest
    z = jnp.dot(x_ref[...].astype(jnp.bfloat16), w_ref[...], preferred_element_type=jnp.float32)
    cos = cos_ref[...]
    sin = sin_ref[...]
    for g in range(NSA_QW // LANES):
        sl = slice(g * LANES, (g + 1) * LANES)
        q_ref[:, sl] = (_rope_pairs(z[:, sl], cos, sin) * ATTN_SCALE).astype(jnp.bfloat16)
    for r in range(_N_KV_ROWS):
        row = z[:, NSA_QW + r * NSA_KVW:NSA_QW + (r + 1) * NSA_KVW]
        if r % 2 == 0:
            row = _rope_pairs(row, cos, sin)
        kv_refs[r][...] = row
        if kv_major and r in _KV_CACHE:
            dst = cache_refs[_KV_CACHE.index(r)]
            dst[0] = row.T
        if kv_major and r in _KV_BF16:
            dst = bf_refs[_KV_BF16.index(r)]
            for n in range(NSA_KV_HEADS):
                dst[0, n] = row[:, n * NSA_HD:(n + 1) * NSA_HD].astype(jnp.bfloat16)
    o = NSA_QW + _N_KV_ROWS * NSA_KVW
    zqk_ref[...] = z[:, o:o + 2 * M_W]
    zv_ref[...] = z[:, o + 2 * M_W:o + 3 * M_W]
    zo_ref[...] = z[:, o + 3 * M_W:o + 4 * M_W]
    zgate_ref[...] = z[:, o + 4 * M_W:o + 4 * M_W + GATE_W]


def in_proj_fused(x, w_in, pos, tm, kv_major):
    B, T, D = x.shape
    M = B * T
    assert M % tm == 0 and NSA_KVW == LANES and (not kv_major or T % tm == 0)
    f32, bf16 = jnp.float32, jnp.bfloat16
    w = jnp.concatenate([w_in[:, _IN_OFF[i]:_IN_OFF[i + 1]] for i in _IN_ORDER], axis=1).astype(bf16)
    half = NSA_HD // 2
    inv = ROPE_THETA ** (-jnp.arange(half, dtype=f32) / half)
    ang = pos.astype(f32)[:, None] * inv[None, :]
    cos = jnp.tile(jnp.cos(ang), (B, 2 * LANES // NSA_HD))
    sin = jnp.tile(jnp.concatenate([-jnp.sin(ang), jnp.sin(ang)], axis=1), (B, LANES // NSA_HD))
    n_w = w.shape[1]

    def rows(width):
        return pl.BlockSpec((tm, width), lambda i: (i, 0))

    out_shape = [jax.ShapeDtypeStruct((M, NSA_QW), bf16)] + [jax.ShapeDtypeStruct((M, NSA_KVW), f32)] * _N_KV_ROWS
    out_specs = [rows(NSA_QW)] + [rows(NSA_KVW)] * _N_KV_ROWS
    if kv_major:
        per_seq = T // tm
        out_shape += [jax.ShapeDtypeStruct((B, NSA_KV_HEADS, T, NSA_HD), bf16)] * len(_KV_BF16)
        out_specs += [pl.BlockSpec((1, NSA_KV_HEADS, tm, NSA_HD),
                                   lambda i: (i // per_seq, 0, i % per_seq, 0))] * len(_KV_BF16)
        out_shape += [jax.ShapeDtypeStruct((B, NSA_KVW, T), f32)] * len(_KV_CACHE)
        out_specs += [pl.BlockSpec((1, NSA_KVW, tm), lambda i: (i // per_seq, 0, i % per_seq))] * len(_KV_CACHE)
    out_shape += [jax.ShapeDtypeStruct((M, 2 * M_W), f32), jax.ShapeDtypeStruct((M, M_W), f32),
                  jax.ShapeDtypeStruct((M, M_W), f32), jax.ShapeDtypeStruct((M, GATE_W), f32)]
    out_specs += [rows(2 * M_W), rows(M_W), rows(M_W), rows(GATE_W)]
    outs = pl.pallas_call(
        functools.partial(_in_proj_kernel, kv_major=kv_major),
        out_shape=tuple(out_shape),
        grid=(M // tm,),
        in_specs=[rows(D), pl.BlockSpec((D, n_w), lambda i: (0, 0)), rows(LANES), rows(LANES)],
        out_specs=tuple(out_specs),
        compiler_params=pltpu.CompilerParams(dimension_semantics=("arbitrary",),
                                             vmem_limit_bytes=VMEM_LIMIT_BYTES),
        name="in_proj",
    )(x.reshape(M, D), w, cos, sin)
    names = ["q", "k_cmp", "v_cmp", "k_slc", "v_slc", "k_win", "v_win"]
    if kv_major:
        names += ["k_slc_bf", "v_slc_bf", "k_win_bf", "v_win_bf"]
        names += ["k_cmp_cache", "v_cmp_cache", "k_slc_cache", "v_slc_cache", "k_win_cache", "v_win_cache"]
    names += ["zqk", "zv", "zo", "zgate"]
    return dict(zip(names, outs))


def nsa_project(zq, zkv, zg, pos):
    B, T, _ = zq.shape
    q = rope(zq.reshape(B, T, NSA_HEADS, NSA_HD), pos)
    kv = zkv.reshape(B, T, 6, NSA_KV_HEADS, NSA_HD)
    rows = (rope(kv[:, :, 0], pos), kv[:, :, 1], rope(kv[:, :, 2], pos), kv[:, :, 3],
            rope(kv[:, :, 4], pos), kv[:, :, 5])
    gates = jax.nn.sigmoid(zg).reshape(B, T, NSA_HEADS, 3)
    return q, rows, gates


def _expanded_w1(w1):
    assert CMP_BLOCK == 2 * CMP_STRIDE
    w1r = w1.reshape(2, CMP_STRIDE, NSA_HD, w1.shape[-1])
    wbig = jnp.einsum('hpdf,kn->pkdnhf', w1r, jnp.eye(NSA_KV_HEADS, dtype=w1.dtype))
    return wbig.reshape(CMP_STRIDE * NSA_KVW, 2 * NSA_KV_HEADS * w1.shape[-1])


def _compress_rows(x, w_ref, bias_ref, w2_ref, o_ref):
    f32, bf16 = jnp.float32, jnp.bfloat16
    rows = x.shape[0]
    f = w2_ref.shape[0]
    proj = jnp.dot(x.astype(bf16), w_ref[...], preferred_element_type=f32)
    for n in range(NSA_KV_HEADS):
        first = proj[:, 2 * n * f:(2 * n + 1) * f]
        second = pltpu.roll(proj[:, (2 * n + 1) * f:(2 * n + 2) * f], rows - 1, 0)
        pre = first + second + bias_ref[...]
        hid = 0.5 * pre * (1.0 + lax.erf(pre * (2.0 ** -0.5)))
        o_ref[:, n * NSA_HD:(n + 1) * NSA_HD] = jnp.dot(hid.astype(bf16), w2_ref[...], preferred_element_type=f32)


def _compress_chunks_kernel(x_ref, w_ref, bias_ref, w2_ref, o_ref):
    _compress_rows(x_ref[...], w_ref, bias_ref, w2_ref, o_ref)


def _compress_weights(pe, w1, b1, w2):
    bf16 = jnp.bfloat16
    bias = jnp.dot(pe.reshape(-1), w1, precision=lax.Precision.HIGHEST) + b1
    return _expanded_w1(w1).astype(bf16), bias[None], w2.astype(bf16)


def compress_chunks(rows, per_seq, pe, w1, b1, w2, tm=512):
    chunks = rows.reshape(-1, CMP_STRIDE * NSA_KVW)
    n = chunks.shape[0]
    tm = min(tm, n)
    assert n % tm == 0 and tm % per_seq == 0
    wbig, bias, w2b = _compress_weights(pe, w1, b1, w2)

    def whole(a):
        return pl.BlockSpec(a.shape, lambda i: (0, 0))

    out = pl.pallas_call(
        _compress_chunks_kernel,
        out_shape=jax.ShapeDtypeStruct((n, NSA_KVW), jnp.float32),
        grid=(n // tm,),
        in_specs=[pl.BlockSpec((tm, chunks.shape[1]), lambda i: (i, 0)), whole(wbig), whole(bias), whole(w2b)],
        out_specs=pl.BlockSpec((tm, NSA_KVW), lambda i: (i, 0)),
        compiler_params=pltpu.CompilerParams(dimension_semantics=("arbitrary",),
                                             vmem_limit_bytes=VMEM_LIMIT_BYTES),
        name="compress_chunks",
    )(chunks, wbig, bias, w2b)
    return out.reshape(n // per_seq, per_seq, NSA_KV_HEADS, NSA_HD)[:, :-1]


PAGE_GROUP = 4


def _compress_pages_kernel(pg_ref, w_ref, bias_ref, w2_ref, o_ref, x_ref, t_ref):
    n_pages = pg_ref.shape[1]
    per_page = PAGE_SIZE // CMP_STRIDE
    group = PAGE_GROUP

    def place(i, carry):
        for u in range(group):
            g = i * group + u
            t_ref[u] = pg_ref[0, g].reshape(NSA_KVW, PAGE_SIZE).T
            row0 = pl.multiple_of(g * per_page, per_page)
            for p in range(CMP_STRIDE):
                x_ref[pl.ds(row0, per_page), p * NSA_KVW:(p + 1) * NSA_KVW] = (
                    t_ref.at[u][pl.ds(p, per_page, stride=CMP_STRIDE), :])
        return carry

    lax.fori_loop(0, n_pages // group, place, 0)
    _compress_rows(x_ref[...], w_ref, bias_ref, w2_ref, o_ref.at[0])


def compress_pages(pages, pe, w1, b1, w2):
    B, n_pages = pages.shape[:2]
    assert pages.shape[2:] == (NSA_KV_HEADS, NSA_HD, PAGE_SIZE) and NSA_KVW == LANES and PAGE_SIZE == LANES
    assert n_pages % PAGE_GROUP == 0
    wbig, bias, w2b = _compress_weights(pe, w1, b1, w2)
    rows = n_pages * (PAGE_SIZE // CMP_STRIDE)

    def whole(a):
        return pl.BlockSpec(a.shape, lambda b: (0, 0))

    out = pl.pallas_call(
        _compress_pages_kernel,
        out_shape=jax.ShapeDtypeStruct((B, rows, NSA_KVW), jnp.float32),
        grid=(B,),
        in_specs=[pl.BlockSpec((1, n_pages, NSA_KV_HEADS, NSA_HD, PAGE_SIZE), lambda b: (b, 0, 0, 0, 0)),
                  whole(wbig), whole(bias), whole(w2b)],
        out_specs=pl.BlockSpec((1, rows, NSA_KVW), lambda b: (b, 0, 0)),
        scratch_shapes=[pltpu.VMEM((rows, wbig.shape[0]), jnp.float32),
                        pltpu.VMEM((PAGE_GROUP, PAGE_SIZE, NSA_KVW), jnp.float32)],
        compiler_params=pltpu.CompilerParams(dimension_semantics=("arbitrary",),
                                             vmem_limit_bytes=VMEM_LIMIT_BYTES),
        name="compress_pages",
    )(pages, wbig, bias, w2b)
    return out.reshape(B, rows, NSA_KV_HEADS, NSA_HD)[:, :-1]


def cmp_attend(q, qpos, kc, vc):
    B, T = q.shape[:2]
    qg = q.reshape(B, T, NSA_KV_HEADS, NSA_GROUP, NSA_HD)
    s = jnp.einsum('btngd,bcnd->btngc', qg, kc) * ATTN_SCALE
    nblk = kc.shape[1]
    blk_end = jnp.arange(nblk) * CMP_STRIDE + CMP_BLOCK - 1
    valid = (blk_end[None, :] <= qpos[:, None])[None, :, None, None, :]
    p = jax.nn.softmax(jnp.where(valid, s, -1e30), axis=-1) * valid
    o = jnp.einsum('btngc,bcnd->btngd', p, vc)
    return o.reshape(B, T, NSA_HEADS, NSA_HD), p


def select_blocks(p, qpos, n_sel):
    imp = p.sum(axis=3)
    R = SEL_BLOCK // CMP_STRIDE
    r = CMP_BLOCK // CMP_STRIDE
    nb = imp.shape[-1]
    right = n_sel * R + R - 1 - nb
    padded = jnp.pad(imp, ((0, 0), (0, 0), (0, 0), (r - 1, right)))
    score = padded[..., 0:(n_sel - 1) * R + 1:R]
    for o in range(1, R + r - 1):
        score = score + padded[..., o:o + (n_sel - 1) * R + 1:R]
    j = jnp.arange(n_sel)[None, :]
    cur = (qpos // SEL_BLOCK)[:, None]
    valid = (j * SEL_BLOCK <= qpos[:, None])[None, :, None, :]
    forced = ((j == 0) | (j == cur) | (j == cur - 1))[None, :, None, :]
    score = jnp.where(forced, jnp.inf, jnp.where(valid, score, -jnp.inf))
    idx = j[0]
    before = (score[..., None, :] > score[..., :, None]) | ((score[..., None, :] == score[..., :, None])
                                                          & (idx[None, :] < idx[:, None]))
    return before.sum(-1) < min(SEL_TOP, n_sel)


def sample_selected_attention(q, qpos, member, k_pool, v_pool, k_new, v_new, page_table):
    B, T = q.shape[:2]
    n_pages = page_table.shape[1]
    per_page = PAGE_SIZE // SEL_BLOCK
    assert member.shape[-1] == n_pages * per_page + 1 and T <= SEL_BLOCK
    kp = k_pool.transpose(0, 2, 3, 1)[page_table]
    vp = v_pool.transpose(0, 2, 3, 1)[page_table]
    qg = q.reshape(B, T, NSA_KV_HEADS, NSA_GROUP, NSA_HD)
    s_past = jnp.einsum('btngd,bpndk->bntgpk', qg, kp) * ATTN_SCALE
    s_new = jnp.einsum('btngd,bsnd->bntgs', qg, k_new) * ATTN_SCALE
    m = member.transpose(0, 2, 1, 3)
    m_past = jnp.repeat(m[..., :-1].reshape(B, NSA_KV_HEADS, T, n_pages, per_page), SEL_BLOCK, axis=-1)
    kpos = (jnp.arange(n_pages) * PAGE_SIZE)[:, None] + jnp.arange(PAGE_SIZE)[None, :]
    m_past = m_past & (kpos[None, None, None] <= qpos[None, None, :, None, None])
    new_pos = n_pages * PAGE_SIZE + jnp.arange(T)
    m_new = m[..., -1:] & (new_pos[None, None, None, :] <= qpos[None, None, :, None])
    logits = jnp.concatenate(
        [jnp.where(m_past[:, :, :, None], s_past, -jnp.inf).reshape(B, NSA_KV_HEADS, T, NSA_GROUP, -1),
         jnp.where(m_new[:, :, :, None], s_new, -jnp.inf)], axis=-1)
    pr = jax.nn.softmax(logits, axis=-1)
    pr_past = pr[..., :n_pages * PAGE_SIZE].reshape(B, NSA_KV_HEADS, T, NSA_GROUP, n_pages, PAGE_SIZE)
    o = (jnp.einsum('bntgpk,bpndk->bntgd', pr_past, vp)
         + jnp.einsum('bntgs,bsnd->bntgd', pr[..., n_pages * PAGE_SIZE:], v_new))
    return o.transpose(0, 2, 1, 3, 4).reshape(B, T, NSA_HEADS, NSA_HD)


def to_blocks(rows, n_sel):
    B, L, KV, hd = rows.shape
    rows = jnp.pad(rows, ((0, 0), (0, n_sel * SEL_BLOCK - L), (0, 0), (0, 0)))
    return rows.reshape(B, n_sel, SEL_BLOCK, KV, hd).transpose(0, 3, 1, 2, 4)


def take_rows(table, idx):
    return table[idx]


def sel_attend(q, qpos, sel, kb, vb):
    B, Tq = q.shape[:2]
    k = sel.shape[-1]
    sel_t = sel.transpose(0, 2, 1, 3)
    gather = jax.vmap(jax.vmap(take_rows))
    kg = gather(kb, sel_t).reshape(B, NSA_KV_HEADS, Tq, k * SEL_BLOCK, NSA_HD)
    vg = gather(vb, sel_t).reshape(B, NSA_KV_HEADS, Tq, k * SEL_BLOCK, NSA_HD)
    kpos = (sel_t[..., None] * SEL_BLOCK + jnp.arange(SEL_BLOCK)).reshape(B, NSA_KV_HEADS, Tq, k * SEL_BLOCK)
    qg = q.reshape(B, Tq, NSA_KV_HEADS, NSA_GROUP, NSA_HD).transpose(0, 2, 1, 3, 4)
    s = jnp.einsum('bntgd,bntsd->bntgs', qg, kg) * ATTN_SCALE
    mask = kpos[:, :, :, None, :] <= qpos[None, None, :, None, None]
    pr = jax.nn.softmax(jnp.where(mask, s, -jnp.inf), axis=-1)
    o = jnp.einsum('bntgs,bntsd->bntgd', pr, vg)
    return o.transpose(0, 2, 1, 3, 4).reshape(B, Tq, NSA_HEADS, NSA_HD)


def win_attend(q, qpos, k, v, kpos):
    B, Tq = q.shape[:2]
    qg = q.reshape(B, Tq, NSA_KV_HEADS, NSA_GROUP, NSA_HD)
    s = jnp.einsum('btngd,bsnd->btngs', qg, k) * ATTN_SCALE
    diff = qpos[:, None] - kpos[None, :]
    mask = ((diff >= 0) & (diff < WINDOW) & (kpos[None, :] >= 0))[None, :, None, None, :]
    pr = jax.nn.softmax(jnp.where(mask, s, -jnp.inf), axis=-1)
    o = jnp.einsum('btngs,bsnd->btngd', pr, v)
    return o.reshape(B, Tq, NSA_HEADS, NSA_HD)


def nsa_combine(gates, o_cmp, o_sel, o_win):
    B, T = gates.shape[:2]
    o = gates[..., 0:1] * o_cmp + gates[..., 1:2] * o_sel + gates[..., 2:3] * o_win
    return o.reshape(B, T, NSA_QW)


NSA_TQ = 128
NSA_CK = 512
MASKED = -1e30


def _softmax_rows(s):
    m = jnp.max(s, axis=-1, keepdims=True)
    e = jnp.exp(s - m)
    return e / jnp.sum(e, axis=-1, keepdims=True)


def _nsa_prompt_kernel(q_ref, kc_ref, vc_ref, ks_ref, vs_ref, kw_ref, vw_ref, zg_ref, msel_ref, exp_ref, o_ref):
    f32, bf16 = jnp.float32, jnp.bfloat16
    tq = NSA_TQ
    q0 = pl.program_id(2) * tq
    qb = q_ref[0]
    qs = jnp.concatenate([qb[:, g * NSA_HD:(g + 1) * NSA_HD] for g in range(NSA_GROUP)], axis=0)
    tpos = q0 + lax.broadcasted_iota(jnp.int32, (tq, 1), 0)

    def per_head(a):
        return jnp.concatenate([a] * NSA_GROUP, axis=0)

    s = lax.dot_general(qs, kc_ref[0, 0], _NT, preferred_element_type=f32)
    cblk = lax.broadcasted_iota(jnp.int32, (tq, 128), 1)
    cvalid = cblk * CMP_STRIDE + (CMP_BLOCK - 1) <= tpos
    s = s + per_head(jnp.where(cvalid, 0.0, MASKED))
    e = jnp.exp(s - jnp.max(s, axis=-1, keepdims=True)) * per_head(jnp.where(cvalid, 1.0, 0.0))
    l = jnp.sum(e, axis=-1, keepdims=True)
    p = e / jnp.where(l > 0.0, l, 1.0)
    o_cmp = jnp.dot(p.astype(bf16), vc_ref[0, 0], preferred_element_type=f32)

    imp = p[0:tq]
    for g in range(1, NSA_GROUP):
        imp = imp + p[g * tq:(g + 1) * tq]
    hi = imp.astype(bf16)
    r1 = imp - hi.astype(f32)
    mid = r1.astype(bf16)
    lo = (r1 - mid.astype(f32)).astype(bf16)
    msel = msel_ref[...]
    score = (lax.dot_general(msel, hi, _NT, preferred_element_type=f32)
             + lax.dot_general(msel, mid, _NT, preferred_element_type=f32)
             + lax.dot_general(msel, lo, _NT, preferred_element_type=f32))
    n_sel = score.shape[0]
    j = lax.broadcasted_iota(jnp.int32, (n_sel, tq), 0)
    tok = q0 + lax.broadcasted_iota(jnp.int32, (n_sel, tq), 1)
    cur = tok // SEL_BLOCK
    forced = (j == 0) | (j == cur) | (j == cur - 1)
    score = jnp.where(forced, jnp.inf, jnp.where(j * SEL_BLOCK <= tok, score, -jnp.inf))
    rank = jnp.zeros((n_sel, tq), f32)
    for jp in range(n_sel):
        row = score[jp:jp + 1, :]
        before = (row > score) | ((row == score) & (j > jp))
        rank = rank + jnp.where(before, 1.0, 0.0)
    chosen_t = jnp.where(rank < SEL_TOP, 1.0, 0.0)
    sel01 = jnp.concatenate([chosen_t, jnp.zeros((LANES - n_sel, tq), f32)], axis=0).T.astype(bf16)

    ck = NSA_CK
    rows = NSA_GROUP * tq

    def sel_chunk(c, carry):
        m, l, acc = carry
        k0 = pl.multiple_of(c * ck, ck)
        s = lax.dot_general(qs, ks_ref[0, 0, pl.ds(k0, ck), :], _NT, preferred_element_type=f32)
        chosen = jnp.dot(sel01, exp_ref[c], preferred_element_type=f32)
        kpos = k0 + lax.broadcasted_iota(jnp.int32, (tq, ck), 1)
        ok = (chosen > 0.5) & (kpos <= tpos)
        s = s + per_head(jnp.where(ok, 0.0, MASKED))
        m_new = jnp.maximum(m, jnp.max(s, axis=-1, keepdims=True))
        a = jnp.exp(m - m_new)
        pr = jnp.exp(s - m_new)
        l = a * l + jnp.sum(pr, axis=-1, keepdims=True)
        acc = a * acc + jnp.dot(pr.astype(bf16), vs_ref[0, 0, pl.ds(k0, ck), :], preferred_element_type=f32)
        return m_new, l, acc

    init = (jnp.full((rows, 1), MASKED, f32), jnp.zeros((rows, 1), f32), jnp.zeros((rows, NSA_HD), f32))
    n_chunks = (q0 + tq + ck - 1) // ck
    _, l_sel, acc_sel = lax.fori_loop(0, n_chunks, sel_chunk, init)
    o_sel = acc_sel / l_sel

    w0 = pl.multiple_of(jnp.maximum(q0 - WINDOW, 0), tq)
    wl = WINDOW + tq
    s = lax.dot_general(qs, kw_ref[0, 0, pl.ds(w0, wl), :], _NT, preferred_element_type=f32)
    diff = tpos - (w0 + lax.broadcasted_iota(jnp.int32, (tq, wl), 1))
    s = s + per_head(jnp.where((diff >= 0) & (diff < WINDOW), 0.0, MASKED))
    o_win = jnp.dot(_softmax_rows(s).astype(bf16), vw_ref[0, 0, pl.ds(w0, wl), :], preferred_element_type=f32)

    gates = jax.nn.sigmoid(zg_ref[0, 0])
    for g in range(NSA_GROUP):
        r = slice(g * tq, (g + 1) * tq)
        o_ref[0, :, g * NSA_HD:(g + 1) * NSA_HD] = (gates[:, 3 * g:3 * g + 1] * o_cmp[r]
                                                    + gates[:, 3 * g + 1:3 * g + 2] * o_sel[r]
                                                    + gates[:, 3 * g + 2:3 * g + 3] * o_win[r])


def nsa_prompt_attention(qs, kc, vc, k_slc, v_slc, k_win, v_win, zg):
    B, S = qs.shape[:2]
    bf16 = jnp.bfloat16
    assert S % NSA_CK == 0 and S % NSA_TQ == 0 and WINDOW % NSA_TQ == 0 and WINDOW + NSA_TQ <= S
    n_sel = S // SEL_BLOCK
    nb = kc.shape[1]
    assert nb <= 128

    def pad_blocks(a):
        return jnp.pad(a.transpose(0, 2, 1, 3).astype(bf16), ((0, 0), (0, 0), (0, 128 - nb), (0, 0)))

    zg4 = zg.reshape(B, S, NSA_KV_HEADS, 3 * NSA_GROUP).transpose(0, 2, 1, 3)
    c = np.arange(128)[:, None]
    jj = np.arange(n_sel)[None, :]
    ratio = SEL_BLOCK // CMP_STRIDE
    msel = ((c >= jj * ratio - (CMP_BLOCK // CMP_STRIDE - 1)) & (c <= jj * ratio + ratio - 1) & (c < nb))
    assert n_sel <= LANES and NSA_TQ == LANES
    expand = (np.arange(S)[None, :] // SEL_BLOCK == np.arange(LANES)[:, None])
    expand = expand.reshape(LANES, S // NSA_CK, NSA_CK).transpose(1, 0, 2)
    row_spec = pl.BlockSpec((1, 1, S, NSA_HD), lambda b, n, i: (b, n, 0, 0))
    blk_spec = pl.BlockSpec((1, 1, 128, NSA_HD), lambda b, n, i: (b, n, 0, 0))
    return pl.pallas_call(
        _nsa_prompt_kernel,
        out_shape=jax.ShapeDtypeStruct((B, S, NSA_QW), jnp.float32),
        grid=(B, NSA_KV_HEADS, S // NSA_TQ),
        in_specs=[pl.BlockSpec((1, NSA_TQ, NSA_GROUP * NSA_HD), lambda b, n, i: (b, i, n)),
                  blk_spec, blk_spec, row_spec, row_spec, row_spec, row_spec,
                  pl.BlockSpec((1, 1, NSA_TQ, 3 * NSA_GROUP), lambda b, n, i: (b, n, i, 0)),
                  pl.BlockSpec((n_sel, 128), lambda b, n, i: (0, 0)),
                  pl.BlockSpec((S // NSA_CK, LANES, NSA_CK), lambda b, n, i: (0, 0, 0))],
        out_specs=pl.BlockSpec((1, NSA_TQ, NSA_GROUP * NSA_HD), lambda b, n, i: (b, i, n)),
        compiler_params=pltpu.CompilerParams(dimension_semantics=("arbitrary", "arbitrary", "arbitrary"),
                                             vmem_limit_bytes=VMEM_LIMIT_BYTES),
        name="nsa_prompt_attention",
    )(qs, pad_blocks(kc), pad_blocks(vc), k_slc, v_slc, k_win, v_win,
      zg4, jnp.asarray(msel.T, bf16), jnp.asarray(expand, bf16))


MLSTM_L = 256
CONV_HALO = 8


def _log_sigmoid(x):
    return -(jnp.maximum(-x, 0.0) + jnp.log1p(jnp.exp(-jnp.abs(x))))


def _mlstm_prompt_kernel(x_ref, xprev_ref, halo0_ref, v_ref, o_ref, gcol_ref, grow_ref, cw_ref, cb_ref,
                         out_ref, c_out, n_out, m_out, c_ref, n_ref, m_ref):
    f32, bf16 = jnp.float32, jnp.bfloat16
    c = pl.program_id(1)
    L = MLSTM_L

    @pl.when(c == 0)
    def _():
        c_ref[...] = jnp.zeros_like(c_ref)
        n_ref[...] = jnp.zeros_like(n_ref)
        m_ref[...] = jnp.zeros_like(m_ref)

    x = x_ref[0]
    halo = jnp.where(c == 0, halo0_ref[0], xprev_ref[0, L - CONV_HALO:L, :])
    ext = jnp.concatenate([halo, x], axis=0)
    conv = cb_ref[...]
    for j in range(CONV_W):
        o = CONV_HALO - (CONV_W - 1) + j
        conv = conv + ext[o:o + L] * cw_ref[j:j + 1, :]
    qk = conv * jax.nn.sigmoid(conv)

    t_id = lax.broadcasted_iota(jnp.int32, (L, L), 0)
    s_id = lax.broadcasted_iota(jnp.int32, (L, L), 1)
    causal = t_id >= s_id
    gcol = gcol_ref[0, 0]
    grow = grow_ref[0, 0]
    for h in range(M_HEADS):
        hd = slice(h * M_HD, (h + 1) * M_HD)
        q = qk[:, hd]
        k = qk[:, M_W + h * M_HD:M_W + (h + 1) * M_HD] * (M_HD ** -0.5)
        v = v_ref[0, :, hd].astype(bf16)
        ig_r = grow[h:h + 1, :]
        ig_c = gcol[:, h:h + 1]
        lf_r = _log_sigmoid(grow[M_HEADS + h:M_HEADS + h + 1, :])
        lf_c = _log_sigmoid(gcol[:, M_HEADS + h:M_HEADS + h + 1])
        b_c = jnp.sum(jnp.where(causal, lf_r, 0.0), axis=1, keepdims=True)
        b_r = jnp.sum(jnp.where(t_id <= s_id, lf_c, 0.0), axis=0, keepdims=True)
        m_prev = m_ref[h]
        dmat = jnp.where(causal, b_c - b_r + ig_r, -jnp.inf)
        inter = b_c + m_prev
        m_t = jnp.maximum(inter, jnp.max(dmat, axis=1, keepdims=True))
        w_intra = jnp.exp(dmat - m_t)
        w_inter = jnp.exp(inter - m_t)
        qb = q.astype(bf16)
        s = lax.dot_general(qb, k.astype(bf16), _NT, preferred_element_type=f32) * w_intra
        num = (jnp.dot(s.astype(bf16), v, preferred_element_type=f32)
               + w_inter * jnp.dot(qb, c_ref[h].astype(bf16), preferred_element_type=f32))
        den = jnp.sum(s, axis=1, keepdims=True) + w_inter * jnp.sum(q * n_ref[h], axis=1, keepdims=True)
        hh = num / jnp.maximum(jnp.abs(den), jnp.exp(-m_t))
        out_ref[0, :, hd] = jax.nn.sigmoid(o_ref[0, :, hd]) * hh
        m_new = m_t[L - 1:L]
        b_last = b_c[L - 1:L]
        w_s = jnp.exp(b_last - b_c + ig_c - m_new)
        w_p = jnp.exp(b_last + m_prev - m_new)
        kw = k * w_s
        c_ref[h] = w_p * c_ref[h] + jnp.dot(kw.T.astype(bf16), v, preferred_element_type=f32)
        n_ref[h] = w_p * n_ref[h] + jnp.sum(kw, axis=0, keepdims=True)
        m_ref[h] = m_new

    @pl.when(c == pl.num_programs(1) - 1)
    def _():
        c_out[0] = c_ref[...]
        n_out[0] = n_ref[...]
        m_out[0] = m_ref[...]


def mlstm_prompt(zqk, zv, zo, zif, conv_w, conv_b, b_if):
    B, T, _ = zqk.shape
    L = MLSTM_L
    assert T % L == 0
    nc = T // L
    f32 = jnp.float32
    gif = zif + b_if
    gcol = gif.reshape(B, nc, L, 2 * M_HEADS)
    grow = gcol.transpose(0, 1, 3, 2)
    halo0 = jnp.zeros((B, CONV_HALO, 2 * M_W), f32)
    out, C, n, m = pl.pallas_call(
        _mlstm_prompt_kernel,
        out_shape=(jax.ShapeDtypeStruct((B, T, M_W), f32),
                   jax.ShapeDtypeStruct((B, M_HEADS, M_HD, M_HD), f32),
                   jax.ShapeDtypeStruct((B, M_HEADS, 1, M_HD), f32),
                   jax.ShapeDtypeStruct((B, M_HEADS, 1, 1), f32)),
        grid=(B, nc),
        in_specs=[pl.BlockSpec((1, L, 2 * M_W), lambda b, c: (b, c, 0)),
                  pl.BlockSpec((1, L, 2 * M_W), lambda b, c: (b, jnp.maximum(c - 1, 0), 0)),
                  pl.BlockSpec((1, CONV_HALO, 2 * M_W), lambda b, c: (b, 0, 0)),
                  pl.BlockSpec((1, L, M_W), lambda b, c: (b, c, 0)),
                  pl.BlockSpec((1, L, M_W), lambda b, c: (b, c, 0)),
                  pl.BlockSpec((1, 1, L, 2 * M_HEADS), lambda b, c: (b, c, 0, 0)),
                  pl.BlockSpec((1, 1, 2 * M_HEADS, L), lambda b, c: (b, c, 0, 0)),
                  pl.BlockSpec((CONV_W, 2 * M_W), lambda b, c: (0, 0)),
                  pl.BlockSpec((1, 2 * M_W), lambda b, c: (0, 0))],
        out_specs=(pl.BlockSpec((1, L, M_W), lambda b, c: (b, c, 0)),
                   pl.BlockSpec((1, M_HEADS, M_HD, M_HD), lambda b, c: (b, 0, 0, 0)),
                   pl.BlockSpec((1, M_HEADS, 1, M_HD), lambda b, c: (b, 0, 0, 0)),
                   pl.BlockSpec((1, M_HEADS, 1, 1), lambda b, c: (b, 0, 0, 0))),
        scratch_shapes=[pltpu.VMEM((M_HEADS, M_HD, M_HD), f32), pltpu.VMEM((M_HEADS, 1, M_HD), f32),
                        pltpu.VMEM((M_HEADS, 1, 1), f32)],
        compiler_params=pltpu.CompilerParams(dimension_semantics=("arbitrary", "arbitrary"),
                                             vmem_limit_bytes=VMEM_LIMIT_BYTES),
        name="mlstm_prompt",
    )(zqk, zqk, halo0, zv, zo, gcol, grow, conv_w, conv_b[None])
    return out, C, n.reshape(B, M_HEADS, M_HD), m.reshape(B, M_HEADS)


def mlstm_chunk(carry, inp):
    C, n, m = carry
    q, k, v, ig, lf = inp
    L = q.shape[2]
    b = jnp.cumsum(lf, axis=-1)
    causal = jnp.tril(jnp.ones((L, L), dtype=bool))
    dmat = jnp.where(causal, b[..., :, None] - b[..., None, :] + ig[..., None, :], -jnp.inf)
    inter = b + m[..., None]
    m_t = jnp.maximum(inter, dmat.max(axis=-1))
    w_intra = jnp.exp(dmat - m_t[..., None])
    w_inter = jnp.exp(inter - m_t)
    s = jnp.einsum('bhtd,bhsd->bhts', q, k) * w_intra
    num = jnp.einsum('bhts,bhsv->bhtv', s, v) + w_inter[..., None] * jnp.einsum('bhtd,bhdv->bhtv', q, C)
    den = s.sum(-1) + w_inter * jnp.einsum('bhtd,bhd->bht', q, n)
    h = num / jnp.maximum(jnp.abs(den), jnp.exp(-m_t))[..., None]
    m_new = m_t[..., -1]
    w_s = jnp.exp(b[..., -1:] - b + ig - m_new[..., None])
    w_p = jnp.exp(b[..., -1] + m - m_new)
    C_new = w_p[..., None, None] * C + jnp.einsum('bhs,bhsd,bhsv->bhdv', w_s, k, v)
    n_new = w_p[..., None] * n + jnp.einsum('bhs,bhsd->bhd', w_s, k)
    return (C_new, n_new, m_new), h


def mlstm_mix(zqk, zv, zo, zif, buf0, C0, n0, m0, conv_w, conv_b, b_if, chunk):
    B, T, _ = zqk.shape
    full = jnp.concatenate([buf0, zqk], axis=1)
    conv = conv_b
    for j in range(CONV_W):
        conv = conv + full[:, j:j + T] * conv_w[j]
    qk = jax.nn.silu(conv)

    def heads(a):
        return a.reshape(B, T, M_HEADS, M_HD).transpose(0, 2, 1, 3)

    q = heads(qk[..., :M_W])
    k = heads(qk[..., M_W:]) * (M_HD ** -0.5)
    v = heads(zv)
    gif = zif + b_if
    ig = gif[..., :M_HEADS].transpose(0, 2, 1)
    lf = jax.nn.log_sigmoid(gif[..., M_HEADS:]).transpose(0, 2, 1)
    nc = T // chunk

    def to_chunks(a):
        return jnp.moveaxis(a.reshape(B, M_HEADS, nc, chunk, *a.shape[3:]), 2, 0)

    (C, n, m), h = lax.scan(mlstm_chunk, (C0, n0, m0),
                            (to_chunks(q), to_chunks(k), to_chunks(v), to_chunks(ig), to_chunks(lf)))
    h = jnp.moveaxis(h, 0, 2).reshape(B, M_HEADS, T, M_HD).transpose(0, 2, 1, 3).reshape(B, T, M_W)
    out = jax.nn.sigmoid(zo) * h
    return out, (C, n, m, full[:, T:])


PEER_COMBOS = 2 * PEER_HEADS
PEER_KEY_ROWS = 8
PEER_TILE = PEER_KEY_ROWS * N_KEYS
PEER_TS_ROWS = 24
LANES = 128
_NT = (((1,), (1,)), ((), ()))


def _peer_topk_kernel(q_ref, keys_ref, s_ref, e0_ref, e1_ref, tau_ref, ts_ref):
    head = pl.program_id(1)
    tt = q_ref.shape[0]
    half_w = PEER_QDIM // 2
    for half in range(2):
        c = 2 * head + half
        qh = q_ref[:, half * half_w:(half + 1) * half_w].astype(jnp.bfloat16)
        s = lax.dot_general(keys_ref[half], qh, _NT, preferred_element_type=jnp.float32)
        s_ref[c] = s
        pad_rows = jnp.full((PEER_TS_ROWS - PEER_TOPK - 1, tt), -jnp.inf, jnp.float32)
        work = s
        rows = []
        for _ in range(PEER_TOPK + 1):
            m = jnp.max(work, axis=0, keepdims=True)
            work = jnp.where(work == m, -jnp.inf, work)
            rows.append(m)
        ts_ref[c] = jnp.concatenate(rows + [pad_rows], axis=0)
        removed = jnp.sum(jnp.where(work == -jnp.inf, 1.0, 0.0), axis=0, keepdims=True)
        has_ties = jnp.max(removed) > PEER_TOPK + 1

        @pl.when(has_ties)
        def _(s=s, c=c):
            key_id = lax.broadcasted_iota(jnp.int32, s.shape, 0)
            work = s
            rows = []
            for _ in range(PEER_TOPK + 1):
                m = jnp.max(work, axis=0, keepdims=True)
                first = jnp.min(jnp.where(work == m, key_id, N_KEYS), axis=0, keepdims=True)
                work = jnp.where(key_id == first, -jnp.inf, work)
                rows.append(m)
            ts_ref[c] = jnp.concatenate(rows + [pad_rows], axis=0)

    @pl.when(head == PEER_HEADS - 1)
    def _():
        for h in range(PEER_HEADS):
            t0 = ts_ref[2 * h]
            t1 = ts_ref[2 * h + 1]
            pieces = [t0[0:1] + t1] + [t0[a:a + 1] + t1[0:8] for a in range(1, 8)] + [t0[8:24] + t1[0:1]]
            cand = jnp.concatenate(pieces, axis=0)
            top = t0[0:1] + t1[0:1]
            n_pad = 2 * (PEER_TS_ROWS - PEER_TOPK - 1)
            work = cand
            vals = []
            for _ in range(PEER_TOPK + 1):
                m = jnp.max(work, axis=0, keepdims=True)
                work = jnp.where(work == m, -jnp.inf, work)
                vals.append(m)
            z = jnp.exp(vals[0] - top)
            for r in range(1, PEER_TOPK):
                z = z + jnp.exp(vals[r] - top)
            tau_ref[h:h + 1, :] = 0.5 * vals[PEER_TOPK - 1] + 0.5 * vals[PEER_TOPK]
            e0_ref[h] = jnp.exp(s_ref[2 * h] - t0[0:1]) / z
            e1_ref[h] = jnp.exp(s_ref[2 * h + 1] - t1[0:1])
            removed = jnp.sum(jnp.where(work == -jnp.inf, 1.0, 0.0), axis=0, keepdims=True)

            @pl.when(jnp.max(removed) > PEER_TOPK + 1 + n_pad)
            def _(cand=cand, top=top, t0=t0, h=h):
                v16 = top
                v17 = top
                z = jnp.zeros_like(top)
                seen = jnp.zeros_like(top)
                for _ in range(PEER_TOPK + 1):
                    m = jnp.max(cand, axis=0, keepdims=True)
                    eq = cand == m
                    cnt = jnp.sum(jnp.where(eq, 1.0, 0.0), axis=0, keepdims=True)
                    active = seen < PEER_TOPK
                    take = jnp.minimum(cnt, PEER_TOPK - seen)
                    v16 = jnp.where(active, m, v16)
                    v17 = jnp.where(seen < PEER_TOPK + 1, m, v17)
                    z = z + jnp.where(active, take * jnp.exp(m - top), 0.0)
                    seen = seen + cnt
                    cand = jnp.where(eq, -jnp.inf, cand)
                tau_ref[h:h + 1, :] = 0.5 * v16 + 0.5 * v17
                e0_ref[h] = jnp.exp(s_ref[2 * h] - t0[0:1]) / z


def peer_scores(q, sub_keys, tt):
    n = q.shape[0]
    assert n % tt == 0
    keys = sub_keys.reshape(PEER_COMBOS, N_KEYS, PEER_QDIM // 2).astype(jnp.bfloat16)
    f32 = jnp.float32
    return pl.pallas_call(
        _peer_topk_kernel,
        out_shape=(jax.ShapeDtypeStruct((PEER_COMBOS, N_KEYS, n), f32),
                   jax.ShapeDtypeStruct((PEER_HEADS, N_KEYS, n), f32),
                   jax.ShapeDtypeStruct((PEER_HEADS, N_KEYS, n), f32),
                   jax.ShapeDtypeStruct((PEER_HEADS, n), f32)),
        grid=(n // tt, PEER_HEADS),
        in_specs=[pl.BlockSpec((tt, PEER_QDIM), lambda i, c: (i, c)),
                  pl.BlockSpec((2, N_KEYS, PEER_QDIM // 2), lambda i, c: (c, 0, 0))],
        out_specs=(pl.BlockSpec((PEER_COMBOS, N_KEYS, tt), lambda i, c: (0, 0, i)),
                   pl.BlockSpec((PEER_HEADS, N_KEYS, tt), lambda i, c: (0, 0, i)),
                   pl.BlockSpec((PEER_HEADS, N_KEYS, tt), lambda i, c: (0, 0, i)),
                   pl.BlockSpec((PEER_HEADS, tt), lambda i, c: (0, i))),
        scratch_shapes=[pltpu.VMEM((PEER_COMBOS, PEER_TS_ROWS, tt), f32)],
        compiler_params=pltpu.CompilerParams(dimension_semantics=("arbitrary", "arbitrary"),
                                             vmem_limit_bytes=VMEM_LIMIT_BYTES),
        name="peer_topk",
    )(q, keys)


def _peer_dense_kernel(xt_ref, h_ref, u_ref, vt_ref, s0_ref, ez_ref, s_ref, e1_ref, tau_ref, g_ref, b_ref,
                       o_ref, acc_ref, a_ref, w_ref):
    e = pl.program_id(1)
    tt = xt_ref.shape[1]

    @pl.when(e == 0)
    def _():
        acc_ref[...] = jnp.zeros_like(acc_ref)

    a_ref[...] = jnp.dot(u_ref[...], xt_ref[...], preferred_element_type=jnp.float32)
    for r in range(PEER_KEY_ROWS):
        rows = slice(r * N_KEYS, (r + 1) * N_KEYS)
        for t in range(tt // LANES):
            tok = slice(t * LANES, (t + 1) * LANES)
            gate = jnp.zeros((N_KEYS, LANES), jnp.float32)
            for h in range(PEER_HEADS):
                need = tau_ref[h:h + 1, tok] - s0_ref[2 * h, r:r + 1, tok]
                picked = jnp.where(s_ref[2 * h + 1, :, tok] >= need, e1_ref[h, :, tok], 0.0)
                gate = gate + picked * ez_ref[h, r:r + 1, tok]
            ar = a_ref[rows, tok]
            act = 0.5 * ar * (1.0 + lax.erf(ar * (2.0 ** -0.5)))
            w_ref[rows, tok] = (gate * act).astype(jnp.bfloat16)
    acc_ref[...] += jnp.dot(vt_ref[0], w_ref[...], preferred_element_type=jnp.float32)

    @pl.when(e == pl.num_programs(1) - 1)
    def _():
        r = ALPHA * h_ref[...] + acc_ref[...].T
        mu = jnp.mean(r, axis=-1, keepdims=True)
        d = r - mu
        var = jnp.mean(d * d, axis=-1, keepdims=True)
        o_ref[...] = d * lax.rsqrt(var + LN_EPS) * g_ref[...] + b_ref[...]


def peer_tail(h, ht, q, sub_keys, u_bf, vt_bf, ln_g, ln_b, tt):
    n, d = h.shape
    s, e0z, e1, tau = peer_scores(q, sub_keys, tt)
    n_exp = u_bf.shape[0]
    return pl.pallas_call(
        _peer_dense_kernel,
        out_shape=jax.ShapeDtypeStruct((n, d), jnp.float32),
        grid=(n // tt, n_exp // PEER_TILE),
        in_specs=[pl.BlockSpec((d, tt), lambda i, e: (0, i)),
                  pl.BlockSpec((tt, d), lambda i, e: (i, 0)),
                  pl.BlockSpec((PEER_TILE, d), lambda i, e: (e, 0)),
                  pl.BlockSpec((1, d, PEER_TILE), lambda i, e: (e, 0, 0)),
                  pl.BlockSpec((PEER_COMBOS, PEER_KEY_ROWS, tt), lambda i, e: (0, e, i)),
                  pl.BlockSpec((PEER_HEADS, PEER_KEY_ROWS, tt), lambda i, e: (0, e, i)),
                  pl.BlockSpec((PEER_COMBOS, N_KEYS, tt), lambda i, e: (0, 0, i)),
                  pl.BlockSpec((PEER_HEADS, N_KEYS, tt), lambda i, e: (0, 0, i)),
                  pl.BlockSpec((PEER_HEADS, tt), lambda i, e: (0, i)),
                  pl.BlockSpec((1, d), lambda i, e: (0, 0)),
                  pl.BlockSpec((1, d), lambda i, e: (0, 0))],
        out_specs=pl.BlockSpec((tt, d), lambda i, e: (i, 0)),
        scratch_shapes=[pltpu.VMEM((d, tt), jnp.float32), pltpu.VMEM((PEER_TILE, tt), jnp.float32),
                        pltpu.VMEM((PEER_TILE, tt), jnp.bfloat16)],
        compiler_params=pltpu.CompilerParams(dimension_semantics=("arbitrary", "arbitrary"),
                                             vmem_limit_bytes=VMEM_LIMIT_BYTES),
        name="peer_dense",
    )(ht, h, u_bf, vt_bf, s, e0z, s, e1, tau, ln_g[None], ln_b[None])


def _out_proj_kernel(x_ref, nsa_ref, m_ref, wn_ref, wm_ref, g_ref, b_ref, wq_ref, h_ref, ht_ref, q_ref):
    f32, bf16 = jnp.float32, jnp.bfloat16
    r = (ALPHA * x_ref[...] + jnp.dot(nsa_ref[...].astype(bf16), wn_ref[...], preferred_element_type=f32)
         + jnp.dot(m_ref[...].astype(bf16), wm_ref[...], preferred_element_type=f32))
    mu = jnp.mean(r, axis=-1, keepdims=True)
    d = r - mu
    var = jnp.mean(d * d, axis=-1, keepdims=True)
    h = d * lax.rsqrt(var + LN_EPS) * g_ref[...] + b_ref[...]
    h_ref[...] = h
    ht_ref[...] = h.T.astype(bf16)
    q_ref[...] = jnp.dot(h.astype(bf16), wq_ref[...], preferred_element_type=f32)


def out_proj_fused(x, o_nsa, o_m, w_out, ln_g, ln_b, w_pq, tm):
    n, d = x.shape
    assert n % tm == 0
    bf16 = jnp.bfloat16
    nq = w_pq.shape[1]

    def rows(width):
        return pl.BlockSpec((tm, width), lambda i: (i, 0))

    def whole(a):
        return pl.BlockSpec(a.shape, lambda i: (0, 0))

    wn = w_out[:NSA_QW].astype(bf16)
    wm = w_out[NSA_QW:].astype(bf16)
    wq = w_pq.astype(bf16)
    g, b = ln_g[None], ln_b[None]
    return pl.pallas_call(
        _out_proj_kernel,
        out_shape=(jax.ShapeDtypeStruct((n, d), jnp.float32), jax.ShapeDtypeStruct((d, n), bf16),
                   jax.ShapeDtypeStruct((n, nq), jnp.float32)),
        grid=(n // tm,),
        in_specs=[rows(d), rows(NSA_QW), rows(M_W), whole(wn), whole(wm), whole(g), whole(b), whole(wq)],
        out_specs=(rows(d), pl.BlockSpec((d, tm), lambda i: (0, i)), rows(nq)),
        compiler_params=pltpu.CompilerParams(dimension_semantics=("arbitrary",),
                                             vmem_limit_bytes=VMEM_LIMIT_BYTES),
        name="out_proj",
    )(x, o_nsa, o_m, wn, wm, g, b, wq)


def block_tail(x, o_nsa, o_m, w_out, ln_g, ln_b, w_pq, sub_keys, u_bf, vt_bf, tt):
    lead = x.shape[:-1]
    h, ht, q = out_proj_fused(x.reshape(-1, D_MODEL), o_nsa.reshape(-1, NSA_QW), o_m.reshape(-1, M_W),
                              w_out, ln_g[0], ln_b[0], w_pq, tt)
    return peer_tail(h, ht, q, sub_keys, u_bf, vt_bf, ln_g[1], ln_b[1], tt).reshape(*lead, D_MODEL)


def prompt_mix(x, w_in, pe, w1, b1, w2, conv_w, conv_b, b_if):
    B, S, _ = x.shape
    z = in_proj_fused(x, w_in, jnp.arange(S), 512, True)
    kc = compress_chunks(z["k_cmp"], S // CMP_STRIDE, pe[0], w1[0], b1[0], w2[0])
    vc = compress_chunks(z["v_cmp"], S // CMP_STRIDE, pe[1], w1[1], b1[1], w2[1])
    zgate = z["zgate"].reshape(B, S, GATE_W)
    o_nsa = nsa_prompt_attention(z["q"].reshape(B, S, NSA_QW), kc, vc, z["k_slc_bf"], z["v_slc_bf"],
                                 z["k_win_bf"], z["v_win_bf"], zgate[..., :IN_SPLITS[2]])
    zqk = z["zqk"].reshape(B, S, 2 * M_W)
    o_m, C, n, m = mlstm_prompt(zqk, z["zv"].reshape(B, S, M_W), z["zo"].reshape(B, S, M_W),
                                zgate[..., IN_SPLITS[2]:], conv_w, conv_b, b_if)
    buf = zqk[:, S - (CONV_W - 1):]
    wl = min(WINDOW, S)
    k_cmp, v_cmp, k_slc, v_slc, k_win, v_win = [
        z[k + "_cache"].reshape(B, NSA_KV_HEADS, NSA_HD, S).transpose(0, 3, 1, 2)
        for k in ("k_cmp", "v_cmp", "k_slc", "v_slc", "k_win", "v_win")]
    return (o_nsa, o_m), (k_cmp, v_cmp, k_slc, v_slc, k_win[:, S - wl:], v_win[:, S - wl:], C, n, m, buf)


def sample_mix(x, kc_pool, vc_pool, ks_pool, vs_pool, kw_buf, vw_buf, C0, n0, m0, buf0, page_table,
               w_in, pe, w1, b1, w2, conv_w, conv_b, b_if):
    B, T, _ = x.shape
    past = page_table.shape[1] * PAGE_SIZE
    pos = past + jnp.arange(T)
    z = in_proj_fused(x, w_in, pos, B * T, False)
    q = z["q"].astype(jnp.float32).reshape(B, T, NSA_HEADS, NSA_HD) * (1.0 / ATTN_SCALE)
    k_cmp, v_cmp, k_slc, v_slc, k_win, v_win = [
        z[k].reshape(B, T, NSA_KV_HEADS, NSA_HD) for k in ("k_cmp", "v_cmp", "k_slc", "v_slc", "k_win", "v_win")]
    zgate = z["zgate"].reshape(B, T, GATE_W)
    gates = jax.nn.sigmoid(zgate[..., :IN_SPLITS[2]]).reshape(B, T, NSA_HEADS, 3)
    zqk, zv, zo, zif = (z["zqk"].reshape(B, T, 2 * M_W), z["zv"].reshape(B, T, M_W), z["zo"].reshape(B, T, M_W),
                        zgate[..., IN_SPLITS[2]:])

    assert (past + T) // CMP_STRIDE == past // CMP_STRIDE

    def compressed(pool, c):
        pages = pool.transpose(0, 2, 3, 1)[page_table]
        return compress_pages(pages, pe[c], w1[c], b1[c], w2[c])

    o_cmp, p = cmp_attend(q, pos, compressed(kc_pool, 0), compressed(vc_pool, 1))
    n_sel = -(-(past + T) // SEL_BLOCK)
    member = select_blocks(p, pos, n_sel)
    o_sel = sample_selected_attention(q, pos, member, ks_pool, vs_pool, k_slc, v_slc, page_table)
    wb = kw_buf.shape[1]
    kw = jnp.concatenate([kw_buf, k_win], axis=1)
    vw = jnp.concatenate([vw_buf, v_win], axis=1)
    kpos = past - wb + jnp.arange(wb + T)
    o_win = win_attend(q, pos, kw, vw, kpos)
    o_nsa = nsa_combine(gates, o_cmp, o_sel, o_win)
    o_m, (C, n, m, buf) = mlstm_mix(zqk, zv, zo, zif, buf0, C0, n0, m0, conv_w, conv_b, b_if, T)
    return (o_nsa, o_m), (k_cmp, v_cmp, k_slc, v_slc, kw[:, T:], vw[:, T:], C, n, m, buf)


def kernel(x_prompt, x_sample, cache_k_cmp, cache_v_cmp, cache_k_slc, cache_v_slc, cache_k_win, cache_v_win,
           state_C, state_n, state_m, state_conv, page_table, w_in, w_out, w_phi1, b_phi1, w_phi2, pe_cmp,
           conv_w, conv_b, b_if, ln_g, ln_b, w_pq, sub_keys, u_tab, v_tab):
    l = 0
    mix_p, st_p = prompt_mix(x_prompt, w_in[l], pe_cmp[l], w_phi1[l], b_phi1[l], w_phi2[l],
                             conv_w[l], conv_b[l], b_if[l])
    mix_s, st_s = sample_mix(x_sample, cache_k_cmp[l], cache_v_cmp[l], cache_k_slc[l], cache_v_slc[l],
                             cache_k_win[l], cache_v_win[l], state_C[l], state_n[l], state_m[l],
                             state_conv[l], page_table, w_in[l], pe_cmp[l], w_phi1[l], b_phi1[l],
                             w_phi2[l], conv_w[l], conv_b[l], b_if[l])
    u_bf = u_tab[l].astype(jnp.bfloat16)
    vt_bf = v_tab[l].astype(jnp.bfloat16).reshape(-1, PEER_TILE, D_MODEL).transpose(0, 2, 1)
    xp = block_tail(x_prompt, *mix_p, w_out[l], ln_g[l], ln_b[l], w_pq[l], sub_keys[l], u_bf, vt_bf, 512)
    xs = block_tail(x_sample, *mix_s, w_out[l], ln_g[l], ln_b[l], w_pq[l], sub_keys[l], u_bf, vt_bf, 128)
    return (xp, xs) + tuple(a[None] for a in st_p) + tuple(a[None] for a in st_s)
```

```python
import functools

import jax
import jax.numpy as jnp
import numpy as np
from jax import lax
from jax.experimental import pallas as pl
from jax.experimental.pallas import tpu as pltpu

D_MODEL = 1024
DEPTH = 1
PAGE_SIZE = 128
NSA_HEADS = 8
NSA_KV_HEADS = 2
NSA_GROUP = NSA_HEADS // NSA_KV_HEADS
NSA_HD = 64
NSA_QW = NSA_HEADS * NSA_HD
NSA_KVW = NSA_KV_HEADS * NSA_HD
CMP_BLOCK = 32
CMP_STRIDE = 16
SEL_BLOCK = 64
SEL_TOP = 16
WINDOW = 512
Q_BLOCK = 64
ATTN_SCALE = NSA_HD ** -0.5
ROPE_THETA = 10000.0
M_HEADS = 4
M_HD = 128
M_W = M_HEADS * M_HD
M_CHUNK = 64
CONV_W = 4
PEER_HEADS = 8
N_KEYS = 128
PEER_TOPK = 16
PEER_QDIM = 256
PEER_BLOCK = 128
IN_SPLITS = (NSA_QW, 6 * NSA_KVW, 3 * NSA_HEADS, 2 * M_W, M_W, M_W, 2 * M_HEADS)
LN_EPS = 1e-5
ALPHA = (2 * DEPTH) ** 0.25

VMEM_LIMIT_BYTES = 56 * 1024 * 1024


def _mm_kernel(x_ref, w_ref, o_ref):
    o_ref[...] = jnp.dot(x_ref[...].astype(jnp.bfloat16), w_ref[...], preferred_element_type=jnp.float32)


def pallas_matmul(x, w, tm=512):
    M, K = x.shape
    N = w.shape[1]
    tm = min(tm, M)
    assert M % tm == 0
    return pl.pallas_call(
        _mm_kernel,
        out_shape=jax.ShapeDtypeStruct((M, N), jnp.float32),
        grid=(M // tm,),
        in_specs=[pl.BlockSpec((tm, K), lambda i: (i, 0)), pl.BlockSpec((K, N), lambda i: (0, 0))],
        out_specs=pl.BlockSpec((tm, N), lambda i: (i, 0)),
        compiler_params=pltpu.CompilerParams(dimension_semantics=("arbitrary",),
                                             vmem_limit_bytes=VMEM_LIMIT_BYTES),
        name="proj_matmul",
    )(x, w.astype(jnp.bfloat16))


def mm3(x, w):
    lead = x.shape[:-1]
    return pallas_matmul(x.reshape(-1, x.shape[-1]), w).reshape(*lead, w.shape[1])


def layer_norm(x, g, b):
    mu = x.mean(-1, keepdims=True)
    var = jnp.square(x - mu).mean(-1, keepdims=True)
    return (x - mu) * lax.rsqrt(var + LN_EPS) * g + b


def rope(x, pos):
    half = x.shape[-1] // 2
    inv = ROPE_THETA ** (-jnp.arange(half, dtype=jnp.float32) / half)
    ang = pos.astype(jnp.float32)[:, None] * inv[None, :]
    cos = jnp.cos(ang)[:, None, :]
    sin = jnp.sin(ang)[:, None, :]
    x1, x2 = x[..., :half], x[..., half:]
    return jnp.concatenate([x1 * cos - x2 * sin, x2 * cos + x1 * sin], axis=-1)


def split_in_proj(x, w_in):
    z = mm3(x, w_in)
    cuts = [int(c) for c in np.cumsum(IN_SPLITS)[:-1]]
    return jnp.split(z, cuts, axis=-1)


_IN_OFF = np.concatenate([[0], np.cumsum(IN_SPLITS)])
_IN_ORDER = (0, 1, 3, 4, 5, 2, 6)
_N_KV_ROWS = 6
_KV_BF16 = (2, 3, 4, 5)
_KV_CACHE = (0, 1, 2, 3, 4, 5)
GATE_W = IN_SPLITS[2] + IN_SPLITS[6]


def _rope_pairs(x, cos, sin_signed):
    half = NSA_HD // 2
    lane = lax.broadcasted_iota(jnp.int32, x.shape, 1)
    partner = jnp.where(lane % NSA_HD < half, pltpu.roll(x, LANES - half, 1), pltpu.roll(x, half, 1))
    return x * cos + partner * sin_signed


def _in_proj_kernel(x_ref, w_ref, cos_ref, sin_ref, q_ref, *rest, kv_major):
    kv_refs = rest[:_N_KV_ROWS]
    rest = rest[_N_KV_ROWS:]
    if kv_major:
        bf_refs, rest = rest[:len(_KV_BF16)], rest[len(_KV_BF16):]
        cache_refs, rest = rest[:len(_KV_CACHE)], rest[len(_KV_CACHE):]
    zqk_ref, zv_ref, zo_ref, zgate_ref = rest
    z = jnp.dot(x_ref[...].astype(jnp.bfloat16), w_ref[...], preferred_element_type=jnp.float32)
    cos = cos_ref[...]
    sin = sin_ref[...]
    for g in range(NSA_QW // LANES):
        sl = slice(g * LANES, (g + 1) * LANES)
        q_ref[:, sl] = (_rope_pairs(z[:, sl], cos, sin) * ATTN_SCALE).astype(jnp.bfloat16)
    for r in range(_N_KV_ROWS):
        row = z[:, NSA_QW + r * NSA_KVW:NSA_QW + (r + 1) * NSA_KVW]
        if r % 2 == 0:
            row = _rope_pairs(row, cos, sin)
        kv_refs[r][...] = row
        if kv_major and r in _KV_CACHE:
            dst = cache_refs[_KV_CACHE.index(r)]
            dst[0] = row.T
        if kv_major and r in _KV_BF16:
            dst = bf_refs[_KV_BF16.index(r)]
            for n in range(NSA_KV_HEADS):
                dst[0, n] = row[:, n * NSA_HD:(n + 1) * NSA_HD].astype(jnp.bfloat16)
    o = NSA_QW + _N_KV_ROWS * NSA_KVW
    zqk_ref[...] = z[:, o:o + 2 * M_W]
    zv_ref[...] = z[:, o + 2 * M_W:o + 3 * M_W]
    zo_ref[...] = z[:, o + 3 * M_W:o + 4 * M_W]
    zgate_ref[...] = z[:, o + 4 * M_W:o + 4 * M_W + GATE_W]


def in_proj_fused(x, w_in, pos, tm, kv_major):
    B, T, D = x.shape
    M = B * T
    assert M % tm == 0 and NSA_KVW == LANES and (not kv_major or T % tm == 0)
    f32, bf16 = jnp.float32, jnp.bfloat16
    w = jnp.concatenate([w_in[:, _IN_OFF[i]:_IN_OFF[i + 1]] for i in _IN_ORDER], axis=1).astype(bf16)
    half = NSA_HD // 2
    inv = ROPE_THETA ** (-jnp.arange(half, dtype=f32) / half)
    ang = pos.astype(f32)[:, None] * inv[None, :]
    cos = jnp.tile(jnp.cos(ang), (B, 2 * LANES // NSA_HD))
    sin = jnp.tile(jnp.concatenate([-jnp.sin(ang), jnp.sin(ang)], axis=1), (B, LANES // NSA_HD))
    n_w = w.shape[1]

    def rows(width):
        return pl.BlockSpec((tm, width), lambda i: (i, 0))

    out_shape = [jax.ShapeDtypeStruct((M, NSA_QW), bf16)] + [jax.ShapeDtypeStruct((M, NSA_KVW), f32)] * _N_KV_ROWS
    out_specs = [rows(NSA_QW)] + [rows(NSA_KVW)] * _N_KV_ROWS
    if kv_major:
        per_seq = T // tm
        out_shape += [jax.ShapeDtypeStruct((B, NSA_KV_HEADS, T, NSA_HD), bf16)] * len(_KV_BF16)
        out_specs += [pl.BlockSpec((1, NSA_KV_HEADS, tm, NSA_HD),
                                   lambda i: (i // per_seq, 0, i % per_seq, 0))] * len(_KV_BF16)
        out_shape += [jax.ShapeDtypeStruct((B, NSA_KVW, T), f32)] * len(_KV_CACHE)
        out_specs += [pl.BlockSpec((1, NSA_KVW, tm), lambda i: (i // per_seq, 0, i % per_seq))] * len(_KV_CACHE)
    out_shape += [jax.ShapeDtypeStruct((M, 2 * M_W), f32), jax.ShapeDtypeStruct((M, M_W), f32),
                  jax.ShapeDtypeStruct((M, M_W), f32), jax.ShapeDtypeStruct((M, GATE_W), f32)]
    out_specs += [rows(2 * M_W), rows(M_W), rows(M_W), rows(GATE_W)]
    outs = pl.pallas_call(
        functools.partial(_in_proj_kernel, kv_major=kv_major),
        out_shape=tuple(out_shape),
        grid=(M // tm,),
        in_specs=[rows(D), pl.BlockSpec((D, n_w), lambda i: (0, 0)), rows(LANES), rows(LANES)],
        out_specs=tuple(out_specs),
        compiler_params=pltpu.CompilerParams(dimension_semantics=("arbitrary",),
                                             vmem_limit_bytes=VMEM_LIMIT_BYTES),
        name="in_proj",
    )(x.reshape(M, D), w, cos, sin)
    names = ["q", "k_cmp", "v_cmp", "k_slc", "v_slc", "k_win", "v_win"]
    if kv_major:
        names += ["k_slc_bf", "v_slc_bf", "k_win_bf", "v_win_bf"]
        names += ["k_cmp_cache", "v_cmp_cache", "k_slc_cache", "v_slc_cache", "k_win_cache", "v_win_cache"]
    names += ["zqk", "zv", "zo", "zgate"]
    return dict(zip(names, outs))


def nsa_project(zq, zkv, zg, pos):
    B, T, _ = zq.shape
    q = rope(zq.reshape(B, T, NSA_HEADS, NSA_HD), pos)
    kv = zkv.reshape(B, T, 6, NSA_KV_HEADS, NSA_HD)
    rows = (rope(kv[:, :, 0], pos), kv[:, :, 1], rope(kv[:, :, 2], pos), kv[:, :, 3],
            rope(kv[:, :, 4], pos), kv[:, :, 5])
    gates = jax.nn.sigmoid(zg).reshape(B, T, NSA_HEADS, 3)
    return q, rows, gates


def _expanded_w1(w1):
    assert CMP_BLOCK == 2 * CMP_STRIDE
    w1r = w1.reshape(2, CMP_STRIDE, NSA_HD, w1.shape[-1])
    wbig = jnp.einsum('hpdf,kn->pkdnhf', w1r, jnp.eye(NSA_KV_HEADS, dtype=w1.dtype))
    return wbig.reshape(CMP_STRIDE * NSA_KVW, 2 * NSA_KV_HEADS * w1.shape[-1])


def _compress_rows(x, w_ref, bias_ref, w2_ref, o_ref):
    f32, bf16 = jnp.float32, jnp.bfloat16
    rows = x.shape[0]
    f = w2_ref.shape[0]
    proj = jnp.dot(x.astype(bf16), w_ref[...], preferred_element_type=f32)
    for n in range(NSA_KV_HEADS):
        first = proj[:, 2 * n * f:(2 * n + 1) * f]
        second = pltpu.roll(proj[:, (2 * n + 1) * f:(2 * n + 2) * f], rows - 1, 0)
        pre = first + second + bias_ref[...]
        hid = 0.5 * pre * (1.0 + lax.erf(pre * (2.0 ** -0.5)))
        o_ref[:, n * NSA_HD:(n + 1) * NSA_HD] = jnp.dot(hid.astype(bf16), w2_ref[...], preferred_element_type=f32)


def _compress_chunks_kernel(x_ref, w_ref, bias_ref, w2_ref, o_ref):
    _compress_rows(x_ref[...], w_ref, bias_ref, w2_ref, o_ref)


def _compress_weights(pe, w1, b1, w2):
    bf16 = jnp.bfloat16
    bias = jnp.dot(pe.reshape(-1), w1, precision=lax.Precision.HIGHEST) + b1
    return _expanded_w1(w1).astype(bf16), bias[None], w2.astype(bf16)


def compress_chunks(rows, per_seq, pe, w1, b1, w2, tm=512):
    chunks = rows.reshape(-1, CMP_STRIDE * NSA_KVW)
    n = chunks.shape[0]
    tm = min(tm, n)
    assert n % tm == 0 and tm % per_seq == 0
    wbig, bias, w2b = _compress_weights(pe, w1, b1, w2)

    def whole(a):
        return pl.BlockSpec(a.shape, lambda i: (0, 0))

    out = pl.pallas_call(
        _compress_chunks_kernel,
        out_shape=jax.ShapeDtypeStruct((n, NSA_KVW), jnp.float32),
        grid=(n // tm,),
        in_specs=[pl.BlockSpec((tm, chunks.shape[1]), lambda i: (i, 0)), whole(wbig), whole(bias), whole(w2b)],
        out_specs=pl.BlockSpec((tm, NSA_KVW), lambda i: (i, 0)),
        compiler_params=pltpu.CompilerParams(dimension_semantics=("arbitrary",),
                                             vmem_limit_bytes=VMEM_LIMIT_BYTES),
        name="compress_chunks",
    )(chunks, wbig, bias, w2b)
    return out.reshape(n // per_seq, per_seq, NSA_KV_HEADS, NSA_HD)[:, :-1]


PAGE_GROUP = 4


def _compress_pages_kernel(pg_ref, w_ref, bias_ref, w2_ref, o_ref, x_ref, t_ref):
    n_pages = pg_ref.shape[1]
    per_page = PAGE_SIZE // CMP_STRIDE
    group = PAGE_GROUP

    def place(i, carry):
        for u in range(group):
            g = i * group + u
            t_ref[u] = pg_ref[0, g].reshape(NSA_KVW, PAGE_SIZE).T
            row0 = pl.multiple_of(g * per_page, per_page)
            for p in range(CMP_STRIDE):
                x_ref[pl.ds(row0, per_page), p * NSA_KVW:(p + 1) * NSA_KVW] = (
                    t_ref.at[u][pl.ds(p, per_page, stride=CMP_STRIDE), :])
        return carry

    lax.fori_loop(0, n_pages // group, place, 0)
    _compress_rows(x_ref[...], w_ref, bias_ref, w2_ref, o_ref.at[0])


def compress_pages(pages, pe, w1, b1, w2):
    B, n_pages = pages.shape[:2]
    assert pages.shape[2:] == (NSA_KV_HEADS, NSA_HD, PAGE_SIZE) and NSA_KVW == LANES and PAGE_SIZE == LANES
    assert n_pages % PAGE_GROUP == 0
    wbig, bias, w2b = _compress_weights(pe, w1, b1, w2)
    rows = n_pages * (PAGE_SIZE // CMP_STRIDE)

    def whole(a):
        return pl.BlockSpec(a.shape, lambda b: (0, 0))

    out = pl.pallas_call(
        _compress_pages_kernel,
        out_shape=jax.ShapeDtypeStruct((B, rows, NSA_KVW), jnp.float32),
        grid=(B,),
        in_specs=[pl.BlockSpec((1, n_pages, NSA_KV_HEADS, NSA_HD, PAGE_SIZE), lambda b: (b, 0, 0, 0, 0)),
                  whole(wbig), whole(bias), whole(w2b)],
        out_specs=pl.BlockSpec((1, rows, NSA_KVW), lambda b: (b, 0, 0)),
        scratch_shapes=[pltpu.VMEM((rows, wbig.shape[0]), jnp.float32),
                        pltpu.VMEM((PAGE_GROUP, PAGE_SIZE, NSA_KVW), jnp.float32)],
        compiler_params=pltpu.CompilerParams(dimension_semantics=("arbitrary",),
                                             vmem_limit_bytes=VMEM_LIMIT_BYTES),
        name="compress_pages",
    )(pages, wbig, bias, w2b)
    return out.reshape(B, rows, NSA_KV_HEADS, NSA_HD)[:, :-1]


def cmp_attend(q, qpos, kc, vc):
    B, T = q.shape[:2]
    qg = q.reshape(B, T, NSA_KV_HEADS, NSA_GROUP, NSA_HD)
    s = jnp.einsum('btngd,bcnd->btngc', qg, kc) * ATTN_SCALE
    nblk = kc.shape[1]
    blk_end = jnp.arange(nblk) * CMP_STRIDE + CMP_BLOCK - 1
    valid = (blk_end[None, :] <= qpos[:, None])[None, :, None, None, :]
    p = jax.nn.softmax(jnp.where(valid, s, -1e30), axis=-1) * valid
    o = jnp.einsum('btngc,bcnd->btngd', p, vc)
    return o.reshape(B, T, NSA_HEADS, NSA_HD), p


def select_blocks(p, qpos, n_sel):
    imp = p.sum(axis=3)
    R = SEL_BLOCK // CMP_STRIDE
    r = CMP_BLOCK // CMP_STRIDE
    nb = imp.shape[-1]
    right = n_sel * R + R - 1 - nb
    padded = jnp.pad(imp, ((0, 0), (0, 0), (0, 0), (r - 1, right)))
    score = padded[..., 0:(n_sel - 1) * R + 1:R]
    for o in range(1, R + r - 1):
        score = score + padded[..., o:o + (n_sel - 1) * R + 1:R]
    j = jnp.arange(n_sel)[None, :]
    cur = (qpos // SEL_BLOCK)[:, None]
    valid = (j * SEL_BLOCK <= qpos[:, None])[None, :, None, :]
    forced = ((j == 0) | (j == cur) | (j == cur - 1))[None, :, None, :]
    score = jnp.where(forced, jnp.inf, jnp.where(valid, score, -jnp.inf))
    idx = j[0]
    before = (score[..., None, :] > score[..., :, None]) | ((score[..., None, :] == score[..., :, None])
                                                          & (idx[None, :] < idx[:, None]))
    return before.sum(-1) < min(SEL_TOP, n_sel)


def sample_selected_attention(q, qpos, member, k_pool, v_pool, k_new, v_new, page_table):
    B, T = q.shape[:2]
    n_pages = page_table.shape[1]
    per_page = PAGE_SIZE // SEL_BLOCK
    assert member.shape[-1] == n_pages * per_page + 1 and T <= SEL_BLOCK
    kp = k_pool.transpose(0, 2, 3, 1)[page_table]
    vp = v_pool.transpose(0, 2, 3, 1)[page_table]
    qg = q.reshape(B, T, NSA_KV_HEADS, NSA_GROUP, NSA_HD)
    s_past = jnp.einsum('btngd,bpndk->bntgpk', qg, kp) * ATTN_SCALE
    s_new = jnp.einsum('btngd,bsnd->bntgs', qg, k_new) * ATTN_SCALE
    m = member.transpose(0, 2, 1, 3)
    m_past = jnp.repeat(m[..., :-1].reshape(B, NSA_KV_HEADS, T, n_pages, per_page), SEL_BLOCK, axis=-1)
    kpos = (jnp.arange(n_pages) * PAGE_SIZE)[:, None] + jnp.arange(PAGE_SIZE)[None, :]
    m_past = m_past & (kpos[None, None, None] <= qpos[None, None, :, None, None])
    new_pos = n_pages * PAGE_SIZE + jnp.arange(T)
    m_new = m[..., -1:] & (new_pos[None, None, None, :] <= qpos[None, None, :, None])
    logits = jnp.concatenate(
        [jnp.where(m_past[:, :, :, None], s_past, -jnp.inf).reshape(B, NSA_KV_HEADS, T, NSA_GROUP, -1),
         jnp.where(m_new[:, :, :, None], s_new, -jnp.inf)], axis=-1)
    pr = jax.nn.softmax(logits, axis=-1)
    pr_past = pr[..., :n_pages * PAGE_SIZE].reshape(B, NSA_KV_HEADS, T, NSA_GROUP, n_pages, PAGE_SIZE)
    o = (jnp.einsum('bntgpk,bpndk->bntgd', pr_past, vp)
         + jnp.einsum('bntgs,bsnd->bntgd', pr[..., n_pages * PAGE_SIZE:], v_new))
    return o.transpose(0, 2, 1, 3, 4).reshape(B, T, NSA_HEADS, NSA_HD)


def to_blocks(rows, n_sel):
    B, L, KV, hd = rows.shape
    rows = jnp.pad(rows, ((0, 0), (0, n_sel * SEL_BLOCK - L), (0, 0), (0, 0)))
    return rows.reshape(B, n_sel, SEL_BLOCK, KV, hd).transpose(0, 3, 1, 2, 4)


def take_rows(table, idx):
    return table[idx]


def sel_attend(q, qpos, sel, kb, vb):
    B, Tq = q.shape[:2]
    k = sel.shape[-1]
    sel_t = sel.transpose(0, 2, 1, 3)
    gather = jax.vmap(jax.vmap(take_rows))
    kg = gather(kb, sel_t).reshape(B, NSA_KV_HEADS, Tq, k * SEL_BLOCK, NSA_HD)
    vg = gather(vb, sel_t).reshape(B, NSA_KV_HEADS, Tq, k * SEL_BLOCK, NSA_HD)
    kpos = (sel_t[..., None] * SEL_BLOCK + jnp.arange(SEL_BLOCK)).reshape(B, NSA_KV_HEADS, Tq, k * SEL_BLOCK)
    qg = q.reshape(B, Tq, NSA_KV_HEADS, NSA_GROUP, NSA_HD).transpose(0, 2, 1, 3, 4)
    s = jnp.einsum('bntgd,bntsd->bntgs', qg, kg) * ATTN_SCALE
    mask = kpos[:, :, :, None, :] <= qpos[None, None, :, None, None]
    pr = jax.nn.softmax(jnp.where(mask, s, -jnp.inf), axis=-1)
    o = jnp.einsum('bntgs,bntsd->bntgd', pr, vg)
    return o.transpose(0, 2, 1, 3, 4).reshape(B, Tq, NSA_HEADS, NSA_HD)


def win_attend(q, qpos, k, v, kpos):
    B, Tq = q.shape[:2]
    qg = q.reshape(B, Tq, NSA_KV_HEADS, NSA_GROUP, NSA_HD)
    s = jnp.einsum('btngd,bsnd->btngs', qg, k) * ATTN_SCALE
    diff = qpos[:, None] - kpos[None, :]
    mask = ((diff >= 0) & (diff < WINDOW) & (kpos[None, :] >= 0))[None, :, None, None, :]
    pr = jax.nn.softmax(jnp.where(mask, s, -jnp.inf), axis=-1)
    o = jnp.einsum('btngs,bsnd->btngd', pr, v)
    return o.reshape(B, Tq, NSA_HEADS, NSA_HD)


def nsa_combine(gates, o_cmp, o_sel, o_win):
    B, T = gates.shape[:2]
    o = gates[..., 0:1] * o_cmp + gates[..., 1:2] * o_sel + gates[..., 2:3] * o_win
    return o.reshape(B, T, NSA_QW)


NSA_TQ = 128
NSA_CK = 512
MASKED = -1e30


def _softmax_rows(s):
    m = jnp.max(s, axis=-1, keepdims=True)
    e = jnp.exp(s - m)
    return e / jnp.sum(e, axis=-1, keepdims=True)


def _nsa_prompt_kernel(q_ref, kc_ref, vc_ref, ks_ref, vs_ref, kw_ref, vw_ref, zg_ref, msel_ref, exp_ref, o_ref):
    f32, bf16 = jnp.float32, jnp.bfloat16
    tq = NSA_TQ
    q0 = pl.program_id(2) * tq
    qb = q_ref[0]
    qs = jnp.concatenate([qb[:, g * NSA_HD:(g + 1) * NSA_HD] for g in range(NSA_GROUP)], axis=0)
    tpos = q0 + lax.broadcasted_iota(jnp.int32, (tq, 1), 0)

    def per_head(a):
        return jnp.concatenate([a] * NSA_GROUP, axis=0)

    s = lax.dot_general(qs, kc_ref[0, 0], _NT, preferred_element_type=f32)
    cblk = lax.broadcasted_iota(jnp.int32, (tq, 128), 1)
    cvalid = cblk * CMP_STRIDE + (CMP_BLOCK - 1) <= tpos
    s = s + per_head(jnp.where(cvalid, 0.0, MASKED))
    e = jnp.exp(s - jnp.max(s, axis=-1, keepdims=True)) * per_head(jnp.where(cvalid, 1.0, 0.0))
    l = jnp.sum(e, axis=-1, keepdims=True)
    p = e / jnp.where(l > 0.0, l, 1.0)
    o_cmp = jnp.dot(p.astype(bf16), vc_ref[0, 0], preferred_element_type=f32)

    imp = p[0:tq]
    for g in range(1, NSA_GROUP):
        imp = imp + p[g * tq:(g + 1) * tq]
    hi = imp.astype(bf16)
    r1 = imp - hi.astype(f32)
    mid = r1.astype(bf16)
    lo = (r1 - mid.astype(f32)).astype(bf16)
    msel = msel_ref[...]
    score = (lax.dot_general(msel, hi, _NT, preferred_element_type=f32)
             + lax.dot_general(msel, mid, _NT, preferred_element_type=f32)
             + lax.dot_general(msel, lo, _NT, preferred_element_type=f32))
    n_sel = score.shape[0]
    j = lax.broadcasted_iota(jnp.int32, (n_sel, tq), 0)
    tok = q0 + lax.broadcasted_iota(jnp.int32, (n_sel, tq), 1)
    cur = tok // SEL_BLOCK
    forced = (j == 0) | (j == cur) | (j == cur - 1)
    score = jnp.where(forced, jnp.inf, jnp.where(j * SEL_BLOCK <= tok, score, -jnp.inf))
    rank = jnp.zeros((n_sel, tq), f32)
    for jp in range(n_sel):
        row = score[jp:jp + 1, :]
        before = (row > score) | ((row == score) & (j > jp))
        rank = rank + jnp.where(before, 1.0, 0.0)
    chosen_t = jnp.where(rank < SEL_TOP, 1.0, 0.0)
    sel01 = jnp.concatenate([chosen_t, jnp.zeros((LANES - n_sel, tq), f32)], axis=0).T.astype(bf16)

    ck = NSA_CK
    rows = NSA_GROUP * tq

    def sel_chunk(c, carry):
        m, l, acc = carry
        k0 = pl.multiple_of(c * ck, ck)
        s = lax.dot_general(qs, ks_ref[0, 0, pl.ds(k0, ck), :], _NT, preferred_element_type=f32)
        chosen = jnp.dot(sel01, exp_ref[c], preferred_element_type=f32)
        kpos = k0 + lax.broadcasted_iota(jnp.int32, (tq, ck), 1)
        ok = (chosen > 0.5) & (kpos <= tpos)
        s = s + per_head(jnp.where(ok, 0.0, MASKED))
        m_new = jnp.maximum(m, jnp.max(s, axis=-1, keepdims=True))
        a = jnp.exp(m - m_new)
        pr = jnp.exp(s - m_new)
        l = a * l + jnp.sum(pr, axis=-1, keepdims=True)
        acc = a * acc + jnp.dot(pr.astype(bf16), vs_ref[0, 0, pl.ds(k0, ck), :], preferred_element_type=f32)
        return m_new, l, acc

    init = (jnp.full((rows, 1), MASKED, f32), jnp.zeros((rows, 1), f32), jnp.zeros((rows, NSA_HD), f32))
    n_chunks = (q0 + tq + ck - 1) // ck
    _, l_sel, acc_sel = lax.fori_loop(0, n_chunks, sel_chunk, init)
    o_sel = acc_sel / l_sel

    w0 = pl.multiple_of(jnp.maximum(q0 - WINDOW, 0), tq)
    wl = WINDOW + tq
    s = lax.dot_general(qs, kw_ref[0, 0, pl.ds(w0, wl), :], _NT, preferred_element_type=f32)
    diff = tpos - (w0 + lax.broadcasted_iota(jnp.int32, (tq, wl), 1))
    s = s + per_head(jnp.where((diff >= 0) & (diff < WINDOW), 0.0, MASKED))
    o_win = jnp.dot(_softmax_rows(s).astype(bf16), vw_ref[0, 0, pl.ds(w0, wl), :], preferred_element_type=f32)

    gates = jax.nn.sigmoid(zg_ref[0, 0])
    for g in range(NSA_GROUP):
        r = slice(g * tq, (g + 1) * tq)
        o_ref[0, :, g * NSA_HD:(g + 1) * NSA_HD] = (gates[:, 3 * g:3 * g + 1] * o_cmp[r]
                                                    + gates[:, 3 * g + 1:3 * g + 2] * o_sel[r]
                                                    + gates[:, 3 * g + 2:3 * g + 3] * o_win[r])


def nsa_prompt_attention(qs, kc, vc, k_slc, v_slc, k_win, v_win, zg):
    B, S = qs.shape[:2]
    bf16 = jnp.bfloat16
    assert S % NSA_CK == 0 and S % NSA_TQ == 0 and WINDOW % NSA_TQ == 0 and WINDOW + NSA_TQ <= S
    n_sel = S // SEL_BLOCK
    nb = kc.shape[1]
    assert nb <= 128

    def pad_blocks(a):
        return jnp.pad(a.transpose(0, 2, 1, 3).astype(bf16), ((0, 0), (0, 0), (0, 128 - nb), (0, 0)))

    zg4 = zg.reshape(B, S, NSA_KV_HEADS, 3 * NSA_GROUP).transpose(0, 2, 1, 3)
    c = np.arange(128)[:, None]
    jj = np.arange(n_sel)[None, :]
    ratio = SEL_BLOCK // CMP_STRIDE
    msel = ((c >= jj * ratio - (CMP_BLOCK // CMP_STRIDE - 1)) & (c <= jj * ratio + ratio - 1) & (c < nb))
    assert n_sel <= LANES and NSA_TQ == LANES
    expand = (np.arange(S)[None, :] // SEL_BLOCK == np.arange(LANES)[:, None])
    expand = expand.reshape(LANES, S // NSA_CK, NSA_CK).transpose(1, 0, 2)
    row_spec = pl.BlockSpec((1, 1, S, NSA_HD), lambda b, n, i: (b, n, 0, 0))
    blk_spec = pl.BlockSpec((1, 1, 128, NSA_HD), lambda b, n, i: (b, n, 0, 0))
    return pl.pallas_call(
        _nsa_prompt_kernel,
        out_shape=jax.ShapeDtypeStruct((B, S, NSA_QW), jnp.float32),
        grid=(B, NSA_KV_HEADS, S // NSA_TQ),
        in_specs=[pl.BlockSpec((1, NSA_TQ, NSA_GROUP * NSA_HD), lambda b, n, i: (b, i, n)),
                  blk_spec, blk_spec, row_spec, row_spec, row_spec, row_spec,
                  pl.BlockSpec((1, 1, NSA_TQ, 3 * NSA_GROUP), lambda b, n, i: (b, n, i, 0)),
                  pl.BlockSpec((n_sel, 128), lambda b, n, i: (0, 0)),
                  pl.BlockSpec((S // NSA_CK, LANES, NSA_CK), lambda b, n, i: (0, 0, 0))],
        out_specs=pl.BlockSpec((1, NSA_TQ, NSA_GROUP * NSA_HD), lambda b, n, i: (b, i, n)),
        compiler_params=pltpu.CompilerParams(dimension_semantics=("arbitrary", "arbitrary", "arbitrary"),
                                             vmem_limit_bytes=VMEM_LIMIT_BYTES),
        name="nsa_prompt_attention",
    )(qs, pad_blocks(kc), pad_blocks(vc), k_slc, v_slc, k_win, v_win,
      zg4, jnp.asarray(msel.T, bf16), jnp.asarray(expand, bf16))


MLSTM_L = 256
CONV_HALO = 8


def _log_sigmoid(x):
    return -(jnp.maximum(-x, 0.0) + jnp.log1p(jnp.exp(-jnp.abs(x))))


def _mlstm_prompt_kernel(x_ref, xprev_ref, halo0_ref, v_ref, o_ref, gcol_ref, grow_ref, cw_ref, cb_ref,
                         out_ref, c_out, n_out, m_out, c_ref, n_ref, m_ref):
    f32, bf16 = jnp.float32, jnp.bfloat16
    c = pl.program_id(1)
    L = MLSTM_L

    @pl.when(c == 0)
    def _():
        c_ref[...] = jnp.zeros_like(c_ref)
        n_ref[...] = jnp.zeros_like(n_ref)
        m_ref[...] = jnp.zeros_like(m_ref)

    x = x_ref[0]
    halo = jnp.where(c == 0, halo0_ref[0], xprev_ref[0, L - CONV_HALO:L, :])
    ext = jnp.concatenate([halo, x], axis=0)
    conv = cb_ref[...]
    for j in range(CONV_W):
        o = CONV_HALO - (CONV_W - 1) + j
        conv = conv + ext[o:o + L] * cw_ref[j:j + 1, :]
    qk = conv * jax.nn.sigmoid(conv)

    t_id = lax.broadcasted_iota(jnp.int32, (L, L), 0)
    s_id = lax.broadcasted_iota(jnp.int32, (L, L), 1)
    causal = t_id >= s_id
    gcol = gcol_ref[0, 0]
    grow = grow_ref[0, 0]
    for h in range(M_HEADS):
        hd = slice(h * M_HD, (h + 1) * M_HD)
        q = qk[:, hd]
        k = qk[:, M_W + h * M_HD:M_W + (h + 1) * M_HD] * (M_HD ** -0.5)
        v = v_ref[0, :, hd].astype(bf16)
        ig_r = grow[h:h + 1, :]
        ig_c = gcol[:, h:h + 1]
        lf_r = _log_sigmoid(grow[M_HEADS + h:M_HEADS + h + 1, :])
        lf_c = _log_sigmoid(gcol[:, M_HEADS + h:M_HEADS + h + 1])
        b_c = jnp.sum(jnp.where(causal, lf_r, 0.0), axis=1, keepdims=True)
        b_r = jnp.sum(jnp.where(t_id <= s_id, lf_c, 0.0), axis=0, keepdims=True)
        m_prev = m_ref[h]
        dmat = jnp.where(causal, b_c - b_r + ig_r, -jnp.inf)
        inter = b_c + m_prev
        m_t = jnp.maximum(inter, jnp.max(dmat, axis=1, keepdims=True))
        w_intra = jnp.exp(dmat - m_t)
        w_inter = jnp.exp(inter - m_t)
        qb = q.astype(bf16)
        s = lax.dot_general(qb, k.astype(bf16), _NT, preferred_element_type=f32) * w_intra
        num = (jnp.dot(s.astype(bf16), v, preferred_element_type=f32)
               + w_inter * jnp.dot(qb, c_ref[h].astype(bf16), preferred_element_type=f32))
        den = jnp.sum(s, axis=1, keepdims=True) + w_inter * jnp.sum(q * n_ref[h], axis=1, keepdims=True)
        hh = num / jnp.maximum(jnp.abs(den), jnp.exp(-m_t))
        out_ref[0, :, hd] = jax.nn.sigmoid(o_ref[0, :, hd]) * hh
        m_new = m_t[L - 1:L]
        b_last = b_c[L - 1:L]
        w_s = jnp.exp(b_last - b_c + ig_c - m_new)
        w_p = jnp.exp(b_last + m_prev - m_new)
        kw = k * w_s
        c_ref[h] = w_p * c_ref[h] + jnp.dot(kw.T.astype(bf16), v, preferred_element_type=f32)
        n_ref[h] = w_p * n_ref[h] + jnp.sum(kw, axis=0, keepdims=True)
        m_ref[h] = m_new

    @pl.when(c == pl.num_programs(1) - 1)
    def _():
        c_out[0] = c_ref[...]
        n_out[0] = n_ref[...]
        m_out[0] = m_ref[...]


def mlstm_prompt(zqk, zv, zo, zif, conv_w, conv_b, b_if):
    B, T, _ = zqk.shape
    L = MLSTM_L
    assert T % L == 0
    nc = T // L
    f32 = jnp.float32
    gif = zif + b_if
    gcol = gif.reshape(B, nc, L, 2 * M_HEADS)
    grow = gcol.transpose(0, 1, 3, 2)
    halo0 = jnp.zeros((B, CONV_HALO, 2 * M_W), f32)
    out, C, n, m = pl.pallas_call(
        _mlstm_prompt_kernel,
        out_shape=(jax.ShapeDtypeStruct((B, T, M_W), f32),
                   jax.ShapeDtypeStruct((B, M_HEADS, M_HD, M_HD), f32),
                   jax.ShapeDtypeStruct((B, M_HEADS, 1, M_HD), f32),
                   jax.ShapeDtypeStruct((B, M_HEADS, 1, 1), f32)),
        grid=(B, nc),
        in_specs=[pl.BlockSpec((1, L, 2 * M_W), lambda b, c: (b, c, 0)),
                  pl.BlockSpec((1, L, 2 * M_W), lambda b, c: (b, jnp.maximum(c - 1, 0), 0)),
                  pl.BlockSpec((1, CONV_HALO, 2 * M_W), lambda b, c: (b, 0, 0)),
                  pl.BlockSpec((1, L, M_W), lambda b, c: (b, c, 0)),
                  pl.BlockSpec((1, L, M_W), lambda b, c: (b, c, 0)),
                  pl.BlockSpec((1, 1, L, 2 * M_HEADS), lambda b, c: (b, c, 0, 0)),
                  pl.BlockSpec((1, 1, 2 * M_HEADS, L), lambda b, c: (b, c, 0, 0)),
                  pl.BlockSpec((CONV_W, 2 * M_W), lambda b, c: (0, 0)),
                  pl.BlockSpec((1, 2 * M_W), lambda b, c: (0, 0))],
        out_specs=(pl.BlockSpec((1, L, M_W), lambda b, c: (b, c, 0)),
                   pl.BlockSpec((1, M_HEADS, M_HD, M_HD), lambda b, c: (b, 0, 0, 0)),
                   pl.BlockSpec((1, M_HEADS, 1, M_HD), lambda b, c: (b, 0, 0, 0)),
                   pl.BlockSpec((1, M_HEADS, 1, 1), lambda b, c: (b, 0, 0, 0))),
        scratch_shapes=[pltpu.VMEM((M_HEADS, M_HD, M_HD), f32), pltpu.VMEM((M_HEADS, 1, M_HD), f32),
                        pltpu.VMEM((M_HEADS, 1, 1), f32)],
        compiler_params=pltpu.CompilerParams(dimension_semantics=("arbitrary", "arbitrary"),
                                             vmem_limit_bytes=VMEM_LIMIT_BYTES),
        name="mlstm_prompt",
    )(zqk, zqk, halo0, zv, zo, gcol, grow, conv_w, conv_b[None])
    return out, C, n.reshape(B, M_HEADS, M_HD), m.reshape(B, M_HEADS)


def mlstm_chunk(carry, inp):
    C, n, m = carry
    q, k, v, ig, lf = inp
    L = q.shape[2]
    b = jnp.cumsum(lf, axis=-1)
    causal = jnp.tril(jnp.ones((L, L), dtype=bool))
    dmat = jnp.where(causal, b[..., :, None] - b[..., None, :] + ig[..., None, :], -jnp.inf)
    inter = b + m[..., None]
    m_t = jnp.maximum(inter, dmat.max(axis=-1))
    w_intra = jnp.exp(dmat - m_t[..., None])
    w_inter = jnp.exp(inter - m_t)
    s = jnp.einsum('bhtd,bhsd->bhts', q, k) * w_intra
    num = jnp.einsum('bhts,bhsv->bhtv', s, v) + w_inter[..., None] * jnp.einsum('bhtd,bhdv->bhtv', q, C)
    den = s.sum(-1) + w_inter * jnp.einsum('bhtd,bhd->bht', q, n)
    h = num / jnp.maximum(jnp.abs(den), jnp.exp(-m_t))[..., None]
    m_new = m_t[..., -1]
    w_s = jnp.exp(b[..., -1:] - b + ig - m_new[..., None])
    w_p = jnp.exp(b[..., -1] + m - m_new)
    C_new = w_p[..., None, None] * C + jnp.einsum('bhs,bhsd,bhsv->bhdv', w_s, k, v)
    n_new = w_p[..., None] * n + jnp.einsum('bhs,bhsd->bhd', w_s, k)
    return (C_new, n_new, m_new), h


def mlstm_mix(zqk, zv, zo, zif, buf0, C0, n0, m0, conv_w, conv_b, b_if, chunk):
    B, T, _ = zqk.shape
    full = jnp.concatenate([buf0, zqk], axis=1)
    conv = conv_b
    for j in range(CONV_W):
        conv = conv + full[:, j:j + T] * conv_w[j]
    qk = jax.nn.silu(conv)

    def heads(a):
        return a.reshape(B, T, M_HEADS, M_HD).transpose(0, 2, 1, 3)

    q = heads(qk[..., :M_W])
    k = heads(qk[..., M_W:]) * (M_HD ** -0.5)
    v = heads(zv)
    gif = zif + b_if
    ig = gif[..., :M_HEADS].transpose(0, 2, 1)
    lf = jax.nn.log_sigmoid(gif[..., M_HEADS:]).transpose(0, 2, 1)
    nc = T // chunk

    def to_chunks(a):
        return jnp.moveaxis(a.reshape(B, M_HEADS, nc, chunk, *a.shape[3:]), 2, 0)

    (C, n, m), h = lax.scan(mlstm_chunk, (C0, n0, m0),
                            (to_chunks(q), to_chunks(k), to_chunks(v), to_chunks(ig), to_chunks(lf)))
    h = jnp.moveaxis(h, 0, 2).reshape(B, M_HEADS, T, M_HD).transpose(0, 2, 1, 3).reshape(B, T, M_W)
    out = jax.nn.sigmoid(zo) * h
    return out, (C, n, m, full[:, T:])


PEER_COMBOS = 2 * PEER_HEADS
PEER_SCORE_TT = 1024
PEER_KEY_ROWS = 8
PEER_TILE = PEER_KEY_ROWS * N_KEYS
PEER_TS_ROWS = 24
LANES = 128
_NT = (((1,), (1,)), ((), ()))


def _peer_topk_kernel(q_ref, keys_ref, s_ref, e0_ref, e1_ref, tau_ref, ts_ref):
    head = pl.program_id(1)
    tt = q_ref.shape[0]
    half_w = PEER_QDIM // 2
    for half in range(2):
        c = 2 * head + half
        qh = q_ref[:, half * half_w:(half + 1) * half_w].astype(jnp.bfloat16)
        s = lax.dot_general(keys_ref[half], qh, _NT, preferred_element_type=jnp.float32)
        s_ref[c] = s
        pad_rows = jnp.full((PEER_TS_ROWS - PEER_TOPK - 1, tt), -jnp.inf, jnp.float32)
        work = s
        rows = []
        for _ in range(PEER_TOPK + 1):
            m = jnp.max(work, axis=0, keepdims=True)
            work = jnp.where(work == m, -jnp.inf, work)
            rows.append(m)
        ts_ref[c] = jnp.concatenate(rows + [pad_rows], axis=0)
        removed = jnp.sum(jnp.where(work == -jnp.inf, 1.0, 0.0), axis=0, keepdims=True)
        has_ties = jnp.max(removed) > PEER_TOPK + 1

        @pl.when(has_ties)
        def _(s=s, c=c):
            key_id = lax.broadcasted_iota(jnp.int32, s.shape, 0)
            work = s
            rows = []
            for _ in range(PEER_TOPK + 1):
                m = jnp.max(work, axis=0, keepdims=True)
                first = jnp.min(jnp.where(work == m, key_id, N_KEYS), axis=0, keepdims=True)
                work = jnp.where(key_id == first, -jnp.inf, work)
                rows.append(m)
            ts_ref[c] = jnp.concatenate(rows + [pad_rows], axis=0)

    @pl.when(head == PEER_HEADS - 1)
    def _():
        for h in range(PEER_HEADS):
            t0 = ts_ref[2 * h]
            t1 = ts_ref[2 * h + 1]
            pieces = [t0[0:1] + t1] + [t0[a:a + 1] + t1[0:8] for a in range(1, 8)] + [t0[8:24] + t1[0:1]]
            cand = jnp.concatenate(pieces, axis=0)
            top = t0[0:1] + t1[0:1]
            n_pad = 2 * (PEER_TS_ROWS - PEER_TOPK - 1)
            work = cand
            vals = []
            for _ in range(PEER_TOPK + 1):
                m = jnp.max(work, axis=0, keepdims=True)
                work = jnp.where(work == m, -jnp.inf, work)
                vals.append(m)
            z = jnp.exp(vals[0] - top)
            for r in range(1, PEER_TOPK):
                z = z + jnp.exp(vals[r] - top)
            tau_ref[h:h + 1, :] = 0.5 * vals[PEER_TOPK - 1] + 0.5 * vals[PEER_TOPK]
            e0_ref[h] = jnp.exp(s_ref[2 * h] - t0[0:1]) / z
            e1_ref[h] = jnp.exp(s_ref[2 * h + 1] - t1[0:1])
            removed = jnp.sum(jnp.where(work == -jnp.inf, 1.0, 0.0), axis=0, keepdims=True)

            @pl.when(jnp.max(removed) > PEER_TOPK + 1 + n_pad)
            def _(cand=cand, top=top, t0=t0, h=h):
                v16 = top
                v17 = top
                z = jnp.zeros_like(top)
                seen = jnp.zeros_like(top)
                for _ in range(PEER_TOPK + 1):
                    m = jnp.max(cand, axis=0, keepdims=True)
                    eq = cand == m
                    cnt = jnp.sum(jnp.where(eq, 1.0, 0.0), axis=0, keepdims=True)
                    active = seen < PEER_TOPK
                    take = jnp.minimum(cnt, PEER_TOPK - seen)
                    v16 = jnp.where(active, m, v16)
                    v17 = jnp.where(seen < PEER_TOPK + 1, m, v17)
                    z = z + jnp.where(active, take * jnp.exp(m - top), 0.0)
                    seen = seen + cnt
                    cand = jnp.where(eq, -jnp.inf, cand)
                tau_ref[h:h + 1, :] = 0.5 * v16 + 0.5 * v17
                e0_ref[h] = jnp.exp(s_ref[2 * h] - t0[0:1]) / z


def peer_scores(q, sub_keys, tt):
    n = q.shape[0]
    assert n % tt == 0
    keys = sub_keys.reshape(PEER_COMBOS, N_KEYS, PEER_QDIM // 2).astype(jnp.bfloat16)
    f32 = jnp.float32
    return pl.pallas_call(
        _peer_topk_kernel,
        out_shape=(jax.ShapeDtypeStruct((PEER_COMBOS, N_KEYS, n), f32),
                   jax.ShapeDtypeStruct((PEER_HEADS, N_KEYS, n), f32),
                   jax.ShapeDtypeStruct((PEER_HEADS, N_KEYS, n), f32),
                   jax.ShapeDtypeStruct((PEER_HEADS, n), f32)),
        grid=(n // tt, PEER_HEADS),
        in_specs=[pl.BlockSpec((tt, PEER_QDIM), lambda i, c: (i, c)),
                  pl.BlockSpec((2, N_KEYS, PEER_QDIM // 2), lambda i, c: (c, 0, 0))],
        out_specs=(pl.BlockSpec((PEER_COMBOS, N_KEYS, tt), lambda i, c: (0, 0, i)),
                   pl.BlockSpec((PEER_HEADS, N_KEYS, tt), lambda i, c: (0, 0, i)),
                   pl.BlockSpec((PEER_HEADS, N_KEYS, tt), lambda i, c: (0, 0, i)),
                   pl.BlockSpec((PEER_HEADS, tt), lambda i, c: (0, i))),
        scratch_shapes=[pltpu.VMEM((PEER_COMBOS, PEER_TS_ROWS, tt), f32)],
        compiler_params=pltpu.CompilerParams(dimension_semantics=("arbitrary", "arbitrary"),
                                             vmem_limit_bytes=VMEM_LIMIT_BYTES),
        name="peer_topk",
    )(q, keys)


def _peer_dense_kernel(xt_ref, h_ref, u_ref, vt_ref, s0_ref, ez_ref, s_ref, e1_ref, tau_ref, g_ref, b_ref,
                       o_ref, acc_ref, a_ref, w_ref):
    e = pl.program_id(1)
    tt = xt_ref.shape[1]

    @pl.when(e == 0)
    def _():
        acc_ref[...] = jnp.zeros_like(acc_ref)

    a_ref[...] = jnp.dot(u_ref[...], xt_ref[...], preferred_element_type=jnp.float32)
    for r in range(PEER_KEY_ROWS):
        rows = slice(r * N_KEYS, (r + 1) * N_KEYS)
        for t in range(tt // LANES):
            tok = slice(t * LANES, (t + 1) * LANES)
            gate = jnp.zeros((N_KEYS, LANES), jnp.float32)
            for h in range(PEER_HEADS):
                need = tau_ref[h:h + 1, tok] - s0_ref[2 * h, r:r + 1, tok]
                picked = jnp.where(s_ref[2 * h + 1, :, tok] >= need, e1_ref[h, :, tok], 0.0)
                gate = gate + picked * ez_ref[h, r:r + 1, tok]
            ar = a_ref[rows, tok]
            act = 0.5 * ar * (1.0 + lax.erf(ar * (2.0 ** -0.5)))
            w_ref[rows, tok] = (gate * act).astype(jnp.bfloat16)
    acc_ref[...] += jnp.dot(vt_ref[0], w_ref[...], preferred_element_type=jnp.float32)

    @pl.when(e == pl.num_programs(1) - 1)
    def _():
        r = ALPHA * h_ref[...] + acc_ref[...].T
        mu = jnp.mean(r, axis=-1, keepdims=True)
        d = r - mu
        var = jnp.mean(d * d, axis=-1, keepdims=True)
        o_ref[...] = d * lax.rsqrt(var + LN_EPS) * g_ref[...] + b_ref[...]


def peer_tail(h, ht, q, sub_keys, u_bf, vt_bf, ln_g, ln_b, tt):
    n, d = h.shape
    s, e0z, e1, tau = peer_scores(q, sub_keys, PEER_SCORE_TT if n % PEER_SCORE_TT == 0 else tt)
    n_exp = u_bf.shape[0]
    return pl.pallas_call(
        _peer_dense_kernel,
        out_shape=jax.ShapeDtypeStruct((n, d), jnp.float32),
        grid=(n // tt, n_exp // PEER_TILE),
        in_specs=[pl.BlockSpec((d, tt), lambda i, e: (0, i)),
                  pl.BlockSpec((tt, d), lambda i, e: (i, 0)),
                  pl.BlockSpec((PEER_TILE, d), lambda i, e: (e, 0)),
                  pl.BlockSpec((1, d, PEER_TILE), lambda i, e: (e, 0, 0)),
                  pl.BlockSpec((PEER_COMBOS, PEER_KEY_ROWS, tt), lambda i, e: (0, e, i)),
                  pl.BlockSpec((PEER_HEADS, PEER_KEY_ROWS, tt), lambda i, e: (0, e, i)),
                  pl.BlockSpec((PEER_COMBOS, N_KEYS, tt), lambda i, e: (0, 0, i)),
                  pl.BlockSpec((PEER_HEADS, N_KEYS, tt), lambda i, e: (0, 0, i)),
                  pl.BlockSpec((PEER_HEADS, tt), lambda i, e: (0, i)),
                  pl.BlockSpec((1, d), lambda i, e: (0, 0)),
                  pl.BlockSpec((1, d), lambda i, e: (0, 0))],
        out_specs=pl.BlockSpec((tt, d), lambda i, e: (i, 0)),
        scratch_shapes=[pltpu.VMEM((d, tt), jnp.float32), pltpu.VMEM((PEER_TILE, tt), jnp.float32),
                        pltpu.VMEM((PEER_TILE, tt), jnp.bfloat16)],
        compiler_params=pltpu.CompilerParams(dimension_semantics=("arbitrary", "arbitrary"),
                                             vmem_limit_bytes=VMEM_LIMIT_BYTES),
        name="peer_dense",
    )(ht, h, u_bf, vt_bf, s, e0z, s, e1, tau, ln_g[None], ln_b[None])


def _out_proj_kernel(x_ref, nsa_ref, m_ref, wn_ref, wm_ref, g_ref, b_ref, wq_ref, h_ref, ht_ref, q_ref):
    f32, bf16 = jnp.float32, jnp.bfloat16
    r = (ALPHA * x_ref[...] + jnp.dot(nsa_ref[...].astype(bf16), wn_ref[...], preferred_element_type=f32)
         + jnp.dot(m_ref[...].astype(bf16), wm_ref[...], preferred_element_type=f32))
    mu = jnp.mean(r, axis=-1, keepdims=True)
    d = r - mu
    var = jnp.mean(d * d, axis=-1, keepdims=True)
    h = d * lax.rsqrt(var + LN_EPS) * g_ref[...] + b_ref[...]
    h_ref[...] = h
    ht_ref[...] = h.T.astype(bf16)
    q_ref[...] = jnp.dot(h.astype(bf16), wq_ref[...], preferred_element_type=f32)


def out_proj_fused(x, o_nsa, o_m, w_out, ln_g, ln_b, w_pq, tm):
    n, d = x.shape
    assert n % tm == 0
    bf16 = jnp.bfloat16
    nq = w_pq.shape[1]

    def rows(width):
        return pl.BlockSpec((tm, width), lambda i: (i, 0))

    def whole(a):
        return pl.BlockSpec(a.shape, lambda i: (0, 0))

    wn = w_out[:NSA_QW].astype(bf16)
    wm = w_out[NSA_QW:].astype(bf16)
    wq = w_pq.astype(bf16)
    g, b = ln_g[None], ln_b[None]
    return pl.pallas_call(
        _out_proj_kernel,
        out_shape=(jax.ShapeDtypeStruct((n, d), jnp.float32), jax.ShapeDtypeStruct((d, n), bf16),
                   jax.ShapeDtypeStruct((n, nq), jnp.float32)),
        grid=(n // tm,),
        in_specs=[rows(d), rows(NSA_QW), rows(M_W), whole(wn), whole(wm), whole(g), whole(b), whole(wq)],
        out_specs=(rows(d), pl.BlockSpec((d, tm), lambda i: (0, i)), rows(nq)),
        compiler_params=pltpu.CompilerParams(dimension_semantics=("arbitrary",),
                                             vmem_limit_bytes=VMEM_LIMIT_BYTES),
        name="out_proj",
    )(x, o_nsa, o_m, wn, wm, g, b, wq)


def block_tail(x, o_nsa, o_m, w_out, ln_g, ln_b, w_pq, sub_keys, u_bf, vt_bf, tt):
    lead = x.shape[:-1]
    h, ht, q = out_proj_fused(x.reshape(-1, D_MODEL), o_nsa.reshape(-1, NSA_QW), o_m.reshape(-1, M_W),
                              w_out, ln_g[0], ln_b[0], w_pq, tt)
    return peer_tail(h, ht, q, sub_keys, u_bf, vt_bf, ln_g[1], ln_b[1], tt).reshape(*lead, D_MODEL)


def prompt_mix(x, w_in, pe, w1, b1, w2, conv_w, conv_b, b_if):
    B, S, _ = x.shape
    z = in_proj_fused(x, w_in, jnp.arange(S), 512, True)
    kc = compress_chunks(z["k_cmp"], S // CMP_STRIDE, pe[0], w1[0], b1[0], w2[0])
    vc = compress_chunks(z["v_cmp"], S // CMP_STRIDE, pe[1], w1[1], b1[1], w2[1])
    zgate = z["zgate"].reshape(B, S, GATE_W)
    o_nsa = nsa_prompt_attention(z["q"].reshape(B, S, NSA_QW), kc, vc, z["k_slc_bf"], z["v_slc_bf"],
                                 z["k_win_bf"], z["v_win_bf"], zgate[..., :IN_SPLITS[2]])
    zqk = z["zqk"].reshape(B, S, 2 * M_W)
    o_m, C, n, m = mlstm_prompt(zqk, z["zv"].reshape(B, S, M_W), z["zo"].reshape(B, S, M_W),
                                zgate[..., IN_SPLITS[2]:], conv_w, conv_b, b_if)
    buf = zqk[:, S - (CONV_W - 1):]
    wl = min(WINDOW, S)
    k_cmp, v_cmp, k_slc, v_slc, k_win, v_win = [
        z[k + "_cache"].reshape(B, NSA_KV_HEADS, NSA_HD, S).transpose(0, 3, 1, 2)
        for k in ("k_cmp", "v_cmp", "k_slc", "v_slc", "k_win", "v_win")]
    return (o_nsa, o_m), (k_cmp, v_cmp, k_slc, v_slc, k_win[:, S - wl:], v_win[:, S - wl:], C, n, m, buf)


def sample_mix(x, kc_pool, vc_pool, ks_pool, vs_pool, kw_buf, vw_buf, C0, n0, m0, buf0, page_table,
               w_in, pe, w1, b1, w2, conv_w, conv_b, b_if):
    B, T, _ = x.shape
    past = page_table.shape[1] * PAGE_SIZE
    pos = past + jnp.arange(T)
    z = in_proj_fused(x, w_in, pos, B * T, False)
    q = z["q"].astype(jnp.float32).reshape(B, T, NSA_HEADS, NSA_HD) * (1.0 / ATTN_SCALE)
    k_cmp, v_cmp, k_slc, v_slc, k_win, v_win = [
        z[k].reshape(B, T, NSA_KV_HEADS, NSA_HD) for k in ("k_cmp", "v_cmp", "k_slc", "v_slc", "k_win", "v_win")]
    zgate = z["zgate"].reshape(B, T, GATE_W)
    gates = jax.nn.sigmoid(zgate[..., :IN_SPLITS[2]]).reshape(B, T, NSA_HEADS, 3)
    zqk, zv, zo, zif = (z["zqk"].reshape(B, T, 2 * M_W), z["zv"].reshape(B, T, M_W), z["zo"].reshape(B, T, M_W),
                        zgate[..., IN_SPLITS[2]:])

    assert (past + T) // CMP_STRIDE == past // CMP_STRIDE

    def compressed(pool, c):
        pages = pool.transpose(0, 2, 3, 1)[page_table]
        return compress_pages(pages, pe[c], w1[c], b1[c], w2[c])

    o_cmp, p = cmp_attend(q, pos, compressed(kc_pool, 0), compressed(vc_pool, 1))
    n_sel = -(-(past + T) // SEL_BLOCK)
    member = select_blocks(p, pos, n_sel)
    o_sel = sample_selected_attention(q, pos, member, ks_pool, vs_pool, k_slc, v_slc, page_table)
    wb = kw_buf.shape[1]
    kw = jnp.concatenate([kw_buf, k_win], axis=1)
    vw = jnp.concatenate([vw_buf, v_win], axis=1)
    kpos = past - wb + jnp.arange(wb + T)
    o_win = win_attend(q, pos, kw, vw, kpos)
    o_nsa = nsa_combine(gates, o_cmp, o_sel, o_win)
    o_m, (C, n, m, buf) = mlstm_mix(zqk, zv, zo, zif, buf0, C0, n0, m0, conv_w, conv_b, b_if, T)
    return (o_nsa, o_m), (k_cmp, v_cmp, k_slc, v_slc, kw[:, T:], vw[:, T:], C, n, m, buf)


def kernel(x_prompt, x_sample, cache_k_cmp, cache_v_cmp, cache_k_slc, cache_v_slc, cache_k_win, cache_v_win,
           state_C, state_n, state_m, state_conv, page_table, w_in, w_out, w_phi1, b_phi1, w_phi2, pe_cmp,
           conv_w, conv_b, b_if, ln_g, ln_b, w_pq, sub_keys, u_tab, v_tab):
    l = 0
    mix_p, st_p = prompt_mix(x_prompt, w_in[l], pe_cmp[l], w_phi1[l], b_phi1[l], w_phi2[l],
                             conv_w[l], conv_b[l], b_if[l])
    mix_s, st_s = sample_mix(x_sample, cache_k_cmp[l], cache_v_cmp[l], cache_k_slc[l], cache_v_slc[l],
                             cache_k_win[l], cache_v_win[l], state_C[l], state_n[l], state_m[l],
                             state_conv[l], page_table, w_in[l], pe_cmp[l], w_phi1[l], b_phi1[l],
                             w_phi2[l], conv_w[l], conv_b[l], b_if[l])
    u_bf = u_tab[l].astype(jnp.bfloat16)
    vt_bf = v_tab[l].astype(jnp.bfloat16).reshape(-1, PEER_TILE, D_MODEL).transpose(0, 2, 1)
    xp = block_tail(x_prompt, *mix_p, w_out[l], ln_g[l], ln_b[l], w_pq[l], sub_keys[l], u_bf, vt_bf, 512)
    xs = block_tail(x_sample, *mix_s, w_out[l], ln_g[l], ln_b[l], w_pq[l], sub_keys[l], u_bf, vt_bf, 128)
    return (xp, xs) + tuple(a[None] for a in st_p) + tuple(a[None] for a in st_s)
```
